```python
import jax, jax.numpy as jnp
from jax import lax
import numpy as np

D_MODEL = 2048
BATCH = 8
SEQ = 4096
DEPTH = 2

CHUNK = 64
N_EVEN = (DEPTH + 1) // 2
N_ODD = DEPTH // 2

A_HEADS = 8
A_DK = 128
A_DV = 128
A_CONV = 4
B_GROUPS = 8
B_DG = 128
B_BLOCK = 128
A_QK = A_HEADS * A_DK
A_V = A_HEADS * A_DV
B_W = B_GROUPS * B_DG
A_QKV = 2 * A_QK + A_V
IN_COLS = A_QKV + A_V + 2 * A_HEADS + 2 * B_W
MIX_W = A_V + B_W
C_WIDTH = 31
D_FF = 4 * D_MODEL
EPS = 1e-6

kernel_name = "hybrid_deltanet_gmlp_conformer_block"


def rmsnorm(x, g):
    xf = x.astype(jnp.float32)
    y = xf * lax.rsqrt(jnp.mean(xf * xf, axis=-1, keepdims=True) + EPS)
    return (y * g.astype(jnp.float32)).astype(x.dtype)


def layernorm(x, g, b):
    xf = x.astype(jnp.float32)
    mu = jnp.mean(xf, axis=-1, keepdims=True)
    xc = xf - mu
    y = xc * lax.rsqrt(jnp.mean(xc * xc, axis=-1, keepdims=True) + EPS)
    return (y * g.astype(jnp.float32) + b.astype(jnp.float32)).astype(x.dtype)


def l2norm(x):
    return x * lax.rsqrt(jnp.sum(x * x, axis=-1, keepdims=True) + EPS)


def causal_dwconv(x, w):
    k = w.shape[0]
    return lax.conv_general_dilated(
        x, w[:, None, :].astype(x.dtype), window_strides=(1,), padding=[(k - 1, 0)],
        dimension_numbers=("NWC", "WIO", "NWC"), feature_group_count=x.shape[-1])


def gated_delta_rule(q, k, v, g, beta):
    bsz, seq, heads, dk = q.shape
    dv = v.shape[-1]
    n = seq // CHUNK

    def chunks(t):
        return jnp.moveaxis(t.astype(jnp.float32).reshape(bsz, n, CHUNK, heads, -1), 3, 1)

    q = chunks(q) * (dk ** -0.5)
    k = chunks(k)
    v = chunks(v)
    g = jnp.moveaxis(g.astype(jnp.float32).reshape(bsz, n, CHUNK, heads), 3, 1)
    beta = jnp.moveaxis(beta.astype(jnp.float32).reshape(bsz, n, CHUNK, heads), 3, 1)
    g = jnp.cumsum(g, axis=-1)

    tri = jnp.tril(jnp.ones((CHUNK, CHUNK), dtype=bool))
    strict = jnp.tril(jnp.ones((CHUNK, CHUNK), dtype=bool), -1)
    eye = jnp.eye(CHUNK, dtype=jnp.float32)
    decay = jnp.exp(jnp.where(tri, g[..., :, None] - g[..., None, :], -jnp.inf))

    k_beta = k * beta[..., None]
    v_beta = v * beta[..., None]
    a = jnp.einsum("bhnid,bhnjd->bhnij", k_beta, k) * decay
    a = jnp.where(strict, a, 0.0) + eye
    rhs = jnp.concatenate([v_beta, k_beta * jnp.exp(g)[..., None]], axis=-1)
    sol = lax.linalg.triangular_solve(a, rhs, left_side=True, lower=True, unit_diagonal=True)
    u = sol[..., :dv]
    w = sol[..., dv:]

    attn = jnp.einsum("bhnid,bhnjd->bhnij", q, k) * decay
    q_dec = q * jnp.exp(g)[..., None]
    k_tail = k * jnp.exp(g[..., -1:] - g)[..., None]
    g_last = jnp.exp(g[..., -1])

    xs = tuple(jnp.moveaxis(t, 2, 0) for t in (attn, q_dec, k_tail, u, w, g_last))

    def step(state, inp):
        attn_c, qd_c, kt_c, u_c, w_c, gl_c = inp
        v_new = u_c - jnp.einsum("bhck,bhkv->bhcv", w_c, state)
        o_c = (jnp.einsum("bhck,bhkv->bhcv", qd_c, state)
               + jnp.einsum("bhij,bhjv->bhiv", attn_c, v_new))
        state = state * gl_c[..., None, None] + jnp.einsum("bhck,bhcv->bhkv", kt_c, v_new)
        return state, o_c

    s0 = jnp.zeros((bsz, heads, dk, dv), jnp.float32)
    _, o = lax.scan(step, s0, xs)
    return jnp.transpose(o, (1, 0, 3, 2, 4)).reshape(bsz, seq, heads, dv)


def even_mixer(h, w_in, conv_w, a_log, dt_bias, o_norm, ln_g, ln_b, w_s, b_s, w_out):
    bsz, seq, _ = h.shape
    proj = h @ w_in.astype(h.dtype)
    cuts = [int(c) for c in np.cumsum([A_QKV, A_V, A_HEADS, A_HEADS, B_W])]
    qkv, z, b_raw, a_raw, u_raw, v_raw = jnp.split(proj, cuts, axis=-1)

    qkv = jax.nn.silu(causal_dwconv(qkv, conv_w))
    q, k, v = jnp.split(qkv, [A_QK, 2 * A_QK], axis=-1)
    q = l2norm(q.astype(jnp.float32).reshape(bsz, seq, A_HEADS, A_DK))
    k = l2norm(k.astype(jnp.float32).reshape(bsz, seq, A_HEADS, A_DK))
    v = v.reshape(bsz, seq, A_HEADS, A_DV)
    beta = jax.nn.sigmoid(b_raw.astype(jnp.float32))
    g = -jnp.exp(a_log.astype(jnp.float32)) * jax.nn.softplus(
        a_raw.astype(jnp.float32) + dt_bias.astype(jnp.float32))
    o = gated_delta_rule(q, k, v, g, beta)
    o = rmsnorm(o, o_norm) * jax.nn.silu(z.astype(jnp.float32).reshape(bsz, seq, A_HEADS, A_DV))
    out_a = o.reshape(bsz, seq, A_V).astype(h.dtype)

    u = jax.nn.gelu(u_raw, approximate=False)
    vg = jax.nn.gelu(v_raw, approximate=False).reshape(bsz, seq, B_GROUPS, B_DG)
    vg = layernorm(vg, ln_g.reshape(B_GROUPS, B_DG), ln_b.reshape(B_GROUPS, B_DG))
    vg = vg.reshape(bsz, seq // B_BLOCK, B_BLOCK, B_GROUPS, B_DG)
    pos = np.arange(B_BLOCK)
    mask = (pos[None, :] // CHUNK) <= (pos[:, None] // CHUNK)
    w_masked = jnp.where(mask, w_s, 0.0).astype(h.dtype)
    mixed = jnp.einsum("gij,bmjgc->bmigc", w_masked, vg) + b_s.T.astype(h.dtype)[None, None, :, :, None]
    out_b = u * mixed.reshape(bsz, seq, B_W)

    return jnp.concatenate([out_a, out_b], axis=-1) @ w_out.astype(h.dtype)


def conformer_conv(h, pw1, pw1_b, dw, dw_b, ln_g, ln_b, pw2, pw2_b):
    z = h @ pw1.astype(h.dtype) + pw1_b.astype(h.dtype)
    z = z[..., :D_MODEL] * jax.nn.sigmoid(z[..., D_MODEL:])
    z = causal_dwconv(z, dw) + dw_b.astype(h.dtype)
    z = jax.nn.silu(layernorm(z, ln_g, ln_b))
    return z @ pw2.astype(h.dtype) + pw2_b.astype(h.dtype)


def sq_relu_mlp(h, w1, w2):
    a = jax.nn.relu(h @ w1.astype(h.dtype))
    return (a * a) @ w2.astype(h.dtype)


def _fwd_setup_inputs(seed: int = 0) -> dict:
    key = jax.random.key(seed)
    ks = iter(jax.random.split(key, 40))

    def nrm(shape, scale):
        return jax.random.normal(next(ks), shape, jnp.float32) * scale

    def gain(shape):
        return 1.0 + nrm(shape, 0.05)

    dt = jnp.exp(jax.random.uniform(next(ks), (N_EVEN, A_HEADS), jnp.float32,
                                    np.log(1e-3), np.log(1e-1)))
    return {
        "x": nrm((BATCH, SEQ, D_MODEL), 1.0),
        "e_norm": gain((N_EVEN, D_MODEL)),
        "e_w_in": nrm((N_EVEN, D_MODEL, IN_COLS), D_MODEL ** -0.5),
        "e_conv_w": nrm((N_EVEN, A_CONV, A_QKV), A_CONV ** -0.5),
        "e_a_log": jnp.log(jax.random.uniform(next(ks), (N_EVEN, A_HEADS), jnp.float32, 1.0, 16.0)),
        "e_dt_bias": dt + jnp.log(-jnp.expm1(-dt)),
        "e_o_norm": gain((N_EVEN, A_DV)),
        "e_ln_g": gain((N_EVEN, B_W)),
        "e_ln_b": nrm((N_EVEN, B_W), 0.02),
        "e_w_s": nrm((N_EVEN, B_GROUPS, B_BLOCK, B_BLOCK), B_BLOCK ** -0.5),
        "e_b_s": gain((N_EVEN, B_GROUPS, B_BLOCK)),
        "e_w_out": nrm((N_EVEN, MIX_W, D_MODEL), MIX_W ** -0.5),
        "o_norm": gain((N_ODD, D_MODEL)),
        "o_pw1": nrm((N_ODD, D_MODEL, 2 * D_MODEL), D_MODEL ** -0.5),
        "o_pw1_b": nrm((N_ODD, 2 * D_MODEL), 0.02),
        "o_dw": nrm((N_ODD, C_WIDTH, D_MODEL), C_WIDTH ** -0.5),
        "o_dw_b": nrm((N_ODD, D_MODEL), 0.02),
        "o_ln_g": gain((N_ODD, D_MODEL)),
        "o_ln_b": nrm((N_ODD, D_MODEL), 0.02),
        "o_pw2": nrm((N_ODD, D_MODEL, D_MODEL), D_MODEL ** -0.5),
        "o_pw2_b": nrm((N_ODD, D_MODEL), 0.02),
        "f_norm": gain((DEPTH, D_MODEL)),
        "f_w1": nrm((DEPTH, D_MODEL, D_FF), D_MODEL ** -0.5),
        "f_w2": nrm((DEPTH, D_FF, D_MODEL), D_FF ** -0.5),
        "final_norm": gain((D_MODEL,)),
    }


def _fwd_reference(x, e_norm, e_w_in, e_conv_w, e_a_log, e_dt_bias, e_o_norm, e_ln_g, e_ln_b,
              e_w_s, e_b_s, e_w_out, o_norm, o_pw1, o_pw1_b, o_dw, o_dw_b, o_ln_g, o_ln_b,
              o_pw2, o_pw2_b, f_norm, f_w1, f_w2, final_norm):
    for layer in range(DEPTH):
        i = layer // 2
        if layer % 2 == 0:
            x = x + even_mixer(rmsnorm(x, e_norm[i]), e_w_in[i], e_conv_w[i], e_a_log[i],
                               e_dt_bias[i], e_o_norm[i], e_ln_g[i], e_ln_b[i], e_w_s[i],
                               e_b_s[i], e_w_out[i])
        else:
            x = x + conformer_conv(rmsnorm(x, o_norm[i]), o_pw1[i], o_pw1_b[i], o_dw[i],
                                   o_dw_b[i], o_ln_g[i], o_ln_b[i], o_pw2[i], o_pw2_b[i])
        x = x + sq_relu_mlp(rmsnorm(x, f_norm[layer]), f_w1[layer], f_w2[layer])
    return rmsnorm(x, final_norm)


import jax as _jax
import jax.numpy as _jnp

TWIN_FORMAT = 'train_step'
FWD_PARAMS = ['x', 'e_norm', 'e_w_in', 'e_conv_w', 'e_a_log', 'e_dt_bias', 'e_o_norm', 'e_ln_g', 'e_ln_b', 'e_w_s', 'e_b_s', 'e_w_out', 'o_norm', 'o_pw1', 'o_pw1_b', 'o_dw', 'o_dw_b', 'o_ln_g', 'o_ln_b', 'o_pw2', 'o_pw2_b', 'f_norm', 'f_w1', 'f_w2', 'final_norm']
TWIN_WEIGHTS = ['e_norm', 'e_w_in', 'e_conv_w', 'e_a_log', 'e_dt_bias', 'e_o_norm', 'e_ln_g', 'e_ln_b', 'e_w_s', 'e_b_s', 'e_w_out', 'o_norm', 'o_pw1', 'o_pw1_b', 'o_dw', 'o_dw_b', 'o_ln_g', 'o_ln_b', 'o_pw2', 'o_pw2_b', 'f_norm', 'f_w1', 'f_w2', 'final_norm']
TWIN_DIFF_INPUT = 'x'
TWIN_INPUTS = ['x', 'e_norm', 'e_w_in', 'e_conv_w', 'e_a_log', 'e_dt_bias', 'e_o_norm', 'e_ln_g', 'e_ln_b', 'e_w_s', 'e_b_s', 'e_w_out', 'o_norm', 'o_pw1', 'o_pw1_b', 'o_dw', 'o_dw_b', 'o_ln_g', 'o_ln_b', 'o_pw2', 'o_pw2_b', 'f_norm', 'f_w1', 'f_w2', 'final_norm', 'loss_target', 'm_e_norm', 'm_e_w_in', 'm_e_conv_w', 'm_e_a_log', 'm_e_dt_bias', 'm_e_o_norm', 'm_e_ln_g', 'm_e_ln_b', 'm_e_w_s', 'm_e_b_s', 'm_e_w_out', 'm_o_norm', 'm_o_pw1', 'm_o_pw1_b', 'm_o_dw', 'm_o_dw_b', 'm_o_ln_g', 'm_o_ln_b', 'm_o_pw2', 'm_o_pw2_b', 'm_f_norm', 'm_f_w1', 'm_f_w2', 'm_final_norm', 'v_e_norm', 'v_e_w_in', 'v_e_conv_w', 'v_e_a_log', 'v_e_dt_bias', 'v_e_o_norm', 'v_e_ln_g', 'v_e_ln_b', 'v_e_w_s', 'v_e_b_s', 'v_e_w_out', 'v_o_norm', 'v_o_pw1', 'v_o_pw1_b', 'v_o_dw', 'v_o_dw_b', 'v_o_ln_g', 'v_o_ln_b', 'v_o_pw2', 'v_o_pw2_b', 'v_f_norm', 'v_f_w1', 'v_f_w2', 'v_final_norm']
TWIN_OUTPUTS = ['loss', 'grad_x', 'grad_e_norm', 'grad_e_w_in', 'grad_e_conv_w', 'grad_e_a_log', 'grad_e_dt_bias', 'grad_e_o_norm', 'grad_e_ln_g', 'grad_e_ln_b', 'grad_e_w_s', 'grad_e_b_s', 'grad_e_w_out', 'grad_o_norm', 'grad_o_pw1', 'grad_o_pw1_b', 'grad_o_dw', 'grad_o_dw_b', 'grad_o_ln_g', 'grad_o_ln_b', 'grad_o_pw2', 'grad_o_pw2_b', 'grad_f_norm', 'grad_f_w1', 'grad_f_w2', 'grad_final_norm', 'delta_e_norm', 'delta_e_w_in', 'delta_e_conv_w', 'delta_e_a_log', 'delta_e_dt_bias', 'delta_e_o_norm', 'delta_e_ln_g', 'delta_e_ln_b', 'delta_e_w_s', 'delta_e_b_s', 'delta_e_w_out', 'delta_o_norm', 'delta_o_pw1', 'delta_o_pw1_b', 'delta_o_dw', 'delta_o_dw_b', 'delta_o_ln_g', 'delta_o_ln_b', 'delta_o_pw2', 'delta_o_pw2_b', 'delta_f_norm', 'delta_f_w1', 'delta_f_w2', 'delta_final_norm', 'new_m_e_norm', 'new_m_e_w_in', 'new_m_e_conv_w', 'new_m_e_a_log', 'new_m_e_dt_bias', 'new_m_e_o_norm', 'new_m_e_ln_g', 'new_m_e_ln_b', 'new_m_e_w_s', 'new_m_e_b_s', 'new_m_e_w_out', 'new_m_o_norm', 'new_m_o_pw1', 'new_m_o_pw1_b', 'new_m_o_dw', 'new_m_o_dw_b', 'new_m_o_ln_g', 'new_m_o_ln_b', 'new_m_o_pw2', 'new_m_o_pw2_b', 'new_m_f_norm', 'new_m_f_w1', 'new_m_f_w2', 'new_m_final_norm', 'new_v_e_norm', 'new_v_e_w_in', 'new_v_e_conv_w', 'new_v_e_a_log', 'new_v_e_dt_bias', 'new_v_e_o_norm', 'new_v_e_ln_g', 'new_v_e_ln_b', 'new_v_e_w_s', 'new_v_e_b_s', 'new_v_e_w_out', 'new_v_o_norm', 'new_v_o_pw1', 'new_v_o_pw1_b', 'new_v_o_dw', 'new_v_o_dw_b', 'new_v_o_ln_g', 'new_v_o_ln_b', 'new_v_o_pw2', 'new_v_o_pw2_b', 'new_v_f_norm', 'new_v_f_w1', 'new_v_f_w2', 'new_v_final_norm']
TWIN_LEAF_KINDS = {'loss': 'loss', 'grad_x': 'grad_x', 'grad_e_norm': 'grad_w', 'grad_e_w_in': 'grad_w', 'grad_e_conv_w': 'grad_w', 'grad_e_a_log': 'grad_w', 'grad_e_dt_bias': 'grad_w', 'grad_e_o_norm': 'grad_w', 'grad_e_ln_g': 'grad_w', 'grad_e_ln_b': 'grad_w', 'grad_e_w_s': 'grad_w', 'grad_e_b_s': 'grad_w', 'grad_e_w_out': 'grad_w', 'grad_o_norm': 'grad_w', 'grad_o_pw1': 'grad_w', 'grad_o_pw1_b': 'grad_w', 'grad_o_dw': 'grad_w', 'grad_o_dw_b': 'grad_w', 'grad_o_ln_g': 'grad_w', 'grad_o_ln_b': 'grad_w', 'grad_o_pw2': 'grad_w', 'grad_o_pw2_b': 'grad_w', 'grad_f_norm': 'grad_w', 'grad_f_w1': 'grad_w', 'grad_f_w2': 'grad_w', 'grad_final_norm': 'grad_w', 'delta_e_norm': 'delta_w', 'delta_e_w_in': 'delta_w', 'delta_e_conv_w': 'delta_w', 'delta_e_a_log': 'delta_w', 'delta_e_dt_bias': 'delta_w', 'delta_e_o_norm': 'delta_w', 'delta_e_ln_g': 'delta_w', 'delta_e_ln_b': 'delta_w', 'delta_e_w_s': 'delta_w', 'delta_e_b_s': 'delta_w', 'delta_e_w_out': 'delta_w', 'delta_o_norm': 'delta_w', 'delta_o_pw1': 'delta_w', 'delta_o_pw1_b': 'delta_w', 'delta_o_dw': 'delta_w', 'delta_o_dw_b': 'delta_w', 'delta_o_ln_g': 'delta_w', 'delta_o_ln_b': 'delta_w', 'delta_o_pw2': 'delta_w', 'delta_o_pw2_b': 'delta_w', 'delta_f_norm': 'delta_w', 'delta_f_w1': 'delta_w', 'delta_f_w2': 'delta_w', 'delta_final_norm': 'delta_w', 'new_m_e_norm': 'new_m', 'new_m_e_w_in': 'new_m', 'new_m_e_conv_w': 'new_m', 'new_m_e_a_log': 'new_m', 'new_m_e_dt_bias': 'new_m', 'new_m_e_o_norm': 'new_m', 'new_m_e_ln_g': 'new_m', 'new_m_e_ln_b': 'new_m', 'new_m_e_w_s': 'new_m', 'new_m_e_b_s': 'new_m', 'new_m_e_w_out': 'new_m', 'new_m_o_norm': 'new_m', 'new_m_o_pw1': 'new_m', 'new_m_o_pw1_b': 'new_m', 'new_m_o_dw': 'new_m', 'new_m_o_dw_b': 'new_m', 'new_m_o_ln_g': 'new_m', 'new_m_o_ln_b': 'new_m', 'new_m_o_pw2': 'new_m', 'new_m_o_pw2_b': 'new_m', 'new_m_f_norm': 'new_m', 'new_m_f_w1': 'new_m', 'new_m_f_w2': 'new_m', 'new_m_final_norm': 'new_m', 'new_v_e_norm': 'new_v', 'new_v_e_w_in': 'new_v', 'new_v_e_conv_w': 'new_v', 'new_v_e_a_log': 'new_v', 'new_v_e_dt_bias': 'new_v', 'new_v_e_o_norm': 'new_v', 'new_v_e_ln_g': 'new_v', 'new_v_e_ln_b': 'new_v', 'new_v_e_w_s': 'new_v', 'new_v_e_b_s': 'new_v', 'new_v_e_w_out': 'new_v', 'new_v_o_norm': 'new_v', 'new_v_o_pw1': 'new_v', 'new_v_o_pw1_b': 'new_v', 'new_v_o_dw': 'new_v', 'new_v_o_dw_b': 'new_v', 'new_v_o_ln_g': 'new_v', 'new_v_o_ln_b': 'new_v', 'new_v_o_pw2': 'new_v', 'new_v_o_pw2_b': 'new_v', 'new_v_f_norm': 'new_v', 'new_v_f_w1': 'new_v', 'new_v_f_w2': 'new_v', 'new_v_final_norm': 'new_v'}


def _forward(args):
    return _fwd_reference(*[args[k] for k in FWD_PARAMS])


def _output_shape():
    def fwd():
        inp = _fwd_setup_inputs(0)
        return _fwd_reference(*[inp[k] for k in FWD_PARAMS])
    out = _jax.eval_shape(fwd)
    return out.shape, out.dtype

N_MICROBATCH = 1
ADAM_LR = 0.001
ADAM_B1 = 0.9
ADAM_B2 = 0.999
ADAM_EPS = 1e-08
ADAM_WD = 0.01
ADAM_STEP = 10
PER_EXAMPLE_BATCH_AXIS = {'x': 0, 'loss_target': 0}
SHARED_INPUTS = []
_WEIGHT_DTYPES = {'e_norm': _jnp.float32, 'e_w_in': _jnp.float32, 'e_conv_w': _jnp.float32, 'e_a_log': _jnp.float32, 'e_dt_bias': _jnp.float32, 'e_o_norm': _jnp.float32, 'e_ln_g': _jnp.float32, 'e_ln_b': _jnp.float32, 'e_w_s': _jnp.float32, 'e_b_s': _jnp.float32, 'e_w_out': _jnp.float32, 'o_norm': _jnp.float32, 'o_pw1': _jnp.float32, 'o_pw1_b': _jnp.float32, 'o_dw': _jnp.float32, 'o_dw_b': _jnp.float32, 'o_ln_g': _jnp.float32, 'o_ln_b': _jnp.float32, 'o_pw2': _jnp.float32, 'o_pw2_b': _jnp.float32, 'f_norm': _jnp.float32, 'f_w1': _jnp.float32, 'f_w2': _jnp.float32, 'final_norm': _jnp.float32}
MOMENT_SCALE = {'e_norm': 8.632185e-02, 'e_w_in': 5.019449e-02, 'e_conv_w': 4.024012e-02, 'e_a_log': 3.020004e-01, 'e_dt_bias': 2.945516e-01, 'e_o_norm': 1.604701e-01, 'e_ln_g': 4.446126e-02, 'e_ln_b': 4.972860e-02, 'e_w_s': 4.644787e-02, 'e_b_s': 5.758355e-02, 'e_w_out': 8.304999e-02, 'o_norm': 4.761373e-02, 'o_pw1': 3.385417e-02, 'o_pw1_b': 9.549737e-02, 'o_dw': 4.749893e-02, 'o_dw_b': 2.293477e-01, 'o_ln_g': 9.613910e-02, 'o_ln_b': 1.324429e-01, 'o_pw2': 6.846678e-02, 'o_pw2_b': 2.710417e-01, 'f_norm': 8.101699e-02, 'f_w1': 4.056786e-02, 'f_w2': 1.356228e-01, 'final_norm': 1.638058e+01}


def _to_microbatches(a, axis):
    t = _jnp.moveaxis(a, axis, 0)
    t = t.reshape((N_MICROBATCH, t.shape[0] // N_MICROBATCH) + t.shape[1:])
    return _jnp.moveaxis(t, 1, axis + 1)


def setup_inputs(seed: int = 0) -> dict:
    inp = _fwd_setup_inputs(seed)
    key = _jax.random.fold_in(_jax.random.key(seed), 7919)
    shape, _ = _output_shape()
    out = dict(inp)
    out["loss_target"] = _jax.random.normal(_jax.random.fold_in(key, 0), shape, _jnp.float32)
    for i, name in enumerate(TWIN_WEIGHTS):
        w = inp[name].astype(_jnp.float32)
        if MOMENT_SCALE is None:
            s = _jnp.sqrt(_jnp.mean(_jnp.square(w)) + 1e-30)
        else:
            s = MOMENT_SCALE[name]
        km, kv = _jax.random.split(_jax.random.fold_in(key, i + 1))
        out[name] = w
        out["m_" + name] = s * _jax.random.normal(km, w.shape, _jnp.float32)
        out["v_" + name] = (s * s) * _jax.random.uniform(kv, w.shape, _jnp.float32, 0.5, 1.5)
    if N_MICROBATCH > 1:
        for name, axis in PER_EXAMPLE_BATCH_AXIS.items():
            out[name] = _to_microbatches(out[name], axis)
    return {'x': out['x'], 'e_norm': out['e_norm'], 'e_w_in': out['e_w_in'], 'e_conv_w': out['e_conv_w'], 'e_a_log': out['e_a_log'], 'e_dt_bias': out['e_dt_bias'], 'e_o_norm': out['e_o_norm'], 'e_ln_g': out['e_ln_g'], 'e_ln_b': out['e_ln_b'], 'e_w_s': out['e_w_s'], 'e_b_s': out['e_b_s'], 'e_w_out': out['e_w_out'], 'o_norm': out['o_norm'], 'o_pw1': out['o_pw1'], 'o_pw1_b': out['o_pw1_b'], 'o_dw': out['o_dw'], 'o_dw_b': out['o_dw_b'], 'o_ln_g': out['o_ln_g'], 'o_ln_b': out['o_ln_b'], 'o_pw2': out['o_pw2'], 'o_pw2_b': out['o_pw2_b'], 'f_norm': out['f_norm'], 'f_w1': out['f_w1'], 'f_w2': out['f_w2'], 'final_norm': out['final_norm'], 'loss_target': out['loss_target'], 'm_e_norm': out['m_e_norm'], 'm_e_w_in': out['m_e_w_in'], 'm_e_conv_w': out['m_e_conv_w'], 'm_e_a_log': out['m_e_a_log'], 'm_e_dt_bias': out['m_e_dt_bias'], 'm_e_o_norm': out['m_e_o_norm'], 'm_e_ln_g': out['m_e_ln_g'], 'm_e_ln_b': out['m_e_ln_b'], 'm_e_w_s': out['m_e_w_s'], 'm_e_b_s': out['m_e_b_s'], 'm_e_w_out': out['m_e_w_out'], 'm_o_norm': out['m_o_norm'], 'm_o_pw1': out['m_o_pw1'], 'm_o_pw1_b': out['m_o_pw1_b'], 'm_o_dw': out['m_o_dw'], 'm_o_dw_b': out['m_o_dw_b'], 'm_o_ln_g': out['m_o_ln_g'], 'm_o_ln_b': out['m_o_ln_b'], 'm_o_pw2': out['m_o_pw2'], 'm_o_pw2_b': out['m_o_pw2_b'], 'm_f_norm': out['m_f_norm'], 'm_f_w1': out['m_f_w1'], 'm_f_w2': out['m_f_w2'], 'm_final_norm': out['m_final_norm'], 'v_e_norm': out['v_e_norm'], 'v_e_w_in': out['v_e_w_in'], 'v_e_conv_w': out['v_e_conv_w'], 'v_e_a_log': out['v_e_a_log'], 'v_e_dt_bias': out['v_e_dt_bias'], 'v_e_o_norm': out['v_e_o_norm'], 'v_e_ln_g': out['v_e_ln_g'], 'v_e_ln_b': out['v_e_ln_b'], 'v_e_w_s': out['v_e_w_s'], 'v_e_b_s': out['v_e_b_s'], 'v_e_w_out': out['v_e_w_out'], 'v_o_norm': out['v_o_norm'], 'v_o_pw1': out['v_o_pw1'], 'v_o_pw1_b': out['v_o_pw1_b'], 'v_o_dw': out['v_o_dw'], 'v_o_dw_b': out['v_o_dw_b'], 'v_o_ln_g': out['v_o_ln_g'], 'v_o_ln_b': out['v_o_ln_b'], 'v_o_pw2': out['v_o_pw2'], 'v_o_pw2_b': out['v_o_pw2_b'], 'v_f_norm': out['v_f_norm'], 'v_f_w1': out['v_f_w1'], 'v_f_w2': out['v_f_w2'], 'v_final_norm': out['v_final_norm']}


def _loss(weights, diff, rest, loss_target):
    with _jax.named_scope("forward"):
        args = {**rest, TWIN_DIFF_INPUT: diff, **{k: w.astype(_WEIGHT_DTYPES[k]) for k, w in weights.items()}}
        y = _forward(args)
    with _jax.named_scope("loss_head"):
        err = _jnp.square(y.astype(_jnp.float32) - loss_target)
        return 0.5 * _jnp.sum(_jnp.mean(err, axis=-1)) if err.ndim else 0.5 * err


def _adamw(w, g, m, v):
    m = ADAM_B1 * m + (1.0 - ADAM_B1) * g
    v = ADAM_B2 * v + (1.0 - ADAM_B2) * _jnp.square(g)
    m_hat = m / (1.0 - ADAM_B1 ** ADAM_STEP)
    v_hat = v / (1.0 - ADAM_B2 ** ADAM_STEP)
    delta = -ADAM_LR * (m_hat / (_jnp.sqrt(v_hat) + ADAM_EPS) + ADAM_WD * w)
    return delta, m, v


def reference(x, e_norm, e_w_in, e_conv_w, e_a_log, e_dt_bias, e_o_norm, e_ln_g, e_ln_b, e_w_s, e_b_s, e_w_out, o_norm, o_pw1, o_pw1_b, o_dw, o_dw_b, o_ln_g, o_ln_b, o_pw2, o_pw2_b, f_norm, f_w1, f_w2, final_norm, loss_target, m_e_norm, m_e_w_in, m_e_conv_w, m_e_a_log, m_e_dt_bias, m_e_o_norm, m_e_ln_g, m_e_ln_b, m_e_w_s, m_e_b_s, m_e_w_out, m_o_norm, m_o_pw1, m_o_pw1_b, m_o_dw, m_o_dw_b, m_o_ln_g, m_o_ln_b, m_o_pw2, m_o_pw2_b, m_f_norm, m_f_w1, m_f_w2, m_final_norm, v_e_norm, v_e_w_in, v_e_conv_w, v_e_a_log, v_e_dt_bias, v_e_o_norm, v_e_ln_g, v_e_ln_b, v_e_w_s, v_e_b_s, v_e_w_out, v_o_norm, v_o_pw1, v_o_pw1_b, v_o_dw, v_o_dw_b, v_o_ln_g, v_o_ln_b, v_o_pw2, v_o_pw2_b, v_f_norm, v_f_w1, v_f_w2, v_final_norm):
    given = dict(x=x, e_norm=e_norm, e_w_in=e_w_in, e_conv_w=e_conv_w, e_a_log=e_a_log, e_dt_bias=e_dt_bias, e_o_norm=e_o_norm, e_ln_g=e_ln_g, e_ln_b=e_ln_b, e_w_s=e_w_s, e_b_s=e_b_s, e_w_out=e_w_out, o_norm=o_norm, o_pw1=o_pw1, o_pw1_b=o_pw1_b, o_dw=o_dw, o_dw_b=o_dw_b, o_ln_g=o_ln_g, o_ln_b=o_ln_b, o_pw2=o_pw2, o_pw2_b=o_pw2_b, f_norm=f_norm, f_w1=f_w1, f_w2=f_w2, final_norm=final_norm, loss_target=loss_target, m_e_norm=m_e_norm, m_e_w_in=m_e_w_in, m_e_conv_w=m_e_conv_w, m_e_a_log=m_e_a_log, m_e_dt_bias=m_e_dt_bias, m_e_o_norm=m_e_o_norm, m_e_ln_g=m_e_ln_g, m_e_ln_b=m_e_ln_b, m_e_w_s=m_e_w_s, m_e_b_s=m_e_b_s, m_e_w_out=m_e_w_out, m_o_norm=m_o_norm, m_o_pw1=m_o_pw1, m_o_pw1_b=m_o_pw1_b, m_o_dw=m_o_dw, m_o_dw_b=m_o_dw_b, m_o_ln_g=m_o_ln_g, m_o_ln_b=m_o_ln_b, m_o_pw2=m_o_pw2, m_o_pw2_b=m_o_pw2_b, m_f_norm=m_f_norm, m_f_w1=m_f_w1, m_f_w2=m_f_w2, m_final_norm=m_final_norm, v_e_norm=v_e_norm, v_e_w_in=v_e_w_in, v_e_conv_w=v_e_conv_w, v_e_a_log=v_e_a_log, v_e_dt_bias=v_e_dt_bias, v_e_o_norm=v_e_o_norm, v_e_ln_g=v_e_ln_g, v_e_ln_b=v_e_ln_b, v_e_w_s=v_e_w_s, v_e_b_s=v_e_b_s, v_e_w_out=v_e_w_out, v_o_norm=v_o_norm, v_o_pw1=v_o_pw1, v_o_pw1_b=v_o_pw1_b, v_o_dw=v_o_dw, v_o_dw_b=v_o_dw_b, v_o_ln_g=v_o_ln_g, v_o_ln_b=v_o_ln_b, v_o_pw2=v_o_pw2, v_o_pw2_b=v_o_pw2_b, v_f_norm=v_f_norm, v_f_w1=v_f_w1, v_f_w2=v_f_w2, v_final_norm=v_final_norm)
    weights = {n: given[n] for n in TWIN_WEIGHTS}
    shared = {n: given[n] for n in SHARED_INPUTS}
    per_example = {n: given[n] for n in ['x']}
    grad_fn = _jax.value_and_grad(_loss, argnums=(0, 1))

    def one_microbatch(ex, loss_target):
        ex = dict(ex)
        diff = ex.pop(TWIN_DIFF_INPUT)
        return grad_fn(weights, diff, {**shared, **ex}, loss_target)

    if N_MICROBATCH == 1:
        loss, (grad_w, grad_x) = one_microbatch(per_example, given["loss_target"])
    else:
        def body(carry, xs):
            loss_sum, grad_sum = carry
            l_k, (gw_k, gx_k) = one_microbatch(xs[0], xs[1])
            with _jax.named_scope("update"):
                return (loss_sum + l_k, _jax.tree.map(_jnp.add, grad_sum, gw_k)), gx_k

        init = (_jnp.zeros((), _jnp.float32), _jax.tree.map(_jnp.zeros_like, weights))
        (loss, grad_w), grad_x = _jax.lax.scan(body, init, (per_example, given["loss_target"]))
    with _jax.named_scope("update"):
        delta_w, new_m, new_v = {}, {}, {}
        for n in TWIN_WEIGHTS:
            delta_w[n], new_m[n], new_v[n] = _adamw(weights[n], grad_w[n], given["m_" + n], given["v_" + n])
    return (loss, grad_x, *[grad_w[n] for n in TWIN_WEIGHTS], *[delta_w[n] for n in TWIN_WEIGHTS],
            *[new_m[n] for n in TWIN_WEIGHTS], *[new_v[n] for n in TWIN_WEIGHTS])
```

```python
import functools
import math

import jax
import jax.numpy as jnp
import numpy as np
from jax import lax
from jax.experimental import pallas as pl
from jax.experimental.pallas import tpu as pltpu

F32 = jnp.float32
BF = jnp.bfloat16
EPS = 1e-6
CHUNK = 64
B_BLOCK = 128
LANE = 128
SUBLANE = 8
N_DEV = 8
VMEM_LIMIT = 56 * 1024 * 1024

ADAM_LR = 0.001
ADAM_B1 = 0.9
ADAM_B2 = 0.999
ADAM_EPS = 1e-08
ADAM_WD = 0.01
ADAM_STEP = 10

MESH = pl.DeviceIdType.MESH
ANY = pl.BlockSpec(memory_space=pl.ANY)


def _tile(n, pref, mult=SUBLANE):
    if n <= pref:
        return n
    t = (pref // mult) * mult
    while t >= mult:
        if n % t == 0:
            return t
        t -= mult
    return n


def _dg(a, b, ca, cb, hi):
    dims = (((ca,), (cb,)), ((), ()))
    if hi:
        return lax.dot_general(a.astype(F32), b.astype(F32), dims,
                               precision=lax.Precision.HIGHEST, preferred_element_type=F32)
    return lax.dot_general(a.astype(BF), b.astype(BF), dims, preferred_element_type=F32)


@functools.partial(jax.custom_vjp, nondiff_argnums=(2, 3, 4))
def mm(a, b, ca, cb, hi=False):
    return _dg(a, b, ca, cb, hi)


def _mm_fwd(a, b, ca, cb, hi):
    return _dg(a, b, ca, cb, hi), (a, b)


def _mm_bwd(ca, cb, hi, res, g):
    a, b = res
    if ca == 1:
        da = mm(g, b, 1, 1 - cb, hi)
    else:
        da = mm(b, g, 1 - cb, 1, hi)
    if cb == 0:
        db = mm(a, g, 1 - ca, 0, hi)
    else:
        db = mm(g, a, 0, 1 - ca, hi)
    return da.astype(a.dtype), db.astype(b.dtype)


mm.defvjp(_mm_fwd, _mm_bwd)


def matmul(name, a, b, mode, epi=None, extras=(), out_dtypes=(F32,), colshard=False,
           tm=512, tn=1024, tk=2048):
    if mode == "nn":
        (M, K), (K2, N) = a.shape, b.shape
    elif mode == "nt":
        (M, K), (N, K2) = a.shape, b.shape
    else:
        (K, M), (K2, N) = a.shape, b.shape
    assert K == K2, (name, a.shape, b.shape, mode)
    tm = _tile(M, tm)
    tn = N // N_DEV if colshard else _tile(N, tn, LANE)
    tk = _tile(K, tk, LANE)
    nk = K // tk
    grid = (M // tm, N // tn, nk)
    if mode == "nn":
        a_spec = pl.BlockSpec((tm, tk), lambda i, j, k: (i, k))
        b_spec = pl.BlockSpec((tk, tn), lambda i, j, k: (k, j))
        ca, cb = 1, 0
    elif mode == "nt":
        a_spec = pl.BlockSpec((tm, tk), lambda i, j, k: (i, k))
        b_spec = pl.BlockSpec((tn, tk), lambda i, j, k: (j, k))
        ca, cb = 1, 1
    else:
        a_spec = pl.BlockSpec((tk, tm), lambda i, j, k: (k, i))
        b_spec = pl.BlockSpec((tk, tn), lambda i, j, k: (k, j))
        ca, cb = 0, 0
    ex_specs = []
    for e in extras:
        if e.shape[0] == 1:
            ex_specs.append(pl.BlockSpec((1, tn), lambda i, j, k: (0, j)))
        else:
            assert e.shape == (M, N), (name, e.shape)
            ex_specs.append(pl.BlockSpec((tm, tn), lambda i, j, k: (i, j)))
    if colshard:
        out_shape = [jax.ShapeDtypeStruct((N_DEV, M, tn), dt) for dt in out_dtypes]
        out_specs = [pl.BlockSpec((None, tm, tn), lambda i, j, k: (j, i, 0)) for _ in out_dtypes]
    else:
        out_shape = [jax.ShapeDtypeStruct((M, N), dt) for dt in out_dtypes]
        out_specs = [pl.BlockSpec((tm, tn), lambda i, j, k: (i, j)) for _ in out_dtypes]
    n_ex, n_out = len(extras), len(out_dtypes)

    def body(*refs):
        a_ref, b_ref = refs[0], refs[1]
        ex_refs = refs[2:2 + n_ex]
        o_refs = refs[2 + n_ex:2 + n_ex + n_out]
        acc_ref = refs[-1]
        k = pl.program_id(2)
        part = _dg(a_ref[...], b_ref[...], ca, cb, False)

        @pl.when(k == 0)
        def _():
            acc_ref[...] = part

        @pl.when(k > 0)
        def _():
            acc_ref[...] += part

        @pl.when(k == nk - 1)
        def _():
            acc = acc_ref[...]
            res = (acc,) if epi is None else epi(acc, *[r[...] for r in ex_refs])
            for o_ref, r in zip(o_refs, res):
                o_ref[...] = r.astype(o_ref.dtype)

    outs = pl.pallas_call(
        body, name=name, grid=grid,
        in_specs=[a_spec, b_spec] + ex_specs, out_specs=out_specs, out_shape=out_shape,
        scratch_shapes=[pltpu.VMEM((tm, tn), F32)],
        compiler_params=pltpu.CompilerParams(
            dimension_semantics=("parallel", "parallel", "arbitrary"),
            vmem_limit_bytes=VMEM_LIMIT),
    )(a, b, *extras)
    return outs[0] if n_out == 1 else tuple(outs)


def rowcall(name, fn, rows, consts, out_rows, out_accs=(), tr=256):
    rows = [r if isinstance(r, tuple) else (r, None) for r in rows]
    R = rows[0][0].shape[-2]
    tr = _tile(R, tr)
    n = R // tr
    in_specs = []
    for arr, lead in rows:
        assert arr.shape[-2] == R, (name, arr.shape, R)
        C = arr.shape[-1]
        if lead is None:
            in_specs.append(pl.BlockSpec((tr, C), lambda i: (i, 0)))
        else:
            in_specs.append(pl.BlockSpec((None, tr, C), lambda i, lead=lead: (lead, i, 0)))
    for c in consts:
        in_specs.append(pl.BlockSpec(c.shape, lambda i, nd=c.ndim: (0,) * nd))
    out_shape = [jax.ShapeDtypeStruct((R, C), dt) for C, dt in out_rows]
    out_specs = [pl.BlockSpec((tr, C), lambda i: (i, 0)) for C, _ in out_rows]
    for shp in out_accs:
        out_shape.append(jax.ShapeDtypeStruct(shp, F32))
        out_specs.append(pl.BlockSpec(shp, lambda i, nd=len(shp): (0,) * nd))
    n_in, n_row, n_acc = len(rows) + len(consts), len(out_rows), len(out_accs)

    def body(*refs):
        ins = [r[...] for r in refs[:n_in]]
        res = fn(*ins)
        if not isinstance(res, (tuple, list)):
            res = (res,)
        o_refs = refs[n_in:]
        for o_ref, r in zip(o_refs[:n_row], res[:n_row]):
            o_ref[...] = r.astype(o_ref.dtype)
        if n_acc:
            first = pl.program_id(0) == 0
            for o_ref, r in zip(o_refs[n_row:], res[n_row:]):
                r = r.astype(F32).reshape(o_ref.shape)

                @pl.when(first)
                def _(o_ref=o_ref, r=r):
                    o_ref[...] = r

                @pl.when(jnp.logical_not(first))
                def _(o_ref=o_ref, r=r):
                    o_ref[...] += r

    outs = pl.pallas_call(
        body, name=name, grid=(n,),
        in_specs=in_specs, out_specs=out_specs, out_shape=out_shape,
        compiler_params=pltpu.CompilerParams(
            dimension_semantics=("arbitrary",), vmem_limit_bytes=VMEM_LIMIT),
    )(*[a for a, _ in rows], *consts)
    return outs[0] if len(outs) == 1 else tuple(outs)


def rms_tile(x, g):
    x = x.astype(F32)
    return x * lax.rsqrt(jnp.mean(x * x, axis=-1, keepdims=True) + EPS) * g


def gelu(x):
    return 0.5 * x * (1.0 + lax.erf(x * (1.0 / math.sqrt(2.0))))


def ln_tile(x, g, b):
    mu = jnp.mean(x, axis=-1, keepdims=True)
    xc = x - mu
    return xc * lax.rsqrt(jnp.mean(xc * xc, axis=-1, keepdims=True) + EPS) * g + b


def lane_groups(fn, width, *arrs):
    n = arrs[0].shape[-1] // width
    outs = [fn(*[a[:, i * width:(i + 1) * width] for a in arrs]) for i in range(n)]
    return jnp.concatenate(outs, axis=-1)


def mixa_post_tile(o, z, o_norm):
    dv = o_norm.shape[-1]
    on = lane_groups(lambda t: rms_tile(t, o_norm), dv, o)
    return on * jax.nn.silu(z)


def mixb_tile(uv, ln_g, ln_b, w_s, bs_t):
    G = w_s.shape[0]
    gw = ln_g.shape[-1]
    dg = gw // G
    tr = uv.shape[0]
    u = gelu(uv[:, :gw])
    vg = gelu(uv[:, gw:])
    ii = lax.broadcasted_iota(jnp.int32, (B_BLOCK, B_BLOCK), 0)
    jj = lax.broadcasted_iota(jnp.int32, (B_BLOCK, B_BLOCK), 1)
    mask = (jj // CHUNK) <= (ii // CHUNK)
    cols = []
    for g in range(G):
        sl = slice(g * dg, (g + 1) * dg)
        vn = ln_tile(vg[:, sl], ln_g[:, sl], ln_b[:, sl])
        wm = jnp.where(mask, w_s[g], 0.0)
        blocks = []
        for m in range(tr // B_BLOCK):
            blk = vn[m * B_BLOCK:(m + 1) * B_BLOCK, :]
            blocks.append(mm(wm, blk, 1, 0) + bs_t[:, g:g + 1])
        mixed = blocks[0] if len(blocks) == 1 else jnp.concatenate(blocks, axis=0)
        cols.append(u[:, sl] * mixed)
    return jnp.concatenate(cols, axis=-1)


def bgc_tile(ba, alog_row, dtb_row):
    tr = ba.shape[0]
    beta = jax.nn.sigmoid(ba)
    g = -jnp.exp(alog_row) * jax.nn.softplus(ba + dtb_row)
    ii = lax.broadcasted_iota(jnp.int32, (tr, tr), 0)
    jj = lax.broadcasted_iota(jnp.int32, (tr, tr), 1)
    tri = jnp.where((ii // CHUNK == jj // CHUNK) & (jj <= ii), 1.0, 0.0).astype(F32)
    gc = mm(tri, g, 1, 0, True)
    return beta, gc


def make_bgc(H):
    def f(ba, alog_row, dtb_row):
        beta, gc = bgc_tile(ba, alog_row, dtb_row)
        lane = lax.broadcasted_iota(jnp.int32, ba.shape, 1)
        return jnp.where(lane < H, beta, jnp.where(lane < 2 * H, gc, 0.0))
    return f


def loss_tile(x, g, target):
    y = rms_tile(x, g)
    err = y - target
    return 0.5 * jnp.sum(jnp.mean(err * err, axis=-1, keepdims=True), axis=0, keepdims=True)


def adamw_tile(w, g, m, v):
    m = ADAM_B1 * m + (1.0 - ADAM_B1) * g
    v = ADAM_B2 * v + (1.0 - ADAM_B2) * (g * g)
    m_hat = m / (1.0 - ADAM_B1 ** ADAM_STEP)
    v_hat = v / (1.0 - ADAM_B2 ** ADAM_STEP)
    delta = -ADAM_LR * (m_hat / (jnp.sqrt(v_hat) + ADAM_EPS) + ADAM_WD * w)
    return delta, m, v


def _conv_specs(T, tr, hb, cw, col_blocks, rev):
    n = T // tr

    def ri(i):
        return (n - 1 - i) if rev else i

    tile_specs, halo_specs = [], []
    for off in col_blocks:
        tile_specs.append(pl.BlockSpec((tr, cw), lambda j, i, off=off: (ri(i), j + off)))
        halo_specs.append(pl.BlockSpec(
            (hb, cw), lambda j, i, off=off: (jnp.maximum(ri(i) * (tr // hb) - 1, 0), j + off)))
    return n, ri, tile_specs, halo_specs


def conv_fwd(name, x, w, consts, pre, post, col_blocks, n_out, K, out_dtype=F32, tr=256, hb=8, cw=512):
    T = x.shape[0]
    C = w.shape[1]
    tr, cw = _tile(T, tr, hb), min(cw, C)
    nb = len(col_blocks)
    n, ri, tile_specs, halo_specs = _conv_specs(T, tr, hb, cw, col_blocks, False)
    w_spec = pl.BlockSpec((K, cw), lambda j, i: (0, j))
    c_specs = [pl.BlockSpec((1, cw), lambda j, i: (0, j)) for _ in consts]

    def body(*refs):
        tiles = [r[...] for r in refs[:nb]]
        halos = [r[...] for r in refs[nb:2 * nb]]
        w_ref = refs[2 * nb]
        cs = [r[...] for r in refs[2 * nb + 1:2 * nb + 1 + len(consts)]]
        o_refs = refs[2 * nb + 1 + len(consts):-1]
        pbuf = refs[-1]
        i = pl.program_id(1)
        pbuf[0:hb, :] = jnp.where(i > 0, pre(*halos), 0.0)
        pbuf[hb:hb + tr, :] = pre(*tiles)
        c = jnp.zeros((tr, cw), F32)
        for k in range(K):
            c = c + w_ref[k:k + 1, :] * pbuf[pl.ds(hb - (K - 1) + k, tr), :]
        res = post(c, pl.program_id(0), *cs)
        for o_ref, r in zip(o_refs, res):
            o_ref[...] = r.astype(o_ref.dtype)

    outs = pl.pallas_call(
        body, name=name, grid=(C // cw, n),
        in_specs=tile_specs + halo_specs + [w_spec] + c_specs,
        out_specs=[pl.BlockSpec((tr, cw), lambda j, i: (i, j)) for _ in range(n_out)],
        out_shape=[jax.ShapeDtypeStruct((T, C), out_dtype) for _ in range(n_out)],
        scratch_shapes=[pltpu.VMEM((hb + tr, cw), F32)],
        compiler_params=pltpu.CompilerParams(
            dimension_semantics=("parallel", "arbitrary"), vmem_limit_bytes=VMEM_LIMIT),
    )(*([x] * nb), *([x] * nb), w, *consts)
    return outs[0] if n_out == 1 else tuple(outs)


def conv_bwd(name, x, w, consts, grads, pre, post, col_blocks, K, tr=256, hb=8, cw=512):
    T, Cx = x.shape
    C = w.shape[1]
    tr, cw = _tile(T, tr, hb), min(cw, C)
    nb = len(col_blocks)
    n, ri, tile_specs, halo_specs = _conv_specs(T, tr, hb, cw, col_blocks, True)
    w_spec = pl.BlockSpec((K, cw), lambda j, i: (0, j))
    c_specs = [pl.BlockSpec((1, cw), lambda j, i: (0, j)) for _ in consts]
    g_specs = [pl.BlockSpec((tr, cw), lambda j, i: (ri(i), j)) for _ in grads]
    nc, ng = len(consts), len(grads)

    def body(*refs):
        p = 0
        tile_refs = refs[p:p + nb]; p += nb
        halo_refs = refs[p:p + nb]; p += nb
        w_ref = refs[p]; p += 1
        cs = [r[...] for r in refs[p:p + nc]]; p += nc
        gs = [r[...] for r in refs[p:p + ng]]; p += ng
        dx_refs = refs[p:p + nb]; p += nb
        sum_refs = refs[p:p + nb]; p += nb
        dw_ref = refs[p]; p += 1
        dc_refs = refs[p:p + nc]; p += nc
        pbuf, dbuf, ebuf, carry = refs[p:p + 4]
        i = pl.program_id(1)
        first = i == 0
        tiles = [r[...] for r in tile_refs]
        halos = [r[...] for r in halo_refs]
        p_tile, vjp_pre = jax.vjp(pre, *tiles)
        pbuf[0:hb, :] = jnp.where(ri(i) > 0, pre(*halos), 0.0)
        pbuf[hb:hb + tr, :] = p_tile
        c = jnp.zeros((tr, cw), F32)
        for k in range(K):
            c = c + w_ref[k:k + 1, :] * pbuf[pl.ds(hb - (K - 1) + k, tr), :]
        cid = pl.program_id(0)
        _, vjp_post = jax.vjp(lambda c_, *cs_: post(c_, cid, *cs_), c, *cs)
        dres = vjp_post(tuple(g.astype(F32) for g in gs))
        dc = dres[0]
        dbuf[0:hb, :] = jnp.zeros((hb, cw), F32)
        dbuf[hb:hb + tr, :] = dc
        dbuf[hb + tr:hb + tr + hb, :] = jnp.zeros((hb, cw), F32)
        dp = jnp.zeros((hb + tr, cw), F32)
        dws = []
        for k in range(K):
            dp = dp + w_ref[k:k + 1, :] * dbuf[pl.ds(K - 1 - k, hb + tr), :]
            dws.append(jnp.sum(dc * pbuf[pl.ds(hb - (K - 1) + k, tr), :], axis=0, keepdims=True))
        dw = jnp.concatenate(dws, axis=0)
        ebuf[...] = dp

        @pl.when(jnp.logical_not(first))
        def _():
            ebuf[tr:tr + hb, :] += carry[...]

        carry[...] = ebuf[0:hb, :]
        dtiles = vjp_pre(ebuf[hb:hb + tr, :])
        for r, s, d in zip(dx_refs, sum_refs, dtiles):
            r[...] = d.astype(r.dtype)
            ds_ = jnp.sum(d, axis=0, keepdims=True)

            @pl.when(first)
            def _(s=s, ds_=ds_):
                s[...] = ds_

            @pl.when(jnp.logical_not(first))
            def _(s=s, ds_=ds_):
                s[...] += ds_

        accs = [(dw_ref, dw)] + [(r, d) for r, d in zip(dc_refs, dres[1:])]
        for r, d in accs:
            @pl.when(first)
            def _(r=r, d=d):
                r[...] = d

            @pl.when(jnp.logical_not(first))
            def _(r=r, d=d):
                r[...] += d

    n_cb = C // cw
    outs = pl.pallas_call(
        body, name=name, grid=(n_cb, n),
        in_specs=tile_specs + halo_specs + [w_spec] + c_specs + g_specs,
        out_specs=([pl.BlockSpec((tr, cw), lambda j, i: (ri(i), j)) for _ in col_blocks]
                   + [pl.BlockSpec((1, cw), lambda j, i: (0, j)) for _ in col_blocks]
                   + [pl.BlockSpec((K, cw), lambda j, i: (0, j))]
                   + [pl.BlockSpec((1, cw), lambda j, i: (0, j)) for _ in consts]),
        out_shape=([jax.ShapeDtypeStruct((T, C), BF) for _ in col_blocks]
                   + [jax.ShapeDtypeStruct((1, C), F32) for _ in col_blocks]
                   + [jax.ShapeDtypeStruct((K, C), F32)]
                   + [jax.ShapeDtypeStruct((1, C), F32) for _ in consts]),
        scratch_shapes=[pltpu.VMEM((hb + tr, cw), F32), pltpu.VMEM((hb + tr + hb, cw), F32),
                        pltpu.VMEM((hb + tr, cw), F32), pltpu.VMEM((hb, cw), F32)],
        compiler_params=pltpu.CompilerParams(
            dimension_semantics=("parallel", "arbitrary"), vmem_limit_bytes=VMEM_LIMIT),
    )(*([x] * nb), *([x] * nb), w, *consts, *grads)
    dxs = outs[:nb]
    sums = outs[nb:2 * nb]
    dw = outs[2 * nb]
    dcs = outs[2 * nb + 1:]
    return dxs, sums, dw, dcs


def make_qkv_post(dk, cw, n_qk_chunks):
    def l2(t):
        return t * lax.rsqrt(jnp.sum(t * t, axis=-1, keepdims=True) + EPS)

    def post(c, cid):
        s = jax.nn.silu(c)
        normed = lane_groups(l2, dk, s)
        return (jnp.where(cid < n_qk_chunks, normed, s),)
    return post


def glu_pre(za, zb):
    return za * jax.nn.sigmoid(zb)


def bias_post(c, cid, b):
    return (c + b,)


def _col_to_row(col):
    C = col.shape[0]
    ii = lax.broadcasted_iota(jnp.int32, (C, C), 0)
    jj = lax.broadcasted_iota(jnp.int32, (C, C), 1)
    return jnp.sum(jnp.where(ii == jj, jnp.broadcast_to(col, (C, C)), 0.0), axis=0, keepdims=True)


@jax.custom_vjp
def solve_with_inverse(a, rhs, x):
    return mm(x, rhs, 1, 0, True)


def _swi_fwd(a, rhs, x):
    sol = mm(x, rhs, 1, 0, True)
    return sol, (x, sol)


def _swi_bwd(res, dsol):
    x, sol = res
    drhs = mm(x, dsol, 0, 0, True)
    da = -mm(drhs, sol, 1, 1, True)
    return da, drhs, jnp.zeros_like(x)


solve_with_inverse.defvjp(_swi_fwd, _swi_bwd)


def unit_lower_inverse(a):
    C = a.shape[0]
    ii = lax.broadcasted_iota(jnp.int32, (C, C), 0)
    jj = lax.broadcasted_iota(jnp.int32, (C, C), 1)
    x = jnp.where(ii == jj, 1.0, 0.0).astype(F32) - a
    p = mm(a, a, 1, 0, True)
    steps = int(math.log2(C)) - 1
    for s in range(steps):
        x = x + mm(x, p, 1, 0, True)
        if s < steps - 1:
            p = mm(p, p, 1, 0, True)
    return x


def dn_masks(C):
    ii = lax.broadcasted_iota(jnp.int32, (C, C), 0)
    jj = lax.broadcasted_iota(jnp.int32, (C, C), 1)
    return ii >= jj, ii > jj


def dn_pre(q, k, v, beta, gc):
    C, dk = q.shape
    tri, strict = dn_masks(C)
    q = q * (dk ** -0.5)
    diff = gc - _col_to_row(gc)
    decay = jnp.where(tri, jnp.exp(jnp.where(tri, diff, 0.0)), 0.0)
    kb = k * beta
    vb = v * beta
    a = jnp.where(strict, mm(kb, k, 1, 1) * decay, 0.0)
    eg = jnp.exp(gc)
    rhs = jnp.concatenate([vb, kb * eg], axis=-1)
    attn = mm(q, k, 1, 1) * decay
    qd = q * eg
    g_last = gc[C - 1:C, :]
    kt = k * jnp.exp(g_last - gc)
    gl = jnp.exp(g_last)
    return a, rhs, attn, qd, kt, gl


def dn_chunk(q, k, v, beta, gc, state, x):
    dv = v.shape[-1]
    a, rhs, attn, qd, kt, gl = dn_pre(q, k, v, beta, gc)
    sol = solve_with_inverse(a, rhs, x)
    u, w = sol[:, :dv], sol[:, dv:]
    vn = u - mm(w, state, 1, 0)
    o = mm(qd, state, 1, 0) + mm(attn, vn, 1, 0)
    new_state = state * gl + mm(kt, vn, 0, 0)
    return o, new_state


def _pick_col(block, idx):
    lane = lax.broadcasted_iota(jnp.int32, block.shape, 1)
    return jnp.sum(jnp.where(lane == idx, block, 0.0), axis=1, keepdims=True)


def deltanet_fwd(q, k, v, bgc, H):
    T = q.shape[0]
    dk, dv = q.shape[1] // H, v.shape[1] // H
    N = T // CHUNK

    def body(q_ref, k_ref, v_ref, bgc_ref, o_ref, x_ref, s_ref, state):
        n, h = pl.program_id(0), pl.program_id(1)

        @pl.when(n == 0)
        def _():
            state[h] = jnp.zeros((dk, dv), F32)

        bg = bgc_ref[...]
        beta, gc = _pick_col(bg, h), _pick_col(bg, h + H)
        qv, kv, vv = q_ref[...], k_ref[...], v_ref[...]
        a = dn_pre(qv, kv, vv, beta, gc)[0]
        x = unit_lower_inverse(a)
        s = state[h]
        o, s_new = dn_chunk(qv, kv, vv, beta, gc, s, x)
        o_ref[...] = o
        x_ref[...] = x
        s_ref[...] = s
        state[h] = s_new

    return pl.pallas_call(
        body, name="deltanet_fwd", grid=(N, H),
        in_specs=[pl.BlockSpec((CHUNK, dk), lambda n, h: (n, h)),
                  pl.BlockSpec((CHUNK, dk), lambda n, h: (n, h)),
                  pl.BlockSpec((CHUNK, dv), lambda n, h: (n, h)),
                  pl.BlockSpec((CHUNK, LANE), lambda n, h: (n, 0))],
        out_specs=[pl.BlockSpec((CHUNK, dv), lambda n, h: (n, h)),
                   pl.BlockSpec((None, None, CHUNK, CHUNK), lambda n, h: (n, h, 0, 0)),
                   pl.BlockSpec((None, None, dk, dv), lambda n, h: (n, h, 0, 0))],
        out_shape=[jax.ShapeDtypeStruct((T, H * dv), F32),
                   jax.ShapeDtypeStruct((N, H, CHUNK, CHUNK), F32),
                   jax.ShapeDtypeStruct((N, H, dk, dv), F32)],
        scratch_shapes=[pltpu.VMEM((H, dk, dv), F32)],
        compiler_params=pltpu.CompilerParams(
            dimension_semantics=("arbitrary", "arbitrary"), vmem_limit_bytes=VMEM_LIMIT),
    )(q, k, v, bgc)


def deltanet_bwd(q, k, v, bgc, xinv, states, do, H):
    T = q.shape[0]
    dk, dv = q.shape[1] // H, v.shape[1] // H
    N = T // CHUNK

    def body(q_ref, k_ref, v_ref, bgc_ref, x_ref, s_ref, do_ref,
             dq_ref, dk_ref, dv_ref, dbgc_ref, dstate):
        n, h = pl.program_id(0), pl.program_id(1)

        @pl.when(n == 0)
        def _():
            dstate[h] = jnp.zeros((dk, dv), F32)

        bg = bgc_ref[...]
        beta, gc = _pick_col(bg, h), _pick_col(bg, h + H)
        _, vjp = jax.vjp(dn_chunk, q_ref[...], k_ref[...], v_ref[...], beta, gc, s_ref[...], x_ref[...])
        dq, dk_, dv_, dbeta, dgc, ds, _ = vjp((do_ref[...], dstate[h]))
        dq_ref[...] = dq
        dk_ref[...] = dk_
        dv_ref[...] = dv_
        dstate[h] = ds
        lane = lax.broadcasted_iota(jnp.int32, (CHUNK, LANE), 1)
        contrib = jnp.where(lane == h, dbeta, 0.0) + jnp.where(lane == h + H, dgc, 0.0)

        @pl.when(h == 0)
        def _():
            dbgc_ref[...] = contrib

        @pl.when(h > 0)
        def _():
            dbgc_ref[...] += contrib

    rn = lambda n: N - 1 - n
    return pl.pallas_call(
        body, name="deltanet_bwd", grid=(N, H),
        in_specs=[pl.BlockSpec((CHUNK, dk), lambda n, h: (rn(n), h)),
                  pl.BlockSpec((CHUNK, dk), lambda n, h: (rn(n), h)),
                  pl.BlockSpec((CHUNK, dv), lambda n, h: (rn(n), h)),
                  pl.BlockSpec((CHUNK, LANE), lambda n, h: (rn(n), 0)),
                  pl.BlockSpec((None, None, CHUNK, CHUNK), lambda n, h: (rn(n), h, 0, 0)),
                  pl.BlockSpec((None, None, dk, dv), lambda n, h: (rn(n), h, 0, 0)),
                  pl.BlockSpec((CHUNK, dv), lambda n, h: (rn(n), h))],
        out_specs=[pl.BlockSpec((CHUNK, dk), lambda n, h: (rn(n), h)),
                   pl.BlockSpec((CHUNK, dk), lambda n, h: (rn(n), h)),
                   pl.BlockSpec((CHUNK, dv), lambda n, h: (rn(n), h)),
                   pl.BlockSpec((CHUNK, LANE), lambda n, h: (rn(n), 0))],
        out_shape=[jax.ShapeDtypeStruct((T, H * dk), F32),
                   jax.ShapeDtypeStruct((T, H * dk), F32),
                   jax.ShapeDtypeStruct((T, H * dv), F32),
                   jax.ShapeDtypeStruct((T, LANE), F32)],
        scratch_shapes=[pltpu.VMEM((H, dk, dv), F32)],
        compiler_params=pltpu.CompilerParams(
            dimension_semantics=("arbitrary", "arbitrary"), vmem_limit_bytes=VMEM_LIMIT),
    )(q, k, v, bgc, xinv, states, do)


def _place():
    x, y, c = lax.axis_index("x"), lax.axis_index("y"), lax.axis_index("c")
    chips = [(1 - x, y), (x, 1 - y), (1 - x, 1 - y)]
    return x, y, c, chips


def all_gather(name, shards):
    na = len(shards)

    def body(*refs):
        ins, outs = refs[:na], refs[na:2 * na]
        send_sems, recv_sems, local_sems = refs[2 * na:]
        x, y, c, chips = _place()
        me, sibling = (x, y, c), (x, y, 1 - c)

        def copy(a, k, block, to, src=None):
            dst = outs[a].at[4 * block[0] + 2 * block[1] + block[2]]
            return pltpu.make_async_remote_copy(
                src_ref=dst if src is None else src, dst_ref=dst,
                send_sem=send_sems.at[a, k], recv_sem=recv_sems.at[a, k],
                device_id=to, device_id_type=MESH)

        mine, first, passed = [], [], []
        for a in range(na):
            cp = pltpu.make_async_copy(ins[a], outs[a].at[4 * x + 2 * y + c], local_sems.at[a])
            cp.start()
            mine.append(cp)
        for a in range(na):
            cps = [copy(a, 0, me, sibling, src=ins[a])]
            cps += [copy(a, 1 + j, me, (*chip, c), src=ins[a]) for j, chip in enumerate(chips)]
            for cp in cps:
                cp.start()
            first += cps
        for a in range(na):
            for j, chip in enumerate(chips):
                copy(a, 1 + j, (*chip, c), me).wait_recv()
                cp = copy(a, 4 + j, (*chip, c), sibling)
                cp.start()
                passed.append(cp)
        for a in range(na):
            copy(a, 0, sibling, me).wait_recv()
            for j, chip in enumerate(chips):
                copy(a, 4 + j, (*chip, 1 - c), me).wait_recv()
        for cp in first + passed:
            cp.wait_send()
        for cp in mine:
            cp.wait()

    outs = pl.pallas_call(
        body, name=name,
        in_specs=[ANY] * na, out_specs=[ANY] * na,
        out_shape=[jax.ShapeDtypeStruct((N_DEV,) + s.shape, s.dtype) for s in shards],
        scratch_shapes=[pltpu.SemaphoreType.DMA((na, 7)), pltpu.SemaphoreType.DMA((na, 7)),
                        pltpu.SemaphoreType.DMA((na,))],
    )(*shards)
    return list(outs)


def exchange_siblings(name, grads):
    na = len(grads)

    def body(*refs):
        ins = refs[:na]
        keeps, recvs = refs[na:2 * na], refs[2 * na:3 * na]
        send_sems, recv_sems, local_sems = refs[3 * na:]
        x, y, c, _ = _place()
        sibling = (x, y, 1 - c)
        sends, locals_ = [], []
        for a in range(na):
            for s in range(4):
                cp = pltpu.make_async_copy(ins[a].at[2 * s + c], keeps[a].at[s], local_sems.at[a, s])
                cp.start()
                locals_.append(cp)
                rc = pltpu.make_async_remote_copy(
                    src_ref=ins[a].at[2 * s + (1 - c)], dst_ref=recvs[a].at[s],
                    send_sem=send_sems.at[a, s], recv_sem=recv_sems.at[a, s],
                    device_id=sibling, device_id_type=MESH)
                rc.start()
                sends.append(rc)
        for rc in sends:
            rc.wait_recv()
        for rc in sends:
            rc.wait_send()
        for cp in locals_:
            cp.wait()

    half = [jax.ShapeDtypeStruct((4,) + g.shape[1:], g.dtype) for g in grads]
    outs = pl.pallas_call(
        body, name=name,
        in_specs=[ANY] * na, out_specs=[ANY] * (2 * na), out_shape=half + half,
        scratch_shapes=[pltpu.SemaphoreType.DMA((na, 4)), pltpu.SemaphoreType.DMA((na, 4)),
                        pltpu.SemaphoreType.DMA((na, 4))],
    )(*grads)
    return list(outs[:na]), list(outs[na:])


def exchange_chips(name, parts):
    na = len(parts)

    def body(*refs):
        ins, outs = refs[:na], refs[na:2 * na]
        send_sems, recv_sems, local_sems = refs[2 * na:]
        x, y, c, chips = _place()
        sends, locals_ = [], []
        for a in range(na):
            cp = pltpu.make_async_copy(ins[a].at[2 * x + y], outs[a].at[0], local_sems.at[a])
            cp.start()
            locals_.append(cp)
            for j, (px, py) in enumerate(chips):
                rc = pltpu.make_async_remote_copy(
                    src_ref=ins[a].at[2 * px + py], dst_ref=outs[a].at[1 + j],
                    send_sem=send_sems.at[a, j], recv_sem=recv_sems.at[a, j],
                    device_id=(px, py, c), device_id_type=MESH)
                rc.start()
                sends.append(rc)
        for rc in sends:
            rc.wait_recv()
        for rc in sends:
            rc.wait_send()
        for cp in locals_:
            cp.wait()

    outs = pl.pallas_call(
        body, name=name,
        in_specs=[ANY] * na, out_specs=[ANY] * na,
        out_shape=[jax.ShapeDtypeStruct(p.shape, p.dtype) for p in parts],
        scratch_shapes=[pltpu.SemaphoreType.DMA((na, 3)), pltpu.SemaphoreType.DMA((na, 3)),
                        pltpu.SemaphoreType.DMA((na,))],
    )(*parts)
    return list(outs)


def pack(arrs):
    pieces = []
    for a in arrs:
        f = a.reshape(-1).astype(F32)
        pad = (-f.shape[0]) % LANE
        if pad:
            f = jnp.concatenate([f, jnp.zeros((pad,), F32)])
        pieces.append(f)
    flat = jnp.concatenate(pieces)
    rows = flat.shape[0] // LANE
    pad_rows = (-rows) % SUBLANE
    if pad_rows:
        flat = jnp.concatenate([flat, jnp.zeros((pad_rows * LANE,), F32)])
    return flat.reshape(-1, LANE)


def unpack(buf, shapes):
    flat = buf.reshape(-1)
    outs, off = [], 0
    for shp in shapes:
        n = int(np.prod(shp))
        outs.append(flat[off:off + n].reshape(shp))
        off += n + ((-n) % LANE)
    return outs


def _vjp_rows(fn, n_row_in, n_cot):
    def bwd(*args):
        rows = args[:n_row_in]
        cots = args[n_row_in:n_row_in + n_cot]
        consts = args[n_row_in + n_cot:]
        out, vjp = jax.vjp(fn, *rows, *consts)
        if isinstance(out, (tuple, list)):
            cot = tuple(c.astype(o.dtype) for c, o in zip(cots, out))
        else:
            cot = cots[0].astype(out.dtype)
        return vjp(cot)
    return bwd


def rms_fwd(name, x, g):
    return rowcall(name, rms_tile, [x], [g], [(x.shape[1], BF)])


def rms_bwd(name, x, g, dh, dres):
    D = x.shape[1]
    vj = _vjp_rows(rms_tile, 1, 1)

    def f(x_, dh_, dres_, g_):
        dx, dg = vj(x_, dh_, g_)
        return dx + dres_, dg
    return rowcall(name, f, [x, dh, dres], [g], [(D, F32)], [(1, D)])


def kernel(x, e_norm, e_w_in, e_conv_w, e_a_log, e_dt_bias, e_o_norm, e_ln_g, e_ln_b, e_w_s, e_b_s, e_w_out, o_norm, o_pw1, o_pw1_b, o_dw, o_dw_b, o_ln_g, o_ln_b, o_pw2, o_pw2_b, f_norm, f_w1, f_w2, final_norm, loss_target, m_e_norm, m_e_w_in, m_e_conv_w, m_e_a_log, m_e_dt_bias, m_e_o_norm, m_e_ln_g, m_e_ln_b, m_e_w_s, m_e_b_s, m_e_w_out, m_o_norm, m_o_pw1, m_o_pw1_b, m_o_dw, m_o_dw_b, m_o_ln_g, m_o_ln_b, m_o_pw2, m_o_pw2_b, m_f_norm, m_f_w1, m_f_w2, m_final_norm, v_e_norm, v_e_w_in, v_e_conv_w, v_e_a_log, v_e_dt_bias, v_e_o_norm, v_e_ln_g, v_e_ln_b, v_e_w_s, v_e_b_s, v_e_w_out, v_o_norm, v_o_pw1, v_o_pw1_b, v_o_dw, v_o_dw_b, v_o_ln_g, v_o_ln_b, v_o_pw2, v_o_pw2_b, v_f_norm, v_f_w1, v_f_w2, v_final_norm):
    names = ['e_norm', 'e_w_in', 'e_conv_w', 'e_a_log', 'e_dt_bias', 'e_o_norm', 'e_ln_g', 'e_ln_b', 'e_w_s', 'e_b_s', 'e_w_out', 'o_norm', 'o_pw1', 'o_pw1_b', 'o_dw', 'o_dw_b', 'o_ln_g', 'o_ln_b', 'o_pw2', 'o_pw2_b', 'f_norm', 'f_w1', 'f_w2', 'final_norm']
    W = dict(zip(names, [e_norm, e_w_in, e_conv_w, e_a_log, e_dt_bias, e_o_norm, e_ln_g, e_ln_b, e_w_s, e_b_s, e_w_out, o_norm, o_pw1, o_pw1_b, o_dw, o_dw_b, o_ln_g, o_ln_b, o_pw2, o_pw2_b, f_norm, f_w1, f_w2, final_norm]))
    Mo = dict(zip(names, [m_e_norm, m_e_w_in, m_e_conv_w, m_e_a_log, m_e_dt_bias, m_e_o_norm, m_e_ln_g, m_e_ln_b, m_e_w_s, m_e_b_s, m_e_w_out, m_o_norm, m_o_pw1, m_o_pw1_b, m_o_dw, m_o_dw_b, m_o_ln_g, m_o_ln_b, m_o_pw2, m_o_pw2_b, m_f_norm, m_f_w1, m_f_w2, m_final_norm]))
    Vo = dict(zip(names, [v_e_norm, v_e_w_in, v_e_conv_w, v_e_a_log, v_e_dt_bias, v_e_o_norm, v_e_ln_g, v_e_ln_b, v_e_w_s, v_e_b_s, v_e_w_out, v_o_norm, v_o_pw1, v_o_pw1_b, v_o_dw, v_o_dw_b, v_o_ln_g, v_o_ln_b, v_o_pw2, v_o_pw2_b, v_f_norm, v_f_w1, v_f_w2, v_final_norm]))

    T, D = x.shape[1], x.shape[2]
    H = e_a_log.shape[-1]
    dv = e_o_norm.shape[-1]
    dk = dv
    G = e_w_s.shape[1]
    AQK, AV, BW = H * dk, H * dv, e_ln_g.shape[-1]
    AQKV = 2 * AQK + AV
    in_cols = AQKV + AV + 2 * H + 2 * BW
    KA = e_conv_w.shape[1]
    KC = o_dw.shape[1]
    L = f_norm.shape[0]
    dev = 4 * lax.axis_index("x") + 2 * lax.axis_index("y") + lax.axis_index("c")
    x2d = x.reshape(T, D)
    tgt = loss_target.reshape(T, D)

    big = [e_w_in[0], e_w_out[0], o_pw1[0], o_pw2[0]] + [f_w1[l] for l in range(L)] + [f_w2[l] for l in range(L)]
    gathered = all_gather("gather_weights", [w.astype(BF) for w in big])
    g_win, g_wout, g_pw1, g_pw2 = gathered[:4]
    g_w1, g_w2 = gathered[4:4 + L], gathered[4 + L:4 + 2 * L]
    win = jnp.moveaxis(g_win, 0, 1).reshape(D, in_cols)
    w_qkv, w_z = win[:, :AQKV], win[:, AQKV:AQKV + AV]
    w_ba = jnp.pad(win[:, AQKV + AV:AQKV + AV + 2 * H], ((0, 0), (0, LANE - 2 * H)))
    w_uv = win[:, AQKV + AV + 2 * H:]
    wout = g_wout.reshape(-1, D)
    pw1 = jnp.moveaxis(g_pw1, 0, 1).reshape(D, 2 * D)
    pw2 = g_pw2.reshape(D, D)
    w1 = [jnp.moveaxis(g, 0, 1).reshape(D, -1) for g in g_w1]
    w2 = [g.reshape(-1, D) for g in g_w2]

    row = lambda a: a.reshape(1, -1).astype(F32)
    small_sharded = ['e_conv_w', 'o_norm', 'o_pw1_b', 'o_dw', 'o_dw_b', 'o_ln_g', 'o_ln_b', 'o_pw2_b']
    sm = all_gather("gather_small", [pack([W[n][0]]) for n in small_sharded])
    full = {}
    for n, g in zip(small_sharded, sm):
        shp = W[n][0].shape
        blocks = [unpack(g[d], [shp])[0] for d in range(N_DEV)]
        full[n] = jnp.concatenate(blocks, axis=-1)
    conv_w = full['e_conv_w']
    on_row, pw1_b_row = row(full['o_norm']), row(full['o_pw1_b'])
    dw_w, dw_b_row = full['o_dw'], row(full['o_dw_b'])
    oln_g_row, oln_b_row, pw2_b_row = row(full['o_ln_g']), row(full['o_ln_b']), row(full['o_pw2_b'])

    en_row = row(e_norm)
    alog_row = jnp.pad(row(e_a_log), ((0, 0), (H, LANE - 2 * H)))
    dtb_row = jnp.pad(row(e_dt_bias), ((0, 0), (H, LANE - 2 * H)))
    eon_row = row(e_o_norm)
    eln_g_row, eln_b_row = row(e_ln_g), row(e_ln_b)
    w_s = e_w_s[0]
    bs_t = e_b_s[0].T
    fn_rows = [row(f_norm[l]) for l in range(L)]
    fin_row = row(final_norm)

    h0 = rms_fwd("rms_e", x2d, en_row)
    qkv_raw = matmul("proj_qkv", h0, w_qkv, "nn")
    z_gate = matmul("proj_z", h0, w_z, "nn")
    ba = matmul("proj_ba", h0, w_ba, "nn")
    uv = matmul("proj_uv", h0, w_uv, "nn")

    cwa = min(512, AQKV)
    qkv_post = make_qkv_post(dk, cwa, 2 * AQK // cwa)
    ident = lambda t: t
    qkv = conv_fwd("qkv_conv", qkv_raw, conv_w, [], ident, qkv_post, [0], 1, KA, cw=cwa)
    qn, kn, vv = qkv[:, :AQK], qkv[:, AQK:2 * AQK], qkv[:, 2 * AQK:]
    bgc_fn = make_bgc(H)
    bgc = rowcall("bgc", bgc_fn, [ba], [alog_row, dtb_row], [(LANE, F32)])
    o_dn, xinv, states = deltanet_fwd(qn, kn, vv, bgc, H)
    out_a = rowcall("mixa_post", mixa_post_tile, [o_dn, z_gate], [eon_row], [(AV, BF)])
    out_b = rowcall("mixb", mixb_tile, [uv], [eln_g_row, eln_b_row, w_s, bs_t], [(BW, BF)])
    mix = jnp.concatenate([out_a, out_b], axis=-1)
    add_epi = lambda acc, r: (acc + r,)
    x1 = matmul("out_proj", mix, wout, "nn", epi=add_epi, extras=[x2d])

    def ffn_fwd(l, xin):
        hf = rms_fwd(f"rms_f{l}", xin, fn_rows[l])

        def epi(acc):
            r = jnp.maximum(acc, 0.0)
            return r * r, r
        a2, ar = matmul(f"ffn_up{l}", hf, w1[l], "nn", epi=epi, out_dtypes=(BF, BF))
        xo = matmul(f"ffn_down{l}", a2, w2[l], "nn", epi=add_epi, extras=[xin], tk=2048)
        return xo, (hf, a2, ar)

    x2, ffn0 = ffn_fwd(0, x1)

    h1 = rms_fwd("rms_o", x2, on_row)
    bias_epi = lambda acc, b: (acc + b,)
    zc = matmul("pw1", h1, pw1, "nn", epi=bias_epi, extras=[pw1_b_row])
    cwc = min(512, D)
    ncb = D // cwc
    cconv = conv_fwd("dw_conv", zc, dw_w, [dw_b_row], glu_pre, bias_post, [0, ncb], 1, KC, hb=32, cw=cwc)
    ln_silu = lambda c, g, b: jax.nn.silu(ln_tile(c, g, b))
    s_act = rowcall("ln_silu", ln_silu, [cconv], [oln_g_row, oln_b_row], [(D, BF)])
    x3 = matmul("pw2", s_act, pw2, "nn", epi=lambda acc, r, b: (r + (acc + b),), extras=[x2, pw2_b_row])
    x4, ffn1 = ffn_fwd(1, x3)

    def loss_bwd_tile(x_, t_, g_):
        l, vjp = jax.vjp(lambda a, b: loss_tile(a, b, t_), x_, g_)
        dx, dg = vjp(jnp.ones_like(l))
        return dx, l, dg
    dx4, loss_part, d_final = rowcall("loss_head", loss_bwd_tile, [x4, tgt], [fin_row],
                                      [(D, F32)], [(1, 1), (1, D)])
    loss = lax.psum(loss_part[0, 0], ("x", "y", "c"))

    def ffn_bwd(l, xin, saved, dxo):
        hf, a2, ar = saved
        dpre = matmul(f"ffn_down_dx{l}", dxo, w2[l], "nt", epi=lambda acc, r: (acc * (2.0 * r.astype(F32)),),
                      extras=[ar], out_dtypes=(BF,))
        dw2 = matmul(f"ffn_down_dw{l}", a2, dxo, "tn")
        dw1 = matmul(f"ffn_up_dw{l}", hf, dpre, "tn", colshard=True)
        dhf = matmul(f"ffn_up_dx{l}", dpre, w1[l], "nt")
        dxin, dg = rms_bwd(f"rms_f_bwd{l}", xin, fn_rows[l], dhf, dxo)
        return dxin, dw1, dw2, dg

    d_fnorm = [None] * L
    d_w1, d_w2 = [None] * L, [None] * L
    dx3, d_w1[1], d_w2[1], d_fnorm[1] = ffn_bwd(1, x3, ffn1, dx4)

    ds_act = matmul("pw2_dx", dx3, pw2, "nt")
    d_pw2 = matmul("pw2_dw", s_act, dx3, "tn")
    ln_silu_bwd = _vjp_rows(ln_silu, 1, 1)

    def ln_silu_bwd_tile(c_, ds_, dx3_, g_, b_):
        dc, dg, db = ln_silu_bwd(c_, ds_, g_, b_)
        return dc, dg, db, jnp.sum(dx3_, axis=0, keepdims=True)
    dcconv, d_oln_g, d_oln_b, d_pw2_b = rowcall(
        "ln_silu_bwd", ln_silu_bwd_tile, [cconv, ds_act, dx3], [oln_g_row, oln_b_row],
        [(D, F32)], [(1, D), (1, D), (1, D)])
    (dza, dzb), (sza, szb), d_dw, (d_dw_b,) = conv_bwd(
        "dw_conv_bwd", zc, dw_w, [dw_b_row], [dcconv], glu_pre, bias_post, [0, ncb], KC, hb=32, cw=cwc, tr=128)
    dzc = jnp.concatenate([dza, dzb], axis=-1)
    d_pw1_b = jnp.concatenate([sza, szb], axis=-1)
    d_pw1 = matmul("pw1_dw", h1, dzc, "tn", colshard=True)
    dh1 = matmul("pw1_dx", dzc, pw1, "nt")
    dx2, d_onorm = rms_bwd("rms_o_bwd", x2, on_row, dh1, dx3)

    dx1, d_w1[0], d_w2[0], d_fnorm[0] = ffn_bwd(0, x1, ffn0, dx2)

    dmix = matmul("out_proj_dx", dx1, wout, "nt")
    d_wout = matmul("out_proj_dw", mix, dx1, "tn")
    dmix_a, dmix_b = dmix[:, :AV], dmix[:, AV:]
    mixb_bwd = _vjp_rows(mixb_tile, 1, 1)
    duv, d_eln_g, d_eln_b, d_ws, d_bs_t = rowcall(
        "mixb_bwd", mixb_bwd, [uv, dmix_b], [eln_g_row, eln_b_row, w_s, bs_t],
        [(2 * BW, BF)], [(1, BW), (1, BW), w_s.shape, bs_t.shape])
    mixa_bwd = _vjp_rows(mixa_post_tile, 2, 1)
    do_dn, dz_gate, d_eon = rowcall(
        "mixa_post_bwd", mixa_bwd, [o_dn, z_gate, dmix_a], [eon_row],
        [(AV, F32), (AV, BF)], [(1, dv)])
    dq, dk_, dv_, dbgc = deltanet_bwd(qn, kn, vv, bgc, xinv, states, do_dn, H)
    bgc_bwd = _vjp_rows(bgc_fn, 1, 1)
    dba, d_alog_row, d_dtb_row = rowcall(
        "bgc_bwd", bgc_bwd, [ba, dbgc], [alog_row, dtb_row], [(LANE, BF)], [(1, LANE), (1, LANE)])
    dqkv = jnp.concatenate([dq, dk_, dv_], axis=-1)
    (dqkv_raw,), _, d_conv_w, _ = conv_bwd(
        "qkv_conv_bwd", qkv_raw, conv_w, [], [dqkv], ident, qkv_post, [0], KA, cw=cwa)

    dw_qkv = matmul("proj_qkv_dw", h0, dqkv_raw, "tn")
    dw_z = matmul("proj_z_dw", h0, dz_gate, "tn")
    dw_ba = matmul("proj_ba_dw", h0, dba, "tn")
    dw_uv = matmul("proj_uv_dw", h0, duv, "tn")
    dh0 = matmul("proj_qkv_dx", dqkv_raw, w_qkv, "nt")
    dh0 = matmul("proj_z_dx", dz_gate, w_z, "nt", epi=add_epi, extras=[dh0])
    dh0 = matmul("proj_ba_dx", dba, w_ba, "nt", epi=add_epi, extras=[dh0])
    dh0 = matmul("proj_uv_dx", duv, w_uv, "nt", epi=add_epi, extras=[dh0])
    grad_x, d_enorm = rms_bwd("rms_e_bwd", x2d, en_row, dh0, dx1)

    d_win = jnp.concatenate([dw_qkv, dw_z, dw_ba[:, :2 * H], dw_uv], axis=-1)
    G_win = jnp.moveaxis(d_win.reshape(D, N_DEV, in_cols // N_DEV), 1, 0)
    G_big = [G_win, d_wout.reshape(N_DEV, -1, D), d_pw1, d_pw2.reshape(N_DEV, -1, D)]
    G_big += d_w1 + [g.reshape(N_DEV, -1, D) for g in d_w2]
    keeps, recvs = exchange_siblings("reduce_d2d", G_big)

    def add_pair(a, b):
        return (a + b,)
    parts = []
    for i, (kp, rc) in enumerate(zip(keeps, recvs)):
        r, c = kp.shape[1], kp.shape[2]
        parts.append(rowcall(f"chip_sum{i}", add_pair, [kp.reshape(4 * r, c), rc.reshape(4 * r, c)], [],
                             [(c, BF)], tr=512).reshape(4, r, c))
    final = exchange_chips("reduce_ici", parts)

    def adam_big(i, fin, w, m, v, lead):
        def f(p0, p1, p2, p3, w_, m_, v_):
            g = ((p0.astype(F32) + p1.astype(F32)) + p2.astype(F32)) + p3.astype(F32)
            return (g,) + adamw_tile(w_, g, m_, v_)
        C = fin.shape[-1]
        return rowcall(f"adam{i}", f, [(fin, 0), (fin, 1), (fin, 2), (fin, 3), (w, lead), (m, lead), (v, lead)], [],
                       [(C, F32)] * 4, tr=256)

    res = {}
    order = [('e_w_in', 0), ('e_w_out', 0), ('o_pw1', 0), ('o_pw2', 0)] + [('f_w1', l) for l in range(L)] + [('f_w2', l) for l in range(L)]
    per_layer = {}
    for i, ((n, l), fin) in enumerate(zip(order, final)):
        out = adam_big(i, fin, W[n], Mo[n], Vo[n], l)
        per_layer.setdefault(n, []).append(out)
    for n, outs in per_layer.items():
        res[n] = tuple(jnp.stack([o[k] for o in outs]) for k in range(4))

    d_alog = d_alog_row[:, H:2 * H]
    d_dtb = d_dtb_row[:, H:2 * H]
    small_names = ['e_norm', 'e_conv_w', 'e_a_log', 'e_dt_bias', 'e_o_norm', 'e_ln_g', 'e_ln_b', 'e_w_s', 'e_b_s',
                   'o_norm', 'o_pw1_b', 'o_dw', 'o_dw_b', 'o_ln_g', 'o_ln_b', 'o_pw2_b', 'f_norm', 'final_norm']
    small_grads = [d_enorm, d_conv_w, d_alog, d_dtb, d_eon, d_eln_g, d_eln_b, d_ws, d_bs_t.T,
                   d_onorm, d_pw1_b, d_dw, d_dw_b, d_oln_g, d_oln_b, d_pw2_b,
                   jnp.concatenate(d_fnorm, axis=0), d_final]
    full_shapes = [g.shape for g in small_grads]
    gs_all = all_gather("gather_small_grads", [pack(small_grads)])[0]

    def sum8(*ps):
        s = ps[0]
        for p in ps[1:]:
            s = s + p
        return (s,)
    gs_sum = rowcall("small_sum", sum8, [(gs_all, d) for d in range(N_DEV)], [], [(LANE, F32)])
    g_full = dict(zip(small_names, unpack(gs_sum, full_shapes)))
    g_loc = {}
    for n in small_names:
        g = g_full[n]
        if n in small_sharded:
            per = g.shape[-1] // N_DEV
            g = lax.dynamic_slice_in_dim(g, dev * per, per, axis=-1)
        g_loc[n] = g.reshape(W[n].shape)
    packs = [pack([A[n] for n in small_names]) for A in (W, g_loc, Mo, Vo)]
    d_s, m_s, v_s = rowcall("adam_small", adamw_tile, packs, [], [(LANE, F32)] * 3)
    shapes = [W[n].shape for n in small_names]
    for n, d_, m_, v_ in zip(small_names, unpack(d_s, shapes), unpack(m_s, shapes), unpack(v_s, shapes)):
        res[n] = (g_loc[n], d_, m_, v_)

    grads = [res[n][0] for n in names]
    deltas = [res[n][1] for n in names]
    new_m = [res[n][2] for n in names]
    new_v = [res[n][3] for n in names]
    return (loss, grad_x.reshape(x.shape), *grads, *deltas, *new_m, *new_v)
```

```python
import functools
import math

import jax
import jax.numpy as jnp
import numpy as np
from jax import lax
from jax.experimental import pallas as pl
from jax.experimental.pallas import tpu as pltpu

F32 = jnp.float32
BF = jnp.bfloat16
EPS = 1e-6
CHUNK = 64
B_BLOCK = 128
LANE = 128
SUBLANE = 8
N_DEV = 8
VMEM_LIMIT = 56 * 1024 * 1024

ADAM_LR = 0.001
ADAM_B1 = 0.9
ADAM_B2 = 0.999
ADAM_EPS = 1e-08
ADAM_WD = 0.01
ADAM_STEP = 10

MESH = pl.DeviceIdType.MESH
ANY = pl.BlockSpec(memory_space=pl.ANY)


def _tile(n, pref, mult=SUBLANE):
    if n <= pref:
        return n
    t = (pref // mult) * mult
    while t >= mult:
        if n % t == 0:
            return t
        t -= mult
    return n


def _dg(a, b, ca, cb, hi):
    dims = (((ca,), (cb,)), ((), ()))
    if hi:
        return lax.dot_general(a.astype(F32), b.astype(F32), dims,
                               precision=lax.Precision.HIGHEST, preferred_element_type=F32)
    return lax.dot_general(a.astype(BF), b.astype(BF), dims, preferred_element_type=F32)


@functools.partial(jax.custom_vjp, nondiff_argnums=(2, 3, 4))
def mm(a, b, ca, cb, hi=False):
    return _dg(a, b, ca, cb, hi)


def _mm_fwd(a, b, ca, cb, hi):
    return _dg(a, b, ca, cb, hi), (a, b)


def _mm_bwd(ca, cb, hi, res, g):
    a, b = res
    if ca == 1:
        da = mm(g, b, 1, 1 - cb, hi)
    else:
        da = mm(b, g, 1 - cb, 1, hi)
    if cb == 0:
        db = mm(a, g, 1 - ca, 0, hi)
    else:
        db = mm(g, a, 0, 1 - ca, hi)
    return da.astype(a.dtype), db.astype(b.dtype)


mm.defvjp(_mm_fwd, _mm_bwd)


def matmul(name, a, b, mode, epi=None, extras=(), out_dtypes=(F32,), colshard=False,
           tm=512, tn=1024, tk=2048):
    if mode == "nn":
        (M, K), (K2, N) = a.shape, b.shape
    elif mode == "nt":
        (M, K), (N, K2) = a.shape, b.shape
    else:
        (K, M), (K2, N) = a.shape, b.shape
    assert K == K2, (name, a.shape, b.shape, mode)
    tm = _tile(M, tm)
    tn = N // N_DEV if colshard else _tile(N, tn, LANE)
    tk = _tile(K, tk, LANE)
    nk = K // tk
    grid = (M // tm, N // tn, nk)
    if mode == "nn":
        a_spec = pl.BlockSpec((tm, tk), lambda i, j, k: (i, k))
        b_spec = pl.BlockSpec((tk, tn), lambda i, j, k: (k, j))
        ca, cb = 1, 0
    elif mode == "nt":
        a_spec = pl.BlockSpec((tm, tk), lambda i, j, k: (i, k))
        b_spec = pl.BlockSpec((tn, tk), lambda i, j, k: (j, k))
        ca, cb = 1, 1
    else:
        a_spec = pl.BlockSpec((tk, tm), lambda i, j, k: (k, i))
        b_spec = pl.BlockSpec((tk, tn), lambda i, j, k: (k, j))
        ca, cb = 0, 0
    ex_specs = []
    for e in extras:
        if e.shape[0] == 1:
            ex_specs.append(pl.BlockSpec((1, tn), lambda i, j, k: (0, j)))
        else:
            assert e.shape == (M, N), (name, e.shape)
            ex_specs.append(pl.BlockSpec((tm, tn), lambda i, j, k: (i, j)))
    if colshard:
        out_shape = [jax.ShapeDtypeStruct((N_DEV, M, tn), dt) for dt in out_dtypes]
        out_specs = [pl.BlockSpec((None, tm, tn), lambda i, j, k: (j, i, 0)) for _ in out_dtypes]
    else:
        out_shape = [jax.ShapeDtypeStruct((M, N), dt) for dt in out_dtypes]
        out_specs = [pl.BlockSpec((tm, tn), lambda i, j, k: (i, j)) for _ in out_dtypes]
    n_ex, n_out = len(extras), len(out_dtypes)

    def body(*refs):
        a_ref, b_ref = refs[0], refs[1]
        ex_refs = refs[2:2 + n_ex]
        o_refs = refs[2 + n_ex:2 + n_ex + n_out]
        acc_ref = refs[-1]
        k = pl.program_id(2)
        part = _dg(a_ref[...], b_ref[...], ca, cb, False)

        @pl.when(k == 0)
        def _():
            acc_ref[...] = part

        @pl.when(k > 0)
        def _():
            acc_ref[...] += part

        @pl.when(k == nk - 1)
        def _():
            acc = acc_ref[...]
            res = (acc,) if epi is None else epi(acc, *[r[...] for r in ex_refs])
            for o_ref, r in zip(o_refs, res):
                o_ref[...] = r.astype(o_ref.dtype)

    outs = pl.pallas_call(
        body, name=name, grid=grid,
        in_specs=[a_spec, b_spec] + ex_specs, out_specs=out_specs, out_shape=out_shape,
        scratch_shapes=[pltpu.VMEM((tm, tn), F32)],
        compiler_params=pltpu.CompilerParams(
            dimension_semantics=("parallel", "parallel", "arbitrary"),
            vmem_limit_bytes=VMEM_LIMIT),
    )(a, b, *extras)
    return outs[0] if n_out == 1 else tuple(outs)


def rowcall(name, fn, rows, consts, out_rows, out_accs=(), tr=256, sp=None, R=None):
    rows = [r if isinstance(r, tuple) else (r, None) for r in rows]
    R = rows[0][0].shape[-2] if R is None else R
    tr = _tile(R, tr)
    n = R // tr
    in_specs = []
    for arr, lead in rows:
        C = arr.shape[-1]
        if lead is None:
            assert arr.shape[-2] == R, (name, arr.shape, R)
            in_specs.append(pl.BlockSpec((tr, C), lambda i, *s: (i, 0)))
        elif callable(lead):
            in_specs.append(pl.BlockSpec((tr, C), lambda i, *s, lead=lead: (lead(i, n, *s), 0)))
        else:
            assert arr.shape[-2] == R, (name, arr.shape, R)
            in_specs.append(pl.BlockSpec((None, tr, C), lambda i, *s, lead=lead: (lead, i, 0)))
    for c in consts:
        in_specs.append(pl.BlockSpec(c.shape, lambda i, *s, nd=c.ndim: (0,) * nd))
    out_shape = [jax.ShapeDtypeStruct((R, C), dt) for C, dt in out_rows]
    out_specs = [pl.BlockSpec((tr, C), lambda i, *s: (i, 0)) for C, _ in out_rows]
    for shp in out_accs:
        out_shape.append(jax.ShapeDtypeStruct(shp, F32))
        out_specs.append(pl.BlockSpec(shp, lambda i, *s, nd=len(shp): (0,) * nd))
    n_in, n_row, n_acc = len(rows) + len(consts), len(out_rows), len(out_accs)
    n_sp = 0 if sp is None else 1

    def body(*refs):
        refs = refs[n_sp:]
        ins = [r[...] for r in refs[:n_in]]
        res = fn(*ins)
        if not isinstance(res, (tuple, list)):
            res = (res,)
        o_refs = refs[n_in:]
        for o_ref, r in zip(o_refs[:n_row], res[:n_row]):
            o_ref[...] = r.astype(o_ref.dtype)
        if n_acc:
            first = pl.program_id(0) == 0
            for o_ref, r in zip(o_refs[n_row:], res[n_row:]):
                r = r.astype(F32).reshape(o_ref.shape)

                @pl.when(first)
                def _(o_ref=o_ref, r=r):
                    o_ref[...] = r

                @pl.when(jnp.logical_not(first))
                def _(o_ref=o_ref, r=r):
                    o_ref[...] += r

    params = pltpu.CompilerParams(dimension_semantics=("arbitrary",), vmem_limit_bytes=VMEM_LIMIT)
    operands = [a for a, _ in rows] + list(consts)
    if sp is None:
        outs = pl.pallas_call(body, name=name, grid=(n,), in_specs=in_specs, out_specs=out_specs,
                              out_shape=out_shape, compiler_params=params)(*operands)
    else:
        outs = pl.pallas_call(
            body, name=name, out_shape=out_shape, compiler_params=params,
            grid_spec=pltpu.PrefetchScalarGridSpec(
                num_scalar_prefetch=1, grid=(n,), in_specs=in_specs, out_specs=out_specs),
        )(sp, *operands)
    return outs[0] if len(outs) == 1 else tuple(outs)


def rms_tile(x, g):
    x = x.astype(F32)
    return x * lax.rsqrt(jnp.mean(x * x, axis=-1, keepdims=True) + EPS) * g


def gelu(x):
    return 0.5 * x * (1.0 + lax.erf(x * (1.0 / math.sqrt(2.0))))


def ln_tile(x, g, b):
    mu = jnp.mean(x, axis=-1, keepdims=True)
    xc = x - mu
    return xc * lax.rsqrt(jnp.mean(xc * xc, axis=-1, keepdims=True) + EPS) * g + b


def lane_groups(fn, width, *arrs):
    n = arrs[0].shape[-1] // width
    outs = [fn(*[a[:, i * width:(i + 1) * width] for a in arrs]) for i in range(n)]
    return jnp.concatenate(outs, axis=-1)


def mixa_post_tile(o, z, o_norm):
    dv = o_norm.shape[-1]
    on = lane_groups(lambda t: rms_tile(t, o_norm), dv, o)
    return on * jax.nn.silu(z)


def mixb_tile(uv, ln_g, ln_b, w_s, bs_t):
    G = w_s.shape[0]
    gw = ln_g.shape[-1]
    dg = gw // G
    tr = uv.shape[0]
    u = gelu(uv[:, :gw])
    vg = gelu(uv[:, gw:])
    ii = lax.broadcasted_iota(jnp.int32, (B_BLOCK, B_BLOCK), 0)
    jj = lax.broadcasted_iota(jnp.int32, (B_BLOCK, B_BLOCK), 1)
    mask = (jj // CHUNK) <= (ii // CHUNK)
    cols = []
    for g in range(G):
        sl = slice(g * dg, (g + 1) * dg)
        vn = ln_tile(vg[:, sl], ln_g[:, sl], ln_b[:, sl])
        wm = jnp.where(mask, w_s[g], 0.0)
        blocks = []
        for m in range(tr // B_BLOCK):
            blk = vn[m * B_BLOCK:(m + 1) * B_BLOCK, :]
            blocks.append(mm(wm, blk, 1, 0) + bs_t[:, g:g + 1])
        mixed = blocks[0] if len(blocks) == 1 else jnp.concatenate(blocks, axis=0)
        cols.append(u[:, sl] * mixed)
    return jnp.concatenate(cols, axis=-1)


def bgc_tile(ba, alog_row, dtb_row):
    tr = ba.shape[0]
    beta = jax.nn.sigmoid(ba)
    g = -jnp.exp(alog_row) * jax.nn.softplus(ba + dtb_row)
    ii = lax.broadcasted_iota(jnp.int32, (tr, tr), 0)
    jj = lax.broadcasted_iota(jnp.int32, (tr, tr), 1)
    tri = jnp.where((ii // CHUNK == jj // CHUNK) & (jj <= ii), 1.0, 0.0).astype(F32)
    gc = mm(tri, g, 1, 0, True)
    return beta, gc


def make_bgc(H):
    def f(ba, alog_row, dtb_row):
        beta, gc = bgc_tile(ba, alog_row, dtb_row)
        lane = lax.broadcasted_iota(jnp.int32, ba.shape, 1)
        return jnp.where(lane < H, beta, jnp.where(lane < 2 * H, gc, 0.0))
    return f


def loss_tile(x, g, target):
    y = rms_tile(x, g)
    err = y - target
    return 0.5 * jnp.sum(jnp.mean(err * err, axis=-1, keepdims=True), axis=0, keepdims=True)


def adamw_tile(w, g, m, v):
    m = ADAM_B1 * m + (1.0 - ADAM_B1) * g
    v = ADAM_B2 * v + (1.0 - ADAM_B2) * (g * g)
    m_hat = m / (1.0 - ADAM_B1 ** ADAM_STEP)
    v_hat = v / (1.0 - ADAM_B2 ** ADAM_STEP)
    delta = -ADAM_LR * (m_hat / (jnp.sqrt(v_hat) + ADAM_EPS) + ADAM_WD * w)
    return delta, m, v


def _conv_specs(T, tr, hb, cw, col_blocks, rev):
    n = T // tr

    def ri(i):
        return (n - 1 - i) if rev else i

    tile_specs, halo_specs = [], []
    for off in col_blocks:
        tile_specs.append(pl.BlockSpec((tr, cw), lambda j, i, off=off: (ri(i), j + off)))
        halo_specs.append(pl.BlockSpec(
            (hb, cw), lambda j, i, off=off: (jnp.maximum(ri(i) * (tr // hb) - 1, 0), j + off)))
    return n, ri, tile_specs, halo_specs


def conv_fwd(name, x, w, consts, pre, post, col_blocks, n_out, K, out_dtype=F32, tr=256, hb=8, cw=512):
    T = x.shape[0]
    C = w.shape[1]
    tr, cw = _tile(T, tr, hb), min(cw, C)
    nb = len(col_blocks)
    n, ri, tile_specs, halo_specs = _conv_specs(T, tr, hb, cw, col_blocks, False)
    w_spec = pl.BlockSpec((K, cw), lambda j, i: (0, j))
    c_specs = [pl.BlockSpec((1, cw), lambda j, i: (0, j)) for _ in consts]

    def body(*refs):
        tiles = [r[...] for r in refs[:nb]]
        halos = [r[...] for r in refs[nb:2 * nb]]
        w_ref = refs[2 * nb]
        cs = [r[...] for r in refs[2 * nb + 1:2 * nb + 1 + len(consts)]]
        o_refs = refs[2 * nb + 1 + len(consts):-1]
        pbuf = refs[-1]
        i = pl.program_id(1)
        pbuf[0:hb, :] = jnp.where(i > 0, pre(*halos), 0.0)
        pbuf[hb:hb + tr, :] = pre(*tiles)
        c = jnp.zeros((tr, cw), F32)
        for k in range(K):
            c = c + w_ref[k:k + 1, :] * pbuf[pl.ds(hb - (K - 1) + k, tr), :]
        res = post(c, pl.program_id(0), *cs)
        for o_ref, r in zip(o_refs, res):
            o_ref[...] = r.astype(o_ref.dtype)

    outs = pl.pallas_call(
        body, name=name, grid=(C // cw, n),
        in_specs=tile_specs + halo_specs + [w_spec] + c_specs,
        out_specs=[pl.BlockSpec((tr, cw), lambda j, i: (i, j)) for _ in range(n_out)],
        out_shape=[jax.ShapeDtypeStruct((T, C), out_dtype) for _ in range(n_out)],
        scratch_shapes=[pltpu.VMEM((hb + tr, cw), F32)],
        compiler_params=pltpu.CompilerParams(
            dimension_semantics=("parallel", "arbitrary"), vmem_limit_bytes=VMEM_LIMIT),
    )(*([x] * nb), *([x] * nb), w, *consts)
    return outs[0] if n_out == 1 else tuple(outs)


def conv_bwd(name, x, w, consts, grads, pre, post, col_blocks, K, tr=256, hb=8, cw=512):
    T, Cx = x.shape
    C = w.shape[1]
    tr, cw = _tile(T, tr, hb), min(cw, C)
    nb = len(col_blocks)
    n, ri, tile_specs, halo_specs = _conv_specs(T, tr, hb, cw, col_blocks, True)
    w_spec = pl.BlockSpec((K, cw), lambda j, i: (0, j))
    c_specs = [pl.BlockSpec((1, cw), lambda j, i: (0, j)) for _ in consts]
    g_specs = [pl.BlockSpec((tr, cw), lambda j, i: (ri(i), j)) for _ in grads]
    nc, ng = len(consts), len(grads)

    def body(*refs):
        p = 0
        tile_refs = refs[p:p + nb]; p += nb
        halo_refs = refs[p:p + nb]; p += nb
        w_ref = refs[p]; p += 1
        cs = [r[...] for r in refs[p:p + nc]]; p += nc
        gs = [r[...] for r in refs[p:p + ng]]; p += ng
        dx_refs = refs[p:p + nb]; p += nb
        sum_refs = refs[p:p + nb]; p += nb
        dw_ref = refs[p]; p += 1
        dc_refs = refs[p:p + nc]; p += nc
        pbuf, dbuf, ebuf, carry = refs[p:p + 4]
        i = pl.program_id(1)
        first = i == 0
        tiles = [r[...] for r in tile_refs]
        halos = [r[...] for r in halo_refs]
        p_tile, vjp_pre = jax.vjp(pre, *tiles)
        pbuf[0:hb, :] = jnp.where(ri(i) > 0, pre(*halos), 0.0)
        pbuf[hb:hb + tr, :] = p_tile
        c = jnp.zeros((tr, cw), F32)
        for k in range(K):
            c = c + w_ref[k:k + 1, :] * pbuf[pl.ds(hb - (K - 1) + k, tr), :]
        cid = pl.program_id(0)
        _, vjp_post = jax.vjp(lambda c_, *cs_: post(c_, cid, *cs_), c, *cs)
        dres = vjp_post(tuple(g.astype(F32) for g in gs))
        dc = dres[0]
        dbuf[0:hb, :] = jnp.zeros((hb, cw), F32)
        dbuf[hb:hb + tr, :] = dc
        dbuf[hb + tr:hb + tr + hb, :] = jnp.zeros((hb, cw), F32)
        dp = jnp.zeros((hb + tr, cw), F32)
        dws = []
        for k in range(K):
            dp = dp + w_ref[k:k + 1, :] * dbuf[pl.ds(K - 1 - k, hb + tr), :]
            dws.append(jnp.sum(dc * pbuf[pl.ds(hb - (K - 1) + k, tr), :], axis=0, keepdims=True))
        dw = jnp.concatenate(dws, axis=0)
        ebuf[...] = dp

        @pl.when(jnp.logical_not(first))
        def _():
            ebuf[tr:tr + hb, :] += carry[...]

        carry[...] = ebuf[0:hb, :]
        dtiles = vjp_pre(ebuf[hb:hb + tr, :])
        for r, s, d in zip(dx_refs, sum_refs, dtiles):
            r[...] = d.astype(r.dtype)
            ds_ = jnp.sum(d, axis=0, keepdims=True)

            @pl.when(first)
            def _(s=s, ds_=ds_):
                s[...] = ds_

            @pl.when(jnp.logical_not(first))
            def _(s=s, ds_=ds_):
                s[...] += ds_

        accs = [(dw_ref, dw)] + [(r, d) for r, d in zip(dc_refs, dres[1:])]
        for r, d in accs:
            @pl.when(first)
            def _(r=r, d=d):
                r[...] = d

            @pl.when(jnp.logical_not(first))
            def _(r=r, d=d):
                r[...] += d

    n_cb = C // cw
    outs = pl.pallas_call(
        body, name=name, grid=(n_cb, n),
        in_specs=tile_specs + halo_specs + [w_spec] + c_specs + g_specs,
        out_specs=([pl.BlockSpec((tr, cw), lambda j, i: (ri(i), j)) for _ in col_blocks]
                   + [pl.BlockSpec((1, cw), lambda j, i: (0, j)) for _ in col_blocks]
                   + [pl.BlockSpec((K, cw), lambda j, i: (0, j))]
                   + [pl.BlockSpec((1, cw), lambda j, i: (0, j)) for _ in consts]),
        out_shape=([jax.ShapeDtypeStruct((T, C), BF) for _ in col_blocks]
                   + [jax.ShapeDtypeStruct((1, C), F32) for _ in col_blocks]
                   + [jax.ShapeDtypeStruct((K, C), F32)]
                   + [jax.ShapeDtypeStruct((1, C), F32) for _ in consts]),
        scratch_shapes=[pltpu.VMEM((hb + tr, cw), F32), pltpu.VMEM((hb + tr + hb, cw), F32),
                        pltpu.VMEM((hb + tr, cw), F32), pltpu.VMEM((hb, cw), F32)],
        compiler_params=pltpu.CompilerParams(
            dimension_semantics=("parallel", "arbitrary"), vmem_limit_bytes=VMEM_LIMIT),
    )(*([x] * nb), *([x] * nb), w, *consts, *grads)
    dxs = outs[:nb]
    sums = outs[nb:2 * nb]
    dw = outs[2 * nb]
    dcs = outs[2 * nb + 1:]
    return dxs, sums, dw, dcs


def make_qkv_post(dk, cw, n_qk_chunks):
    def l2(t):
        return t * lax.rsqrt(jnp.sum(t * t, axis=-1, keepdims=True) + EPS)

    def post(c, cid):
        s = jax.nn.silu(c)
        normed = lane_groups(l2, dk, s)
        return (jnp.where(cid < n_qk_chunks, normed, s),)
    return post


def glu_pre(za, zb):
    return za * jax.nn.sigmoid(zb)


def bias_post(c, cid, b):
    return (c + b,)


def _col_to_row(col):
    C = col.shape[0]
    ii = lax.broadcasted_iota(jnp.int32, (C, C), 0)
    jj = lax.broadcasted_iota(jnp.int32, (C, C), 1)
    return jnp.sum(jnp.where(ii == jj, jnp.broadcast_to(col, (C, C)), 0.0), axis=0, keepdims=True)


@jax.custom_vjp
def solve_with_inverse(a, rhs, x):
    return mm(x, rhs, 1, 0, True)


def _swi_fwd(a, rhs, x):
    sol = mm(x, rhs, 1, 0, True)
    return sol, (x, sol)


def _swi_bwd(res, dsol):
    x, sol = res
    drhs = mm(x, dsol, 0, 0, True)
    da = -mm(drhs, sol, 1, 1, True)
    return da, drhs, jnp.zeros_like(x)


solve_with_inverse.defvjp(_swi_fwd, _swi_bwd)


def unit_lower_inverse(a):
    C = a.shape[0]
    ii = lax.broadcasted_iota(jnp.int32, (C, C), 0)
    jj = lax.broadcasted_iota(jnp.int32, (C, C), 1)
    x = jnp.where(ii == jj, 1.0, 0.0).astype(F32) - a
    p = mm(a, a, 1, 0, True)
    steps = int(math.log2(C)) - 1
    for s in range(steps):
        x = x + mm(x, p, 1, 0, True)
        if s < steps - 1:
            p = mm(p, p, 1, 0, True)
    return x


def dn_masks(C):
    ii = lax.broadcasted_iota(jnp.int32, (C, C), 0)
    jj = lax.broadcasted_iota(jnp.int32, (C, C), 1)
    return ii >= jj, ii > jj


def dn_pre(q, k, v, beta, gc):
    C, dk = q.shape
    tri, strict = dn_masks(C)
    q = q * (dk ** -0.5)
    diff = gc - _col_to_row(gc)
    decay = jnp.where(tri, jnp.exp(jnp.where(tri, diff, 0.0)), 0.0)
    kb = k * beta
    vb = v * beta
    a = jnp.where(strict, mm(kb, k, 1, 1) * decay, 0.0)
    eg = jnp.exp(gc)
    rhs = jnp.concatenate([vb, kb * eg], axis=-1)
    attn = mm(q, k, 1, 1) * decay
    qd = q * eg
    g_last = gc[C - 1:C, :]
    kt = k * jnp.exp(g_last - gc)
    gl = jnp.exp(g_last)
    return a, rhs, attn, qd, kt, gl


def dn_chunk(q, k, v, beta, gc, state, x):
    dv = v.shape[-1]
    a, rhs, attn, qd, kt, gl = dn_pre(q, k, v, beta, gc)
    sol = solve_with_inverse(a, rhs, x)
    u, w = sol[:, :dv], sol[:, dv:]
    vn = u - mm(w, state, 1, 0)
    o = mm(qd, state, 1, 0) + mm(attn, vn, 1, 0)
    new_state = state * gl + mm(kt, vn, 0, 0)
    return o, new_state


def deltanet_fwd(q, k, v, bgc, H):
    T = q.shape[0]
    dk, dv = q.shape[1] // H, v.shape[1] // H
    N = T // CHUNK

    def body(q_ref, k_ref, v_ref, bgc_ref, o_ref, x_ref, s_ref, state):
        @pl.when(pl.program_id(0) == 0)
        def _():
            state[...] = jnp.zeros((H, dk, dv), F32)

        bg = bgc_ref[...]
        for h in range(H):
            beta, gc = bg[:, h:h + 1], bg[:, H + h:H + h + 1]
            qv = q_ref[:, h * dk:(h + 1) * dk]
            kv = k_ref[:, h * dk:(h + 1) * dk]
            vv = v_ref[:, h * dv:(h + 1) * dv]
            a = dn_pre(qv, kv, vv, beta, gc)[0]
            x = unit_lower_inverse(a)
            s = state[h]
            o, s_new = dn_chunk(qv, kv, vv, beta, gc, s, x)
            o_ref[:, h * dv:(h + 1) * dv] = o
            x_ref[h] = x
            s_ref[h] = s
            state[h] = s_new

    return pl.pallas_call(
        body, name="deltanet_fwd", grid=(N,),
        in_specs=[pl.BlockSpec((CHUNK, H * dk), lambda n: (n, 0)),
                  pl.BlockSpec((CHUNK, H * dk), lambda n: (n, 0)),
                  pl.BlockSpec((CHUNK, H * dv), lambda n: (n, 0)),
                  pl.BlockSpec((CHUNK, LANE), lambda n: (n, 0))],
        out_specs=[pl.BlockSpec((CHUNK, H * dv), lambda n: (n, 0)),
                   pl.BlockSpec((None, H, CHUNK, CHUNK), lambda n: (n, 0, 0, 0)),
                   pl.BlockSpec((None, H, dk, dv), lambda n: (n, 0, 0, 0))],
        out_shape=[jax.ShapeDtypeStruct((T, H * dv), F32),
                   jax.ShapeDtypeStruct((N, H, CHUNK, CHUNK), F32),
                   jax.ShapeDtypeStruct((N, H, dk, dv), F32)],
        scratch_shapes=[pltpu.VMEM((H, dk, dv), F32)],
        compiler_params=pltpu.CompilerParams(
            dimension_semantics=("arbitrary",), vmem_limit_bytes=VMEM_LIMIT),
    )(q, k, v, bgc)


def deltanet_bwd(q, k, v, bgc, xinv, states, do, H):
    T = q.shape[0]
    dk, dv = q.shape[1] // H, v.shape[1] // H
    N = T // CHUNK

    def body(q_ref, k_ref, v_ref, bgc_ref, x_ref, s_ref, do_ref,
             dq_ref, dk_ref, dv_ref, dbgc_ref, dstate):
        @pl.when(pl.program_id(0) == 0)
        def _():
            dstate[...] = jnp.zeros((H, dk, dv), F32)

        bg = bgc_ref[...]
        lane = lax.broadcasted_iota(jnp.int32, (CHUNK, LANE), 1)
        dbgc = jnp.zeros((CHUNK, LANE), F32)
        for h in range(H):
            beta, gc = bg[:, h:h + 1], bg[:, H + h:H + h + 1]
            qs, vs = slice(h * dk, (h + 1) * dk), slice(h * dv, (h + 1) * dv)
            _, vjp = jax.vjp(dn_chunk, q_ref[:, qs], k_ref[:, qs], v_ref[:, vs], beta, gc, s_ref[h], x_ref[h])
            dq, dk_, dv_, dbeta, dgc, ds, _ = vjp((do_ref[:, vs], dstate[h]))
            dq_ref[:, qs] = dq
            dk_ref[:, qs] = dk_
            dv_ref[:, vs] = dv_
            dstate[h] = ds
            dbgc = dbgc + jnp.where(lane == h, dbeta, 0.0) + jnp.where(lane == h + H, dgc, 0.0)
        dbgc_ref[...] = dbgc

    rn = lambda n: N - 1 - n
    return pl.pallas_call(
        body, name="deltanet_bwd", grid=(N,),
        in_specs=[pl.BlockSpec((CHUNK, H * dk), lambda n: (rn(n), 0)),
                  pl.BlockSpec((CHUNK, H * dk), lambda n: (rn(n), 0)),
                  pl.BlockSpec((CHUNK, H * dv), lambda n: (rn(n), 0)),
                  pl.BlockSpec((CHUNK, LANE), lambda n: (rn(n), 0)),
                  pl.BlockSpec((None, H, CHUNK, CHUNK), lambda n: (rn(n), 0, 0, 0)),
                  pl.BlockSpec((None, H, dk, dv), lambda n: (rn(n), 0, 0, 0)),
                  pl.BlockSpec((CHUNK, H * dv), lambda n: (rn(n), 0))],
        out_specs=[pl.BlockSpec((CHUNK, H * dk), lambda n: (rn(n), 0)),
                   pl.BlockSpec((CHUNK, H * dk), lambda n: (rn(n), 0)),
                   pl.BlockSpec((CHUNK, H * dv), lambda n: (rn(n), 0)),
                   pl.BlockSpec((CHUNK, LANE), lambda n: (rn(n), 0))],
        out_shape=[jax.ShapeDtypeStruct((T, H * dk), F32),
                   jax.ShapeDtypeStruct((T, H * dk), F32),
                   jax.ShapeDtypeStruct((T, H * dv), F32),
                   jax.ShapeDtypeStruct((T, LANE), F32)],
        scratch_shapes=[pltpu.VMEM((H, dk, dv), F32)],
        compiler_params=pltpu.CompilerParams(
            dimension_semantics=("arbitrary",), vmem_limit_bytes=VMEM_LIMIT),
    )(q, k, v, bgc, xinv, states, do)


def _place():
    x, y, c = lax.axis_index("x"), lax.axis_index("y"), lax.axis_index("c")
    chips = [(1 - x, y), (x, 1 - y), (1 - x, 1 - y)]
    return x, y, c, chips


def all_gather(name, shards):
    na = len(shards)

    def body(*refs):
        ins, outs = refs[:na], refs[na:2 * na]
        send_sems, recv_sems, local_sems = refs[2 * na:]
        x, y, c, chips = _place()
        me, sibling = (x, y, c), (x, y, 1 - c)

        def copy(a, k, block, to, src=None):
            dst = outs[a].at[4 * block[0] + 2 * block[1] + block[2]]
            return pltpu.make_async_remote_copy(
                src_ref=dst if src is None else src, dst_ref=dst,
                send_sem=send_sems.at[a, k], recv_sem=recv_sems.at[a, k],
                device_id=to, device_id_type=MESH)

        mine, first, passed = [], [], []
        for a in range(na):
            cp = pltpu.make_async_copy(ins[a], outs[a].at[4 * x + 2 * y + c], local_sems.at[a])
            cp.start()
            mine.append(cp)
        for a in range(na):
            cps = [copy(a, 0, me, sibling, src=ins[a])]
            cps += [copy(a, 1 + j, me, (*chip, c), src=ins[a]) for j, chip in enumerate(chips)]
            for cp in cps:
                cp.start()
            first += cps
        for a in range(na):
            for j, chip in enumerate(chips):
                copy(a, 1 + j, (*chip, c), me).wait_recv()
                cp = copy(a, 4 + j, (*chip, c), sibling)
                cp.start()
                passed.append(cp)
        for a in range(na):
            copy(a, 0, sibling, me).wait_recv()
            for j, chip in enumerate(chips):
                copy(a, 4 + j, (*chip, 1 - c), me).wait_recv()
        for cp in first + passed:
            cp.wait_send()
        for cp in mine:
            cp.wait()

    outs = pl.pallas_call(
        body, name=name,
        in_specs=[ANY] * na, out_specs=[ANY] * na,
        out_shape=[jax.ShapeDtypeStruct((N_DEV,) + s.shape, s.dtype) for s in shards],
        scratch_shapes=[pltpu.SemaphoreType.DMA((na, 7)), pltpu.SemaphoreType.DMA((na, 7)),
                        pltpu.SemaphoreType.DMA((na,))],
    )(*shards)
    return list(outs)


def exchange_siblings(name, grads):
    na = len(grads)

    def body(*refs):
        ins, recvs = refs[:na], refs[na:2 * na]
        send_sems, recv_sems = refs[2 * na:]
        x, y, c, _ = _place()
        sends = []
        for a in range(na):
            for s in range(4):
                rc = pltpu.make_async_remote_copy(
                    src_ref=ins[a].at[2 * s + (1 - c)], dst_ref=recvs[a].at[s],
                    send_sem=send_sems.at[a, s], recv_sem=recv_sems.at[a, s],
                    device_id=(x, y, 1 - c), device_id_type=MESH)
                rc.start()
                sends.append(rc)
        for rc in sends:
            rc.wait_recv()
        for rc in sends:
            rc.wait_send()

    outs = pl.pallas_call(
        body, name=name,
        in_specs=[ANY] * na, out_specs=[ANY] * na,
        out_shape=[jax.ShapeDtypeStruct((4,) + g.shape[1:], g.dtype) for g in grads],
        scratch_shapes=[pltpu.SemaphoreType.DMA((na, 4)), pltpu.SemaphoreType.DMA((na, 4))],
    )(*grads)
    return list(outs)


def exchange_chips(name, parts):
    na = len(parts)

    def body(*refs):
        ins, outs = refs[:na], refs[na:2 * na]
        send_sems, recv_sems = refs[2 * na:]
        x, y, c, chips = _place()
        sends = []
        for a in range(na):
            for j, (px, py) in enumerate(chips):
                rc = pltpu.make_async_remote_copy(
                    src_ref=ins[a].at[2 * px + py], dst_ref=outs[a].at[j],
                    send_sem=send_sems.at[a, j], recv_sem=recv_sems.at[a, j],
                    device_id=(px, py, c), device_id_type=MESH)
                rc.start()
                sends.append(rc)
        for rc in sends:
            rc.wait_recv()
        for rc in sends:
            rc.wait_send()

    outs = pl.pallas_call(
        body, name=name,
        in_specs=[ANY] * na, out_specs=[ANY] * na,
        out_shape=[jax.ShapeDtypeStruct((3,) + p.shape[1:], p.dtype) for p in parts],
        scratch_shapes=[pltpu.SemaphoreType.DMA((na, 3)), pltpu.SemaphoreType.DMA((na, 3))],
    )(*parts)
    return list(outs)


def pack(arrs):
    pieces = []
    for a in arrs:
        f = a.reshape(-1).astype(F32)
        pad = (-f.shape[0]) % LANE
        if pad:
            f = jnp.concatenate([f, jnp.zeros((pad,), F32)])
        pieces.append(f)
    flat = jnp.concatenate(pieces)
    rows = flat.shape[0] // LANE
    pad_rows = (-rows) % SUBLANE
    if pad_rows:
        flat = jnp.concatenate([flat, jnp.zeros((pad_rows * LANE,), F32)])
    return flat.reshape(-1, LANE)


def unpack(buf, shapes):
    flat = buf.reshape(-1)
    outs, off = [], 0
    for shp in shapes:
        n = int(np.prod(shp))
        outs.append(flat[off:off + n].reshape(shp))
        off += n + ((-n) % LANE)
    return outs


def _vjp_rows(fn, n_row_in, n_cot):
    def bwd(*args):
        rows = args[:n_row_in]
        cots = args[n_row_in:n_row_in + n_cot]
        consts = args[n_row_in + n_cot:]
        out, vjp = jax.vjp(fn, *rows, *consts)
        if isinstance(out, (tuple, list)):
            cot = tuple(c.astype(o.dtype) for c, o in zip(cots, out))
        else:
            cot = cots[0].astype(out.dtype)
        return vjp(cot)
    return bwd


def rms_fwd(name, x, g):
    return rowcall(name, rms_tile, [x], [g], [(x.shape[1], BF)])


def rms_bwd(name, x, g, dh, dres):
    D = x.shape[1]
    vj = _vjp_rows(rms_tile, 1, 1)

    def f(x_, dh_, dres_, g_):
        dx, dg = vj(x_, dh_, g_)
        return dx + dres_, dg
    return rowcall(name, f, [x, dh, dres], [g], [(D, F32)], [(1, D)])


def kernel(x, e_norm, e_w_in, e_conv_w, e_a_log, e_dt_bias, e_o_norm, e_ln_g, e_ln_b, e_w_s, e_b_s, e_w_out, o_norm, o_pw1, o_pw1_b, o_dw, o_dw_b, o_ln_g, o_ln_b, o_pw2, o_pw2_b, f_norm, f_w1, f_w2, final_norm, loss_target, m_e_norm, m_e_w_in, m_e_conv_w, m_e_a_log, m_e_dt_bias, m_e_o_norm, m_e_ln_g, m_e_ln_b, m_e_w_s, m_e_b_s, m_e_w_out, m_o_norm, m_o_pw1, m_o_pw1_b, m_o_dw, m_o_dw_b, m_o_ln_g, m_o_ln_b, m_o_pw2, m_o_pw2_b, m_f_norm, m_f_w1, m_f_w2, m_final_norm, v_e_norm, v_e_w_in, v_e_conv_w, v_e_a_log, v_e_dt_bias, v_e_o_norm, v_e_ln_g, v_e_ln_b, v_e_w_s, v_e_b_s, v_e_w_out, v_o_norm, v_o_pw1, v_o_pw1_b, v_o_dw, v_o_dw_b, v_o_ln_g, v_o_ln_b, v_o_pw2, v_o_pw2_b, v_f_norm, v_f_w1, v_f_w2, v_final_norm):
    names = ['e_norm', 'e_w_in', 'e_conv_w', 'e_a_log', 'e_dt_bias', 'e_o_norm', 'e_ln_g', 'e_ln_b', 'e_w_s', 'e_b_s', 'e_w_out', 'o_norm', 'o_pw1', 'o_pw1_b', 'o_dw', 'o_dw_b', 'o_ln_g', 'o_ln_b', 'o_pw2', 'o_pw2_b', 'f_norm', 'f_w1', 'f_w2', 'final_norm']
    W = dict(zip(names, [e_norm, e_w_in, e_conv_w, e_a_log, e_dt_bias, e_o_norm, e_ln_g, e_ln_b, e_w_s, e_b_s, e_w_out, o_norm, o_pw1, o_pw1_b, o_dw, o_dw_b, o_ln_g, o_ln_b, o_pw2, o_pw2_b, f_norm, f_w1, f_w2, final_norm]))
    Mo = dict(zip(names, [m_e_norm, m_e_w_in, m_e_conv_w, m_e_a_log, m_e_dt_bias, m_e_o_norm, m_e_ln_g, m_e_ln_b, m_e_w_s, m_e_b_s, m_e_w_out, m_o_norm, m_o_pw1, m_o_pw1_b, m_o_dw, m_o_dw_b, m_o_ln_g, m_o_ln_b, m_o_pw2, m_o_pw2_b, m_f_norm, m_f_w1, m_f_w2, m_final_norm]))
    Vo = dict(zip(names, [v_e_norm, v_e_w_in, v_e_conv_w, v_e_a_log, v_e_dt_bias, v_e_o_norm, v_e_ln_g, v_e_ln_b, v_e_w_s, v_e_b_s, v_e_w_out, v_o_norm, v_o_pw1, v_o_pw1_b, v_o_dw, v_o_dw_b, v_o_ln_g, v_o_ln_b, v_o_pw2, v_o_pw2_b, v_f_norm, v_f_w1, v_f_w2, v_final_norm]))

    T, D = x.shape[1], x.shape[2]
    H = e_a_log.shape[-1]
    dv = e_o_norm.shape[-1]
    dk = dv
    G = e_w_s.shape[1]
    AQK, AV, BW = H * dk, H * dv, e_ln_g.shape[-1]
    AQKV = 2 * AQK + AV
    in_cols = AQKV + AV + 2 * H + 2 * BW
    KA = e_conv_w.shape[1]
    KC = o_dw.shape[1]
    L = f_norm.shape[0]
    dev = 4 * lax.axis_index("x") + 2 * lax.axis_index("y") + lax.axis_index("c")
    x2d = x.reshape(T, D)
    tgt = loss_target.reshape(T, D)

    big = [e_w_in[0], e_w_out[0], o_pw1[0], o_pw2[0]] + [f_w1[l] for l in range(L)] + [f_w2[l] for l in range(L)]
    gathered = all_gather("gather_weights", [w.astype(BF) for w in big])
    g_win, g_wout, g_pw1, g_pw2 = gathered[:4]
    g_w1, g_w2 = gathered[4:4 + L], gathered[4 + L:4 + 2 * L]
    win = jnp.moveaxis(g_win, 0, 1).reshape(D, in_cols)
    w_qkv, w_z = win[:, :AQKV], win[:, AQKV:AQKV + AV]
    w_ba = jnp.pad(win[:, AQKV + AV:AQKV + AV + 2 * H], ((0, 0), (0, LANE - 2 * H)))
    w_uv = win[:, AQKV + AV + 2 * H:]
    wout = g_wout.reshape(-1, D)
    pw1 = jnp.moveaxis(g_pw1, 0, 1).reshape(D, 2 * D)
    pw2 = g_pw2.reshape(D, D)
    w1 = [jnp.moveaxis(g, 0, 1).reshape(D, -1) for g in g_w1]
    w2 = [g.reshape(-1, D) for g in g_w2]

    row = lambda a: a.reshape(1, -1).astype(F32)
    small_sharded = ['e_conv_w', 'o_norm', 'o_pw1_b', 'o_dw', 'o_dw_b', 'o_ln_g', 'o_ln_b', 'o_pw2_b']
    sm = all_gather("gather_small", [pack([W[n][0]]) for n in small_sharded])
    full = {}
    for n, g in zip(small_sharded, sm):
        shp = W[n][0].shape
        blocks = [unpack(g[d], [shp])[0] for d in range(N_DEV)]
        full[n] = jnp.concatenate(blocks, axis=-1)
    conv_w = full['e_conv_w']
    on_row, pw1_b_row = row(full['o_norm']), row(full['o_pw1_b'])
    dw_w, dw_b_row = full['o_dw'], row(full['o_dw_b'])
    oln_g_row, oln_b_row, pw2_b_row = row(full['o_ln_g']), row(full['o_ln_b']), row(full['o_pw2_b'])

    en_row = row(e_norm)
    alog_row = jnp.pad(row(e_a_log), ((0, 0), (H, LANE - 2 * H)))
    dtb_row = jnp.pad(row(e_dt_bias), ((0, 0), (H, LANE - 2 * H)))
    eon_row = row(e_o_norm)
    eln_g_row, eln_b_row = row(e_ln_g), row(e_ln_b)
    w_s = e_w_s[0]
    bs_t = e_b_s[0].T
    fn_rows = [row(f_norm[l]) for l in range(L)]
    fin_row = row(final_norm)

    h0 = rms_fwd("rms_e", x2d, en_row)
    qkv_raw = matmul("proj_qkv", h0, w_qkv, "nn")
    z_gate = matmul("proj_z", h0, w_z, "nn")
    ba = matmul("proj_ba", h0, w_ba, "nn")
    uv = matmul("proj_uv", h0, w_uv, "nn")

    cwa = min(512, AQKV)
    qkv_post = make_qkv_post(dk, cwa, 2 * AQK // cwa)
    ident = lambda t: t
    qkv = conv_fwd("qkv_conv", qkv_raw, conv_w, [], ident, qkv_post, [0], 1, KA, cw=cwa)
    qn, kn, vv = qkv[:, :AQK], qkv[:, AQK:2 * AQK], qkv[:, 2 * AQK:]
    bgc_fn = make_bgc(H)
    bgc = rowcall("bgc", bgc_fn, [ba], [alog_row, dtb_row], [(LANE, F32)])
    o_dn, xinv, states = deltanet_fwd(qn, kn, vv, bgc, H)
    out_a = rowcall("mixa_post", mixa_post_tile, [o_dn, z_gate], [eon_row], [(AV, BF)])
    out_b = rowcall("mixb", mixb_tile, [uv], [eln_g_row, eln_b_row, w_s, bs_t], [(BW, BF)])
    mix = jnp.concatenate([out_a, out_b], axis=-1)
    add_epi = lambda acc, r: (acc + r,)
    x1 = matmul("out_proj", mix, wout, "nn", epi=add_epi, extras=[x2d])

    def ffn_fwd(l, xin):
        hf = rms_fwd(f"rms_f{l}", xin, fn_rows[l])

        def epi(acc):
            r = jnp.maximum(acc, 0.0)
            return r * r, r
        a2, ar = matmul(f"ffn_up{l}", hf, w1[l], "nn", epi=epi, out_dtypes=(BF, BF))
        xo = matmul(f"ffn_down{l}", a2, w2[l], "nn", epi=add_epi, extras=[xin], tk=2048)
        return xo, (hf, a2, ar)

    x2, ffn0 = ffn_fwd(0, x1)

    h1 = rms_fwd("rms_o", x2, on_row)
    bias_epi = lambda acc, b: (acc + b,)
    zc = matmul("pw1", h1, pw1, "nn", epi=bias_epi, extras=[pw1_b_row])
    cwc = min(512, D)
    ncb = D // cwc
    cconv = conv_fwd("dw_conv", zc, dw_w, [dw_b_row], glu_pre, bias_post, [0, ncb], 1, KC, hb=32, cw=cwc)
    ln_silu = lambda c, g, b: jax.nn.silu(ln_tile(c, g, b))
    s_act = rowcall("ln_silu", ln_silu, [cconv], [oln_g_row, oln_b_row], [(D, BF)])
    x3 = matmul("pw2", s_act, pw2, "nn", epi=lambda acc, r, b: (r + (acc + b),), extras=[x2, pw2_b_row])
    x4, ffn1 = ffn_fwd(1, x3)

    def loss_bwd_tile(x_, t_, g_):
        l, vjp = jax.vjp(lambda a, b: loss_tile(a, b, t_), x_, g_)
        dx, dg = vjp(jnp.ones_like(l))
        return dx, l, dg
    dx4, loss_part, d_final = rowcall("loss_head", loss_bwd_tile, [x4, tgt], [fin_row],
                                      [(D, F32)], [(1, 1), (1, D)])
    loss = lax.psum(loss_part[0, 0], ("x", "y", "c"))

    def ffn_bwd(l, xin, saved, dxo):
        hf, a2, ar = saved
        dpre = matmul(f"ffn_down_dx{l}", dxo, w2[l], "nt", epi=lambda acc, r: (acc * (2.0 * r.astype(F32)),),
                      extras=[ar], out_dtypes=(BF,))
        dw2 = matmul(f"ffn_down_dw{l}", a2, dxo, "tn")
        dw1 = matmul(f"ffn_up_dw{l}", hf, dpre, "tn", colshard=True)
        dhf = matmul(f"ffn_up_dx{l}", dpre, w1[l], "nt")
        dxin, dg = rms_bwd(f"rms_f_bwd{l}", xin, fn_rows[l], dhf, dxo)
        return dxin, dw1, dw2, dg

    d_fnorm = [None] * L
    d_w1, d_w2 = [None] * L, [None] * L
    dx3, d_w1[1], d_w2[1], d_fnorm[1] = ffn_bwd(1, x3, ffn1, dx4)

    ds_act = matmul("pw2_dx", dx3, pw2, "nt")
    d_pw2 = matmul("pw2_dw", s_act, dx3, "tn")
    ln_silu_bwd = _vjp_rows(ln_silu, 1, 1)

    def ln_silu_bwd_tile(c_, ds_, dx3_, g_, b_):
        dc, dg, db = ln_silu_bwd(c_, ds_, g_, b_)
        return dc, dg, db, jnp.sum(dx3_, axis=0, keepdims=True)
    dcconv, d_oln_g, d_oln_b, d_pw2_b = rowcall(
        "ln_silu_bwd", ln_silu_bwd_tile, [cconv, ds_act, dx3], [oln_g_row, oln_b_row],
        [(D, F32)], [(1, D), (1, D), (1, D)])
    (dza, dzb), (sza, szb), d_dw, (d_dw_b,) = conv_bwd(
        "dw_conv_bwd", zc, dw_w, [dw_b_row], [dcconv], glu_pre, bias_post, [0, ncb], KC, hb=32, cw=cwc, tr=128)
    dzc = jnp.concatenate([dza, dzb], axis=-1)
    d_pw1_b = jnp.concatenate([sza, szb], axis=-1)
    d_pw1 = matmul("pw1_dw", h1, dzc, "tn", colshard=True)
    dh1 = matmul("pw1_dx", dzc, pw1, "nt")
    dx2, d_onorm = rms_bwd("rms_o_bwd", x2, on_row, dh1, dx3)

    dx1, d_w1[0], d_w2[0], d_fnorm[0] = ffn_bwd(0, x1, ffn0, dx2)

    dmix = matmul("out_proj_dx", dx1, wout, "nt")
    d_wout = matmul("out_proj_dw", mix, dx1, "tn")
    dmix_a, dmix_b = dmix[:, :AV], dmix[:, AV:]
    mixb_bwd = _vjp_rows(mixb_tile, 1, 1)
    duv, d_eln_g, d_eln_b, d_ws, d_bs_t = rowcall(
        "mixb_bwd", mixb_bwd, [uv, dmix_b], [eln_g_row, eln_b_row, w_s, bs_t],
        [(2 * BW, BF)], [(1, BW), (1, BW), w_s.shape, bs_t.shape])
    mixa_bwd = _vjp_rows(mixa_post_tile, 2, 1)
    do_dn, dz_gate, d_eon = rowcall(
        "mixa_post_bwd", mixa_bwd, [o_dn, z_gate, dmix_a], [eon_row],
        [(AV, F32), (AV, BF)], [(1, dv)])
    dq, dk_, dv_, dbgc = deltanet_bwd(qn, kn, vv, bgc, xinv, states, do_dn, H)
    bgc_bwd = _vjp_rows(bgc_fn, 1, 1)
    dba, d_alog_row, d_dtb_row = rowcall(
        "bgc_bwd", bgc_bwd, [ba, dbgc], [alog_row, dtb_row], [(LANE, BF)], [(1, LANE), (1, LANE)])
    dqkv = jnp.concatenate([dq, dk_, dv_], axis=-1)
    (dqkv_raw,), _, d_conv_w, _ = conv_bwd(
        "qkv_conv_bwd", qkv_raw, conv_w, [], [dqkv], ident, qkv_post, [0], KA, cw=cwa)

    dw_qkv = matmul("proj_qkv_dw", h0, dqkv_raw, "tn")
    dw_z = matmul("proj_z_dw", h0, dz_gate, "tn")
    dw_ba = matmul("proj_ba_dw", h0, dba, "tn")
    dw_uv = matmul("proj_uv_dw", h0, duv, "tn")
    dh0 = matmul("proj_qkv_dx", dqkv_raw, w_qkv, "nt")
    dh0 = matmul("proj_z_dx", dz_gate, w_z, "nt", epi=add_epi, extras=[dh0])
    dh0 = matmul("proj_ba_dx", dba, w_ba, "nt", epi=add_epi, extras=[dh0])
    dh0 = matmul("proj_uv_dx", duv, w_uv, "nt", epi=add_epi, extras=[dh0])
    grad_x, d_enorm = rms_bwd("rms_e_bwd", x2d, en_row, dh0, dx1)

    d_win = jnp.concatenate([dw_qkv, dw_z, dw_ba[:, :2 * H], dw_uv], axis=-1)
    G_win = jnp.moveaxis(d_win.reshape(D, N_DEV, in_cols // N_DEV), 1, 0)
    G_big = [G_win, d_wout.reshape(N_DEV, -1, D), d_pw1, d_pw2.reshape(N_DEV, -1, D)]
    G_big += d_w1 + [g.reshape(N_DEV, -1, D) for g in d_w2]
    recvs = exchange_siblings("reduce_d2d", G_big)
    where = jnp.stack([lax.axis_index("c"), 2 * lax.axis_index("x") + lax.axis_index("y")]).astype(jnp.int32)

    def add_pair(a, b):
        return (a + b,)
    parts = []
    for i, (g, rc) in enumerate(zip(G_big, recvs)):
        r, c = g.shape[1], g.shape[2]
        mine = lambda i, n, s: (2 * (i // (n // 4)) + s[0]) * (n // 4) + i % (n // 4)
        parts.append(rowcall(f"chip_sum{i}", add_pair, [(g.reshape(N_DEV * r, c), mine), rc.reshape(4 * r, c)], [],
                             [(c, BF)], tr=_tile(r, 512), sp=where, R=4 * r).reshape(4, r, c))
    final = exchange_chips("reduce_ici", parts)

    def adam_big(i, part, fin, w, m, v, lead):
        def f(p0, p1, p2, p3, w_, m_, v_):
            g = ((p0.astype(F32) + p1.astype(F32)) + p2.astype(F32)) + p3.astype(F32)
            return (g,) + adamw_tile(w_, g, m_, v_)
        r, C = fin.shape[-2], fin.shape[-1]
        own = lambda i, n, s: s[1] * n + i
        return rowcall(f"adam{i}", f, [(part.reshape(4 * r, C), own), (fin, 0), (fin, 1), (fin, 2),
                                       (w, lead), (m, lead), (v, lead)], [],
                       [(C, F32)] * 4, tr=256, sp=where, R=r)

    res = {}
    order = [('e_w_in', 0), ('e_w_out', 0), ('o_pw1', 0), ('o_pw2', 0)] + [('f_w1', l) for l in range(L)] + [('f_w2', l) for l in range(L)]
    per_layer = {}
    for i, ((n, l), part, fin) in enumerate(zip(order, parts, final)):
        out = adam_big(i, part, fin, W[n], Mo[n], Vo[n], l)
        per_layer.setdefault(n, []).append(out)
    for n, outs in per_layer.items():
        res[n] = tuple(jnp.stack([o[k] for o in outs]) for k in range(4))

    d_alog = d_alog_row[:, H:2 * H]
    d_dtb = d_dtb_row[:, H:2 * H]
    small_names = ['e_norm', 'e_conv_w', 'e_a_log', 'e_dt_bias', 'e_o_norm', 'e_ln_g', 'e_ln_b', 'e_w_s', 'e_b_s',
                   'o_norm', 'o_pw1_b', 'o_dw', 'o_dw_b', 'o_ln_g', 'o_ln_b', 'o_pw2_b', 'f_norm', 'final_norm']
    small_grads = [d_enorm, d_conv_w, d_alog, d_dtb, d_eon, d_eln_g, d_eln_b, d_ws, d_bs_t.T,
                   d_onorm, d_pw1_b, d_dw, d_dw_b, d_oln_g, d_oln_b, d_pw2_b,
                   jnp.concatenate(d_fnorm, axis=0), d_final]
    full_shapes = [g.shape for g in small_grads]
    gs_all = all_gather("gather_small_grads", [pack(small_grads)])[0]

    def sum8(*ps):
        s = ps[0]
        for p in ps[1:]:
            s = s + p
        return (s,)
    gs_sum = rowcall("small_sum", sum8, [(gs_all, d) for d in range(N_DEV)], [], [(LANE, F32)])
    g_full = dict(zip(small_names, unpack(gs_sum, full_shapes)))
    g_loc = {}
    for n in small_names:
        g = g_full[n]
        if n in small_sharded:
            per = g.shape[-1] // N_DEV
            g = lax.dynamic_slice_in_dim(g, dev * per, per, axis=-1)
        g_loc[n] = g.reshape(W[n].shape)
    packs = [pack([A[n] for n in small_names]) for A in (W, g_loc, Mo, Vo)]
    d_s, m_s, v_s = rowcall("adam_small", adamw_tile, packs, [], [(LANE, F32)] * 3)
    shapes = [W[n].shape for n in small_names]
    for n, d_, m_, v_ in zip(small_names, unpack(d_s, shapes), unpack(m_s, shapes), unpack(v_s, shapes)):
        res[n] = (g_loc[n], d_, m_, v_)

    grads = [res[n][0] for n in names]
    deltas = [res[n][1] for n in names]
    new_m = [res[n][2] for n in names]
    new_v = [res[n][3] for n in names]
    return (loss, grad_x.reshape(x.shape), *grads, *deltas, *new_m, *new_v)
```

```python
import functools
import math

import jax
import jax.numpy as jnp
import numpy as np
from jax import lax
from jax.experimental import pallas as pl
from jax.experimental.pallas import tpu as pltpu

F32 = jnp.float32
BF = jnp.bfloat16
EPS = 1e-6
CHUNK = 64
B_BLOCK = 128
LANE = 128
SUBLANE = 8
N_DEV = 8
VMEM_LIMIT = 56 * 1024 * 1024

ADAM_LR = 0.001
ADAM_B1 = 0.9
ADAM_B2 = 0.999
ADAM_EPS = 1e-08
ADAM_WD = 0.01
ADAM_STEP = 10

MESH = pl.DeviceIdType.MESH
ANY = pl.BlockSpec(memory_space=pl.ANY)


def _tile(n, pref, mult=SUBLANE):
    if n <= pref:
        return n
    t = (pref // mult) * mult
    while t >= mult:
        if n % t == 0:
            return t
        t -= mult
    return n


def _dg(a, b, ca, cb, hi):
    dims = (((ca,), (cb,)), ((), ()))
    if hi:
        return lax.dot_general(a.astype(F32), b.astype(F32), dims,
                               precision=lax.Precision.HIGHEST, preferred_element_type=F32)
    return lax.dot_general(a.astype(BF), b.astype(BF), dims, preferred_element_type=F32)


@functools.partial(jax.custom_vjp, nondiff_argnums=(2, 3, 4))
def mm(a, b, ca, cb, hi=False):
    return _dg(a, b, ca, cb, hi)


def _mm_fwd(a, b, ca, cb, hi):
    return _dg(a, b, ca, cb, hi), (a, b)


def _mm_bwd(ca, cb, hi, res, g):
    a, b = res
    if ca == 1:
        da = mm(g, b, 1, 1 - cb, hi)
    else:
        da = mm(b, g, 1 - cb, 1, hi)
    if cb == 0:
        db = mm(a, g, 1 - ca, 0, hi)
    else:
        db = mm(g, a, 0, 1 - ca, hi)
    return da.astype(a.dtype), db.astype(b.dtype)


mm.defvjp(_mm_fwd, _mm_bwd)


def matmul(name, a, b, mode, epi=None, extras=(), out_dtypes=(F32,), colshard=False,
           tm=512, tn=1024, tk=2048, after=None):
    afters = [] if after is None else [after]
    if mode == "nn":
        (M, K), (K2, N) = a.shape, b.shape
    elif mode == "nt":
        (M, K), (N, K2) = a.shape, b.shape
    else:
        (K, M), (K2, N) = a.shape, b.shape
    assert K == K2, (name, a.shape, b.shape, mode)
    tm = _tile(M, tm)
    tn = N // N_DEV if colshard else _tile(N, tn, LANE)
    tk = _tile(K, tk, LANE)
    nk = K // tk
    grid = (M // tm, N // tn, nk)
    if mode == "nn":
        a_spec = pl.BlockSpec((tm, tk), lambda i, j, k: (i, k))
        b_spec = pl.BlockSpec((tk, tn), lambda i, j, k: (k, j))
        ca, cb = 1, 0
    elif mode == "nt":
        a_spec = pl.BlockSpec((tm, tk), lambda i, j, k: (i, k))
        b_spec = pl.BlockSpec((tn, tk), lambda i, j, k: (j, k))
        ca, cb = 1, 1
    else:
        a_spec = pl.BlockSpec((tk, tm), lambda i, j, k: (k, i))
        b_spec = pl.BlockSpec((tk, tn), lambda i, j, k: (k, j))
        ca, cb = 0, 0
    ex_specs = []
    for e in extras:
        if e.shape[0] == 1:
            ex_specs.append(pl.BlockSpec((1, tn), lambda i, j, k: (0, j)))
        else:
            assert e.shape == (M, N), (name, e.shape)
            ex_specs.append(pl.BlockSpec((tm, tn), lambda i, j, k: (i, j)))
    if colshard:
        out_shape = [jax.ShapeDtypeStruct((N_DEV, M, tn), dt) for dt in out_dtypes]
        out_specs = [pl.BlockSpec((None, tm, tn), lambda i, j, k: (j, i, 0)) for _ in out_dtypes]
    else:
        out_shape = [jax.ShapeDtypeStruct((M, N), dt) for dt in out_dtypes]
        out_specs = [pl.BlockSpec((tm, tn), lambda i, j, k: (i, j)) for _ in out_dtypes]
    n_ex, n_out = len(extras), len(out_dtypes)

    def body(*refs):
        a_ref, b_ref = refs[0], refs[1]
        ex_refs = refs[2:2 + n_ex]
        first_out = 2 + n_ex + len(afters)
        o_refs = refs[first_out:first_out + n_out]
        acc_ref = refs[-1]
        k = pl.program_id(2)
        part = _dg(a_ref[...], b_ref[...], ca, cb, False)

        @pl.when(k == 0)
        def _():
            acc_ref[...] = part

        @pl.when(k > 0)
        def _():
            acc_ref[...] += part

        @pl.when(k == nk - 1)
        def _():
            acc = acc_ref[...]
            res = (acc,) if epi is None else epi(acc, *[r[...] for r in ex_refs])
            for o_ref, r in zip(o_refs, res):
                o_ref[...] = r.astype(o_ref.dtype)

    outs = pl.pallas_call(
        body, name=name, grid=grid,
        in_specs=[a_spec, b_spec] + ex_specs + [ANY] * len(afters),
        out_specs=out_specs, out_shape=out_shape,
        scratch_shapes=[pltpu.VMEM((tm, tn), F32)],
        compiler_params=pltpu.CompilerParams(
            dimension_semantics=("parallel", "parallel", "arbitrary"),
            vmem_limit_bytes=VMEM_LIMIT),
    )(a, b, *extras, *afters)
    return outs[0] if n_out == 1 else tuple(outs)


def rowcall(name, fn, rows, consts, out_rows, out_accs=(), tr=256, sp=None, R=None, after=None):
    afters = [] if after is None else [after]
    rows = [r if isinstance(r, tuple) else (r, None) for r in rows]
    R = rows[0][0].shape[-2] if R is None else R
    tr = _tile(R, tr)
    n = R // tr
    in_specs = []
    for arr, lead in rows:
        C = arr.shape[-1]
        if lead is None:
            assert arr.shape[-2] == R, (name, arr.shape, R)
            in_specs.append(pl.BlockSpec((tr, C), lambda i, *s: (i, 0)))
        elif callable(lead):
            in_specs.append(pl.BlockSpec((tr, C), lambda i, *s, lead=lead: (lead(i, n, *s), 0)))
        else:
            assert arr.shape[-2] == R, (name, arr.shape, R)
            in_specs.append(pl.BlockSpec((None, tr, C), lambda i, *s, lead=lead: (lead, i, 0)))
    for c in consts:
        in_specs.append(pl.BlockSpec(c.shape, lambda i, *s, nd=c.ndim: (0,) * nd))
    out_shape = [jax.ShapeDtypeStruct((R, C), dt) for C, dt in out_rows]
    out_specs = [pl.BlockSpec((tr, C), lambda i, *s: (i, 0)) for C, _ in out_rows]
    for shp in out_accs:
        out_shape.append(jax.ShapeDtypeStruct(shp, F32))
        out_specs.append(pl.BlockSpec(shp, lambda i, *s, nd=len(shp): (0,) * nd))
    n_in, n_row, n_acc = len(rows) + len(consts), len(out_rows), len(out_accs)
    n_sp = 0 if sp is None else 1

    def body(*refs):
        refs = refs[n_sp:]
        ins = [r[...] for r in refs[:n_in]]
        res = fn(*ins)
        if not isinstance(res, (tuple, list)):
            res = (res,)
        o_refs = refs[n_in + len(afters):]
        for o_ref, r in zip(o_refs[:n_row], res[:n_row]):
            o_ref[...] = r.astype(o_ref.dtype)
        if n_acc:
            first = pl.program_id(0) == 0
            for o_ref, r in zip(o_refs[n_row:], res[n_row:]):
                r = r.astype(F32).reshape(o_ref.shape)

                @pl.when(first)
                def _(o_ref=o_ref, r=r):
                    o_ref[...] = r

                @pl.when(jnp.logical_not(first))
                def _(o_ref=o_ref, r=r):
                    o_ref[...] += r

    params = pltpu.CompilerParams(dimension_semantics=("arbitrary",), vmem_limit_bytes=VMEM_LIMIT)
    operands = [a for a, _ in rows] + list(consts) + afters
    in_specs = in_specs + [ANY] * len(afters)
    if sp is None:
        outs = pl.pallas_call(body, name=name, grid=(n,), in_specs=in_specs, out_specs=out_specs,
                              out_shape=out_shape, compiler_params=params)(*operands)
    else:
        outs = pl.pallas_call(
            body, name=name, out_shape=out_shape, compiler_params=params,
            grid_spec=pltpu.PrefetchScalarGridSpec(
                num_scalar_prefetch=1, grid=(n,), in_specs=in_specs, out_specs=out_specs),
        )(sp, *operands)
    return outs[0] if len(outs) == 1 else tuple(outs)


def rms_tile(x, g):
    x = x.astype(F32)
    return x * lax.rsqrt(jnp.mean(x * x, axis=-1, keepdims=True) + EPS) * g


def gelu(x):
    return 0.5 * x * (1.0 + lax.erf(x * (1.0 / math.sqrt(2.0))))


def ln_tile(x, g, b):
    mu = jnp.mean(x, axis=-1, keepdims=True)
    xc = x - mu
    return xc * lax.rsqrt(jnp.mean(xc * xc, axis=-1, keepdims=True) + EPS) * g + b


def lane_groups(fn, width, *arrs):
    n = arrs[0].shape[-1] // width
    outs = [fn(*[a[:, i * width:(i + 1) * width] for a in arrs]) for i in range(n)]
    return jnp.concatenate(outs, axis=-1)


def mixa_post_tile(o, z, o_norm):
    dv = o_norm.shape[-1]
    on = lane_groups(lambda t: rms_tile(t, o_norm), dv, o)
    return on * jax.nn.silu(z)


def mixb_tile(uv, ln_g, ln_b, w_s, bs_t):
    G = w_s.shape[0]
    gw = ln_g.shape[-1]
    dg = gw // G
    tr = uv.shape[0]
    u = gelu(uv[:, :gw])
    vg = gelu(uv[:, gw:])
    ii = lax.broadcasted_iota(jnp.int32, (B_BLOCK, B_BLOCK), 0)
    jj = lax.broadcasted_iota(jnp.int32, (B_BLOCK, B_BLOCK), 1)
    mask = (jj // CHUNK) <= (ii // CHUNK)
    cols = []
    for g in range(G):
        sl = slice(g * dg, (g + 1) * dg)
        vn = ln_tile(vg[:, sl], ln_g[:, sl], ln_b[:, sl])
        wm = jnp.where(mask, w_s[g], 0.0)
        blocks = []
        for m in range(tr // B_BLOCK):
            blk = vn[m * B_BLOCK:(m + 1) * B_BLOCK, :]
            blocks.append(mm(wm, blk, 1, 0) + bs_t[:, g:g + 1])
        mixed = blocks[0] if len(blocks) == 1 else jnp.concatenate(blocks, axis=0)
        cols.append(u[:, sl] * mixed)
    return jnp.concatenate(cols, axis=-1)


def bgc_tile(ba, alog_row, dtb_row):
    tr = ba.shape[0]
    beta = jax.nn.sigmoid(ba)
    g = -jnp.exp(alog_row) * jax.nn.softplus(ba + dtb_row)
    ii = lax.broadcasted_iota(jnp.int32, (tr, tr), 0)
    jj = lax.broadcasted_iota(jnp.int32, (tr, tr), 1)
    tri = jnp.where((ii // CHUNK == jj // CHUNK) & (jj <= ii), 1.0, 0.0).astype(F32)
    gc = mm(tri, g, 1, 0, True)
    return beta, gc


def make_bgc(H):
    def f(ba, alog_row, dtb_row):
        beta, gc = bgc_tile(ba, alog_row, dtb_row)
        lane = lax.broadcasted_iota(jnp.int32, ba.shape, 1)
        return jnp.where(lane < H, beta, jnp.where(lane < 2 * H, gc, 0.0))
    return f


def loss_tile(x, g, target):
    y = rms_tile(x, g)
    err = y - target
    return 0.5 * jnp.sum(jnp.mean(err * err, axis=-1, keepdims=True), axis=0, keepdims=True)


def adamw_tile(w, g, m, v):
    m = ADAM_B1 * m + (1.0 - ADAM_B1) * g
    v = ADAM_B2 * v + (1.0 - ADAM_B2) * (g * g)
    m_hat = m / (1.0 - ADAM_B1 ** ADAM_STEP)
    v_hat = v / (1.0 - ADAM_B2 ** ADAM_STEP)
    delta = -ADAM_LR * (m_hat / (jnp.sqrt(v_hat) + ADAM_EPS) + ADAM_WD * w)
    return delta, m, v


def _conv_specs(T, tr, hb, cw, col_blocks, rev):
    n = T // tr

    def ri(i):
        return (n - 1 - i) if rev else i

    tile_specs, halo_specs = [], []
    for off in col_blocks:
        tile_specs.append(pl.BlockSpec((tr, cw), lambda j, i, off=off: (ri(i), j + off)))
        halo_specs.append(pl.BlockSpec(
            (hb, cw), lambda j, i, off=off: (jnp.maximum(ri(i) * (tr // hb) - 1, 0), j + off)))
    return n, ri, tile_specs, halo_specs


def conv_fwd(name, x, w, consts, pre, post, col_blocks, n_out, K, out_dtype=F32, tr=256, hb=8, cw=512):
    T = x.shape[0]
    C = w.shape[1]
    tr, cw = _tile(T, tr, hb), min(cw, C)
    nb = len(col_blocks)
    n, ri, tile_specs, halo_specs = _conv_specs(T, tr, hb, cw, col_blocks, False)
    w_spec = pl.BlockSpec((K, cw), lambda j, i: (0, j))
    c_specs = [pl.BlockSpec((1, cw), lambda j, i: (0, j)) for _ in consts]

    def body(*refs):
        tiles = [r[...] for r in refs[:nb]]
        halos = [r[...] for r in refs[nb:2 * nb]]
        w_ref = refs[2 * nb]
        cs = [r[...] for r in refs[2 * nb + 1:2 * nb + 1 + len(consts)]]
        o_refs = refs[2 * nb + 1 + len(consts):-1]
        pbuf = refs[-1]
        i = pl.program_id(1)
        pbuf[0:hb, :] = jnp.where(i > 0, pre(*halos), 0.0)
        pbuf[hb:hb + tr, :] = pre(*tiles)
        c = jnp.zeros((tr, cw), F32)
        for k in range(K):
            c = c + w_ref[k:k + 1, :] * pbuf[pl.ds(hb - (K - 1) + k, tr), :]
        res = post(c, pl.program_id(0), *cs)
        for o_ref, r in zip(o_refs, res):
            o_ref[...] = r.astype(o_ref.dtype)

    outs = pl.pallas_call(
        body, name=name, grid=(C // cw, n),
        in_specs=tile_specs + halo_specs + [w_spec] + c_specs,
        out_specs=[pl.BlockSpec((tr, cw), lambda j, i: (i, j)) for _ in range(n_out)],
        out_shape=[jax.ShapeDtypeStruct((T, C), out_dtype) for _ in range(n_out)],
        scratch_shapes=[pltpu.VMEM((hb + tr, cw), F32)],
        compiler_params=pltpu.CompilerParams(
            dimension_semantics=("parallel", "arbitrary"), vmem_limit_bytes=VMEM_LIMIT),
    )(*([x] * nb), *([x] * nb), w, *consts)
    return outs[0] if n_out == 1 else tuple(outs)


def conv_bwd(name, x, w, consts, grads, pre, post, col_blocks, K, tr=256, hb=8, cw=512):
    T, Cx = x.shape
    C = w.shape[1]
    tr, cw = _tile(T, tr, hb), min(cw, C)
    nb = len(col_blocks)
    n, ri, tile_specs, halo_specs = _conv_specs(T, tr, hb, cw, col_blocks, True)
    w_spec = pl.BlockSpec((K, cw), lambda j, i: (0, j))
    c_specs = [pl.BlockSpec((1, cw), lambda j, i: (0, j)) for _ in consts]
    g_specs = [pl.BlockSpec((tr, cw), lambda j, i: (ri(i), j)) for _ in grads]
    nc, ng = len(consts), len(grads)

    def body(*refs):
        p = 0
        tile_refs = refs[p:p + nb]; p += nb
        halo_refs = refs[p:p + nb]; p += nb
        w_ref = refs[p]; p += 1
        cs = [r[...] for r in refs[p:p + nc]]; p += nc
        gs = [r[...] for r in refs[p:p + ng]]; p += ng
        dx_refs = refs[p:p + nb]; p += nb
        sum_refs = refs[p:p + nb]; p += nb
        dw_ref = refs[p]; p += 1
        dc_refs = refs[p:p + nc]; p += nc
        pbuf, dbuf, ebuf, carry = refs[p:p + 4]
        i = pl.program_id(1)
        first = i == 0
        tiles = [r[...] for r in tile_refs]
        halos = [r[...] for r in halo_refs]
        p_tile, vjp_pre = jax.vjp(pre, *tiles)
        pbuf[0:hb, :] = jnp.where(ri(i) > 0, pre(*halos), 0.0)
        pbuf[hb:hb + tr, :] = p_tile
        c = jnp.zeros((tr, cw), F32)
        for k in range(K):
            c = c + w_ref[k:k + 1, :] * pbuf[pl.ds(hb - (K - 1) + k, tr), :]
        cid = pl.program_id(0)
        _, vjp_post = jax.vjp(lambda c_, *cs_: post(c_, cid, *cs_), c, *cs)
        dres = vjp_post(tuple(g.astype(F32) for g in gs))
        dc = dres[0]
        dbuf[0:hb, :] = jnp.zeros((hb, cw), F32)
        dbuf[hb:hb + tr, :] = dc
        dbuf[hb + tr:hb + tr + hb, :] = jnp.zeros((hb, cw), F32)
        dp = jnp.zeros((hb + tr, cw), F32)
        dws = []
        for k in range(K):
            dp = dp + w_ref[k:k + 1, :] * dbuf[pl.ds(K - 1 - k, hb + tr), :]
            dws.append(jnp.sum(dc * pbuf[pl.ds(hb - (K - 1) + k, tr), :], axis=0, keepdims=True))
        dw = jnp.concatenate(dws, axis=0)
        ebuf[...] = dp

        @pl.when(jnp.logical_not(first))
        def _():
            ebuf[tr:tr + hb, :] += carry[...]

        carry[...] = ebuf[0:hb, :]
        dtiles = vjp_pre(ebuf[hb:hb + tr, :])
        for r, s, d in zip(dx_refs, sum_refs, dtiles):
            r[...] = d.astype(r.dtype)
            ds_ = jnp.sum(d, axis=0, keepdims=True)

            @pl.when(first)
            def _(s=s, ds_=ds_):
                s[...] = ds_

            @pl.when(jnp.logical_not(first))
            def _(s=s, ds_=ds_):
                s[...] += ds_

        accs = [(dw_ref, dw)] + [(r, d) for r, d in zip(dc_refs, dres[1:])]
        for r, d in accs:
            @pl.when(first)
            def _(r=r, d=d):
                r[...] = d

            @pl.when(jnp.logical_not(first))
            def _(r=r, d=d):
                r[...] += d

    n_cb = C // cw
    outs = pl.pallas_call(
        body, name=name, grid=(n_cb, n),
        in_specs=tile_specs + halo_specs + [w_spec] + c_specs + g_specs,
        out_specs=([pl.BlockSpec((tr, cw), lambda j, i: (ri(i), j)) for _ in col_blocks]
                   + [pl.BlockSpec((1, cw), lambda j, i: (0, j)) for _ in col_blocks]
                   + [pl.BlockSpec((K, cw), lambda j, i: (0, j))]
                   + [pl.BlockSpec((1, cw), lambda j, i: (0, j)) for _ in consts]),
        out_shape=([jax.ShapeDtypeStruct((T, C), BF) for _ in col_blocks]
                   + [jax.ShapeDtypeStruct((1, C), F32) for _ in col_blocks]
                   + [jax.ShapeDtypeStruct((K, C), F32)]
                   + [jax.ShapeDtypeStruct((1, C), F32) for _ in consts]),
        scratch_shapes=[pltpu.VMEM((hb + tr, cw), F32), pltpu.VMEM((hb + tr + hb, cw), F32),
                        pltpu.VMEM((hb + tr, cw), F32), pltpu.VMEM((hb, cw), F32)],
        compiler_params=pltpu.CompilerParams(
            dimension_semantics=("parallel", "arbitrary"), vmem_limit_bytes=VMEM_LIMIT),
    )(*([x] * nb), *([x] * nb), w, *consts, *grads)
    dxs = outs[:nb]
    sums = outs[nb:2 * nb]
    dw = outs[2 * nb]
    dcs = outs[2 * nb + 1:]
    return dxs, sums, dw, dcs


def make_qkv_post(dk, cw, n_qk_chunks):
    def l2(t):
        return t * lax.rsqrt(jnp.sum(t * t, axis=-1, keepdims=True) + EPS)

    def post(c, cid):
        s = jax.nn.silu(c)
        normed = lane_groups(l2, dk, s)
        return (jnp.where(cid < n_qk_chunks, normed, s),)
    return post


def glu_pre(za, zb):
    return za * jax.nn.sigmoid(zb)


def bias_post(c, cid, b):
    return (c + b,)


def _col_to_row(col):
    C = col.shape[0]
    ii = lax.broadcasted_iota(jnp.int32, (C, C), 0)
    jj = lax.broadcasted_iota(jnp.int32, (C, C), 1)
    return jnp.sum(jnp.where(ii == jj, jnp.broadcast_to(col, (C, C)), 0.0), axis=0, keepdims=True)


@jax.custom_vjp
def solve_with_inverse(a, rhs, x):
    return mm(x, rhs, 1, 0, True)


def _swi_fwd(a, rhs, x):
    sol = mm(x, rhs, 1, 0, True)
    return sol, (x, sol)


def _swi_bwd(res, dsol):
    x, sol = res
    drhs = mm(x, dsol, 0, 0, True)
    da = -mm(drhs, sol, 1, 1, True)
    return da, drhs, jnp.zeros_like(x)


solve_with_inverse.defvjp(_swi_fwd, _swi_bwd)


def unit_lower_inverse(a):
    C = a.shape[0]
    ii = lax.broadcasted_iota(jnp.int32, (C, C), 0)
    jj = lax.broadcasted_iota(jnp.int32, (C, C), 1)
    x = jnp.where(ii == jj, 1.0, 0.0).astype(F32) - a
    p = mm(a, a, 1, 0, True)
    steps = int(math.log2(C)) - 1
    for s in range(steps):
        x = x + mm(x, p, 1, 0, True)
        if s < steps - 1:
            p = mm(p, p, 1, 0, True)
    return x


def dn_masks(C):
    ii = lax.broadcasted_iota(jnp.int32, (C, C), 0)
    jj = lax.broadcasted_iota(jnp.int32, (C, C), 1)
    return ii >= jj, ii > jj


def dn_pre(q, k, v, beta, gc):
    C, dk = q.shape
    tri, strict = dn_masks(C)
    q = q * (dk ** -0.5)
    diff = gc - _col_to_row(gc)
    decay = jnp.where(tri, jnp.exp(jnp.where(tri, diff, 0.0)), 0.0)
    kb = k * beta
    vb = v * beta
    a = jnp.where(strict, mm(kb, k, 1, 1) * decay, 0.0)
    eg = jnp.exp(gc)
    rhs = jnp.concatenate([vb, kb * eg], axis=-1)
    attn = mm(q, k, 1, 1) * decay
    qd = q * eg
    g_last = gc[C - 1:C, :]
    kt = k * jnp.exp(g_last - gc)
    gl = jnp.exp(g_last)
    return a, rhs, attn, qd, kt, gl


def dn_chunk(q, k, v, beta, gc, state, x):
    dv = v.shape[-1]
    a, rhs, attn, qd, kt, gl = dn_pre(q, k, v, beta, gc)
    sol = solve_with_inverse(a, rhs, x)
    u, w = sol[:, :dv], sol[:, dv:]
    vn = u - mm(w, state, 1, 0)
    o = mm(qd, state, 1, 0) + mm(attn, vn, 1, 0)
    new_state = state * gl + mm(kt, vn, 0, 0)
    return o, new_state


def deltanet_fwd(q, k, v, bgc, H):
    T = q.shape[0]
    dk, dv = q.shape[1] // H, v.shape[1] // H
    N = T // CHUNK

    def body(q_ref, k_ref, v_ref, bgc_ref, o_ref, x_ref, s_ref, state):
        @pl.when(pl.program_id(0) == 0)
        def _():
            state[...] = jnp.zeros((H, dk, dv), F32)

        bg = bgc_ref[...]
        for h in range(H):
            beta, gc = bg[:, h:h + 1], bg[:, H + h:H + h + 1]
            qv = q_ref[:, h * dk:(h + 1) * dk]
            kv = k_ref[:, h * dk:(h + 1) * dk]
            vv = v_ref[:, h * dv:(h + 1) * dv]
            a = dn_pre(qv, kv, vv, beta, gc)[0]
            x = unit_lower_inverse(a)
            s = state[h]
            o, s_new = dn_chunk(qv, kv, vv, beta, gc, s, x)
            o_ref[:, h * dv:(h + 1) * dv] = o
            x_ref[h] = x
            s_ref[h] = s
            state[h] = s_new

    return pl.pallas_call(
        body, name="deltanet_fwd", grid=(N,),
        in_specs=[pl.BlockSpec((CHUNK, H * dk), lambda n: (n, 0)),
                  pl.BlockSpec((CHUNK, H * dk), lambda n: (n, 0)),
                  pl.BlockSpec((CHUNK, H * dv), lambda n: (n, 0)),
                  pl.BlockSpec((CHUNK, LANE), lambda n: (n, 0))],
        out_specs=[pl.BlockSpec((CHUNK, H * dv), lambda n: (n, 0)),
                   pl.BlockSpec((None, H, CHUNK, CHUNK), lambda n: (n, 0, 0, 0)),
                   pl.BlockSpec((None, H, dk, dv), lambda n: (n, 0, 0, 0))],
        out_shape=[jax.ShapeDtypeStruct((T, H * dv), F32),
                   jax.ShapeDtypeStruct((N, H, CHUNK, CHUNK), F32),
                   jax.ShapeDtypeStruct((N, H, dk, dv), F32)],
        scratch_shapes=[pltpu.VMEM((H, dk, dv), F32)],
        compiler_params=pltpu.CompilerParams(
            dimension_semantics=("arbitrary",), vmem_limit_bytes=VMEM_LIMIT),
    )(q, k, v, bgc)


def deltanet_bwd(q, k, v, bgc, xinv, states, do, H):
    T = q.shape[0]
    dk, dv = q.shape[1] // H, v.shape[1] // H
    N = T // CHUNK

    def body(q_ref, k_ref, v_ref, bgc_ref, x_ref, s_ref, do_ref,
             dq_ref, dk_ref, dv_ref, dbgc_ref, dstate):
        @pl.when(pl.program_id(0) == 0)
        def _():
            dstate[...] = jnp.zeros((H, dk, dv), F32)

        bg = bgc_ref[...]
        lane = lax.broadcasted_iota(jnp.int32, (CHUNK, LANE), 1)
        dbgc = jnp.zeros((CHUNK, LANE), F32)
        for h in range(H):
            beta, gc = bg[:, h:h + 1], bg[:, H + h:H + h + 1]
            qs, vs = slice(h * dk, (h + 1) * dk), slice(h * dv, (h + 1) * dv)
            _, vjp = jax.vjp(dn_chunk, q_ref[:, qs], k_ref[:, qs], v_ref[:, vs], beta, gc, s_ref[h], x_ref[h])
            dq, dk_, dv_, dbeta, dgc, ds, _ = vjp((do_ref[:, vs], dstate[h]))
            dq_ref[:, qs] = dq
            dk_ref[:, qs] = dk_
            dv_ref[:, vs] = dv_
            dstate[h] = ds
            dbgc = dbgc + jnp.where(lane == h, dbeta, 0.0) + jnp.where(lane == h + H, dgc, 0.0)
        dbgc_ref[...] = dbgc

    rn = lambda n: N - 1 - n
    return pl.pallas_call(
        body, name="deltanet_bwd", grid=(N,),
        in_specs=[pl.BlockSpec((CHUNK, H * dk), lambda n: (rn(n), 0)),
                  pl.BlockSpec((CHUNK, H * dk), lambda n: (rn(n), 0)),
                  pl.BlockSpec((CHUNK, H * dv), lambda n: (rn(n), 0)),
                  pl.BlockSpec((CHUNK, LANE), lambda n: (rn(n), 0)),
                  pl.BlockSpec((None, H, CHUNK, CHUNK), lambda n: (rn(n), 0, 0, 0)),
                  pl.BlockSpec((None, H, dk, dv), lambda n: (rn(n), 0, 0, 0)),
                  pl.BlockSpec((CHUNK, H * dv), lambda n: (rn(n), 0))],
        out_specs=[pl.BlockSpec((CHUNK, H * dk), lambda n: (rn(n), 0)),
                   pl.BlockSpec((CHUNK, H * dk), lambda n: (rn(n), 0)),
                   pl.BlockSpec((CHUNK, H * dv), lambda n: (rn(n), 0)),
                   pl.BlockSpec((CHUNK, LANE), lambda n: (rn(n), 0))],
        out_shape=[jax.ShapeDtypeStruct((T, H * dk), F32),
                   jax.ShapeDtypeStruct((T, H * dk), F32),
                   jax.ShapeDtypeStruct((T, H * dv), F32),
                   jax.ShapeDtypeStruct((T, LANE), F32)],
        scratch_shapes=[pltpu.VMEM((H, dk, dv), F32)],
        compiler_params=pltpu.CompilerParams(
            dimension_semantics=("arbitrary",), vmem_limit_bytes=VMEM_LIMIT),
    )(q, k, v, bgc, xinv, states, do)


def _place():
    x, y, c = lax.axis_index("x"), lax.axis_index("y"), lax.axis_index("c")
    chips = [(1 - x, y), (x, 1 - y), (1 - x, 1 - y)]
    return x, y, c, chips


def all_gather(name, shards):
    na = len(shards)

    def body(*refs):
        ins, outs = refs[:na], refs[na:2 * na]
        send_sems, recv_sems, local_sems = refs[2 * na:]
        x, y, c, chips = _place()
        me, sibling = (x, y, c), (x, y, 1 - c)

        def copy(a, k, block, to, src=None):
            dst = outs[a].at[4 * block[0] + 2 * block[1] + block[2]]
            return pltpu.make_async_remote_copy(
                src_ref=dst if src is None else src, dst_ref=dst,
                send_sem=send_sems.at[a, k], recv_sem=recv_sems.at[a, k],
                device_id=to, device_id_type=MESH)

        mine, first, passed = [], [], []
        for a in range(na):
            cp = pltpu.make_async_copy(ins[a], outs[a].at[4 * x + 2 * y + c], local_sems.at[a])
            cp.start()
            mine.append(cp)
        for a in range(na):
            cps = [copy(a, 0, me, sibling, src=ins[a])]
            cps += [copy(a, 1 + j, me, (*chip, c), src=ins[a]) for j, chip in enumerate(chips)]
            for cp in cps:
                cp.start()
            first += cps
        for a in range(na):
            for j, chip in enumerate(chips):
                copy(a, 1 + j, (*chip, c), me).wait_recv()
                cp = copy(a, 4 + j, (*chip, c), sibling)
                cp.start()
                passed.append(cp)
        for a in range(na):
            copy(a, 0, sibling, me).wait_recv()
            for j, chip in enumerate(chips):
                copy(a, 4 + j, (*chip, 1 - c), me).wait_recv()
        for cp in first + passed:
            cp.wait_send()
        for cp in mine:
            cp.wait()

    outs = pl.pallas_call(
        body, name=name,
        in_specs=[ANY] * na, out_specs=[ANY] * na,
        out_shape=[jax.ShapeDtypeStruct((N_DEV,) + s.shape, s.dtype) for s in shards],
        scratch_shapes=[pltpu.SemaphoreType.DMA((na, 7)), pltpu.SemaphoreType.DMA((na, 7)),
                        pltpu.SemaphoreType.DMA((na,))],
    )(*shards)
    return list(outs)


HBM_SPEC = pl.BlockSpec(memory_space=pltpu.HBM)
SEM_SPEC = pl.BlockSpec(memory_space=pltpu.SEMAPHORE)
EFFECT = pltpu.SideEffectType.DATAFLOW_SIDE_EFFECTING


def _descriptors(plan, bufs, send_sems, recv_sems):
    return [pltpu.make_async_remote_copy(src_ref=src, dst_ref=dst, send_sem=send_sems.at[k],
                                         recv_sem=recv_sems.at[k], device_id=dev, device_id_type=MESH)
            for k, (src, dst, dev) in enumerate(plan(bufs))]


def split_start(name, bufs, plan, n, after):
    nb = len(bufs)

    def body(*refs):
        for cp in _descriptors(plan, refs[:nb], refs[nb + 1], refs[nb + 2]):
            cp.start()
        refs[-1][...] = jnp.zeros((SUBLANE, LANE), F32)

    outs = pl.pallas_call(
        body, name=name,
        out_shape=(pltpu.SemaphoreType.DMA((n,)), pltpu.SemaphoreType.DMA((n,)),
                   *[pltpu.HBM(b.shape, b.dtype) for b in bufs],
                   jax.ShapeDtypeStruct((SUBLANE, LANE), F32)),
        in_specs=[HBM_SPEC] * nb + [ANY],
        out_specs=(SEM_SPEC, SEM_SPEC, *[HBM_SPEC] * nb, pl.BlockSpec(memory_space=pltpu.VMEM)),
        input_output_aliases={i: 2 + i for i in range(nb)},
        compiler_params=pltpu.CompilerParams(has_side_effects=EFFECT),
    )(*[pltpu.with_memory_space_constraint(b, pltpu.HBM) for b in bufs], after)
    return outs[0], outs[1], list(outs[2:2 + nb]), outs[-1]


def split_wait(name, send_sems, recv_sems, bufs, plan, after):
    nb = len(bufs)

    def body(*refs):
        cps = _descriptors(plan, refs[:nb], refs[nb], refs[nb + 1])
        for cp in cps:
            cp.wait_recv()
        for cp in cps:
            cp.wait_send()
        refs[-1][...] = jnp.zeros((SUBLANE, LANE), F32)

    outs = pl.pallas_call(
        body, name=name,
        out_shape=[pltpu.HBM(b.shape, b.dtype) for b in bufs] + [jax.ShapeDtypeStruct((SUBLANE, LANE), F32)],
        in_specs=[HBM_SPEC] * nb + [SEM_SPEC, SEM_SPEC, ANY],
        out_specs=[HBM_SPEC] * nb + [pl.BlockSpec(memory_space=pltpu.VMEM)],
        input_output_aliases={i: i for i in range(nb)},
        compiler_params=pltpu.CompilerParams(has_side_effects=EFFECT),
    )(*bufs, send_sems, recv_sems, after)
    return list(outs[:nb]), outs[-1]


def _block(px, py, pc):
    return 4 * px + 2 * py + pc


def plan_gather_ici(na):
    def plan(bufs):
        x, y, c, chips = _place()
        out = []
        for a in range(na):
            dst = bufs[na + a].at[_block(x, y, c)]
            out.append((bufs[a], dst, (x, y, 1 - c)))
            out += [(bufs[a], dst, (px, py, c)) for px, py in chips]
        return out
    return plan


def plan_gather_pass(na):
    def plan(bufs):
        x, y, c, chips = _place()
        out = []
        for a in range(na):
            for px, py in chips:
                blk = bufs[a].at[_block(px, py, c)]
                out.append((blk, blk, (x, y, 1 - c)))
        return out
    return plan


def plan_reduce_d2d(na):
    def plan(bufs):
        x, y, c, _ = _place()
        return [(bufs[a].at[2 * s + (1 - c)], bufs[na + a].at[s], (x, y, 1 - c))
                for a in range(na) for s in range(4)]
    return plan


def plan_reduce_ici(na):
    def plan(bufs):
        x, y, c, chips = _place()
        return [(bufs[a].at[2 * px + py], bufs[na + a].at[j], (px, py, c))
                for a in range(na) for j, (px, py) in enumerate(chips)]
    return plan


def place_own(name, land, shard, dev):
    r, c = shard.shape
    tr = _tile(r, 512)

    def body(sp_ref, s_ref, land_ref, o_ref):
        o_ref[...] = s_ref[...]

    return pl.pallas_call(
        body, name=name, out_shape=jax.ShapeDtypeStruct(land.shape, land.dtype),
        grid_spec=pltpu.PrefetchScalarGridSpec(
            num_scalar_prefetch=1, grid=(r // tr,),
            in_specs=[pl.BlockSpec((tr, c), lambda i, s: (i, 0)), ANY],
            out_specs=pl.BlockSpec((None, tr, c), lambda i, s: (s[0], i, 0))),
        input_output_aliases={2: 0},
        compiler_params=pltpu.CompilerParams(dimension_semantics=("arbitrary",)),
    )(dev, shard, land)


def pack(arrs, row_mult=SUBLANE):
    pieces = []
    for a in arrs:
        f = a.reshape(-1).astype(F32)
        pad = (-f.shape[0]) % LANE
        if pad:
            f = jnp.concatenate([f, jnp.zeros((pad,), F32)])
        pieces.append(f)
    flat = jnp.concatenate(pieces)
    rows = flat.shape[0] // LANE
    pad_rows = (-rows) % row_mult
    if pad_rows:
        flat = jnp.concatenate([flat, jnp.zeros((pad_rows * LANE,), F32)])
    return flat.reshape(-1, LANE)


def unpack(buf, shapes):
    flat = buf.reshape(-1)
    outs, off = [], 0
    for shp in shapes:
        n = int(np.prod(shp))
        outs.append(flat[off:off + n].reshape(shp))
        off += n + ((-n) % LANE)
    return outs


def _vjp_rows(fn, n_row_in, n_cot):
    def bwd(*args):
        rows = args[:n_row_in]
        cots = args[n_row_in:n_row_in + n_cot]
        consts = args[n_row_in + n_cot:]
        out, vjp = jax.vjp(fn, *rows, *consts)
        if isinstance(out, (tuple, list)):
            cot = tuple(c.astype(o.dtype) for c, o in zip(cots, out))
        else:
            cot = cots[0].astype(out.dtype)
        return vjp(cot)
    return bwd


def rms_fwd(name, x, g):
    return rowcall(name, rms_tile, [x], [g], [(x.shape[1], BF)])


def rms_bwd(name, x, g, dh, dres):
    D = x.shape[1]
    vj = _vjp_rows(rms_tile, 1, 1)

    def f(x_, dh_, dres_, g_):
        dx, dg = vj(x_, dh_, g_)
        return dx + dres_, dg
    return rowcall(name, f, [x, dh, dres], [g], [(D, F32)], [(1, D)])


def kernel(x, e_norm, e_w_in, e_conv_w, e_a_log, e_dt_bias, e_o_norm, e_ln_g, e_ln_b, e_w_s, e_b_s, e_w_out, o_norm, o_pw1, o_pw1_b, o_dw, o_dw_b, o_ln_g, o_ln_b, o_pw2, o_pw2_b, f_norm, f_w1, f_w2, final_norm, loss_target, m_e_norm, m_e_w_in, m_e_conv_w, m_e_a_log, m_e_dt_bias, m_e_o_norm, m_e_ln_g, m_e_ln_b, m_e_w_s, m_e_b_s, m_e_w_out, m_o_norm, m_o_pw1, m_o_pw1_b, m_o_dw, m_o_dw_b, m_o_ln_g, m_o_ln_b, m_o_pw2, m_o_pw2_b, m_f_norm, m_f_w1, m_f_w2, m_final_norm, v_e_norm, v_e_w_in, v_e_conv_w, v_e_a_log, v_e_dt_bias, v_e_o_norm, v_e_ln_g, v_e_ln_b, v_e_w_s, v_e_b_s, v_e_w_out, v_o_norm, v_o_pw1, v_o_pw1_b, v_o_dw, v_o_dw_b, v_o_ln_g, v_o_ln_b, v_o_pw2, v_o_pw2_b, v_f_norm, v_f_w1, v_f_w2, v_final_norm):
    names = ['e_norm', 'e_w_in', 'e_conv_w', 'e_a_log', 'e_dt_bias', 'e_o_norm', 'e_ln_g', 'e_ln_b', 'e_w_s', 'e_b_s', 'e_w_out', 'o_norm', 'o_pw1', 'o_pw1_b', 'o_dw', 'o_dw_b', 'o_ln_g', 'o_ln_b', 'o_pw2', 'o_pw2_b', 'f_norm', 'f_w1', 'f_w2', 'final_norm']
    W = dict(zip(names, [e_norm, e_w_in, e_conv_w, e_a_log, e_dt_bias, e_o_norm, e_ln_g, e_ln_b, e_w_s, e_b_s, e_w_out, o_norm, o_pw1, o_pw1_b, o_dw, o_dw_b, o_ln_g, o_ln_b, o_pw2, o_pw2_b, f_norm, f_w1, f_w2, final_norm]))
    Mo = dict(zip(names, [m_e_norm, m_e_w_in, m_e_conv_w, m_e_a_log, m_e_dt_bias, m_e_o_norm, m_e_ln_g, m_e_ln_b, m_e_w_s, m_e_b_s, m_e_w_out, m_o_norm, m_o_pw1, m_o_pw1_b, m_o_dw, m_o_dw_b, m_o_ln_g, m_o_ln_b, m_o_pw2, m_o_pw2_b, m_f_norm, m_f_w1, m_f_w2, m_final_norm]))
    Vo = dict(zip(names, [v_e_norm, v_e_w_in, v_e_conv_w, v_e_a_log, v_e_dt_bias, v_e_o_norm, v_e_ln_g, v_e_ln_b, v_e_w_s, v_e_b_s, v_e_w_out, v_o_norm, v_o_pw1, v_o_pw1_b, v_o_dw, v_o_dw_b, v_o_ln_g, v_o_ln_b, v_o_pw2, v_o_pw2_b, v_f_norm, v_f_w1, v_f_w2, v_final_norm]))

    T, D = x.shape[1], x.shape[2]
    H = e_a_log.shape[-1]
    dv = e_o_norm.shape[-1]
    dk = dv
    G = e_w_s.shape[1]
    AQK, AV, BW = H * dk, H * dv, e_ln_g.shape[-1]
    AQKV = 2 * AQK + AV
    in_cols = AQKV + AV + 2 * H + 2 * BW
    KA = e_conv_w.shape[1]
    KC = o_dw.shape[1]
    L = f_norm.shape[0]
    dev = 4 * lax.axis_index("x") + 2 * lax.axis_index("y") + lax.axis_index("c")
    x2d = x.reshape(T, D)
    tgt = loss_target.reshape(T, D)

    dev_sp = dev.astype(jnp.int32).reshape(1)
    where = jnp.stack([lax.axis_index("c"), 2 * lax.axis_index("x") + lax.axis_index("y")]).astype(jnp.int32)
    row = lambda a: a.reshape(1, -1).astype(F32)
    en_row = row(e_norm)

    def gather_begin(tag, shards, after):
        na = len(shards)
        lands = [lax.empty((N_DEV,) + s.shape, s.dtype) for s in shards]
        ss, rs, bufs, tok = split_start(f"gather{tag}_ici_start", shards + lands, plan_gather_ici(na), 4 * na, after)
        return (na, ss, rs, bufs), tok

    def gather_pass(tag, h, after):
        na, ss, rs, bufs = h
        bufs, tok = split_wait(f"gather{tag}_ici_wait", ss, rs, bufs, plan_gather_ici(na), after)
        ss, rs, lands, tok = split_start(f"gather{tag}_pass_start", bufs[na:], plan_gather_pass(na), 3 * na, tok)
        return (na, ss, rs, bufs[:na], lands), tok

    def gather_end(tag, h, after):
        na, ss, rs, shards, lands = h
        lands, _ = split_wait(f"gather{tag}_pass_wait", ss, rs, lands, plan_gather_pass(na), after)
        return [place_own(f"gather{tag}_own{a}", lands[a], shards[a], dev_sp) for a in range(na)]

    bfw = lambda w: w.astype(BF)
    hA0, tok = gather_begin("0", [bfw(e_w_in[0])], en_row)
    hA1, tok = gather_begin("1", [bfw(e_w_out[0]), bfw(f_w1[0]), bfw(f_w2[0])], tok)
    hA2, tok = gather_begin("2", [bfw(o_pw1[0]), bfw(o_pw2[0]), bfw(f_w1[1]), bfw(f_w2[1])], tok)
    h0 = rowcall("rms_e", rms_tile, [x2d], [en_row], [(D, BF)], after=tok)
    hB0, tok = gather_pass("0", hA0, h0)

    small_sharded = ['e_conv_w', 'o_norm', 'o_pw1_b', 'o_dw', 'o_dw_b', 'o_ln_g', 'o_ln_b', 'o_pw2_b']
    sm = all_gather("gather_small", [pack([W[n][0]]) for n in small_sharded])
    full = {}
    for n, g in zip(small_sharded, sm):
        shp = W[n][0].shape
        blocks = [unpack(g[d], [shp])[0] for d in range(N_DEV)]
        full[n] = jnp.concatenate(blocks, axis=-1)
    conv_w = full['e_conv_w']
    on_row, pw1_b_row = row(full['o_norm']), row(full['o_pw1_b'])
    dw_w, dw_b_row = full['o_dw'], row(full['o_dw_b'])
    oln_g_row, oln_b_row, pw2_b_row = row(full['o_ln_g']), row(full['o_ln_b']), row(full['o_pw2_b'])

    (g_win,) = gather_end("0", hB0, sm[0])
    win = jnp.moveaxis(g_win, 0, 1).reshape(D, in_cols)
    w_qkv, w_z = win[:, :AQKV], win[:, AQKV:AQKV + AV]
    w_ba = jnp.pad(win[:, AQKV + AV:AQKV + AV + 2 * H], ((0, 0), (0, LANE - 2 * H)))
    w_uv = win[:, AQKV + AV + 2 * H:]

    alog_row = jnp.pad(row(e_a_log), ((0, 0), (H, LANE - 2 * H)))
    dtb_row = jnp.pad(row(e_dt_bias), ((0, 0), (H, LANE - 2 * H)))
    eon_row = row(e_o_norm)
    eln_g_row, eln_b_row = row(e_ln_g), row(e_ln_b)
    w_s = e_w_s[0]
    bs_t = e_b_s[0].T
    fn_rows = [row(f_norm[l]) for l in range(L)]
    fin_row = row(final_norm)

    qkv_raw = matmul("proj_qkv", h0, w_qkv, "nn")
    z_gate = matmul("proj_z", h0, w_z, "nn")
    ba = matmul("proj_ba", h0, w_ba, "nn")
    uv = matmul("proj_uv", h0, w_uv, "nn")

    cwa = min(512, AQKV)
    qkv_post = make_qkv_post(dk, cwa, 2 * AQK // cwa)
    ident = lambda t: t
    qkv = conv_fwd("qkv_conv", qkv_raw, conv_w, [], ident, qkv_post, [0], 1, KA, cw=cwa)
    qn, kn, vv = qkv[:, :AQK], qkv[:, AQK:2 * AQK], qkv[:, 2 * AQK:]
    bgc_fn = make_bgc(H)
    bgc = rowcall("bgc", bgc_fn, [ba], [alog_row, dtb_row], [(LANE, F32)])
    o_dn, xinv, states = deltanet_fwd(qn, kn, vv, bgc, H)
    hB1, tok = gather_pass("1", hA1, o_dn)
    out_a = rowcall("mixa_post", mixa_post_tile, [o_dn, z_gate], [eon_row], [(AV, BF)], after=tok)
    out_b = rowcall("mixb", mixb_tile, [uv], [eln_g_row, eln_b_row, w_s, bs_t], [(BW, BF)])
    g_wout, g_w1_0, g_w2_0 = gather_end("1", hB1, out_b)
    wout = g_wout.reshape(-1, D)
    w1 = [jnp.moveaxis(g_w1_0, 0, 1).reshape(D, -1), None]
    w2 = [g_w2_0.reshape(-1, D), None]
    mix = jnp.concatenate([out_a, out_b], axis=-1)
    add_epi = lambda acc, r: (acc + r,)
    x1 = matmul("out_proj", mix, wout, "nn", epi=add_epi, extras=[x2d])

    def relu2_epi(acc):
        r = jnp.maximum(acc, 0.0)
        return r * r, r

    hf0 = rms_fwd("rms_f0", x1, fn_rows[0])
    hB2, tok = gather_pass("2", hA2, hf0)
    a2_0, ar_0 = matmul("ffn_up0", hf0, w1[0], "nn", epi=relu2_epi, out_dtypes=(BF, BF), after=tok)
    x2 = matmul("ffn_down0", a2_0, w2[0], "nn", epi=add_epi, extras=[x1])
    ffn0 = (hf0, a2_0, ar_0)
    g_pw1, g_pw2, g_w1_1, g_w2_1 = gather_end("2", hB2, x2)
    pw1 = jnp.moveaxis(g_pw1, 0, 1).reshape(D, 2 * D)
    pw2 = g_pw2.reshape(D, D)
    w1[1] = jnp.moveaxis(g_w1_1, 0, 1).reshape(D, -1)
    w2[1] = g_w2_1.reshape(-1, D)

    h1 = rms_fwd("rms_o", x2, on_row)
    bias_epi = lambda acc, b: (acc + b,)
    zc = matmul("pw1", h1, pw1, "nn", epi=bias_epi, extras=[pw1_b_row])
    cwc = min(512, D)
    ncb = D // cwc
    cconv = conv_fwd("dw_conv", zc, dw_w, [dw_b_row], glu_pre, bias_post, [0, ncb], 1, KC, hb=32, cw=cwc)
    ln_silu = lambda c, g, b: jax.nn.silu(ln_tile(c, g, b))
    s_act = rowcall("ln_silu", ln_silu, [cconv], [oln_g_row, oln_b_row], [(D, BF)])
    x3 = matmul("pw2", s_act, pw2, "nn", epi=lambda acc, r, b: (r + (acc + b),), extras=[x2, pw2_b_row])
    hf1 = rms_fwd("rms_f1", x3, fn_rows[1])
    a2_1, ar_1 = matmul("ffn_up1", hf1, w1[1], "nn", epi=relu2_epi, out_dtypes=(BF, BF))
    x4 = matmul("ffn_down1", a2_1, w2[1], "nn", epi=add_epi, extras=[x3])
    ffn1 = (hf1, a2_1, ar_1)

    def loss_bwd_tile(x_, t_, g_):
        l, vjp = jax.vjp(lambda a, b: loss_tile(a, b, t_), x_, g_)
        dx, dg = vjp(jnp.ones_like(l))
        return dx, l, dg
    dx4, loss_part, d_final = rowcall("loss_head", loss_bwd_tile, [x4, tgt], [fin_row],
                                      [(D, F32)], [(1, 1), (1, D)])
    loss = lax.psum(loss_part[0, 0], ("x", "y", "c"))

    def reduce_begin(tag, grads, after):
        na = len(grads)
        lands = [lax.empty((4,) + g.shape[1:], g.dtype) for g in grads]
        ss, rs, bufs, tok = split_start(f"reduce{tag}_d2d_start", grads + lands, plan_reduce_d2d(na), 4 * na, after)
        return (na, ss, rs, bufs), tok

    def reduce_mid(tag, h, after):
        na, ss, rs, bufs = h
        bufs, tok = split_wait(f"reduce{tag}_d2d_wait", ss, rs, bufs, plan_reduce_d2d(na), after)
        parts = []
        for a, (g, rc) in enumerate(zip(bufs[:na], bufs[na:])):
            r, c = g.shape[1], g.shape[2]
            mine = lambda i, n, s: (2 * (i // (n // 4)) + s[0]) * (n // 4) + i % (n // 4)
            parts.append(rowcall(f"chip_sum{tag}_{a}", lambda p, q: (p + q,),
                                 [(g.reshape(N_DEV * r, c), mine), rc.reshape(4 * r, c)], [],
                                 [(c, BF)], tr=_tile(r, 512), sp=where, R=4 * r).reshape(4, r, c))
        lands = [lax.empty((3,) + p.shape[1:], p.dtype) for p in parts]
        ss, rs, bufs, tok = split_start(f"reduce{tag}_ici_start", parts + lands, plan_reduce_ici(na), 3 * na, tok)
        return (na, ss, rs, bufs), tok

    res = {}

    def reduce_end(tag, h, after, targets):
        na, ss, rs, bufs = h
        bufs, _ = split_wait(f"reduce{tag}_ici_wait", ss, rs, bufs, plan_reduce_ici(na), after)
        for a, (part, fin, (n, l)) in enumerate(zip(bufs[:na], bufs[na:], targets)):
            def f(p0, p1, p2, p3, w_, m_, v_):
                g = ((p0.astype(F32) + p1.astype(F32)) + p2.astype(F32)) + p3.astype(F32)
                return (g,) + adamw_tile(w_, g, m_, v_)
            r, C = fin.shape[-2], fin.shape[-1]
            own = lambda i, n_, s: s[1] * n_ + i
            out = rowcall(f"adam{tag}_{a}", f, [(part.reshape(4 * r, C), own), (fin, 0), (fin, 1), (fin, 2),
                                                (W[n], l), (Mo[n], l), (Vo[n], l)], [],
                          [(C, F32)] * 4, tr=256, sp=where, R=r)
            res.setdefault(n, {})[l] = out

    def tie(small, tok):
        return small + tok[0:1, 0:1]

    dscale_epi = lambda acc, r: (acc * (2.0 * r.astype(F32)),)
    d_fnorm = [None] * L

    dpre1 = matmul("ffn_down_dx1", dx4, w2[1], "nt", epi=dscale_epi, extras=[ar_1], out_dtypes=(BF,))
    dw2_1 = matmul("ffn_down_dw1", a2_1, dx4, "tn")
    dw1_1 = matmul("ffn_up_dw1", hf1, dpre1, "tn", colshard=True)
    hD1, tok = reduce_begin("1", [dw1_1, dw2_1.reshape(N_DEV, -1, D)], dpre1)
    dhf1 = matmul("ffn_up_dx1", dpre1, w1[1], "nt", after=tok)
    dx3, d_fnorm[1] = rms_bwd("rms_f_bwd1", x3, fn_rows[1], dhf1, dx4)

    ds_act = matmul("pw2_dx", dx3, pw2, "nt")
    hI1, tok = reduce_mid("1", hD1, ds_act)
    d_pw2 = matmul("pw2_dw", s_act, dx3, "tn", after=tok)
    ln_silu_bwd = _vjp_rows(ln_silu, 1, 1)

    def ln_silu_bwd_tile(c_, ds_, dx3_, g_, b_):
        dc, dg, db = ln_silu_bwd(c_, ds_, g_, b_)
        return dc, dg, db, jnp.sum(dx3_, axis=0, keepdims=True)
    dcconv, d_oln_g, d_oln_b, d_pw2_b = rowcall(
        "ln_silu_bwd", ln_silu_bwd_tile, [cconv, ds_act, dx3], [oln_g_row, oln_b_row],
        [(D, F32)], [(1, D), (1, D), (1, D)])
    (dza, dzb), (sza, szb), d_dw, (d_dw_b,) = conv_bwd(
        "dw_conv_bwd", zc, dw_w, [tie(dw_b_row, tok)], [dcconv], glu_pre, bias_post, [0, ncb], KC, hb=32, cw=cwc, tr=128)
    dzc = jnp.concatenate([dza, dzb], axis=-1)
    d_pw1_b = jnp.concatenate([sza, szb], axis=-1)
    d_pw1 = matmul("pw1_dw", h1, dzc, "tn", colshard=True)
    dh1 = matmul("pw1_dx", dzc, pw1, "nt")
    dx2, d_onorm = rms_bwd("rms_o_bwd", x2, on_row, dh1, dx3)
    reduce_end("1", hI1, dx2, [('f_w1', 1), ('f_w2', 1)])

    hD2, tok = reduce_begin("2", [d_pw1, d_pw2.reshape(N_DEV, -1, D)], dx2)
    dpre0 = matmul("ffn_down_dx0", dx2, w2[0], "nt", epi=dscale_epi, extras=[ar_0], out_dtypes=(BF,), after=tok)
    dw2_0 = matmul("ffn_down_dw0", a2_0, dx2, "tn")
    hI2, tok = reduce_mid("2", hD2, dw2_0)
    dw1_0 = matmul("ffn_up_dw0", hf0, dpre0, "tn", colshard=True, after=tok)
    dhf0 = matmul("ffn_up_dx0", dpre0, w1[0], "nt")
    dx1, d_fnorm[0] = rms_bwd("rms_f_bwd0", x1, fn_rows[0], dhf0, dx2)
    reduce_end("2", hI2, dx1, [('o_pw1', 0), ('o_pw2', 0)])

    dmix = matmul("out_proj_dx", dx1, wout, "nt")
    d_wout = matmul("out_proj_dw", mix, dx1, "tn")
    hD3, tok = reduce_begin("3", [dw1_0, dw2_0.reshape(N_DEV, -1, D), d_wout.reshape(N_DEV, -1, D)], dmix)
    dmix_a, dmix_b = dmix[:, :AV], dmix[:, AV:]
    mixb_bwd = _vjp_rows(mixb_tile, 1, 1)
    duv, d_eln_g, d_eln_b, d_ws, d_bs_t = rowcall(
        "mixb_bwd", mixb_bwd, [uv, dmix_b], [eln_g_row, eln_b_row, w_s, bs_t],
        [(2 * BW, BF)], [(1, BW), (1, BW), w_s.shape, bs_t.shape], after=tok)
    mixa_bwd = _vjp_rows(mixa_post_tile, 2, 1)
    do_dn, dz_gate, d_eon = rowcall(
        "mixa_post_bwd", mixa_bwd, [o_dn, z_gate, dmix_a], [eon_row],
        [(AV, F32), (AV, BF)], [(1, dv)])
    hI3, tok = reduce_mid("3", hD3, do_dn)
    dq, dk_, dv_, dbgc = deltanet_bwd(qn, kn, vv, bgc, xinv, states, do_dn, H)
    bgc_bwd = _vjp_rows(bgc_fn, 1, 1)
    dba, d_alog_row, d_dtb_row = rowcall(
        "bgc_bwd", bgc_bwd, [ba, dbgc], [alog_row, dtb_row], [(LANE, BF)], [(1, LANE), (1, LANE)], after=tok)
    dqkv = jnp.concatenate([dq, dk_, dv_], axis=-1)
    (dqkv_raw,), _, d_conv_w, _ = conv_bwd(
        "qkv_conv_bwd", qkv_raw, conv_w, [], [dqkv], ident, qkv_post, [0], KA, cw=cwa)
    reduce_end("3", hI3, dqkv_raw, [('f_w1', 0), ('f_w2', 0), ('e_w_out', 0)])

    dw_qkv = matmul("proj_qkv_dw", h0, dqkv_raw, "tn")
    dw_z = matmul("proj_z_dw", h0, dz_gate, "tn")
    dw_ba = matmul("proj_ba_dw", h0, dba, "tn")
    dw_uv = matmul("proj_uv_dw", h0, duv, "tn")
    d_win = jnp.concatenate([dw_qkv, dw_z, dw_ba[:, :2 * H], dw_uv], axis=-1)
    G_win = jnp.moveaxis(d_win.reshape(D, N_DEV, in_cols // N_DEV), 1, 0)
    hD4, tok = reduce_begin("4", [G_win], dw_uv)
    dh0 = matmul("proj_qkv_dx", dqkv_raw, w_qkv, "nt", after=tok)
    dh0 = matmul("proj_z_dx", dz_gate, w_z, "nt", epi=add_epi, extras=[dh0])
    hI4, tok = reduce_mid("4", hD4, dh0)
    dh0 = matmul("proj_ba_dx", dba, w_ba, "nt", epi=add_epi, extras=[dh0], after=tok)
    dh0 = matmul("proj_uv_dx", duv, w_uv, "nt", epi=add_epi, extras=[dh0])
    grad_x, d_enorm = rms_bwd("rms_e_bwd", x2d, en_row, dh0, dx1)

    d_alog = d_alog_row[:, H:2 * H]
    d_dtb = d_dtb_row[:, H:2 * H]
    small_names = ['e_norm', 'e_conv_w', 'e_a_log', 'e_dt_bias', 'e_o_norm', 'e_ln_g', 'e_ln_b', 'e_w_s', 'e_b_s',
                   'o_norm', 'o_pw1_b', 'o_dw', 'o_dw_b', 'o_ln_g', 'o_ln_b', 'o_pw2_b', 'f_norm', 'final_norm']
    small_grads = [d_enorm, d_conv_w, d_alog, d_dtb, d_eon, d_eln_g, d_eln_b, d_ws, d_bs_t.T,
                   d_onorm, d_pw1_b, d_dw, d_dw_b, d_oln_g, d_oln_b, d_pw2_b,
                   jnp.concatenate(d_fnorm, axis=0), d_final]
    full_shapes = [g.shape for g in small_grads]
    gs_all = all_gather("gather_small_grads", [pack(small_grads, 256)])[0]

    def sum8(*ps):
        s = ps[0]
        for p in ps[1:]:
            s = s + p
        return (s,)
    gs_sum = rowcall("small_sum", sum8, [(gs_all, d) for d in range(N_DEV)], [], [(LANE, F32)])
    reduce_end("4", hI4, gs_sum, [('e_w_in', 0)])
    for n in list(res):
        res[n] = tuple(jnp.stack([res[n][l][k] for l in sorted(res[n])]) for k in range(4))
    g_full = dict(zip(small_names, unpack(gs_sum, full_shapes)))
    g_loc = {}
    for n in small_names:
        g = g_full[n]
        if n in small_sharded:
            per = g.shape[-1] // N_DEV
            g = lax.dynamic_slice_in_dim(g, dev * per, per, axis=-1)
        g_loc[n] = g.reshape(W[n].shape)
    packs = [pack([A[n] for n in small_names], 256) for A in (W, g_loc, Mo, Vo)]
    d_s, m_s, v_s = rowcall("adam_small", adamw_tile, packs, [], [(LANE, F32)] * 3)
    shapes = [W[n].shape for n in small_names]
    for n, d_, m_, v_ in zip(small_names, unpack(d_s, shapes), unpack(m_s, shapes), unpack(v_s, shapes)):
        res[n] = (g_loc[n], d_, m_, v_)

    grads = [res[n][0] for n in names]
    deltas = [res[n][1] for n in names]
    new_m = [res[n][2] for n in names]
    new_v = [res[n][3] for n in names]
    return (loss, grad_x.reshape(x.shape), *grads, *deltas, *new_m, *new_v)
```

```python
import functools
import math

import jax
import jax.numpy as jnp
import numpy as np
from jax import lax
from jax.experimental import pallas as pl
from jax.experimental.pallas import tpu as pltpu

F32 = jnp.float32
BF = jnp.bfloat16
EPS = 1e-6
CHUNK = 64
B_BLOCK = 128
LANE = 128
SUBLANE = 8
N_DEV = 8
VMEM_LIMIT = 56 * 1024 * 1024

ADAM_LR = 0.001
ADAM_B1 = 0.9
ADAM_B2 = 0.999
ADAM_EPS = 1e-08
ADAM_WD = 0.01
ADAM_STEP = 10

MESH = pl.DeviceIdType.MESH
ANY = pl.BlockSpec(memory_space=pl.ANY)


def _tile(n, pref, mult=SUBLANE):
    if n <= pref:
        return n
    t = (pref // mult) * mult
    while t >= mult:
        if n % t == 0:
            return t
        t -= mult
    return n


def _dg(a, b, ca, cb, hi):
    nb = a.ndim - 2
    batch = tuple(range(nb))
    dims = (((ca + nb,), (cb + nb,)), (batch, batch))
    if hi:
        return lax.dot_general(a.astype(F32), b.astype(F32), dims,
                               precision=lax.Precision.HIGHEST, preferred_element_type=F32)
    return lax.dot_general(a.astype(BF), b.astype(BF), dims, preferred_element_type=F32)


@functools.partial(jax.custom_vjp, nondiff_argnums=(2, 3, 4))
def mm(a, b, ca, cb, hi=False):
    return _dg(a, b, ca, cb, hi)


def _mm_fwd(a, b, ca, cb, hi):
    return _dg(a, b, ca, cb, hi), (a, b)


def _mm_bwd(ca, cb, hi, res, g):
    a, b = res
    if ca == 1:
        da = mm(g, b, 1, 1 - cb, hi)
    else:
        da = mm(b, g, 1 - cb, 1, hi)
    if cb == 0:
        db = mm(a, g, 1 - ca, 0, hi)
    else:
        db = mm(g, a, 0, 1 - ca, hi)
    return da.astype(a.dtype), db.astype(b.dtype)


mm.defvjp(_mm_fwd, _mm_bwd)


def matmul(name, a, b, mode, epi=None, extras=(), out_dtypes=(F32,), colshard=False,
           tm=512, tn=1024, tk=2048, after=None):
    afters = [] if after is None else [after]
    if mode == "nn":
        (M, K), (K2, N) = a.shape, b.shape
    elif mode == "nt":
        (M, K), (N, K2) = a.shape, b.shape
    else:
        (K, M), (K2, N) = a.shape, b.shape
    assert K == K2, (name, a.shape, b.shape, mode)
    tm = _tile(M, tm)
    tn = N // N_DEV if colshard else _tile(N, tn, LANE)
    tk = _tile(K, tk, LANE)
    nk = K // tk
    grid = (M // tm, N // tn, nk)
    if mode == "nn":
        a_spec = pl.BlockSpec((tm, tk), lambda i, j, k: (i, k))
        b_spec = pl.BlockSpec((tk, tn), lambda i, j, k: (k, j))
        ca, cb = 1, 0
    elif mode == "nt":
        a_spec = pl.BlockSpec((tm, tk), lambda i, j, k: (i, k))
        b_spec = pl.BlockSpec((tn, tk), lambda i, j, k: (j, k))
        ca, cb = 1, 1
    else:
        a_spec = pl.BlockSpec((tk, tm), lambda i, j, k: (k, i))
        b_spec = pl.BlockSpec((tk, tn), lambda i, j, k: (k, j))
        ca, cb = 0, 0
    ex_specs = []
    for e in extras:
        if e.shape[0] == 1:
            ex_specs.append(pl.BlockSpec((1, tn), lambda i, j, k: (0, j)))
        else:
            assert e.shape == (M, N), (name, e.shape)
            ex_specs.append(pl.BlockSpec((tm, tn), lambda i, j, k: (i, j)))
    if colshard:
        out_shape = [jax.ShapeDtypeStruct((N_DEV, M, tn), dt) for dt in out_dtypes]
        out_specs = [pl.BlockSpec((None, tm, tn), lambda i, j, k: (j, i, 0)) for _ in out_dtypes]
    else:
        out_shape = [jax.ShapeDtypeStruct((M, N), dt) for dt in out_dtypes]
        out_specs = [pl.BlockSpec((tm, tn), lambda i, j, k: (i, j)) for _ in out_dtypes]
    n_ex, n_out = len(extras), len(out_dtypes)

    def body(*refs):
        a_ref, b_ref = refs[0], refs[1]
        ex_refs = refs[2:2 + n_ex]
        first_out = 2 + n_ex + len(afters)
        o_refs = refs[first_out:first_out + n_out]
        acc_ref = refs[-1]
        k = pl.program_id(2)
        part = _dg(a_ref[...], b_ref[...], ca, cb, False)

        @pl.when(k == 0)
        def _():
            acc_ref[...] = part

        @pl.when(k > 0)
        def _():
            acc_ref[...] += part

        @pl.when(k == nk - 1)
        def _():
            acc = acc_ref[...]
            res = (acc,) if epi is None else epi(acc, *[r[...] for r in ex_refs])
            for o_ref, r in zip(o_refs, res):
                o_ref[...] = r.astype(o_ref.dtype)

    outs = pl.pallas_call(
        body, name=name, grid=grid,
        in_specs=[a_spec, b_spec] + ex_specs + [ANY] * len(afters),
        out_specs=out_specs, out_shape=out_shape,
        scratch_shapes=[pltpu.VMEM((tm, tn), F32)],
        compiler_params=pltpu.CompilerParams(
            dimension_semantics=("parallel", "parallel", "arbitrary"),
            vmem_limit_bytes=VMEM_LIMIT),
    )(a, b, *extras, *afters)
    return outs[0] if n_out == 1 else tuple(outs)


def rowcall(name, fn, rows, consts, out_rows, out_accs=(), tr=256, sp=None, R=None, after=None):
    afters = [] if after is None else [after]
    rows = [r if isinstance(r, tuple) else (r, None) for r in rows]
    R = rows[0][0].shape[-2] if R is None else R
    tr = _tile(R, tr)
    n = R // tr
    in_specs = []
    for arr, lead in rows:
        C = arr.shape[-1]
        if lead is None:
            assert arr.shape[-2] == R, (name, arr.shape, R)
            in_specs.append(pl.BlockSpec((tr, C), lambda i, *s: (i, 0)))
        elif callable(lead):
            in_specs.append(pl.BlockSpec((tr, C), lambda i, *s, lead=lead: (lead(i, n, *s), 0)))
        else:
            assert arr.shape[-2] == R, (name, arr.shape, R)
            in_specs.append(pl.BlockSpec((None, tr, C), lambda i, *s, lead=lead: (lead, i, 0)))
    for c in consts:
        in_specs.append(pl.BlockSpec(c.shape, lambda i, *s, nd=c.ndim: (0,) * nd))
    out_shape = [jax.ShapeDtypeStruct((R, C), dt) for C, dt in out_rows]
    out_specs = [pl.BlockSpec((tr, C), lambda i, *s: (i, 0)) for C, _ in out_rows]
    for shp in out_accs:
        out_shape.append(jax.ShapeDtypeStruct(shp, F32))
        out_specs.append(pl.BlockSpec(shp, lambda i, *s, nd=len(shp): (0,) * nd))
    n_in, n_row, n_acc = len(rows) + len(consts), len(out_rows), len(out_accs)
    n_sp = 0 if sp is None else 1

    def body(*refs):
        refs = refs[n_sp:]
        ins = [r[...] for r in refs[:n_in]]
        res = fn(*ins)
        if not isinstance(res, (tuple, list)):
            res = (res,)
        o_refs = refs[n_in + len(afters):]
        for o_ref, r in zip(o_refs[:n_row], res[:n_row]):
            o_ref[...] = r.astype(o_ref.dtype)
        if n_acc:
            first = pl.program_id(0) == 0
            for o_ref, r in zip(o_refs[n_row:], res[n_row:]):
                r = r.astype(F32).reshape(o_ref.shape)

                @pl.when(first)
                def _(o_ref=o_ref, r=r):
                    o_ref[...] = r

                @pl.when(jnp.logical_not(first))
                def _(o_ref=o_ref, r=r):
                    o_ref[...] += r

    params = pltpu.CompilerParams(dimension_semantics=("arbitrary",), vmem_limit_bytes=VMEM_LIMIT)
    operands = [a for a, _ in rows] + list(consts) + afters
    in_specs = in_specs + [ANY] * len(afters)
    if sp is None:
        outs = pl.pallas_call(body, name=name, grid=(n,), in_specs=in_specs, out_specs=out_specs,
                              out_shape=out_shape, compiler_params=params)(*operands)
    else:
        outs = pl.pallas_call(
            body, name=name, out_shape=out_shape, compiler_params=params,
            grid_spec=pltpu.PrefetchScalarGridSpec(
                num_scalar_prefetch=1, grid=(n,), in_specs=in_specs, out_specs=out_specs),
        )(sp, *operands)
    return outs[0] if len(outs) == 1 else tuple(outs)


def rms_tile(x, g):
    x = x.astype(F32)
    return x * lax.rsqrt(jnp.mean(x * x, axis=-1, keepdims=True) + EPS) * g


def gelu(x):
    return 0.5 * x * (1.0 + lax.erf(x * (1.0 / math.sqrt(2.0))))


def ln_tile(x, g, b):
    mu = jnp.mean(x, axis=-1, keepdims=True)
    xc = x - mu
    return xc * lax.rsqrt(jnp.mean(xc * xc, axis=-1, keepdims=True) + EPS) * g + b


def lane_groups(fn, width, *arrs):
    n = arrs[0].shape[-1] // width
    outs = [fn(*[a[:, i * width:(i + 1) * width] for a in arrs]) for i in range(n)]
    return jnp.concatenate(outs, axis=-1)


def mixa_post_tile(o, z, o_norm):
    dv = o_norm.shape[-1]
    on = lane_groups(lambda t: rms_tile(t, o_norm), dv, o)
    return on * jax.nn.silu(z)


def mixb_tile(uv, ln_g, ln_b, w_s, bs_t):
    G = w_s.shape[0]
    gw = ln_g.shape[-1]
    dg = gw // G
    tr = uv.shape[0]
    u = gelu(uv[:, :gw])
    vg = gelu(uv[:, gw:])
    ii = lax.broadcasted_iota(jnp.int32, (B_BLOCK, B_BLOCK), 0)
    jj = lax.broadcasted_iota(jnp.int32, (B_BLOCK, B_BLOCK), 1)
    mask = (jj // CHUNK) <= (ii // CHUNK)
    cols = []
    for g in range(G):
        sl = slice(g * dg, (g + 1) * dg)
        vn = ln_tile(vg[:, sl], ln_g[:, sl], ln_b[:, sl])
        wm = jnp.where(mask, w_s[g], 0.0)
        blocks = []
        for m in range(tr // B_BLOCK):
            blk = vn[m * B_BLOCK:(m + 1) * B_BLOCK, :]
            blocks.append(mm(wm, blk, 1, 0) + bs_t[:, g:g + 1])
        mixed = blocks[0] if len(blocks) == 1 else jnp.concatenate(blocks, axis=0)
        cols.append(u[:, sl] * mixed)
    return jnp.concatenate(cols, axis=-1)


def bgc_tile(ba, alog_row, dtb_row):
    tr = ba.shape[0]
    beta = jax.nn.sigmoid(ba)
    g = -jnp.exp(alog_row) * jax.nn.softplus(ba + dtb_row)
    ii = lax.broadcasted_iota(jnp.int32, (tr, tr), 0)
    jj = lax.broadcasted_iota(jnp.int32, (tr, tr), 1)
    tri = jnp.where((ii // CHUNK == jj // CHUNK) & (jj <= ii), 1.0, 0.0).astype(F32)
    gc = mm(tri, g, 1, 0, True)
    return beta, gc


def make_bgc(H):
    def f(ba, alog_row, dtb_row):
        beta, gc = bgc_tile(ba, alog_row, dtb_row)
        lane = lax.broadcasted_iota(jnp.int32, ba.shape, 1)
        return jnp.where(lane < H, beta, jnp.where(lane < 2 * H, gc, 0.0))
    return f


def loss_tile(x, g, target):
    y = rms_tile(x, g)
    err = y - target
    return 0.5 * jnp.sum(jnp.mean(err * err, axis=-1, keepdims=True), axis=0, keepdims=True)


def adamw_tile(w, g, m, v):
    m = ADAM_B1 * m + (1.0 - ADAM_B1) * g
    v = ADAM_B2 * v + (1.0 - ADAM_B2) * (g * g)
    m_hat = m / (1.0 - ADAM_B1 ** ADAM_STEP)
    v_hat = v / (1.0 - ADAM_B2 ** ADAM_STEP)
    delta = -ADAM_LR * (m_hat / (jnp.sqrt(v_hat) + ADAM_EPS) + ADAM_WD * w)
    return delta, m, v


def _conv_specs(T, tr, hb, cw, col_blocks, rev):
    n = T // tr

    def ri(i):
        return (n - 1 - i) if rev else i

    tile_specs, halo_specs = [], []
    for off in col_blocks:
        tile_specs.append(pl.BlockSpec((tr, cw), lambda j, i, off=off: (ri(i), j + off)))
        halo_specs.append(pl.BlockSpec(
            (hb, cw), lambda j, i, off=off: (jnp.maximum(ri(i) * (tr // hb) - 1, 0), j + off)))
    return n, ri, tile_specs, halo_specs


def conv_fwd(name, x, w, consts, pre, post, col_blocks, n_out, K, out_dtype=F32, tr=256, hb=8, cw=512):
    T = x.shape[0]
    C = w.shape[1]
    tr, cw = _tile(T, tr, hb), min(cw, C)
    nb = len(col_blocks)
    n, ri, tile_specs, halo_specs = _conv_specs(T, tr, hb, cw, col_blocks, False)
    w_spec = pl.BlockSpec((K, cw), lambda j, i: (0, j))
    c_specs = [pl.BlockSpec((1, cw), lambda j, i: (0, j)) for _ in consts]

    def body(*refs):
        tiles = [r[...] for r in refs[:nb]]
        halos = [r[...] for r in refs[nb:2 * nb]]
        w_ref = refs[2 * nb]
        cs = [r[...] for r in refs[2 * nb + 1:2 * nb + 1 + len(consts)]]
        o_refs = refs[2 * nb + 1 + len(consts):-1]
        pbuf = refs[-1]
        i = pl.program_id(1)
        pbuf[0:hb, :] = jnp.where(i > 0, pre(*halos), 0.0)
        pbuf[hb:hb + tr, :] = pre(*tiles)
        c = jnp.zeros((tr, cw), F32)
        for k in range(K):
            c = c + w_ref[k:k + 1, :] * pbuf[pl.ds(hb - (K - 1) + k, tr), :]
        res = post(c, pl.program_id(0), *cs)
        for o_ref, r in zip(o_refs, res):
            o_ref[...] = r.astype(o_ref.dtype)

    outs = pl.pallas_call(
        body, name=name, grid=(C // cw, n),
        in_specs=tile_specs + halo_specs + [w_spec] + c_specs,
        out_specs=[pl.BlockSpec((tr, cw), lambda j, i: (i, j)) for _ in range(n_out)],
        out_shape=[jax.ShapeDtypeStruct((T, C), out_dtype) for _ in range(n_out)],
        scratch_shapes=[pltpu.VMEM((hb + tr, cw), F32)],
        compiler_params=pltpu.CompilerParams(
            dimension_semantics=("parallel", "arbitrary"), vmem_limit_bytes=VMEM_LIMIT),
    )(*([x] * nb), *([x] * nb), w, *consts)
    return outs[0] if n_out == 1 else tuple(outs)


def conv_bwd(name, x, w, consts, grads, pre, post, col_blocks, K, tr=256, hb=8, cw=512):
    T, Cx = x.shape
    C = w.shape[1]
    tr, cw = _tile(T, tr, hb), min(cw, C)
    nb = len(col_blocks)
    n, ri, tile_specs, halo_specs = _conv_specs(T, tr, hb, cw, col_blocks, True)
    w_spec = pl.BlockSpec((K, cw), lambda j, i: (0, j))
    c_specs = [pl.BlockSpec((1, cw), lambda j, i: (0, j)) for _ in consts]
    g_specs = [pl.BlockSpec((tr, cw), lambda j, i: (ri(i), j)) for _ in grads]
    nc, ng = len(consts), len(grads)

    def body(*refs):
        p = 0
        tile_refs = refs[p:p + nb]; p += nb
        halo_refs = refs[p:p + nb]; p += nb
        w_ref = refs[p]; p += 1
        cs = [r[...] for r in refs[p:p + nc]]; p += nc
        gs = [r[...] for r in refs[p:p + ng]]; p += ng
        dx_refs = refs[p:p + nb]; p += nb
        sum_refs = refs[p:p + nb]; p += nb
        dw_ref = refs[p]; p += 1
        dc_refs = refs[p:p + nc]; p += nc
        pbuf, dbuf, ebuf, carry = refs[p:p + 4]
        i = pl.program_id(1)
        first = i == 0
        tiles = [r[...] for r in tile_refs]
        halos = [r[...] for r in halo_refs]
        p_tile, vjp_pre = jax.vjp(pre, *tiles)
        pbuf[0:hb, :] = jnp.where(ri(i) > 0, pre(*halos), 0.0)
        pbuf[hb:hb + tr, :] = p_tile
        c = jnp.zeros((tr, cw), F32)
        for k in range(K):
            c = c + w_ref[k:k + 1, :] * pbuf[pl.ds(hb - (K - 1) + k, tr), :]
        cid = pl.program_id(0)
        _, vjp_post = jax.vjp(lambda c_, *cs_: post(c_, cid, *cs_), c, *cs)
        dres = vjp_post(tuple(g.astype(F32) for g in gs))
        dc = dres[0]
        dbuf[0:hb, :] = jnp.zeros((hb, cw), F32)
        dbuf[hb:hb + tr, :] = dc
        dbuf[hb + tr:hb + tr + hb, :] = jnp.zeros((hb, cw), F32)
        dp = jnp.zeros((hb + tr, cw), F32)
        dws = []
        for k in range(K):
            dp = dp + w_ref[k:k + 1, :] * dbuf[pl.ds(K - 1 - k, hb + tr), :]
            dws.append(jnp.sum(dc * pbuf[pl.ds(hb - (K - 1) + k, tr), :], axis=0, keepdims=True))
        dw = jnp.concatenate(dws, axis=0)
        ebuf[...] = dp

        @pl.when(jnp.logical_not(first))
        def _():
            ebuf[tr:tr + hb, :] += carry[...]

        carry[...] = ebuf[0:hb, :]
        dtiles = vjp_pre(ebuf[hb:hb + tr, :])
        for r, s, d in zip(dx_refs, sum_refs, dtiles):
            r[...] = d.astype(r.dtype)
            ds_ = jnp.sum(d, axis=0, keepdims=True)

            @pl.when(first)
            def _(s=s, ds_=ds_):
                s[...] = ds_

            @pl.when(jnp.logical_not(first))
            def _(s=s, ds_=ds_):
                s[...] += ds_

        accs = [(dw_ref, dw)] + [(r, d) for r, d in zip(dc_refs, dres[1:])]
        for r, d in accs:
            @pl.when(first)
            def _(r=r, d=d):
                r[...] = d

            @pl.when(jnp.logical_not(first))
            def _(r=r, d=d):
                r[...] += d

    n_cb = C // cw
    outs = pl.pallas_call(
        body, name=name, grid=(n_cb, n),
        in_specs=tile_specs + halo_specs + [w_spec] + c_specs + g_specs,
        out_specs=([pl.BlockSpec((tr, cw), lambda j, i: (ri(i), j)) for _ in col_blocks]
                   + [pl.BlockSpec((1, cw), lambda j, i: (0, j)) for _ in col_blocks]
                   + [pl.BlockSpec((K, cw), lambda j, i: (0, j))]
                   + [pl.BlockSpec((1, cw), lambda j, i: (0, j)) for _ in consts]),
        out_shape=([jax.ShapeDtypeStruct((T, C), BF) for _ in col_blocks]
                   + [jax.ShapeDtypeStruct((1, C), F32) for _ in col_blocks]
                   + [jax.ShapeDtypeStruct((K, C), F32)]
                   + [jax.ShapeDtypeStruct((1, C), F32) for _ in consts]),
        scratch_shapes=[pltpu.VMEM((hb + tr, cw), F32), pltpu.VMEM((hb + tr + hb, cw), F32),
                        pltpu.VMEM((hb + tr, cw), F32), pltpu.VMEM((hb, cw), F32)],
        compiler_params=pltpu.CompilerParams(
            dimension_semantics=("parallel", "arbitrary"), vmem_limit_bytes=VMEM_LIMIT),
    )(*([x] * nb), *([x] * nb), w, *consts, *grads)
    dxs = outs[:nb]
    sums = outs[nb:2 * nb]
    dw = outs[2 * nb]
    dcs = outs[2 * nb + 1:]
    return dxs, sums, dw, dcs


def make_qkv_post(dk, cw, n_qk_chunks):
    def l2(t):
        return t * lax.rsqrt(jnp.sum(t * t, axis=-1, keepdims=True) + EPS)

    def post(c, cid):
        s = jax.nn.silu(c)
        normed = lane_groups(l2, dk, s)
        return (jnp.where(cid < n_qk_chunks, normed, s),)
    return post


def glu_pre(za, zb):
    return za * jax.nn.sigmoid(zb)


def bias_post(c, cid, b):
    return (c + b,)


def _col_to_row(col):
    C = col.shape[-2]
    ii = lax.broadcasted_iota(jnp.int32, (C, C), 0)
    jj = lax.broadcasted_iota(jnp.int32, (C, C), 1)
    wide = jnp.broadcast_to(col, col.shape[:-1] + (C,))
    return jnp.sum(jnp.where(ii == jj, wide, 0.0), axis=-2, keepdims=True)


@jax.custom_vjp
def solve_with_inverse(a, rhs, x):
    return mm(x, rhs, 1, 0, True)


def _swi_fwd(a, rhs, x):
    sol = mm(x, rhs, 1, 0, True)
    return sol, (x, sol)


def _swi_bwd(res, dsol):
    x, sol = res
    drhs = mm(x, dsol, 0, 0, True)
    da = -mm(drhs, sol, 1, 1, True)
    return da, drhs, jnp.zeros_like(x)


solve_with_inverse.defvjp(_swi_fwd, _swi_bwd)


def unit_lower_inverse(a):
    C = a.shape[-1]
    ii = lax.broadcasted_iota(jnp.int32, (C, C), 0)
    jj = lax.broadcasted_iota(jnp.int32, (C, C), 1)
    x = jnp.where(ii == jj, 1.0, 0.0).astype(F32) - a
    p = mm(a, a, 1, 0, True)
    steps = int(math.log2(C)) - 1
    for s in range(steps):
        x = x + mm(x, p, 1, 0, True)
        if s < steps - 1:
            p = mm(p, p, 1, 0, True)
    return x


def dn_masks(C):
    ii = lax.broadcasted_iota(jnp.int32, (C, C), 0)
    jj = lax.broadcasted_iota(jnp.int32, (C, C), 1)
    return ii >= jj, ii > jj


def dn_pre(q, k, v, beta, gc):
    C, dk = q.shape[-2:]
    tri, strict = dn_masks(C)
    q = q * (dk ** -0.5)
    diff = gc - _col_to_row(gc)
    decay = jnp.where(tri, jnp.exp(jnp.where(tri, diff, 0.0)), 0.0)
    kb = k * beta
    vb = v * beta
    a = jnp.where(strict, mm(kb, k, 1, 1) * decay, 0.0)
    eg = jnp.exp(gc)
    rhs = jnp.concatenate([vb, kb * eg], axis=-1)
    attn = mm(q, k, 1, 1) * decay
    qd = q * eg
    g_last = gc[..., C - 1:C, :]
    kt = k * jnp.exp(g_last - gc)
    gl = jnp.exp(g_last)
    return a, rhs, attn, qd, kt, gl


def dn_chunk(q, k, v, beta, gc, state, x):
    dv = v.shape[-1]
    a, rhs, attn, qd, kt, gl = dn_pre(q, k, v, beta, gc)
    sol = solve_with_inverse(a, rhs, x)
    u, w = sol[..., :dv], sol[..., dv:]
    vn = u - mm(w, state, 1, 0)
    o = mm(qd, state, 1, 0) + mm(attn, vn, 1, 0)
    new_state = state * gl + mm(kt, vn, 0, 0)
    return o, new_state


def _by_head(q_ref, k_ref, v_ref, bg, H, dk, dv):
    qv = jnp.stack([q_ref[:, h * dk:(h + 1) * dk] for h in range(H)])
    kv = jnp.stack([k_ref[:, h * dk:(h + 1) * dk] for h in range(H)])
    vv = jnp.stack([v_ref[:, h * dv:(h + 1) * dv] for h in range(H)])
    beta = jnp.stack([bg[:, h:h + 1] for h in range(H)])
    gc = jnp.stack([bg[:, H + h:H + h + 1] for h in range(H)])
    return qv, kv, vv, beta, gc


def deltanet_fwd(q, k, v, bgc, H):
    T = q.shape[0]
    dk, dv = q.shape[1] // H, v.shape[1] // H
    N = T // CHUNK

    def body(q_ref, k_ref, v_ref, bgc_ref, o_ref, x_ref, s_ref, state):
        @pl.when(pl.program_id(0) == 0)
        def _():
            state[...] = jnp.zeros((H, dk, dv), F32)

        qv, kv, vv, beta, gc = _by_head(q_ref, k_ref, v_ref, bgc_ref[...], H, dk, dv)
        a = dn_pre(qv, kv, vv, beta, gc)[0]
        x = unit_lower_inverse(a)
        s = state[...]
        o, s_new = dn_chunk(qv, kv, vv, beta, gc, s, x)
        for h in range(H):
            o_ref[:, h * dv:(h + 1) * dv] = o[h]
        x_ref[...] = x
        s_ref[...] = s
        state[...] = s_new

    return pl.pallas_call(
        body, name="deltanet_fwd", grid=(N,),
        in_specs=[pl.BlockSpec((CHUNK, H * dk), lambda n: (n, 0)),
                  pl.BlockSpec((CHUNK, H * dk), lambda n: (n, 0)),
                  pl.BlockSpec((CHUNK, H * dv), lambda n: (n, 0)),
                  pl.BlockSpec((CHUNK, LANE), lambda n: (n, 0))],
        out_specs=[pl.BlockSpec((CHUNK, H * dv), lambda n: (n, 0)),
                   pl.BlockSpec((None, H, CHUNK, CHUNK), lambda n: (n, 0, 0, 0)),
                   pl.BlockSpec((None, H, dk, dv), lambda n: (n, 0, 0, 0))],
        out_shape=[jax.ShapeDtypeStruct((T, H * dv), F32),
                   jax.ShapeDtypeStruct((N, H, CHUNK, CHUNK), F32),
                   jax.ShapeDtypeStruct((N, H, dk, dv), F32)],
        scratch_shapes=[pltpu.VMEM((H, dk, dv), F32)],
        compiler_params=pltpu.CompilerParams(
            dimension_semantics=("arbitrary",), vmem_limit_bytes=VMEM_LIMIT),
    )(q, k, v, bgc)


def deltanet_bwd(q, k, v, bgc, xinv, states, do, H):
    T = q.shape[0]
    dk, dv = q.shape[1] // H, v.shape[1] // H
    N = T // CHUNK

    def body(q_ref, k_ref, v_ref, bgc_ref, x_ref, s_ref, do_ref,
             dq_ref, dk_ref, dv_ref, dbgc_ref, dstate):
        @pl.when(pl.program_id(0) == 0)
        def _():
            dstate[...] = jnp.zeros((H, dk, dv), F32)

        qv, kv, vv, beta, gc = _by_head(q_ref, k_ref, v_ref, bgc_ref[...], H, dk, dv)
        do = jnp.stack([do_ref[:, h * dv:(h + 1) * dv] for h in range(H)])
        _, vjp = jax.vjp(dn_chunk, qv, kv, vv, beta, gc, s_ref[...], x_ref[...])
        dq, dk_, dv_, dbeta, dgc, ds, _ = vjp((do, dstate[...]))
        dstate[...] = ds
        lane = lax.broadcasted_iota(jnp.int32, (CHUNK, LANE), 1)
        dbgc = jnp.zeros((CHUNK, LANE), F32)
        for h in range(H):
            dq_ref[:, h * dk:(h + 1) * dk] = dq[h]
            dk_ref[:, h * dk:(h + 1) * dk] = dk_[h]
            dv_ref[:, h * dv:(h + 1) * dv] = dv_[h]
            dbgc = dbgc + jnp.where(lane == h, dbeta[h], 0.0) + jnp.where(lane == h + H, dgc[h], 0.0)
        dbgc_ref[...] = dbgc

    rn = lambda n: N - 1 - n
    return pl.pallas_call(
        body, name="deltanet_bwd", grid=(N,),
        in_specs=[pl.BlockSpec((CHUNK, H * dk), lambda n: (rn(n), 0)),
                  pl.BlockSpec((CHUNK, H * dk), lambda n: (rn(n), 0)),
                  pl.BlockSpec((CHUNK, H * dv), lambda n: (rn(n), 0)),
                  pl.BlockSpec((CHUNK, LANE), lambda n: (rn(n), 0)),
                  pl.BlockSpec((None, H, CHUNK, CHUNK), lambda n: (rn(n), 0, 0, 0)),
                  pl.BlockSpec((None, H, dk, dv), lambda n: (rn(n), 0, 0, 0)),
                  pl.BlockSpec((CHUNK, H * dv), lambda n: (rn(n), 0))],
        out_specs=[pl.BlockSpec((CHUNK, H * dk), lambda n: (rn(n), 0)),
                   pl.BlockSpec((CHUNK, H * dk), lambda n: (rn(n), 0)),
                   pl.BlockSpec((CHUNK, H * dv), lambda n: (rn(n), 0)),
                   pl.BlockSpec((CHUNK, LANE), lambda n: (rn(n), 0))],
        out_shape=[jax.ShapeDtypeStruct((T, H * dk), F32),
                   jax.ShapeDtypeStruct((T, H * dk), F32),
                   jax.ShapeDtypeStruct((T, H * dv), F32),
                   jax.ShapeDtypeStruct((T, LANE), F32)],
        scratch_shapes=[pltpu.VMEM((H, dk, dv), F32)],
        compiler_params=pltpu.CompilerParams(
            dimension_semantics=("arbitrary",), vmem_limit_bytes=VMEM_LIMIT),
    )(q, k, v, bgc, xinv, states, do)


def _place():
    x, y, c = lax.axis_index("x"), lax.axis_index("y"), lax.axis_index("c")
    chips = [(1 - x, y), (x, 1 - y), (1 - x, 1 - y)]
    return x, y, c, chips


def all_gather(name, shards):
    na = len(shards)

    def body(*refs):
        ins, outs = refs[:na], refs[na:2 * na]
        send_sems, recv_sems, local_sems = refs[2 * na:]
        x, y, c, chips = _place()
        me, sibling = (x, y, c), (x, y, 1 - c)

        def copy(a, k, block, to, src=None):
            dst = outs[a].at[4 * block[0] + 2 * block[1] + block[2]]
            return pltpu.make_async_remote_copy(
                src_ref=dst if src is None else src, dst_ref=dst,
                send_sem=send_sems.at[a, k], recv_sem=recv_sems.at[a, k],
                device_id=to, device_id_type=MESH)

        mine, first, passed = [], [], []
        for a in range(na):
            cp = pltpu.make_async_copy(ins[a], outs[a].at[4 * x + 2 * y + c], local_sems.at[a])
            cp.start()
            mine.append(cp)
        for a in range(na):
            cps = [copy(a, 0, me, sibling, src=ins[a])]
            cps += [copy(a, 1 + j, me, (*chip, c), src=ins[a]) for j, chip in enumerate(chips)]
            for cp in cps:
                cp.start()
            first += cps
        for a in range(na):
            for j, chip in enumerate(chips):
                copy(a, 1 + j, (*chip, c), me).wait_recv()
                cp = copy(a, 4 + j, (*chip, c), sibling)
                cp.start()
                passed.append(cp)
        for a in range(na):
            copy(a, 0, sibling, me).wait_recv()
            for j, chip in enumerate(chips):
                copy(a, 4 + j, (*chip, 1 - c), me).wait_recv()
        for cp in first + passed:
            cp.wait_send()
        for cp in mine:
            cp.wait()

    outs = pl.pallas_call(
        body, name=name,
        in_specs=[ANY] * na, out_specs=[ANY] * na,
        out_shape=[jax.ShapeDtypeStruct((N_DEV,) + s.shape, s.dtype) for s in shards],
        scratch_shapes=[pltpu.SemaphoreType.DMA((na, 7)), pltpu.SemaphoreType.DMA((na, 7)),
                        pltpu.SemaphoreType.DMA((na,))],
    )(*shards)
    return list(outs)


HBM_SPEC = pl.BlockSpec(memory_space=pltpu.HBM)
SEM_SPEC = pl.BlockSpec(memory_space=pltpu.SEMAPHORE)
EFFECT = pltpu.SideEffectType.DATAFLOW_SIDE_EFFECTING


def _descriptors(plan, bufs, send_sems, recv_sems):
    return [pltpu.make_async_remote_copy(src_ref=src, dst_ref=dst, send_sem=send_sems.at[k],
                                         recv_sem=recv_sems.at[k], device_id=dev, device_id_type=MESH)
            for k, (src, dst, dev) in enumerate(plan(bufs))]


def split_start(name, bufs, plan, n, after):
    nb = len(bufs)

    def body(*refs):
        for cp in _descriptors(plan, refs[:nb], refs[nb + 1], refs[nb + 2]):
            cp.start()
        refs[-1][...] = jnp.zeros((SUBLANE, LANE), F32)

    outs = pl.pallas_call(
        body, name=name,
        out_shape=(pltpu.SemaphoreType.DMA((n,)), pltpu.SemaphoreType.DMA((n,)),
                   *[pltpu.HBM(b.shape, b.dtype) for b in bufs],
                   jax.ShapeDtypeStruct((SUBLANE, LANE), F32)),
        in_specs=[HBM_SPEC] * nb + [ANY],
        out_specs=(SEM_SPEC, SEM_SPEC, *[HBM_SPEC] * nb, pl.BlockSpec(memory_space=pltpu.VMEM)),
        input_output_aliases={i: 2 + i for i in range(nb)},
        compiler_params=pltpu.CompilerParams(has_side_effects=EFFECT),
    )(*[pltpu.with_memory_space_constraint(b, pltpu.HBM) for b in bufs], after)
    return outs[0], outs[1], list(outs[2:2 + nb]), outs[-1]


def split_wait(name, send_sems, recv_sems, bufs, plan, after):
    nb = len(bufs)

    def body(*refs):
        cps = _descriptors(plan, refs[:nb], refs[nb], refs[nb + 1])
        for cp in cps:
            cp.wait_recv()
        for cp in cps:
            cp.wait_send()
        refs[-1][...] = jnp.zeros((SUBLANE, LANE), F32)

    outs = pl.pallas_call(
        body, name=name,
        out_shape=[pltpu.HBM(b.shape, b.dtype) for b in bufs] + [jax.ShapeDtypeStruct((SUBLANE, LANE), F32)],
        in_specs=[HBM_SPEC] * nb + [SEM_SPEC, SEM_SPEC, ANY],
        out_specs=[HBM_SPEC] * nb + [pl.BlockSpec(memory_space=pltpu.VMEM)],
        input_output_aliases={i: i for i in range(nb)},
        compiler_params=pltpu.CompilerParams(has_side_effects=EFFECT),
    )(*bufs, send_sems, recv_sems, after)
    return list(outs[:nb]), outs[-1]


def _block(px, py, pc):
    return 4 * px + 2 * py + pc


def plan_gather_ici(na):
    def plan(bufs):
        x, y, c, chips = _place()
        out = []
        for a in range(na):
            dst = bufs[na + a].at[_block(x, y, c)]
            out.append((bufs[a], dst, (x, y, 1 - c)))
            out += [(bufs[a], dst, (px, py, c)) for px, py in chips]
        return out
    return plan


def plan_gather_pass(na):
    def plan(bufs):
        x, y, c, chips = _place()
        out = []
        for a in range(na):
            for px, py in chips:
                blk = bufs[a].at[_block(px, py, c)]
                out.append((blk, blk, (x, y, 1 - c)))
        return out
    return plan


def plan_reduce_d2d(na):
    def plan(bufs):
        x, y, c, _ = _place()
        return [(bufs[a].at[2 * s + (1 - c)], bufs[na + a].at[s], (x, y, 1 - c))
                for a in range(na) for s in range(4)]
    return plan


def plan_reduce_ici(na):
    def plan(bufs):
        x, y, c, chips = _place()
        return [(bufs[a].at[2 * px + py], bufs[na + a].at[j], (px, py, c))
                for a in range(na) for j, (px, py) in enumerate(chips)]
    return plan


def place_own(name, land, shard, dev):
    r, c = shard.shape
    tr = _tile(r, 512)

    def body(sp_ref, s_ref, land_ref, o_ref):
        o_ref[...] = s_ref[...]

    return pl.pallas_call(
        body, name=name, out_shape=jax.ShapeDtypeStruct(land.shape, land.dtype),
        grid_spec=pltpu.PrefetchScalarGridSpec(
            num_scalar_prefetch=1, grid=(r // tr,),
            in_specs=[pl.BlockSpec((tr, c), lambda i, s: (i, 0)), ANY],
            out_specs=pl.BlockSpec((None, tr, c), lambda i, s: (s[0], i, 0))),
        input_output_aliases={2: 0},
        compiler_params=pltpu.CompilerParams(dimension_semantics=("arbitrary",)),
    )(dev, shard, land)


def pack(arrs, row_mult=SUBLANE):
    pieces = []
    for a in arrs:
        f = a.reshape(-1).astype(F32)
        pad = (-f.shape[0]) % LANE
        if pad:
            f = jnp.concatenate([f, jnp.zeros((pad,), F32)])
        pieces.append(f)
    flat = jnp.concatenate(pieces)
    rows = flat.shape[0] // LANE
    pad_rows = (-rows) % row_mult
    if pad_rows:
        flat = jnp.concatenate([flat, jnp.zeros((pad_rows * LANE,), F32)])
    return flat.reshape(-1, LANE)


def unpack(buf, shapes):
    flat = buf.reshape(-1)
    outs, off = [], 0
    for shp in shapes:
        n = int(np.prod(shp))
        outs.append(flat[off:off + n].reshape(shp))
        off += n + ((-n) % LANE)
    return outs


def _vjp_rows(fn, n_row_in, n_cot):
    def bwd(*args):
        rows = args[:n_row_in]
        cots = args[n_row_in:n_row_in + n_cot]
        consts = args[n_row_in + n_cot:]
        out, vjp = jax.vjp(fn, *rows, *consts)
        if isinstance(out, (tuple, list)):
            cot = tuple(c.astype(o.dtype) for c, o in zip(cots, out))
        else:
            cot = cots[0].astype(out.dtype)
        return vjp(cot)
    return bwd


def rms_fwd(name, x, g):
    return rowcall(name, rms_tile, [x], [g], [(x.shape[1], BF)])


def rms_bwd(name, x, g, dh, dres):
    D = x.shape[1]
    vj = _vjp_rows(rms_tile, 1, 1)

    def f(x_, dh_, dres_, g_):
        dx, dg = vj(x_, dh_, g_)
        return dx + dres_, dg
    return rowcall(name, f, [x, dh, dres], [g], [(D, F32)], [(1, D)])


def kernel(x, e_norm, e_w_in, e_conv_w, e_a_log, e_dt_bias, e_o_norm, e_ln_g, e_ln_b, e_w_s, e_b_s, e_w_out, o_norm, o_pw1, o_pw1_b, o_dw, o_dw_b, o_ln_g, o_ln_b, o_pw2, o_pw2_b, f_norm, f_w1, f_w2, final_norm, loss_target, m_e_norm, m_e_w_in, m_e_conv_w, m_e_a_log, m_e_dt_bias, m_e_o_norm, m_e_ln_g, m_e_ln_b, m_e_w_s, m_e_b_s, m_e_w_out, m_o_norm, m_o_pw1, m_o_pw1_b, m_o_dw, m_o_dw_b, m_o_ln_g, m_o_ln_b, m_o_pw2, m_o_pw2_b, m_f_norm, m_f_w1, m_f_w2, m_final_norm, v_e_norm, v_e_w_in, v_e_conv_w, v_e_a_log, v_e_dt_bias, v_e_o_norm, v_e_ln_g, v_e_ln_b, v_e_w_s, v_e_b_s, v_e_w_out, v_o_norm, v_o_pw1, v_o_pw1_b, v_o_dw, v_o_dw_b, v_o_ln_g, v_o_ln_b, v_o_pw2, v_o_pw2_b, v_f_norm, v_f_w1, v_f_w2, v_final_norm):
    names = ['e_norm', 'e_w_in', 'e_conv_w', 'e_a_log', 'e_dt_bias', 'e_o_norm', 'e_ln_g', 'e_ln_b', 'e_w_s', 'e_b_s', 'e_w_out', 'o_norm', 'o_pw1', 'o_pw1_b', 'o_dw', 'o_dw_b', 'o_ln_g', 'o_ln_b', 'o_pw2', 'o_pw2_b', 'f_norm', 'f_w1', 'f_w2', 'final_norm']
    W = dict(zip(names, [e_norm, e_w_in, e_conv_w, e_a_log, e_dt_bias, e_o_norm, e_ln_g, e_ln_b, e_w_s, e_b_s, e_w_out, o_norm, o_pw1, o_pw1_b, o_dw, o_dw_b, o_ln_g, o_ln_b, o_pw2, o_pw2_b, f_norm, f_w1, f_w2, final_norm]))
    Mo = dict(zip(names, [m_e_norm, m_e_w_in, m_e_conv_w, m_e_a_log, m_e_dt_bias, m_e_o_norm, m_e_ln_g, m_e_ln_b, m_e_w_s, m_e_b_s, m_e_w_out, m_o_norm, m_o_pw1, m_o_pw1_b, m_o_dw, m_o_dw_b, m_o_ln_g, m_o_ln_b, m_o_pw2, m_o_pw2_b, m_f_norm, m_f_w1, m_f_w2, m_final_norm]))
    Vo = dict(zip(names, [v_e_norm, v_e_w_in, v_e_conv_w, v_e_a_log, v_e_dt_bias, v_e_o_norm, v_e_ln_g, v_e_ln_b, v_e_w_s, v_e_b_s, v_e_w_out, v_o_norm, v_o_pw1, v_o_pw1_b, v_o_dw, v_o_dw_b, v_o_ln_g, v_o_ln_b, v_o_pw2, v_o_pw2_b, v_f_norm, v_f_w1, v_f_w2, v_final_norm]))

    T, D = x.shape[1], x.shape[2]
    H = e_a_log.shape[-1]
    dv = e_o_norm.shape[-1]
    dk = dv
    G = e_w_s.shape[1]
    AQK, AV, BW = H * dk, H * dv, e_ln_g.shape[-1]
    AQKV = 2 * AQK + AV
    in_cols = AQKV + AV + 2 * H + 2 * BW
    KA = e_conv_w.shape[1]
    KC = o_dw.shape[1]
    L = f_norm.shape[0]
    dev = 4 * lax.axis_index("x") + 2 * lax.axis_index("y") + lax.axis_index("c")
    x2d = x.reshape(T, D)
    tgt = loss_target.reshape(T, D)

    dev_sp = dev.astype(jnp.int32).reshape(1)
    where = jnp.stack([lax.axis_index("c"), 2 * lax.axis_index("x") + lax.axis_index("y")]).astype(jnp.int32)
    row = lambda a: a.reshape(1, -1).astype(F32)
    en_row = row(e_norm)

    def gather_begin(tag, shards, after):
        na = len(shards)
        lands = [lax.empty((N_DEV,) + s.shape, s.dtype) for s in shards]
        ss, rs, bufs, tok = split_start(f"gather{tag}_ici_start", shards + lands, plan_gather_ici(na), 4 * na, after)
        return (na, ss, rs, bufs), tok

    def gather_pass(tag, h, after):
        na, ss, rs, bufs = h
        bufs, tok = split_wait(f"gather{tag}_ici_wait", ss, rs, bufs, plan_gather_ici(na), after)
        ss, rs, lands, tok = split_start(f"gather{tag}_pass_start", bufs[na:], plan_gather_pass(na), 3 * na, tok)
        return (na, ss, rs, bufs[:na], lands), tok

    def gather_end(tag, h, after):
        na, ss, rs, shards, lands = h
        lands, _ = split_wait(f"gather{tag}_pass_wait", ss, rs, lands, plan_gather_pass(na), after)
        return [place_own(f"gather{tag}_own{a}", lands[a], shards[a], dev_sp) for a in range(na)]

    small_sharded = ['e_conv_w', 'o_norm', 'o_pw1_b', 'o_dw', 'o_dw_b', 'o_ln_g', 'o_ln_b', 'o_pw2_b']
    sm = all_gather("gather_small", [pack([W[n][0]]) for n in small_sharded])

    bfw = lambda w: w.astype(BF)
    hA0, tok = gather_begin("0", [bfw(e_w_in[0])], sm[0])
    hA1, tok = gather_begin("1", [bfw(e_w_out[0]), bfw(f_w1[0]), bfw(f_w2[0])], tok)
    hA2, tok = gather_begin("2", [bfw(o_pw1[0]), bfw(o_pw2[0]), bfw(f_w1[1]), bfw(f_w2[1])], tok)
    h0 = rowcall("rms_e", rms_tile, [x2d], [en_row], [(D, BF)], after=tok)
    hB0, tok = gather_pass("0", hA0, h0)
    (g_win,) = gather_end("0", hB0, tok)

    full = {}
    for n, g in zip(small_sharded, sm):
        shp = W[n][0].shape
        blocks = [unpack(g[d], [shp])[0] for d in range(N_DEV)]
        full[n] = jnp.concatenate(blocks, axis=-1)
    conv_w = full['e_conv_w']
    on_row, pw1_b_row = row(full['o_norm']), row(full['o_pw1_b'])
    dw_w, dw_b_row = full['o_dw'], row(full['o_dw_b'])
    oln_g_row, oln_b_row, pw2_b_row = row(full['o_ln_g']), row(full['o_ln_b']), row(full['o_pw2_b'])

    win = jnp.moveaxis(g_win, 0, 1).reshape(D, in_cols)
    w_qkv, w_z = win[:, :AQKV], win[:, AQKV:AQKV + AV]
    w_ba = jnp.pad(win[:, AQKV + AV:AQKV + AV + 2 * H], ((0, 0), (0, LANE - 2 * H)))
    w_uv = win[:, AQKV + AV + 2 * H:]

    alog_row = jnp.pad(row(e_a_log), ((0, 0), (H, LANE - 2 * H)))
    dtb_row = jnp.pad(row(e_dt_bias), ((0, 0), (H, LANE - 2 * H)))
    eon_row = row(e_o_norm)
    eln_g_row, eln_b_row = row(e_ln_g), row(e_ln_b)
    w_s = e_w_s[0]
    bs_t = e_b_s[0].T
    fn_rows = [row(f_norm[l]) for l in range(L)]
    fin_row = row(final_norm)

    qkv_raw = matmul("proj_qkv", h0, w_qkv, "nn")
    z_gate = matmul("proj_z", h0, w_z, "nn")
    ba = matmul("proj_ba", h0, w_ba, "nn")
    uv = matmul("proj_uv", h0, w_uv, "nn")

    cwa = min(512, AQKV)
    qkv_post = make_qkv_post(dk, cwa, 2 * AQK // cwa)
    ident = lambda t: t
    qkv = conv_fwd("qkv_conv", qkv_raw, conv_w, [], ident, qkv_post, [0], 1, KA, cw=cwa)
    qn, kn, vv = qkv[:, :AQK], qkv[:, AQK:2 * AQK], qkv[:, 2 * AQK:]
    bgc_fn = make_bgc(H)
    bgc = rowcall("bgc", bgc_fn, [ba], [alog_row, dtb_row], [(LANE, F32)])
    o_dn, xinv, states = deltanet_fwd(qn, kn, vv, bgc, H)
    hB1, tok = gather_pass("1", hA1, o_dn)
    out_a = rowcall("mixa_post", mixa_post_tile, [o_dn, z_gate], [eon_row], [(AV, BF)], after=tok)
    out_b = rowcall("mixb", mixb_tile, [uv], [eln_g_row, eln_b_row, w_s, bs_t], [(BW, BF)])
    g_wout, g_w1_0, g_w2_0 = gather_end("1", hB1, out_b)
    wout = g_wout.reshape(-1, D)
    w1 = [jnp.moveaxis(g_w1_0, 0, 1).reshape(D, -1), None]
    w2 = [g_w2_0.reshape(-1, D), None]
    mix = jnp.concatenate([out_a, out_b], axis=-1)
    add_epi = lambda acc, r: (acc + r,)
    x1 = matmul("out_proj", mix, wout, "nn", epi=add_epi, extras=[x2d])

    def relu2_epi(acc):
        r = jnp.maximum(acc, 0.0)
        return r * r, r

    hf0 = rms_fwd("rms_f0", x1, fn_rows[0])
    hB2, tok = gather_pass("2", hA2, hf0)
    a2_0, ar_0 = matmul("ffn_up0", hf0, w1[0], "nn", epi=relu2_epi, out_dtypes=(BF, BF), after=tok)
    x2 = matmul("ffn_down0", a2_0, w2[0], "nn", epi=add_epi, extras=[x1])
    ffn0 = (hf0, a2_0, ar_0)
    g_pw1, g_pw2, g_w1_1, g_w2_1 = gather_end("2", hB2, x2)
    pw1 = jnp.moveaxis(g_pw1, 0, 1).reshape(D, 2 * D)
    pw2 = g_pw2.reshape(D, D)
    w1[1] = jnp.moveaxis(g_w1_1, 0, 1).reshape(D, -1)
    w2[1] = g_w2_1.reshape(-1, D)

    h1 = rms_fwd("rms_o", x2, on_row)
    bias_epi = lambda acc, b: (acc + b,)
    zc = matmul("pw1", h1, pw1, "nn", epi=bias_epi, extras=[pw1_b_row])
    cwc = min(512, D)
    ncb = D // cwc
    cconv = conv_fwd("dw_conv", zc, dw_w, [dw_b_row], glu_pre, bias_post, [0, ncb], 1, KC, hb=32, cw=cwc)
    ln_silu = lambda c, g, b: jax.nn.silu(ln_tile(c, g, b))
    s_act = rowcall("ln_silu", ln_silu, [cconv], [oln_g_row, oln_b_row], [(D, BF)])
    x3 = matmul("pw2", s_act, pw2, "nn", epi=lambda acc, r, b: (r + (acc + b),), extras=[x2, pw2_b_row])
    hf1 = rms_fwd("rms_f1", x3, fn_rows[1])
    a2_1, ar_1 = matmul("ffn_up1", hf1, w1[1], "nn", epi=relu2_epi, out_dtypes=(BF, BF))
    x4 = matmul("ffn_down1", a2_1, w2[1], "nn", epi=add_epi, extras=[x3])
    ffn1 = (hf1, a2_1, ar_1)

    def loss_bwd_tile(x_, t_, g_):
        l, vjp = jax.vjp(lambda a, b: loss_tile(a, b, t_), x_, g_)
        dx, dg = vjp(jnp.ones_like(l))
        return dx, l, dg
    dx4, loss_part, d_final = rowcall("loss_head", loss_bwd_tile, [x4, tgt], [fin_row],
                                      [(D, F32)], [(1, 1), (1, D)])
    loss = lax.psum(loss_part[0, 0], ("x", "y", "c"))

    def reduce_begin(tag, grads, after):
        na = len(grads)
        lands = [lax.empty((4,) + g.shape[1:], g.dtype) for g in grads]
        ss, rs, bufs, tok = split_start(f"reduce{tag}_d2d_start", grads + lands, plan_reduce_d2d(na), 4 * na, after)
        return (na, ss, rs, bufs), tok

    def reduce_mid(tag, h, after):
        na, ss, rs, bufs = h
        bufs, tok = split_wait(f"reduce{tag}_d2d_wait", ss, rs, bufs, plan_reduce_d2d(na), after)
        parts = []
        for a, (g, rc) in enumerate(zip(bufs[:na], bufs[na:])):
            r, c = g.shape[1], g.shape[2]
            mine = lambda i, n, s: (2 * (i // (n // 4)) + s[0]) * (n // 4) + i % (n // 4)
            parts.append(rowcall(f"chip_sum{tag}_{a}", lambda p, q: (p + q,),
                                 [(g.reshape(N_DEV * r, c), mine), rc.reshape(4 * r, c)], [],
                                 [(c, BF)], tr=_tile(r, 512), sp=where, R=4 * r).reshape(4, r, c))
        lands = [lax.empty((3,) + p.shape[1:], p.dtype) for p in parts]
        ss, rs, bufs, tok = split_start(f"reduce{tag}_ici_start", parts + lands, plan_reduce_ici(na), 3 * na, tok)
        return (na, ss, rs, bufs), tok

    res = {}

    def reduce_end(tag, h, after, targets):
        na, ss, rs, bufs = h
        bufs, _ = split_wait(f"reduce{tag}_ici_wait", ss, rs, bufs, plan_reduce_ici(na), after)
        for a, (part, fin, (n, l)) in enumerate(zip(bufs[:na], bufs[na:], targets)):
            def f(p0, p1, p2, p3, w_, m_, v_):
                g = ((p0.astype(F32) + p1.astype(F32)) + p2.astype(F32)) + p3.astype(F32)
                return (g,) + adamw_tile(w_, g, m_, v_)
            r, C = fin.shape[-2], fin.shape[-1]
            own = lambda i, n_, s: s[1] * n_ + i
            out = rowcall(f"adam{tag}_{a}", f, [(part.reshape(4 * r, C), own), (fin, 0), (fin, 1), (fin, 2),
                                                (W[n], l), (Mo[n], l), (Vo[n], l)], [],
                          [(C, F32)] * 4, tr=256, sp=where, R=r)
            res.setdefault(n, {})[l] = out

    def tie(small, tok):
        return small + tok[0:1, 0:1]

    dscale_epi = lambda acc, r: (acc * (2.0 * r.astype(F32)),)
    d_fnorm = [None] * L

    dpre1 = matmul("ffn_down_dx1", dx4, w2[1], "nt", epi=dscale_epi, extras=[ar_1], out_dtypes=(BF,))
    dw2_1 = matmul("ffn_down_dw1", a2_1, dx4, "tn")
    dw1_1 = matmul("ffn_up_dw1", hf1, dpre1, "tn", colshard=True)
    hD1, tok = reduce_begin("1", [dw1_1, dw2_1.reshape(N_DEV, -1, D)], dpre1)
    dhf1 = matmul("ffn_up_dx1", dpre1, w1[1], "nt", after=tok)
    dx3, d_fnorm[1] = rms_bwd("rms_f_bwd1", x3, fn_rows[1], dhf1, dx4)

    ds_act = matmul("pw2_dx", dx3, pw2, "nt")
    hI1, tok = reduce_mid("1", hD1, ds_act)
    d_pw2 = matmul("pw2_dw", s_act, dx3, "tn", after=tok)
    ln_silu_bwd = _vjp_rows(ln_silu, 1, 1)

    def ln_silu_bwd_tile(c_, ds_, dx3_, g_, b_):
        dc, dg, db = ln_silu_bwd(c_, ds_, g_, b_)
        return dc, dg, db, jnp.sum(dx3_, axis=0, keepdims=True)
    dcconv, d_oln_g, d_oln_b, d_pw2_b = rowcall(
        "ln_silu_bwd", ln_silu_bwd_tile, [cconv, ds_act, dx3], [oln_g_row, oln_b_row],
        [(D, F32)], [(1, D), (1, D), (1, D)])
    (dza, dzb), (sza, szb), d_dw, (d_dw_b,) = conv_bwd(
        "dw_conv_bwd", zc, dw_w, [tie(dw_b_row, tok)], [dcconv], glu_pre, bias_post, [0, ncb], KC, hb=32, cw=cwc, tr=128)
    dzc = jnp.concatenate([dza, dzb], axis=-1)
    d_pw1_b = jnp.concatenate([sza, szb], axis=-1)
    d_pw1 = matmul("pw1_dw", h1, dzc, "tn", colshard=True)
    dh1 = matmul("pw1_dx", dzc, pw1, "nt")
    dx2, d_onorm = rms_bwd("rms_o_bwd", x2, on_row, dh1, dx3)
    reduce_end("1", hI1, dx2, [('f_w1', 1), ('f_w2', 1)])

    hD2, tok = reduce_begin("2", [d_pw1, d_pw2.reshape(N_DEV, -1, D)], dx2)
    dpre0 = matmul("ffn_down_dx0", dx2, w2[0], "nt", epi=dscale_epi, extras=[ar_0], out_dtypes=(BF,), after=tok)
    dw2_0 = matmul("ffn_down_dw0", a2_0, dx2, "tn")
    hI2, tok = reduce_mid("2", hD2, dw2_0)
    dw1_0 = matmul("ffn_up_dw0", hf0, dpre0, "tn", colshard=True, after=tok)
    dhf0 = matmul("ffn_up_dx0", dpre0, w1[0], "nt")
    dx1, d_fnorm[0] = rms_bwd("rms_f_bwd0", x1, fn_rows[0], dhf0, dx2)
    reduce_end("2", hI2, dx1, [('o_pw1', 0), ('o_pw2', 0)])

    dmix = matmul("out_proj_dx", dx1, wout, "nt")
    d_wout = matmul("out_proj_dw", mix, dx1, "tn")
    hD3, tok = reduce_begin("3", [dw1_0, dw2_0.reshape(N_DEV, -1, D), d_wout.reshape(N_DEV, -1, D)], dmix)
    dmix_a, dmix_b = dmix[:, :AV], dmix[:, AV:]
    mixb_bwd = _vjp_rows(mixb_tile, 1, 1)
    duv, d_eln_g, d_eln_b, d_ws, d_bs_t = rowcall(
        "mixb_bwd", mixb_bwd, [uv, dmix_b], [eln_g_row, eln_b_row, w_s, bs_t],
        [(2 * BW, BF)], [(1, BW), (1, BW), w_s.shape, bs_t.shape])
    mixa_bwd = _vjp_rows(mixa_post_tile, 2, 1)
    do_dn, dz_gate, d_eon = rowcall(
        "mixa_post_bwd", mixa_bwd, [o_dn, z_gate, dmix_a], [eon_row],
        [(AV, F32), (AV, BF)], [(1, dv)], after=tok)
    dq, dk_, dv_, dbgc = deltanet_bwd(qn, kn, vv, bgc, xinv, states, do_dn, H)
    hI3, tok = reduce_mid("3", hD3, dq)
    bgc_bwd = _vjp_rows(bgc_fn, 1, 1)
    dba, d_alog_row, d_dtb_row = rowcall(
        "bgc_bwd", bgc_bwd, [ba, dbgc], [alog_row, dtb_row], [(LANE, BF)], [(1, LANE), (1, LANE)])
    dqkv = jnp.concatenate([dq, dk_, dv_], axis=-1)
    (dqkv_raw,), _, d_conv_w, _ = conv_bwd(
        "qkv_conv_bwd", qkv_raw, tie(conv_w, tok), [], [dqkv], ident, qkv_post, [0], KA, cw=cwa)

    dw_qkv = matmul("proj_qkv_dw", h0, dqkv_raw, "tn")
    dw_z = matmul("proj_z_dw", h0, dz_gate, "tn")
    dw_ba = matmul("proj_ba_dw", h0, dba, "tn")
    dw_uv = matmul("proj_uv_dw", h0, duv, "tn")
    d_win = jnp.concatenate([dw_qkv, dw_z, dw_ba[:, :2 * H], dw_uv], axis=-1)
    G_win = jnp.moveaxis(d_win.reshape(D, N_DEV, in_cols // N_DEV), 1, 0)
    reduce_end("3", hI3, dw_uv, [('f_w1', 0), ('f_w2', 0), ('e_w_out', 0)])
    hD4, tok = reduce_begin("4", [G_win], dw_uv)
    dh0 = matmul("proj_qkv_dx", dqkv_raw, w_qkv, "nt", after=tok)
    dh0 = matmul("proj_z_dx", dz_gate, w_z, "nt", epi=add_epi, extras=[dh0])
    hI4, tok = reduce_mid("4", hD4, dh0)
    dh0 = matmul("proj_ba_dx", dba, w_ba, "nt", epi=add_epi, extras=[dh0], after=tok)
    dh0 = matmul("proj_uv_dx", duv, w_uv, "nt", epi=add_epi, extras=[dh0])
    grad_x, d_enorm = rms_bwd("rms_e_bwd", x2d, en_row, dh0, dx1)

    d_alog = d_alog_row[:, H:2 * H]
    d_dtb = d_dtb_row[:, H:2 * H]
    small_names = ['e_norm', 'e_conv_w', 'e_a_log', 'e_dt_bias', 'e_o_norm', 'e_ln_g', 'e_ln_b', 'e_w_s', 'e_b_s',
                   'o_norm', 'o_pw1_b', 'o_dw', 'o_dw_b', 'o_ln_g', 'o_ln_b', 'o_pw2_b', 'f_norm', 'final_norm']
    small_grads = [d_enorm, d_conv_w, d_alog, d_dtb, d_eon, d_eln_g, d_eln_b, d_ws, d_bs_t.T,
                   d_onorm, d_pw1_b, d_dw, d_dw_b, d_oln_g, d_oln_b, d_pw2_b,
                   jnp.concatenate(d_fnorm, axis=0), d_final]
    full_shapes = [g.shape for g in small_grads]
    gs_all = all_gather("gather_small_grads", [pack(small_grads, 256)])[0]

    def sum8(*ps):
        s = ps[0]
        for p in ps[1:]:
            s = s + p
        return (s,)
    gs_sum = rowcall("small_sum", sum8, [(gs_all, d) for d in range(N_DEV)], [], [(LANE, F32)])
    reduce_end("4", hI4, gs_sum, [('e_w_in', 0)])
    for n in list(res):
        res[n] = tuple(jnp.stack([res[n][l][k] for l in sorted(res[n])]) for k in range(4))
    g_full = dict(zip(small_names, unpack(gs_sum, full_shapes)))
    g_loc = {}
    for n in small_names:
        g = g_full[n]
        if n in small_sharded:
            per = g.shape[-1] // N_DEV
            g = lax.dynamic_slice_in_dim(g, dev * per, per, axis=-1)
        g_loc[n] = g.reshape(W[n].shape)
    packs = [pack([A[n] for n in small_names], 256) for A in (W, g_loc, Mo, Vo)]
    d_s, m_s, v_s = rowcall("adam_small", adamw_tile, packs, [], [(LANE, F32)] * 3)
    shapes = [W[n].shape for n in small_names]
    for n, d_, m_, v_ in zip(small_names, unpack(d_s, shapes), unpack(m_s, shapes), unpack(v_s, shapes)):
        res[n] = (g_loc[n], d_, m_, v_)

    grads = [res[n][0] for n in names]
    deltas = [res[n][1] for n in names]
    new_m = [res[n][2] for n in names]
    new_v = [res[n][3] for n in names]
    return (loss, grad_x.reshape(x.shape), *grads, *deltas, *new_m, *new_v)
```

```python
import functools
import math

import jax
import jax.numpy as jnp
import numpy as np
from jax import lax
from jax.experimental import pallas as pl
from jax.experimental.pallas import tpu as pltpu

F32 = jnp.float32
BF = jnp.bfloat16
EPS = 1e-6
CHUNK = 64
B_BLOCK = 128
LANE = 128
SUBLANE = 8
N_DEV = 8
VMEM_LIMIT = 56 * 1024 * 1024

ADAM_LR = 0.001
ADAM_B1 = 0.9
ADAM_B2 = 0.999
ADAM_EPS = 1e-08
ADAM_WD = 0.01
ADAM_STEP = 10

MESH = pl.DeviceIdType.MESH
ANY = pl.BlockSpec(memory_space=pl.ANY)


def _tile(n, pref, mult=SUBLANE):
    if n <= pref:
        return n
    t = (pref // mult) * mult
    while t >= mult:
        if n % t == 0:
            return t
        t -= mult
    return n


def _dg(a, b, ca, cb, hi):
    nb = a.ndim - 2
    batch = tuple(range(nb))
    dims = (((ca + nb,), (cb + nb,)), (batch, batch))
    if hi:
        return lax.dot_general(a.astype(F32), b.astype(F32), dims,
                               precision=lax.Precision.HIGHEST, preferred_element_type=F32)
    return lax.dot_general(a.astype(BF), b.astype(BF), dims, preferred_element_type=F32)


@functools.partial(jax.custom_vjp, nondiff_argnums=(2, 3, 4))
def mm(a, b, ca, cb, hi=False):
    return _dg(a, b, ca, cb, hi)


def _mm_fwd(a, b, ca, cb, hi):
    return _dg(a, b, ca, cb, hi), (a, b)


def _mm_bwd(ca, cb, hi, res, g):
    a, b = res
    if ca == 1:
        da = mm(g, b, 1, 1 - cb, hi)
    else:
        da = mm(b, g, 1 - cb, 1, hi)
    if cb == 0:
        db = mm(a, g, 1 - ca, 0, hi)
    else:
        db = mm(g, a, 0, 1 - ca, hi)
    return da.astype(a.dtype), db.astype(b.dtype)


mm.defvjp(_mm_fwd, _mm_bwd)


def matmul(name, a, b, mode, epi=None, extras=(), out_dtypes=(F32,), colshard=False,
           tm=None, tn=1024, tk=2048, after=None):
    afters = [] if after is None else [after]
    if mode == "nn":
        (M, K), (K2, N) = a.shape, b.shape
    elif mode == "nt":
        (M, K), (N, K2) = a.shape, b.shape
    else:
        (K, M), (K2, N) = a.shape, b.shape
    assert K == K2, (name, a.shape, b.shape, mode)
    if tm is None:
        tm = 1024 if mode == "tn" else 512
    tm = _tile(M, tm)
    tn = N // N_DEV if colshard else _tile(N, tn, LANE)
    tk = _tile(K, tk, LANE)
    nk = K // tk
    grid = (M // tm, N // tn, nk)
    if mode == "nn":
        a_spec = pl.BlockSpec((tm, tk), lambda i, j, k: (i, k))
        b_spec = pl.BlockSpec((tk, tn), lambda i, j, k: (k, j))
        ca, cb = 1, 0
    elif mode == "nt":
        a_spec = pl.BlockSpec((tm, tk), lambda i, j, k: (i, k))
        b_spec = pl.BlockSpec((tn, tk), lambda i, j, k: (j, k))
        ca, cb = 1, 1
    else:
        a_spec = pl.BlockSpec((tk, tm), lambda i, j, k: (k, i))
        b_spec = pl.BlockSpec((tk, tn), lambda i, j, k: (k, j))
        ca, cb = 0, 0
    ex_specs = []
    for e in extras:
        if e.shape[0] == 1:
            ex_specs.append(pl.BlockSpec((1, tn), lambda i, j, k: (0, j)))
        else:
            assert e.shape == (M, N), (name, e.shape)
            ex_specs.append(pl.BlockSpec((tm, tn), lambda i, j, k: (i, j)))
    if colshard:
        out_shape = [jax.ShapeDtypeStruct((N_DEV, M, tn), dt) for dt in out_dtypes]
        out_specs = [pl.BlockSpec((None, tm, tn), lambda i, j, k: (j, i, 0)) for _ in out_dtypes]
    else:
        out_shape = [jax.ShapeDtypeStruct((M, N), dt) for dt in out_dtypes]
        out_specs = [pl.BlockSpec((tm, tn), lambda i, j, k: (i, j)) for _ in out_dtypes]
    n_ex, n_out = len(extras), len(out_dtypes)

    def body(*refs):
        a_ref, b_ref = refs[0], refs[1]
        ex_refs = refs[2:2 + n_ex]
        first_out = 2 + n_ex + len(afters)
        o_refs = refs[first_out:first_out + n_out]
        acc_ref = refs[-1]
        k = pl.program_id(2)
        part = _dg(a_ref[...], b_ref[...], ca, cb, False)

        @pl.when(k == 0)
        def _():
            acc_ref[...] = part

        @pl.when(k > 0)
        def _():
            acc_ref[...] += part

        @pl.when(k == nk - 1)
        def _():
            acc = acc_ref[...]
            res = (acc,) if epi is None else epi(acc, *[r[...] for r in ex_refs])
            for o_ref, r in zip(o_refs, res):
                o_ref[...] = r.astype(o_ref.dtype)

    outs = pl.pallas_call(
        body, name=name, grid=grid,
        in_specs=[a_spec, b_spec] + ex_specs + [ANY] * len(afters),
        out_specs=out_specs, out_shape=out_shape,
        scratch_shapes=[pltpu.VMEM((tm, tn), F32)],
        compiler_params=pltpu.CompilerParams(
            dimension_semantics=("parallel", "parallel", "arbitrary"),
            vmem_limit_bytes=VMEM_LIMIT),
    )(a, b, *extras, *afters)
    return outs[0] if n_out == 1 else tuple(outs)


def rowcall(name, fn, rows, consts, out_rows, out_accs=(), tr=256, sp=None, R=None, after=None):
    afters = [] if after is None else [after]
    rows = [r if isinstance(r, tuple) else (r, None) for r in rows]
    R = rows[0][0].shape[-2] if R is None else R
    tr = _tile(R, tr)
    n = R // tr
    in_specs = []
    for arr, lead in rows:
        C = arr.shape[-1]
        if lead is None:
            assert arr.shape[-2] == R, (name, arr.shape, R)
            in_specs.append(pl.BlockSpec((tr, C), lambda i, *s: (i, 0)))
        elif callable(lead):
            in_specs.append(pl.BlockSpec((tr, C), lambda i, *s, lead=lead: (lead(i, n, *s), 0)))
        else:
            assert arr.shape[-2] == R, (name, arr.shape, R)
            in_specs.append(pl.BlockSpec((None, tr, C), lambda i, *s, lead=lead: (lead, i, 0)))
    for c in consts:
        in_specs.append(pl.BlockSpec(c.shape, lambda i, *s, nd=c.ndim: (0,) * nd))
    out_shape = [jax.ShapeDtypeStruct((R, C), dt) for C, dt in out_rows]
    out_specs = [pl.BlockSpec((tr, C), lambda i, *s: (i, 0)) for C, _ in out_rows]
    for shp in out_accs:
        out_shape.append(jax.ShapeDtypeStruct(shp, F32))
        out_specs.append(pl.BlockSpec(shp, lambda i, *s, nd=len(shp): (0,) * nd))
    n_in, n_row, n_acc = len(rows) + len(consts), len(out_rows), len(out_accs)
    n_sp = 0 if sp is None else 1

    def body(*refs):
        refs = refs[n_sp:]
        ins = [r[...] for r in refs[:n_in]]
        res = fn(*ins)
        if not isinstance(res, (tuple, list)):
            res = (res,)
        o_refs = refs[n_in + len(afters):]
        for o_ref, r in zip(o_refs[:n_row], res[:n_row]):
            o_ref[...] = r.astype(o_ref.dtype)
        if n_acc:
            first = pl.program_id(0) == 0
            for o_ref, r in zip(o_refs[n_row:], res[n_row:]):
                r = r.astype(F32).reshape(o_ref.shape)

                @pl.when(first)
                def _(o_ref=o_ref, r=r):
                    o_ref[...] = r

                @pl.when(jnp.logical_not(first))
                def _(o_ref=o_ref, r=r):
                    o_ref[...] += r

    params = pltpu.CompilerParams(dimension_semantics=("arbitrary",), vmem_limit_bytes=VMEM_LIMIT)
    operands = [a for a, _ in rows] + list(consts) + afters
    in_specs = in_specs + [ANY] * len(afters)
    if sp is None:
        outs = pl.pallas_call(body, name=name, grid=(n,), in_specs=in_specs, out_specs=out_specs,
                              out_shape=out_shape, compiler_params=params)(*operands)
    else:
        outs = pl.pallas_call(
            body, name=name, out_shape=out_shape, compiler_params=params,
            grid_spec=pltpu.PrefetchScalarGridSpec(
                num_scalar_prefetch=1, grid=(n,), in_specs=in_specs, out_specs=out_specs),
        )(sp, *operands)
    return outs[0] if len(outs) == 1 else tuple(outs)


def rms_tile(x, g):
    x = x.astype(F32)
    return x * lax.rsqrt(jnp.mean(x * x, axis=-1, keepdims=True) + EPS) * g


def gelu(x):
    return 0.5 * x * (1.0 + lax.erf(x * (1.0 / math.sqrt(2.0))))


def ln_tile(x, g, b):
    mu = jnp.mean(x, axis=-1, keepdims=True)
    xc = x - mu
    return xc * lax.rsqrt(jnp.mean(xc * xc, axis=-1, keepdims=True) + EPS) * g + b


def lane_groups(fn, width, *arrs):
    n = arrs[0].shape[-1] // width
    outs = [fn(*[a[:, i * width:(i + 1) * width] for a in arrs]) for i in range(n)]
    return jnp.concatenate(outs, axis=-1)


def mixa_post_tile(o, z, o_norm):
    dv = o_norm.shape[-1]
    on = lane_groups(lambda t: rms_tile(t, o_norm), dv, o)
    return on * jax.nn.silu(z)


def mixb_tile(uv, ln_g, ln_b, w_s, bs_t):
    G = w_s.shape[0]
    gw = ln_g.shape[-1]
    dg = gw // G
    tr = uv.shape[0]
    u = gelu(uv[:, :gw])
    vg = gelu(uv[:, gw:])
    ii = lax.broadcasted_iota(jnp.int32, (B_BLOCK, B_BLOCK), 0)
    jj = lax.broadcasted_iota(jnp.int32, (B_BLOCK, B_BLOCK), 1)
    mask = (jj // CHUNK) <= (ii // CHUNK)
    cols = []
    for g in range(G):
        sl = slice(g * dg, (g + 1) * dg)
        vn = ln_tile(vg[:, sl], ln_g[:, sl], ln_b[:, sl])
        wm = jnp.where(mask, w_s[g], 0.0)
        blocks = []
        for m in range(tr // B_BLOCK):
            blk = vn[m * B_BLOCK:(m + 1) * B_BLOCK, :]
            blocks.append(mm(wm, blk, 1, 0) + bs_t[:, g:g + 1])
        mixed = blocks[0] if len(blocks) == 1 else jnp.concatenate(blocks, axis=0)
        cols.append(u[:, sl] * mixed)
    return jnp.concatenate(cols, axis=-1)


def bgc_tile(ba, alog_row, dtb_row):
    tr = ba.shape[0]
    beta = jax.nn.sigmoid(ba)
    g = -jnp.exp(alog_row) * jax.nn.softplus(ba + dtb_row)
    ii = lax.broadcasted_iota(jnp.int32, (tr, tr), 0)
    jj = lax.broadcasted_iota(jnp.int32, (tr, tr), 1)
    tri = jnp.where((ii // CHUNK == jj // CHUNK) & (jj <= ii), 1.0, 0.0).astype(F32)
    gc = mm(tri, g, 1, 0, True)
    return beta, gc


def make_bgc(H):
    def f(ba, alog_row, dtb_row):
        beta, gc = bgc_tile(ba, alog_row, dtb_row)
        lane = lax.broadcasted_iota(jnp.int32, ba.shape, 1)
        return jnp.where(lane < H, beta, jnp.where(lane < 2 * H, gc, 0.0))
    return f


def loss_tile(x, g, target):
    y = rms_tile(x, g)
    err = y - target
    return 0.5 * jnp.sum(jnp.mean(err * err, axis=-1, keepdims=True), axis=0, keepdims=True)


def adamw_tile(w, g, m, v):
    m = ADAM_B1 * m + (1.0 - ADAM_B1) * g
    v = ADAM_B2 * v + (1.0 - ADAM_B2) * (g * g)
    m_hat = m / (1.0 - ADAM_B1 ** ADAM_STEP)
    v_hat = v / (1.0 - ADAM_B2 ** ADAM_STEP)
    delta = -ADAM_LR * (m_hat / (jnp.sqrt(v_hat) + ADAM_EPS) + ADAM_WD * w)
    return delta, m, v


CONV_ROWS = 32


def _shifted_copies(src, sh, rows):
    for b in range(SUBLANE):
        sh[b] = src[pl.ds(b, rows), :]


def _window(sh, off, rows):
    b = off % SUBLANE
    return sh[b, pl.ds(off - b, rows), :]


def _conv_rows(out, sh, w_ref, offsets, rows):
    for r0 in range(0, rows, CONV_ROWS):
        rc = min(CONV_ROWS, rows - r0)
        acc = w_ref[0:1, :] * _window(sh, offsets[0] + r0, rc)
        for k in range(1, len(offsets)):
            acc = acc + w_ref[k:k + 1, :] * _window(sh, offsets[k] + r0, rc)
        out[r0:r0 + rc, :] = acc


def _conv_wgrad(dsrc, d0, sh, offsets, rows):
    dws = []
    for off in offsets:
        acc = None
        for r0 in range(0, rows, CONV_ROWS):
            rc = min(CONV_ROWS, rows - r0)
            prod = dsrc[d0 + r0:d0 + r0 + rc, :] * _window(sh, off + r0, rc)
            for g in range(0, rc, SUBLANE):
                part = prod[g:g + SUBLANE, :]
                acc = part if acc is None else acc + part
        dws.append(jnp.sum(acc, axis=0, keepdims=True))
    return jnp.concatenate(dws, axis=0)


def _conv_specs(T, tr, hb, cw, col_blocks, rev):
    n = T // tr

    def ri(i):
        return (n - 1 - i) if rev else i

    tile_specs, halo_specs = [], []
    for off in col_blocks:
        tile_specs.append(pl.BlockSpec((tr, cw), lambda j, i, off=off: (ri(i), j + off)))
        halo_specs.append(pl.BlockSpec(
            (hb, cw), lambda j, i, off=off: (jnp.maximum(ri(i) * (tr // hb) - 1, 0), j + off)))
    return n, ri, tile_specs, halo_specs


def conv_fwd(name, x, w, consts, pre, post, col_blocks, n_out, K, out_dtype=F32, tr=256, hb=8, cw=512):
    T = x.shape[0]
    C = w.shape[1]
    tr, cw = _tile(T, tr, hb), min(cw, C)
    nb = len(col_blocks)
    n, ri, tile_specs, halo_specs = _conv_specs(T, tr, hb, cw, col_blocks, False)
    w_spec = pl.BlockSpec((K, cw), lambda j, i: (0, j))
    c_specs = [pl.BlockSpec((1, cw), lambda j, i: (0, j)) for _ in consts]

    def body(*refs):
        tiles = [r[...] for r in refs[:nb]]
        halos = [r[...] for r in refs[nb:2 * nb]]
        w_ref = refs[2 * nb]
        cs = [r[...] for r in refs[2 * nb + 1:2 * nb + 1 + len(consts)]]
        o_refs = refs[2 * nb + 1 + len(consts):-3]
        pbuf, shp, cbuf = refs[-3:]
        i = pl.program_id(1)
        pbuf[0:hb, :] = jnp.where(i > 0, pre(*halos), 0.0)
        pbuf[hb:hb + tr, :] = pre(*tiles)
        pbuf[hb + tr:hb + tr + SUBLANE, :] = jnp.zeros((SUBLANE, cw), F32)
        _shifted_copies(pbuf, shp, hb + tr)
        _conv_rows(cbuf, shp, w_ref, [hb - (K - 1) + k for k in range(K)], tr)
        res = post(cbuf[...], pl.program_id(0), *cs)
        for o_ref, r in zip(o_refs, res):
            o_ref[...] = r.astype(o_ref.dtype)

    outs = pl.pallas_call(
        body, name=name, grid=(C // cw, n),
        in_specs=tile_specs + halo_specs + [w_spec] + c_specs,
        out_specs=[pl.BlockSpec((tr, cw), lambda j, i: (i, j)) for _ in range(n_out)],
        out_shape=[jax.ShapeDtypeStruct((T, C), out_dtype) for _ in range(n_out)],
        scratch_shapes=[pltpu.VMEM((hb + tr + SUBLANE, cw), F32), pltpu.VMEM((SUBLANE, hb + tr, cw), F32),
                        pltpu.VMEM((tr, cw), F32)],
        compiler_params=pltpu.CompilerParams(
            dimension_semantics=("parallel", "arbitrary"), vmem_limit_bytes=VMEM_LIMIT),
    )(*([x] * nb), *([x] * nb), w, *consts)
    return outs[0] if n_out == 1 else tuple(outs)


def conv_bwd(name, x, w, consts, grads, pre, post, col_blocks, K, tr=256, hb=8, cw=512, recompute=True):
    T, Cx = x.shape
    C = w.shape[1]
    tr, cw = _tile(T, tr, hb), min(cw, C)
    nb = len(col_blocks)
    n, ri, tile_specs, halo_specs = _conv_specs(T, tr, hb, cw, col_blocks, True)
    w_spec = pl.BlockSpec((K, cw), lambda j, i: (0, j))
    c_specs = [pl.BlockSpec((1, cw), lambda j, i: (0, j)) for _ in consts]
    g_specs = [pl.BlockSpec((tr, cw), lambda j, i: (ri(i), j)) for _ in grads]
    nc, ng = len(consts), len(grads)

    def body(*refs):
        p = 0
        tile_refs = refs[p:p + nb]; p += nb
        halo_refs = refs[p:p + nb]; p += nb
        w_ref = refs[p]; p += 1
        cs = [r[...] for r in refs[p:p + nc]]; p += nc
        gs = [r[...] for r in refs[p:p + ng]]; p += ng
        dx_refs = refs[p:p + nb]; p += nb
        sum_refs = refs[p:p + nb]; p += nb
        dw_ref = refs[p]; p += 1
        dc_refs = refs[p:p + nc]; p += nc
        pbuf, dbuf, ebuf, carry, shp, shd, cbuf = refs[p:p + 7]
        i = pl.program_id(1)
        first = i == 0
        tiles = [r[...] for r in tile_refs]
        halos = [r[...] for r in halo_refs]
        p_tile, vjp_pre = jax.vjp(pre, *tiles)
        pbuf[0:hb, :] = jnp.where(ri(i) > 0, pre(*halos), 0.0)
        pbuf[hb:hb + tr, :] = p_tile
        pbuf[hb + tr:hb + tr + SUBLANE, :] = jnp.zeros((SUBLANE, cw), F32)
        _shifted_copies(pbuf, shp, hb + tr)
        taps = [hb - (K - 1) + k for k in range(K)]
        if recompute:
            _conv_rows(cbuf, shp, w_ref, taps, tr)
            c = cbuf[...]
        else:
            c = jnp.zeros((tr, cw), F32)
        cid = pl.program_id(0)
        _, vjp_post = jax.vjp(lambda c_, *cs_: post(c_, cid, *cs_), c, *cs)
        dres = vjp_post(tuple(g.astype(F32) for g in gs))
        dbuf[0:hb, :] = jnp.zeros((hb, cw), F32)
        dbuf[hb:hb + tr, :] = dres[0]
        dbuf[hb + tr:hb + tr + hb + SUBLANE, :] = jnp.zeros((hb + SUBLANE, cw), F32)
        _shifted_copies(dbuf, shd, hb + tr + hb)
        _conv_rows(ebuf, shd, w_ref, [K - 1 - k for k in range(K)], hb + tr)
        dw = _conv_wgrad(dbuf, hb, shp, taps, tr)

        @pl.when(jnp.logical_not(first))
        def _():
            ebuf[tr:tr + hb, :] += carry[...]

        carry[...] = ebuf[0:hb, :]
        dtiles = vjp_pre(ebuf[hb:hb + tr, :])
        for r, s, d in zip(dx_refs, sum_refs, dtiles):
            r[...] = d.astype(r.dtype)
            ds_ = jnp.sum(d, axis=0, keepdims=True)

            @pl.when(first)
            def _(s=s, ds_=ds_):
                s[...] = ds_

            @pl.when(jnp.logical_not(first))
            def _(s=s, ds_=ds_):
                s[...] += ds_

        accs = [(dw_ref, dw)] + [(r, d) for r, d in zip(dc_refs, dres[1:])]
        for r, d in accs:
            @pl.when(first)
            def _(r=r, d=d):
                r[...] = d

            @pl.when(jnp.logical_not(first))
            def _(r=r, d=d):
                r[...] += d

    n_cb = C // cw
    outs = pl.pallas_call(
        body, name=name, grid=(n_cb, n),
        in_specs=tile_specs + halo_specs + [w_spec] + c_specs + g_specs,
        out_specs=([pl.BlockSpec((tr, cw), lambda j, i: (ri(i), j)) for _ in col_blocks]
                   + [pl.BlockSpec((1, cw), lambda j, i: (0, j)) for _ in col_blocks]
                   + [pl.BlockSpec((K, cw), lambda j, i: (0, j))]
                   + [pl.BlockSpec((1, cw), lambda j, i: (0, j)) for _ in consts]),
        out_shape=([jax.ShapeDtypeStruct((T, C), BF) for _ in col_blocks]
                   + [jax.ShapeDtypeStruct((1, C), F32) for _ in col_blocks]
                   + [jax.ShapeDtypeStruct((K, C), F32)]
                   + [jax.ShapeDtypeStruct((1, C), F32) for _ in consts]),
        scratch_shapes=[pltpu.VMEM((hb + tr + SUBLANE, cw), F32), pltpu.VMEM((hb + tr + hb + SUBLANE, cw), F32),
                        pltpu.VMEM((hb + tr, cw), F32), pltpu.VMEM((hb, cw), F32),
                        pltpu.VMEM((SUBLANE, hb + tr, cw), F32), pltpu.VMEM((SUBLANE, hb + tr + hb, cw), F32),
                        pltpu.VMEM((tr, cw), F32)],
        compiler_params=pltpu.CompilerParams(
            dimension_semantics=("parallel", "arbitrary"), vmem_limit_bytes=VMEM_LIMIT),
    )(*([x] * nb), *([x] * nb), w, *consts, *grads)
    dxs = outs[:nb]
    sums = outs[nb:2 * nb]
    dw = outs[2 * nb]
    dcs = outs[2 * nb + 1:]
    return dxs, sums, dw, dcs


def make_qkv_post(dk, cw, n_qk_chunks):
    def l2(t):
        return t * lax.rsqrt(jnp.sum(t * t, axis=-1, keepdims=True) + EPS)

    def post(c, cid):
        s = jax.nn.silu(c)
        normed = lane_groups(l2, dk, s)
        return (jnp.where(cid < n_qk_chunks, normed, s),)
    return post


def glu_pre(za, zb):
    return za * jax.nn.sigmoid(zb)


def bias_post(c, cid, b):
    return (c + b,)


def _col_to_row(col):
    C = col.shape[-2]
    ii = lax.broadcasted_iota(jnp.int32, (C, C), 0)
    jj = lax.broadcasted_iota(jnp.int32, (C, C), 1)
    wide = jnp.broadcast_to(col, col.shape[:-1] + (C,))
    return jnp.sum(jnp.where(ii == jj, wide, 0.0), axis=-2, keepdims=True)


@jax.custom_vjp
def solve_with_inverse(a, rhs, x):
    return mm(x, rhs, 1, 0, True)


def _swi_fwd(a, rhs, x):
    sol = mm(x, rhs, 1, 0, True)
    return sol, (x, sol)


def _swi_bwd(res, dsol):
    x, sol = res
    drhs = mm(x, dsol, 0, 0, True)
    da = -mm(drhs, sol, 1, 1, True)
    return da, drhs, jnp.zeros_like(x)


solve_with_inverse.defvjp(_swi_fwd, _swi_bwd)


def unit_lower_inverse(a):
    C = a.shape[-1]
    ii = lax.broadcasted_iota(jnp.int32, (C, C), 0)
    jj = lax.broadcasted_iota(jnp.int32, (C, C), 1)
    x = jnp.where(ii == jj, 1.0, 0.0).astype(F32) - a
    p = mm(a, a, 1, 0, True)
    steps = int(math.log2(C)) - 1
    for s in range(steps):
        x = x + mm(x, p, 1, 0, True)
        if s < steps - 1:
            p = mm(p, p, 1, 0, True)
    return x


def dn_masks(C):
    ii = lax.broadcasted_iota(jnp.int32, (C, C), 0)
    jj = lax.broadcasted_iota(jnp.int32, (C, C), 1)
    return ii >= jj, ii > jj


def dn_pre(q, k, v, beta, gc):
    C, dk = q.shape[-2:]
    tri, strict = dn_masks(C)
    q = q * (dk ** -0.5)
    diff = gc - _col_to_row(gc)
    decay = jnp.where(tri, jnp.exp(jnp.where(tri, diff, 0.0)), 0.0)
    kb = k * beta
    vb = v * beta
    a = jnp.where(strict, mm(kb, k, 1, 1) * decay, 0.0)
    eg = jnp.exp(gc)
    rhs = jnp.concatenate([vb, kb * eg], axis=-1)
    attn = mm(q, k, 1, 1) * decay
    qd = q * eg
    g_last = gc[..., C - 1:C, :]
    kt = k * jnp.exp(g_last - gc)
    gl = jnp.exp(g_last)
    return a, rhs, attn, qd, kt, gl


def dn_chunk(q, k, v, beta, gc, state, x):
    dv = v.shape[-1]
    a, rhs, attn, qd, kt, gl = dn_pre(q, k, v, beta, gc)
    sol = solve_with_inverse(a, rhs, x)
    u, w = sol[..., :dv], sol[..., dv:]
    vn = u - mm(w, state, 1, 0)
    o = mm(qd, state, 1, 0) + mm(attn, vn, 1, 0)
    new_state = state * gl + mm(kt, vn, 0, 0)
    return o, new_state


def _by_head(q_ref, k_ref, v_ref, bg, H, dk, dv):
    qv = jnp.stack([q_ref[:, h * dk:(h + 1) * dk] for h in range(H)])
    kv = jnp.stack([k_ref[:, h * dk:(h + 1) * dk] for h in range(H)])
    vv = jnp.stack([v_ref[:, h * dv:(h + 1) * dv] for h in range(H)])
    beta = jnp.stack([bg[:, h:h + 1] for h in range(H)])
    gc = jnp.stack([bg[:, H + h:H + h + 1] for h in range(H)])
    return qv, kv, vv, beta, gc


def deltanet_fwd(qkv, bgc, H):
    T = qkv.shape[0]
    dk = dv = qkv.shape[1] // (3 * H)
    N = T // CHUNK

    def body(q_ref, k_ref, v_ref, bgc_ref, o_ref, x_ref, s_ref, state):
        @pl.when(pl.program_id(0) == 0)
        def _():
            state[...] = jnp.zeros((H, dk, dv), F32)

        qv, kv, vv, beta, gc = _by_head(q_ref, k_ref, v_ref, bgc_ref[...], H, dk, dv)
        a = dn_pre(qv, kv, vv, beta, gc)[0]
        x = unit_lower_inverse(a)
        s = state[...]
        o, s_new = dn_chunk(qv, kv, vv, beta, gc, s, x)
        for h in range(H):
            o_ref[:, h * dv:(h + 1) * dv] = o[h]
        x_ref[...] = x
        s_ref[...] = s
        state[...] = s_new

    return pl.pallas_call(
        body, name="deltanet_fwd", grid=(N,),
        in_specs=[pl.BlockSpec((CHUNK, H * dk), lambda n: (n, 0)),
                  pl.BlockSpec((CHUNK, H * dk), lambda n: (n, 1)),
                  pl.BlockSpec((CHUNK, H * dv), lambda n: (n, 2)),
                  pl.BlockSpec((CHUNK, LANE), lambda n: (n, 0))],
        out_specs=[pl.BlockSpec((CHUNK, H * dv), lambda n: (n, 0)),
                   pl.BlockSpec((None, H, CHUNK, CHUNK), lambda n: (n, 0, 0, 0)),
                   pl.BlockSpec((None, H, dk, dv), lambda n: (n, 0, 0, 0))],
        out_shape=[jax.ShapeDtypeStruct((T, H * dv), F32),
                   jax.ShapeDtypeStruct((N, H, CHUNK, CHUNK), F32),
                   jax.ShapeDtypeStruct((N, H, dk, dv), F32)],
        scratch_shapes=[pltpu.VMEM((H, dk, dv), F32)],
        compiler_params=pltpu.CompilerParams(
            dimension_semantics=("arbitrary",), vmem_limit_bytes=VMEM_LIMIT),
    )(qkv, qkv, qkv, bgc)


def deltanet_bwd(qkv, bgc, xinv, states, do, H):
    T = qkv.shape[0]
    dk = dv = qkv.shape[1] // (3 * H)
    N = T // CHUNK

    def body(q_ref, k_ref, v_ref, bgc_ref, x_ref, s_ref, do_ref, dqkv_ref, dbgc_ref, dstate):
        @pl.when(pl.program_id(0) == 0)
        def _():
            dstate[...] = jnp.zeros((H, dk, dv), F32)

        qv, kv, vv, beta, gc = _by_head(q_ref, k_ref, v_ref, bgc_ref[...], H, dk, dv)
        do = jnp.stack([do_ref[:, h * dv:(h + 1) * dv] for h in range(H)])
        _, vjp = jax.vjp(dn_chunk, qv, kv, vv, beta, gc, s_ref[...], x_ref[...])
        dq, dk_, dv_, dbeta, dgc, ds, _ = vjp((do, dstate[...]))
        dstate[...] = ds
        lane = lax.broadcasted_iota(jnp.int32, (CHUNK, LANE), 1)
        dbgc = jnp.zeros((CHUNK, LANE), F32)
        for h in range(H):
            dqkv_ref[:, h * dk:(h + 1) * dk] = dq[h]
            dqkv_ref[:, (H + h) * dk:(H + h + 1) * dk] = dk_[h]
            dqkv_ref[:, (2 * H + h) * dk:(2 * H + h + 1) * dk] = dv_[h]
            dbgc = dbgc + jnp.where(lane == h, dbeta[h], 0.0) + jnp.where(lane == h + H, dgc[h], 0.0)
        dbgc_ref[...] = dbgc

    rn = lambda n: N - 1 - n
    return pl.pallas_call(
        body, name="deltanet_bwd", grid=(N,),
        in_specs=[pl.BlockSpec((CHUNK, H * dk), lambda n: (rn(n), 0)),
                  pl.BlockSpec((CHUNK, H * dk), lambda n: (rn(n), 1)),
                  pl.BlockSpec((CHUNK, H * dv), lambda n: (rn(n), 2)),
                  pl.BlockSpec((CHUNK, LANE), lambda n: (rn(n), 0)),
                  pl.BlockSpec((None, H, CHUNK, CHUNK), lambda n: (rn(n), 0, 0, 0)),
                  pl.BlockSpec((None, H, dk, dv), lambda n: (rn(n), 0, 0, 0)),
                  pl.BlockSpec((CHUNK, H * dv), lambda n: (rn(n), 0))],
        out_specs=[pl.BlockSpec((CHUNK, 3 * H * dk), lambda n: (rn(n), 0)),
                   pl.BlockSpec((CHUNK, LANE), lambda n: (rn(n), 0))],
        out_shape=[jax.ShapeDtypeStruct((T, 3 * H * dk), F32),
                   jax.ShapeDtypeStruct((T, LANE), F32)],
        scratch_shapes=[pltpu.VMEM((H, dk, dv), F32)],
        compiler_params=pltpu.CompilerParams(
            dimension_semantics=("arbitrary",), vmem_limit_bytes=VMEM_LIMIT),
    )(qkv, qkv, qkv, bgc, xinv, states, do)


def _place():
    x, y, c = lax.axis_index("x"), lax.axis_index("y"), lax.axis_index("c")
    chips = [(1 - x, y), (x, 1 - y), (1 - x, 1 - y)]
    return x, y, c, chips


def all_gather(name, shards):
    na = len(shards)

    def body(*refs):
        ins, outs = refs[:na], refs[na:2 * na]
        send_sems, recv_sems, local_sems = refs[2 * na:]
        x, y, c, chips = _place()
        me, sibling = (x, y, c), (x, y, 1 - c)

        def copy(a, k, block, to, src=None):
            dst = outs[a].at[4 * block[0] + 2 * block[1] + block[2]]
            return pltpu.make_async_remote_copy(
                src_ref=dst if src is None else src, dst_ref=dst,
                send_sem=send_sems.at[a, k], recv_sem=recv_sems.at[a, k],
                device_id=to, device_id_type=MESH)

        mine, first, passed = [], [], []
        for a in range(na):
            cp = pltpu.make_async_copy(ins[a], outs[a].at[4 * x + 2 * y + c], local_sems.at[a])
            cp.start()
            mine.append(cp)
        for a in range(na):
            cps = [copy(a, 0, me, sibling, src=ins[a])]
            cps += [copy(a, 1 + j, me, (*chip, c), src=ins[a]) for j, chip in enumerate(chips)]
            for cp in cps:
                cp.start()
            first += cps
        for a in range(na):
            for j, chip in enumerate(chips):
                copy(a, 1 + j, (*chip, c), me).wait_recv()
                cp = copy(a, 4 + j, (*chip, c), sibling)
                cp.start()
                passed.append(cp)
        for a in range(na):
            copy(a, 0, sibling, me).wait_recv()
            for j, chip in enumerate(chips):
                copy(a, 4 + j, (*chip, 1 - c), me).wait_recv()
        for cp in first + passed:
            cp.wait_send()
        for cp in mine:
            cp.wait()

    outs = pl.pallas_call(
        body, name=name,
        in_specs=[ANY] * na, out_specs=[ANY] * na,
        out_shape=[jax.ShapeDtypeStruct((N_DEV,) + s.shape, s.dtype) for s in shards],
        scratch_shapes=[pltpu.SemaphoreType.DMA((na, 7)), pltpu.SemaphoreType.DMA((na, 7)),
                        pltpu.SemaphoreType.DMA((na,))],
    )(*shards)
    return list(outs)


HBM_SPEC = pl.BlockSpec(memory_space=pltpu.HBM)
SEM_SPEC = pl.BlockSpec(memory_space=pltpu.SEMAPHORE)
EFFECT = pltpu.SideEffectType.DATAFLOW_SIDE_EFFECTING


def _descriptors(plan, bufs, send_sems, recv_sems):
    return [pltpu.make_async_remote_copy(src_ref=src, dst_ref=dst, send_sem=send_sems.at[k],
                                         recv_sem=recv_sems.at[k], device_id=dev, device_id_type=MESH)
            for k, (src, dst, dev) in enumerate(plan(bufs))]


def split_start(name, bufs, plan, n, after):
    nb = len(bufs)

    def body(*refs):
        for cp in _descriptors(plan, refs[:nb], refs[nb + 1], refs[nb + 2]):
            cp.start()
        refs[-1][...] = jnp.zeros((SUBLANE, LANE), F32)

    outs = pl.pallas_call(
        body, name=name,
        out_shape=(pltpu.SemaphoreType.DMA((n,)), pltpu.SemaphoreType.DMA((n,)),
                   *[pltpu.HBM(b.shape, b.dtype) for b in bufs],
                   jax.ShapeDtypeStruct((SUBLANE, LANE), F32)),
        in_specs=[HBM_SPEC] * nb + [ANY],
        out_specs=(SEM_SPEC, SEM_SPEC, *[HBM_SPEC] * nb, pl.BlockSpec(memory_space=pltpu.VMEM)),
        input_output_aliases={i: 2 + i for i in range(nb)},
        compiler_params=pltpu.CompilerParams(has_side_effects=EFFECT),
    )(*[pltpu.with_memory_space_constraint(b, pltpu.HBM) for b in bufs], after)
    return outs[0], outs[1], list(outs[2:2 + nb]), outs[-1]


def split_wait(name, send_sems, recv_sems, bufs, plan, after):
    nb = len(bufs)

    def body(*refs):
        cps = _descriptors(plan, refs[:nb], refs[nb], refs[nb + 1])
        for cp in cps:
            cp.wait_recv()
        for cp in cps:
            cp.wait_send()
        refs[-1][...] = jnp.zeros((SUBLANE, LANE), F32)

    outs = pl.pallas_call(
        body, name=name,
        out_shape=[pltpu.HBM(b.shape, b.dtype) for b in bufs] + [jax.ShapeDtypeStruct((SUBLANE, LANE), F32)],
        in_specs=[HBM_SPEC] * nb + [SEM_SPEC, SEM_SPEC, ANY],
        out_specs=[HBM_SPEC] * nb + [pl.BlockSpec(memory_space=pltpu.VMEM)],
        input_output_aliases={i: i for i in range(nb)},
        compiler_params=pltpu.CompilerParams(has_side_effects=EFFECT),
    )(*bufs, send_sems, recv_sems, after)
    return list(outs[:nb]), outs[-1]


def _block(px, py, pc):
    return 4 * px + 2 * py + pc


def plan_gather_ici(na):
    def plan(bufs):
        x, y, c, chips = _place()
        out = []
        for a in range(na):
            dst = bufs[na + a].at[_block(x, y, c)]
            out.append((bufs[a], dst, (x, y, 1 - c)))
            out += [(bufs[a], dst, (px, py, c)) for px, py in chips]
        return out
    return plan


def plan_gather_pass(na):
    def plan(bufs):
        x, y, c, chips = _place()
        out = []
        for a in range(na):
            for px, py in chips:
                blk = bufs[a].at[_block(px, py, c)]
                out.append((blk, blk, (x, y, 1 - c)))
        return out
    return plan


def plan_reduce_d2d(na):
    def plan(bufs):
        x, y, c, _ = _place()
        return [(bufs[a].at[2 * s + (1 - c)], bufs[na + a].at[s], (x, y, 1 - c))
                for a in range(na) for s in range(4)]
    return plan


def plan_reduce_ici(na):
    def plan(bufs):
        x, y, c, chips = _place()
        return [(bufs[a].at[2 * px + py], bufs[na + a].at[j], (px, py, c))
                for a in range(na) for j, (px, py) in enumerate(chips)]
    return plan


def place_own(name, land, shard, dev):
    r, c = shard.shape
    tr = _tile(r, 512)

    def body(sp_ref, s_ref, land_ref, o_ref):
        o_ref[...] = s_ref[...]

    return pl.pallas_call(
        body, name=name, out_shape=jax.ShapeDtypeStruct(land.shape, land.dtype),
        grid_spec=pltpu.PrefetchScalarGridSpec(
            num_scalar_prefetch=1, grid=(r // tr,),
            in_specs=[pl.BlockSpec((tr, c), lambda i, s: (i, 0)), ANY],
            out_specs=pl.BlockSpec((None, tr, c), lambda i, s: (s[0], i, 0))),
        input_output_aliases={2: 0},
        compiler_params=pltpu.CompilerParams(dimension_semantics=("arbitrary",)),
    )(dev, shard, land)


def pack(arrs, row_mult=SUBLANE):
    pieces = []
    for a in arrs:
        f = a.reshape(-1).astype(F32)
        pad = (-f.shape[0]) % LANE
        if pad:
            f = jnp.concatenate([f, jnp.zeros((pad,), F32)])
        pieces.append(f)
    flat = jnp.concatenate(pieces)
    rows = flat.shape[0] // LANE
    pad_rows = (-rows) % row_mult
    if pad_rows:
        flat = jnp.concatenate([flat, jnp.zeros((pad_rows * LANE,), F32)])
    return flat.reshape(-1, LANE)


def unpack(buf, shapes):
    flat = buf.reshape(-1)
    outs, off = [], 0
    for shp in shapes:
        n = int(np.prod(shp))
        outs.append(flat[off:off + n].reshape(shp))
        off += n + ((-n) % LANE)
    return outs


def _vjp_rows(fn, n_row_in, n_cot):
    def bwd(*args):
        rows = args[:n_row_in]
        cots = args[n_row_in:n_row_in + n_cot]
        consts = args[n_row_in + n_cot:]
        out, vjp = jax.vjp(fn, *rows, *consts)
        if isinstance(out, (tuple, list)):
            cot = tuple(c.astype(o.dtype) for c, o in zip(cots, out))
        else:
            cot = cots[0].astype(out.dtype)
        return vjp(cot)
    return bwd


def rms_fwd(name, x, g):
    return rowcall(name, rms_tile, [x], [g], [(x.shape[1], BF)])


def rms_bwd(name, x, g, dh, dres):
    D = x.shape[1]
    vj = _vjp_rows(rms_tile, 1, 1)

    def f(x_, dh_, dres_, g_):
        dx, dg = vj(x_, dh_, g_)
        dx = dx + dres_
        return dx, dx, dg
    return rowcall(name, f, [x, dh, dres], [g], [(D, F32), (D, BF)], [(1, D)])


def kernel(x, e_norm, e_w_in, e_conv_w, e_a_log, e_dt_bias, e_o_norm, e_ln_g, e_ln_b, e_w_s, e_b_s, e_w_out, o_norm, o_pw1, o_pw1_b, o_dw, o_dw_b, o_ln_g, o_ln_b, o_pw2, o_pw2_b, f_norm, f_w1, f_w2, final_norm, loss_target, m_e_norm, m_e_w_in, m_e_conv_w, m_e_a_log, m_e_dt_bias, m_e_o_norm, m_e_ln_g, m_e_ln_b, m_e_w_s, m_e_b_s, m_e_w_out, m_o_norm, m_o_pw1, m_o_pw1_b, m_o_dw, m_o_dw_b, m_o_ln_g, m_o_ln_b, m_o_pw2, m_o_pw2_b, m_f_norm, m_f_w1, m_f_w2, m_final_norm, v_e_norm, v_e_w_in, v_e_conv_w, v_e_a_log, v_e_dt_bias, v_e_o_norm, v_e_ln_g, v_e_ln_b, v_e_w_s, v_e_b_s, v_e_w_out, v_o_norm, v_o_pw1, v_o_pw1_b, v_o_dw, v_o_dw_b, v_o_ln_g, v_o_ln_b, v_o_pw2, v_o_pw2_b, v_f_norm, v_f_w1, v_f_w2, v_final_norm):
    names = ['e_norm', 'e_w_in', 'e_conv_w', 'e_a_log', 'e_dt_bias', 'e_o_norm', 'e_ln_g', 'e_ln_b', 'e_w_s', 'e_b_s', 'e_w_out', 'o_norm', 'o_pw1', 'o_pw1_b', 'o_dw', 'o_dw_b', 'o_ln_g', 'o_ln_b', 'o_pw2', 'o_pw2_b', 'f_norm', 'f_w1', 'f_w2', 'final_norm']
    W = dict(zip(names, [e_norm, e_w_in, e_conv_w, e_a_log, e_dt_bias, e_o_norm, e_ln_g, e_ln_b, e_w_s, e_b_s, e_w_out, o_norm, o_pw1, o_pw1_b, o_dw, o_dw_b, o_ln_g, o_ln_b, o_pw2, o_pw2_b, f_norm, f_w1, f_w2, final_norm]))
    Mo = dict(zip(names, [m_e_norm, m_e_w_in, m_e_conv_w, m_e_a_log, m_e_dt_bias, m_e_o_norm, m_e_ln_g, m_e_ln_b, m_e_w_s, m_e_b_s, m_e_w_out, m_o_norm, m_o_pw1, m_o_pw1_b, m_o_dw, m_o_dw_b, m_o_ln_g, m_o_ln_b, m_o_pw2, m_o_pw2_b, m_f_norm, m_f_w1, m_f_w2, m_final_norm]))
    Vo = dict(zip(names, [v_e_norm, v_e_w_in, v_e_conv_w, v_e_a_log, v_e_dt_bias, v_e_o_norm, v_e_ln_g, v_e_ln_b, v_e_w_s, v_e_b_s, v_e_w_out, v_o_norm, v_o_pw1, v_o_pw1_b, v_o_dw, v_o_dw_b, v_o_ln_g, v_o_ln_b, v_o_pw2, v_o_pw2_b, v_f_norm, v_f_w1, v_f_w2, v_final_norm]))

    T, D = x.shape[1], x.shape[2]
    H = e_a_log.shape[-1]
    dv = e_o_norm.shape[-1]
    dk = dv
    G = e_w_s.shape[1]
    AQK, AV, BW = H * dk, H * dv, e_ln_g.shape[-1]
    AQKV = 2 * AQK + AV
    in_cols = AQKV + AV + 2 * H + 2 * BW
    KA = e_conv_w.shape[1]
    KC = o_dw.shape[1]
    L = f_norm.shape[0]
    dev = 4 * lax.axis_index("x") + 2 * lax.axis_index("y") + lax.axis_index("c")
    x2d = x.reshape(T, D)
    tgt = loss_target.reshape(T, D)

    dev_sp = dev.astype(jnp.int32).reshape(1)
    where = jnp.stack([lax.axis_index("c"), 2 * lax.axis_index("x") + lax.axis_index("y")]).astype(jnp.int32)
    row = lambda a: a.reshape(1, -1).astype(F32)
    en_row = row(e_norm)

    def gather_begin(tag, shards, after):
        na = len(shards)
        lands = [lax.empty((N_DEV,) + s.shape, s.dtype) for s in shards]
        ss, rs, bufs, tok = split_start(f"gather{tag}_ici_start", shards + lands, plan_gather_ici(na), 4 * na, after)
        return (na, ss, rs, bufs), tok

    def gather_pass(tag, h, after):
        na, ss, rs, bufs = h
        bufs, tok = split_wait(f"gather{tag}_ici_wait", ss, rs, bufs, plan_gather_ici(na), after)
        ss, rs, lands, tok = split_start(f"gather{tag}_pass_start", bufs[na:], plan_gather_pass(na), 3 * na, tok)
        return (na, ss, rs, bufs[:na], lands), tok

    def gather_end(tag, h, after):
        na, ss, rs, shards, lands = h
        lands, _ = split_wait(f"gather{tag}_pass_wait", ss, rs, lands, plan_gather_pass(na), after)
        return [place_own(f"gather{tag}_own{a}", lands[a], shards[a], dev_sp) for a in range(na)]

    small_sharded = ['e_conv_w', 'o_norm', 'o_pw1_b', 'o_dw', 'o_dw_b', 'o_ln_g', 'o_ln_b', 'o_pw2_b']
    sm = all_gather("gather_small", [pack([W[n][0]]) for n in small_sharded])

    bfw = lambda w: w.astype(BF)
    hA0, tok = gather_begin("0", [bfw(e_w_in[0])], sm[0])
    hA1, tok = gather_begin("1", [bfw(e_w_out[0]), bfw(f_w1[0]), bfw(f_w2[0])], tok)
    hA2, tok = gather_begin("2", [bfw(o_pw1[0]), bfw(o_pw2[0]), bfw(f_w1[1]), bfw(f_w2[1])], tok)
    h0 = rowcall("rms_e", rms_tile, [x2d], [en_row], [(D, BF)], after=tok)
    hB0, tok = gather_pass("0", hA0, h0)
    (g_win,) = gather_end("0", hB0, tok)

    full = {}
    for n, g in zip(small_sharded, sm):
        shp = W[n][0].shape
        blocks = [unpack(g[d], [shp])[0] for d in range(N_DEV)]
        full[n] = jnp.concatenate(blocks, axis=-1)
    conv_w = full['e_conv_w']
    on_row, pw1_b_row = row(full['o_norm']), row(full['o_pw1_b'])
    dw_w, dw_b_row = full['o_dw'], row(full['o_dw_b'])
    oln_g_row, oln_b_row, pw2_b_row = row(full['o_ln_g']), row(full['o_ln_b']), row(full['o_pw2_b'])

    win = jnp.moveaxis(g_win, 0, 1).reshape(D, in_cols)
    w_qkv, w_z = win[:, :AQKV], win[:, AQKV:AQKV + AV]
    w_ba = jnp.pad(win[:, AQKV + AV:AQKV + AV + 2 * H], ((0, 0), (0, LANE - 2 * H)))
    w_uv = win[:, AQKV + AV + 2 * H:]

    alog_row = jnp.pad(row(e_a_log), ((0, 0), (H, LANE - 2 * H)))
    dtb_row = jnp.pad(row(e_dt_bias), ((0, 0), (H, LANE - 2 * H)))
    eon_row = row(e_o_norm)
    eln_g_row, eln_b_row = row(e_ln_g), row(e_ln_b)
    w_s = e_w_s[0]
    bs_t = e_b_s[0].T
    fn_rows = [row(f_norm[l]) for l in range(L)]
    fin_row = row(final_norm)

    qkv_raw = matmul("proj_qkv", h0, w_qkv, "nn")
    z_gate = matmul("proj_z", h0, w_z, "nn")
    ba = matmul("proj_ba", h0, w_ba, "nn")
    uv = matmul("proj_uv", h0, w_uv, "nn")

    cwa = min(512, AQKV)
    qkv_post = make_qkv_post(dk, cwa, 2 * AQK // cwa)
    ident = lambda t: t
    qkv = conv_fwd("qkv_conv", qkv_raw, conv_w, [], ident, qkv_post, [0], 1, KA, cw=cwa)
    bgc_fn = make_bgc(H)
    bgc = rowcall("bgc", bgc_fn, [ba], [alog_row, dtb_row], [(LANE, F32)])
    o_dn, xinv, states = deltanet_fwd(qkv, bgc, H)
    hB1, tok = gather_pass("1", hA1, o_dn)

    def mix_tile(o, z, uv_, o_norm_, ln_g, ln_b, w_s_, bs_t_):
        return jnp.concatenate([mixa_post_tile(o, z, o_norm_), mixb_tile(uv_, ln_g, ln_b, w_s_, bs_t_)], axis=-1)
    mix_consts = [eon_row, eln_g_row, eln_b_row, w_s, bs_t]
    mix = rowcall("mix", mix_tile, [o_dn, z_gate, uv], mix_consts, [(AV + BW, BF)], after=tok)
    g_wout, g_w1_0, g_w2_0 = gather_end("1", hB1, mix)
    wout = g_wout.reshape(-1, D)
    w1 = [jnp.moveaxis(g_w1_0, 0, 1).reshape(D, -1), None]
    w2 = [g_w2_0.reshape(-1, D), None]
    add_epi = lambda acc, r: (acc + r,)
    x1 = matmul("out_proj", mix, wout, "nn", epi=add_epi, extras=[x2d])

    def relu2_epi(acc):
        r = jnp.maximum(acc, 0.0)
        return r * r, r

    hf0 = rms_fwd("rms_f0", x1, fn_rows[0])
    hB2, tok = gather_pass("2", hA2, hf0)
    a2_0, ar_0 = matmul("ffn_up0", hf0, w1[0], "nn", epi=relu2_epi, out_dtypes=(BF, BF), after=tok)
    x2 = matmul("ffn_down0", a2_0, w2[0], "nn", epi=add_epi, extras=[x1])
    ffn0 = (hf0, a2_0, ar_0)
    g_pw1, g_pw2, g_w1_1, g_w2_1 = gather_end("2", hB2, x2)
    pw1 = jnp.moveaxis(g_pw1, 0, 1).reshape(D, 2 * D)
    pw2 = g_pw2.reshape(D, D)
    w1[1] = jnp.moveaxis(g_w1_1, 0, 1).reshape(D, -1)
    w2[1] = g_w2_1.reshape(-1, D)

    h1 = rms_fwd("rms_o", x2, on_row)
    bias_epi = lambda acc, b: (acc + b,)
    zc = matmul("pw1", h1, pw1, "nn", epi=bias_epi, extras=[pw1_b_row])
    cwc = min(512, D)
    ncb = D // cwc
    cconv = conv_fwd("dw_conv", zc, dw_w, [dw_b_row], glu_pre, bias_post, [0, ncb], 1, KC, hb=32, cw=cwc)
    ln_silu = lambda c, g, b: jax.nn.silu(ln_tile(c, g, b))
    s_act = rowcall("ln_silu", ln_silu, [cconv], [oln_g_row, oln_b_row], [(D, BF)])
    x3 = matmul("pw2", s_act, pw2, "nn", epi=lambda acc, r, b: (r + (acc + b),), extras=[x2, pw2_b_row])
    hf1 = rms_fwd("rms_f1", x3, fn_rows[1])
    a2_1, ar_1 = matmul("ffn_up1", hf1, w1[1], "nn", epi=relu2_epi, out_dtypes=(BF, BF))
    x4 = matmul("ffn_down1", a2_1, w2[1], "nn", epi=add_epi, extras=[x3])
    ffn1 = (hf1, a2_1, ar_1)

    def loss_bwd_tile(x_, t_, g_):
        l, vjp = jax.vjp(lambda a, b: loss_tile(a, b, t_), x_, g_)
        dx, dg = vjp(jnp.ones_like(l))
        return dx, dx, l, dg
    dx4, dx4_b, loss_part, d_final = rowcall("loss_head", loss_bwd_tile, [x4, tgt], [fin_row],
                                             [(D, F32), (D, BF)], [(1, 1), (1, D)])
    loss = lax.psum(loss_part[0, 0], ("x", "y", "c"))

    def reduce_begin(tag, grads, after):
        na = len(grads)
        lands = [lax.empty((4,) + g.shape[1:], g.dtype) for g in grads]
        ss, rs, bufs, tok = split_start(f"reduce{tag}_d2d_start", grads + lands, plan_reduce_d2d(na), 4 * na, after)
        return (na, ss, rs, bufs), tok

    def reduce_mid(tag, h, after):
        na, ss, rs, bufs = h
        bufs, tok = split_wait(f"reduce{tag}_d2d_wait", ss, rs, bufs, plan_reduce_d2d(na), after)
        parts = []
        for a, (g, rc) in enumerate(zip(bufs[:na], bufs[na:])):
            r, c = g.shape[1], g.shape[2]
            mine = lambda i, n, s: (2 * (i // (n // 4)) + s[0]) * (n // 4) + i % (n // 4)
            parts.append(rowcall(f"chip_sum{tag}_{a}", lambda p, q: (p + q,),
                                 [(g.reshape(N_DEV * r, c), mine), rc.reshape(4 * r, c)], [],
                                 [(c, BF)], tr=_tile(r, 512), sp=where, R=4 * r).reshape(4, r, c))
        lands = [lax.empty((3,) + p.shape[1:], p.dtype) for p in parts]
        ss, rs, bufs, tok = split_start(f"reduce{tag}_ici_start", parts + lands, plan_reduce_ici(na), 3 * na, tok)
        return (na, ss, rs, bufs), tok

    res = {}

    def reduce_end(tag, h, after, targets):
        na, ss, rs, bufs = h
        bufs, _ = split_wait(f"reduce{tag}_ici_wait", ss, rs, bufs, plan_reduce_ici(na), after)
        for a, (part, fin, (n, l)) in enumerate(zip(bufs[:na], bufs[na:], targets)):
            def f(p0, p1, p2, p3, w_, m_, v_):
                g = ((p0.astype(F32) + p1.astype(F32)) + p2.astype(F32)) + p3.astype(F32)
                return (g,) + adamw_tile(w_, g, m_, v_)
            r, C = fin.shape[-2], fin.shape[-1]
            own = lambda i, n_, s: s[1] * n_ + i
            out = rowcall(f"adam{tag}_{a}", f, [(part.reshape(4 * r, C), own), (fin, 0), (fin, 1), (fin, 2),
                                                (W[n], l), (Mo[n], l), (Vo[n], l)], [],
                          [(C, F32)] * 4, tr=256, sp=where, R=r)
            res.setdefault(n, {})[l] = out

    def tie(small, tok):
        return small + tok[0:1, 0:1]

    dscale_epi = lambda acc, r: (acc * (2.0 * r.astype(F32)),)
    d_fnorm = [None] * L

    dpre1 = matmul("ffn_down_dx1", dx4_b, w2[1], "nt", epi=dscale_epi, extras=[ar_1], out_dtypes=(BF,))
    dw2_1 = matmul("ffn_down_dw1", a2_1, dx4_b, "tn")
    dw1_1 = matmul("ffn_up_dw1", hf1, dpre1, "tn", colshard=True)
    hD1, tok = reduce_begin("1", [dw1_1, dw2_1.reshape(N_DEV, -1, D)], dpre1)
    dhf1 = matmul("ffn_up_dx1", dpre1, w1[1], "nt", after=tok, tk=4096)
    dx3, dx3_b, d_fnorm[1] = rms_bwd("rms_f_bwd1", x3, fn_rows[1], dhf1, dx4)

    ds_act = matmul("pw2_dx", dx3_b, pw2, "nt")
    hI1, tok = reduce_mid("1", hD1, ds_act)
    d_pw2 = matmul("pw2_dw", s_act, dx3_b, "tn", after=tok)
    ln_silu_bwd = _vjp_rows(ln_silu, 1, 1)

    def ln_silu_bwd_tile(c_, ds_, dx3_, g_, b_):
        dc, dg, db = ln_silu_bwd(c_, ds_, g_, b_)
        return dc, dg, db, jnp.sum(dx3_, axis=0, keepdims=True)
    dcconv, d_oln_g, d_oln_b, d_pw2_b = rowcall(
        "ln_silu_bwd", ln_silu_bwd_tile, [cconv, ds_act, dx3], [oln_g_row, oln_b_row],
        [(D, F32)], [(1, D), (1, D), (1, D)])
    (dza, dzb), (sza, szb), d_dw, (d_dw_b,) = conv_bwd(
        "dw_conv_bwd", zc, dw_w, [tie(dw_b_row, tok)], [dcconv], glu_pre, bias_post, [0, ncb], KC, hb=32, cw=cwc, tr=128,
        recompute=False)
    dzc = jnp.concatenate([dza, dzb], axis=-1)
    d_pw1_b = jnp.concatenate([sza, szb], axis=-1)
    d_pw1 = matmul("pw1_dw", h1, dzc, "tn", colshard=True)
    dh1 = matmul("pw1_dx", dzc, pw1, "nt")
    dx2, dx2_b, d_onorm = rms_bwd("rms_o_bwd", x2, on_row, dh1, dx3)
    reduce_end("1", hI1, dx2, [('f_w1', 1), ('f_w2', 1)])

    hD2, tok = reduce_begin("2", [d_pw1, d_pw2.reshape(N_DEV, -1, D)], dx2)
    dpre0 = matmul("ffn_down_dx0", dx2_b, w2[0], "nt", epi=dscale_epi, extras=[ar_0], out_dtypes=(BF,), after=tok)
    dw2_0 = matmul("ffn_down_dw0", a2_0, dx2_b, "tn")
    hI2, tok = reduce_mid("2", hD2, dw2_0)
    dw1_0 = matmul("ffn_up_dw0", hf0, dpre0, "tn", colshard=True, after=tok)
    dhf0 = matmul("ffn_up_dx0", dpre0, w1[0], "nt", tk=4096)
    dx1, dx1_b, d_fnorm[0] = rms_bwd("rms_f_bwd0", x1, fn_rows[0], dhf0, dx2)
    reduce_end("2", hI2, dx1, [('o_pw1', 0), ('o_pw2', 0)])

    dmix = matmul("out_proj_dx", dx1_b, wout, "nt")
    d_wout = matmul("out_proj_dw", mix, dx1_b, "tn")
    hD3, tok = reduce_begin("3", [dw1_0, dw2_0.reshape(N_DEV, -1, D), d_wout.reshape(N_DEV, -1, D)], dmix)
    do_dn, dz_gate, duv, d_eon, d_eln_g, d_eln_b, d_ws, d_bs_t = rowcall(
        "mix_bwd", _vjp_rows(mix_tile, 3, 1), [o_dn, z_gate, uv, dmix], mix_consts,
        [(AV, F32), (AV, BF), (2 * BW, BF)], [(1, dv), (1, BW), (1, BW), w_s.shape, bs_t.shape], after=tok)
    dqkv, dbgc = deltanet_bwd(qkv, bgc, xinv, states, do_dn, H)
    hI3, tok = reduce_mid("3", hD3, dbgc)
    bgc_bwd = _vjp_rows(bgc_fn, 1, 1)
    dba, d_alog_row, d_dtb_row = rowcall(
        "bgc_bwd", bgc_bwd, [ba, dbgc], [alog_row, dtb_row], [(LANE, BF)], [(1, LANE), (1, LANE)])
    (dqkv_raw,), _, d_conv_w, _ = conv_bwd(
        "qkv_conv_bwd", qkv_raw, tie(conv_w, tok), [], [dqkv], ident, qkv_post, [0], KA, cw=cwa)

    dw_qkv = matmul("proj_qkv_dw", h0, dqkv_raw, "tn")
    dw_z = matmul("proj_z_dw", h0, dz_gate, "tn")
    dw_ba = matmul("proj_ba_dw", h0, dba, "tn")
    dw_uv = matmul("proj_uv_dw", h0, duv, "tn")
    d_win = jnp.concatenate([dw_qkv, dw_z, dw_ba[:, :2 * H], dw_uv], axis=-1)
    G_win = jnp.moveaxis(d_win.reshape(D, N_DEV, in_cols // N_DEV), 1, 0)
    reduce_end("3", hI3, dw_uv, [('f_w1', 0), ('f_w2', 0), ('e_w_out', 0)])
    hD4, tok = reduce_begin("4", [G_win], dw_uv)
    dh0 = matmul("proj_qkv_dx", dqkv_raw, w_qkv, "nt", after=tok)
    dh0 = matmul("proj_z_dx", dz_gate, w_z, "nt", epi=add_epi, extras=[dh0])
    hI4, tok = reduce_mid("4", hD4, dh0)
    dh0 = matmul("proj_ba_dx", dba, w_ba, "nt", epi=add_epi, extras=[dh0], after=tok)
    dh0 = matmul("proj_uv_dx", duv, w_uv, "nt", epi=add_epi, extras=[dh0])
    grad_x, _, d_enorm = rms_bwd("rms_e_bwd", x2d, en_row, dh0, dx1)

    d_alog = d_alog_row[:, H:2 * H]
    d_dtb = d_dtb_row[:, H:2 * H]
    small_names = ['e_norm', 'e_conv_w', 'e_a_log', 'e_dt_bias', 'e_o_norm', 'e_ln_g', 'e_ln_b', 'e_w_s', 'e_b_s',
                   'o_norm', 'o_pw1_b', 'o_dw', 'o_dw_b', 'o_ln_g', 'o_ln_b', 'o_pw2_b', 'f_norm', 'final_norm']
    small_grads = [d_enorm, d_conv_w, d_alog, d_dtb, d_eon, d_eln_g, d_eln_b, d_ws, d_bs_t.T,
                   d_onorm, d_pw1_b, d_dw, d_dw_b, d_oln_g, d_oln_b, d_pw2_b,
                   jnp.concatenate(d_fnorm, axis=0), d_final]
    full_shapes = [g.shape for g in small_grads]
    gs_all = all_gather("gather_small_grads", [pack(small_grads, 256)])[0]

    def sum8(*ps):
        s = ps[0]
        for p in ps[1:]:
            s = s + p
        return (s,)
    gs_sum = rowcall("small_sum", sum8, [(gs_all, d) for d in range(N_DEV)], [], [(LANE, F32)])
    reduce_end("4", hI4, gs_sum, [('e_w_in', 0)])
    for n in list(res):
        res[n] = tuple(jnp.stack([res[n][l][k] for l in sorted(res[n])]) for k in range(4))
    g_full = dict(zip(small_names, unpack(gs_sum, full_shapes)))
    g_loc = {}
    for n in small_names:
        g = g_full[n]
        if n in small_sharded:
            per = g.shape[-1] // N_DEV
            g = lax.dynamic_slice_in_dim(g, dev * per, per, axis=-1)
        g_loc[n] = g.reshape(W[n].shape)
    packs = [pack([A[n] for n in small_names], 256) for A in (W, g_loc, Mo, Vo)]
    d_s, m_s, v_s = rowcall("adam_small", adamw_tile, packs, [], [(LANE, F32)] * 3)
    shapes = [W[n].shape for n in small_names]
    for n, d_, m_, v_ in zip(small_names, unpack(d_s, shapes), unpack(m_s, shapes), unpack(v_s, shapes)):
        res[n] = (g_loc[n], d_, m_, v_)

    grads = [res[n][0] for n in names]
    deltas = [res[n][1] for n in names]
    new_m = [res[n][2] for n in names]
    new_v = [res[n][3] for n in names]
    return (loss, grad_x.reshape(x.shape), *grads, *deltas, *new_m, *new_v)
```

```python
import functools
import math

import jax
import jax.numpy as jnp
import numpy as np
from jax import lax
from jax.experimental import pallas as pl
from jax.experimental.pallas import tpu as pltpu

F32 = jnp.float32
BF = jnp.bfloat16
EPS = 1e-6
CHUNK = 64
B_BLOCK = 128
LANE = 128
SUBLANE = 8
N_DEV = 8
VMEM_LIMIT = 56 * 1024 * 1024

ADAM_LR = 0.001
ADAM_B1 = 0.9
ADAM_B2 = 0.999
ADAM_EPS = 1e-08
ADAM_WD = 0.01
ADAM_STEP = 10

MESH = pl.DeviceIdType.MESH
ANY = pl.BlockSpec(memory_space=pl.ANY)


def _tile(n, pref, mult=SUBLANE):
    if n <= pref:
        return n
    t = (pref // mult) * mult
    while t >= mult:
        if n % t == 0:
            return t
        t -= mult
    return n


def _dg(a, b, ca, cb, hi):
    nb = a.ndim - 2
    batch = tuple(range(nb))
    dims = (((ca + nb,), (cb + nb,)), (batch, batch))
    if hi:
        return lax.dot_general(a.astype(F32), b.astype(F32), dims,
                               precision=lax.Precision.HIGHEST, preferred_element_type=F32)
    return lax.dot_general(a.astype(BF), b.astype(BF), dims, preferred_element_type=F32)


@functools.partial(jax.custom_vjp, nondiff_argnums=(2, 3, 4))
def mm(a, b, ca, cb, hi=False):
    return _dg(a, b, ca, cb, hi)


def _mm_fwd(a, b, ca, cb, hi):
    return _dg(a, b, ca, cb, hi), (a, b)


def _mm_bwd(ca, cb, hi, res, g):
    a, b = res
    if ca == 1:
        da = mm(g, b, 1, 1 - cb, hi)
    else:
        da = mm(b, g, 1 - cb, 1, hi)
    if cb == 0:
        db = mm(a, g, 1 - ca, 0, hi)
    else:
        db = mm(g, a, 0, 1 - ca, hi)
    return da.astype(a.dtype), db.astype(b.dtype)


mm.defvjp(_mm_fwd, _mm_bwd)


def matmul(name, a, b, mode, epi=None, extras=(), out_dtypes=(F32,), colshard=False,
           tm=None, tn=1024, tk=2048, after=None):
    afters = [] if after is None else [after]
    if mode == "nn":
        (M, K), (K2, N) = a.shape, b.shape
    elif mode == "nt":
        (M, K), (N, K2) = a.shape, b.shape
    else:
        (K, M), (K2, N) = a.shape, b.shape
    assert K == K2, (name, a.shape, b.shape, mode)
    if tm is None:
        tm = 1024 if mode == "tn" else 512
    tm = _tile(M, tm)
    tn = N // N_DEV if colshard else _tile(N, tn, LANE)
    tk = _tile(K, tk, LANE)
    nk = K // tk
    grid = (M // tm, N // tn, nk)
    if mode == "nn":
        a_spec = pl.BlockSpec((tm, tk), lambda i, j, k: (i, k))
        b_spec = pl.BlockSpec((tk, tn), lambda i, j, k: (k, j))
        ca, cb = 1, 0
    elif mode == "nt":
        a_spec = pl.BlockSpec((tm, tk), lambda i, j, k: (i, k))
        b_spec = pl.BlockSpec((tn, tk), lambda i, j, k: (j, k))
        ca, cb = 1, 1
    else:
        a_spec = pl.BlockSpec((tk, tm), lambda i, j, k: (k, i))
        b_spec = pl.BlockSpec((tk, tn), lambda i, j, k: (k, j))
        ca, cb = 0, 0
    ex_specs = []
    for e in extras:
        if e.shape[0] == 1:
            ex_specs.append(pl.BlockSpec((1, tn), lambda i, j, k: (0, j)))
        else:
            assert e.shape == (M, N), (name, e.shape)
            ex_specs.append(pl.BlockSpec((tm, tn), lambda i, j, k: (i, j)))
    if colshard:
        out_shape = [jax.ShapeDtypeStruct((N_DEV, M, tn), dt) for dt in out_dtypes]
        out_specs = [pl.BlockSpec((None, tm, tn), lambda i, j, k: (j, i, 0)) for _ in out_dtypes]
    else:
        out_shape = [jax.ShapeDtypeStruct((M, N), dt) for dt in out_dtypes]
        out_specs = [pl.BlockSpec((tm, tn), lambda i, j, k: (i, j)) for _ in out_dtypes]
    n_ex, n_out = len(extras), len(out_dtypes)

    def body(*refs):
        a_ref, b_ref = refs[0], refs[1]
        ex_refs = refs[2:2 + n_ex]
        first_out = 2 + n_ex + len(afters)
        o_refs = refs[first_out:first_out + n_out]
        acc_ref = refs[-1]
        k = pl.program_id(2)
        part = _dg(a_ref[...], b_ref[...], ca, cb, False)

        @pl.when(k == 0)
        def _():
            acc_ref[...] = part

        @pl.when(k > 0)
        def _():
            acc_ref[...] += part

        @pl.when(k == nk - 1)
        def _():
            acc = acc_ref[...]
            res = (acc,) if epi is None else epi(acc, *[r[...] for r in ex_refs])
            for o_ref, r in zip(o_refs, res):
                o_ref[...] = r.astype(o_ref.dtype)

    outs = pl.pallas_call(
        body, name=name, grid=grid,
        in_specs=[a_spec, b_spec] + ex_specs + [ANY] * len(afters),
        out_specs=out_specs, out_shape=out_shape,
        scratch_shapes=[pltpu.VMEM((tm, tn), F32)],
        compiler_params=pltpu.CompilerParams(
            dimension_semantics=("parallel", "parallel", "arbitrary"),
            vmem_limit_bytes=VMEM_LIMIT),
    )(a, b, *extras, *afters)
    return outs[0] if n_out == 1 else tuple(outs)


def matmul_nt_sum(name, pairs, tm=512, tn=1024, after=None):
    M, N = pairs[0][0].shape[0], pairs[0][1].shape[0]
    tm, tn = _tile(M, tm), _tile(N, tn, LANE)
    afters = [] if after is None else [after]
    in_specs, operands = [], []
    for a, b in pairs:
        assert a.shape == (M, b.shape[1]) and b.shape[0] == N, (name, a.shape, b.shape)
        in_specs += [pl.BlockSpec((tm, a.shape[1]), lambda i, j: (i, 0)),
                     pl.BlockSpec((tn, b.shape[1]), lambda i, j: (j, 0))]
        operands += [a, b]
    n = len(pairs)

    def body(*refs):
        acc = _dg(refs[0][...], refs[1][...], 1, 1, False)
        for p in range(1, n):
            acc = acc + _dg(refs[2 * p][...], refs[2 * p + 1][...], 1, 1, False)
        refs[-1][...] = acc

    return pl.pallas_call(
        body, name=name, grid=(M // tm, N // tn),
        in_specs=in_specs + [ANY] * len(afters),
        out_specs=pl.BlockSpec((tm, tn), lambda i, j: (i, j)),
        out_shape=jax.ShapeDtypeStruct((M, N), F32),
        compiler_params=pltpu.CompilerParams(
            dimension_semantics=("parallel", "parallel"), vmem_limit_bytes=VMEM_LIMIT),
    )(*operands, *afters)


def rowcall(name, fn, rows, consts, out_rows, out_accs=(), tr=256, sp=None, R=None, after=None,
            out_lead=None, into=None):
    afters = ([] if after is None else [after]) + ([] if into is None else list(into))
    rows = [r if isinstance(r, tuple) else (r, None) for r in rows]
    R = rows[0][0].shape[-2] if R is None else R
    tr = _tile(R, tr)
    n = R // tr
    in_specs = []
    for arr, lead in rows:
        C = arr.shape[-1]
        if lead is None:
            assert arr.shape[-2] == R, (name, arr.shape, R)
            in_specs.append(pl.BlockSpec((tr, C), lambda i, *s: (i, 0)))
        elif callable(lead):
            in_specs.append(pl.BlockSpec((tr, C), lambda i, *s, lead=lead: (lead(i, n, *s), 0)))
        else:
            assert arr.shape[-2] == R, (name, arr.shape, R)
            in_specs.append(pl.BlockSpec((None, tr, C), lambda i, *s, lead=lead: (lead, i, 0)))
    for c in consts:
        in_specs.append(pl.BlockSpec(c.shape, lambda i, *s, nd=c.ndim: (0,) * nd))
    if out_lead is None:
        out_shape = [jax.ShapeDtypeStruct((R, C), dt) for C, dt in out_rows]
        out_specs = [pl.BlockSpec((tr, C), lambda i, *s: (i, 0)) for C, _ in out_rows]
    else:
        n_slab, slab = out_lead
        out_shape = [jax.ShapeDtypeStruct((n_slab, R, C), dt) for C, dt in out_rows]
        out_specs = [pl.BlockSpec((None, tr, C), lambda i, *s: (slab, i, 0)) for C, _ in out_rows]
    for shp in out_accs:
        out_shape.append(jax.ShapeDtypeStruct(shp, F32))
        out_specs.append(pl.BlockSpec(shp, lambda i, *s, nd=len(shp): (0,) * nd))
    n_in, n_row, n_acc = len(rows) + len(consts), len(out_rows), len(out_accs)
    n_sp = 0 if sp is None else 1

    def body(*refs):
        refs = refs[n_sp:]
        ins = [r[...] for r in refs[:n_in]]
        res = fn(*ins)
        if not isinstance(res, (tuple, list)):
            res = (res,)
        o_refs = refs[n_in + len(afters):]
        for o_ref, r in zip(o_refs[:n_row], res[:n_row]):
            o_ref[...] = r.astype(o_ref.dtype)
        if n_acc:
            first = pl.program_id(0) == 0
            for o_ref, r in zip(o_refs[n_row:], res[n_row:]):
                r = r.astype(F32).reshape(o_ref.shape)

                @pl.when(first)
                def _(o_ref=o_ref, r=r):
                    o_ref[...] = r

                @pl.when(jnp.logical_not(first))
                def _(o_ref=o_ref, r=r):
                    o_ref[...] += r

    params = pltpu.CompilerParams(dimension_semantics=("arbitrary",), vmem_limit_bytes=VMEM_LIMIT)
    operands = [a for a, _ in rows] + list(consts) + afters
    in_specs = in_specs + [ANY] * len(afters)
    aliases = {} if into is None else {n_sp + len(operands) - len(into) + k: k for k in range(len(into))}
    if sp is None:
        outs = pl.pallas_call(body, name=name, grid=(n,), in_specs=in_specs, out_specs=out_specs,
                              out_shape=out_shape, input_output_aliases=aliases,
                              compiler_params=params)(*operands)
    else:
        outs = pl.pallas_call(
            body, name=name, out_shape=out_shape, compiler_params=params, input_output_aliases=aliases,
            grid_spec=pltpu.PrefetchScalarGridSpec(
                num_scalar_prefetch=1, grid=(n,), in_specs=in_specs, out_specs=out_specs),
        )(sp, *operands)
    return outs[0] if len(outs) == 1 else tuple(outs)


def rms_tile(x, g):
    x = x.astype(F32)
    return x * lax.rsqrt(jnp.mean(x * x, axis=-1, keepdims=True) + EPS) * g


def gelu(x):
    return 0.5 * x * (1.0 + lax.erf(x * (1.0 / math.sqrt(2.0))))


def ln_tile(x, g, b):
    mu = jnp.mean(x, axis=-1, keepdims=True)
    xc = x - mu
    return xc * lax.rsqrt(jnp.mean(xc * xc, axis=-1, keepdims=True) + EPS) * g + b


def lane_groups(fn, width, *arrs):
    n = arrs[0].shape[-1] // width
    outs = [fn(*[a[:, i * width:(i + 1) * width] for a in arrs]) for i in range(n)]
    return jnp.concatenate(outs, axis=-1)


def mixa_post_tile(o, z, o_norm):
    dv = o_norm.shape[-1]
    on = lane_groups(lambda t: rms_tile(t, o_norm), dv, o)
    return on * jax.nn.silu(z)


def mixb_tile(uv, ln_g, ln_b, w_s, bs_t):
    G = w_s.shape[0]
    gw = ln_g.shape[-1]
    dg = gw // G
    tr = uv.shape[0]
    u = gelu(uv[:, :gw])
    vg = gelu(uv[:, gw:])
    ii = lax.broadcasted_iota(jnp.int32, (B_BLOCK, B_BLOCK), 0)
    jj = lax.broadcasted_iota(jnp.int32, (B_BLOCK, B_BLOCK), 1)
    mask = (jj // CHUNK) <= (ii // CHUNK)
    cols = []
    for g in range(G):
        sl = slice(g * dg, (g + 1) * dg)
        vn = ln_tile(vg[:, sl], ln_g[:, sl], ln_b[:, sl])
        wm = jnp.where(mask, w_s[g], 0.0)
        blocks = []
        for m in range(tr // B_BLOCK):
            blk = vn[m * B_BLOCK:(m + 1) * B_BLOCK, :]
            blocks.append(mm(wm, blk, 1, 0) + bs_t[:, g:g + 1])
        mixed = blocks[0] if len(blocks) == 1 else jnp.concatenate(blocks, axis=0)
        cols.append(u[:, sl] * mixed)
    return jnp.concatenate(cols, axis=-1)


def bgc_tile(ba, alog_row, dtb_row):
    tr = ba.shape[0]
    beta = jax.nn.sigmoid(ba)
    g = -jnp.exp(alog_row) * jax.nn.softplus(ba + dtb_row)
    ii = lax.broadcasted_iota(jnp.int32, (tr, tr), 0)
    jj = lax.broadcasted_iota(jnp.int32, (tr, tr), 1)
    tri = jnp.where((ii // CHUNK == jj // CHUNK) & (jj <= ii), 1.0, 0.0).astype(F32)
    gc = mm(tri, g, 1, 0, True)
    return beta, gc


def make_bgc(H):
    def f(ba, alog_row, dtb_row):
        beta, gc = bgc_tile(ba, alog_row, dtb_row)
        lane = lax.broadcasted_iota(jnp.int32, ba.shape, 1)
        return jnp.where(lane < H, beta, jnp.where(lane < 2 * H, gc, 0.0))
    return f


def loss_tile(x, g, target):
    y = rms_tile(x, g)
    err = y - target
    return 0.5 * jnp.sum(jnp.mean(err * err, axis=-1, keepdims=True), axis=0, keepdims=True)


def adamw_tile(w, g, m, v):
    m = ADAM_B1 * m + (1.0 - ADAM_B1) * g
    v = ADAM_B2 * v + (1.0 - ADAM_B2) * (g * g)
    m_hat = m / (1.0 - ADAM_B1 ** ADAM_STEP)
    v_hat = v / (1.0 - ADAM_B2 ** ADAM_STEP)
    delta = -ADAM_LR * (m_hat / (jnp.sqrt(v_hat) + ADAM_EPS) + ADAM_WD * w)
    return delta, m, v


CONV_ROWS = 32


def _shifted_copies(src, sh, rows):
    for b in range(SUBLANE):
        sh[b] = src[pl.ds(b, rows), :]


def _window(sh, off, rows):
    b = off % SUBLANE
    return sh[b, pl.ds(off - b, rows), :]


def _conv_rows(out, sh, w_ref, offsets, rows):
    for r0 in range(0, rows, CONV_ROWS):
        rc = min(CONV_ROWS, rows - r0)
        acc = w_ref[0:1, :] * _window(sh, offsets[0] + r0, rc)
        for k in range(1, len(offsets)):
            acc = acc + w_ref[k:k + 1, :] * _window(sh, offsets[k] + r0, rc)
        out[r0:r0 + rc, :] = acc


def _conv_wgrad(dsrc, d0, sh, offsets, rows):
    dws = []
    for off in offsets:
        acc = None
        for r0 in range(0, rows, CONV_ROWS):
            rc = min(CONV_ROWS, rows - r0)
            prod = dsrc[d0 + r0:d0 + r0 + rc, :] * _window(sh, off + r0, rc)
            for g in range(0, rc, SUBLANE):
                part = prod[g:g + SUBLANE, :]
                acc = part if acc is None else acc + part
        dws.append(jnp.sum(acc, axis=0, keepdims=True))
    return jnp.concatenate(dws, axis=0)


def _conv_specs(T, tr, hb, cw, col_blocks, rev):
    n = T // tr

    def ri(i):
        return (n - 1 - i) if rev else i

    tile_specs, halo_specs = [], []
    for off in col_blocks:
        tile_specs.append(pl.BlockSpec((tr, cw), lambda j, i, off=off: (ri(i), j + off)))
        halo_specs.append(pl.BlockSpec(
            (hb, cw), lambda j, i, off=off: (jnp.maximum(ri(i) * (tr // hb) - 1, 0), j + off)))
    return n, ri, tile_specs, halo_specs


def conv_fwd(name, x, w, consts, pre, post, col_blocks, n_out, K, out_dtype=F32, tr=256, hb=8, cw=512):
    T = x.shape[0]
    C = w.shape[1]
    tr, cw = _tile(T, tr, hb), min(cw, C)
    nb = len(col_blocks)
    n, ri, tile_specs, halo_specs = _conv_specs(T, tr, hb, cw, col_blocks, False)
    w_spec = pl.BlockSpec((K, cw), lambda j, i: (0, j))
    c_specs = [pl.BlockSpec((1, cw), lambda j, i: (0, j)) for _ in consts]

    def body(*refs):
        tiles = [r[...] for r in refs[:nb]]
        halos = [r[...] for r in refs[nb:2 * nb]]
        w_ref = refs[2 * nb]
        cs = [r[...] for r in refs[2 * nb + 1:2 * nb + 1 + len(consts)]]
        o_refs = refs[2 * nb + 1 + len(consts):-3]
        pbuf, shp, cbuf = refs[-3:]
        i = pl.program_id(1)
        pbuf[0:hb, :] = jnp.where(i > 0, pre(*halos), 0.0)
        pbuf[hb:hb + tr, :] = pre(*tiles)
        pbuf[hb + tr:hb + tr + SUBLANE, :] = jnp.zeros((SUBLANE, cw), F32)
        _shifted_copies(pbuf, shp, hb + tr)
        _conv_rows(cbuf, shp, w_ref, [hb - (K - 1) + k for k in range(K)], tr)
        res = post(cbuf[...], pl.program_id(0), *cs)
        for o_ref, r in zip(o_refs, res):
            o_ref[...] = r.astype(o_ref.dtype)

    outs = pl.pallas_call(
        body, name=name, grid=(C // cw, n),
        in_specs=tile_specs + halo_specs + [w_spec] + c_specs,
        out_specs=[pl.BlockSpec((tr, cw), lambda j, i: (i, j)) for _ in range(n_out)],
        out_shape=[jax.ShapeDtypeStruct((T, C), out_dtype) for _ in range(n_out)],
        scratch_shapes=[pltpu.VMEM((hb + tr + SUBLANE, cw), F32), pltpu.VMEM((SUBLANE, hb + tr, cw), F32),
                        pltpu.VMEM((tr, cw), F32)],
        compiler_params=pltpu.CompilerParams(
            dimension_semantics=("parallel", "arbitrary"), vmem_limit_bytes=VMEM_LIMIT),
    )(*([x] * nb), *([x] * nb), w, *consts)
    return outs[0] if n_out == 1 else tuple(outs)


def conv_bwd(name, x, w, consts, grads, pre, post, col_blocks, K, tr=256, hb=8, cw=512, recompute=True):
    T, Cx = x.shape
    C = w.shape[1]
    tr, cw = _tile(T, tr, hb), min(cw, C)
    nb = len(col_blocks)
    n, ri, tile_specs, halo_specs = _conv_specs(T, tr, hb, cw, col_blocks, True)
    w_spec = pl.BlockSpec((K, cw), lambda j, i: (0, j))
    c_specs = [pl.BlockSpec((1, cw), lambda j, i: (0, j)) for _ in consts]
    g_specs = [pl.BlockSpec((tr, cw), lambda j, i: (ri(i), j)) for _ in grads]
    nc, ng = len(consts), len(grads)

    def body(*refs):
        p = 0
        tile_refs = refs[p:p + nb]; p += nb
        halo_refs = refs[p:p + nb]; p += nb
        w_ref = refs[p]; p += 1
        cs = [r[...] for r in refs[p:p + nc]]; p += nc
        gs = [r[...] for r in refs[p:p + ng]]; p += ng
        dx_refs = refs[p:p + nb]; p += nb
        sum_refs = refs[p:p + nb]; p += nb
        dw_ref = refs[p]; p += 1
        dc_refs = refs[p:p + nc]; p += nc
        pbuf, dbuf, ebuf, carry, shp, shd, cbuf = refs[p:p + 7]
        i = pl.program_id(1)
        first = i == 0
        tiles = [r[...] for r in tile_refs]
        halos = [r[...] for r in halo_refs]
        p_tile, vjp_pre = jax.vjp(pre, *tiles)
        pbuf[0:hb, :] = jnp.where(ri(i) > 0, pre(*halos), 0.0)
        pbuf[hb:hb + tr, :] = p_tile
        pbuf[hb + tr:hb + tr + SUBLANE, :] = jnp.zeros((SUBLANE, cw), F32)
        _shifted_copies(pbuf, shp, hb + tr)
        taps = [hb - (K - 1) + k for k in range(K)]
        if recompute:
            _conv_rows(cbuf, shp, w_ref, taps, tr)
            c = cbuf[...]
        else:
            c = jnp.zeros((tr, cw), F32)
        cid = pl.program_id(0)
        _, vjp_post = jax.vjp(lambda c_, *cs_: post(c_, cid, *cs_), c, *cs)
        dres = vjp_post(tuple(g.astype(F32) for g in gs))
        dbuf[0:hb, :] = jnp.zeros((hb, cw), F32)
        dbuf[hb:hb + tr, :] = dres[0]
        dbuf[hb + tr:hb + tr + hb + SUBLANE, :] = jnp.zeros((hb + SUBLANE, cw), F32)
        _shifted_copies(dbuf, shd, hb + tr + hb)
        _conv_rows(ebuf, shd, w_ref, [K - 1 - k for k in range(K)], hb + tr)
        dw = _conv_wgrad(dbuf, hb, shp, taps, tr)

        @pl.when(jnp.logical_not(first))
        def _():
            ebuf[tr:tr + hb, :] += carry[...]

        carry[...] = ebuf[0:hb, :]
        dtiles = vjp_pre(ebuf[hb:hb + tr, :])
        for r, s, d in zip(dx_refs, sum_refs, dtiles):
            r[...] = d.astype(r.dtype)
            ds_ = jnp.sum(d, axis=0, keepdims=True)

            @pl.when(first)
            def _(s=s, ds_=ds_):
                s[...] = ds_

            @pl.when(jnp.logical_not(first))
            def _(s=s, ds_=ds_):
                s[...] += ds_

        accs = [(dw_ref, dw)] + [(r, d) for r, d in zip(dc_refs, dres[1:])]
        for r, d in accs:
            @pl.when(first)
            def _(r=r, d=d):
                r[...] = d

            @pl.when(jnp.logical_not(first))
            def _(r=r, d=d):
                r[...] += d

    n_cb = C // cw
    outs = pl.pallas_call(
        body, name=name, grid=(n_cb, n),
        in_specs=tile_specs + halo_specs + [w_spec] + c_specs + g_specs,
        out_specs=([pl.BlockSpec((tr, cw), lambda j, i: (ri(i), j)) for _ in col_blocks]
                   + [pl.BlockSpec((1, cw), lambda j, i: (0, j)) for _ in col_blocks]
                   + [pl.BlockSpec((K, cw), lambda j, i: (0, j))]
                   + [pl.BlockSpec((1, cw), lambda j, i: (0, j)) for _ in consts]),
        out_shape=([jax.ShapeDtypeStruct((T, C), BF) for _ in col_blocks]
                   + [jax.ShapeDtypeStruct((1, C), F32) for _ in col_blocks]
                   + [jax.ShapeDtypeStruct((K, C), F32)]
                   + [jax.ShapeDtypeStruct((1, C), F32) for _ in consts]),
        scratch_shapes=[pltpu.VMEM((hb + tr + SUBLANE, cw), F32), pltpu.VMEM((hb + tr + hb + SUBLANE, cw), F32),
                        pltpu.VMEM((hb + tr, cw), F32), pltpu.VMEM((hb, cw), F32),
                        pltpu.VMEM((SUBLANE, hb + tr, cw), F32), pltpu.VMEM((SUBLANE, hb + tr + hb, cw), F32),
                        pltpu.VMEM((tr, cw), F32)],
        compiler_params=pltpu.CompilerParams(
            dimension_semantics=("parallel", "arbitrary"), vmem_limit_bytes=VMEM_LIMIT),
    )(*([x] * nb), *([x] * nb), w, *consts, *grads)
    dxs = outs[:nb]
    sums = outs[nb:2 * nb]
    dw = outs[2 * nb]
    dcs = outs[2 * nb + 1:]
    return dxs, sums, dw, dcs


def make_qkv_post(dk, cw, n_qk_chunks):
    def l2(t):
        return t * lax.rsqrt(jnp.sum(t * t, axis=-1, keepdims=True) + EPS)

    def post(c, cid):
        s = jax.nn.silu(c)
        normed = lane_groups(l2, dk, s)
        return (jnp.where(cid < n_qk_chunks, normed, s),)
    return post


def glu_pre(za, zb):
    return za * jax.nn.sigmoid(zb)


def bias_post(c, cid, b):
    return (c + b,)


def _col_to_row(col):
    C = col.shape[-2]
    ii = lax.broadcasted_iota(jnp.int32, (C, C), 0)
    jj = lax.broadcasted_iota(jnp.int32, (C, C), 1)
    wide = jnp.broadcast_to(col, col.shape[:-1] + (C,))
    return jnp.sum(jnp.where(ii == jj, wide, 0.0), axis=-2, keepdims=True)


@jax.custom_vjp
def solve_with_inverse(a, rhs, x):
    return mm(x, rhs, 1, 0, True)


def _swi_fwd(a, rhs, x):
    sol = mm(x, rhs, 1, 0, True)
    return sol, (x, sol)


def _swi_bwd(res, dsol):
    x, sol = res
    drhs = mm(x, dsol, 0, 0, True)
    da = -mm(drhs, sol, 1, 1, True)
    return da, drhs, jnp.zeros_like(x)


solve_with_inverse.defvjp(_swi_fwd, _swi_bwd)


def unit_lower_inverse(a):
    C = a.shape[-1]
    ii = lax.broadcasted_iota(jnp.int32, (C, C), 0)
    jj = lax.broadcasted_iota(jnp.int32, (C, C), 1)
    x = jnp.where(ii == jj, 1.0, 0.0).astype(F32) - a
    p = mm(a, a, 1, 0, True)
    steps = int(math.log2(C)) - 1
    for s in range(steps):
        x = x + mm(x, p, 1, 0, True)
        if s < steps - 1:
            p = mm(p, p, 1, 0, True)
    return x


def dn_masks(C):
    ii = lax.broadcasted_iota(jnp.int32, (C, C), 0)
    jj = lax.broadcasted_iota(jnp.int32, (C, C), 1)
    return ii >= jj, ii > jj


def dn_pre(q, k, v, beta, gc):
    C, dk = q.shape[-2:]
    tri, strict = dn_masks(C)
    q = q * (dk ** -0.5)
    diff = gc - _col_to_row(gc)
    decay = jnp.where(tri, jnp.exp(jnp.where(tri, diff, 0.0)), 0.0)
    kb = k * beta
    vb = v * beta
    a = jnp.where(strict, mm(kb, k, 1, 1) * decay, 0.0)
    eg = jnp.exp(gc)
    rhs = jnp.concatenate([vb, kb * eg], axis=-1)
    attn = mm(q, k, 1, 1) * decay
    qd = q * eg
    g_last = gc[..., C - 1:C, :]
    kt = k * jnp.exp(g_last - gc)
    gl = jnp.exp(g_last)
    return a, rhs, attn, qd, kt, gl


def dn_chunk(q, k, v, beta, gc, state, x):
    dv = v.shape[-1]
    a, rhs, attn, qd, kt, gl = dn_pre(q, k, v, beta, gc)
    sol = solve_with_inverse(a, rhs, x)
    u, w = sol[..., :dv], sol[..., dv:]
    vn = u - mm(w, state, 1, 0)
    o = mm(qd, state, 1, 0) + mm(attn, vn, 1, 0)
    new_state = state * gl + mm(kt, vn, 0, 0)
    return o, new_state


def _by_head(q_ref, k_ref, v_ref, bg, H, dk, dv):
    qv = jnp.stack([q_ref[:, h * dk:(h + 1) * dk] for h in range(H)])
    kv = jnp.stack([k_ref[:, h * dk:(h + 1) * dk] for h in range(H)])
    vv = jnp.stack([v_ref[:, h * dv:(h + 1) * dv] for h in range(H)])
    beta = jnp.stack([bg[:, h:h + 1] for h in range(H)])
    gc = jnp.stack([bg[:, H + h:H + h + 1] for h in range(H)])
    return qv, kv, vv, beta, gc


def deltanet_fwd(qkv, bgc, H):
    T = qkv.shape[0]
    dk = dv = qkv.shape[1] // (3 * H)
    N = T // CHUNK

    def body(q_ref, k_ref, v_ref, bgc_ref, o_ref, x_ref, s_ref, state):
        @pl.when(pl.program_id(0) == 0)
        def _():
            state[...] = jnp.zeros((H, dk, dv), F32)

        qv, kv, vv, beta, gc = _by_head(q_ref, k_ref, v_ref, bgc_ref[...], H, dk, dv)
        a = dn_pre(qv, kv, vv, beta, gc)[0]
        x = unit_lower_inverse(a)
        s = state[...]
        o, s_new = dn_chunk(qv, kv, vv, beta, gc, s, x)
        for h in range(H):
            o_ref[:, h * dv:(h + 1) * dv] = o[h]
        x_ref[...] = x
        s_ref[...] = s
        state[...] = s_new

    return pl.pallas_call(
        body, name="deltanet_fwd", grid=(N,),
        in_specs=[pl.BlockSpec((CHUNK, H * dk), lambda n: (n, 0)),
                  pl.BlockSpec((CHUNK, H * dk), lambda n: (n, 1)),
                  pl.BlockSpec((CHUNK, H * dv), lambda n: (n, 2)),
                  pl.BlockSpec((CHUNK, LANE), lambda n: (n, 0))],
        out_specs=[pl.BlockSpec((CHUNK, H * dv), lambda n: (n, 0)),
                   pl.BlockSpec((None, H, CHUNK, CHUNK), lambda n: (n, 0, 0, 0)),
                   pl.BlockSpec((None, H, dk, dv), lambda n: (n, 0, 0, 0))],
        out_shape=[jax.ShapeDtypeStruct((T, H * dv), F32),
                   jax.ShapeDtypeStruct((N, H, CHUNK, CHUNK), F32),
                   jax.ShapeDtypeStruct((N, H, dk, dv), F32)],
        scratch_shapes=[pltpu.VMEM((H, dk, dv), F32)],
        compiler_params=pltpu.CompilerParams(
            dimension_semantics=("arbitrary",), vmem_limit_bytes=VMEM_LIMIT),
    )(qkv, qkv, qkv, bgc)


def deltanet_bwd(qkv, bgc, xinv, states, do, H):
    T = qkv.shape[0]
    dk = dv = qkv.shape[1] // (3 * H)
    N = T // CHUNK

    def body(q_ref, k_ref, v_ref, bgc_ref, x_ref, s_ref, do_ref, dqkv_ref, dbgc_ref, dstate):
        @pl.when(pl.program_id(0) == 0)
        def _():
            dstate[...] = jnp.zeros((H, dk, dv), F32)

        qv, kv, vv, beta, gc = _by_head(q_ref, k_ref, v_ref, bgc_ref[...], H, dk, dv)
        do = jnp.stack([do_ref[:, h * dv:(h + 1) * dv] for h in range(H)])
        _, vjp = jax.vjp(dn_chunk, qv, kv, vv, beta, gc, s_ref[...], x_ref[...])
        dq, dk_, dv_, dbeta, dgc, ds, _ = vjp((do, dstate[...]))
        dstate[...] = ds
        lane = lax.broadcasted_iota(jnp.int32, (CHUNK, LANE), 1)
        dbgc = jnp.zeros((CHUNK, LANE), F32)
        for h in range(H):
            dqkv_ref[:, h * dk:(h + 1) * dk] = dq[h]
            dqkv_ref[:, (H + h) * dk:(H + h + 1) * dk] = dk_[h]
            dqkv_ref[:, (2 * H + h) * dk:(2 * H + h + 1) * dk] = dv_[h]
            dbgc = dbgc + jnp.where(lane == h, dbeta[h], 0.0) + jnp.where(lane == h + H, dgc[h], 0.0)
        dbgc_ref[...] = dbgc

    rn = lambda n: N - 1 - n
    return pl.pallas_call(
        body, name="deltanet_bwd", grid=(N,),
        in_specs=[pl.BlockSpec((CHUNK, H * dk), lambda n: (rn(n), 0)),
                  pl.BlockSpec((CHUNK, H * dk), lambda n: (rn(n), 1)),
                  pl.BlockSpec((CHUNK, H * dv), lambda n: (rn(n), 2)),
                  pl.BlockSpec((CHUNK, LANE), lambda n: (rn(n), 0)),
                  pl.BlockSpec((None, H, CHUNK, CHUNK), lambda n: (rn(n), 0, 0, 0)),
                  pl.BlockSpec((None, H, dk, dv), lambda n: (rn(n), 0, 0, 0)),
                  pl.BlockSpec((CHUNK, H * dv), lambda n: (rn(n), 0))],
        out_specs=[pl.BlockSpec((CHUNK, 3 * H * dk), lambda n: (rn(n), 0)),
                   pl.BlockSpec((CHUNK, LANE), lambda n: (rn(n), 0))],
        out_shape=[jax.ShapeDtypeStruct((T, 3 * H * dk), F32),
                   jax.ShapeDtypeStruct((T, LANE), F32)],
        scratch_shapes=[pltpu.VMEM((H, dk, dv), F32)],
        compiler_params=pltpu.CompilerParams(
            dimension_semantics=("arbitrary",), vmem_limit_bytes=VMEM_LIMIT),
    )(qkv, qkv, qkv, bgc, xinv, states, do)


def _place():
    x, y, c = lax.axis_index("x"), lax.axis_index("y"), lax.axis_index("c")
    chips = [(1 - x, y), (x, 1 - y), (1 - x, 1 - y)]
    return x, y, c, chips


def all_gather(name, shards):
    na = len(shards)

    def body(*refs):
        ins, outs = refs[:na], refs[na:2 * na]
        send_sems, recv_sems, local_sems = refs[2 * na:]
        x, y, c, chips = _place()
        me, sibling = (x, y, c), (x, y, 1 - c)

        def copy(a, k, block, to, src=None):
            dst = outs[a].at[4 * block[0] + 2 * block[1] + block[2]]
            return pltpu.make_async_remote_copy(
                src_ref=dst if src is None else src, dst_ref=dst,
                send_sem=send_sems.at[a, k], recv_sem=recv_sems.at[a, k],
                device_id=to, device_id_type=MESH)

        mine, first, passed = [], [], []
        for a in range(na):
            cp = pltpu.make_async_copy(ins[a], outs[a].at[4 * x + 2 * y + c], local_sems.at[a])
            cp.start()
            mine.append(cp)
        for a in range(na):
            cps = [copy(a, 0, me, sibling, src=ins[a])]
            cps += [copy(a, 1 + j, me, (*chip, c), src=ins[a]) for j, chip in enumerate(chips)]
            for cp in cps:
                cp.start()
            first += cps
        for a in range(na):
            for j, chip in enumerate(chips):
                copy(a, 1 + j, (*chip, c), me).wait_recv()
                cp = copy(a, 4 + j, (*chip, c), sibling)
                cp.start()
                passed.append(cp)
        for a in range(na):
            copy(a, 0, sibling, me).wait_recv()
            for j, chip in enumerate(chips):
                copy(a, 4 + j, (*chip, 1 - c), me).wait_recv()
        for cp in first + passed:
            cp.wait_send()
        for cp in mine:
            cp.wait()

    outs = pl.pallas_call(
        body, name=name,
        in_specs=[ANY] * na, out_specs=[ANY] * na,
        out_shape=[jax.ShapeDtypeStruct((N_DEV,) + s.shape, s.dtype) for s in shards],
        scratch_shapes=[pltpu.SemaphoreType.DMA((na, 7)), pltpu.SemaphoreType.DMA((na, 7)),
                        pltpu.SemaphoreType.DMA((na,))],
    )(*shards)
    return list(outs)


HBM_SPEC = pl.BlockSpec(memory_space=pltpu.HBM)
SEM_SPEC = pl.BlockSpec(memory_space=pltpu.SEMAPHORE)
EFFECT = pltpu.SideEffectType.DATAFLOW_SIDE_EFFECTING


def _descriptors(plan, bufs, send_sems, recv_sems):
    return [pltpu.make_async_remote_copy(src_ref=src, dst_ref=dst, send_sem=send_sems.at[k],
                                         recv_sem=recv_sems.at[k], device_id=dev, device_id_type=MESH)
            for k, (src, dst, dev) in enumerate(plan(bufs))]


def split_start(name, bufs, plan, n, after):
    nb = len(bufs)

    def body(*refs):
        for cp in _descriptors(plan, refs[:nb], refs[nb + 1], refs[nb + 2]):
            cp.start()
        refs[-1][...] = jnp.zeros((SUBLANE, LANE), F32)

    outs = pl.pallas_call(
        body, name=name,
        out_shape=(pltpu.SemaphoreType.DMA((n,)), pltpu.SemaphoreType.DMA((n,)),
                   *[pltpu.HBM(b.shape, b.dtype) for b in bufs],
                   jax.ShapeDtypeStruct((SUBLANE, LANE), F32)),
        in_specs=[HBM_SPEC] * nb + [ANY],
        out_specs=(SEM_SPEC, SEM_SPEC, *[HBM_SPEC] * nb, pl.BlockSpec(memory_space=pltpu.VMEM)),
        input_output_aliases={i: 2 + i for i in range(nb)},
        compiler_params=pltpu.CompilerParams(has_side_effects=EFFECT),
    )(*[pltpu.with_memory_space_constraint(b, pltpu.HBM) for b in bufs], after)
    return outs[0], outs[1], list(outs[2:2 + nb]), outs[-1]


def split_wait(name, send_sems, recv_sems, bufs, plan, after):
    nb = len(bufs)

    def body(*refs):
        cps = _descriptors(plan, refs[:nb], refs[nb], refs[nb + 1])
        for cp in cps:
            cp.wait_recv()
        for cp in cps:
            cp.wait_send()
        refs[-1][...] = jnp.zeros((SUBLANE, LANE), F32)

    afters = list(after) if isinstance(after, (list, tuple)) else [after]
    outs = pl.pallas_call(
        body, name=name,
        out_shape=[pltpu.HBM(b.shape, b.dtype) for b in bufs] + [jax.ShapeDtypeStruct((SUBLANE, LANE), F32)],
        in_specs=[HBM_SPEC] * nb + [SEM_SPEC, SEM_SPEC] + [ANY] * len(afters),
        out_specs=[HBM_SPEC] * nb + [pl.BlockSpec(memory_space=pltpu.VMEM)],
        input_output_aliases={i: i for i in range(nb)},
        compiler_params=pltpu.CompilerParams(has_side_effects=EFFECT),
    )(*bufs, send_sems, recv_sems, *afters)
    return list(outs[:nb]), outs[-1]


def _block(px, py, pc):
    return 4 * px + 2 * py + pc


def plan_gather_ici(na):
    def plan(bufs):
        x, y, c, chips = _place()
        out = []
        for a in range(na):
            dst = bufs[na + a].at[_block(x, y, c)]
            out.append((bufs[a], dst, (x, y, 1 - c)))
            out += [(bufs[a], dst, (px, py, c)) for px, py in chips]
        return out
    return plan


def plan_gather_pass(na):
    def plan(bufs):
        x, y, c, chips = _place()
        out = []
        for a in range(na):
            for px, py in chips:
                blk = bufs[a].at[_block(px, py, c)]
                out.append((blk, blk, (x, y, 1 - c)))
        return out
    return plan


def plan_reduce_d2d(na):
    def plan(bufs):
        x, y, c, _ = _place()
        return [(bufs[a].at[2 * s + (1 - c)], bufs[na + a].at[s], (x, y, 1 - c))
                for a in range(na) for s in range(4)]
    return plan


def plan_reduce_ici(na):
    def plan(bufs):
        x, y, c, chips = _place()
        return [(bufs[a].at[2 * px + py], bufs[na + a].at[j], (px, py, c))
                for a in range(na) for j, (px, py) in enumerate(chips)]
    return plan


def place_own(name, land, shard, dev):
    r, c = shard.shape
    tr = _tile(r, 512)

    def body(sp_ref, s_ref, land_ref, o_ref):
        o_ref[...] = s_ref[...]

    return pl.pallas_call(
        body, name=name, out_shape=jax.ShapeDtypeStruct(land.shape, land.dtype),
        grid_spec=pltpu.PrefetchScalarGridSpec(
            num_scalar_prefetch=1, grid=(r // tr,),
            in_specs=[pl.BlockSpec((tr, c), lambda i, s: (i, 0)), ANY],
            out_specs=pl.BlockSpec((None, tr, c), lambda i, s: (s[0], i, 0))),
        input_output_aliases={2: 0},
        compiler_params=pltpu.CompilerParams(dimension_semantics=("arbitrary",)),
    )(dev, shard, land)


def pack(arrs, row_mult=SUBLANE):
    pieces = []
    for a in arrs:
        f = a.reshape(-1).astype(F32)
        pad = (-f.shape[0]) % LANE
        if pad:
            f = jnp.concatenate([f, jnp.zeros((pad,), F32)])
        pieces.append(f)
    flat = jnp.concatenate(pieces)
    rows = flat.shape[0] // LANE
    pad_rows = (-rows) % row_mult
    if pad_rows:
        flat = jnp.concatenate([flat, jnp.zeros((pad_rows * LANE,), F32)])
    return flat.reshape(-1, LANE)


def unpack(buf, shapes):
    flat = buf.reshape(-1)
    outs, off = [], 0
    for shp in shapes:
        n = int(np.prod(shp))
        outs.append(flat[off:off + n].reshape(shp))
        off += n + ((-n) % LANE)
    return outs


def _vjp_rows(fn, n_row_in, n_cot):
    def bwd(*args):
        rows = args[:n_row_in]
        cots = args[n_row_in:n_row_in + n_cot]
        consts = args[n_row_in + n_cot:]
        out, vjp = jax.vjp(fn, *rows, *consts)
        if isinstance(out, (tuple, list)):
            cot = tuple(c.astype(o.dtype) for c, o in zip(cots, out))
        else:
            cot = cots[0].astype(out.dtype)
        return vjp(cot)
    return bwd


def rms_fwd(name, x, g):
    return rowcall(name, rms_tile, [x], [g], [(x.shape[1], BF)])


def rms_bwd(name, x, g, dh, dres, after=None):
    D = x.shape[1]
    vj = _vjp_rows(rms_tile, 1, 1)

    def f(x_, dh_, dres_, g_):
        dx, dg = vj(x_, dh_, g_)
        dx = dx + dres_
        return dx, dx, dg
    return rowcall(name, f, [x, dh, dres], [g], [(D, F32), (D, BF)], [(1, D)], after=after)


def kernel(x, e_norm, e_w_in, e_conv_w, e_a_log, e_dt_bias, e_o_norm, e_ln_g, e_ln_b, e_w_s, e_b_s, e_w_out, o_norm, o_pw1, o_pw1_b, o_dw, o_dw_b, o_ln_g, o_ln_b, o_pw2, o_pw2_b, f_norm, f_w1, f_w2, final_norm, loss_target, m_e_norm, m_e_w_in, m_e_conv_w, m_e_a_log, m_e_dt_bias, m_e_o_norm, m_e_ln_g, m_e_ln_b, m_e_w_s, m_e_b_s, m_e_w_out, m_o_norm, m_o_pw1, m_o_pw1_b, m_o_dw, m_o_dw_b, m_o_ln_g, m_o_ln_b, m_o_pw2, m_o_pw2_b, m_f_norm, m_f_w1, m_f_w2, m_final_norm, v_e_norm, v_e_w_in, v_e_conv_w, v_e_a_log, v_e_dt_bias, v_e_o_norm, v_e_ln_g, v_e_ln_b, v_e_w_s, v_e_b_s, v_e_w_out, v_o_norm, v_o_pw1, v_o_pw1_b, v_o_dw, v_o_dw_b, v_o_ln_g, v_o_ln_b, v_o_pw2, v_o_pw2_b, v_f_norm, v_f_w1, v_f_w2, v_final_norm):
    names = ['e_norm', 'e_w_in', 'e_conv_w', 'e_a_log', 'e_dt_bias', 'e_o_norm', 'e_ln_g', 'e_ln_b', 'e_w_s', 'e_b_s', 'e_w_out', 'o_norm', 'o_pw1', 'o_pw1_b', 'o_dw', 'o_dw_b', 'o_ln_g', 'o_ln_b', 'o_pw2', 'o_pw2_b', 'f_norm', 'f_w1', 'f_w2', 'final_norm']
    W = dict(zip(names, [e_norm, e_w_in, e_conv_w, e_a_log, e_dt_bias, e_o_norm, e_ln_g, e_ln_b, e_w_s, e_b_s, e_w_out, o_norm, o_pw1, o_pw1_b, o_dw, o_dw_b, o_ln_g, o_ln_b, o_pw2, o_pw2_b, f_norm, f_w1, f_w2, final_norm]))
    Mo = dict(zip(names, [m_e_norm, m_e_w_in, m_e_conv_w, m_e_a_log, m_e_dt_bias, m_e_o_norm, m_e_ln_g, m_e_ln_b, m_e_w_s, m_e_b_s, m_e_w_out, m_o_norm, m_o_pw1, m_o_pw1_b, m_o_dw, m_o_dw_b, m_o_ln_g, m_o_ln_b, m_o_pw2, m_o_pw2_b, m_f_norm, m_f_w1, m_f_w2, m_final_norm]))
    Vo = dict(zip(names, [v_e_norm, v_e_w_in, v_e_conv_w, v_e_a_log, v_e_dt_bias, v_e_o_norm, v_e_ln_g, v_e_ln_b, v_e_w_s, v_e_b_s, v_e_w_out, v_o_norm, v_o_pw1, v_o_pw1_b, v_o_dw, v_o_dw_b, v_o_ln_g, v_o_ln_b, v_o_pw2, v_o_pw2_b, v_f_norm, v_f_w1, v_f_w2, v_final_norm]))

    T, D = x.shape[1], x.shape[2]
    H = e_a_log.shape[-1]
    dv = e_o_norm.shape[-1]
    dk = dv
    G = e_w_s.shape[1]
    AQK, AV, BW = H * dk, H * dv, e_ln_g.shape[-1]
    AQKV = 2 * AQK + AV
    in_cols = AQKV + AV + 2 * H + 2 * BW
    KA = e_conv_w.shape[1]
    KC = o_dw.shape[1]
    L = f_norm.shape[0]
    dev = 4 * lax.axis_index("x") + 2 * lax.axis_index("y") + lax.axis_index("c")
    x2d = x.reshape(T, D)
    tgt = loss_target.reshape(T, D)

    dev_sp = dev.astype(jnp.int32).reshape(1)
    where = jnp.stack([lax.axis_index("c"), 2 * lax.axis_index("x") + lax.axis_index("y")]).astype(jnp.int32)
    row = lambda a: a.reshape(1, -1).astype(F32)
    en_row = row(e_norm)

    def gather_begin(tag, shards, after):
        na = len(shards)
        lands = [lax.empty((N_DEV,) + s.shape, s.dtype) for s in shards]
        ss, rs, bufs, tok = split_start(f"gather{tag}_ici_start", shards + lands, plan_gather_ici(na), 4 * na, after)
        return (na, ss, rs, bufs), tok

    def gather_pass(tag, h, after):
        na, ss, rs, bufs = h
        bufs, tok = split_wait(f"gather{tag}_ici_wait", ss, rs, bufs, plan_gather_ici(na), after)
        ss, rs, lands, tok = split_start(f"gather{tag}_pass_start", bufs[na:], plan_gather_pass(na), 3 * na, tok)
        return (na, ss, rs, bufs[:na], lands), tok

    def gather_end(tag, h, after):
        na, ss, rs, shards, lands = h
        lands, _ = split_wait(f"gather{tag}_pass_wait", ss, rs, lands, plan_gather_pass(na), after)
        return [place_own(f"gather{tag}_own{a}", lands[a], shards[a], dev_sp) for a in range(na)]

    small_sharded = ['e_conv_w', 'o_norm', 'o_pw1_b', 'o_dw', 'o_dw_b', 'o_ln_g', 'o_ln_b', 'o_pw2_b']
    sm = all_gather("gather_small", [pack([W[n][0]]) for n in small_sharded])

    bfw = lambda w: w.astype(BF)
    hA0, tok = gather_begin("0", [bfw(e_w_in[0])], sm[0])
    hA1, tok = gather_begin("1", [bfw(e_w_out[0]), bfw(f_w1[0]), bfw(f_w2[0])], tok)
    hA2, tok = gather_begin("2", [bfw(o_pw1[0]), bfw(o_pw2[0]), bfw(f_w1[1]), bfw(f_w2[1])], tok)
    h0 = rowcall("rms_e", rms_tile, [x2d], [en_row], [(D, BF)], after=tok)

    full = {}
    for n, g in zip(small_sharded, sm):
        shp = W[n][0].shape
        blocks = [unpack(g[d], [shp])[0] for d in range(N_DEV)]
        full[n] = jnp.concatenate(blocks, axis=-1)
    conv_w = full['e_conv_w']
    on_row, pw1_b_row = row(full['o_norm']), row(full['o_pw1_b'])
    dw_w, dw_b_row = full['o_dw'], row(full['o_dw_b'])
    oln_g_row, oln_b_row, pw2_b_row = row(full['o_ln_g']), row(full['o_ln_b']), row(full['o_pw2_b'])
    small_names = ['e_norm', 'e_conv_w', 'e_a_log', 'e_dt_bias', 'e_o_norm', 'e_ln_g', 'e_ln_b', 'e_w_s', 'e_b_s',
                   'o_norm', 'o_pw1_b', 'o_dw', 'o_dw_b', 'o_ln_g', 'o_ln_b', 'o_pw2_b', 'f_norm', 'final_norm']
    packed_wmv = [pack([A[n] for n in small_names], 256) for A in (W, Mo, Vo)]

    hB0, tok = gather_pass("0", hA0, [h0, conv_w, dw_w, pw1_b_row] + packed_wmv)
    (g_win,) = gather_end("0", hB0, tok)
    win = jnp.moveaxis(g_win, 0, 1).reshape(D, in_cols)
    w_qkv, w_z = win[:, :AQKV], win[:, AQKV:AQKV + AV]
    w_ba = jnp.pad(win[:, AQKV + AV:AQKV + AV + 2 * H], ((0, 0), (0, LANE - 2 * H)))
    w_uv = win[:, AQKV + AV + 2 * H:]

    alog_row = jnp.pad(row(e_a_log), ((0, 0), (H, LANE - 2 * H)))
    dtb_row = jnp.pad(row(e_dt_bias), ((0, 0), (H, LANE - 2 * H)))
    eon_row = row(e_o_norm)
    eln_g_row, eln_b_row = row(e_ln_g), row(e_ln_b)
    w_s = e_w_s[0]
    bs_t = e_b_s[0].T
    fn_rows = [row(f_norm[l]) for l in range(L)]
    fin_row = row(final_norm)

    qkv_raw = matmul("proj_qkv", h0, w_qkv, "nn")
    z_gate = matmul("proj_z", h0, w_z, "nn")
    ba = matmul("proj_ba", h0, w_ba, "nn")
    uv = matmul("proj_uv", h0, w_uv, "nn")

    cwa = min(512, AQKV)
    qkv_post = make_qkv_post(dk, cwa, 2 * AQK // cwa)
    ident = lambda t: t
    qkv = conv_fwd("qkv_conv", qkv_raw, conv_w, [], ident, qkv_post, [0], 1, KA, cw=cwa)
    bgc_fn = make_bgc(H)
    bgc = rowcall("bgc", bgc_fn, [ba], [alog_row, dtb_row], [(LANE, F32)])
    o_dn, xinv, states = deltanet_fwd(qkv, bgc, H)
    hB1, tok = gather_pass("1", hA1, o_dn)

    def mix_tile(o, z, uv_, o_norm_, ln_g, ln_b, w_s_, bs_t_):
        return jnp.concatenate([mixa_post_tile(o, z, o_norm_), mixb_tile(uv_, ln_g, ln_b, w_s_, bs_t_)], axis=-1)
    mix_consts = [eon_row, eln_g_row, eln_b_row, w_s, bs_t]
    mix = rowcall("mix", mix_tile, [o_dn, z_gate, uv], mix_consts, [(AV + BW, BF)], after=tok)
    g_wout, g_w1_0, g_w2_0 = gather_end("1", hB1, mix)
    wout = g_wout.reshape(-1, D)
    w1 = [jnp.moveaxis(g_w1_0, 0, 1).reshape(D, -1), None]
    w2 = [g_w2_0.reshape(-1, D), None]
    add_epi = lambda acc, r: (acc + r,)
    x1 = matmul("out_proj", mix, wout, "nn", epi=add_epi, extras=[x2d])

    def relu2_epi(acc):
        r = jnp.maximum(acc, 0.0)
        return r * r, r

    hf0 = rms_fwd("rms_f0", x1, fn_rows[0])
    hB2, tok = gather_pass("2", hA2, hf0)
    a2_0, ar_0 = matmul("ffn_up0", hf0, w1[0], "nn", epi=relu2_epi, out_dtypes=(BF, BF), after=tok)
    x2 = matmul("ffn_down0", a2_0, w2[0], "nn", epi=add_epi, extras=[x1])
    ffn0 = (hf0, a2_0, ar_0)
    g_pw1, g_pw2, g_w1_1, g_w2_1 = gather_end("2", hB2, x2)
    pw1 = jnp.moveaxis(g_pw1, 0, 1).reshape(D, 2 * D)
    pw2 = g_pw2.reshape(D, D)
    w1[1] = jnp.moveaxis(g_w1_1, 0, 1).reshape(D, -1)
    w2[1] = g_w2_1.reshape(-1, D)

    h1 = rms_fwd("rms_o", x2, on_row)
    bias_epi = lambda acc, b: (acc + b,)
    zc = matmul("pw1", h1, pw1, "nn", epi=bias_epi, extras=[pw1_b_row])
    cwc = min(512, D)
    ncb = D // cwc
    cconv = conv_fwd("dw_conv", zc, dw_w, [dw_b_row], glu_pre, bias_post, [0, ncb], 1, KC, hb=32, cw=cwc)
    ln_silu = lambda c, g, b: jax.nn.silu(ln_tile(c, g, b))
    s_act = rowcall("ln_silu", ln_silu, [cconv], [oln_g_row, oln_b_row], [(D, BF)])
    x3 = matmul("pw2", s_act, pw2, "nn", epi=lambda acc, r, b: (r + (acc + b),), extras=[x2, pw2_b_row])
    hf1 = rms_fwd("rms_f1", x3, fn_rows[1])
    a2_1, ar_1 = matmul("ffn_up1", hf1, w1[1], "nn", epi=relu2_epi, out_dtypes=(BF, BF))
    x4 = matmul("ffn_down1", a2_1, w2[1], "nn", epi=add_epi, extras=[x3])
    ffn1 = (hf1, a2_1, ar_1)

    def loss_bwd_tile(x_, t_, g_):
        l, vjp = jax.vjp(lambda a, b: loss_tile(a, b, t_), x_, g_)
        dx, dg = vjp(jnp.ones_like(l))
        return dx, dx, l, dg
    dx4, dx4_b, loss_part, d_final = rowcall("loss_head", loss_bwd_tile, [x4, tgt], [fin_row],
                                             [(D, F32), (D, BF)], [(1, 1), (1, D)])
    loss = lax.psum(loss_part[0, 0], ("x", "y", "c"))

    def reduce_begin(tag, grads, after):
        na = len(grads)
        lands = [lax.empty((4,) + g.shape[1:], g.dtype) for g in grads]
        ss, rs, bufs, tok = split_start(f"reduce{tag}_d2d_start", grads + lands, plan_reduce_d2d(na), 4 * na, after)
        return (na, ss, rs, bufs), tok

    def reduce_mid(tag, h, after):
        na, ss, rs, bufs = h
        bufs, tok = split_wait(f"reduce{tag}_d2d_wait", ss, rs, bufs, plan_reduce_d2d(na), after)
        parts = []
        for a, (g, rc) in enumerate(zip(bufs[:na], bufs[na:])):
            r, c = g.shape[1], g.shape[2]
            mine = lambda i, n, s: (2 * (i // (n // 4)) + s[0]) * (n // 4) + i % (n // 4)
            parts.append(rowcall(f"chip_sum{tag}_{a}", lambda p, q: (p + q,),
                                 [(g.reshape(N_DEV * r, c), mine), rc.reshape(4 * r, c)], [],
                                 [(c, BF)], tr=_tile(r, 512), sp=where, R=4 * r).reshape(4, r, c))
        lands = [lax.empty((3,) + p.shape[1:], p.dtype) for p in parts]
        ss, rs, bufs, tok = split_start(f"reduce{tag}_ici_start", parts + lands, plan_reduce_ici(na), 3 * na, tok)
        return (na, ss, rs, bufs), tok

    res = {}

    def reduce_end(tag, h, after, targets):
        na, ss, rs, bufs = h
        bufs, _ = split_wait(f"reduce{tag}_ici_wait", ss, rs, bufs, plan_reduce_ici(na), after)
        for a, (part, fin, (n, l)) in enumerate(zip(bufs[:na], bufs[na:], targets)):
            def f(p0, p1, p2, p3, w_, m_, v_):
                g = ((p0.astype(F32) + p1.astype(F32)) + p2.astype(F32)) + p3.astype(F32)
                return (g,) + adamw_tile(w_, g, m_, v_)
            r, C = fin.shape[-2], fin.shape[-1]
            own = lambda i, n_, s: s[1] * n_ + i
            res[n] = rowcall(f"adam{tag}_{a}", f, [(part.reshape(4 * r, C), own), (fin, 0), (fin, 1), (fin, 2),
                                                   (W[n], l), (Mo[n], l), (Vo[n], l)], [],
                             [(C, F32)] * 4, tr=256, sp=where, R=r,
                             out_lead=(W[n].shape[0], l), into=res.get(n))

    def tie(small, tok):
        return small + tok[0:1, 0:1]

    dscale_epi = lambda acc, r: (acc * (2.0 * r.astype(F32)),)
    d_fnorm = [None] * L

    dpre1 = matmul("ffn_down_dx1", dx4_b, w2[1], "nt", epi=dscale_epi, extras=[ar_1], out_dtypes=(BF,))
    dw2_1 = matmul("ffn_down_dw1", a2_1, dx4_b, "tn")
    dw1_1 = matmul("ffn_up_dw1", hf1, dpre1, "tn", colshard=True)
    hD1, tok = reduce_begin("1", [dw1_1, dw2_1.reshape(N_DEV, -1, D)], dpre1)
    dhf1 = matmul("ffn_up_dx1", dpre1, w1[1], "nt", after=tok, tk=4096)
    dx3, dx3_b, d_fnorm[1] = rms_bwd("rms_f_bwd1", x3, fn_rows[1], dhf1, dx4)

    ds_act = matmul("pw2_dx", dx3_b, pw2, "nt")
    hI1, tok = reduce_mid("1", hD1, ds_act)
    d_pw2 = matmul("pw2_dw", s_act, dx3_b, "tn", after=tok)
    ln_silu_bwd = _vjp_rows(ln_silu, 1, 1)

    def ln_silu_bwd_tile(c_, ds_, dx3_, g_, b_):
        dc, dg, db = ln_silu_bwd(c_, ds_, g_, b_)
        return dc, dg, db, jnp.sum(dx3_, axis=0, keepdims=True)
    dcconv, d_oln_g, d_oln_b, d_pw2_b = rowcall(
        "ln_silu_bwd", ln_silu_bwd_tile, [cconv, ds_act, dx3], [oln_g_row, oln_b_row],
        [(D, F32)], [(1, D), (1, D), (1, D)])
    (dza, dzb), (sza, szb), d_dw, (d_dw_b,) = conv_bwd(
        "dw_conv_bwd", zc, dw_w, [tie(dw_b_row, tok)], [dcconv], glu_pre, bias_post, [0, ncb], KC, hb=32, cw=cwc, tr=128,
        recompute=False)
    dzc = jnp.concatenate([dza, dzb], axis=-1)
    d_pw1_b = jnp.concatenate([sza, szb], axis=-1)
    d_pw1 = matmul("pw1_dw", h1, dzc, "tn", colshard=True)
    dh1 = matmul("pw1_dx", dzc, pw1, "nt")
    dx2, dx2_b, d_onorm = rms_bwd("rms_o_bwd", x2, on_row, dh1, dx3)
    reduce_end("1", hI1, dx2, [('f_w1', 1), ('f_w2', 1)])

    hD2, tok = reduce_begin("2", [d_pw1, d_pw2.reshape(N_DEV, -1, D)], dx2)
    dpre0 = matmul("ffn_down_dx0", dx2_b, w2[0], "nt", epi=dscale_epi, extras=[ar_0], out_dtypes=(BF,), after=tok)
    dw2_0 = matmul("ffn_down_dw0", a2_0, dx2_b, "tn")
    hI2, tok = reduce_mid("2", hD2, dw2_0)
    dw1_0 = matmul("ffn_up_dw0", hf0, dpre0, "tn", colshard=True, after=tok)
    dhf0 = matmul("ffn_up_dx0", dpre0, w1[0], "nt", tk=4096)
    dx1, dx1_b, d_fnorm[0] = rms_bwd("rms_f_bwd0", x1, fn_rows[0], dhf0, dx2)
    reduce_end("2", hI2, dx1, [('o_pw1', 0), ('o_pw2', 0)])

    dmix = matmul("out_proj_dx", dx1_b, wout, "nt")
    d_wout = matmul("out_proj_dw", mix, dx1_b, "tn")
    hD3, tok = reduce_begin("3", [dw1_0, dw2_0.reshape(N_DEV, -1, D), d_wout.reshape(N_DEV, -1, D)], dmix)
    do_dn, dz_gate, duv, d_eon, d_eln_g, d_eln_b, d_ws, d_bs_t = rowcall(
        "mix_bwd", _vjp_rows(mix_tile, 3, 1), [o_dn, z_gate, uv, dmix], mix_consts,
        [(AV, F32), (AV, BF), (2 * BW, BF)], [(1, dv), (1, BW), (1, BW), w_s.shape, bs_t.shape], after=tok)
    dqkv, dbgc = deltanet_bwd(qkv, bgc, xinv, states, do_dn, H)
    hI3, tok = reduce_mid("3", hD3, dbgc)
    bgc_bwd = _vjp_rows(bgc_fn, 1, 1)
    dba, d_alog_row, d_dtb_row = rowcall(
        "bgc_bwd", bgc_bwd, [ba, dbgc], [alog_row, dtb_row], [(LANE, BF)], [(1, LANE), (1, LANE)])
    (dqkv_raw,), _, d_conv_w, _ = conv_bwd(
        "qkv_conv_bwd", qkv_raw, tie(conv_w, tok), [], [dqkv], ident, qkv_post, [0], KA, cw=cwa)

    dw_qkv = matmul("proj_qkv_dw", h0, dqkv_raw, "tn")
    dw_z = matmul("proj_z_dw", h0, dz_gate, "tn")
    dw_ba = matmul("proj_ba_dw", h0, dba, "tn")
    dw_uv = matmul("proj_uv_dw", h0, duv, "tn")
    d_win = jnp.concatenate([dw_qkv, dw_z, dw_ba[:, :2 * H], dw_uv], axis=-1)
    G_win = jnp.moveaxis(d_win.reshape(D, N_DEV, in_cols // N_DEV), 1, 0)
    reduce_end("3", hI3, dw_uv, [('f_w1', 0), ('f_w2', 0), ('e_w_out', 0)])
    hD4, tok = reduce_begin("4", [G_win], dw_uv)
    dh0 = matmul_nt_sum("proj_dx", [(dqkv_raw, w_qkv), (dz_gate, w_z), (dba, w_ba), (duv, w_uv)], after=tok)
    hI4, tok = reduce_mid("4", hD4, dh0)
    grad_x, _, d_enorm = rms_bwd("rms_e_bwd", x2d, en_row, dh0, dx1, after=tok)

    d_alog = d_alog_row[:, H:2 * H]
    d_dtb = d_dtb_row[:, H:2 * H]
    small_grads = [d_enorm, d_conv_w, d_alog, d_dtb, d_eon, d_eln_g, d_eln_b, d_ws, d_bs_t.T,
                   d_onorm, d_pw1_b, d_dw, d_dw_b, d_oln_g, d_oln_b, d_pw2_b,
                   jnp.concatenate(d_fnorm, axis=0), d_final]
    full_shapes = [g.shape for g in small_grads]
    gs_all = all_gather("gather_small_grads", [pack(small_grads, 256)])[0]

    def sum8(*ps):
        s = ps[0]
        for p in ps[1:]:
            s = s + p
        return (s,)
    gs_sum = rowcall("small_sum", sum8, [(gs_all, d) for d in range(N_DEV)], [], [(LANE, F32)])
    reduce_end("4", hI4, gs_sum, [('e_w_in', 0)])
    g_full = dict(zip(small_names, unpack(gs_sum, full_shapes)))
    g_loc = {}
    for n in small_names:
        g = g_full[n]
        if n in small_sharded:
            per = g.shape[-1] // N_DEV
            g = lax.dynamic_slice_in_dim(g, dev * per, per, axis=-1)
        g_loc[n] = g.reshape(W[n].shape)
    packs = [packed_wmv[0], pack([g_loc[n] for n in small_names], 256), packed_wmv[1], packed_wmv[2]]
    d_s, m_s, v_s = rowcall("adam_small", adamw_tile, packs, [], [(LANE, F32)] * 3)
    shapes = [W[n].shape for n in small_names]
    for n, d_, m_, v_ in zip(small_names, unpack(d_s, shapes), unpack(m_s, shapes), unpack(v_s, shapes)):
        res[n] = (g_loc[n], d_, m_, v_)

    grads = [res[n][0] for n in names]
    deltas = [res[n][1] for n in names]
    new_m = [res[n][2] for n in names]
    new_v = [res[n][3] for n in names]
    return (loss, grad_x.reshape(x.shape), *grads, *deltas, *new_m, *new_v)
```

```python
import functools
import math

import jax
import jax.numpy as jnp
import numpy as np
from jax import lax
from jax.experimental import pallas as pl
from jax.experimental.pallas import tpu as pltpu

F32 = jnp.float32
BF = jnp.bfloat16
EPS = 1e-6
CHUNK = 64
B_BLOCK = 128
LANE = 128
SUBLANE = 8
N_DEV = 8
VMEM_LIMIT = 56 * 1024 * 1024

ADAM_LR = 0.001
ADAM_B1 = 0.9
ADAM_B2 = 0.999
ADAM_EPS = 1e-08
ADAM_WD = 0.01
ADAM_STEP = 10

MESH = pl.DeviceIdType.MESH
ANY = pl.BlockSpec(memory_space=pl.ANY)


def _tile(n, pref, mult=SUBLANE):
    if n <= pref:
        return n
    t = (pref // mult) * mult
    while t >= mult:
        if n % t == 0:
            return t
        t -= mult
    return n


def _dg(a, b, ca, cb, hi):
    nb = a.ndim - 2
    batch = tuple(range(nb))
    dims = (((ca + nb,), (cb + nb,)), (batch, batch))
    if hi:
        return lax.dot_general(a.astype(F32), b.astype(F32), dims,
                               precision=lax.Precision.HIGHEST, preferred_element_type=F32)
    return lax.dot_general(a.astype(BF), b.astype(BF), dims, preferred_element_type=F32)


@functools.partial(jax.custom_vjp, nondiff_argnums=(2, 3, 4))
def mm(a, b, ca, cb, hi=False):
    return _dg(a, b, ca, cb, hi)


def _mm_fwd(a, b, ca, cb, hi):
    return _dg(a, b, ca, cb, hi), (a, b)


def _mm_bwd(ca, cb, hi, res, g):
    a, b = res
    if ca == 1:
        da = mm(g, b, 1, 1 - cb, hi)
    else:
        da = mm(b, g, 1 - cb, 1, hi)
    if cb == 0:
        db = mm(a, g, 1 - ca, 0, hi)
    else:
        db = mm(g, a, 0, 1 - ca, hi)
    return da.astype(a.dtype), db.astype(b.dtype)


mm.defvjp(_mm_fwd, _mm_bwd)


def matmul(name, a, b, mode, epi=None, extras=(), out_dtypes=(F32,), colshard=False,
           tm=None, tn=1024, tk=2048, after=None):
    afters = [] if after is None else [after]
    if mode == "nn":
        (M, K), (K2, N) = a.shape, b.shape
    elif mode == "nt":
        (M, K), (N, K2) = a.shape, b.shape
    else:
        (K, M), (K2, N) = a.shape, b.shape
    assert K == K2, (name, a.shape, b.shape, mode)
    if tm is None:
        tm = 1024
    tm = _tile(M, tm)
    tn = N // N_DEV if colshard else _tile(N, tn, LANE)
    tk = _tile(K, tk, LANE)
    nk = K // tk
    grid = (M // tm, N // tn, nk)
    if mode == "nn":
        a_spec = pl.BlockSpec((tm, tk), lambda i, j, k: (i, k))
        b_spec = pl.BlockSpec((tk, tn), lambda i, j, k: (k, j))
        ca, cb = 1, 0
    elif mode == "nt":
        a_spec = pl.BlockSpec((tm, tk), lambda i, j, k: (i, k))
        b_spec = pl.BlockSpec((tn, tk), lambda i, j, k: (j, k))
        ca, cb = 1, 1
    else:
        a_spec = pl.BlockSpec((tk, tm), lambda i, j, k: (k, i))
        b_spec = pl.BlockSpec((tk, tn), lambda i, j, k: (k, j))
        ca, cb = 0, 0
    ex_specs = []
    for e in extras:
        if e.shape[0] == 1:
            ex_specs.append(pl.BlockSpec((1, tn), lambda i, j, k: (0, j)))
        else:
            assert e.shape == (M, N), (name, e.shape)
            ex_specs.append(pl.BlockSpec((tm, tn), lambda i, j, k: (i, j)))
    if colshard:
        out_shape = [jax.ShapeDtypeStruct((N_DEV, M, tn), dt) for dt in out_dtypes]
        out_specs = [pl.BlockSpec((None, tm, tn), lambda i, j, k: (j, i, 0)) for _ in out_dtypes]
    else:
        out_shape = [jax.ShapeDtypeStruct((M, N), dt) for dt in out_dtypes]
        out_specs = [pl.BlockSpec((tm, tn), lambda i, j, k: (i, j)) for _ in out_dtypes]
    n_ex, n_out = len(extras), len(out_dtypes)

    def body(*refs):
        a_ref, b_ref = refs[0], refs[1]
        ex_refs = refs[2:2 + n_ex]
        first_out = 2 + n_ex + len(afters)
        o_refs = refs[first_out:first_out + n_out]
        part = _dg(a_ref[...], b_ref[...], ca, cb, False)

        def finish(acc):
            res = (acc,) if epi is None else epi(acc, *[r[...] for r in ex_refs])
            for o_ref, r in zip(o_refs, res):
                o_ref[...] = r.astype(o_ref.dtype)

        if nk == 1:
            finish(part)
            return
        acc_ref = refs[-1]
        k = pl.program_id(2)

        @pl.when(k == 0)
        def _():
            acc_ref[...] = part

        @pl.when(k > 0)
        def _():
            acc_ref[...] += part

        @pl.when(k == nk - 1)
        def _():
            finish(acc_ref[...])

    outs = pl.pallas_call(
        body, name=name, grid=grid,
        in_specs=[a_spec, b_spec] + ex_specs + [ANY] * len(afters),
        out_specs=out_specs, out_shape=out_shape,
        scratch_shapes=[pltpu.VMEM((tm, tn), F32)] if nk > 1 else [],
        compiler_params=pltpu.CompilerParams(
            dimension_semantics=("parallel", "parallel", "arbitrary"),
            vmem_limit_bytes=VMEM_LIMIT),
    )(a, b, *extras, *afters)
    return outs[0] if n_out == 1 else tuple(outs)


def matmul_nt_sum(name, pairs, tm=512, tn=1024, after=None):
    M, N = pairs[0][0].shape[0], pairs[0][1].shape[0]
    tm, tn = _tile(M, tm), _tile(N, tn, LANE)
    afters = [] if after is None else [after]
    in_specs, operands = [], []
    for a, b in pairs:
        assert a.shape == (M, b.shape[1]) and b.shape[0] == N, (name, a.shape, b.shape)
        in_specs += [pl.BlockSpec((tm, a.shape[1]), lambda i, j: (i, 0)),
                     pl.BlockSpec((tn, b.shape[1]), lambda i, j: (j, 0))]
        operands += [a, b]
    n = len(pairs)

    def body(*refs):
        acc = _dg(refs[0][...], refs[1][...], 1, 1, False)
        for p in range(1, n):
            acc = acc + _dg(refs[2 * p][...], refs[2 * p + 1][...], 1, 1, False)
        refs[-1][...] = acc

    return pl.pallas_call(
        body, name=name, grid=(M // tm, N // tn),
        in_specs=in_specs + [ANY] * len(afters),
        out_specs=pl.BlockSpec((tm, tn), lambda i, j: (i, j)),
        out_shape=jax.ShapeDtypeStruct((M, N), F32),
        compiler_params=pltpu.CompilerParams(
            dimension_semantics=("parallel", "parallel"), vmem_limit_bytes=VMEM_LIMIT),
    )(*operands, *afters)


def rowcall(name, fn, rows, consts, out_rows, out_accs=(), tr=256, sp=None, R=None, after=None,
            out_lead=None, into=None):
    afters = ([] if after is None else [after]) + ([] if into is None else list(into))
    rows = [r if isinstance(r, tuple) else (r, None) for r in rows]
    R = rows[0][0].shape[-2] if R is None else R
    tr = _tile(R, tr)
    n = R // tr
    in_specs = []
    for arr, lead in rows:
        C = arr.shape[-1]
        if lead is None:
            assert arr.shape[-2] == R, (name, arr.shape, R)
            in_specs.append(pl.BlockSpec((tr, C), lambda i, *s: (i, 0)))
        elif callable(lead):
            in_specs.append(pl.BlockSpec((tr, C), lambda i, *s, lead=lead: (lead(i, n, *s), 0)))
        else:
            assert arr.shape[-2] == R, (name, arr.shape, R)
            in_specs.append(pl.BlockSpec((None, tr, C), lambda i, *s, lead=lead: (lead, i, 0)))
    for c in consts:
        in_specs.append(pl.BlockSpec(c.shape, lambda i, *s, nd=c.ndim: (0,) * nd))
    if out_lead is None:
        out_shape = [jax.ShapeDtypeStruct((R, C), dt) for C, dt in out_rows]
        out_specs = [pl.BlockSpec((tr, C), lambda i, *s: (i, 0)) for C, _ in out_rows]
    else:
        n_slab, slab = out_lead
        out_shape = [jax.ShapeDtypeStruct((n_slab, R, C), dt) for C, dt in out_rows]
        out_specs = [pl.BlockSpec((None, tr, C), lambda i, *s: (slab, i, 0)) for C, _ in out_rows]
    for shp in out_accs:
        out_shape.append(jax.ShapeDtypeStruct(shp, F32))
        out_specs.append(pl.BlockSpec(shp, lambda i, *s, nd=len(shp): (0,) * nd))
    n_in, n_row, n_acc = len(rows) + len(consts), len(out_rows), len(out_accs)
    n_sp = 0 if sp is None else 1

    def body(*refs):
        refs = refs[n_sp:]
        ins = [r[...] for r in refs[:n_in]]
        res = fn(*ins)
        if not isinstance(res, (tuple, list)):
            res = (res,)
        o_refs = refs[n_in + len(afters):]
        for o_ref, r in zip(o_refs[:n_row], res[:n_row]):
            o_ref[...] = r.astype(o_ref.dtype)
        if n_acc:
            first = pl.program_id(0) == 0
            for o_ref, r in zip(o_refs[n_row:], res[n_row:]):
                r = r.astype(F32).reshape(o_ref.shape)

                @pl.when(first)
                def _(o_ref=o_ref, r=r):
                    o_ref[...] = r

                @pl.when(jnp.logical_not(first))
                def _(o_ref=o_ref, r=r):
                    o_ref[...] += r

    params = pltpu.CompilerParams(dimension_semantics=("arbitrary",), vmem_limit_bytes=VMEM_LIMIT)
    operands = [a for a, _ in rows] + list(consts) + afters
    in_specs = in_specs + [ANY] * len(afters)
    aliases = {} if into is None else {n_sp + len(operands) - len(into) + k: k for k in range(len(into))}
    if sp is None:
        outs = pl.pallas_call(body, name=name, grid=(n,), in_specs=in_specs, out_specs=out_specs,
                              out_shape=out_shape, input_output_aliases=aliases,
                              compiler_params=params)(*operands)
    else:
        outs = pl.pallas_call(
            body, name=name, out_shape=out_shape, compiler_params=params, input_output_aliases=aliases,
            grid_spec=pltpu.PrefetchScalarGridSpec(
                num_scalar_prefetch=1, grid=(n,), in_specs=in_specs, out_specs=out_specs),
        )(sp, *operands)
    return outs[0] if len(outs) == 1 else tuple(outs)


def rms_tile(x, g):
    x = x.astype(F32)
    return x * lax.rsqrt(jnp.mean(x * x, axis=-1, keepdims=True) + EPS) * g


def gelu(x):
    return 0.5 * x * (1.0 + lax.erf(x * (1.0 / math.sqrt(2.0))))


def ln_tile(x, g, b):
    mu = jnp.mean(x, axis=-1, keepdims=True)
    xc = x - mu
    return xc * lax.rsqrt(jnp.mean(xc * xc, axis=-1, keepdims=True) + EPS) * g + b


def lane_groups(fn, width, *arrs):
    n = arrs[0].shape[-1] // width
    outs = [fn(*[a[:, i * width:(i + 1) * width] for a in arrs]) for i in range(n)]
    return jnp.concatenate(outs, axis=-1)


def mixa_post_tile(o, z, o_norm):
    dv = o_norm.shape[-1]
    on = lane_groups(lambda t: rms_tile(t, o_norm), dv, o)
    return on * jax.nn.silu(z)


def mixb_tile(uv, ln_g, ln_b, w_s, bs_t):
    G = w_s.shape[0]
    gw = ln_g.shape[-1]
    dg = gw // G
    tr = uv.shape[0]
    u = gelu(uv[:, :gw])
    vg = gelu(uv[:, gw:])
    ii = lax.broadcasted_iota(jnp.int32, (B_BLOCK, B_BLOCK), 0)
    jj = lax.broadcasted_iota(jnp.int32, (B_BLOCK, B_BLOCK), 1)
    mask = (jj // CHUNK) <= (ii // CHUNK)
    cols = []
    for g in range(G):
        sl = slice(g * dg, (g + 1) * dg)
        vn = ln_tile(vg[:, sl], ln_g[:, sl], ln_b[:, sl])
        wm = jnp.where(mask, w_s[g], 0.0)
        blocks = []
        for m in range(tr // B_BLOCK):
            blk = vn[m * B_BLOCK:(m + 1) * B_BLOCK, :]
            blocks.append(mm(wm, blk, 1, 0) + bs_t[:, g:g + 1])
        mixed = blocks[0] if len(blocks) == 1 else jnp.concatenate(blocks, axis=0)
        cols.append(u[:, sl] * mixed)
    return jnp.concatenate(cols, axis=-1)


def bgc_tile(ba, alog_row, dtb_row):
    tr = ba.shape[0]
    beta = jax.nn.sigmoid(ba)
    g = -jnp.exp(alog_row) * jax.nn.softplus(ba + dtb_row)
    ii = lax.broadcasted_iota(jnp.int32, (tr, tr), 0)
    jj = lax.broadcasted_iota(jnp.int32, (tr, tr), 1)
    tri = jnp.where((ii // CHUNK == jj // CHUNK) & (jj <= ii), 1.0, 0.0).astype(F32)
    gc = mm(tri, g, 1, 0, True)
    return beta, gc


def make_bgc(H):
    def f(ba, alog_row, dtb_row):
        beta, gc = bgc_tile(ba, alog_row, dtb_row)
        lane = lax.broadcasted_iota(jnp.int32, ba.shape, 1)
        return jnp.where(lane < H, beta, jnp.where(lane < 2 * H, gc, 0.0))
    return f


def loss_tile(x, g, target):
    y = rms_tile(x, g)
    err = y - target
    return 0.5 * jnp.sum(jnp.mean(err * err, axis=-1, keepdims=True), axis=0, keepdims=True)


def adamw_tile(w, g, m, v):
    m = ADAM_B1 * m + (1.0 - ADAM_B1) * g
    v = ADAM_B2 * v + (1.0 - ADAM_B2) * (g * g)
    m_hat = m / (1.0 - ADAM_B1 ** ADAM_STEP)
    v_hat = v / (1.0 - ADAM_B2 ** ADAM_STEP)
    delta = -ADAM_LR * (m_hat / (jnp.sqrt(v_hat) + ADAM_EPS) + ADAM_WD * w)
    return delta, m, v


CONV_ROWS = 32


def _shifted_copies(src, sh, rows):
    for b in range(SUBLANE):
        sh[b] = src[pl.ds(b, rows), :]


def _window(sh, off, rows):
    b = off % SUBLANE
    return sh[b, pl.ds(off - b, rows), :]


def _conv_rows(out, sh, w_ref, offsets, rows):
    for r0 in range(0, rows, CONV_ROWS):
        rc = min(CONV_ROWS, rows - r0)
        acc = w_ref[0:1, :] * _window(sh, offsets[0] + r0, rc)
        for k in range(1, len(offsets)):
            acc = acc + w_ref[k:k + 1, :] * _window(sh, offsets[k] + r0, rc)
        out[r0:r0 + rc, :] = acc


def _conv_wgrad(dsrc, d0, sh, offsets, rows):
    dws = []
    for off in offsets:
        acc = None
        for r0 in range(0, rows, CONV_ROWS):
            rc = min(CONV_ROWS, rows - r0)
            prod = dsrc[d0 + r0:d0 + r0 + rc, :] * _window(sh, off + r0, rc)
            for g in range(0, rc, SUBLANE):
                part = prod[g:g + SUBLANE, :]
                acc = part if acc is None else acc + part
        dws.append(jnp.sum(acc, axis=0, keepdims=True))
    return jnp.concatenate(dws, axis=0)


def _conv_specs(T, tr, hb, cw, col_blocks, rev):
    n = T // tr

    def ri(i):
        return (n - 1 - i) if rev else i

    tile_specs, halo_specs = [], []
    for off in col_blocks:
        tile_specs.append(pl.BlockSpec((tr, cw), lambda j, i, off=off: (ri(i), j + off)))
        halo_specs.append(pl.BlockSpec(
            (hb, cw), lambda j, i, off=off: (jnp.maximum(ri(i) * (tr // hb) - 1, 0), j + off)))
    return n, ri, tile_specs, halo_specs


def conv_fwd(name, x, w, consts, pre, post, col_blocks, n_out, K, out_dtype=F32, tr=256, hb=8, cw=512):
    T = x.shape[0]
    C = w.shape[1]
    tr, cw = _tile(T, tr, hb), min(cw, C)
    nb = len(col_blocks)
    n, ri, tile_specs, halo_specs = _conv_specs(T, tr, hb, cw, col_blocks, False)
    w_spec = pl.BlockSpec((K, cw), lambda j, i: (0, j))
    c_specs = [pl.BlockSpec((1, cw), lambda j, i: (0, j)) for _ in consts]

    def body(*refs):
        tiles = [r[...] for r in refs[:nb]]
        halos = [r[...] for r in refs[nb:2 * nb]]
        w_ref = refs[2 * nb]
        cs = [r[...] for r in refs[2 * nb + 1:2 * nb + 1 + len(consts)]]
        o_refs = refs[2 * nb + 1 + len(consts):-3]
        pbuf, shp, cbuf = refs[-3:]
        i = pl.program_id(1)
        pbuf[0:hb, :] = jnp.where(i > 0, pre(*halos), 0.0)
        pbuf[hb:hb + tr, :] = pre(*tiles)
        pbuf[hb + tr:hb + tr + SUBLANE, :] = jnp.zeros((SUBLANE, cw), F32)
        _shifted_copies(pbuf, shp, hb + tr)
        _conv_rows(cbuf, shp, w_ref, [hb - (K - 1) + k for k in range(K)], tr)
        res = post(cbuf[...], pl.program_id(0), *cs)
        for o_ref, r in zip(o_refs, res):
            o_ref[...] = r.astype(o_ref.dtype)

    outs = pl.pallas_call(
        body, name=name, grid=(C // cw, n),
        in_specs=tile_specs + halo_specs + [w_spec] + c_specs,
        out_specs=[pl.BlockSpec((tr, cw), lambda j, i: (i, j)) for _ in range(n_out)],
        out_shape=[jax.ShapeDtypeStruct((T, C), out_dtype) for _ in range(n_out)],
        scratch_shapes=[pltpu.VMEM((hb + tr + SUBLANE, cw), F32), pltpu.VMEM((SUBLANE, hb + tr, cw), F32),
                        pltpu.VMEM((tr, cw), F32)],
        compiler_params=pltpu.CompilerParams(
            dimension_semantics=("parallel", "arbitrary"), vmem_limit_bytes=VMEM_LIMIT),
    )(*([x] * nb), *([x] * nb), w, *consts)
    return outs[0] if n_out == 1 else tuple(outs)


def conv_bwd(name, x, w, consts, grads, pre, post, col_blocks, K, tr=256, hb=8, cw=512, recompute=True):
    T, Cx = x.shape
    C = w.shape[1]
    tr, cw = _tile(T, tr, hb), min(cw, C)
    nb = len(col_blocks)
    n, ri, tile_specs, halo_specs = _conv_specs(T, tr, hb, cw, col_blocks, True)
    w_spec = pl.BlockSpec((K, cw), lambda j, i: (0, j))
    c_specs = [pl.BlockSpec((1, cw), lambda j, i: (0, j)) for _ in consts]
    g_specs = [pl.BlockSpec((tr, cw), lambda j, i: (ri(i), j)) for _ in grads]
    nc, ng = len(consts), len(grads)

    def body(*refs):
        p = 0
        tile_refs = refs[p:p + nb]; p += nb
        halo_refs = refs[p:p + nb]; p += nb
        w_ref = refs[p]; p += 1
        cs = [r[...] for r in refs[p:p + nc]]; p += nc
        gs = [r[...] for r in refs[p:p + ng]]; p += ng
        dx_refs = refs[p:p + nb]; p += nb
        sum_refs = refs[p:p + nb]; p += nb
        dw_ref = refs[p]; p += 1
        dc_refs = refs[p:p + nc]; p += nc
        pbuf, dbuf, ebuf, carry, shp, shd, cbuf = refs[p:p + 7]
        i = pl.program_id(1)
        first = i == 0
        tiles = [r[...] for r in tile_refs]
        halos = [r[...] for r in halo_refs]
        p_tile, vjp_pre = jax.vjp(pre, *tiles)
        pbuf[0:hb, :] = jnp.where(ri(i) > 0, pre(*halos), 0.0)
        pbuf[hb:hb + tr, :] = p_tile
        pbuf[hb + tr:hb + tr + SUBLANE, :] = jnp.zeros((SUBLANE, cw), F32)
        _shifted_copies(pbuf, shp, hb + tr)
        taps = [hb - (K - 1) + k for k in range(K)]
        if recompute:
            _conv_rows(cbuf, shp, w_ref, taps, tr)
            c = cbuf[...]
        else:
            c = jnp.zeros((tr, cw), F32)
        cid = pl.program_id(0)
        _, vjp_post = jax.vjp(lambda c_, *cs_: post(c_, cid, *cs_), c, *cs)
        dres = vjp_post(tuple(g.astype(F32) for g in gs))
        dbuf[0:hb, :] = jnp.zeros((hb, cw), F32)
        dbuf[hb:hb + tr, :] = dres[0]
        dbuf[hb + tr:hb + tr + hb + SUBLANE, :] = jnp.zeros((hb + SUBLANE, cw), F32)
        _shifted_copies(dbuf, shd, hb + tr + hb)
        _conv_rows(ebuf, shd, w_ref, [K - 1 - k for k in range(K)], hb + tr)
        dw = _conv_wgrad(dbuf, hb, shp, taps, tr)

        @pl.when(jnp.logical_not(first))
        def _():
            ebuf[tr:tr + hb, :] += carry[...]

        carry[...] = ebuf[0:hb, :]
        dtiles = vjp_pre(ebuf[hb:hb + tr, :])
        for r, s, d in zip(dx_refs, sum_refs, dtiles):
            r[...] = d.astype(r.dtype)
            ds_ = jnp.sum(d, axis=0, keepdims=True)

            @pl.when(first)
            def _(s=s, ds_=ds_):
                s[...] = ds_

            @pl.when(jnp.logical_not(first))
            def _(s=s, ds_=ds_):
                s[...] += ds_

        accs = [(dw_ref, dw)] + [(r, d) for r, d in zip(dc_refs, dres[1:])]
        for r, d in accs:
            @pl.when(first)
            def _(r=r, d=d):
                r[...] = d

            @pl.when(jnp.logical_not(first))
            def _(r=r, d=d):
                r[...] += d

    n_cb = C // cw
    outs = pl.pallas_call(
        body, name=name, grid=(n_cb, n),
        in_specs=tile_specs + halo_specs + [w_spec] + c_specs + g_specs,
        out_specs=([pl.BlockSpec((tr, cw), lambda j, i: (ri(i), j)) for _ in col_blocks]
                   + [pl.BlockSpec((1, cw), lambda j, i: (0, j)) for _ in col_blocks]
                   + [pl.BlockSpec((K, cw), lambda j, i: (0, j))]
                   + [pl.BlockSpec((1, cw), lambda j, i: (0, j)) for _ in consts]),
        out_shape=([jax.ShapeDtypeStruct((T, C), BF) for _ in col_blocks]
                   + [jax.ShapeDtypeStruct((1, C), F32) for _ in col_blocks]
                   + [jax.ShapeDtypeStruct((K, C), F32)]
                   + [jax.ShapeDtypeStruct((1, C), F32) for _ in consts]),
        scratch_shapes=[pltpu.VMEM((hb + tr + SUBLANE, cw), F32), pltpu.VMEM((hb + tr + hb + SUBLANE, cw), F32),
                        pltpu.VMEM((hb + tr, cw), F32), pltpu.VMEM((hb, cw), F32),
                        pltpu.VMEM((SUBLANE, hb + tr, cw), F32), pltpu.VMEM((SUBLANE, hb + tr + hb, cw), F32),
                        pltpu.VMEM((tr, cw), F32)],
        compiler_params=pltpu.CompilerParams(
            dimension_semantics=("parallel", "arbitrary"), vmem_limit_bytes=VMEM_LIMIT),
    )(*([x] * nb), *([x] * nb), w, *consts, *grads)
    dxs = outs[:nb]
    sums = outs[nb:2 * nb]
    dw = outs[2 * nb]
    dcs = outs[2 * nb + 1:]
    return dxs, sums, dw, dcs


def make_qkv_post(dk, cw, n_qk_chunks):
    def l2(t):
        return t * lax.rsqrt(jnp.sum(t * t, axis=-1, keepdims=True) + EPS)

    def post(c, cid):
        s = jax.nn.silu(c)
        normed = lane_groups(l2, dk, s)
        return (jnp.where(cid < n_qk_chunks, normed, s),)
    return post


def glu_pre(za, zb):
    return za * jax.nn.sigmoid(zb)


def bias_post(c, cid, b):
    return (c + b,)


def _col_to_row(col):
    C = col.shape[-2]
    ii = lax.broadcasted_iota(jnp.int32, (C, C), 0)
    jj = lax.broadcasted_iota(jnp.int32, (C, C), 1)
    wide = jnp.broadcast_to(col, col.shape[:-1] + (C,))
    return jnp.sum(jnp.where(ii == jj, wide, 0.0), axis=-2, keepdims=True)


@jax.custom_vjp
def solve_with_inverse(a, rhs, x):
    return mm(x, rhs, 1, 0, True)


def _swi_fwd(a, rhs, x):
    sol = mm(x, rhs, 1, 0, True)
    return sol, (x, sol)


def _swi_bwd(res, dsol):
    x, sol = res
    drhs = mm(x, dsol, 0, 0, True)
    da = -mm(drhs, sol, 1, 1, True)
    return da, drhs, jnp.zeros_like(x)


solve_with_inverse.defvjp(_swi_fwd, _swi_bwd)


def unit_lower_inverse(a):
    C = a.shape[-1]
    ii = lax.broadcasted_iota(jnp.int32, (C, C), 0)
    jj = lax.broadcasted_iota(jnp.int32, (C, C), 1)
    x = jnp.where(ii == jj, 1.0, 0.0).astype(F32) - a
    p = mm(a, a, 1, 0, True)
    steps = int(math.log2(C)) - 1
    for s in range(steps):
        x = x + mm(x, p, 1, 0, True)
        if s < steps - 1:
            p = mm(p, p, 1, 0, True)
    return x


def dn_masks(C):
    ii = lax.broadcasted_iota(jnp.int32, (C, C), 0)
    jj = lax.broadcasted_iota(jnp.int32, (C, C), 1)
    return ii >= jj, ii > jj


def dn_pre(q, k, v, beta, gc):
    C, dk = q.shape[-2:]
    tri, strict = dn_masks(C)
    q = q * (dk ** -0.5)
    diff = gc - _col_to_row(gc)
    decay = jnp.where(tri, jnp.exp(jnp.where(tri, diff, 0.0)), 0.0)
    kb = k * beta
    vb = v * beta
    a = jnp.where(strict, mm(kb, k, 1, 1) * decay, 0.0)
    eg = jnp.exp(gc)
    rhs = jnp.concatenate([vb, kb * eg], axis=-1)
    attn = mm(q, k, 1, 1) * decay
    qd = q * eg
    g_last = gc[..., C - 1:C, :]
    kt = k * jnp.exp(g_last - gc)
    gl = jnp.exp(g_last)
    return a, rhs, attn, qd, kt, gl


def dn_chunk(q, k, v, beta, gc, state, x):
    dv = v.shape[-1]
    a, rhs, attn, qd, kt, gl = dn_pre(q, k, v, beta, gc)
    sol = solve_with_inverse(a, rhs, x)
    u, w = sol[..., :dv], sol[..., dv:]
    vn = u - mm(w, state, 1, 0)
    o = mm(qd, state, 1, 0) + mm(attn, vn, 1, 0)
    new_state = state * gl + mm(kt, vn, 0, 0)
    return o, new_state


def _by_head(q_ref, k_ref, v_ref, bg, H, dk, dv):
    qv = jnp.stack([q_ref[:, h * dk:(h + 1) * dk] for h in range(H)])
    kv = jnp.stack([k_ref[:, h * dk:(h + 1) * dk] for h in range(H)])
    vv = jnp.stack([v_ref[:, h * dv:(h + 1) * dv] for h in range(H)])
    beta = jnp.stack([bg[:, h:h + 1] for h in range(H)])
    gc = jnp.stack([bg[:, H + h:H + h + 1] for h in range(H)])
    return qv, kv, vv, beta, gc


def deltanet_fwd(qkv, bgc, H):
    T = qkv.shape[0]
    dk = dv = qkv.shape[1] // (3 * H)
    N = T // CHUNK

    def body(q_ref, k_ref, v_ref, bgc_ref, o_ref, x_ref, s_ref, state):
        @pl.when(pl.program_id(0) == 0)
        def _():
            state[...] = jnp.zeros((H, dk, dv), F32)

        qv, kv, vv, beta, gc = _by_head(q_ref, k_ref, v_ref, bgc_ref[...], H, dk, dv)
        a = dn_pre(qv, kv, vv, beta, gc)[0]
        x = unit_lower_inverse(a)
        s = state[...]
        o, s_new = dn_chunk(qv, kv, vv, beta, gc, s, x)
        for h in range(H):
            o_ref[:, h * dv:(h + 1) * dv] = o[h]
        x_ref[...] = x
        s_ref[...] = s
        state[...] = s_new

    return pl.pallas_call(
        body, name="deltanet_fwd", grid=(N,),
        in_specs=[pl.BlockSpec((CHUNK, H * dk), lambda n: (n, 0)),
                  pl.BlockSpec((CHUNK, H * dk), lambda n: (n, 1)),
                  pl.BlockSpec((CHUNK, H * dv), lambda n: (n, 2)),
                  pl.BlockSpec((CHUNK, LANE), lambda n: (n, 0))],
        out_specs=[pl.BlockSpec((CHUNK, H * dv), lambda n: (n, 0)),
                   pl.BlockSpec((None, H, CHUNK, CHUNK), lambda n: (n, 0, 0, 0)),
                   pl.BlockSpec((None, H, dk, dv), lambda n: (n, 0, 0, 0))],
        out_shape=[jax.ShapeDtypeStruct((T, H * dv), F32),
                   jax.ShapeDtypeStruct((N, H, CHUNK, CHUNK), F32),
                   jax.ShapeDtypeStruct((N, H, dk, dv), F32)],
        scratch_shapes=[pltpu.VMEM((H, dk, dv), F32)],
        compiler_params=pltpu.CompilerParams(
            dimension_semantics=("arbitrary",), vmem_limit_bytes=VMEM_LIMIT),
    )(qkv, qkv, qkv, bgc)


def deltanet_bwd(qkv, bgc, xinv, states, do, H):
    T = qkv.shape[0]
    dk = dv = qkv.shape[1] // (3 * H)
    N = T // CHUNK

    def body(q_ref, k_ref, v_ref, bgc_ref, x_ref, s_ref, do_ref, dqkv_ref, dbgc_ref, dstate):
        @pl.when(pl.program_id(0) == 0)
        def _():
            dstate[...] = jnp.zeros((H, dk, dv), F32)

        qv, kv, vv, beta, gc = _by_head(q_ref, k_ref, v_ref, bgc_ref[...], H, dk, dv)
        do = jnp.stack([do_ref[:, h * dv:(h + 1) * dv] for h in range(H)])
        _, vjp = jax.vjp(dn_chunk, qv, kv, vv, beta, gc, s_ref[...], x_ref[...])
        dq, dk_, dv_, dbeta, dgc, ds, _ = vjp((do, dstate[...]))
        dstate[...] = ds
        lane = lax.broadcasted_iota(jnp.int32, (CHUNK, LANE), 1)
        dbgc = jnp.zeros((CHUNK, LANE), F32)
        for h in range(H):
            dqkv_ref[:, h * dk:(h + 1) * dk] = dq[h]
            dqkv_ref[:, (H + h) * dk:(H + h + 1) * dk] = dk_[h]
            dqkv_ref[:, (2 * H + h) * dk:(2 * H + h + 1) * dk] = dv_[h]
            dbgc = dbgc + jnp.where(lane == h, dbeta[h], 0.0) + jnp.where(lane == h + H, dgc[h], 0.0)
        dbgc_ref[...] = dbgc

    rn = lambda n: N - 1 - n
    return pl.pallas_call(
        body, name="deltanet_bwd", grid=(N,),
        in_specs=[pl.BlockSpec((CHUNK, H * dk), lambda n: (rn(n), 0)),
                  pl.BlockSpec((CHUNK, H * dk), lambda n: (rn(n), 1)),
                  pl.BlockSpec((CHUNK, H * dv), lambda n: (rn(n), 2)),
                  pl.BlockSpec((CHUNK, LANE), lambda n: (rn(n), 0)),
                  pl.BlockSpec((None, H, CHUNK, CHUNK), lambda n: (rn(n), 0, 0, 0)),
                  pl.BlockSpec((None, H, dk, dv), lambda n: (rn(n), 0, 0, 0)),
                  pl.BlockSpec((CHUNK, H * dv), lambda n: (rn(n), 0))],
        out_specs=[pl.BlockSpec((CHUNK, 3 * H * dk), lambda n: (rn(n), 0)),
                   pl.BlockSpec((CHUNK, LANE), lambda n: (rn(n), 0))],
        out_shape=[jax.ShapeDtypeStruct((T, 3 * H * dk), F32),
                   jax.ShapeDtypeStruct((T, LANE), F32)],
        scratch_shapes=[pltpu.VMEM((H, dk, dv), F32)],
        compiler_params=pltpu.CompilerParams(
            dimension_semantics=("arbitrary",), vmem_limit_bytes=VMEM_LIMIT),
    )(qkv, qkv, qkv, bgc, xinv, states, do)


def _place():
    x, y, c = lax.axis_index("x"), lax.axis_index("y"), lax.axis_index("c")
    chips = [(1 - x, y), (x, 1 - y), (1 - x, 1 - y)]
    return x, y, c, chips


def all_gather(name, shards):
    na = len(shards)

    def body(*refs):
        ins, outs = refs[:na], refs[na:2 * na]
        send_sems, recv_sems, local_sems = refs[2 * na:]
        x, y, c, chips = _place()
        me, sibling = (x, y, c), (x, y, 1 - c)

        def copy(a, k, block, to, src=None):
            dst = outs[a].at[4 * block[0] + 2 * block[1] + block[2]]
            return pltpu.make_async_remote_copy(
                src_ref=dst if src is None else src, dst_ref=dst,
                send_sem=send_sems.at[a, k], recv_sem=recv_sems.at[a, k],
                device_id=to, device_id_type=MESH)

        mine, first, passed = [], [], []
        for a in range(na):
            cp = pltpu.make_async_copy(ins[a], outs[a].at[4 * x + 2 * y + c], local_sems.at[a])
            cp.start()
            mine.append(cp)
        for a in range(na):
            cps = [copy(a, 0, me, sibling, src=ins[a])]
            cps += [copy(a, 1 + j, me, (*chip, c), src=ins[a]) for j, chip in enumerate(chips)]
            for cp in cps:
                cp.start()
            first += cps
        for a in range(na):
            for j, chip in enumerate(chips):
                copy(a, 1 + j, (*chip, c), me).wait_recv()
                cp = copy(a, 4 + j, (*chip, c), sibling)
                cp.start()
                passed.append(cp)
        for a in range(na):
            copy(a, 0, sibling, me).wait_recv()
            for j, chip in enumerate(chips):
                copy(a, 4 + j, (*chip, 1 - c), me).wait_recv()
        for cp in first + passed:
            cp.wait_send()
        for cp in mine:
            cp.wait()

    outs = pl.pallas_call(
        body, name=name,
        in_specs=[ANY] * na, out_specs=[ANY] * na,
        out_shape=[jax.ShapeDtypeStruct((N_DEV,) + s.shape, s.dtype) for s in shards],
        scratch_shapes=[pltpu.SemaphoreType.DMA((na, 7)), pltpu.SemaphoreType.DMA((na, 7)),
                        pltpu.SemaphoreType.DMA((na,))],
    )(*shards)
    return list(outs)


HBM_SPEC = pl.BlockSpec(memory_space=pltpu.HBM)
SEM_SPEC = pl.BlockSpec(memory_space=pltpu.SEMAPHORE)
EFFECT = pltpu.SideEffectType.DATAFLOW_SIDE_EFFECTING


def _descriptors(plan, bufs, send_sems, recv_sems):
    return [pltpu.make_async_remote_copy(src_ref=src, dst_ref=dst, send_sem=send_sems.at[k],
                                         recv_sem=recv_sems.at[k], device_id=dev, device_id_type=MESH)
            for k, (src, dst, dev) in enumerate(plan(bufs))]


def split_start(name, bufs, plan, n, after):
    nb = len(bufs)

    def body(*refs):
        for cp in _descriptors(plan, refs[:nb], refs[nb + 1], refs[nb + 2]):
            cp.start()
        refs[-1][...] = jnp.zeros((SUBLANE, LANE), F32)

    outs = pl.pallas_call(
        body, name=name,
        out_shape=(pltpu.SemaphoreType.DMA((n,)), pltpu.SemaphoreType.DMA((n,)),
                   *[pltpu.HBM(b.shape, b.dtype) for b in bufs],
                   jax.ShapeDtypeStruct((SUBLANE, LANE), F32)),
        in_specs=[HBM_SPEC] * nb + [ANY],
        out_specs=(SEM_SPEC, SEM_SPEC, *[HBM_SPEC] * nb, pl.BlockSpec(memory_space=pltpu.VMEM)),
        input_output_aliases={i: 2 + i for i in range(nb)},
        compiler_params=pltpu.CompilerParams(has_side_effects=EFFECT),
    )(*[pltpu.with_memory_space_constraint(b, pltpu.HBM) for b in bufs], after)
    return outs[0], outs[1], list(outs[2:2 + nb]), outs[-1]


def split_wait(name, send_sems, recv_sems, bufs, plan, after):
    nb = len(bufs)

    def body(*refs):
        cps = _descriptors(plan, refs[:nb], refs[nb], refs[nb + 1])
        for cp in cps:
            cp.wait_recv()
        for cp in cps:
            cp.wait_send()
        refs[-1][...] = jnp.zeros((SUBLANE, LANE), F32)

    afters = list(after) if isinstance(after, (list, tuple)) else [after]
    outs = pl.pallas_call(
        body, name=name,
        out_shape=[pltpu.HBM(b.shape, b.dtype) for b in bufs] + [jax.ShapeDtypeStruct((SUBLANE, LANE), F32)],
        in_specs=[HBM_SPEC] * nb + [SEM_SPEC, SEM_SPEC] + [ANY] * len(afters),
        out_specs=[HBM_SPEC] * nb + [pl.BlockSpec(memory_space=pltpu.VMEM)],
        input_output_aliases={i: i for i in range(nb)},
        compiler_params=pltpu.CompilerParams(has_side_effects=EFFECT),
    )(*bufs, send_sems, recv_sems, *afters)
    return list(outs[:nb]), outs[-1]


def _block(px, py, pc):
    return 4 * px + 2 * py + pc


def plan_gather_ici(na):
    def plan(bufs):
        x, y, c, chips = _place()
        out = []
        for a in range(na):
            dst = bufs[na + a].at[_block(x, y, c)]
            out.append((bufs[a], dst, (x, y, 1 - c)))
            out += [(bufs[a], dst, (px, py, c)) for px, py in chips]
        return out
    return plan


def plan_gather_pass(na):
    def plan(bufs):
        x, y, c, chips = _place()
        out = []
        for a in range(na):
            for px, py in chips:
                blk = bufs[a].at[_block(px, py, c)]
                out.append((blk, blk, (x, y, 1 - c)))
        return out
    return plan


def plan_reduce_d2d(na):
    def plan(bufs):
        x, y, c, _ = _place()
        return [(bufs[a].at[2 * s + (1 - c)], bufs[na + a].at[s], (x, y, 1 - c))
                for a in range(na) for s in range(4)]
    return plan


def plan_reduce_ici(na):
    def plan(bufs):
        x, y, c, chips = _place()
        return [(bufs[a].at[2 * px + py], bufs[na + a].at[j], (px, py, c))
                for a in range(na) for j, (px, py) in enumerate(chips)]
    return plan


def place_own(name, land, shard, dev):
    r, c = shard.shape
    tr = _tile(r, 512)

    def body(sp_ref, s_ref, land_ref, o_ref):
        o_ref[...] = s_ref[...]

    return pl.pallas_call(
        body, name=name, out_shape=jax.ShapeDtypeStruct(land.shape, land.dtype),
        grid_spec=pltpu.PrefetchScalarGridSpec(
            num_scalar_prefetch=1, grid=(r // tr,),
            in_specs=[pl.BlockSpec((tr, c), lambda i, s: (i, 0)), ANY],
            out_specs=pl.BlockSpec((None, tr, c), lambda i, s: (s[0], i, 0))),
        input_output_aliases={2: 0},
        compiler_params=pltpu.CompilerParams(dimension_semantics=("arbitrary",)),
    )(dev, shard, land)


def pack(arrs, row_mult=SUBLANE):
    pieces = []
    for a in arrs:
        f = a.reshape(-1).astype(F32)
        pad = (-f.shape[0]) % LANE
        if pad:
            f = jnp.concatenate([f, jnp.zeros((pad,), F32)])
        pieces.append(f)
    flat = jnp.concatenate(pieces)
    rows = flat.shape[0] // LANE
    pad_rows = (-rows) % row_mult
    if pad_rows:
        flat = jnp.concatenate([flat, jnp.zeros((pad_rows * LANE,), F32)])
    return flat.reshape(-1, LANE)


def unpack(buf, shapes):
    flat = buf.reshape(-1)
    outs, off = [], 0
    for shp in shapes:
        n = int(np.prod(shp))
        outs.append(flat[off:off + n].reshape(shp))
        off += n + ((-n) % LANE)
    return outs


def _vjp_rows(fn, n_row_in, n_cot):
    def bwd(*args):
        rows = args[:n_row_in]
        cots = args[n_row_in:n_row_in + n_cot]
        consts = args[n_row_in + n_cot:]
        out, vjp = jax.vjp(fn, *rows, *consts)
        if isinstance(out, (tuple, list)):
            cot = tuple(c.astype(o.dtype) for c, o in zip(cots, out))
        else:
            cot = cots[0].astype(out.dtype)
        return vjp(cot)
    return bwd


def rms_fwd(name, x, g):
    return rowcall(name, rms_tile, [x], [g], [(x.shape[1], BF)])


def rms_bwd(name, x, g, dh, dres, after=None):
    D = x.shape[1]
    vj = _vjp_rows(rms_tile, 1, 1)

    def f(x_, dh_, dres_, g_):
        dx, dg = vj(x_, dh_, g_)
        dx = dx + dres_
        return dx, dx, dg
    return rowcall(name, f, [x, dh, dres], [g], [(D, F32), (D, BF)], [(1, D)], after=after)


def kernel(x, e_norm, e_w_in, e_conv_w, e_a_log, e_dt_bias, e_o_norm, e_ln_g, e_ln_b, e_w_s, e_b_s, e_w_out, o_norm, o_pw1, o_pw1_b, o_dw, o_dw_b, o_ln_g, o_ln_b, o_pw2, o_pw2_b, f_norm, f_w1, f_w2, final_norm, loss_target, m_e_norm, m_e_w_in, m_e_conv_w, m_e_a_log, m_e_dt_bias, m_e_o_norm, m_e_ln_g, m_e_ln_b, m_e_w_s, m_e_b_s, m_e_w_out, m_o_norm, m_o_pw1, m_o_pw1_b, m_o_dw, m_o_dw_b, m_o_ln_g, m_o_ln_b, m_o_pw2, m_o_pw2_b, m_f_norm, m_f_w1, m_f_w2, m_final_norm, v_e_norm, v_e_w_in, v_e_conv_w, v_e_a_log, v_e_dt_bias, v_e_o_norm, v_e_ln_g, v_e_ln_b, v_e_w_s, v_e_b_s, v_e_w_out, v_o_norm, v_o_pw1, v_o_pw1_b, v_o_dw, v_o_dw_b, v_o_ln_g, v_o_ln_b, v_o_pw2, v_o_pw2_b, v_f_norm, v_f_w1, v_f_w2, v_final_norm):
    names = ['e_norm', 'e_w_in', 'e_conv_w', 'e_a_log', 'e_dt_bias', 'e_o_norm', 'e_ln_g', 'e_ln_b', 'e_w_s', 'e_b_s', 'e_w_out', 'o_norm', 'o_pw1', 'o_pw1_b', 'o_dw', 'o_dw_b', 'o_ln_g', 'o_ln_b', 'o_pw2', 'o_pw2_b', 'f_norm', 'f_w1', 'f_w2', 'final_norm']
    W = dict(zip(names, [e_norm, e_w_in, e_conv_w, e_a_log, e_dt_bias, e_o_norm, e_ln_g, e_ln_b, e_w_s, e_b_s, e_w_out, o_norm, o_pw1, o_pw1_b, o_dw, o_dw_b, o_ln_g, o_ln_b, o_pw2, o_pw2_b, f_norm, f_w1, f_w2, final_norm]))
    Mo = dict(zip(names, [m_e_norm, m_e_w_in, m_e_conv_w, m_e_a_log, m_e_dt_bias, m_e_o_norm, m_e_ln_g, m_e_ln_b, m_e_w_s, m_e_b_s, m_e_w_out, m_o_norm, m_o_pw1, m_o_pw1_b, m_o_dw, m_o_dw_b, m_o_ln_g, m_o_ln_b, m_o_pw2, m_o_pw2_b, m_f_norm, m_f_w1, m_f_w2, m_final_norm]))
    Vo = dict(zip(names, [v_e_norm, v_e_w_in, v_e_conv_w, v_e_a_log, v_e_dt_bias, v_e_o_norm, v_e_ln_g, v_e_ln_b, v_e_w_s, v_e_b_s, v_e_w_out, v_o_norm, v_o_pw1, v_o_pw1_b, v_o_dw, v_o_dw_b, v_o_ln_g, v_o_ln_b, v_o_pw2, v_o_pw2_b, v_f_norm, v_f_w1, v_f_w2, v_final_norm]))

    T, D = x.shape[1], x.shape[2]
    H = e_a_log.shape[-1]
    dv = e_o_norm.shape[-1]
    dk = dv
    G = e_w_s.shape[1]
    AQK, AV, BW = H * dk, H * dv, e_ln_g.shape[-1]
    AQKV = 2 * AQK + AV
    in_cols = AQKV + AV + 2 * H + 2 * BW
    KA = e_conv_w.shape[1]
    KC = o_dw.shape[1]
    L = f_norm.shape[0]
    dev = 4 * lax.axis_index("x") + 2 * lax.axis_index("y") + lax.axis_index("c")
    x2d = x.reshape(T, D)
    tgt = loss_target.reshape(T, D)

    dev_sp = dev.astype(jnp.int32).reshape(1)
    where = jnp.stack([lax.axis_index("c"), 2 * lax.axis_index("x") + lax.axis_index("y")]).astype(jnp.int32)
    row = lambda a: a.reshape(1, -1).astype(F32)
    en_row = row(e_norm)

    def gather_begin(tag, shards, after):
        na = len(shards)
        lands = [lax.empty((N_DEV,) + s.shape, s.dtype) for s in shards]
        ss, rs, bufs, tok = split_start(f"gather{tag}_ici_start", shards + lands, plan_gather_ici(na), 4 * na, after)
        return (na, ss, rs, bufs), tok

    def gather_pass(tag, h, after):
        na, ss, rs, bufs = h
        bufs, tok = split_wait(f"gather{tag}_ici_wait", ss, rs, bufs, plan_gather_ici(na), after)
        ss, rs, lands, tok = split_start(f"gather{tag}_pass_start", bufs[na:], plan_gather_pass(na), 3 * na, tok)
        return (na, ss, rs, bufs[:na], lands), tok

    def gather_end(tag, h, after):
        na, ss, rs, shards, lands = h
        lands, _ = split_wait(f"gather{tag}_pass_wait", ss, rs, lands, plan_gather_pass(na), after)
        return [place_own(f"gather{tag}_own{a}", lands[a], shards[a], dev_sp) for a in range(na)]

    small_sharded = ['e_conv_w', 'o_norm', 'o_pw1_b', 'o_dw', 'o_dw_b', 'o_ln_g', 'o_ln_b', 'o_pw2_b']
    sm = all_gather("gather_small", [pack([W[n][0]]) for n in small_sharded])

    bfw = lambda w: w.astype(BF)
    hA0, tok = gather_begin("0", [bfw(e_w_in[0])], sm[0])
    hA1, tok = gather_begin("1", [bfw(e_w_out[0]), bfw(f_w1[0]), bfw(f_w2[0])], tok)
    hA2, tok = gather_begin("2", [bfw(o_pw1[0]), bfw(o_pw2[0]), bfw(f_w1[1]), bfw(f_w2[1])], tok)
    h0 = rowcall("rms_e", rms_tile, [x2d], [en_row], [(D, BF)], after=tok)

    full = {}
    for n, g in zip(small_sharded, sm):
        shp = W[n][0].shape
        blocks = [unpack(g[d], [shp])[0] for d in range(N_DEV)]
        full[n] = jnp.concatenate(blocks, axis=-1)
    conv_w = full['e_conv_w']
    on_row, pw1_b_row = row(full['o_norm']), row(full['o_pw1_b'])
    dw_w, dw_b_row = full['o_dw'], row(full['o_dw_b'])
    oln_g_row, oln_b_row, pw2_b_row = row(full['o_ln_g']), row(full['o_ln_b']), row(full['o_pw2_b'])
    small_names = ['e_norm', 'e_conv_w', 'e_a_log', 'e_dt_bias', 'e_o_norm', 'e_ln_g', 'e_ln_b', 'e_w_s', 'e_b_s',
                   'o_norm', 'o_pw1_b', 'o_dw', 'o_dw_b', 'o_ln_g', 'o_ln_b', 'o_pw2_b', 'f_norm', 'final_norm']
    packed_wmv = [pack([A[n] for n in small_names], 256) for A in (W, Mo, Vo)]

    hB0, tok = gather_pass("0", hA0, [h0, conv_w, dw_w, pw1_b_row] + packed_wmv)
    (g_win,) = gather_end("0", hB0, tok)
    win = jnp.moveaxis(g_win, 0, 1).reshape(D, in_cols)
    w_qkv, w_z = win[:, :AQKV], win[:, AQKV:AQKV + AV]
    w_ba = jnp.pad(win[:, AQKV + AV:AQKV + AV + 2 * H], ((0, 0), (0, LANE - 2 * H)))
    w_uv = win[:, AQKV + AV + 2 * H:]

    alog_row = jnp.pad(row(e_a_log), ((0, 0), (H, LANE - 2 * H)))
    dtb_row = jnp.pad(row(e_dt_bias), ((0, 0), (H, LANE - 2 * H)))
    eon_row = row(e_o_norm)
    eln_g_row, eln_b_row = row(e_ln_g), row(e_ln_b)
    w_s = e_w_s[0]
    bs_t = e_b_s[0].T
    fn_rows = [row(f_norm[l]) for l in range(L)]
    fin_row = row(final_norm)

    qkv_raw = matmul("proj_qkv", h0, w_qkv, "nn")
    z_gate = matmul("proj_z", h0, w_z, "nn")
    ba = matmul("proj_ba", h0, w_ba, "nn")
    uv = matmul("proj_uv", h0, w_uv, "nn")

    cwa = min(512, AQKV)
    qkv_post = make_qkv_post(dk, cwa, 2 * AQK // cwa)
    ident = lambda t: t
    qkv = conv_fwd("qkv_conv", qkv_raw, conv_w, [], ident, qkv_post, [0], 1, KA, cw=cwa)
    bgc_fn = make_bgc(H)
    bgc = rowcall("bgc", bgc_fn, [ba], [alog_row, dtb_row], [(LANE, F32)])
    o_dn, xinv, states = deltanet_fwd(qkv, bgc, H)
    hB1, tok = gather_pass("1", hA1, o_dn)

    def mix_tile(o, z, uv_, o_norm_, ln_g, ln_b, w_s_, bs_t_):
        return jnp.concatenate([mixa_post_tile(o, z, o_norm_), mixb_tile(uv_, ln_g, ln_b, w_s_, bs_t_)], axis=-1)
    mix_consts = [eon_row, eln_g_row, eln_b_row, w_s, bs_t]
    mix = rowcall("mix", mix_tile, [o_dn, z_gate, uv], mix_consts, [(AV + BW, BF)], after=tok)
    g_wout, g_w1_0, g_w2_0 = gather_end("1", hB1, mix)
    wout = g_wout.reshape(-1, D)
    w1 = [jnp.moveaxis(g_w1_0, 0, 1).reshape(D, -1), None]
    w2 = [g_w2_0.reshape(-1, D), None]
    add_epi = lambda acc, r: (acc + r,)
    x1 = matmul("out_proj", mix, wout, "nn", epi=add_epi, extras=[x2d])

    def relu2_epi(acc):
        r = jnp.maximum(acc, 0.0)
        return r * r, r

    hf0 = rms_fwd("rms_f0", x1, fn_rows[0])
    a2_0, ar_0 = matmul("ffn_up0", hf0, w1[0], "nn", epi=relu2_epi, out_dtypes=(BF, BF))
    hB2, tok = gather_pass("2", hA2, a2_0)
    x2 = matmul("ffn_down0", a2_0, w2[0], "nn", epi=add_epi, extras=[x1], after=tok)
    ffn0 = (hf0, a2_0, ar_0)
    g_pw1, g_pw2, g_w1_1, g_w2_1 = gather_end("2", hB2, x2)
    pw1 = jnp.moveaxis(g_pw1, 0, 1).reshape(D, 2 * D)
    pw2 = g_pw2.reshape(D, D)
    w1[1] = jnp.moveaxis(g_w1_1, 0, 1).reshape(D, -1)
    w2[1] = g_w2_1.reshape(-1, D)

    h1 = rms_fwd("rms_o", x2, on_row)
    bias_epi = lambda acc, b: (acc + b,)
    zc = matmul("pw1", h1, pw1, "nn", epi=bias_epi, extras=[pw1_b_row])
    cwc = min(512, D)
    ncb = D // cwc
    cconv = conv_fwd("dw_conv", zc, dw_w, [dw_b_row], glu_pre, bias_post, [0, ncb], 1, KC, hb=32, cw=cwc)
    ln_silu = lambda c, g, b: jax.nn.silu(ln_tile(c, g, b))
    s_act = rowcall("ln_silu", ln_silu, [cconv], [oln_g_row, oln_b_row], [(D, BF)])
    x3 = matmul("pw2", s_act, pw2, "nn", epi=lambda acc, r, b: (r + (acc + b),), extras=[x2, pw2_b_row])
    hf1 = rms_fwd("rms_f1", x3, fn_rows[1])
    a2_1, ar_1 = matmul("ffn_up1", hf1, w1[1], "nn", epi=relu2_epi, out_dtypes=(BF, BF))
    x4 = matmul("ffn_down1", a2_1, w2[1], "nn", epi=add_epi, extras=[x3])
    ffn1 = (hf1, a2_1, ar_1)

    def loss_bwd_tile(x_, t_, g_):
        l, vjp = jax.vjp(lambda a, b: loss_tile(a, b, t_), x_, g_)
        dx, dg = vjp(jnp.ones_like(l))
        return dx, dx, l, dg
    dx4, dx4_b, loss_part, d_final = rowcall("loss_head", loss_bwd_tile, [x4, tgt], [fin_row],
                                             [(D, F32), (D, BF)], [(1, 1), (1, D)])
    loss = lax.psum(loss_part[0, 0], ("x", "y", "c"))

    def reduce_begin(tag, grads, after):
        na = len(grads)
        lands = [lax.empty((4,) + g.shape[1:], g.dtype) for g in grads]
        ss, rs, bufs, tok = split_start(f"reduce{tag}_d2d_start", grads + lands, plan_reduce_d2d(na), 4 * na, after)
        return (na, ss, rs, bufs), tok

    def reduce_mid(tag, h, after):
        na, ss, rs, bufs = h
        bufs, tok = split_wait(f"reduce{tag}_d2d_wait", ss, rs, bufs, plan_reduce_d2d(na), after)
        parts = []
        for a, (g, rc) in enumerate(zip(bufs[:na], bufs[na:])):
            r, c = g.shape[1], g.shape[2]
            mine = lambda i, n, s: (2 * (i // (n // 4)) + s[0]) * (n // 4) + i % (n // 4)
            parts.append(rowcall(f"chip_sum{tag}_{a}", lambda p, q: (p + q,),
                                 [(g.reshape(N_DEV * r, c), mine), rc.reshape(4 * r, c)], [],
                                 [(c, BF)], tr=_tile(r, 512), sp=where, R=4 * r).reshape(4, r, c))
        lands = [lax.empty((3,) + p.shape[1:], p.dtype) for p in parts]
        ss, rs, bufs, tok = split_start(f"reduce{tag}_ici_start", parts + lands, plan_reduce_ici(na), 3 * na, tok)
        return (na, ss, rs, bufs), tok

    res = {}

    def reduce_end(tag, h, after, targets):
        na, ss, rs, bufs = h
        bufs, _ = split_wait(f"reduce{tag}_ici_wait", ss, rs, bufs, plan_reduce_ici(na), after)
        for a, (part, fin, (n, l)) in enumerate(zip(bufs[:na], bufs[na:], targets)):
            def f(p0, p1, p2, p3, w_, m_, v_):
                g = ((p0.astype(F32) + p1.astype(F32)) + p2.astype(F32)) + p3.astype(F32)
                return (g,) + adamw_tile(w_, g, m_, v_)
            r, C = fin.shape[-2], fin.shape[-1]
            own = lambda i, n_, s: s[1] * n_ + i
            res[n] = rowcall(f"adam{tag}_{a}", f, [(part.reshape(4 * r, C), own), (fin, 0), (fin, 1), (fin, 2),
                                                   (W[n], l), (Mo[n], l), (Vo[n], l)], [],
                             [(C, F32)] * 4, tr=256, sp=where, R=r,
                             out_lead=(W[n].shape[0], l), into=res.get(n))

    def tie(small, tok):
        return small + tok[0:1, 0:1]

    dscale_epi = lambda acc, r: (acc * (2.0 * r.astype(F32)),)
    d_fnorm = [None] * L

    dpre1 = matmul("ffn_down_dx1", dx4_b, w2[1], "nt", epi=dscale_epi, extras=[ar_1], out_dtypes=(BF,))
    dw2_1 = matmul("ffn_down_dw1", a2_1, dx4_b, "tn")
    dw1_1 = matmul("ffn_up_dw1", hf1, dpre1, "tn", colshard=True)
    hD1, tok = reduce_begin("1", [dw1_1, dw2_1.reshape(N_DEV, -1, D)], dpre1)
    dhf1 = matmul("ffn_up_dx1", dpre1, w1[1], "nt", after=tok, tk=4096)
    dx3, dx3_b, d_fnorm[1] = rms_bwd("rms_f_bwd1", x3, fn_rows[1], dhf1, dx4)

    ds_act = matmul("pw2_dx", dx3_b, pw2, "nt")
    hI1, tok = reduce_mid("1", hD1, ds_act)
    d_pw2 = matmul("pw2_dw", s_act, dx3_b, "tn", after=tok)
    ln_silu_bwd = _vjp_rows(ln_silu, 1, 1)

    def ln_silu_bwd_tile(c_, ds_, dx3_, g_, b_):
        dc, dg, db = ln_silu_bwd(c_, ds_, g_, b_)
        return dc, dg, db, jnp.sum(dx3_, axis=0, keepdims=True)
    dcconv, d_oln_g, d_oln_b, d_pw2_b = rowcall(
        "ln_silu_bwd", ln_silu_bwd_tile, [cconv, ds_act, dx3], [oln_g_row, oln_b_row],
        [(D, F32)], [(1, D), (1, D), (1, D)])
    (dza, dzb), (sza, szb), d_dw, (d_dw_b,) = conv_bwd(
        "dw_conv_bwd", zc, dw_w, [tie(dw_b_row, tok)], [dcconv], glu_pre, bias_post, [0, ncb], KC, hb=32, cw=cwc, tr=128,
        recompute=False)
    dzc = jnp.concatenate([dza, dzb], axis=-1)
    d_pw1_b = jnp.concatenate([sza, szb], axis=-1)
    d_pw1 = matmul("pw1_dw", h1, dzc, "tn", colshard=True)
    dh1 = matmul("pw1_dx", dzc, pw1, "nt")
    dx2, dx2_b, d_onorm = rms_bwd("rms_o_bwd", x2, on_row, dh1, dx3)
    reduce_end("1", hI1, dx2, [('f_w1', 1), ('f_w2', 1)])

    hD2, tok = reduce_begin("2", [d_pw1, d_pw2.reshape(N_DEV, -1, D)], dx2)
    dpre0 = matmul("ffn_down_dx0", dx2_b, w2[0], "nt", epi=dscale_epi, extras=[ar_0], out_dtypes=(BF,), after=tok)
    dw2_0 = matmul("ffn_down_dw0", a2_0, dx2_b, "tn")
    hI2, tok = reduce_mid("2", hD2, dw2_0)
    dw1_0 = matmul("ffn_up_dw0", hf0, dpre0, "tn", colshard=True, after=tok)
    dhf0 = matmul("ffn_up_dx0", dpre0, w1[0], "nt", tk=4096)
    dx1, dx1_b, d_fnorm[0] = rms_bwd("rms_f_bwd0", x1, fn_rows[0], dhf0, dx2)
    reduce_end("2", hI2, dx1, [('o_pw1', 0), ('o_pw2', 0)])

    dmix = matmul("out_proj_dx", dx1_b, wout, "nt")
    d_wout = matmul("out_proj_dw", mix, dx1_b, "tn")
    hD3, tok = reduce_begin("3", [dw1_0, dw2_0.reshape(N_DEV, -1, D), d_wout.reshape(N_DEV, -1, D)], dmix)
    do_dn, dz_gate, duv, d_eon, d_eln_g, d_eln_b, d_ws, d_bs_t = rowcall(
        "mix_bwd", _vjp_rows(mix_tile, 3, 1), [o_dn, z_gate, uv, dmix], mix_consts,
        [(AV, F32), (AV, BF), (2 * BW, BF)], [(1, dv), (1, BW), (1, BW), w_s.shape, bs_t.shape], after=tok)
    dqkv, dbgc = deltanet_bwd(qkv, bgc, xinv, states, do_dn, H)
    hI3, tok = reduce_mid("3", hD3, dbgc)
    bgc_bwd = _vjp_rows(bgc_fn, 1, 1)
    dba, d_alog_row, d_dtb_row = rowcall(
        "bgc_bwd", bgc_bwd, [ba, dbgc], [alog_row, dtb_row], [(LANE, BF)], [(1, LANE), (1, LANE)])
    (dqkv_raw,), _, d_conv_w, _ = conv_bwd(
        "qkv_conv_bwd", qkv_raw, tie(conv_w, tok), [], [dqkv], ident, qkv_post, [0], KA, cw=cwa)

    dw_qkv = matmul("proj_qkv_dw", h0, dqkv_raw, "tn")
    dw_z = matmul("proj_z_dw", h0, dz_gate, "tn")
    dw_ba = matmul("proj_ba_dw", h0, dba, "tn")
    dw_uv = matmul("proj_uv_dw", h0, duv, "tn")
    d_win = jnp.concatenate([dw_qkv, dw_z, dw_ba[:, :2 * H], dw_uv], axis=-1)
    G_win = jnp.moveaxis(d_win.reshape(D, N_DEV, in_cols // N_DEV), 1, 0)
    hD4, tok = reduce_begin("4", [G_win], dw_uv)
    reduce_end("3", hI3, tok, [('f_w1', 0), ('f_w2', 0), ('e_w_out', 0)])
    hI4, tok = reduce_mid("4", hD4, [res[n][0] for n in ('f_w1', 'f_w2', 'o_pw1', 'e_w_out')])
    dh0 = matmul_nt_sum("proj_dx", [(dqkv_raw, w_qkv), (dz_gate, w_z), (dba, w_ba), (duv, w_uv)], after=tok)
    grad_x, _, d_enorm = rms_bwd("rms_e_bwd", x2d, en_row, dh0, dx1, after=tok)

    d_alog = d_alog_row[:, H:2 * H]
    d_dtb = d_dtb_row[:, H:2 * H]
    small_grads = [d_enorm, d_conv_w, d_alog, d_dtb, d_eon, d_eln_g, d_eln_b, d_ws, d_bs_t.T,
                   d_onorm, d_pw1_b, d_dw, d_dw_b, d_oln_g, d_oln_b, d_pw2_b,
                   jnp.concatenate(d_fnorm, axis=0), d_final]
    full_shapes = [g.shape for g in small_grads]
    gs_all = all_gather("gather_small_grads", [pack(small_grads, 256)])[0]

    def sum8(*ps):
        s = ps[0]
        for p in ps[1:]:
            s = s + p
        return (s,)
    gs_sum = rowcall("small_sum", sum8, [(gs_all, d) for d in range(N_DEV)], [], [(LANE, F32)])
    reduce_end("4", hI4, gs_sum, [('e_w_in', 0)])
    g_full = dict(zip(small_names, unpack(gs_sum, full_shapes)))
    g_loc = {}
    for n in small_names:
        g = g_full[n]
        if n in small_sharded:
            per = g.shape[-1] // N_DEV
            g = lax.dynamic_slice_in_dim(g, dev * per, per, axis=-1)
        g_loc[n] = g.reshape(W[n].shape)
    packs = [packed_wmv[0], pack([g_loc[n] for n in small_names], 256), packed_wmv[1], packed_wmv[2]]
    d_s, m_s, v_s = rowcall("adam_small", adamw_tile, packs, [], [(LANE, F32)] * 3)
    shapes = [W[n].shape for n in small_names]
    for n, d_, m_, v_ in zip(small_names, unpack(d_s, shapes), unpack(m_s, shapes), unpack(v_s, shapes)):
        res[n] = (g_loc[n], d_, m_, v_)

    grads = [res[n][0] for n in names]
    deltas = [res[n][1] for n in names]
    new_m = [res[n][2] for n in names]
    new_v = [res[n][3] for n in names]
    return (loss, grad_x.reshape(x.shape), *grads, *deltas, *new_m, *new_v)
```

```python
import functools
import math

import jax
import jax.numpy as jnp
import numpy as np
from jax import lax
from jax.experimental import pallas as pl
from jax.experimental.pallas import tpu as pltpu

F32 = jnp.float32
BF = jnp.bfloat16
EPS = 1e-6
CHUNK = 64
B_BLOCK = 128
LANE = 128
SUBLANE = 8
N_DEV = 8
VMEM_LIMIT = 56 * 1024 * 1024

ADAM_LR = 0.001
ADAM_B1 = 0.9
ADAM_B2 = 0.999
ADAM_EPS = 1e-08
ADAM_WD = 0.01
ADAM_STEP = 10

MESH = pl.DeviceIdType.MESH
ANY = pl.BlockSpec(memory_space=pl.ANY)


def _tile(n, pref, mult=SUBLANE):
    if n <= pref:
        return n
    t = (pref // mult) * mult
    while t >= mult:
        if n % t == 0:
            return t
        t -= mult
    return n


def _dg(a, b, ca, cb, hi):
    nb = a.ndim - 2
    batch = tuple(range(nb))
    dims = (((ca + nb,), (cb + nb,)), (batch, batch))
    if hi:
        return lax.dot_general(a.astype(F32), b.astype(F32), dims,
                               precision=lax.Precision.HIGHEST, preferred_element_type=F32)
    return lax.dot_general(a.astype(BF), b.astype(BF), dims, preferred_element_type=F32)


@functools.partial(jax.custom_vjp, nondiff_argnums=(2, 3, 4))
def mm(a, b, ca, cb, hi=False):
    return _dg(a, b, ca, cb, hi)


def _mm_fwd(a, b, ca, cb, hi):
    return _dg(a, b, ca, cb, hi), (a, b)


def _mm_bwd(ca, cb, hi, res, g):
    a, b = res
    if ca == 1:
        da = mm(g, b, 1, 1 - cb, hi)
    else:
        da = mm(b, g, 1 - cb, 1, hi)
    if cb == 0:
        db = mm(a, g, 1 - ca, 0, hi)
    else:
        db = mm(g, a, 0, 1 - ca, hi)
    return da.astype(a.dtype), db.astype(b.dtype)


mm.defvjp(_mm_fwd, _mm_bwd)


def matmul(name, a, b, mode, epi=None, extras=(), out_dtypes=(F32,), colshard=False,
           tm=None, tn=1024, tk=2048, after=None):
    afters = [] if after is None else [after]
    if mode == "nn":
        (M, K), (K2, N) = a.shape, b.shape
    elif mode == "nt":
        (M, K), (N, K2) = a.shape, b.shape
    else:
        (K, M), (K2, N) = a.shape, b.shape
    assert K == K2, (name, a.shape, b.shape, mode)
    if tm is None:
        tm = 1024
    tm = _tile(M, tm)
    tn = N // N_DEV if colshard else _tile(N, tn, LANE)
    tk = _tile(K, tk, LANE)
    nk = K // tk
    grid = (M // tm, N // tn, nk)
    if mode == "nn":
        a_spec = pl.BlockSpec((tm, tk), lambda i, j, k: (i, k))
        b_spec = pl.BlockSpec((tk, tn), lambda i, j, k: (k, j))
        ca, cb = 1, 0
    elif mode == "nt":
        a_spec = pl.BlockSpec((tm, tk), lambda i, j, k: (i, k))
        b_spec = pl.BlockSpec((tn, tk), lambda i, j, k: (j, k))
        ca, cb = 1, 1
    else:
        a_spec = pl.BlockSpec((tk, tm), lambda i, j, k: (k, i))
        b_spec = pl.BlockSpec((tk, tn), lambda i, j, k: (k, j))
        ca, cb = 0, 0
    ex_specs = []
    for e in extras:
        if e.shape[0] == 1:
            ex_specs.append(pl.BlockSpec((1, tn), lambda i, j, k: (0, j)))
        else:
            assert e.shape == (M, N), (name, e.shape)
            ex_specs.append(pl.BlockSpec((tm, tn), lambda i, j, k: (i, j)))
    if colshard:
        out_shape = [jax.ShapeDtypeStruct((N_DEV, M, tn), dt) for dt in out_dtypes]
        out_specs = [pl.BlockSpec((None, tm, tn), lambda i, j, k: (j, i, 0)) for _ in out_dtypes]
    else:
        out_shape = [jax.ShapeDtypeStruct((M, N), dt) for dt in out_dtypes]
        out_specs = [pl.BlockSpec((tm, tn), lambda i, j, k: (i, j)) for _ in out_dtypes]
    n_ex, n_out = len(extras), len(out_dtypes)

    def body(*refs):
        a_ref, b_ref = refs[0], refs[1]
        ex_refs = refs[2:2 + n_ex]
        first_out = 2 + n_ex + len(afters)
        o_refs = refs[first_out:first_out + n_out]
        part = _dg(a_ref[...], b_ref[...], ca, cb, False)

        def finish(acc):
            res = (acc,) if epi is None else epi(acc, *[r[...] for r in ex_refs])
            for o_ref, r in zip(o_refs, res):
                o_ref[...] = r.astype(o_ref.dtype)

        if nk == 1:
            finish(part)
            return
        acc_ref = refs[-1]
        k = pl.program_id(2)

        @pl.when(k == 0)
        def _():
            acc_ref[...] = part

        @pl.when(k > 0)
        def _():
            acc_ref[...] += part

        @pl.when(k == nk - 1)
        def _():
            finish(acc_ref[...])

    outs = pl.pallas_call(
        body, name=name, grid=grid,
        in_specs=[a_spec, b_spec] + ex_specs + [ANY] * len(afters),
        out_specs=out_specs, out_shape=out_shape,
        scratch_shapes=[pltpu.VMEM((tm, tn), F32)] if nk > 1 else [],
        compiler_params=pltpu.CompilerParams(
            dimension_semantics=("parallel", "parallel", "arbitrary"),
            vmem_limit_bytes=VMEM_LIMIT),
    )(a, b, *extras, *afters)
    return outs[0] if n_out == 1 else tuple(outs)


def matmul_sum(name, pairs, tm=512, tn=1024, after=None):
    M, N = pairs[0][0].shape[0], pairs[0][1].shape[1]
    tm, tn = _tile(M, tm), _tile(N, tn, LANE)
    afters = [] if after is None else [after]
    in_specs, operands = [], []
    for a, b in pairs:
        assert a.shape == (M, b.shape[0]) and b.shape[1] == N, (name, a.shape, b.shape)
        in_specs += [pl.BlockSpec((tm, a.shape[1]), lambda i, j: (i, 0)),
                     pl.BlockSpec((b.shape[0], tn), lambda i, j: (0, j))]
        operands += [a, b]
    n = len(pairs)

    def body(*refs):
        acc = _dg(refs[0][...], refs[1][...], 1, 0, False)
        for p in range(1, n):
            acc = acc + _dg(refs[2 * p][...], refs[2 * p + 1][...], 1, 0, False)
        refs[-1][...] = acc

    return pl.pallas_call(
        body, name=name, grid=(M // tm, N // tn),
        in_specs=in_specs + [ANY] * len(afters),
        out_specs=pl.BlockSpec((tm, tn), lambda i, j: (i, j)),
        out_shape=jax.ShapeDtypeStruct((M, N), F32),
        compiler_params=pltpu.CompilerParams(
            dimension_semantics=("parallel", "parallel"), vmem_limit_bytes=VMEM_LIMIT),
    )(*operands, *afters)


def rowcall(name, fn, rows, consts, out_rows, out_accs=(), tr=256, sp=None, R=None, after=None,
            out_lead=None, into=None):
    afters = ([] if after is None else [after]) + ([] if into is None else list(into))
    rows = [r if isinstance(r, tuple) else (r, None) for r in rows]
    R = rows[0][0].shape[-2] if R is None else R
    tr = _tile(R, tr)
    n = R // tr
    in_specs = []
    for arr, lead in rows:
        C = arr.shape[-1]
        if lead is None:
            assert arr.shape[-2] == R, (name, arr.shape, R)
            in_specs.append(pl.BlockSpec((tr, C), lambda i, *s: (i, 0)))
        elif callable(lead):
            in_specs.append(pl.BlockSpec((tr, C), lambda i, *s, lead=lead: (lead(i, n, *s), 0)))
        else:
            assert arr.shape[-2] == R, (name, arr.shape, R)
            in_specs.append(pl.BlockSpec((None, tr, C), lambda i, *s, lead=lead: (lead, i, 0)))
    for c in consts:
        in_specs.append(pl.BlockSpec(c.shape, lambda i, *s, nd=c.ndim: (0,) * nd))
    if out_lead is None:
        out_shape = [jax.ShapeDtypeStruct((R, C), dt) for C, dt in out_rows]
        out_specs = [pl.BlockSpec((tr, C), lambda i, *s: (i, 0)) for C, _ in out_rows]
    else:
        n_slab, slab = out_lead
        out_shape = [jax.ShapeDtypeStruct((n_slab, R, C), dt) for C, dt in out_rows]
        out_specs = [pl.BlockSpec((None, tr, C), lambda i, *s: (slab, i, 0)) for C, _ in out_rows]
    for shp in out_accs:
        out_shape.append(jax.ShapeDtypeStruct(shp, F32))
        out_specs.append(pl.BlockSpec(shp, lambda i, *s, nd=len(shp): (0,) * nd))
    n_in, n_row, n_acc = len(rows) + len(consts), len(out_rows), len(out_accs)
    n_sp = 0 if sp is None else 1

    def body(*refs):
        refs = refs[n_sp:]
        ins = [r[...] for r in refs[:n_in]]
        res = fn(*ins)
        if not isinstance(res, (tuple, list)):
            res = (res,)
        o_refs = refs[n_in + len(afters):]
        for o_ref, r in zip(o_refs[:n_row], res[:n_row]):
            o_ref[...] = r.astype(o_ref.dtype)
        if n_acc:
            first = pl.program_id(0) == 0
            for o_ref, r in zip(o_refs[n_row:], res[n_row:]):
                r = r.astype(F32).reshape(o_ref.shape)

                @pl.when(first)
                def _(o_ref=o_ref, r=r):
                    o_ref[...] = r

                @pl.when(jnp.logical_not(first))
                def _(o_ref=o_ref, r=r):
                    o_ref[...] += r

    params = pltpu.CompilerParams(dimension_semantics=("arbitrary",), vmem_limit_bytes=VMEM_LIMIT)
    operands = [a for a, _ in rows] + list(consts) + afters
    in_specs = in_specs + [ANY] * len(afters)
    aliases = {} if into is None else {n_sp + len(operands) - len(into) + k: k for k in range(len(into))}
    if sp is None:
        outs = pl.pallas_call(body, name=name, grid=(n,), in_specs=in_specs, out_specs=out_specs,
                              out_shape=out_shape, input_output_aliases=aliases,
                              compiler_params=params)(*operands)
    else:
        outs = pl.pallas_call(
            body, name=name, out_shape=out_shape, compiler_params=params, input_output_aliases=aliases,
            grid_spec=pltpu.PrefetchScalarGridSpec(
                num_scalar_prefetch=1, grid=(n,), in_specs=in_specs, out_specs=out_specs),
        )(sp, *operands)
    return outs[0] if len(outs) == 1 else tuple(outs)


def slabcall(name, fn, ins, out_dtypes, sp, n_out_slabs=None, cw=512):
    R, C = ins[0][0].shape[-2:]
    cw = _tile(C, cw, LANE)
    in_specs = []
    for arr, slab in ins:
        assert arr.shape[-2:] == (R, C), (name, arr.shape)
        if slab is None:
            in_specs.append(pl.BlockSpec((R, cw), lambda i, j, s: (0, j)))
        elif callable(slab):
            in_specs.append(pl.BlockSpec((None, R, cw), lambda i, j, s, slab=slab: (slab(i, s), 0, j)))
        else:
            in_specs.append(pl.BlockSpec((None, R, cw), lambda i, j, s, slab=slab: (slab, 0, j)))
    if n_out_slabs is None:
        out_shape = [jax.ShapeDtypeStruct((R, C), dt) for dt in out_dtypes]
        out_specs = [pl.BlockSpec((R, cw), lambda i, j, s: (0, j)) for _ in out_dtypes]
    else:
        out_shape = [jax.ShapeDtypeStruct((n_out_slabs, R, C), dt) for dt in out_dtypes]
        out_specs = [pl.BlockSpec((None, R, cw), lambda i, j, s: (i, 0, j)) for _ in out_dtypes]
    n_in = len(ins)

    def body(sp_ref, *refs):
        res = fn(*[r[...] for r in refs[:n_in]])
        for o_ref, r in zip(refs[n_in:], res):
            o_ref[...] = r.astype(o_ref.dtype)

    outs = pl.pallas_call(
        body, name=name, out_shape=out_shape,
        grid_spec=pltpu.PrefetchScalarGridSpec(
            num_scalar_prefetch=1, grid=(n_out_slabs or 1, C // cw), in_specs=in_specs, out_specs=out_specs),
        compiler_params=pltpu.CompilerParams(
            dimension_semantics=("arbitrary", "arbitrary"), vmem_limit_bytes=VMEM_LIMIT),
    )(sp, *[a for a, _ in ins])
    return outs[0] if len(outs) == 1 else tuple(outs)


def rms_tile(x, g):
    x = x.astype(F32)
    return x * lax.rsqrt(jnp.mean(x * x, axis=-1, keepdims=True) + EPS) * g


def gelu(x):
    return 0.5 * x * (1.0 + lax.erf(x * (1.0 / math.sqrt(2.0))))


def ln_tile(x, g, b):
    mu = jnp.mean(x, axis=-1, keepdims=True)
    xc = x - mu
    return xc * lax.rsqrt(jnp.mean(xc * xc, axis=-1, keepdims=True) + EPS) * g + b


def lane_groups(fn, width, *arrs):
    n = arrs[0].shape[-1] // width
    outs = [fn(*[a[:, i * width:(i + 1) * width] for a in arrs]) for i in range(n)]
    return jnp.concatenate(outs, axis=-1)


def mixa_post_tile(o, z, o_norm):
    dv = o_norm.shape[-1]
    on = lane_groups(lambda t: rms_tile(t, o_norm), dv, o)
    return on * jax.nn.silu(z)


def mixb_tile(uv, ln_g, ln_b, w_s, bs_t):
    G = w_s.shape[0]
    gw = ln_g.shape[-1]
    dg = gw // G
    tr = uv.shape[0]
    u = gelu(uv[:, :gw])
    vg = gelu(uv[:, gw:])
    ii = lax.broadcasted_iota(jnp.int32, (B_BLOCK, B_BLOCK), 0)
    jj = lax.broadcasted_iota(jnp.int32, (B_BLOCK, B_BLOCK), 1)
    mask = (jj // CHUNK) <= (ii // CHUNK)
    cols = []
    for g in range(G):
        sl = slice(g * dg, (g + 1) * dg)
        vn = ln_tile(vg[:, sl], ln_g[:, sl], ln_b[:, sl])
        wm = jnp.where(mask, w_s[g], 0.0)
        blocks = []
        for m in range(tr // B_BLOCK):
            blk = vn[m * B_BLOCK:(m + 1) * B_BLOCK, :]
            blocks.append(mm(wm, blk, 1, 0) + bs_t[:, g:g + 1])
        mixed = blocks[0] if len(blocks) == 1 else jnp.concatenate(blocks, axis=0)
        cols.append(u[:, sl] * mixed)
    return jnp.concatenate(cols, axis=-1)


def bgc_tile(ba, alog_row, dtb_row):
    tr = ba.shape[0]
    beta = jax.nn.sigmoid(ba)
    g = -jnp.exp(alog_row) * jax.nn.softplus(ba + dtb_row)
    ii = lax.broadcasted_iota(jnp.int32, (tr, tr), 0)
    jj = lax.broadcasted_iota(jnp.int32, (tr, tr), 1)
    tri = jnp.where((ii // CHUNK == jj // CHUNK) & (jj <= ii), 1.0, 0.0).astype(F32)
    gc = mm(tri, g, 1, 0, True)
    return beta, gc


def make_bgc(H):
    def f(ba, alog_row, dtb_row):
        beta, gc = bgc_tile(ba, alog_row, dtb_row)
        lane = lax.broadcasted_iota(jnp.int32, ba.shape, 1)
        return jnp.where(lane < H, beta, jnp.where(lane < 2 * H, gc, 0.0))
    return f


def loss_tile(x, g, target):
    y = rms_tile(x, g)
    err = y - target
    return 0.5 * jnp.sum(jnp.mean(err * err, axis=-1, keepdims=True), axis=0, keepdims=True)


def adamw_tile(w, g, m, v):
    m = ADAM_B1 * m + (1.0 - ADAM_B1) * g
    v = ADAM_B2 * v + (1.0 - ADAM_B2) * (g * g)
    m_hat = m / (1.0 - ADAM_B1 ** ADAM_STEP)
    v_hat = v / (1.0 - ADAM_B2 ** ADAM_STEP)
    delta = -ADAM_LR * (m_hat / (jnp.sqrt(v_hat) + ADAM_EPS) + ADAM_WD * w)
    return delta, m, v


CONV_ROWS = 32


def _shifted_copies(src, sh, rows):
    for b in range(SUBLANE):
        sh[b] = src[pl.ds(b, rows), :]


def _window(sh, off, rows):
    b = off % SUBLANE
    return sh[b, pl.ds(off - b, rows), :]


def _conv_rows(out, sh, w_ref, offsets, rows):
    for r0 in range(0, rows, CONV_ROWS):
        rc = min(CONV_ROWS, rows - r0)
        acc = w_ref[0:1, :] * _window(sh, offsets[0] + r0, rc)
        for k in range(1, len(offsets)):
            acc = acc + w_ref[k:k + 1, :] * _window(sh, offsets[k] + r0, rc)
        out[r0:r0 + rc, :] = acc


def _conv_wgrad(dsrc, d0, sh, offsets, rows):
    dws = []
    for off in offsets:
        acc = None
        for r0 in range(0, rows, CONV_ROWS):
            rc = min(CONV_ROWS, rows - r0)
            prod = dsrc[d0 + r0:d0 + r0 + rc, :] * _window(sh, off + r0, rc)
            for g in range(0, rc, SUBLANE):
                part = prod[g:g + SUBLANE, :]
                acc = part if acc is None else acc + part
        dws.append(jnp.sum(acc, axis=0, keepdims=True))
    return jnp.concatenate(dws, axis=0)


def _conv_specs(T, tr, hb, cw, col_blocks, rev):
    n = T // tr

    def ri(i):
        return (n - 1 - i) if rev else i

    tile_specs, halo_specs = [], []
    for off in col_blocks:
        tile_specs.append(pl.BlockSpec((tr, cw), lambda j, i, off=off: (ri(i), j + off)))
        halo_specs.append(pl.BlockSpec(
            (hb, cw), lambda j, i, off=off: (jnp.maximum(ri(i) * (tr // hb) - 1, 0), j + off)))
    return n, ri, tile_specs, halo_specs


def conv_fwd(name, x, w, consts, pre, post, col_blocks, n_out, K, out_dtype=F32, tr=256, hb=8, cw=512):
    T = x.shape[0]
    C = w.shape[1]
    tr, cw = _tile(T, tr, hb), min(cw, C)
    nb = len(col_blocks)
    n, ri, tile_specs, halo_specs = _conv_specs(T, tr, hb, cw, col_blocks, False)
    w_spec = pl.BlockSpec((K, cw), lambda j, i: (0, j))
    c_specs = [pl.BlockSpec((1, cw), lambda j, i: (0, j)) for _ in consts]

    def body(*refs):
        tiles = [r[...] for r in refs[:nb]]
        halos = [r[...] for r in refs[nb:2 * nb]]
        w_ref = refs[2 * nb]
        cs = [r[...] for r in refs[2 * nb + 1:2 * nb + 1 + len(consts)]]
        o_refs = refs[2 * nb + 1 + len(consts):-3]
        pbuf, shp, cbuf = refs[-3:]
        i = pl.program_id(1)
        pbuf[0:hb, :] = jnp.where(i > 0, pre(*halos), 0.0)
        pbuf[hb:hb + tr, :] = pre(*tiles)
        pbuf[hb + tr:hb + tr + SUBLANE, :] = jnp.zeros((SUBLANE, cw), F32)
        _shifted_copies(pbuf, shp, hb + tr)
        _conv_rows(cbuf, shp, w_ref, [hb - (K - 1) + k for k in range(K)], tr)
        res = post(cbuf[...], pl.program_id(0), *cs)
        for o_ref, r in zip(o_refs, res):
            o_ref[...] = r.astype(o_ref.dtype)

    outs = pl.pallas_call(
        body, name=name, grid=(C // cw, n),
        in_specs=tile_specs + halo_specs + [w_spec] + c_specs,
        out_specs=[pl.BlockSpec((tr, cw), lambda j, i: (i, j)) for _ in range(n_out)],
        out_shape=[jax.ShapeDtypeStruct((T, C), out_dtype) for _ in range(n_out)],
        scratch_shapes=[pltpu.VMEM((hb + tr + SUBLANE, cw), F32), pltpu.VMEM((SUBLANE, hb + tr, cw), F32),
                        pltpu.VMEM((tr, cw), F32)],
        compiler_params=pltpu.CompilerParams(
            dimension_semantics=("parallel", "arbitrary"), vmem_limit_bytes=VMEM_LIMIT),
    )(*([x] * nb), *([x] * nb), w, *consts)
    return outs[0] if n_out == 1 else tuple(outs)


def conv_bwd(name, x, w, consts, grads, pre, post, col_blocks, K, tr=256, hb=8, cw=512, recompute=True):
    T, Cx = x.shape
    C = w.shape[1]
    tr, cw = _tile(T, tr, hb), min(cw, C)
    nb = len(col_blocks)
    n, ri, tile_specs, halo_specs = _conv_specs(T, tr, hb, cw, col_blocks, True)
    w_spec = pl.BlockSpec((K, cw), lambda j, i: (0, j))
    c_specs = [pl.BlockSpec((1, cw), lambda j, i: (0, j)) for _ in consts]
    g_specs = [pl.BlockSpec((tr, cw), lambda j, i: (ri(i), j)) for _ in grads]
    nc, ng = len(consts), len(grads)

    def body(*refs):
        p = 0
        tile_refs = refs[p:p + nb]; p += nb
        halo_refs = refs[p:p + nb]; p += nb
        w_ref = refs[p]; p += 1
        cs = [r[...] for r in refs[p:p + nc]]; p += nc
        gs = [r[...] for r in refs[p:p + ng]]; p += ng
        dx_refs = refs[p:p + nb]; p += nb
        sum_refs = refs[p:p + nb]; p += nb
        dw_ref = refs[p]; p += 1
        dc_refs = refs[p:p + nc]; p += nc
        pbuf, dbuf, ebuf, carry, shp, shd, cbuf = refs[p:p + 7]
        i = pl.program_id(1)
        first = i == 0
        tiles = [r[...] for r in tile_refs]
        halos = [r[...] for r in halo_refs]
        p_tile, vjp_pre = jax.vjp(pre, *tiles)
        pbuf[0:hb, :] = jnp.where(ri(i) > 0, pre(*halos), 0.0)
        pbuf[hb:hb + tr, :] = p_tile
        pbuf[hb + tr:hb + tr + SUBLANE, :] = jnp.zeros((SUBLANE, cw), F32)
        _shifted_copies(pbuf, shp, hb + tr)
        taps = [hb - (K - 1) + k for k in range(K)]
        if recompute:
            _conv_rows(cbuf, shp, w_ref, taps, tr)
            c = cbuf[...]
        else:
            c = jnp.zeros((tr, cw), F32)
        cid = pl.program_id(0)
        _, vjp_post = jax.vjp(lambda c_, *cs_: post(c_, cid, *cs_), c, *cs)
        dres = vjp_post(tuple(g.astype(F32) for g in gs))
        dbuf[0:hb, :] = jnp.zeros((hb, cw), F32)
        dbuf[hb:hb + tr, :] = dres[0]
        dbuf[hb + tr:hb + tr + hb + SUBLANE, :] = jnp.zeros((hb + SUBLANE, cw), F32)
        _shifted_copies(dbuf, shd, hb + tr + hb)
        _conv_rows(ebuf, shd, w_ref, [K - 1 - k for k in range(K)], hb + tr)
        dw = _conv_wgrad(dbuf, hb, shp, taps, tr)

        @pl.when(jnp.logical_not(first))
        def _():
            ebuf[tr:tr + hb, :] += carry[...]

        carry[...] = ebuf[0:hb, :]
        dtiles = vjp_pre(ebuf[hb:hb + tr, :])
        for r, s, d in zip(dx_refs, sum_refs, dtiles):
            r[...] = d.astype(r.dtype)
            ds_ = jnp.sum(d, axis=0, keepdims=True)

            @pl.when(first)
            def _(s=s, ds_=ds_):
                s[...] = ds_

            @pl.when(jnp.logical_not(first))
            def _(s=s, ds_=ds_):
                s[...] += ds_

        accs = [(dw_ref, dw)] + [(r, d) for r, d in zip(dc_refs, dres[1:])]
        for r, d in accs:
            @pl.when(first)
            def _(r=r, d=d):
                r[...] = d

            @pl.when(jnp.logical_not(first))
            def _(r=r, d=d):
                r[...] += d

    n_cb = C // cw
    outs = pl.pallas_call(
        body, name=name, grid=(n_cb, n),
        in_specs=tile_specs + halo_specs + [w_spec] + c_specs + g_specs,
        out_specs=([pl.BlockSpec((tr, cw), lambda j, i: (ri(i), j)) for _ in col_blocks]
                   + [pl.BlockSpec((1, cw), lambda j, i: (0, j)) for _ in col_blocks]
                   + [pl.BlockSpec((K, cw), lambda j, i: (0, j))]
                   + [pl.BlockSpec((1, cw), lambda j, i: (0, j)) for _ in consts]),
        out_shape=([jax.ShapeDtypeStruct((T, C), BF) for _ in col_blocks]
                   + [jax.ShapeDtypeStruct((1, C), F32) for _ in col_blocks]
                   + [jax.ShapeDtypeStruct((K, C), F32)]
                   + [jax.ShapeDtypeStruct((1, C), F32) for _ in consts]),
        scratch_shapes=[pltpu.VMEM((hb + tr + SUBLANE, cw), F32), pltpu.VMEM((hb + tr + hb + SUBLANE, cw), F32),
                        pltpu.VMEM((hb + tr, cw), F32), pltpu.VMEM((hb, cw), F32),
                        pltpu.VMEM((SUBLANE, hb + tr, cw), F32), pltpu.VMEM((SUBLANE, hb + tr + hb, cw), F32),
                        pltpu.VMEM((tr, cw), F32)],
        compiler_params=pltpu.CompilerParams(
            dimension_semantics=("parallel", "arbitrary"), vmem_limit_bytes=VMEM_LIMIT),
    )(*([x] * nb), *([x] * nb), w, *consts, *grads)
    dxs = outs[:nb]
    sums = outs[nb:2 * nb]
    dw = outs[2 * nb]
    dcs = outs[2 * nb + 1:]
    return dxs, sums, dw, dcs


def make_qkv_post(dk, cw, n_qk_chunks):
    def l2(t):
        return t * lax.rsqrt(jnp.sum(t * t, axis=-1, keepdims=True) + EPS)

    def post(c, cid):
        s = jax.nn.silu(c)
        normed = lane_groups(l2, dk, s)
        return (jnp.where(cid < n_qk_chunks, normed, s),)
    return post


def glu_pre(za, zb):
    return za * jax.nn.sigmoid(zb)


def bias_post(c, cid, b):
    return (c + b,)


def _col_to_row(col):
    C = col.shape[-2]
    ii = lax.broadcasted_iota(jnp.int32, (C, C), 0)
    jj = lax.broadcasted_iota(jnp.int32, (C, C), 1)
    wide = jnp.broadcast_to(col, col.shape[:-1] + (C,))
    return jnp.sum(jnp.where(ii == jj, wide, 0.0), axis=-2, keepdims=True)


@jax.custom_vjp
def solve_with_inverse(a, rhs, x):
    return mm(x, rhs, 1, 0, True)


def _swi_fwd(a, rhs, x):
    sol = mm(x, rhs, 1, 0, True)
    return sol, (x, sol)


def _swi_bwd(res, dsol):
    x, sol = res
    drhs = mm(x, dsol, 0, 0, True)
    da = -mm(drhs, sol, 1, 1, True)
    return da, drhs, jnp.zeros_like(x)


solve_with_inverse.defvjp(_swi_fwd, _swi_bwd)


def unit_lower_inverse(a):
    C = a.shape[-1]
    ii = lax.broadcasted_iota(jnp.int32, (C, C), 0)
    jj = lax.broadcasted_iota(jnp.int32, (C, C), 1)
    x = jnp.where(ii == jj, 1.0, 0.0).astype(F32) - a
    p = mm(a, a, 1, 0, True)
    steps = int(math.log2(C)) - 1
    for s in range(steps):
        x = x + mm(x, p, 1, 0, True)
        if s < steps - 1:
            p = mm(p, p, 1, 0, True)
    return x


def dn_masks(C):
    ii = lax.broadcasted_iota(jnp.int32, (C, C), 0)
    jj = lax.broadcasted_iota(jnp.int32, (C, C), 1)
    return ii >= jj, ii > jj


def dn_pre(q, k, v, beta, gc):
    C, dk = q.shape[-2:]
    tri, strict = dn_masks(C)
    q = q * (dk ** -0.5)
    diff = gc - _col_to_row(gc)
    decay = jnp.where(tri, jnp.exp(jnp.where(tri, diff, 0.0)), 0.0)
    kb = k * beta
    vb = v * beta
    a = jnp.where(strict, mm(kb, k, 1, 1) * decay, 0.0)
    eg = jnp.exp(gc)
    rhs = jnp.concatenate([vb, kb * eg], axis=-1)
    attn = mm(q, k, 1, 1) * decay
    qd = q * eg
    g_last = gc[..., C - 1:C, :]
    kt = k * jnp.exp(g_last - gc)
    gl = jnp.exp(g_last)
    return a, rhs, attn, qd, kt, gl


def dn_chunk(q, k, v, beta, gc, state, x):
    dv = v.shape[-1]
    a, rhs, attn, qd, kt, gl = dn_pre(q, k, v, beta, gc)
    sol = solve_with_inverse(a, rhs, x)
    u, w = sol[..., :dv], sol[..., dv:]
    vn = u - mm(w, state, 1, 0)
    o = mm(qd, state, 1, 0) + mm(attn, vn, 1, 0)
    new_state = state * gl + mm(kt, vn, 0, 0)
    return o, new_state


def _by_head(q_ref, k_ref, v_ref, bg, H, dk, dv):
    qv = jnp.stack([q_ref[:, h * dk:(h + 1) * dk] for h in range(H)])
    kv = jnp.stack([k_ref[:, h * dk:(h + 1) * dk] for h in range(H)])
    vv = jnp.stack([v_ref[:, h * dv:(h + 1) * dv] for h in range(H)])
    beta = jnp.stack([bg[:, h:h + 1] for h in range(H)])
    gc = jnp.stack([bg[:, H + h:H + h + 1] for h in range(H)])
    return qv, kv, vv, beta, gc


def deltanet_fwd(qkv, bgc, H):
    T = qkv.shape[0]
    dk = dv = qkv.shape[1] // (3 * H)
    N = T // CHUNK

    def body(q_ref, k_ref, v_ref, bgc_ref, o_ref, x_ref, s_ref, state):
        @pl.when(pl.program_id(0) == 0)
        def _():
            state[...] = jnp.zeros((H, dk, dv), F32)

        qv, kv, vv, beta, gc = _by_head(q_ref, k_ref, v_ref, bgc_ref[...], H, dk, dv)
        a = dn_pre(qv, kv, vv, beta, gc)[0]
        x = unit_lower_inverse(a)
        s = state[...]
        o, s_new = dn_chunk(qv, kv, vv, beta, gc, s, x)
        for h in range(H):
            o_ref[:, h * dv:(h + 1) * dv] = o[h]
        x_ref[...] = x
        s_ref[...] = s
        state[...] = s_new

    return pl.pallas_call(
        body, name="deltanet_fwd", grid=(N,),
        in_specs=[pl.BlockSpec((CHUNK, H * dk), lambda n: (n, 0)),
                  pl.BlockSpec((CHUNK, H * dk), lambda n: (n, 1)),
                  pl.BlockSpec((CHUNK, H * dv), lambda n: (n, 2)),
                  pl.BlockSpec((CHUNK, LANE), lambda n: (n, 0))],
        out_specs=[pl.BlockSpec((CHUNK, H * dv), lambda n: (n, 0)),
                   pl.BlockSpec((None, H, CHUNK, CHUNK), lambda n: (n, 0, 0, 0)),
                   pl.BlockSpec((None, H, dk, dv), lambda n: (n, 0, 0, 0))],
        out_shape=[jax.ShapeDtypeStruct((T, H * dv), F32),
                   jax.ShapeDtypeStruct((N, H, CHUNK, CHUNK), F32),
                   jax.ShapeDtypeStruct((N, H, dk, dv), F32)],
        scratch_shapes=[pltpu.VMEM((H, dk, dv), F32)],
        compiler_params=pltpu.CompilerParams(
            dimension_semantics=("arbitrary",), vmem_limit_bytes=VMEM_LIMIT),
    )(qkv, qkv, qkv, bgc)


def deltanet_bwd(qkv, bgc, xinv, states, do, H):
    T = qkv.shape[0]
    dk = dv = qkv.shape[1] // (3 * H)
    N = T // CHUNK

    def body(q_ref, k_ref, v_ref, bgc_ref, x_ref, s_ref, do_ref, dqkv_ref, dbgc_ref, dstate):
        @pl.when(pl.program_id(0) == 0)
        def _():
            dstate[...] = jnp.zeros((H, dk, dv), F32)

        qv, kv, vv, beta, gc = _by_head(q_ref, k_ref, v_ref, bgc_ref[...], H, dk, dv)
        do = jnp.stack([do_ref[:, h * dv:(h + 1) * dv] for h in range(H)])
        _, vjp = jax.vjp(dn_chunk, qv, kv, vv, beta, gc, s_ref[...], x_ref[...])
        dq, dk_, dv_, dbeta, dgc, ds, _ = vjp((do, dstate[...]))
        dstate[...] = ds
        lane = lax.broadcasted_iota(jnp.int32, (CHUNK, LANE), 1)
        dbgc = jnp.zeros((CHUNK, LANE), F32)
        for h in range(H):
            dqkv_ref[:, h * dk:(h + 1) * dk] = dq[h]
            dqkv_ref[:, (H + h) * dk:(H + h + 1) * dk] = dk_[h]
            dqkv_ref[:, (2 * H + h) * dk:(2 * H + h + 1) * dk] = dv_[h]
            dbgc = dbgc + jnp.where(lane == h, dbeta[h], 0.0) + jnp.where(lane == h + H, dgc[h], 0.0)
        dbgc_ref[...] = dbgc

    rn = lambda n: N - 1 - n
    return pl.pallas_call(
        body, name="deltanet_bwd", grid=(N,),
        in_specs=[pl.BlockSpec((CHUNK, H * dk), lambda n: (rn(n), 0)),
                  pl.BlockSpec((CHUNK, H * dk), lambda n: (rn(n), 1)),
                  pl.BlockSpec((CHUNK, H * dv), lambda n: (rn(n), 2)),
                  pl.BlockSpec((CHUNK, LANE), lambda n: (rn(n), 0)),
                  pl.BlockSpec((None, H, CHUNK, CHUNK), lambda n: (rn(n), 0, 0, 0)),
                  pl.BlockSpec((None, H, dk, dv), lambda n: (rn(n), 0, 0, 0)),
                  pl.BlockSpec((CHUNK, H * dv), lambda n: (rn(n), 0))],
        out_specs=[pl.BlockSpec((CHUNK, 3 * H * dk), lambda n: (rn(n), 0)),
                   pl.BlockSpec((CHUNK, LANE), lambda n: (rn(n), 0))],
        out_shape=[jax.ShapeDtypeStruct((T, 3 * H * dk), F32),
                   jax.ShapeDtypeStruct((T, LANE), F32)],
        scratch_shapes=[pltpu.VMEM((H, dk, dv), F32)],
        compiler_params=pltpu.CompilerParams(
            dimension_semantics=("arbitrary",), vmem_limit_bytes=VMEM_LIMIT),
    )(qkv, qkv, qkv, bgc, xinv, states, do)


def _place():
    x, y, c = lax.axis_index("x"), lax.axis_index("y"), lax.axis_index("c")
    chips = [(1 - x, y), (x, 1 - y), (1 - x, 1 - y)]
    return x, y, c, chips


def all_gather(name, shards):
    na = len(shards)

    def body(*refs):
        ins, outs = refs[:na], refs[na:2 * na]
        send_sems, recv_sems, local_sems = refs[2 * na:]
        x, y, c, chips = _place()
        me, sibling = (x, y, c), (x, y, 1 - c)

        def copy(a, k, block, to, src=None):
            dst = outs[a].at[4 * block[0] + 2 * block[1] + block[2]]
            return pltpu.make_async_remote_copy(
                src_ref=dst if src is None else src, dst_ref=dst,
                send_sem=send_sems.at[a, k], recv_sem=recv_sems.at[a, k],
                device_id=to, device_id_type=MESH)

        mine, first, passed = [], [], []
        for a in range(na):
            cp = pltpu.make_async_copy(ins[a], outs[a].at[4 * x + 2 * y + c], local_sems.at[a])
            cp.start()
            mine.append(cp)
        for a in range(na):
            cps = [copy(a, 0, me, sibling, src=ins[a])]
            cps += [copy(a, 1 + j, me, (*chip, c), src=ins[a]) for j, chip in enumerate(chips)]
            for cp in cps:
                cp.start()
            first += cps
        for a in range(na):
            for j, chip in enumerate(chips):
                copy(a, 1 + j, (*chip, c), me).wait_recv()
                cp = copy(a, 4 + j, (*chip, c), sibling)
                cp.start()
                passed.append(cp)
        for a in range(na):
            copy(a, 0, sibling, me).wait_recv()
            for j, chip in enumerate(chips):
                copy(a, 4 + j, (*chip, 1 - c), me).wait_recv()
        for cp in first + passed:
            cp.wait_send()
        for cp in mine:
            cp.wait()

    outs = pl.pallas_call(
        body, name=name,
        in_specs=[ANY] * na, out_specs=[ANY] * na,
        out_shape=[jax.ShapeDtypeStruct((N_DEV,) + s.shape, s.dtype) for s in shards],
        scratch_shapes=[pltpu.SemaphoreType.DMA((na, 7)), pltpu.SemaphoreType.DMA((na, 7)),
                        pltpu.SemaphoreType.DMA((na,))],
    )(*shards)
    return list(outs)


HBM_SPEC = pl.BlockSpec(memory_space=pltpu.HBM)
SEM_SPEC = pl.BlockSpec(memory_space=pltpu.SEMAPHORE)
EFFECT = pltpu.SideEffectType.DATAFLOW_SIDE_EFFECTING


def _descriptors(plan, bufs, send_sems, recv_sems):
    return [pltpu.make_async_remote_copy(src_ref=src, dst_ref=dst, send_sem=send_sems.at[k],
                                         recv_sem=recv_sems.at[k], device_id=dev, device_id_type=MESH)
            for k, (src, dst, dev) in enumerate(plan(bufs))]


def split_start(name, bufs, plan, n, after):
    nb = len(bufs)

    def body(*refs):
        for cp in _descriptors(plan, refs[:nb], refs[nb + 1], refs[nb + 2]):
            cp.start()
        refs[-1][...] = jnp.zeros((SUBLANE, LANE), F32)

    outs = pl.pallas_call(
        body, name=name,
        out_shape=(pltpu.SemaphoreType.DMA((n,)), pltpu.SemaphoreType.DMA((n,)),
                   *[pltpu.HBM(b.shape, b.dtype) for b in bufs],
                   jax.ShapeDtypeStruct((SUBLANE, LANE), F32)),
        in_specs=[HBM_SPEC] * nb + [ANY],
        out_specs=(SEM_SPEC, SEM_SPEC, *[HBM_SPEC] * nb, pl.BlockSpec(memory_space=pltpu.VMEM)),
        input_output_aliases={i: 2 + i for i in range(nb)},
        compiler_params=pltpu.CompilerParams(has_side_effects=EFFECT),
    )(*[pltpu.with_memory_space_constraint(b, pltpu.HBM) for b in bufs], after)
    return outs[0], outs[1], list(outs[2:2 + nb]), outs[-1]


def split_wait(name, send_sems, recv_sems, bufs, plan, after):
    nb = len(bufs)

    def body(*refs):
        cps = _descriptors(plan, refs[:nb], refs[nb], refs[nb + 1])
        for cp in cps:
            cp.wait_recv()
        for cp in cps:
            cp.wait_send()
        refs[-1][...] = jnp.zeros((SUBLANE, LANE), F32)

    afters = list(after) if isinstance(after, (list, tuple)) else [after]
    outs = pl.pallas_call(
        body, name=name,
        out_shape=[pltpu.HBM(b.shape, b.dtype) for b in bufs] + [jax.ShapeDtypeStruct((SUBLANE, LANE), F32)],
        in_specs=[HBM_SPEC] * nb + [SEM_SPEC, SEM_SPEC] + [ANY] * len(afters),
        out_specs=[HBM_SPEC] * nb + [pl.BlockSpec(memory_space=pltpu.VMEM)],
        input_output_aliases={i: i for i in range(nb)},
        compiler_params=pltpu.CompilerParams(has_side_effects=EFFECT),
    )(*bufs, send_sems, recv_sems, *afters)
    return list(outs[:nb]), outs[-1]


def _block(px, py, pc):
    return 4 * px + 2 * py + pc


def plan_gather_ici(na):
    def plan(bufs):
        x, y, c, chips = _place()
        out = []
        for a in range(na):
            dst = bufs[na + a].at[_block(x, y, c)]
            out.append((bufs[a], dst, (x, y, 1 - c)))
            out += [(bufs[a], dst, (px, py, c)) for px, py in chips]
        return out
    return plan


def plan_gather_pass(na):
    def plan(bufs):
        x, y, c, chips = _place()
        out = []
        for a in range(na):
            for px, py in chips:
                blk = bufs[a].at[_block(px, py, c)]
                out.append((blk, blk, (x, y, 1 - c)))
        return out
    return plan


def plan_reduce_d2d(na):
    def plan(bufs):
        x, y, c, _ = _place()
        return [(bufs[a].at[2 * s + (1 - c)], bufs[na + a].at[s], (x, y, 1 - c))
                for a in range(na) for s in range(4)]
    return plan


def plan_reduce_ici(na):
    def plan(bufs):
        x, y, c, chips = _place()
        return [(bufs[a].at[2 * px + py], bufs[na + a].at[j], (px, py, c))
                for a in range(na) for j, (px, py) in enumerate(chips)]
    return plan


def place_own(name, land, shard, dev):
    r, c = shard.shape
    tr = _tile(r, 512)

    def body(sp_ref, s_ref, land_ref, o_ref):
        o_ref[...] = s_ref[...]

    return pl.pallas_call(
        body, name=name, out_shape=jax.ShapeDtypeStruct(land.shape, land.dtype),
        grid_spec=pltpu.PrefetchScalarGridSpec(
            num_scalar_prefetch=1, grid=(r // tr,),
            in_specs=[pl.BlockSpec((tr, c), lambda i, s: (i, 0)), ANY],
            out_specs=pl.BlockSpec((None, tr, c), lambda i, s: (s[0], i, 0))),
        input_output_aliases={2: 0},
        compiler_params=pltpu.CompilerParams(dimension_semantics=("arbitrary",)),
    )(dev, shard, land)


def pack(arrs, row_mult=SUBLANE):
    pieces = []
    for a in arrs:
        f = a.reshape(-1).astype(F32)
        pad = (-f.shape[0]) % LANE
        if pad:
            f = jnp.concatenate([f, jnp.zeros((pad,), F32)])
        pieces.append(f)
    flat = jnp.concatenate(pieces)
    rows = flat.shape[0] // LANE
    pad_rows = (-rows) % row_mult
    if pad_rows:
        flat = jnp.concatenate([flat, jnp.zeros((pad_rows * LANE,), F32)])
    return flat.reshape(-1, LANE)


def unpack(buf, shapes):
    flat = buf.reshape(-1)
    outs, off = [], 0
    for shp in shapes:
        n = int(np.prod(shp))
        outs.append(flat[off:off + n].reshape(shp))
        off += n + ((-n) % LANE)
    return outs


def _vjp_rows(fn, n_row_in, n_cot):
    def bwd(*args):
        rows = args[:n_row_in]
        cots = args[n_row_in:n_row_in + n_cot]
        consts = args[n_row_in + n_cot:]
        out, vjp = jax.vjp(fn, *rows, *consts)
        if isinstance(out, (tuple, list)):
            cot = tuple(c.astype(o.dtype) for c, o in zip(cots, out))
        else:
            cot = cots[0].astype(out.dtype)
        return vjp(cot)
    return bwd


def rms_fwd(name, x, g):
    return rowcall(name, rms_tile, [x], [g], [(x.shape[1], BF)])


def rms_bwd(name, x, g, dh, dres, after=None):
    D = x.shape[1]
    vj = _vjp_rows(rms_tile, 1, 1)

    def f(x_, dh_, dres_, g_):
        dx, dg = vj(x_, dh_, g_)
        dx = dx + dres_
        return dx, dx, dg
    return rowcall(name, f, [x, dh, dres], [g], [(D, F32), (D, BF)], [(1, D)], after=after)


def kernel(x, e_norm, e_w_in, e_conv_w, e_a_log, e_dt_bias, e_o_norm, e_ln_g, e_ln_b, e_w_s, e_b_s, e_w_out, o_norm, o_pw1, o_pw1_b, o_dw, o_dw_b, o_ln_g, o_ln_b, o_pw2, o_pw2_b, f_norm, f_w1, f_w2, final_norm, loss_target, m_e_norm, m_e_w_in, m_e_conv_w, m_e_a_log, m_e_dt_bias, m_e_o_norm, m_e_ln_g, m_e_ln_b, m_e_w_s, m_e_b_s, m_e_w_out, m_o_norm, m_o_pw1, m_o_pw1_b, m_o_dw, m_o_dw_b, m_o_ln_g, m_o_ln_b, m_o_pw2, m_o_pw2_b, m_f_norm, m_f_w1, m_f_w2, m_final_norm, v_e_norm, v_e_w_in, v_e_conv_w, v_e_a_log, v_e_dt_bias, v_e_o_norm, v_e_ln_g, v_e_ln_b, v_e_w_s, v_e_b_s, v_e_w_out, v_o_norm, v_o_pw1, v_o_pw1_b, v_o_dw, v_o_dw_b, v_o_ln_g, v_o_ln_b, v_o_pw2, v_o_pw2_b, v_f_norm, v_f_w1, v_f_w2, v_final_norm):
    names = ['e_norm', 'e_w_in', 'e_conv_w', 'e_a_log', 'e_dt_bias', 'e_o_norm', 'e_ln_g', 'e_ln_b', 'e_w_s', 'e_b_s', 'e_w_out', 'o_norm', 'o_pw1', 'o_pw1_b', 'o_dw', 'o_dw_b', 'o_ln_g', 'o_ln_b', 'o_pw2', 'o_pw2_b', 'f_norm', 'f_w1', 'f_w2', 'final_norm']
    W = dict(zip(names, [e_norm, e_w_in, e_conv_w, e_a_log, e_dt_bias, e_o_norm, e_ln_g, e_ln_b, e_w_s, e_b_s, e_w_out, o_norm, o_pw1, o_pw1_b, o_dw, o_dw_b, o_ln_g, o_ln_b, o_pw2, o_pw2_b, f_norm, f_w1, f_w2, final_norm]))
    Mo = dict(zip(names, [m_e_norm, m_e_w_in, m_e_conv_w, m_e_a_log, m_e_dt_bias, m_e_o_norm, m_e_ln_g, m_e_ln_b, m_e_w_s, m_e_b_s, m_e_w_out, m_o_norm, m_o_pw1, m_o_pw1_b, m_o_dw, m_o_dw_b, m_o_ln_g, m_o_ln_b, m_o_pw2, m_o_pw2_b, m_f_norm, m_f_w1, m_f_w2, m_final_norm]))
    Vo = dict(zip(names, [v_e_norm, v_e_w_in, v_e_conv_w, v_e_a_log, v_e_dt_bias, v_e_o_norm, v_e_ln_g, v_e_ln_b, v_e_w_s, v_e_b_s, v_e_w_out, v_o_norm, v_o_pw1, v_o_pw1_b, v_o_dw, v_o_dw_b, v_o_ln_g, v_o_ln_b, v_o_pw2, v_o_pw2_b, v_f_norm, v_f_w1, v_f_w2, v_final_norm]))

    T, D = x.shape[1], x.shape[2]
    H = e_a_log.shape[-1]
    dv = e_o_norm.shape[-1]
    dk = dv
    G = e_w_s.shape[1]
    AQK, AV, BW = H * dk, H * dv, e_ln_g.shape[-1]
    AQKV = 2 * AQK + AV
    in_cols = AQKV + AV + 2 * H + 2 * BW
    KA = e_conv_w.shape[1]
    KC = o_dw.shape[1]
    L = f_norm.shape[0]
    dev = 4 * lax.axis_index("x") + 2 * lax.axis_index("y") + lax.axis_index("c")
    x2d = x.reshape(T, D)
    tgt = loss_target.reshape(T, D)

    dev_sp = dev.astype(jnp.int32).reshape(1)
    where = jnp.stack([lax.axis_index("c"), 2 * lax.axis_index("x") + lax.axis_index("y")]).astype(jnp.int32)
    row = lambda a: a.reshape(1, -1).astype(F32)
    en_row = row(e_norm)

    def gather_begin(tag, shards, after):
        na = len(shards)
        lands = [lax.empty((N_DEV,) + s.shape, s.dtype) for s in shards]
        ss, rs, bufs, tok = split_start(f"gather{tag}_ici_start", shards + lands, plan_gather_ici(na), 4 * na, after)
        return (na, ss, rs, bufs), tok

    def gather_pass(tag, h, after):
        na, ss, rs, bufs = h
        bufs, tok = split_wait(f"gather{tag}_ici_wait", ss, rs, bufs, plan_gather_ici(na), after)
        ss, rs, lands, tok = split_start(f"gather{tag}_pass_start", bufs[na:], plan_gather_pass(na), 3 * na, tok)
        return (na, ss, rs, bufs[:na], lands), tok

    def gather_end(tag, h, after):
        na, ss, rs, shards, lands = h
        lands, _ = split_wait(f"gather{tag}_pass_wait", ss, rs, lands, plan_gather_pass(na), after)
        return [place_own(f"gather{tag}_own{a}", lands[a], shards[a], dev_sp) for a in range(na)]

    small_sharded = ['e_conv_w', 'o_norm', 'o_pw1_b', 'o_dw', 'o_dw_b', 'o_ln_g', 'o_ln_b', 'o_pw2_b']
    sm = all_gather("gather_small", [pack([W[n][0]]) for n in small_sharded])

    bfw = lambda w: w.astype(BF)
    hA0, tok = gather_begin("0", [bfw(jnp.swapaxes(e_w_in[0], 0, 1))], sm[0])
    hA1, tok = gather_begin("1", [bfw(e_w_out[0]), bfw(f_w1[0]), bfw(f_w2[0])], tok)
    hA2, tok = gather_begin("2", [bfw(o_pw1[0]), bfw(o_pw2[0]), bfw(f_w1[1]), bfw(f_w2[1])], tok)
    h0 = rowcall("rms_e", rms_tile, [x2d], [en_row], [(D, BF)], after=tok)

    full = {}
    for n, g in zip(small_sharded, sm):
        shp = W[n][0].shape
        blocks = [unpack(g[d], [shp])[0] for d in range(N_DEV)]
        full[n] = jnp.concatenate(blocks, axis=-1)
    conv_w = full['e_conv_w']
    on_row, pw1_b_row = row(full['o_norm']), row(full['o_pw1_b'])
    dw_w, dw_b_row = full['o_dw'], row(full['o_dw_b'])
    oln_g_row, oln_b_row, pw2_b_row = row(full['o_ln_g']), row(full['o_ln_b']), row(full['o_pw2_b'])
    small_names = ['e_norm', 'e_conv_w', 'e_a_log', 'e_dt_bias', 'e_o_norm', 'e_ln_g', 'e_ln_b', 'e_w_s', 'e_b_s',
                   'o_norm', 'o_pw1_b', 'o_dw', 'o_dw_b', 'o_ln_g', 'o_ln_b', 'o_pw2_b', 'f_norm', 'final_norm']
    packed_wmv = [pack([A[n] for n in small_names], 256) for A in (W, Mo, Vo)]

    hB0, tok = gather_pass("0", hA0, [h0, conv_w, dw_w, pw1_b_row] + packed_wmv)
    (g_win,) = gather_end("0", hB0, tok)
    win_t = g_win.reshape(in_cols, D)
    wt_qkv, wt_z = win_t[:AQKV], win_t[AQKV:AQKV + AV]
    wt_ba = jnp.pad(win_t[AQKV + AV:AQKV + AV + 2 * H], ((0, LANE - 2 * H), (0, 0)))
    wt_uv = win_t[AQKV + AV + 2 * H:]

    alog_row = jnp.pad(row(e_a_log), ((0, 0), (H, LANE - 2 * H)))
    dtb_row = jnp.pad(row(e_dt_bias), ((0, 0), (H, LANE - 2 * H)))
    eon_row = row(e_o_norm)
    eln_g_row, eln_b_row = row(e_ln_g), row(e_ln_b)
    w_s = e_w_s[0]
    bs_t = e_b_s[0].T
    fn_rows = [row(f_norm[l]) for l in range(L)]
    fin_row = row(final_norm)

    qkv_raw = matmul("proj_qkv", h0, wt_qkv, "nt")
    z_gate = matmul("proj_z", h0, wt_z, "nt")
    ba = matmul("proj_ba", h0, wt_ba, "nt")
    uv = matmul("proj_uv", h0, wt_uv, "nt")

    cwa = min(512, AQKV)
    qkv_post = make_qkv_post(dk, cwa, 2 * AQK // cwa)
    ident = lambda t: t
    qkv = conv_fwd("qkv_conv", qkv_raw, conv_w, [], ident, qkv_post, [0], 1, KA, cw=cwa)
    bgc_fn = make_bgc(H)
    bgc = rowcall("bgc", bgc_fn, [ba], [alog_row, dtb_row], [(LANE, F32)])
    o_dn, xinv, states = deltanet_fwd(qkv, bgc, H)
    hB1, tok = gather_pass("1", hA1, o_dn)

    def mix_tile(o, z, uv_, o_norm_, ln_g, ln_b, w_s_, bs_t_):
        return jnp.concatenate([mixa_post_tile(o, z, o_norm_), mixb_tile(uv_, ln_g, ln_b, w_s_, bs_t_)], axis=-1)
    mix_consts = [eon_row, eln_g_row, eln_b_row, w_s, bs_t]
    mix = rowcall("mix", mix_tile, [o_dn, z_gate, uv], mix_consts, [(AV + BW, BF)], after=tok)
    g_wout, g_w1_0, g_w2_0 = gather_end("1", hB1, mix)
    wout = g_wout.reshape(-1, D)
    w1 = [jnp.moveaxis(g_w1_0, 0, 1).reshape(D, -1), None]
    w2 = [g_w2_0.reshape(-1, D), None]
    add_epi = lambda acc, r: (acc + r,)
    x1 = matmul("out_proj", mix, wout, "nn", epi=add_epi, extras=[x2d])

    def relu2_epi(acc):
        r = jnp.maximum(acc, 0.0)
        return r * r, r

    hf0 = rms_fwd("rms_f0", x1, fn_rows[0])
    a2_0, ar_0 = matmul("ffn_up0", hf0, w1[0], "nn", epi=relu2_epi, out_dtypes=(BF, BF))
    hB2, tok = gather_pass("2", hA2, a2_0)
    x2 = matmul("ffn_down0", a2_0, w2[0], "nn", epi=add_epi, extras=[x1], after=tok)
    ffn0 = (hf0, a2_0, ar_0)
    g_pw1, g_pw2, g_w1_1, g_w2_1 = gather_end("2", hB2, x2)
    pw1 = jnp.moveaxis(g_pw1, 0, 1).reshape(D, 2 * D)
    pw2 = g_pw2.reshape(D, D)
    w1[1] = jnp.moveaxis(g_w1_1, 0, 1).reshape(D, -1)
    w2[1] = g_w2_1.reshape(-1, D)

    h1 = rms_fwd("rms_o", x2, on_row)
    bias_epi = lambda acc, b: (acc + b,)
    zc = matmul("pw1", h1, pw1, "nn", epi=bias_epi, extras=[pw1_b_row])
    cwc = min(512, D)
    ncb = D // cwc
    cconv = conv_fwd("dw_conv", zc, dw_w, [dw_b_row], glu_pre, bias_post, [0, ncb], 1, KC, hb=32, cw=cwc)
    ln_silu = lambda c, g, b: jax.nn.silu(ln_tile(c, g, b))
    s_act = rowcall("ln_silu", ln_silu, [cconv], [oln_g_row, oln_b_row], [(D, BF)])
    x3 = matmul("pw2", s_act, pw2, "nn", epi=lambda acc, r, b: (r + (acc + b),), extras=[x2, pw2_b_row])
    hf1 = rms_fwd("rms_f1", x3, fn_rows[1])
    a2_1, ar_1 = matmul("ffn_up1", hf1, w1[1], "nn", epi=relu2_epi, out_dtypes=(BF, BF))
    x4 = matmul("ffn_down1", a2_1, w2[1], "nn", epi=add_epi, extras=[x3])
    ffn1 = (hf1, a2_1, ar_1)

    def loss_bwd_tile(x_, t_, g_):
        l, vjp = jax.vjp(lambda a, b: loss_tile(a, b, t_), x_, g_)
        dx, dg = vjp(jnp.ones_like(l))
        return dx, dx, l, dg
    dx4, dx4_b, loss_part, d_final = rowcall("loss_head", loss_bwd_tile, [x4, tgt], [fin_row],
                                             [(D, F32), (D, BF)], [(1, 1), (1, D)])
    loss = lax.psum(loss_part[0, 0], ("x", "y", "c"))

    def reduce_begin(tag, grads, after):
        na = len(grads)
        lands = [lax.empty((4,) + g.shape[1:], g.dtype) for g in grads]
        ss, rs, bufs, tok = split_start(f"reduce{tag}_d2d_start", grads + lands, plan_reduce_d2d(na), 4 * na, after)
        return (na, ss, rs, bufs), tok

    def reduce_mid(tag, h, after):
        na, ss, rs, bufs = h
        bufs, tok = split_wait(f"reduce{tag}_d2d_wait", ss, rs, bufs, plan_reduce_d2d(na), after)
        parts = []
        for a, (g, rc) in enumerate(zip(bufs[:na], bufs[na:])):
            r, c = g.shape[1], g.shape[2]
            if r % SUBLANE:
                parts.append(slabcall(f"chip_sum{tag}_{a}", lambda p, q: (p + q,),
                                      [(g, lambda i, s: 2 * i + s[0]), (rc, lambda i, s: i)], [BF], where, 4))
                continue
            mine = lambda i, n, s: (2 * (i // (n // 4)) + s[0]) * (n // 4) + i % (n // 4)
            parts.append(rowcall(f"chip_sum{tag}_{a}", lambda p, q: (p + q,),
                                 [(g.reshape(N_DEV * r, c), mine), rc.reshape(4 * r, c)], [],
                                 [(c, BF)], tr=_tile(r, 512), sp=where, R=4 * r).reshape(4, r, c))
        lands = [lax.empty((3,) + p.shape[1:], p.dtype) for p in parts]
        ss, rs, bufs, tok = split_start(f"reduce{tag}_ici_start", parts + lands, plan_reduce_ici(na), 3 * na, tok)
        return (na, ss, rs, bufs), tok

    res = {}

    def reduce_end(tag, h, after, targets):
        na, ss, rs, bufs = h
        bufs, _ = split_wait(f"reduce{tag}_ici_wait", ss, rs, bufs, plan_reduce_ici(na), after)
        for a, (part, fin, (n, l)) in enumerate(zip(bufs[:na], bufs[na:], targets)):
            def f(p0, p1, p2, p3, w_, m_, v_):
                g = ((p0.astype(F32) + p1.astype(F32)) + p2.astype(F32)) + p3.astype(F32)
                return (g,) + adamw_tile(w_, g, m_, v_)
            r, C = fin.shape[-2], fin.shape[-1]
            if W[n].shape[1:] == (C, r):
                t = lambda arr: jnp.swapaxes(arr[l], 0, 1)
                outs = slabcall(f"adam{tag}_{a}", f, [(part, lambda i, s: s[1]), (fin, 0), (fin, 1), (fin, 2),
                                                      (t(W[n]), None), (t(Mo[n]), None), (t(Vo[n]), None)],
                                [F32] * 4, where)
                res[n] = tuple(jnp.swapaxes(o, 0, 1)[None] for o in outs)
                continue
            own = lambda i, n_, s: s[1] * n_ + i
            res[n] = rowcall(f"adam{tag}_{a}", f, [(part.reshape(4 * r, C), own), (fin, 0), (fin, 1), (fin, 2),
                                                   (W[n], l), (Mo[n], l), (Vo[n], l)], [],
                             [(C, F32)] * 4, tr=256, sp=where, R=r,
                             out_lead=(W[n].shape[0], l), into=res.get(n))

    def tie(small, tok):
        return small + tok[0:1, 0:1]

    dscale_epi = lambda acc, r: (acc * (2.0 * r.astype(F32)),)
    d_fnorm = [None] * L

    dpre1 = matmul("ffn_down_dx1", dx4_b, w2[1], "nt", epi=dscale_epi, extras=[ar_1], out_dtypes=(BF,))
    dw2_1 = matmul("ffn_down_dw1", a2_1, dx4_b, "tn")
    dw1_1 = matmul("ffn_up_dw1", hf1, dpre1, "tn", colshard=True)
    hD1, tok = reduce_begin("1", [dw1_1, dw2_1.reshape(N_DEV, -1, D)], dpre1)
    dhf1 = matmul("ffn_up_dx1", dpre1, w1[1], "nt", after=tok, tk=4096)
    dx3, dx3_b, d_fnorm[1] = rms_bwd("rms_f_bwd1", x3, fn_rows[1], dhf1, dx4)

    ds_act = matmul("pw2_dx", dx3_b, pw2, "nt")
    hI1, tok = reduce_mid("1", hD1, ds_act)
    d_pw2 = matmul("pw2_dw", s_act, dx3_b, "tn", after=tok)
    ln_silu_bwd = _vjp_rows(ln_silu, 1, 1)

    def ln_silu_bwd_tile(c_, ds_, dx3_, g_, b_):
        dc, dg, db = ln_silu_bwd(c_, ds_, g_, b_)
        return dc, dg, db, jnp.sum(dx3_, axis=0, keepdims=True)
    dcconv, d_oln_g, d_oln_b, d_pw2_b = rowcall(
        "ln_silu_bwd", ln_silu_bwd_tile, [cconv, ds_act, dx3], [oln_g_row, oln_b_row],
        [(D, F32)], [(1, D), (1, D), (1, D)])
    (dza, dzb), (sza, szb), d_dw, (d_dw_b,) = conv_bwd(
        "dw_conv_bwd", zc, dw_w, [tie(dw_b_row, tok)], [dcconv], glu_pre, bias_post, [0, ncb], KC, hb=32, cw=cwc, tr=128,
        recompute=False)
    dzc = jnp.concatenate([dza, dzb], axis=-1)
    d_pw1_b = jnp.concatenate([sza, szb], axis=-1)
    d_pw1 = matmul("pw1_dw", h1, dzc, "tn", colshard=True)
    dh1 = matmul("pw1_dx", dzc, pw1, "nt")
    dx2, dx2_b, d_onorm = rms_bwd("rms_o_bwd", x2, on_row, dh1, dx3)
    reduce_end("1", hI1, dx2, [('f_w1', 1), ('f_w2', 1)])

    hD2, tok = reduce_begin("2", [d_pw1, d_pw2.reshape(N_DEV, -1, D)], dx2)
    dpre0 = matmul("ffn_down_dx0", dx2_b, w2[0], "nt", epi=dscale_epi, extras=[ar_0], out_dtypes=(BF,), after=tok)
    dw2_0 = matmul("ffn_down_dw0", a2_0, dx2_b, "tn")
    hI2, tok = reduce_mid("2", hD2, dw2_0)
    dw1_0 = matmul("ffn_up_dw0", hf0, dpre0, "tn", colshard=True, after=tok)
    dhf0 = matmul("ffn_up_dx0", dpre0, w1[0], "nt", tk=4096)
    dx1, dx1_b, d_fnorm[0] = rms_bwd("rms_f_bwd0", x1, fn_rows[0], dhf0, dx2)
    reduce_end("2", hI2, dx1, [('o_pw1', 0), ('o_pw2', 0)])

    dmix = matmul("out_proj_dx", dx1_b, wout, "nt")
    d_wout = matmul("out_proj_dw", mix, dx1_b, "tn")
    hD3, tok = reduce_begin("3", [dw1_0, dw2_0.reshape(N_DEV, -1, D), d_wout.reshape(N_DEV, -1, D)], dmix)
    do_dn, dz_gate, duv, d_eon, d_eln_g, d_eln_b, d_ws, d_bs_t = rowcall(
        "mix_bwd", _vjp_rows(mix_tile, 3, 1), [o_dn, z_gate, uv, dmix], mix_consts,
        [(AV, F32), (AV, BF), (2 * BW, BF)], [(1, dv), (1, BW), (1, BW), w_s.shape, bs_t.shape], after=tok)
    dqkv, dbgc = deltanet_bwd(qkv, bgc, xinv, states, do_dn, H)
    hI3, tok = reduce_mid("3", hD3, dbgc)
    bgc_bwd = _vjp_rows(bgc_fn, 1, 1)
    dba, d_alog_row, d_dtb_row = rowcall(
        "bgc_bwd", bgc_bwd, [ba, dbgc], [alog_row, dtb_row], [(LANE, BF)], [(1, LANE), (1, LANE)])
    (dqkv_raw,), _, d_conv_w, _ = conv_bwd(
        "qkv_conv_bwd", qkv_raw, tie(conv_w, tok), [], [dqkv], ident, qkv_post, [0], KA, cw=cwa)

    dw_qkv = matmul("proj_qkv_dw", dqkv_raw, h0, "tn")
    dw_z = matmul("proj_z_dw", dz_gate, h0, "tn")
    dw_ba = matmul("proj_ba_dw", dba, h0, "tn")
    dw_uv = matmul("proj_uv_dw", duv, h0, "tn")
    d_win_t = jnp.concatenate([dw_qkv, dw_z, dw_ba[:2 * H], dw_uv], axis=0)
    G_win = d_win_t.reshape(N_DEV, in_cols // N_DEV, D)
    hD4, tok = reduce_begin("4", [G_win], dw_uv)
    reduce_end("3", hI3, tok, [('f_w1', 0), ('f_w2', 0), ('e_w_out', 0)])
    hI4, tok = reduce_mid("4", hD4, [res[n][0] for n in ('f_w1', 'f_w2', 'o_pw1', 'e_w_out')])
    dh0 = matmul_sum("proj_dx", [(dqkv_raw, wt_qkv), (dz_gate, wt_z), (dba, wt_ba), (duv, wt_uv)], after=tok)
    grad_x, _, d_enorm = rms_bwd("rms_e_bwd", x2d, en_row, dh0, dx1, after=tok)

    d_alog = d_alog_row[:, H:2 * H]
    d_dtb = d_dtb_row[:, H:2 * H]
    small_grads = [d_enorm, d_conv_w, d_alog, d_dtb, d_eon, d_eln_g, d_eln_b, d_ws, d_bs_t.T,
                   d_onorm, d_pw1_b, d_dw, d_dw_b, d_oln_g, d_oln_b, d_pw2_b,
                   jnp.concatenate(d_fnorm, axis=0), d_final]
    full_shapes = [g.shape for g in small_grads]
    gs_all = all_gather("gather_small_grads", [pack(small_grads, 256)])[0]

    def sum8(*ps):
        s = ps[0]
        for p in ps[1:]:
            s = s + p
        return (s,)
    gs_sum = rowcall("small_sum", sum8, [(gs_all, d) for d in range(N_DEV)], [], [(LANE, F32)])
    reduce_end("4", hI4, gs_sum, [('e_w_in', 0)])
    g_full = dict(zip(small_names, unpack(gs_sum, full_shapes)))
    g_loc = {}
    for n in small_names:
        g = g_full[n]
        if n in small_sharded:
            per = g.shape[-1] // N_DEV
            g = lax.dynamic_slice_in_dim(g, dev * per, per, axis=-1)
        g_loc[n] = g.reshape(W[n].shape)
    packs = [packed_wmv[0], pack([g_loc[n] for n in small_names], 256), packed_wmv[1], packed_wmv[2]]
    d_s, m_s, v_s = rowcall("adam_small", adamw_tile, packs, [], [(LANE, F32)] * 3)
    shapes = [W[n].shape for n in small_names]
    for n, d_, m_, v_ in zip(small_names, unpack(d_s, shapes), unpack(m_s, shapes), unpack(v_s, shapes)):
        res[n] = (g_loc[n], d_, m_, v_)

    grads = [res[n][0] for n in names]
    deltas = [res[n][1] for n in names]
    new_m = [res[n][2] for n in names]
    new_v = [res[n][3] for n in names]
    return (loss, grad_x.reshape(x.shape), *grads, *deltas, *new_m, *new_v)
```

```python
import functools
import math

import jax
import jax.numpy as jnp
import numpy as np
from jax import lax
from jax.experimental import pallas as pl
from jax.experimental.pallas import tpu as pltpu

F32 = jnp.float32
BF = jnp.bfloat16
EPS = 1e-6
CHUNK = 64
B_BLOCK = 128
LANE = 128
SUBLANE = 8
N_DEV = 8
VMEM_LIMIT = 56 * 1024 * 1024

ADAM_LR = 0.001
ADAM_B1 = 0.9
ADAM_B2 = 0.999
ADAM_EPS = 1e-08
ADAM_WD = 0.01
ADAM_STEP = 10

MESH = pl.DeviceIdType.MESH
ANY = pl.BlockSpec(memory_space=pl.ANY)


def _tile(n, pref, mult=SUBLANE):
    if n <= pref:
        return n
    t = (pref // mult) * mult
    while t >= mult:
        if n % t == 0:
            return t
        t -= mult
    return n


THREE_PASS = 3


def _dg(a, b, ca, cb, hi):
    nb = a.ndim - 2
    batch = tuple(range(nb))
    dims = (((ca + nb,), (cb + nb,)), (batch, batch))
    if hi == THREE_PASS:
        a, b = a.astype(F32), b.astype(F32)
        ah, bh = a.astype(BF), b.astype(BF)
        al, bl = (a - ah.astype(F32)).astype(BF), (b - bh.astype(F32)).astype(BF)
        dot = lambda p, q: lax.dot_general(p, q, dims, preferred_element_type=F32)
        return dot(ah, bh) + (dot(ah, bl) + dot(al, bh))
    if hi:
        return lax.dot_general(a.astype(F32), b.astype(F32), dims,
                               precision=lax.Precision.HIGHEST, preferred_element_type=F32)
    return lax.dot_general(a.astype(BF), b.astype(BF), dims, preferred_element_type=F32)


@functools.partial(jax.custom_vjp, nondiff_argnums=(2, 3, 4))
def mm(a, b, ca, cb, hi=False):
    return _dg(a, b, ca, cb, hi)


def _mm_fwd(a, b, ca, cb, hi):
    return _dg(a, b, ca, cb, hi), (a, b)


def _mm_bwd(ca, cb, hi, res, g):
    a, b = res
    if ca == 1:
        da = mm(g, b, 1, 1 - cb, hi)
    else:
        da = mm(b, g, 1 - cb, 1, hi)
    if cb == 0:
        db = mm(a, g, 1 - ca, 0, hi)
    else:
        db = mm(g, a, 0, 1 - ca, hi)
    return da.astype(a.dtype), db.astype(b.dtype)


mm.defvjp(_mm_fwd, _mm_bwd)


def matmul(name, a, b, mode, epi=None, extras=(), out_dtypes=(F32,), colshard=False,
           tm=None, tn=1024, tk=2048, after=None):
    afters = [] if after is None else [after]
    if mode == "nn":
        (M, K), (K2, N) = a.shape, b.shape
    elif mode == "nt":
        (M, K), (N, K2) = a.shape, b.shape
    else:
        (K, M), (K2, N) = a.shape, b.shape
    assert K == K2, (name, a.shape, b.shape, mode)
    if tm is None:
        tm = 1024
    tm = _tile(M, tm)
    tn = N // N_DEV if colshard else _tile(N, tn, LANE)
    tk = _tile(K, tk, LANE)
    nk = K // tk
    grid = (M // tm, N // tn, nk)
    if mode == "nn":
        a_spec = pl.BlockSpec((tm, tk), lambda i, j, k: (i, k))
        b_spec = pl.BlockSpec((tk, tn), lambda i, j, k: (k, j))
        ca, cb = 1, 0
    elif mode == "nt":
        a_spec = pl.BlockSpec((tm, tk), lambda i, j, k: (i, k))
        b_spec = pl.BlockSpec((tn, tk), lambda i, j, k: (j, k))
        ca, cb = 1, 1
    else:
        a_spec = pl.BlockSpec((tk, tm), lambda i, j, k: (k, i))
        b_spec = pl.BlockSpec((tk, tn), lambda i, j, k: (k, j))
        ca, cb = 0, 0
    ex_specs = []
    for e in extras:
        if e.shape[0] == 1:
            ex_specs.append(pl.BlockSpec((1, tn), lambda i, j, k: (0, j)))
        else:
            assert e.shape == (M, N), (name, e.shape)
            ex_specs.append(pl.BlockSpec((tm, tn), lambda i, j, k: (i, j)))
    if colshard:
        out_shape = [jax.ShapeDtypeStruct((N_DEV, M, tn), dt) for dt in out_dtypes]
        out_specs = [pl.BlockSpec((None, tm, tn), lambda i, j, k: (j, i, 0)) for _ in out_dtypes]
    else:
        out_shape = [jax.ShapeDtypeStruct((M, N), dt) for dt in out_dtypes]
        out_specs = [pl.BlockSpec((tm, tn), lambda i, j, k: (i, j)) for _ in out_dtypes]
    n_ex, n_out = len(extras), len(out_dtypes)

    def body(*refs):
        a_ref, b_ref = refs[0], refs[1]
        ex_refs = refs[2:2 + n_ex]
        first_out = 2 + n_ex + len(afters)
        o_refs = refs[first_out:first_out + n_out]
        part = _dg(a_ref[...], b_ref[...], ca, cb, False)

        def finish(acc):
            res = (acc,) if epi is None else epi(acc, *[r[...] for r in ex_refs])
            for o_ref, r in zip(o_refs, res):
                o_ref[...] = r.astype(o_ref.dtype)

        if nk == 1:
            finish(part)
            return
        acc_ref = refs[-1]
        k = pl.program_id(2)

        @pl.when(k == 0)
        def _():
            acc_ref[...] = part

        @pl.when(k > 0)
        def _():
            acc_ref[...] += part

        @pl.when(k == nk - 1)
        def _():
            finish(acc_ref[...])

    outs = pl.pallas_call(
        body, name=name, grid=grid,
        in_specs=[a_spec, b_spec] + ex_specs + [ANY] * len(afters),
        out_specs=out_specs, out_shape=out_shape,
        scratch_shapes=[pltpu.VMEM((tm, tn), F32)] if nk > 1 else [],
        compiler_params=pltpu.CompilerParams(
            dimension_semantics=("parallel", "parallel", "arbitrary"),
            vmem_limit_bytes=VMEM_LIMIT),
    )(a, b, *extras, *afters)
    return outs[0] if n_out == 1 else tuple(outs)


def matmul_sum(name, pairs, tm=512, tn=1024, after=None):
    M, N = pairs[0][0].shape[0], pairs[0][1].shape[1]
    tm, tn = _tile(M, tm), _tile(N, tn, LANE)
    afters = [] if after is None else [after]
    in_specs, operands = [], []
    for a, b in pairs:
        assert a.shape == (M, b.shape[0]) and b.shape[1] == N, (name, a.shape, b.shape)
        in_specs += [pl.BlockSpec((tm, a.shape[1]), lambda i, j: (i, 0)),
                     pl.BlockSpec((b.shape[0], tn), lambda i, j: (0, j))]
        operands += [a, b]
    n = len(pairs)

    def body(*refs):
        acc = _dg(refs[0][...], refs[1][...], 1, 0, False)
        for p in range(1, n):
            acc = acc + _dg(refs[2 * p][...], refs[2 * p + 1][...], 1, 0, False)
        refs[-1][...] = acc

    return pl.pallas_call(
        body, name=name, grid=(M // tm, N // tn),
        in_specs=in_specs + [ANY] * len(afters),
        out_specs=pl.BlockSpec((tm, tn), lambda i, j: (i, j)),
        out_shape=jax.ShapeDtypeStruct((M, N), F32),
        compiler_params=pltpu.CompilerParams(
            dimension_semantics=("parallel", "parallel"), vmem_limit_bytes=VMEM_LIMIT),
    )(*operands, *afters)


def rowcall(name, fn, rows, consts, out_rows, out_accs=(), tr=256, sp=None, R=None, after=None,
            out_lead=None, into=None):
    afters = ([] if after is None else [after]) + ([] if into is None else list(into))
    rows = [r if isinstance(r, tuple) else (r, None) for r in rows]
    R = rows[0][0].shape[-2] if R is None else R
    tr = _tile(R, tr)
    n = R // tr
    in_specs = []
    for arr, lead in rows:
        C = arr.shape[-1]
        if lead is None:
            assert arr.shape[-2] == R, (name, arr.shape, R)
            in_specs.append(pl.BlockSpec((tr, C), lambda i, *s: (i, 0)))
        elif callable(lead):
            in_specs.append(pl.BlockSpec((tr, C), lambda i, *s, lead=lead: (lead(i, n, *s), 0)))
        else:
            assert arr.shape[-2] == R, (name, arr.shape, R)
            in_specs.append(pl.BlockSpec((None, tr, C), lambda i, *s, lead=lead: (lead, i, 0)))
    for c in consts:
        in_specs.append(pl.BlockSpec(c.shape, lambda i, *s, nd=c.ndim: (0,) * nd))
    if out_lead is None:
        out_shape = [jax.ShapeDtypeStruct((R, C), dt) for C, dt in out_rows]
        out_specs = [pl.BlockSpec((tr, C), lambda i, *s: (i, 0)) for C, _ in out_rows]
    else:
        n_slab, slab = out_lead
        out_shape = [jax.ShapeDtypeStruct((n_slab, R, C), dt) for C, dt in out_rows]
        out_specs = [pl.BlockSpec((None, tr, C), lambda i, *s: (slab, i, 0)) for C, _ in out_rows]
    for shp in out_accs:
        out_shape.append(jax.ShapeDtypeStruct(shp, F32))
        out_specs.append(pl.BlockSpec(shp, lambda i, *s, nd=len(shp): (0,) * nd))
    n_in, n_row, n_acc = len(rows) + len(consts), len(out_rows), len(out_accs)
    n_sp = 0 if sp is None else 1

    def body(*refs):
        refs = refs[n_sp:]
        ins = [r[...] for r in refs[:n_in]]
        res = fn(*ins)
        if not isinstance(res, (tuple, list)):
            res = (res,)
        o_refs = refs[n_in + len(afters):]
        for o_ref, r in zip(o_refs[:n_row], res[:n_row]):
            o_ref[...] = r.astype(o_ref.dtype)
        if n_acc:
            first = pl.program_id(0) == 0
            for o_ref, r in zip(o_refs[n_row:], res[n_row:]):
                r = r.astype(F32).reshape(o_ref.shape)

                @pl.when(first)
                def _(o_ref=o_ref, r=r):
                    o_ref[...] = r

                @pl.when(jnp.logical_not(first))
                def _(o_ref=o_ref, r=r):
                    o_ref[...] += r

    params = pltpu.CompilerParams(dimension_semantics=("arbitrary",), vmem_limit_bytes=VMEM_LIMIT)
    operands = [a for a, _ in rows] + list(consts) + afters
    in_specs = in_specs + [ANY] * len(afters)
    aliases = {} if into is None else {n_sp + len(operands) - len(into) + k: k for k in range(len(into))}
    if sp is None:
        outs = pl.pallas_call(body, name=name, grid=(n,), in_specs=in_specs, out_specs=out_specs,
                              out_shape=out_shape, input_output_aliases=aliases,
                              compiler_params=params)(*operands)
    else:
        outs = pl.pallas_call(
            body, name=name, out_shape=out_shape, compiler_params=params, input_output_aliases=aliases,
            grid_spec=pltpu.PrefetchScalarGridSpec(
                num_scalar_prefetch=1, grid=(n,), in_specs=in_specs, out_specs=out_specs),
        )(sp, *operands)
    return outs[0] if len(outs) == 1 else tuple(outs)


def slabcall(name, fn, ins, out_dtypes, sp, n_out_slabs=None, cw=512):
    R, C = ins[0][0].shape[-2:]
    cw = _tile(C, cw, LANE)
    in_specs = []
    for arr, slab in ins:
        assert arr.shape[-2:] == (R, C), (name, arr.shape)
        if slab is None:
            in_specs.append(pl.BlockSpec((R, cw), lambda i, j, s: (0, j)))
        elif callable(slab):
            in_specs.append(pl.BlockSpec((None, R, cw), lambda i, j, s, slab=slab: (slab(i, s), 0, j)))
        else:
            in_specs.append(pl.BlockSpec((None, R, cw), lambda i, j, s, slab=slab: (slab, 0, j)))
    if n_out_slabs is None:
        out_shape = [jax.ShapeDtypeStruct((R, C), dt) for dt in out_dtypes]
        out_specs = [pl.BlockSpec((R, cw), lambda i, j, s: (0, j)) for _ in out_dtypes]
    else:
        out_shape = [jax.ShapeDtypeStruct((n_out_slabs, R, C), dt) for dt in out_dtypes]
        out_specs = [pl.BlockSpec((None, R, cw), lambda i, j, s: (i, 0, j)) for _ in out_dtypes]
    n_in = len(ins)

    def body(sp_ref, *refs):
        res = fn(*[r[...] for r in refs[:n_in]])
        for o_ref, r in zip(refs[n_in:], res):
            o_ref[...] = r.astype(o_ref.dtype)

    outs = pl.pallas_call(
        body, name=name, out_shape=out_shape,
        grid_spec=pltpu.PrefetchScalarGridSpec(
            num_scalar_prefetch=1, grid=(n_out_slabs or 1, C // cw), in_specs=in_specs, out_specs=out_specs),
        compiler_params=pltpu.CompilerParams(
            dimension_semantics=("arbitrary", "arbitrary"), vmem_limit_bytes=VMEM_LIMIT),
    )(sp, *[a for a, _ in ins])
    return outs[0] if len(outs) == 1 else tuple(outs)


def rms_tile(x, g):
    x = x.astype(F32)
    return x * lax.rsqrt(jnp.mean(x * x, axis=-1, keepdims=True) + EPS) * g


def gelu(x):
    return 0.5 * x * (1.0 + lax.erf(x * (1.0 / math.sqrt(2.0))))


def ln_tile(x, g, b):
    mu = jnp.mean(x, axis=-1, keepdims=True)
    xc = x - mu
    return xc * lax.rsqrt(jnp.mean(xc * xc, axis=-1, keepdims=True) + EPS) * g + b


def lane_groups(fn, width, *arrs):
    n = arrs[0].shape[-1] // width
    outs = [fn(*[a[:, i * width:(i + 1) * width] for a in arrs]) for i in range(n)]
    return jnp.concatenate(outs, axis=-1)


def mixa_post_tile(o, z, o_norm):
    dv = o_norm.shape[-1]
    on = lane_groups(lambda t: rms_tile(t, o_norm), dv, o)
    return on * jax.nn.silu(z)


def mixb_tile(uv, ln_g, ln_b, w_s, bs_t):
    G = w_s.shape[0]
    gw = ln_g.shape[-1]
    dg = gw // G
    tr = uv.shape[0]
    u = gelu(uv[:, :gw])
    vg = gelu(uv[:, gw:])
    ii = lax.broadcasted_iota(jnp.int32, (B_BLOCK, B_BLOCK), 0)
    jj = lax.broadcasted_iota(jnp.int32, (B_BLOCK, B_BLOCK), 1)
    mask = (jj // CHUNK) <= (ii // CHUNK)
    cols = []
    for g in range(G):
        sl = slice(g * dg, (g + 1) * dg)
        vn = ln_tile(vg[:, sl], ln_g[:, sl], ln_b[:, sl])
        wm = jnp.where(mask, w_s[g], 0.0)
        blocks = []
        for m in range(tr // B_BLOCK):
            blk = vn[m * B_BLOCK:(m + 1) * B_BLOCK, :]
            blocks.append(mm(wm, blk, 1, 0) + bs_t[:, g:g + 1])
        mixed = blocks[0] if len(blocks) == 1 else jnp.concatenate(blocks, axis=0)
        cols.append(u[:, sl] * mixed)
    return jnp.concatenate(cols, axis=-1)


def bgc_tile(ba, alog_row, dtb_row):
    tr = ba.shape[0]
    beta = jax.nn.sigmoid(ba)
    g = -jnp.exp(alog_row) * jax.nn.softplus(ba + dtb_row)
    ii = lax.broadcasted_iota(jnp.int32, (tr, tr), 0)
    jj = lax.broadcasted_iota(jnp.int32, (tr, tr), 1)
    tri = jnp.where((ii // CHUNK == jj // CHUNK) & (jj <= ii), 1.0, 0.0).astype(F32)
    gc = mm(tri, g, 1, 0, True)
    return beta, gc


def make_bgc(H):
    def f(ba, alog_row, dtb_row):
        beta, gc = bgc_tile(ba, alog_row, dtb_row)
        lane = lax.broadcasted_iota(jnp.int32, ba.shape, 1)
        return jnp.where(lane < H, beta, jnp.where(lane < 2 * H, gc, 0.0))
    return f


def loss_tile(x, g, target):
    y = rms_tile(x, g)
    err = y - target
    return 0.5 * jnp.sum(jnp.mean(err * err, axis=-1, keepdims=True), axis=0, keepdims=True)


def adamw_tile(w, g, m, v):
    m = ADAM_B1 * m + (1.0 - ADAM_B1) * g
    v = ADAM_B2 * v + (1.0 - ADAM_B2) * (g * g)
    m_hat = m / (1.0 - ADAM_B1 ** ADAM_STEP)
    v_hat = v / (1.0 - ADAM_B2 ** ADAM_STEP)
    delta = -ADAM_LR * (m_hat / (jnp.sqrt(v_hat) + ADAM_EPS) + ADAM_WD * w)
    return delta, m, v


CONV_ROWS = 32


def _shifted_copies(src, sh, rows):
    for b in range(SUBLANE):
        sh[b] = src[pl.ds(b, rows), :]


def _window(sh, off, rows):
    b = off % SUBLANE
    return sh[b, pl.ds(off - b, rows), :]


def _conv_rows(out, sh, w_ref, offsets, rows):
    for r0 in range(0, rows, CONV_ROWS):
        rc = min(CONV_ROWS, rows - r0)
        acc = w_ref[0:1, :] * _window(sh, offsets[0] + r0, rc)
        for k in range(1, len(offsets)):
            acc = acc + w_ref[k:k + 1, :] * _window(sh, offsets[k] + r0, rc)
        out[r0:r0 + rc, :] = acc


def _conv_wgrad(dsrc, d0, sh, offsets, rows):
    dws = []
    for off in offsets:
        acc = None
        for r0 in range(0, rows, CONV_ROWS):
            rc = min(CONV_ROWS, rows - r0)
            prod = dsrc[d0 + r0:d0 + r0 + rc, :] * _window(sh, off + r0, rc)
            for g in range(0, rc, SUBLANE):
                part = prod[g:g + SUBLANE, :]
                acc = part if acc is None else acc + part
        dws.append(jnp.sum(acc, axis=0, keepdims=True))
    return jnp.concatenate(dws, axis=0)


def _conv_specs(T, tr, hb, cw, col_blocks, rev):
    n = T // tr

    def ri(i):
        return (n - 1 - i) if rev else i

    tile_specs, halo_specs = [], []
    for off in col_blocks:
        tile_specs.append(pl.BlockSpec((tr, cw), lambda j, i, off=off: (ri(i), j + off)))
        halo_specs.append(pl.BlockSpec(
            (hb, cw), lambda j, i, off=off: (jnp.maximum(ri(i) * (tr // hb) - 1, 0), j + off)))
    return n, ri, tile_specs, halo_specs


def conv_fwd(name, x, w, consts, pre, post, col_blocks, n_out, K, out_dtype=F32, tr=256, hb=8, cw=512):
    T = x.shape[0]
    C = w.shape[1]
    tr, cw = _tile(T, tr, hb), min(cw, C)
    nb = len(col_blocks)
    n, ri, tile_specs, halo_specs = _conv_specs(T, tr, hb, cw, col_blocks, False)
    w_spec = pl.BlockSpec((K, cw), lambda j, i: (0, j))
    c_specs = [pl.BlockSpec((1, cw), lambda j, i: (0, j)) for _ in consts]

    def body(*refs):
        tiles = [r[...] for r in refs[:nb]]
        halos = [r[...] for r in refs[nb:2 * nb]]
        w_ref = refs[2 * nb]
        cs = [r[...] for r in refs[2 * nb + 1:2 * nb + 1 + len(consts)]]
        o_refs = refs[2 * nb + 1 + len(consts):-3]
        pbuf, shp, cbuf = refs[-3:]
        i = pl.program_id(1)
        pbuf[0:hb, :] = jnp.where(i > 0, pre(*halos), 0.0)
        pbuf[hb:hb + tr, :] = pre(*tiles)
        pbuf[hb + tr:hb + tr + SUBLANE, :] = jnp.zeros((SUBLANE, cw), F32)
        _shifted_copies(pbuf, shp, hb + tr)
        _conv_rows(cbuf, shp, w_ref, [hb - (K - 1) + k for k in range(K)], tr)
        res = post(cbuf[...], pl.program_id(0), *cs)
        for o_ref, r in zip(o_refs, res):
            o_ref[...] = r.astype(o_ref.dtype)

    outs = pl.pallas_call(
        body, name=name, grid=(C // cw, n),
        in_specs=tile_specs + halo_specs + [w_spec] + c_specs,
        out_specs=[pl.BlockSpec((tr, cw), lambda j, i: (i, j)) for _ in range(n_out)],
        out_shape=[jax.ShapeDtypeStruct((T, C), out_dtype) for _ in range(n_out)],
        scratch_shapes=[pltpu.VMEM((hb + tr + SUBLANE, cw), F32), pltpu.VMEM((SUBLANE, hb + tr, cw), F32),
                        pltpu.VMEM((tr, cw), F32)],
        compiler_params=pltpu.CompilerParams(
            dimension_semantics=("parallel", "arbitrary"), vmem_limit_bytes=VMEM_LIMIT),
    )(*([x] * nb), *([x] * nb), w, *consts)
    return outs[0] if n_out == 1 else tuple(outs)


def conv_bwd(name, x, w, consts, grads, pre, post, col_blocks, K, tr=256, hb=8, cw=512, recompute=True):
    T, Cx = x.shape
    C = w.shape[1]
    tr, cw = _tile(T, tr, hb), min(cw, C)
    nb = len(col_blocks)
    n, ri, tile_specs, halo_specs = _conv_specs(T, tr, hb, cw, col_blocks, True)
    w_spec = pl.BlockSpec((K, cw), lambda j, i: (0, j))
    c_specs = [pl.BlockSpec((1, cw), lambda j, i: (0, j)) for _ in consts]
    g_specs = [pl.BlockSpec((tr, cw), lambda j, i: (ri(i), j)) for _ in grads]
    nc, ng = len(consts), len(grads)

    def body(*refs):
        p = 0
        tile_refs = refs[p:p + nb]; p += nb
        halo_refs = refs[p:p + nb]; p += nb
        w_ref = refs[p]; p += 1
        cs = [r[...] for r in refs[p:p + nc]]; p += nc
        gs = [r[...] for r in refs[p:p + ng]]; p += ng
        dx_refs = refs[p:p + nb]; p += nb
        sum_refs = refs[p:p + nb]; p += nb
        dw_ref = refs[p]; p += 1
        dc_refs = refs[p:p + nc]; p += nc
        pbuf, dbuf, ebuf, carry, shp, shd, cbuf = refs[p:p + 7]
        i = pl.program_id(1)
        first = i == 0
        tiles = [r[...] for r in tile_refs]
        halos = [r[...] for r in halo_refs]
        p_tile, vjp_pre = jax.vjp(pre, *tiles)
        pbuf[0:hb, :] = jnp.where(ri(i) > 0, pre(*halos), 0.0)
        pbuf[hb:hb + tr, :] = p_tile
        pbuf[hb + tr:hb + tr + SUBLANE, :] = jnp.zeros((SUBLANE, cw), F32)
        _shifted_copies(pbuf, shp, hb + tr)
        taps = [hb - (K - 1) + k for k in range(K)]
        if recompute:
            _conv_rows(cbuf, shp, w_ref, taps, tr)
            c = cbuf[...]
        else:
            c = jnp.zeros((tr, cw), F32)
        cid = pl.program_id(0)
        _, vjp_post = jax.vjp(lambda c_, *cs_: post(c_, cid, *cs_), c, *cs)
        dres = vjp_post(tuple(g.astype(F32) for g in gs))
        dbuf[0:hb, :] = jnp.zeros((hb, cw), F32)
        dbuf[hb:hb + tr, :] = dres[0]
        dbuf[hb + tr:hb + tr + hb + SUBLANE, :] = jnp.zeros((hb + SUBLANE, cw), F32)
        _shifted_copies(dbuf, shd, hb + tr + hb)
        _conv_rows(ebuf, shd, w_ref, [K - 1 - k for k in range(K)], hb + tr)
        dw = _conv_wgrad(dbuf, hb, shp, taps, tr)

        @pl.when(jnp.logical_not(first))
        def _():
            ebuf[tr:tr + hb, :] += carry[...]

        carry[...] = ebuf[0:hb, :]
        dtiles = vjp_pre(ebuf[hb:hb + tr, :])
        for r, s, d in zip(dx_refs, sum_refs, dtiles):
            r[...] = d.astype(r.dtype)
            ds_ = jnp.sum(d, axis=0, keepdims=True)

            @pl.when(first)
            def _(s=s, ds_=ds_):
                s[...] = ds_

            @pl.when(jnp.logical_not(first))
            def _(s=s, ds_=ds_):
                s[...] += ds_

        accs = [(dw_ref, dw)] + [(r, d) for r, d in zip(dc_refs, dres[1:])]
        for r, d in accs:
            @pl.when(first)
            def _(r=r, d=d):
                r[...] = d

            @pl.when(jnp.logical_not(first))
            def _(r=r, d=d):
                r[...] += d

    n_cb = C // cw
    outs = pl.pallas_call(
        body, name=name, grid=(n_cb, n),
        in_specs=tile_specs + halo_specs + [w_spec] + c_specs + g_specs,
        out_specs=([pl.BlockSpec((tr, cw), lambda j, i: (ri(i), j)) for _ in col_blocks]
                   + [pl.BlockSpec((1, cw), lambda j, i: (0, j)) for _ in col_blocks]
                   + [pl.BlockSpec((K, cw), lambda j, i: (0, j))]
                   + [pl.BlockSpec((1, cw), lambda j, i: (0, j)) for _ in consts]),
        out_shape=([jax.ShapeDtypeStruct((T, C), BF) for _ in col_blocks]
                   + [jax.ShapeDtypeStruct((1, C), F32) for _ in col_blocks]
                   + [jax.ShapeDtypeStruct((K, C), F32)]
                   + [jax.ShapeDtypeStruct((1, C), F32) for _ in consts]),
        scratch_shapes=[pltpu.VMEM((hb + tr + SUBLANE, cw), F32), pltpu.VMEM((hb + tr + hb + SUBLANE, cw), F32),
                        pltpu.VMEM((hb + tr, cw), F32), pltpu.VMEM((hb, cw), F32),
                        pltpu.VMEM((SUBLANE, hb + tr, cw), F32), pltpu.VMEM((SUBLANE, hb + tr + hb, cw), F32),
                        pltpu.VMEM((tr, cw), F32)],
        compiler_params=pltpu.CompilerParams(
            dimension_semantics=("parallel", "arbitrary"), vmem_limit_bytes=VMEM_LIMIT),
    )(*([x] * nb), *([x] * nb), w, *consts, *grads)
    dxs = outs[:nb]
    sums = outs[nb:2 * nb]
    dw = outs[2 * nb]
    dcs = outs[2 * nb + 1:]
    return dxs, sums, dw, dcs


def make_qkv_post(dk, cw, n_qk_chunks):
    def l2(t):
        return t * lax.rsqrt(jnp.sum(t * t, axis=-1, keepdims=True) + EPS)

    def post(c, cid):
        s = jax.nn.silu(c)
        normed = lane_groups(l2, dk, s)
        return (jnp.where(cid < n_qk_chunks, normed, s),)
    return post


def glu_pre(za, zb):
    return za * jax.nn.sigmoid(zb)


def bias_post(c, cid, b):
    return (c + b,)


def _col_to_row(col):
    C = col.shape[-2]
    ii = lax.broadcasted_iota(jnp.int32, (C, C), 0)
    jj = lax.broadcasted_iota(jnp.int32, (C, C), 1)
    wide = jnp.broadcast_to(col, col.shape[:-1] + (C,))
    return jnp.sum(jnp.where(ii == jj, wide, 0.0), axis=-2, keepdims=True)


@jax.custom_vjp
def solve_with_inverse(a, rhs, x):
    return mm(x, rhs, 1, 0, True)


def _swi_fwd(a, rhs, x):
    sol = mm(x, rhs, 1, 0, True)
    return sol, (x, sol)


def _swi_bwd(res, dsol):
    x, sol = res
    drhs = mm(x, dsol, 0, 0, True)
    da = -mm(drhs, sol, 1, 1, True)
    return da, drhs, jnp.zeros_like(x)


solve_with_inverse.defvjp(_swi_fwd, _swi_bwd)


def unit_lower_inverse(a):
    C = a.shape[-1]
    ii = lax.broadcasted_iota(jnp.int32, (C, C), 0)
    jj = lax.broadcasted_iota(jnp.int32, (C, C), 1)
    x = jnp.where(ii == jj, 1.0, 0.0).astype(F32) - a
    p = mm(a, a, 1, 0, THREE_PASS)
    steps = int(math.log2(C)) - 1
    for s in range(steps):
        x = x + mm(x, p, 1, 0, THREE_PASS)
        if s < steps - 1:
            p = mm(p, p, 1, 0, THREE_PASS)
    return x


def dn_masks(C):
    ii = lax.broadcasted_iota(jnp.int32, (C, C), 0)
    jj = lax.broadcasted_iota(jnp.int32, (C, C), 1)
    return ii >= jj, ii > jj


def dn_pre(q, k, v, beta, gc):
    C, dk = q.shape[-2:]
    tri, strict = dn_masks(C)
    q = q * (dk ** -0.5)
    diff = gc - _col_to_row(gc)
    decay = jnp.where(tri, jnp.exp(jnp.where(tri, diff, 0.0)), 0.0)
    kb = k * beta
    vb = v * beta
    a = jnp.where(strict, mm(kb, k, 1, 1) * decay, 0.0)
    eg = jnp.exp(gc)
    rhs = jnp.concatenate([vb, kb * eg], axis=-1)
    attn = mm(q, k, 1, 1) * decay
    qd = q * eg
    g_last = gc[..., C - 1:C, :]
    kt = k * jnp.exp(g_last - gc)
    gl = jnp.exp(g_last)
    return a, rhs, attn, qd, kt, gl


def dn_chunk(q, k, v, beta, gc, state, x):
    dv = v.shape[-1]
    a, rhs, attn, qd, kt, gl = dn_pre(q, k, v, beta, gc)
    sol = solve_with_inverse(a, rhs, x)
    u, w = sol[..., :dv], sol[..., dv:]
    vn = u - mm(w, state, 1, 0)
    o = mm(qd, state, 1, 0) + mm(attn, vn, 1, 0)
    new_state = state * gl + mm(kt, vn, 0, 0)
    return o, new_state


def _by_head(q_ref, k_ref, v_ref, bg, H, dk, dv):
    qv = jnp.stack([q_ref[:, h * dk:(h + 1) * dk] for h in range(H)])
    kv = jnp.stack([k_ref[:, h * dk:(h + 1) * dk] for h in range(H)])
    vv = jnp.stack([v_ref[:, h * dv:(h + 1) * dv] for h in range(H)])
    beta = jnp.stack([bg[:, h:h + 1] for h in range(H)])
    gc = jnp.stack([bg[:, H + h:H + h + 1] for h in range(H)])
    return qv, kv, vv, beta, gc


def deltanet_fwd(qkv, bgc, H):
    T = qkv.shape[0]
    dk = dv = qkv.shape[1] // (3 * H)
    N = T // CHUNK

    def body(q_ref, k_ref, v_ref, bgc_ref, o_ref, x_ref, s_ref, state):
        @pl.when(pl.program_id(0) == 0)
        def _():
            state[...] = jnp.zeros((H, dk, dv), F32)

        qv, kv, vv, beta, gc = _by_head(q_ref, k_ref, v_ref, bgc_ref[...], H, dk, dv)
        a = dn_pre(qv, kv, vv, beta, gc)[0]
        x = unit_lower_inverse(a)
        s = state[...]
        o, s_new = dn_chunk(qv, kv, vv, beta, gc, s, x)
        for h in range(H):
            o_ref[:, h * dv:(h + 1) * dv] = o[h]
        x_ref[...] = x
        s_ref[...] = s
        state[...] = s_new

    return pl.pallas_call(
        body, name="deltanet_fwd", grid=(N,),
        in_specs=[pl.BlockSpec((CHUNK, H * dk), lambda n: (n, 0)),
                  pl.BlockSpec((CHUNK, H * dk), lambda n: (n, 1)),
                  pl.BlockSpec((CHUNK, H * dv), lambda n: (n, 2)),
                  pl.BlockSpec((CHUNK, LANE), lambda n: (n, 0))],
        out_specs=[pl.BlockSpec((CHUNK, H * dv), lambda n: (n, 0)),
                   pl.BlockSpec((None, H, CHUNK, CHUNK), lambda n: (n, 0, 0, 0)),
                   pl.BlockSpec((None, H, dk, dv), lambda n: (n, 0, 0, 0))],
        out_shape=[jax.ShapeDtypeStruct((T, H * dv), F32),
                   jax.ShapeDtypeStruct((N, H, CHUNK, CHUNK), F32),
                   jax.ShapeDtypeStruct((N, H, dk, dv), F32)],
        scratch_shapes=[pltpu.VMEM((H, dk, dv), F32)],
        compiler_params=pltpu.CompilerParams(
            dimension_semantics=("arbitrary",), vmem_limit_bytes=VMEM_LIMIT),
    )(qkv, qkv, qkv, bgc)


def deltanet_bwd(qkv, bgc, xinv, states, do, H):
    T = qkv.shape[0]
    dk = dv = qkv.shape[1] // (3 * H)
    N = T // CHUNK

    def body(q_ref, k_ref, v_ref, bgc_ref, x_ref, s_ref, do_ref, dqkv_ref, dbgc_ref, dstate):
        @pl.when(pl.program_id(0) == 0)
        def _():
            dstate[...] = jnp.zeros((H, dk, dv), F32)

        qv, kv, vv, beta, gc = _by_head(q_ref, k_ref, v_ref, bgc_ref[...], H, dk, dv)
        do = jnp.stack([do_ref[:, h * dv:(h + 1) * dv] for h in range(H)])
        _, vjp = jax.vjp(dn_chunk, qv, kv, vv, beta, gc, s_ref[...], x_ref[...])
        dq, dk_, dv_, dbeta, dgc, ds, _ = vjp((do, dstate[...]))
        dstate[...] = ds
        lane = lax.broadcasted_iota(jnp.int32, (CHUNK, LANE), 1)
        dbgc = jnp.zeros((CHUNK, LANE), F32)
        for h in range(H):
            dqkv_ref[:, h * dk:(h + 1) * dk] = dq[h]
            dqkv_ref[:, (H + h) * dk:(H + h + 1) * dk] = dk_[h]
            dqkv_ref[:, (2 * H + h) * dk:(2 * H + h + 1) * dk] = dv_[h]
            dbgc = dbgc + jnp.where(lane == h, dbeta[h], 0.0) + jnp.where(lane == h + H, dgc[h], 0.0)
        dbgc_ref[...] = dbgc

    rn = lambda n: N - 1 - n
    return pl.pallas_call(
        body, name="deltanet_bwd", grid=(N,),
        in_specs=[pl.BlockSpec((CHUNK, H * dk), lambda n: (rn(n), 0)),
                  pl.BlockSpec((CHUNK, H * dk), lambda n: (rn(n), 1)),
                  pl.BlockSpec((CHUNK, H * dv), lambda n: (rn(n), 2)),
                  pl.BlockSpec((CHUNK, LANE), lambda n: (rn(n), 0)),
                  pl.BlockSpec((None, H, CHUNK, CHUNK), lambda n: (rn(n), 0, 0, 0)),
                  pl.BlockSpec((None, H, dk, dv), lambda n: (rn(n), 0, 0, 0)),
                  pl.BlockSpec((CHUNK, H * dv), lambda n: (rn(n), 0))],
        out_specs=[pl.BlockSpec((CHUNK, 3 * H * dk), lambda n: (rn(n), 0)),
                   pl.BlockSpec((CHUNK, LANE), lambda n: (rn(n), 0))],
        out_shape=[jax.ShapeDtypeStruct((T, 3 * H * dk), F32),
                   jax.ShapeDtypeStruct((T, LANE), F32)],
        scratch_shapes=[pltpu.VMEM((H, dk, dv), F32)],
        compiler_params=pltpu.CompilerParams(
            dimension_semantics=("arbitrary",), vmem_limit_bytes=VMEM_LIMIT),
    )(qkv, qkv, qkv, bgc, xinv, states, do)


def _place():
    x, y, c = lax.axis_index("x"), lax.axis_index("y"), lax.axis_index("c")
    chips = [(1 - x, y), (x, 1 - y), (1 - x, 1 - y)]
    return x, y, c, chips


def all_gather(name, shards):
    na = len(shards)

    def body(*refs):
        ins, outs = refs[:na], refs[na:2 * na]
        send_sems, recv_sems, local_sems = refs[2 * na:]
        x, y, c, chips = _place()
        me, sibling = (x, y, c), (x, y, 1 - c)

        def copy(a, k, block, to, src=None):
            dst = outs[a].at[4 * block[0] + 2 * block[1] + block[2]]
            return pltpu.make_async_remote_copy(
                src_ref=dst if src is None else src, dst_ref=dst,
                send_sem=send_sems.at[a, k], recv_sem=recv_sems.at[a, k],
                device_id=to, device_id_type=MESH)

        mine, first, passed = [], [], []
        for a in range(na):
            cp = pltpu.make_async_copy(ins[a], outs[a].at[4 * x + 2 * y + c], local_sems.at[a])
            cp.start()
            mine.append(cp)
        for a in range(na):
            cps = [copy(a, 0, me, sibling, src=ins[a])]
            cps += [copy(a, 1 + j, me, (*chip, c), src=ins[a]) for j, chip in enumerate(chips)]
            for cp in cps:
                cp.start()
            first += cps
        for a in range(na):
            for j, chip in enumerate(chips):
                copy(a, 1 + j, (*chip, c), me).wait_recv()
                cp = copy(a, 4 + j, (*chip, c), sibling)
                cp.start()
                passed.append(cp)
        for a in range(na):
            copy(a, 0, sibling, me).wait_recv()
            for j, chip in enumerate(chips):
                copy(a, 4 + j, (*chip, 1 - c), me).wait_recv()
        for cp in first + passed:
            cp.wait_send()
        for cp in mine:
            cp.wait()

    outs = pl.pallas_call(
        body, name=name,
        in_specs=[ANY] * na, out_specs=[ANY] * na,
        out_shape=[jax.ShapeDtypeStruct((N_DEV,) + s.shape, s.dtype) for s in shards],
        scratch_shapes=[pltpu.SemaphoreType.DMA((na, 7)), pltpu.SemaphoreType.DMA((na, 7)),
                        pltpu.SemaphoreType.DMA((na,))],
    )(*shards)
    return list(outs)


HBM_SPEC = pl.BlockSpec(memory_space=pltpu.HBM)
SEM_SPEC = pl.BlockSpec(memory_space=pltpu.SEMAPHORE)
EFFECT = pltpu.SideEffectType.DATAFLOW_SIDE_EFFECTING


def _descriptors(plan, bufs, send_sems, recv_sems):
    return [pltpu.make_async_remote_copy(src_ref=src, dst_ref=dst, send_sem=send_sems.at[k],
                                         recv_sem=recv_sems.at[k], device_id=dev, device_id_type=MESH)
            for k, (src, dst, dev) in enumerate(plan(bufs))]


def split_start(name, bufs, plan, n, after):
    nb = len(bufs)

    def body(*refs):
        for cp in _descriptors(plan, refs[:nb], refs[nb + 1], refs[nb + 2]):
            cp.start()
        refs[-1][...] = jnp.zeros((SUBLANE, LANE), F32)

    outs = pl.pallas_call(
        body, name=name,
        out_shape=(pltpu.SemaphoreType.DMA((n,)), pltpu.SemaphoreType.DMA((n,)),
                   *[pltpu.HBM(b.shape, b.dtype) for b in bufs],
                   jax.ShapeDtypeStruct((SUBLANE, LANE), F32)),
        in_specs=[HBM_SPEC] * nb + [ANY],
        out_specs=(SEM_SPEC, SEM_SPEC, *[HBM_SPEC] * nb, pl.BlockSpec(memory_space=pltpu.VMEM)),
        input_output_aliases={i: 2 + i for i in range(nb)},
        compiler_params=pltpu.CompilerParams(has_side_effects=EFFECT),
    )(*[pltpu.with_memory_space_constraint(b, pltpu.HBM) for b in bufs], after)
    return outs[0], outs[1], list(outs[2:2 + nb]), outs[-1]


def split_wait(name, send_sems, recv_sems, bufs, plan, after):
    nb = len(bufs)

    def body(*refs):
        cps = _descriptors(plan, refs[:nb], refs[nb], refs[nb + 1])
        for cp in cps:
            cp.wait_recv()
        for cp in cps:
            cp.wait_send()
        refs[-1][...] = jnp.zeros((SUBLANE, LANE), F32)

    afters = list(after) if isinstance(after, (list, tuple)) else [after]
    outs = pl.pallas_call(
        body, name=name,
        out_shape=[pltpu.HBM(b.shape, b.dtype) for b in bufs] + [jax.ShapeDtypeStruct((SUBLANE, LANE), F32)],
        in_specs=[HBM_SPEC] * nb + [SEM_SPEC, SEM_SPEC] + [ANY] * len(afters),
        out_specs=[HBM_SPEC] * nb + [pl.BlockSpec(memory_space=pltpu.VMEM)],
        input_output_aliases={i: i for i in range(nb)},
        compiler_params=pltpu.CompilerParams(has_side_effects=EFFECT),
    )(*bufs, send_sems, recv_sems, *afters)
    return list(outs[:nb]), outs[-1]


def _block(px, py, pc):
    return 4 * px + 2 * py + pc


def plan_gather_ici(na):
    def plan(bufs):
        x, y, c, chips = _place()
        out = []
        for a in range(na):
            dst = bufs[na + a].at[_block(x, y, c)]
            out.append((bufs[a], dst, (x, y, 1 - c)))
            out += [(bufs[a], dst, (px, py, c)) for px, py in chips]
        return out
    return plan


def plan_gather_pass(na):
    def plan(bufs):
        x, y, c, chips = _place()
        out = []
        for a in range(na):
            for px, py in chips:
                blk = bufs[a].at[_block(px, py, c)]
                out.append((blk, blk, (x, y, 1 - c)))
        return out
    return plan


def plan_reduce_d2d(na):
    def plan(bufs):
        x, y, c, _ = _place()
        return [(bufs[a].at[2 * s + (1 - c)], bufs[na + a].at[s], (x, y, 1 - c))
                for a in range(na) for s in range(4)]
    return plan


def plan_reduce_ici(na):
    def plan(bufs):
        x, y, c, chips = _place()
        return [(bufs[a].at[2 * px + py], bufs[na + a].at[j], (px, py, c))
                for a in range(na) for j, (px, py) in enumerate(chips)]
    return plan


def place_own(name, land, shard, dev):
    r, c = shard.shape
    tr = _tile(r, 512)

    def body(sp_ref, s_ref, land_ref, o_ref):
        o_ref[...] = s_ref[...]

    return pl.pallas_call(
        body, name=name, out_shape=jax.ShapeDtypeStruct(land.shape, land.dtype),
        grid_spec=pltpu.PrefetchScalarGridSpec(
            num_scalar_prefetch=1, grid=(r // tr,),
            in_specs=[pl.BlockSpec((tr, c), lambda i, s: (i, 0)), ANY],
            out_specs=pl.BlockSpec((None, tr, c), lambda i, s: (s[0], i, 0))),
        input_output_aliases={2: 0},
        compiler_params=pltpu.CompilerParams(dimension_semantics=("arbitrary",)),
    )(dev, shard, land)


def pack(arrs, row_mult=SUBLANE):
    pieces = []
    for a in arrs:
        f = a.reshape(-1).astype(F32)
        pad = (-f.shape[0]) % LANE
        if pad:
            f = jnp.concatenate([f, jnp.zeros((pad,), F32)])
        pieces.append(f)
    flat = jnp.concatenate(pieces)
    rows = flat.shape[0] // LANE
    pad_rows = (-rows) % row_mult
    if pad_rows:
        flat = jnp.concatenate([flat, jnp.zeros((pad_rows * LANE,), F32)])
    return flat.reshape(-1, LANE)


def unpack(buf, shapes):
    flat = buf.reshape(-1)
    outs, off = [], 0
    for shp in shapes:
        n = int(np.prod(shp))
        outs.append(flat[off:off + n].reshape(shp))
        off += n + ((-n) % LANE)
    return outs


def _vjp_rows(fn, n_row_in, n_cot):
    def bwd(*args):
        rows = args[:n_row_in]
        cots = args[n_row_in:n_row_in + n_cot]
        consts = args[n_row_in + n_cot:]
        out, vjp = jax.vjp(fn, *rows, *consts)
        if isinstance(out, (tuple, list)):
            cot = tuple(c.astype(o.dtype) for c, o in zip(cots, out))
        else:
            cot = cots[0].astype(out.dtype)
        return vjp(cot)
    return bwd


def rms_fwd(name, x, g):
    return rowcall(name, rms_tile, [x], [g], [(x.shape[1], BF)])


def rms_bwd(name, x, g, dh, dres, after=None):
    D = x.shape[1]
    vj = _vjp_rows(rms_tile, 1, 1)

    def f(x_, dh_, dres_, g_):
        dx, dg = vj(x_, dh_, g_)
        dx = dx + dres_
        return dx, dx, dg
    return rowcall(name, f, [x, dh, dres], [g], [(D, F32), (D, BF)], [(1, D)], after=after)


def kernel(x, e_norm, e_w_in, e_conv_w, e_a_log, e_dt_bias, e_o_norm, e_ln_g, e_ln_b, e_w_s, e_b_s, e_w_out, o_norm, o_pw1, o_pw1_b, o_dw, o_dw_b, o_ln_g, o_ln_b, o_pw2, o_pw2_b, f_norm, f_w1, f_w2, final_norm, loss_target, m_e_norm, m_e_w_in, m_e_conv_w, m_e_a_log, m_e_dt_bias, m_e_o_norm, m_e_ln_g, m_e_ln_b, m_e_w_s, m_e_b_s, m_e_w_out, m_o_norm, m_o_pw1, m_o_pw1_b, m_o_dw, m_o_dw_b, m_o_ln_g, m_o_ln_b, m_o_pw2, m_o_pw2_b, m_f_norm, m_f_w1, m_f_w2, m_final_norm, v_e_norm, v_e_w_in, v_e_conv_w, v_e_a_log, v_e_dt_bias, v_e_o_norm, v_e_ln_g, v_e_ln_b, v_e_w_s, v_e_b_s, v_e_w_out, v_o_norm, v_o_pw1, v_o_pw1_b, v_o_dw, v_o_dw_b, v_o_ln_g, v_o_ln_b, v_o_pw2, v_o_pw2_b, v_f_norm, v_f_w1, v_f_w2, v_final_norm):
    names = ['e_norm', 'e_w_in', 'e_conv_w', 'e_a_log', 'e_dt_bias', 'e_o_norm', 'e_ln_g', 'e_ln_b', 'e_w_s', 'e_b_s', 'e_w_out', 'o_norm', 'o_pw1', 'o_pw1_b', 'o_dw', 'o_dw_b', 'o_ln_g', 'o_ln_b', 'o_pw2', 'o_pw2_b', 'f_norm', 'f_w1', 'f_w2', 'final_norm']
    W = dict(zip(names, [e_norm, e_w_in, e_conv_w, e_a_log, e_dt_bias, e_o_norm, e_ln_g, e_ln_b, e_w_s, e_b_s, e_w_out, o_norm, o_pw1, o_pw1_b, o_dw, o_dw_b, o_ln_g, o_ln_b, o_pw2, o_pw2_b, f_norm, f_w1, f_w2, final_norm]))
    Mo = dict(zip(names, [m_e_norm, m_e_w_in, m_e_conv_w, m_e_a_log, m_e_dt_bias, m_e_o_norm, m_e_ln_g, m_e_ln_b, m_e_w_s, m_e_b_s, m_e_w_out, m_o_norm, m_o_pw1, m_o_pw1_b, m_o_dw, m_o_dw_b, m_o_ln_g, m_o_ln_b, m_o_pw2, m_o_pw2_b, m_f_norm, m_f_w1, m_f_w2, m_final_norm]))
    Vo = dict(zip(names, [v_e_norm, v_e_w_in, v_e_conv_w, v_e_a_log, v_e_dt_bias, v_e_o_norm, v_e_ln_g, v_e_ln_b, v_e_w_s, v_e_b_s, v_e_w_out, v_o_norm, v_o_pw1, v_o_pw1_b, v_o_dw, v_o_dw_b, v_o_ln_g, v_o_ln_b, v_o_pw2, v_o_pw2_b, v_f_norm, v_f_w1, v_f_w2, v_final_norm]))

    T, D = x.shape[1], x.shape[2]
    H = e_a_log.shape[-1]
    dv = e_o_norm.shape[-1]
    dk = dv
    G = e_w_s.shape[1]
    AQK, AV, BW = H * dk, H * dv, e_ln_g.shape[-1]
    AQKV = 2 * AQK + AV
    in_cols = AQKV + AV + 2 * H + 2 * BW
    KA = e_conv_w.shape[1]
    KC = o_dw.shape[1]
    L = f_norm.shape[0]
    dev = 4 * lax.axis_index("x") + 2 * lax.axis_index("y") + lax.axis_index("c")
    x2d = x.reshape(T, D)
    tgt = loss_target.reshape(T, D)

    dev_sp = dev.astype(jnp.int32).reshape(1)
    where = jnp.stack([lax.axis_index("c"), 2 * lax.axis_index("x") + lax.axis_index("y")]).astype(jnp.int32)
    row = lambda a: a.reshape(1, -1).astype(F32)
    en_row = row(e_norm)

    def gather_begin(tag, shards, after):
        na = len(shards)
        lands = [lax.empty((N_DEV,) + s.shape, s.dtype) for s in shards]
        ss, rs, bufs, tok = split_start(f"gather{tag}_ici_start", shards + lands, plan_gather_ici(na), 4 * na, after)
        return (na, ss, rs, bufs), tok

    def gather_pass(tag, h, after):
        na, ss, rs, bufs = h
        bufs, tok = split_wait(f"gather{tag}_ici_wait", ss, rs, bufs, plan_gather_ici(na), after)
        ss, rs, lands, tok = split_start(f"gather{tag}_pass_start", bufs[na:], plan_gather_pass(na), 3 * na, tok)
        return (na, ss, rs, bufs[:na], lands), tok

    def gather_end(tag, h, after):
        na, ss, rs, shards, lands = h
        lands, _ = split_wait(f"gather{tag}_pass_wait", ss, rs, lands, plan_gather_pass(na), after)
        return [place_own(f"gather{tag}_own{a}", lands[a], shards[a], dev_sp) for a in range(na)]

    small_sharded = ['e_conv_w', 'o_norm', 'o_pw1_b', 'o_dw', 'o_dw_b', 'o_ln_g', 'o_ln_b', 'o_pw2_b']
    sm = all_gather("gather_small", [pack([W[n][0]]) for n in small_sharded])

    bfw = lambda w: w.astype(BF)
    hA0, tok = gather_begin("0", [bfw(jnp.swapaxes(e_w_in[0], 0, 1))], sm[0])
    hA1, tok = gather_begin("1", [bfw(e_w_out[0]), bfw(f_w1[0]), bfw(f_w2[0])], tok)
    hA2, tok = gather_begin("2", [bfw(o_pw1[0]), bfw(o_pw2[0]), bfw(f_w1[1]), bfw(f_w2[1])], tok)
    h0 = rowcall("rms_e", rms_tile, [x2d], [en_row], [(D, BF)], after=tok)

    full = {}
    for n, g in zip(small_sharded, sm):
        shp = W[n][0].shape
        blocks = [unpack(g[d], [shp])[0] for d in range(N_DEV)]
        full[n] = jnp.concatenate(blocks, axis=-1)
    conv_w = full['e_conv_w']
    on_row, pw1_b_row = row(full['o_norm']), row(full['o_pw1_b'])
    dw_w, dw_b_row = full['o_dw'], row(full['o_dw_b'])
    oln_g_row, oln_b_row, pw2_b_row = row(full['o_ln_g']), row(full['o_ln_b']), row(full['o_pw2_b'])
    small_names = ['e_norm', 'e_conv_w', 'e_a_log', 'e_dt_bias', 'e_o_norm', 'e_ln_g', 'e_ln_b', 'e_w_s', 'e_b_s',
                   'o_norm', 'o_pw1_b', 'o_dw', 'o_dw_b', 'o_ln_g', 'o_ln_b', 'o_pw2_b', 'f_norm', 'final_norm']
    packed_wmv = [pack([A[n] for n in small_names], 256) for A in (W, Mo, Vo)]

    hB0, tok = gather_pass("0", hA0, [h0, conv_w, dw_w, pw1_b_row] + packed_wmv)
    (g_win,) = gather_end("0", hB0, tok)
    win_t = g_win.reshape(in_cols, D)
    wt_qkv, wt_z = win_t[:AQKV], win_t[AQKV:AQKV + AV]
    wt_ba = jnp.pad(win_t[AQKV + AV:AQKV + AV + 2 * H], ((0, LANE - 2 * H), (0, 0)))
    wt_uv = win_t[AQKV + AV + 2 * H:]

    alog_row = jnp.pad(row(e_a_log), ((0, 0), (H, LANE - 2 * H)))
    dtb_row = jnp.pad(row(e_dt_bias), ((0, 0), (H, LANE - 2 * H)))
    eon_row = row(e_o_norm)
    eln_g_row, eln_b_row = row(e_ln_g), row(e_ln_b)
    w_s = e_w_s[0]
    bs_t = e_b_s[0].T
    fn_rows = [row(f_norm[l]) for l in range(L)]
    fin_row = row(final_norm)

    qkv_raw = matmul("proj_qkv", h0, wt_qkv, "nt")
    z_gate = matmul("proj_z", h0, wt_z, "nt")
    ba = matmul("proj_ba", h0, wt_ba, "nt")
    uv = matmul("proj_uv", h0, wt_uv, "nt")

    cwa = min(512, AQKV)
    qkv_post = make_qkv_post(dk, cwa, 2 * AQK // cwa)
    ident = lambda t: t
    qkv = conv_fwd("qkv_conv", qkv_raw, conv_w, [], ident, qkv_post, [0], 1, KA, cw=cwa, tr=512)
    bgc_fn = make_bgc(H)
    bgc = rowcall("bgc", bgc_fn, [ba], [alog_row, dtb_row], [(LANE, F32)])
    o_dn, xinv, states = deltanet_fwd(qkv, bgc, H)
    hB1, tok = gather_pass("1", hA1, o_dn)

    def mix_tile(o, z, uv_, o_norm_, ln_g, ln_b, w_s_, bs_t_):
        return jnp.concatenate([mixa_post_tile(o, z, o_norm_), mixb_tile(uv_, ln_g, ln_b, w_s_, bs_t_)], axis=-1)
    mix_consts = [eon_row, eln_g_row, eln_b_row, w_s, bs_t]
    mix = rowcall("mix", mix_tile, [o_dn, z_gate, uv], mix_consts, [(AV + BW, BF)], after=tok)
    g_wout, g_w1_0, g_w2_0 = gather_end("1", hB1, mix)
    wout = g_wout.reshape(-1, D)
    w1 = [jnp.moveaxis(g_w1_0, 0, 1).reshape(D, -1), None]
    w2 = [g_w2_0.reshape(-1, D), None]
    add_epi = lambda acc, r: (acc + r,)
    x1 = matmul("out_proj", mix, wout, "nn", epi=add_epi, extras=[x2d], tm=512, tn=2048)

    def relu2_epi(acc):
        r = jnp.maximum(acc, 0.0)
        return r * r, r

    hf0 = rms_fwd("rms_f0", x1, fn_rows[0])
    a2_0, ar_0 = matmul("ffn_up0", hf0, w1[0], "nn", epi=relu2_epi, out_dtypes=(BF, BF))
    hB2, tok = gather_pass("2", hA2, a2_0)
    x2 = matmul("ffn_down0", a2_0, w2[0], "nn", epi=add_epi, extras=[x1], after=tok)
    ffn0 = (hf0, a2_0, ar_0)
    g_pw1, g_pw2, g_w1_1, g_w2_1 = gather_end("2", hB2, x2)
    pw1 = jnp.moveaxis(g_pw1, 0, 1).reshape(D, 2 * D)
    pw2 = g_pw2.reshape(D, D)
    w1[1] = jnp.moveaxis(g_w1_1, 0, 1).reshape(D, -1)
    w2[1] = g_w2_1.reshape(-1, D)

    h1 = rms_fwd("rms_o", x2, on_row)
    bias_epi = lambda acc, b: (acc + b,)
    zc = matmul("pw1", h1, pw1, "nn", epi=bias_epi, extras=[pw1_b_row])
    cwc = min(512, D)
    ncb = D // cwc
    cconv = conv_fwd("dw_conv", zc, dw_w, [dw_b_row], glu_pre, bias_post, [0, ncb], 1, KC, hb=32, cw=cwc, tr=512)
    ln_silu = lambda c, g, b: jax.nn.silu(ln_tile(c, g, b))
    s_act = rowcall("ln_silu", ln_silu, [cconv], [oln_g_row, oln_b_row], [(D, BF)])
    x3 = matmul("pw2", s_act, pw2, "nn", epi=lambda acc, r, b: (r + (acc + b),), extras=[x2, pw2_b_row],
                tm=512, tn=2048)
    hf1 = rms_fwd("rms_f1", x3, fn_rows[1])
    a2_1, ar_1 = matmul("ffn_up1", hf1, w1[1], "nn", epi=relu2_epi, out_dtypes=(BF, BF))
    x4 = matmul("ffn_down1", a2_1, w2[1], "nn", epi=add_epi, extras=[x3])
    ffn1 = (hf1, a2_1, ar_1)

    def loss_bwd_tile(x_, t_, g_):
        l, vjp = jax.vjp(lambda a, b: loss_tile(a, b, t_), x_, g_)
        dx, dg = vjp(jnp.ones_like(l))
        return dx, dx, l, dg
    dx4, dx4_b, loss_part, d_final = rowcall("loss_head", loss_bwd_tile, [x4, tgt], [fin_row],
                                             [(D, F32), (D, BF)], [(1, 1), (1, D)])
    loss = lax.psum(loss_part[0, 0], ("x", "y", "c"))

    def reduce_begin(tag, grads, after):
        na = len(grads)
        lands = [lax.empty((4,) + g.shape[1:], g.dtype) for g in grads]
        ss, rs, bufs, tok = split_start(f"reduce{tag}_d2d_start", grads + lands, plan_reduce_d2d(na), 4 * na, after)
        return (na, ss, rs, bufs), tok

    def reduce_mid(tag, h, after):
        na, ss, rs, bufs = h
        bufs, tok = split_wait(f"reduce{tag}_d2d_wait", ss, rs, bufs, plan_reduce_d2d(na), after)
        parts = []
        for a, (g, rc) in enumerate(zip(bufs[:na], bufs[na:])):
            r, c = g.shape[1], g.shape[2]
            if r % SUBLANE:
                parts.append(slabcall(f"chip_sum{tag}_{a}", lambda p, q: (p + q,),
                                      [(g, lambda i, s: 2 * i + s[0]), (rc, lambda i, s: i)], [BF], where, 4))
                continue
            mine = lambda i, n, s: (2 * (i // (n // 4)) + s[0]) * (n // 4) + i % (n // 4)
            parts.append(rowcall(f"chip_sum{tag}_{a}", lambda p, q: (p + q,),
                                 [(g.reshape(N_DEV * r, c), mine), rc.reshape(4 * r, c)], [],
                                 [(c, BF)], tr=_tile(r, 512), sp=where, R=4 * r).reshape(4, r, c))
        lands = [lax.empty((3,) + p.shape[1:], p.dtype) for p in parts]
        ss, rs, bufs, tok = split_start(f"reduce{tag}_ici_start", parts + lands, plan_reduce_ici(na), 3 * na, tok)
        return (na, ss, rs, bufs), tok

    res = {}

    def reduce_end(tag, h, after, targets):
        na, ss, rs, bufs = h
        bufs, _ = split_wait(f"reduce{tag}_ici_wait", ss, rs, bufs, plan_reduce_ici(na), after)
        for a, (part, fin, (n, l)) in enumerate(zip(bufs[:na], bufs[na:], targets)):
            def f(p0, p1, p2, p3, w_, m_, v_):
                g = ((p0.astype(F32) + p1.astype(F32)) + p2.astype(F32)) + p3.astype(F32)
                return (g,) + adamw_tile(w_, g, m_, v_)
            r, C = fin.shape[-2], fin.shape[-1]
            if W[n].shape[1:] == (C, r):
                t = lambda arr: jnp.swapaxes(arr[l], 0, 1)
                outs = slabcall(f"adam{tag}_{a}", f, [(part, lambda i, s: s[1]), (fin, 0), (fin, 1), (fin, 2),
                                                      (t(W[n]), None), (t(Mo[n]), None), (t(Vo[n]), None)],
                                [F32] * 4, where)
                res[n] = tuple(jnp.swapaxes(o, 0, 1)[None] for o in outs)
                continue
            own = lambda i, n_, s: s[1] * n_ + i
            res[n] = rowcall(f"adam{tag}_{a}", f, [(part.reshape(4 * r, C), own), (fin, 0), (fin, 1), (fin, 2),
                                                   (W[n], l), (Mo[n], l), (Vo[n], l)], [],
                             [(C, F32)] * 4, tr=256, sp=where, R=r,
                             out_lead=(W[n].shape[0], l), into=res.get(n))

    def tie(small, tok):
        return small + tok[0:1, 0:1]

    dscale_epi = lambda acc, r: (acc * (2.0 * r.astype(F32)),)
    d_fnorm = [None] * L

    dpre1 = matmul("ffn_down_dx1", dx4_b, w2[1], "nt", epi=dscale_epi, extras=[ar_1], out_dtypes=(BF,))
    dw2_1 = matmul("ffn_down_dw1", a2_1, dx4_b, "tn")
    dw1_1 = matmul("ffn_up_dw1", hf1, dpre1, "tn", colshard=True)
    hD1, tok = reduce_begin("1", [dw1_1, dw2_1.reshape(N_DEV, -1, D)], dpre1)
    dhf1 = matmul("ffn_up_dx1", dpre1, w1[1], "nt", after=tok, tk=4096)
    dx3, dx3_b, d_fnorm[1] = rms_bwd("rms_f_bwd1", x3, fn_rows[1], dhf1, dx4)

    ds_act = matmul("pw2_dx", dx3_b, pw2, "nt")
    hI1, tok = reduce_mid("1", hD1, ds_act)
    d_pw2 = matmul("pw2_dw", s_act, dx3_b, "tn", after=tok)
    ln_silu_bwd = _vjp_rows(ln_silu, 1, 1)

    def ln_silu_bwd_tile(c_, ds_, dx3_, g_, b_):
        dc, dg, db = ln_silu_bwd(c_, ds_, g_, b_)
        return dc, dg, db, jnp.sum(dx3_, axis=0, keepdims=True)
    dcconv, d_oln_g, d_oln_b, d_pw2_b = rowcall(
        "ln_silu_bwd", ln_silu_bwd_tile, [cconv, ds_act, dx3], [oln_g_row, oln_b_row],
        [(D, F32)], [(1, D), (1, D), (1, D)])
    (dza, dzb), (sza, szb), d_dw, (d_dw_b,) = conv_bwd(
        "dw_conv_bwd", zc, dw_w, [tie(dw_b_row, tok)], [dcconv], glu_pre, bias_post, [0, ncb], KC, hb=32, cw=cwc, tr=256,
        recompute=False)
    dzc = jnp.concatenate([dza, dzb], axis=-1)
    d_pw1_b = jnp.concatenate([sza, szb], axis=-1)
    d_pw1 = matmul("pw1_dw", h1, dzc, "tn", colshard=True)
    dh1 = matmul("pw1_dx", dzc, pw1, "nt")
    dx2, dx2_b, d_onorm = rms_bwd("rms_o_bwd", x2, on_row, dh1, dx3)
    reduce_end("1", hI1, dx2, [('f_w1', 1), ('f_w2', 1)])

    hD2, tok = reduce_begin("2", [d_pw1, d_pw2.reshape(N_DEV, -1, D)], dx2)
    dpre0 = matmul("ffn_down_dx0", dx2_b, w2[0], "nt", epi=dscale_epi, extras=[ar_0], out_dtypes=(BF,), after=tok)
    dw2_0 = matmul("ffn_down_dw0", a2_0, dx2_b, "tn")
    hI2, tok = reduce_mid("2", hD2, dw2_0)
    dw1_0 = matmul("ffn_up_dw0", hf0, dpre0, "tn", colshard=True, after=tok)
    dhf0 = matmul("ffn_up_dx0", dpre0, w1[0], "nt", tk=4096)
    dx1, dx1_b, d_fnorm[0] = rms_bwd("rms_f_bwd0", x1, fn_rows[0], dhf0, dx2)
    reduce_end("2", hI2, dx1, [('o_pw1', 0), ('o_pw2', 0)])

    dmix = matmul("out_proj_dx", dx1_b, wout, "nt")
    d_wout = matmul("out_proj_dw", mix, dx1_b, "tn")
    hD3, tok = reduce_begin("3", [dw1_0, dw2_0.reshape(N_DEV, -1, D), d_wout.reshape(N_DEV, -1, D)], dmix)
    do_dn, dz_gate, duv, d_eon, d_eln_g, d_eln_b, d_ws, d_bs_t = rowcall(
        "mix_bwd", _vjp_rows(mix_tile, 3, 1), [o_dn, z_gate, uv, dmix], mix_consts,
        [(AV, F32), (AV, BF), (2 * BW, BF)], [(1, dv), (1, BW), (1, BW), w_s.shape, bs_t.shape], after=tok)
    dqkv, dbgc = deltanet_bwd(qkv, bgc, xinv, states, do_dn, H)
    hI3, tok = reduce_mid("3", hD3, dbgc)
    bgc_bwd = _vjp_rows(bgc_fn, 1, 1)
    dba, d_alog_row, d_dtb_row = rowcall(
        "bgc_bwd", bgc_bwd, [ba, dbgc], [alog_row, dtb_row], [(LANE, BF)], [(1, LANE), (1, LANE)])
    (dqkv_raw,), _, d_conv_w, _ = conv_bwd(
        "qkv_conv_bwd", qkv_raw, tie(conv_w, tok), [], [dqkv], ident, qkv_post, [0], KA, cw=cwa, tr=512)

    dw_qkv = matmul("proj_qkv_dw", dqkv_raw, h0, "tn")
    dw_z = matmul("proj_z_dw", dz_gate, h0, "tn")
    dw_ba = matmul("proj_ba_dw", dba, h0, "tn")
    dw_uv = matmul("proj_uv_dw", duv, h0, "tn")
    d_win_t = jnp.concatenate([dw_qkv, dw_z, dw_ba[:2 * H], dw_uv], axis=0)
    G_win = d_win_t.reshape(N_DEV, in_cols // N_DEV, D)
    hD4, tok = reduce_begin("4", [G_win], dw_uv)
    reduce_end("3", hI3, tok, [('f_w1', 0), ('f_w2', 0), ('e_w_out', 0)])
    hI4, tok = reduce_mid("4", hD4, [res[n][0] for n in ('f_w1', 'f_w2', 'o_pw1', 'e_w_out')])
    dh0 = matmul_sum("proj_dx", [(dqkv_raw, wt_qkv), (dz_gate, wt_z), (dba, wt_ba), (duv, wt_uv)], after=tok)
    grad_x, _, d_enorm = rms_bwd("rms_e_bwd", x2d, en_row, dh0, dx1, after=tok)

    d_alog = d_alog_row[:, H:2 * H]
    d_dtb = d_dtb_row[:, H:2 * H]
    small_grads = [d_enorm, d_conv_w, d_alog, d_dtb, d_eon, d_eln_g, d_eln_b, d_ws, d_bs_t.T,
                   d_onorm, d_pw1_b, d_dw, d_dw_b, d_oln_g, d_oln_b, d_pw2_b,
                   jnp.concatenate(d_fnorm, axis=0), d_final]
    full_shapes = [g.shape for g in small_grads]
    gs_all = all_gather("gather_small_grads", [pack(small_grads, 256)])[0]

    def sum8(*ps):
        s = ps[0]
        for p in ps[1:]:
            s = s + p
        return (s,)
    gs_sum = rowcall("small_sum", sum8, [(gs_all, d) for d in range(N_DEV)], [], [(LANE, F32)])
    reduce_end("4", hI4, gs_sum, [('e_w_in', 0)])
    g_full = dict(zip(small_names, unpack(gs_sum, full_shapes)))
    g_loc = {}
    for n in small_names:
        g = g_full[n]
        if n in small_sharded:
            per = g.shape[-1] // N_DEV
            g = lax.dynamic_slice_in_dim(g, dev * per, per, axis=-1)
        g_loc[n] = g.reshape(W[n].shape)
    packs = [packed_wmv[0], pack([g_loc[n] for n in small_names], 256), packed_wmv[1], packed_wmv[2]]
    d_s, m_s, v_s = rowcall("adam_small", adamw_tile, packs, [], [(LANE, F32)] * 3)
    shapes = [W[n].shape for n in small_names]
    for n, d_, m_, v_ in zip(small_names, unpack(d_s, shapes), unpack(m_s, shapes), unpack(v_s, shapes)):
        res[n] = (g_loc[n], d_, m_, v_)

    grads = [res[n][0] for n in names]
    deltas = [res[n][1] for n in names]
    new_m = [res[n][2] for n in names]
    new_v = [res[n][3] for n in names]
    return (loss, grad_x.reshape(x.shape), *grads, *deltas, *new_m, *new_v)
```

```python
import functools
import math

import jax
import jax.numpy as jnp
import numpy as np
from jax import lax
from jax.experimental import pallas as pl
from jax.experimental.pallas import tpu as pltpu

F32 = jnp.float32
BF = jnp.bfloat16
EPS = 1e-6
CHUNK = 64
B_BLOCK = 128
LANE = 128
SUBLANE = 8
N_DEV = 8
VMEM_LIMIT = 56 * 1024 * 1024

ADAM_LR = 0.001
ADAM_B1 = 0.9
ADAM_B2 = 0.999
ADAM_EPS = 1e-08
ADAM_WD = 0.01
ADAM_STEP = 10

MESH = pl.DeviceIdType.MESH
ANY = pl.BlockSpec(memory_space=pl.ANY)


def _tile(n, pref, mult=SUBLANE):
    if n <= pref:
        return n
    t = (pref // mult) * mult
    while t >= mult:
        if n % t == 0:
            return t
        t -= mult
    return n


THREE_PASS = 3


def _dg(a, b, ca, cb, hi):
    nb = a.ndim - 2
    batch = tuple(range(nb))
    dims = (((ca + nb,), (cb + nb,)), (batch, batch))
    if hi == THREE_PASS:
        a, b = a.astype(F32), b.astype(F32)
        ah, bh = a.astype(BF), b.astype(BF)
        al, bl = (a - ah.astype(F32)).astype(BF), (b - bh.astype(F32)).astype(BF)
        dot = lambda p, q: lax.dot_general(p, q, dims, preferred_element_type=F32)
        return dot(ah, bh) + (dot(ah, bl) + dot(al, bh))
    if hi:
        return lax.dot_general(a.astype(F32), b.astype(F32), dims,
                               precision=lax.Precision.HIGHEST, preferred_element_type=F32)
    return lax.dot_general(a.astype(BF), b.astype(BF), dims, preferred_element_type=F32)


@functools.partial(jax.custom_vjp, nondiff_argnums=(2, 3, 4))
def mm(a, b, ca, cb, hi=False):
    return _dg(a, b, ca, cb, hi)


def _mm_fwd(a, b, ca, cb, hi):
    return _dg(a, b, ca, cb, hi), (a, b)


def _mm_bwd(ca, cb, hi, res, g):
    a, b = res
    if ca == 1:
        da = mm(g, b, 1, 1 - cb, hi)
    else:
        da = mm(b, g, 1 - cb, 1, hi)
    if cb == 0:
        db = mm(a, g, 1 - ca, 0, hi)
    else:
        db = mm(g, a, 0, 1 - ca, hi)
    return da.astype(a.dtype), db.astype(b.dtype)


mm.defvjp(_mm_fwd, _mm_bwd)


def matmul(name, a, b, mode, epi=None, extras=(), out_dtypes=(F32,), colshard=False,
           tm=None, tn=1024, tk=2048, after=None):
    afters = [] if after is None else [after]
    if mode == "nn":
        (M, K), (K2, N) = a.shape, b.shape
    elif mode == "nt":
        (M, K), (N, K2) = a.shape, b.shape
    else:
        (K, M), (K2, N) = a.shape, b.shape
    assert K == K2, (name, a.shape, b.shape, mode)
    if tm is None:
        tm = 1024
    tm = _tile(M, tm)
    tn = N // N_DEV if colshard else _tile(N, tn, LANE)
    tk = _tile(K, tk, LANE)
    nk = K // tk
    grid = (M // tm, N // tn, nk)
    if mode == "nn":
        a_spec = pl.BlockSpec((tm, tk), lambda i, j, k: (i, k))
        b_spec = pl.BlockSpec((tk, tn), lambda i, j, k: (k, j))
        ca, cb = 1, 0
    elif mode == "nt":
        a_spec = pl.BlockSpec((tm, tk), lambda i, j, k: (i, k))
        b_spec = pl.BlockSpec((tn, tk), lambda i, j, k: (j, k))
        ca, cb = 1, 1
    else:
        a_spec = pl.BlockSpec((tk, tm), lambda i, j, k: (k, i))
        b_spec = pl.BlockSpec((tk, tn), lambda i, j, k: (k, j))
        ca, cb = 0, 0
    ex_specs = []
    for e in extras:
        if e.shape[0] == 1:
            ex_specs.append(pl.BlockSpec((1, tn), lambda i, j, k: (0, j)))
        else:
            assert e.shape == (M, N), (name, e.shape)
            ex_specs.append(pl.BlockSpec((tm, tn), lambda i, j, k: (i, j)))
    if colshard:
        out_shape = [jax.ShapeDtypeStruct((N_DEV, M, tn), dt) for dt in out_dtypes]
        out_specs = [pl.BlockSpec((None, tm, tn), lambda i, j, k: (j, i, 0)) for _ in out_dtypes]
    else:
        out_shape = [jax.ShapeDtypeStruct((M, N), dt) for dt in out_dtypes]
        out_specs = [pl.BlockSpec((tm, tn), lambda i, j, k: (i, j)) for _ in out_dtypes]
    n_ex, n_out = len(extras), len(out_dtypes)

    def body(*refs):
        a_ref, b_ref = refs[0], refs[1]
        ex_refs = refs[2:2 + n_ex]
        first_out = 2 + n_ex + len(afters)
        o_refs = refs[first_out:first_out + n_out]
        part = _dg(a_ref[...], b_ref[...], ca, cb, False)

        def finish(acc):
            res = (acc,) if epi is None else epi(acc, *[r[...] for r in ex_refs])
            for o_ref, r in zip(o_refs, res):
                o_ref[...] = r.astype(o_ref.dtype)

        if nk == 1:
            finish(part)
            return
        acc_ref = refs[-1]
        k = pl.program_id(2)

        @pl.when(k == 0)
        def _():
            acc_ref[...] = part

        @pl.when(k > 0)
        def _():
            acc_ref[...] += part

        @pl.when(k == nk - 1)
        def _():
            finish(acc_ref[...])

    outs = pl.pallas_call(
        body, name=name, grid=grid,
        in_specs=[a_spec, b_spec] + ex_specs + [ANY] * len(afters),
        out_specs=out_specs, out_shape=out_shape,
        scratch_shapes=[pltpu.VMEM((tm, tn), F32)] if nk > 1 else [],
        compiler_params=pltpu.CompilerParams(
            dimension_semantics=("parallel", "parallel", "arbitrary"),
            vmem_limit_bytes=VMEM_LIMIT),
    )(a, b, *extras, *afters)
    return outs[0] if n_out == 1 else tuple(outs)


def norm_matmul(name, x, g, b, epi=None, extras=(), out_dtypes=(F32,), tm=1024, tn=1024, after=None):
    (M, K), N = x.shape, b.shape[1]
    tm, tn = _tile(M, tm), _tile(N, tn, LANE)
    afters = [] if after is None else [after]
    ex_specs = [pl.BlockSpec((1, tn), lambda i, j: (0, j)) if e.shape[0] == 1
                else pl.BlockSpec((tm, tn), lambda i, j: (i, j)) for e in extras]
    n_ex, n_out = len(extras), len(out_dtypes)

    def body(*refs):
        x_ref, g_ref, b_ref = refs[:3]
        ex_refs = refs[3:3 + n_ex]
        first_out = 3 + n_ex + len(afters)
        o_refs = refs[first_out:first_out + n_out]
        h_ref, h_scr = refs[first_out + n_out], refs[-1]

        @pl.when(pl.program_id(1) == 0)
        def _():
            h = rms_tile(x_ref[...], g_ref[...]).astype(h_scr.dtype)
            h_scr[...] = h
            h_ref[...] = h

        acc = _dg(h_scr[...], b_ref[...], 1, 0, False)
        res = (acc,) if epi is None else epi(acc, *[r[...] for r in ex_refs])
        for o_ref, r in zip(o_refs, res):
            o_ref[...] = r.astype(o_ref.dtype)

    outs = pl.pallas_call(
        body, name=name, grid=(M // tm, N // tn),
        in_specs=[pl.BlockSpec((tm, K), lambda i, j: (i, 0)), pl.BlockSpec((1, K), lambda i, j: (0, 0)),
                  pl.BlockSpec((K, tn), lambda i, j: (0, j))] + ex_specs + [ANY] * len(afters),
        out_specs=[pl.BlockSpec((tm, tn), lambda i, j: (i, j)) for _ in out_dtypes]
        + [pl.BlockSpec((tm, K), lambda i, j: (i, 0))],
        out_shape=[jax.ShapeDtypeStruct((M, N), dt) for dt in out_dtypes] + [jax.ShapeDtypeStruct((M, K), BF)],
        scratch_shapes=[pltpu.VMEM((tm, K), BF)],
        compiler_params=pltpu.CompilerParams(
            dimension_semantics=("parallel", "arbitrary"), vmem_limit_bytes=VMEM_LIMIT),
    )(x, g, b, *extras, *afters)
    return tuple(outs)


def matmul_sum(name, pairs, tm=512, tn=1024, after=None):
    M, N = pairs[0][0].shape[0], pairs[0][1].shape[1]
    tm, tn = _tile(M, tm), _tile(N, tn, LANE)
    afters = [] if after is None else [after]
    in_specs, operands = [], []
    for a, b in pairs:
        assert a.shape == (M, b.shape[0]) and b.shape[1] == N, (name, a.shape, b.shape)
        in_specs += [pl.BlockSpec((tm, a.shape[1]), lambda i, j: (i, 0)),
                     pl.BlockSpec((b.shape[0], tn), lambda i, j: (0, j))]
        operands += [a, b]
    n = len(pairs)

    def body(*refs):
        acc = _dg(refs[0][...], refs[1][...], 1, 0, False)
        for p in range(1, n):
            acc = acc + _dg(refs[2 * p][...], refs[2 * p + 1][...], 1, 0, False)
        refs[-1][...] = acc

    return pl.pallas_call(
        body, name=name, grid=(M // tm, N // tn),
        in_specs=in_specs + [ANY] * len(afters),
        out_specs=pl.BlockSpec((tm, tn), lambda i, j: (i, j)),
        out_shape=jax.ShapeDtypeStruct((M, N), F32),
        compiler_params=pltpu.CompilerParams(
            dimension_semantics=("parallel", "parallel"), vmem_limit_bytes=VMEM_LIMIT),
    )(*operands, *afters)


def rowcall(name, fn, rows, consts, out_rows, out_accs=(), tr=256, sp=None, R=None, after=None,
            out_lead=None, into=None):
    afters = ([] if after is None else [after]) + ([] if into is None else list(into))
    rows = [r if isinstance(r, tuple) else (r, None) for r in rows]
    R = rows[0][0].shape[-2] if R is None else R
    tr = _tile(R, tr)
    n = R // tr
    in_specs = []
    for arr, lead in rows:
        C = arr.shape[-1]
        if lead is None:
            assert arr.shape[-2] == R, (name, arr.shape, R)
            in_specs.append(pl.BlockSpec((tr, C), lambda i, *s: (i, 0)))
        elif callable(lead):
            in_specs.append(pl.BlockSpec((tr, C), lambda i, *s, lead=lead: (lead(i, n, *s), 0)))
        else:
            assert arr.shape[-2] == R, (name, arr.shape, R)
            in_specs.append(pl.BlockSpec((None, tr, C), lambda i, *s, lead=lead: (lead, i, 0)))
    for c in consts:
        in_specs.append(pl.BlockSpec(c.shape, lambda i, *s, nd=c.ndim: (0,) * nd))
    if out_lead is None:
        out_shape = [jax.ShapeDtypeStruct((R, C), dt) for C, dt in out_rows]
        out_specs = [pl.BlockSpec((tr, C), lambda i, *s: (i, 0)) for C, _ in out_rows]
    else:
        n_slab, slab = out_lead
        out_shape = [jax.ShapeDtypeStruct((n_slab, R, C), dt) for C, dt in out_rows]
        out_specs = [pl.BlockSpec((None, tr, C), lambda i, *s: (slab, i, 0)) for C, _ in out_rows]
    for shp in out_accs:
        out_shape.append(jax.ShapeDtypeStruct(shp, F32))
        out_specs.append(pl.BlockSpec(shp, lambda i, *s, nd=len(shp): (0,) * nd))
    n_in, n_row, n_acc = len(rows) + len(consts), len(out_rows), len(out_accs)
    n_sp = 0 if sp is None else 1

    def body(*refs):
        refs = refs[n_sp:]
        ins = [r[...] for r in refs[:n_in]]
        res = fn(*ins)
        if not isinstance(res, (tuple, list)):
            res = (res,)
        o_refs = refs[n_in + len(afters):]
        for o_ref, r in zip(o_refs[:n_row], res[:n_row]):
            o_ref[...] = r.astype(o_ref.dtype)
        if n_acc:
            first = pl.program_id(0) == 0
            for o_ref, r in zip(o_refs[n_row:], res[n_row:]):
                r = r.astype(F32).reshape(o_ref.shape)

                @pl.when(first)
                def _(o_ref=o_ref, r=r):
                    o_ref[...] = r

                @pl.when(jnp.logical_not(first))
                def _(o_ref=o_ref, r=r):
                    o_ref[...] += r

    params = pltpu.CompilerParams(dimension_semantics=("arbitrary",), vmem_limit_bytes=VMEM_LIMIT)
    operands = [a for a, _ in rows] + list(consts) + afters
    in_specs = in_specs + [ANY] * len(afters)
    aliases = {} if into is None else {n_sp + len(operands) - len(into) + k: k for k in range(len(into))}
    if sp is None:
        outs = pl.pallas_call(body, name=name, grid=(n,), in_specs=in_specs, out_specs=out_specs,
                              out_shape=out_shape, input_output_aliases=aliases,
                              compiler_params=params)(*operands)
    else:
        outs = pl.pallas_call(
            body, name=name, out_shape=out_shape, compiler_params=params, input_output_aliases=aliases,
            grid_spec=pltpu.PrefetchScalarGridSpec(
                num_scalar_prefetch=1, grid=(n,), in_specs=in_specs, out_specs=out_specs),
        )(sp, *operands)
    return outs[0] if len(outs) == 1 else tuple(outs)


def slabcall(name, fn, ins, out_dtypes, sp, n_out_slabs=None, cw=512):
    R, C = ins[0][0].shape[-2:]
    cw = _tile(C, cw, LANE)
    in_specs = []
    for arr, slab in ins:
        assert arr.shape[-2:] == (R, C), (name, arr.shape)
        if slab is None:
            in_specs.append(pl.BlockSpec((R, cw), lambda i, j, s: (0, j)))
        elif callable(slab):
            in_specs.append(pl.BlockSpec((None, R, cw), lambda i, j, s, slab=slab: (slab(i, s), 0, j)))
        else:
            in_specs.append(pl.BlockSpec((None, R, cw), lambda i, j, s, slab=slab: (slab, 0, j)))
    if n_out_slabs is None:
        out_shape = [jax.ShapeDtypeStruct((R, C), dt) for dt in out_dtypes]
        out_specs = [pl.BlockSpec((R, cw), lambda i, j, s: (0, j)) for _ in out_dtypes]
    else:
        out_shape = [jax.ShapeDtypeStruct((n_out_slabs, R, C), dt) for dt in out_dtypes]
        out_specs = [pl.BlockSpec((None, R, cw), lambda i, j, s: (i, 0, j)) for _ in out_dtypes]
    n_in = len(ins)

    def body(sp_ref, *refs):
        res = fn(*[r[...] for r in refs[:n_in]])
        for o_ref, r in zip(refs[n_in:], res):
            o_ref[...] = r.astype(o_ref.dtype)

    outs = pl.pallas_call(
        body, name=name, out_shape=out_shape,
        grid_spec=pltpu.PrefetchScalarGridSpec(
            num_scalar_prefetch=1, grid=(n_out_slabs or 1, C // cw), in_specs=in_specs, out_specs=out_specs),
        compiler_params=pltpu.CompilerParams(
            dimension_semantics=("arbitrary", "arbitrary"), vmem_limit_bytes=VMEM_LIMIT),
    )(sp, *[a for a, _ in ins])
    return outs[0] if len(outs) == 1 else tuple(outs)


def rms_tile(x, g):
    x = x.astype(F32)
    return x * lax.rsqrt(jnp.mean(x * x, axis=-1, keepdims=True) + EPS) * g


def gelu(x):
    return 0.5 * x * (1.0 + lax.erf(x * (1.0 / math.sqrt(2.0))))


def ln_tile(x, g, b):
    mu = jnp.mean(x, axis=-1, keepdims=True)
    xc = x - mu
    return xc * lax.rsqrt(jnp.mean(xc * xc, axis=-1, keepdims=True) + EPS) * g + b


def lane_groups(fn, width, *arrs):
    n = arrs[0].shape[-1] // width
    outs = [fn(*[a[:, i * width:(i + 1) * width] for a in arrs]) for i in range(n)]
    return jnp.concatenate(outs, axis=-1)


def mixa_post_tile(o, z, o_norm):
    dv = o_norm.shape[-1]
    on = lane_groups(lambda t: rms_tile(t, o_norm), dv, o)
    return on * jax.nn.silu(z)


def mixb_tile(uv, ln_g, ln_b, w_s, bs_t):
    G = w_s.shape[0]
    gw = ln_g.shape[-1]
    dg = gw // G
    tr = uv.shape[0]
    u = gelu(uv[:, :gw])
    vg = gelu(uv[:, gw:])
    ii = lax.broadcasted_iota(jnp.int32, (B_BLOCK, B_BLOCK), 0)
    jj = lax.broadcasted_iota(jnp.int32, (B_BLOCK, B_BLOCK), 1)
    mask = (jj // CHUNK) <= (ii // CHUNK)
    cols = []
    for g in range(G):
        sl = slice(g * dg, (g + 1) * dg)
        vn = ln_tile(vg[:, sl], ln_g[:, sl], ln_b[:, sl])
        wm = jnp.where(mask, w_s[g], 0.0)
        blocks = []
        for m in range(tr // B_BLOCK):
            blk = vn[m * B_BLOCK:(m + 1) * B_BLOCK, :]
            blocks.append(mm(wm, blk, 1, 0) + bs_t[:, g:g + 1])
        mixed = blocks[0] if len(blocks) == 1 else jnp.concatenate(blocks, axis=0)
        cols.append(u[:, sl] * mixed)
    return jnp.concatenate(cols, axis=-1)


def bgc_tile(ba, alog_row, dtb_row):
    tr = ba.shape[0]
    beta = jax.nn.sigmoid(ba)
    g = -jnp.exp(alog_row) * jax.nn.softplus(ba + dtb_row)
    ii = lax.broadcasted_iota(jnp.int32, (tr, tr), 0)
    jj = lax.broadcasted_iota(jnp.int32, (tr, tr), 1)
    tri = jnp.where((ii // CHUNK == jj // CHUNK) & (jj <= ii), 1.0, 0.0).astype(F32)
    gc = mm(tri, g, 1, 0, True)
    return beta, gc


def make_bgc(H):
    def f(ba, alog_row, dtb_row):
        beta, gc = bgc_tile(ba, alog_row, dtb_row)
        lane = lax.broadcasted_iota(jnp.int32, ba.shape, 1)
        return jnp.where(lane < H, beta, jnp.where(lane < 2 * H, gc, 0.0))
    return f


def loss_tile(x, g, target):
    y = rms_tile(x, g)
    err = y - target
    return 0.5 * jnp.sum(jnp.mean(err * err, axis=-1, keepdims=True), axis=0, keepdims=True)


def adamw_tile(w, g, m, v):
    m = ADAM_B1 * m + (1.0 - ADAM_B1) * g
    v = ADAM_B2 * v + (1.0 - ADAM_B2) * (g * g)
    m_hat = m / (1.0 - ADAM_B1 ** ADAM_STEP)
    v_hat = v / (1.0 - ADAM_B2 ** ADAM_STEP)
    delta = -ADAM_LR * (m_hat / (jnp.sqrt(v_hat) + ADAM_EPS) + ADAM_WD * w)
    return delta, m, v


CONV_ROWS = 32


def _shifted_copies(src, sh, rows):
    for b in range(SUBLANE):
        sh[b] = src[pl.ds(b, rows), :]


def _window(sh, off, rows):
    b = off % SUBLANE
    return sh[b, pl.ds(off - b, rows), :]


def _conv_rows(out, sh, w_ref, offsets, rows):
    for r0 in range(0, rows, CONV_ROWS):
        rc = min(CONV_ROWS, rows - r0)
        acc = w_ref[0:1, :] * _window(sh, offsets[0] + r0, rc)
        for k in range(1, len(offsets)):
            acc = acc + w_ref[k:k + 1, :] * _window(sh, offsets[k] + r0, rc)
        out[r0:r0 + rc, :] = acc


def _conv_wgrad(dsrc, d0, sh, offsets, rows):
    dws = []
    for off in offsets:
        acc = None
        for r0 in range(0, rows, CONV_ROWS):
            rc = min(CONV_ROWS, rows - r0)
            prod = dsrc[d0 + r0:d0 + r0 + rc, :] * _window(sh, off + r0, rc)
            for g in range(0, rc, SUBLANE):
                part = prod[g:g + SUBLANE, :]
                acc = part if acc is None else acc + part
        dws.append(jnp.sum(acc, axis=0, keepdims=True))
    return jnp.concatenate(dws, axis=0)


def _conv_specs(T, tr, hb, cw, col_blocks, rev):
    n = T // tr

    def ri(i):
        return (n - 1 - i) if rev else i

    tile_specs, halo_specs = [], []
    for off in col_blocks:
        tile_specs.append(pl.BlockSpec((tr, cw), lambda j, i, off=off: (ri(i), j + off)))
        halo_specs.append(pl.BlockSpec(
            (hb, cw), lambda j, i, off=off: (jnp.maximum(ri(i) * (tr // hb) - 1, 0), j + off)))
    return n, ri, tile_specs, halo_specs


def conv_fwd(name, x, w, consts, pre, post, col_blocks, n_out, K, out_dtype=F32, tr=256, hb=8, cw=512):
    T = x.shape[0]
    C = w.shape[1]
    tr, cw = _tile(T, tr, hb), min(cw, C)
    nb = len(col_blocks)
    n, ri, tile_specs, halo_specs = _conv_specs(T, tr, hb, cw, col_blocks, False)
    w_spec = pl.BlockSpec((K, cw), lambda j, i: (0, j))
    c_specs = [pl.BlockSpec((1, cw), lambda j, i: (0, j)) for _ in consts]

    def body(*refs):
        tiles = [r[...] for r in refs[:nb]]
        halos = [r[...] for r in refs[nb:2 * nb]]
        w_ref = refs[2 * nb]
        cs = [r[...] for r in refs[2 * nb + 1:2 * nb + 1 + len(consts)]]
        o_refs = refs[2 * nb + 1 + len(consts):-3]
        pbuf, shp, cbuf = refs[-3:]
        i = pl.program_id(1)
        pbuf[0:hb, :] = jnp.where(i > 0, pre(*halos), 0.0)
        pbuf[hb:hb + tr, :] = pre(*tiles)
        pbuf[hb + tr:hb + tr + SUBLANE, :] = jnp.zeros((SUBLANE, cw), F32)
        _shifted_copies(pbuf, shp, hb + tr)
        _conv_rows(cbuf, shp, w_ref, [hb - (K - 1) + k for k in range(K)], tr)
        res = post(cbuf[...], pl.program_id(0), *cs)
        for o_ref, r in zip(o_refs, res):
            o_ref[...] = r.astype(o_ref.dtype)

    outs = pl.pallas_call(
        body, name=name, grid=(C // cw, n),
        in_specs=tile_specs + halo_specs + [w_spec] + c_specs,
        out_specs=[pl.BlockSpec((tr, cw), lambda j, i: (i, j)) for _ in range(n_out)],
        out_shape=[jax.ShapeDtypeStruct((T, C), out_dtype) for _ in range(n_out)],
        scratch_shapes=[pltpu.VMEM((hb + tr + SUBLANE, cw), F32), pltpu.VMEM((SUBLANE, hb + tr, cw), F32),
                        pltpu.VMEM((tr, cw), F32)],
        compiler_params=pltpu.CompilerParams(
            dimension_semantics=("parallel", "arbitrary"), vmem_limit_bytes=VMEM_LIMIT),
    )(*([x] * nb), *([x] * nb), w, *consts)
    return outs[0] if n_out == 1 else tuple(outs)


def conv_bwd(name, x, w, consts, grads, pre, post, col_blocks, K, tr=256, hb=8, cw=512, recompute=True):
    T, Cx = x.shape
    C = w.shape[1]
    tr, cw = _tile(T, tr, hb), min(cw, C)
    nb = len(col_blocks)
    n, ri, tile_specs, halo_specs = _conv_specs(T, tr, hb, cw, col_blocks, True)
    w_spec = pl.BlockSpec((K, cw), lambda j, i: (0, j))
    c_specs = [pl.BlockSpec((1, cw), lambda j, i: (0, j)) for _ in consts]
    g_specs = [pl.BlockSpec((tr, cw), lambda j, i: (ri(i), j)) for _ in grads]
    nc, ng = len(consts), len(grads)

    def body(*refs):
        p = 0
        tile_refs = refs[p:p + nb]; p += nb
        halo_refs = refs[p:p + nb]; p += nb
        w_ref = refs[p]; p += 1
        cs = [r[...] for r in refs[p:p + nc]]; p += nc
        gs = [r[...] for r in refs[p:p + ng]]; p += ng
        dx_refs = refs[p:p + nb]; p += nb
        sum_refs = refs[p:p + nb]; p += nb
        dw_ref = refs[p]; p += 1
        dc_refs = refs[p:p + nc]; p += nc
        pbuf, dbuf, ebuf, carry, shp, shd, cbuf = refs[p:p + 7]
        i = pl.program_id(1)
        first = i == 0
        tiles = [r[...] for r in tile_refs]
        halos = [r[...] for r in halo_refs]
        p_tile, vjp_pre = jax.vjp(pre, *tiles)
        pbuf[0:hb, :] = jnp.where(ri(i) > 0, pre(*halos), 0.0)
        pbuf[hb:hb + tr, :] = p_tile
        pbuf[hb + tr:hb + tr + SUBLANE, :] = jnp.zeros((SUBLANE, cw), F32)
        _shifted_copies(pbuf, shp, hb + tr)
        taps = [hb - (K - 1) + k for k in range(K)]
        if recompute:
            _conv_rows(cbuf, shp, w_ref, taps, tr)
            c = cbuf[...]
        else:
            c = jnp.zeros((tr, cw), F32)
        cid = pl.program_id(0)
        _, vjp_post = jax.vjp(lambda c_, *cs_: post(c_, cid, *cs_), c, *cs)
        dres = vjp_post(tuple(g.astype(F32) for g in gs))
        dbuf[0:hb, :] = jnp.zeros((hb, cw), F32)
        dbuf[hb:hb + tr, :] = dres[0]
        dbuf[hb + tr:hb + tr + hb + SUBLANE, :] = jnp.zeros((hb + SUBLANE, cw), F32)
        _shifted_copies(dbuf, shd, hb + tr + hb)
        _conv_rows(ebuf, shd, w_ref, [K - 1 - k for k in range(K)], hb + tr)
        dw = _conv_wgrad(dbuf, hb, shp, taps, tr)

        @pl.when(jnp.logical_not(first))
        def _():
            ebuf[tr:tr + hb, :] += carry[...]

        carry[...] = ebuf[0:hb, :]
        dtiles = vjp_pre(ebuf[hb:hb + tr, :])
        for r, s, d in zip(dx_refs, sum_refs, dtiles):
            r[...] = d.astype(r.dtype)
            ds_ = jnp.sum(d, axis=0, keepdims=True)

            @pl.when(first)
            def _(s=s, ds_=ds_):
                s[...] = ds_

            @pl.when(jnp.logical_not(first))
            def _(s=s, ds_=ds_):
                s[...] += ds_

        accs = [(dw_ref, dw)] + [(r, d) for r, d in zip(dc_refs, dres[1:])]
        for r, d in accs:
            @pl.when(first)
            def _(r=r, d=d):
                r[...] = d

            @pl.when(jnp.logical_not(first))
            def _(r=r, d=d):
                r[...] += d

    n_cb = C // cw
    outs = pl.pallas_call(
        body, name=name, grid=(n_cb, n),
        in_specs=tile_specs + halo_specs + [w_spec] + c_specs + g_specs,
        out_specs=([pl.BlockSpec((tr, cw), lambda j, i: (ri(i), j)) for _ in col_blocks]
                   + [pl.BlockSpec((1, cw), lambda j, i: (0, j)) for _ in col_blocks]
                   + [pl.BlockSpec((K, cw), lambda j, i: (0, j))]
                   + [pl.BlockSpec((1, cw), lambda j, i: (0, j)) for _ in consts]),
        out_shape=([jax.ShapeDtypeStruct((T, C), BF) for _ in col_blocks]
                   + [jax.ShapeDtypeStruct((1, C), F32) for _ in col_blocks]
                   + [jax.ShapeDtypeStruct((K, C), F32)]
                   + [jax.ShapeDtypeStruct((1, C), F32) for _ in consts]),
        scratch_shapes=[pltpu.VMEM((hb + tr + SUBLANE, cw), F32), pltpu.VMEM((hb + tr + hb + SUBLANE, cw), F32),
                        pltpu.VMEM((hb + tr, cw), F32), pltpu.VMEM((hb, cw), F32),
                        pltpu.VMEM((SUBLANE, hb + tr, cw), F32), pltpu.VMEM((SUBLANE, hb + tr + hb, cw), F32),
                        pltpu.VMEM((tr, cw), F32)],
        compiler_params=pltpu.CompilerParams(
            dimension_semantics=("parallel", "arbitrary"), vmem_limit_bytes=VMEM_LIMIT),
    )(*([x] * nb), *([x] * nb), w, *consts, *grads)
    dxs = outs[:nb]
    sums = outs[nb:2 * nb]
    dw = outs[2 * nb]
    dcs = outs[2 * nb + 1:]
    return dxs, sums, dw, dcs


def make_qkv_post(dk, cw, n_qk_chunks):
    def l2(t):
        return t * lax.rsqrt(jnp.sum(t * t, axis=-1, keepdims=True) + EPS)

    def post(c, cid):
        s = jax.nn.silu(c)
        normed = lane_groups(l2, dk, s)
        return (jnp.where(cid < n_qk_chunks, normed, s),)
    return post


def glu_pre(za, zb):
    return za * jax.nn.sigmoid(zb)


def bias_post(c, cid, b):
    return (c + b,)


def _col_to_row(col):
    C = col.shape[-2]
    ii = lax.broadcasted_iota(jnp.int32, (C, C), 0)
    jj = lax.broadcasted_iota(jnp.int32, (C, C), 1)
    wide = jnp.broadcast_to(col, col.shape[:-1] + (C,))
    return jnp.sum(jnp.where(ii == jj, wide, 0.0), axis=-2, keepdims=True)


@jax.custom_vjp
def solve_with_inverse(a, rhs, x):
    return mm(x, rhs, 1, 0, THREE_PASS)


def _swi_fwd(a, rhs, x):
    sol = mm(x, rhs, 1, 0, THREE_PASS)
    return sol, (x, sol)


def _swi_bwd(res, dsol):
    x, sol = res
    drhs = mm(x, dsol, 0, 0, THREE_PASS)
    da = -mm(drhs, sol, 1, 1, THREE_PASS)
    return da, drhs, jnp.zeros_like(x)


solve_with_inverse.defvjp(_swi_fwd, _swi_bwd)


def unit_lower_inverse(a):
    C = a.shape[-1]
    ii = lax.broadcasted_iota(jnp.int32, (C, C), 0)
    jj = lax.broadcasted_iota(jnp.int32, (C, C), 1)
    x = jnp.where(ii == jj, 1.0, 0.0).astype(F32) - a
    p = mm(a, a, 1, 0, THREE_PASS)
    steps = int(math.log2(C)) - 1
    for s in range(steps):
        x = x + mm(x, p, 1, 0, THREE_PASS)
        if s < steps - 1:
            p = mm(p, p, 1, 0, THREE_PASS)
    return x


def dn_masks(C):
    ii = lax.broadcasted_iota(jnp.int32, (C, C), 0)
    jj = lax.broadcasted_iota(jnp.int32, (C, C), 1)
    return ii >= jj, ii > jj


def dn_pre(q, k, v, beta, gc):
    C, dk = q.shape[-2:]
    tri, strict = dn_masks(C)
    q = q * (dk ** -0.5)
    diff = gc - _col_to_row(gc)
    decay = jnp.where(tri, jnp.exp(jnp.where(tri, diff, 0.0)), 0.0)
    kb = k * beta
    vb = v * beta
    a = jnp.where(strict, mm(kb, k, 1, 1) * decay, 0.0)
    eg = jnp.exp(gc)
    rhs = jnp.concatenate([vb, kb * eg], axis=-1)
    attn = mm(q, k, 1, 1) * decay
    qd = q * eg
    g_last = gc[..., C - 1:C, :]
    kt = k * jnp.exp(g_last - gc)
    gl = jnp.exp(g_last)
    return a, rhs, attn, qd, kt, gl


def dn_chunk(q, k, v, beta, gc, state, x):
    dv = v.shape[-1]
    a, rhs, attn, qd, kt, gl = dn_pre(q, k, v, beta, gc)
    sol = solve_with_inverse(a, rhs, x)
    u, w = sol[..., :dv], sol[..., dv:]
    vn = u - mm(w, state, 1, 0)
    o = mm(qd, state, 1, 0) + mm(attn, vn, 1, 0)
    new_state = state * gl + mm(kt, vn, 0, 0)
    return o, new_state


def _by_head(q_ref, k_ref, v_ref, bg, H, dk, dv):
    qv = jnp.stack([q_ref[:, h * dk:(h + 1) * dk] for h in range(H)])
    kv = jnp.stack([k_ref[:, h * dk:(h + 1) * dk] for h in range(H)])
    vv = jnp.stack([v_ref[:, h * dv:(h + 1) * dv] for h in range(H)])
    beta = jnp.stack([bg[:, h:h + 1] for h in range(H)])
    gc = jnp.stack([bg[:, H + h:H + h + 1] for h in range(H)])
    return qv, kv, vv, beta, gc


def deltanet_fwd(qkv, bgc, H):
    T = qkv.shape[0]
    dk = dv = qkv.shape[1] // (3 * H)
    N = T // CHUNK

    def body(q_ref, k_ref, v_ref, bgc_ref, o_ref, x_ref, s_ref, state):
        @pl.when(pl.program_id(0) == 0)
        def _():
            state[...] = jnp.zeros((H, dk, dv), F32)

        qv, kv, vv, beta, gc = _by_head(q_ref, k_ref, v_ref, bgc_ref[...], H, dk, dv)
        a = dn_pre(qv, kv, vv, beta, gc)[0]
        x = unit_lower_inverse(a)
        s = state[...]
        o, s_new = dn_chunk(qv, kv, vv, beta, gc, s, x)
        for h in range(H):
            o_ref[:, h * dv:(h + 1) * dv] = o[h]
        x_ref[...] = x
        s_ref[...] = s
        state[...] = s_new

    return pl.pallas_call(
        body, name="deltanet_fwd", grid=(N,),
        in_specs=[pl.BlockSpec((CHUNK, H * dk), lambda n: (n, 0)),
                  pl.BlockSpec((CHUNK, H * dk), lambda n: (n, 1)),
                  pl.BlockSpec((CHUNK, H * dv), lambda n: (n, 2)),
                  pl.BlockSpec((CHUNK, LANE), lambda n: (n, 0))],
        out_specs=[pl.BlockSpec((CHUNK, H * dv), lambda n: (n, 0)),
                   pl.BlockSpec((None, H, CHUNK, CHUNK), lambda n: (n, 0, 0, 0)),
                   pl.BlockSpec((None, H, dk, dv), lambda n: (n, 0, 0, 0))],
        out_shape=[jax.ShapeDtypeStruct((T, H * dv), F32),
                   jax.ShapeDtypeStruct((N, H, CHUNK, CHUNK), F32),
                   jax.ShapeDtypeStruct((N, H, dk, dv), F32)],
        scratch_shapes=[pltpu.VMEM((H, dk, dv), F32)],
        compiler_params=pltpu.CompilerParams(
            dimension_semantics=("arbitrary",), vmem_limit_bytes=VMEM_LIMIT),
    )(qkv, qkv, qkv, bgc)


def deltanet_bwd(qkv, bgc, xinv, states, do, H):
    T = qkv.shape[0]
    dk = dv = qkv.shape[1] // (3 * H)
    N = T // CHUNK

    def body(q_ref, k_ref, v_ref, bgc_ref, x_ref, s_ref, do_ref, dqkv_ref, dbgc_ref, dstate):
        @pl.when(pl.program_id(0) == 0)
        def _():
            dstate[...] = jnp.zeros((H, dk, dv), F32)

        qv, kv, vv, beta, gc = _by_head(q_ref, k_ref, v_ref, bgc_ref[...], H, dk, dv)
        do = jnp.stack([do_ref[:, h * dv:(h + 1) * dv] for h in range(H)])
        _, vjp = jax.vjp(dn_chunk, qv, kv, vv, beta, gc, s_ref[...], x_ref[...])
        dq, dk_, dv_, dbeta, dgc, ds, _ = vjp((do, dstate[...]))
        dstate[...] = ds
        lane = lax.broadcasted_iota(jnp.int32, (CHUNK, LANE), 1)
        dbgc = jnp.zeros((CHUNK, LANE), F32)
        for h in range(H):
            dqkv_ref[:, h * dk:(h + 1) * dk] = dq[h]
            dqkv_ref[:, (H + h) * dk:(H + h + 1) * dk] = dk_[h]
            dqkv_ref[:, (2 * H + h) * dk:(2 * H + h + 1) * dk] = dv_[h]
            dbgc = dbgc + jnp.where(lane == h, dbeta[h], 0.0) + jnp.where(lane == h + H, dgc[h], 0.0)
        dbgc_ref[...] = dbgc

    rn = lambda n: N - 1 - n
    return pl.pallas_call(
        body, name="deltanet_bwd", grid=(N,),
        in_specs=[pl.BlockSpec((CHUNK, H * dk), lambda n: (rn(n), 0)),
                  pl.BlockSpec((CHUNK, H * dk), lambda n: (rn(n), 1)),
                  pl.BlockSpec((CHUNK, H * dv), lambda n: (rn(n), 2)),
                  pl.BlockSpec((CHUNK, LANE), lambda n: (rn(n), 0)),
                  pl.BlockSpec((None, H, CHUNK, CHUNK), lambda n: (rn(n), 0, 0, 0)),
                  pl.BlockSpec((None, H, dk, dv), lambda n: (rn(n), 0, 0, 0)),
                  pl.BlockSpec((CHUNK, H * dv), lambda n: (rn(n), 0))],
        out_specs=[pl.BlockSpec((CHUNK, 3 * H * dk), lambda n: (rn(n), 0)),
                   pl.BlockSpec((CHUNK, LANE), lambda n: (rn(n), 0))],
        out_shape=[jax.ShapeDtypeStruct((T, 3 * H * dk), F32),
                   jax.ShapeDtypeStruct((T, LANE), F32)],
        scratch_shapes=[pltpu.VMEM((H, dk, dv), F32)],
        compiler_params=pltpu.CompilerParams(
            dimension_semantics=("arbitrary",), vmem_limit_bytes=VMEM_LIMIT),
    )(qkv, qkv, qkv, bgc, xinv, states, do)


def _place():
    x, y, c = lax.axis_index("x"), lax.axis_index("y"), lax.axis_index("c")
    chips = [(1 - x, y), (x, 1 - y), (1 - x, 1 - y)]
    return x, y, c, chips


def all_gather(name, shards):
    na = len(shards)

    def body(*refs):
        ins, outs = refs[:na], refs[na:2 * na]
        send_sems, recv_sems, local_sems = refs[2 * na:]
        x, y, c, chips = _place()
        me, sibling = (x, y, c), (x, y, 1 - c)

        def copy(a, k, block, to, src=None):
            dst = outs[a].at[4 * block[0] + 2 * block[1] + block[2]]
            return pltpu.make_async_remote_copy(
                src_ref=dst if src is None else src, dst_ref=dst,
                send_sem=send_sems.at[a, k], recv_sem=recv_sems.at[a, k],
                device_id=to, device_id_type=MESH)

        mine, first, passed = [], [], []
        for a in range(na):
            cp = pltpu.make_async_copy(ins[a], outs[a].at[4 * x + 2 * y + c], local_sems.at[a])
            cp.start()
            mine.append(cp)
        for a in range(na):
            cps = [copy(a, 0, me, sibling, src=ins[a])]
            cps += [copy(a, 1 + j, me, (*chip, c), src=ins[a]) for j, chip in enumerate(chips)]
            for cp in cps:
                cp.start()
            first += cps
        for a in range(na):
            for j, chip in enumerate(chips):
                copy(a, 1 + j, (*chip, c), me).wait_recv()
                cp = copy(a, 4 + j, (*chip, c), sibling)
                cp.start()
                passed.append(cp)
        for a in range(na):
            copy(a, 0, sibling, me).wait_recv()
            for j, chip in enumerate(chips):
                copy(a, 4 + j, (*chip, 1 - c), me).wait_recv()
        for cp in first + passed:
            cp.wait_send()
        for cp in mine:
            cp.wait()

    outs = pl.pallas_call(
        body, name=name,
        in_specs=[ANY] * na, out_specs=[ANY] * na,
        out_shape=[jax.ShapeDtypeStruct((N_DEV,) + s.shape, s.dtype) for s in shards],
        scratch_shapes=[pltpu.SemaphoreType.DMA((na, 7)), pltpu.SemaphoreType.DMA((na, 7)),
                        pltpu.SemaphoreType.DMA((na,))],
    )(*shards)
    return list(outs)


HBM_SPEC = pl.BlockSpec(memory_space=pltpu.HBM)
SEM_SPEC = pl.BlockSpec(memory_space=pltpu.SEMAPHORE)
EFFECT = pltpu.SideEffectType.DATAFLOW_SIDE_EFFECTING


def _descriptors(plan, bufs, send_sems, recv_sems):
    return [pltpu.make_async_remote_copy(src_ref=src, dst_ref=dst, send_sem=send_sems.at[k],
                                         recv_sem=recv_sems.at[k], device_id=dev, device_id_type=MESH)
            for k, (src, dst, dev) in enumerate(plan(bufs))]


def split_start(name, bufs, plan, n, after):
    nb = len(bufs)

    def body(*refs):
        for cp in _descriptors(plan, refs[:nb], refs[nb + 1], refs[nb + 2]):
            cp.start()
        refs[-1][...] = jnp.zeros((SUBLANE, LANE), F32)

    outs = pl.pallas_call(
        body, name=name,
        out_shape=(pltpu.SemaphoreType.DMA((n,)), pltpu.SemaphoreType.DMA((n,)),
                   *[pltpu.HBM(b.shape, b.dtype) for b in bufs],
                   jax.ShapeDtypeStruct((SUBLANE, LANE), F32)),
        in_specs=[HBM_SPEC] * nb + [ANY],
        out_specs=(SEM_SPEC, SEM_SPEC, *[HBM_SPEC] * nb, pl.BlockSpec(memory_space=pltpu.VMEM)),
        input_output_aliases={i: 2 + i for i in range(nb)},
        compiler_params=pltpu.CompilerParams(has_side_effects=EFFECT),
    )(*[pltpu.with_memory_space_constraint(b, pltpu.HBM) for b in bufs], after)
    return outs[0], outs[1], list(outs[2:2 + nb]), outs[-1]


def split_wait(name, send_sems, recv_sems, bufs, plan, after):
    nb = len(bufs)

    def body(*refs):
        cps = _descriptors(plan, refs[:nb], refs[nb], refs[nb + 1])
        for cp in cps:
            cp.wait_recv()
        for cp in cps:
            cp.wait_send()
        refs[-1][...] = jnp.zeros((SUBLANE, LANE), F32)

    afters = list(after) if isinstance(after, (list, tuple)) else [after]
    outs = pl.pallas_call(
        body, name=name,
        out_shape=[pltpu.HBM(b.shape, b.dtype) for b in bufs] + [jax.ShapeDtypeStruct((SUBLANE, LANE), F32)],
        in_specs=[HBM_SPEC] * nb + [SEM_SPEC, SEM_SPEC] + [ANY] * len(afters),
        out_specs=[HBM_SPEC] * nb + [pl.BlockSpec(memory_space=pltpu.VMEM)],
        input_output_aliases={i: i for i in range(nb)},
        compiler_params=pltpu.CompilerParams(has_side_effects=EFFECT),
    )(*bufs, send_sems, recv_sems, *afters)
    return list(outs[:nb]), outs[-1]


def _block(px, py, pc):
    return 4 * px + 2 * py + pc


def plan_gather_ici(na):
    def plan(bufs):
        x, y, c, chips = _place()
        out = []
        for a in range(na):
            dst = bufs[na + a].at[_block(x, y, c)]
            out.append((bufs[a], dst, (x, y, 1 - c)))
            out += [(bufs[a], dst, (px, py, c)) for px, py in chips]
        return out
    return plan


def plan_gather_pass(na):
    def plan(bufs):
        x, y, c, chips = _place()
        out = []
        for a in range(na):
            for px, py in chips:
                blk = bufs[a].at[_block(px, py, c)]
                out.append((blk, blk, (x, y, 1 - c)))
        return out
    return plan


def plan_reduce_d2d(na):
    def plan(bufs):
        x, y, c, _ = _place()
        return [(bufs[a].at[2 * s + (1 - c)], bufs[na + a].at[s], (x, y, 1 - c))
                for a in range(na) for s in range(4)]
    return plan


def plan_reduce_ici(na):
    def plan(bufs):
        x, y, c, chips = _place()
        return [(bufs[a].at[2 * px + py], bufs[na + a].at[j], (px, py, c))
                for a in range(na) for j, (px, py) in enumerate(chips)]
    return plan


def place_own(name, land, shard, dev):
    r, c = shard.shape
    tr = _tile(r, 512)

    def body(sp_ref, s_ref, land_ref, o_ref):
        o_ref[...] = s_ref[...]

    return pl.pallas_call(
        body, name=name, out_shape=jax.ShapeDtypeStruct(land.shape, land.dtype),
        grid_spec=pltpu.PrefetchScalarGridSpec(
            num_scalar_prefetch=1, grid=(r // tr,),
            in_specs=[pl.BlockSpec((tr, c), lambda i, s: (i, 0)), ANY],
            out_specs=pl.BlockSpec((None, tr, c), lambda i, s: (s[0], i, 0))),
        input_output_aliases={2: 0},
        compiler_params=pltpu.CompilerParams(dimension_semantics=("arbitrary",)),
    )(dev, shard, land)


def pack(arrs, row_mult=SUBLANE):
    pieces = []
    for a in arrs:
        f = a.reshape(-1).astype(F32)
        pad = (-f.shape[0]) % LANE
        if pad:
            f = jnp.concatenate([f, jnp.zeros((pad,), F32)])
        pieces.append(f)
    flat = jnp.concatenate(pieces)
    rows = flat.shape[0] // LANE
    pad_rows = (-rows) % row_mult
    if pad_rows:
        flat = jnp.concatenate([flat, jnp.zeros((pad_rows * LANE,), F32)])
    return flat.reshape(-1, LANE)


def unpack(buf, shapes):
    flat = buf.reshape(-1)
    outs, off = [], 0
    for shp in shapes:
        n = int(np.prod(shp))
        outs.append(flat[off:off + n].reshape(shp))
        off += n + ((-n) % LANE)
    return outs


def _vjp_rows(fn, n_row_in, n_cot):
    def bwd(*args):
        rows = args[:n_row_in]
        cots = args[n_row_in:n_row_in + n_cot]
        consts = args[n_row_in + n_cot:]
        out, vjp = jax.vjp(fn, *rows, *consts)
        if isinstance(out, (tuple, list)):
            cot = tuple(c.astype(o.dtype) for c, o in zip(cots, out))
        else:
            cot = cots[0].astype(out.dtype)
        return vjp(cot)
    return bwd


def rms_bwd(name, x, g, dh, dres, after=None):
    D = x.shape[1]
    vj = _vjp_rows(rms_tile, 1, 1)

    def f(x_, dh_, dres_, g_):
        dx, dg = vj(x_, dh_, g_)
        dx = dx + dres_
        return dx, dx, dg
    return rowcall(name, f, [x, dh, dres], [g], [(D, F32), (D, BF)], [(1, D)], after=after)


def kernel(x, e_norm, e_w_in, e_conv_w, e_a_log, e_dt_bias, e_o_norm, e_ln_g, e_ln_b, e_w_s, e_b_s, e_w_out, o_norm, o_pw1, o_pw1_b, o_dw, o_dw_b, o_ln_g, o_ln_b, o_pw2, o_pw2_b, f_norm, f_w1, f_w2, final_norm, loss_target, m_e_norm, m_e_w_in, m_e_conv_w, m_e_a_log, m_e_dt_bias, m_e_o_norm, m_e_ln_g, m_e_ln_b, m_e_w_s, m_e_b_s, m_e_w_out, m_o_norm, m_o_pw1, m_o_pw1_b, m_o_dw, m_o_dw_b, m_o_ln_g, m_o_ln_b, m_o_pw2, m_o_pw2_b, m_f_norm, m_f_w1, m_f_w2, m_final_norm, v_e_norm, v_e_w_in, v_e_conv_w, v_e_a_log, v_e_dt_bias, v_e_o_norm, v_e_ln_g, v_e_ln_b, v_e_w_s, v_e_b_s, v_e_w_out, v_o_norm, v_o_pw1, v_o_pw1_b, v_o_dw, v_o_dw_b, v_o_ln_g, v_o_ln_b, v_o_pw2, v_o_pw2_b, v_f_norm, v_f_w1, v_f_w2, v_final_norm):
    names = ['e_norm', 'e_w_in', 'e_conv_w', 'e_a_log', 'e_dt_bias', 'e_o_norm', 'e_ln_g', 'e_ln_b', 'e_w_s', 'e_b_s', 'e_w_out', 'o_norm', 'o_pw1', 'o_pw1_b', 'o_dw', 'o_dw_b', 'o_ln_g', 'o_ln_b', 'o_pw2', 'o_pw2_b', 'f_norm', 'f_w1', 'f_w2', 'final_norm']
    W = dict(zip(names, [e_norm, e_w_in, e_conv_w, e_a_log, e_dt_bias, e_o_norm, e_ln_g, e_ln_b, e_w_s, e_b_s, e_w_out, o_norm, o_pw1, o_pw1_b, o_dw, o_dw_b, o_ln_g, o_ln_b, o_pw2, o_pw2_b, f_norm, f_w1, f_w2, final_norm]))
    Mo = dict(zip(names, [m_e_norm, m_e_w_in, m_e_conv_w, m_e_a_log, m_e_dt_bias, m_e_o_norm, m_e_ln_g, m_e_ln_b, m_e_w_s, m_e_b_s, m_e_w_out, m_o_norm, m_o_pw1, m_o_pw1_b, m_o_dw, m_o_dw_b, m_o_ln_g, m_o_ln_b, m_o_pw2, m_o_pw2_b, m_f_norm, m_f_w1, m_f_w2, m_final_norm]))
    Vo = dict(zip(names, [v_e_norm, v_e_w_in, v_e_conv_w, v_e_a_log, v_e_dt_bias, v_e_o_norm, v_e_ln_g, v_e_ln_b, v_e_w_s, v_e_b_s, v_e_w_out, v_o_norm, v_o_pw1, v_o_pw1_b, v_o_dw, v_o_dw_b, v_o_ln_g, v_o_ln_b, v_o_pw2, v_o_pw2_b, v_f_norm, v_f_w1, v_f_w2, v_final_norm]))

    T, D = x.shape[1], x.shape[2]
    H = e_a_log.shape[-1]
    dv = e_o_norm.shape[-1]
    dk = dv
    G = e_w_s.shape[1]
    AQK, AV, BW = H * dk, H * dv, e_ln_g.shape[-1]
    AQKV = 2 * AQK + AV
    in_cols = AQKV + AV + 2 * H + 2 * BW
    KA = e_conv_w.shape[1]
    KC = o_dw.shape[1]
    L = f_norm.shape[0]
    dev = 4 * lax.axis_index("x") + 2 * lax.axis_index("y") + lax.axis_index("c")
    x2d = x.reshape(T, D)
    tgt = loss_target.reshape(T, D)

    def tie(small, tok):
        return small + tok[0:1, 0:1]

    dev_sp = dev.astype(jnp.int32).reshape(1)
    where = jnp.stack([lax.axis_index("c"), 2 * lax.axis_index("x") + lax.axis_index("y")]).astype(jnp.int32)
    row = lambda a: a.reshape(1, -1).astype(F32)
    en_row = row(e_norm)

    def gather_begin(tag, shards, after):
        na = len(shards)
        lands = [lax.empty((N_DEV,) + s.shape, s.dtype) for s in shards]
        ss, rs, bufs, tok = split_start(f"gather{tag}_ici_start", shards + lands, plan_gather_ici(na), 4 * na, after)
        return (na, ss, rs, bufs), tok

    def gather_pass(tag, h, after):
        na, ss, rs, bufs = h
        bufs, tok = split_wait(f"gather{tag}_ici_wait", ss, rs, bufs, plan_gather_ici(na), after)
        ss, rs, lands, tok = split_start(f"gather{tag}_pass_start", bufs[na:], plan_gather_pass(na), 3 * na, tok)
        return (na, ss, rs, bufs[:na], lands), tok

    def gather_end(tag, h, after):
        na, ss, rs, shards, lands = h
        lands, _ = split_wait(f"gather{tag}_pass_wait", ss, rs, lands, plan_gather_pass(na), after)
        return [place_own(f"gather{tag}_own{a}", lands[a], shards[a], dev_sp) for a in range(na)]

    small_sharded = ['e_conv_w', 'o_norm', 'o_pw1_b', 'o_dw', 'o_dw_b', 'o_ln_g', 'o_ln_b', 'o_pw2_b']
    sm = all_gather("gather_small", [pack([W[n][0]]) for n in small_sharded])

    bfw = lambda w: w.astype(BF)
    hA0, tok = gather_begin("0", [bfw(jnp.swapaxes(e_w_in[0], 0, 1))], sm[0])
    hA1, tok = gather_begin("1", [bfw(e_w_out[0]), bfw(f_w1[0]), bfw(f_w2[0])], tok)
    hA2, tok = gather_begin("2", [bfw(o_pw1[0]), bfw(o_pw2[0])], tok)
    hA3, tok = gather_begin("3", [bfw(f_w1[1]), bfw(f_w2[1])], tok)
    h0 = rowcall("rms_e", rms_tile, [x2d], [en_row], [(D, BF)], after=tok)

    full = {}
    for n, g in zip(small_sharded, sm):
        shp = W[n][0].shape
        blocks = [unpack(g[d], [shp])[0] for d in range(N_DEV)]
        full[n] = jnp.concatenate(blocks, axis=-1)
    conv_w = full['e_conv_w']
    on_row, pw1_b_row = row(full['o_norm']), row(full['o_pw1_b'])
    dw_w, dw_b_row = full['o_dw'], row(full['o_dw_b'])
    oln_g_row, oln_b_row, pw2_b_row = row(full['o_ln_g']), row(full['o_ln_b']), row(full['o_pw2_b'])
    small_names = ['e_norm', 'e_conv_w', 'e_a_log', 'e_dt_bias', 'e_o_norm', 'e_ln_g', 'e_ln_b', 'e_w_s', 'e_b_s',
                   'o_norm', 'o_pw1_b', 'o_dw', 'o_dw_b', 'o_ln_g', 'o_ln_b', 'o_pw2_b', 'f_norm', 'final_norm']
    packed_wmv = [pack([A[n] for n in small_names], 256) for A in (W, Mo, Vo)]

    hB0, tok = gather_pass("0", hA0, [h0, conv_w, dw_w, pw1_b_row] + packed_wmv)
    (g_win,) = gather_end("0", hB0, tok)
    win_t = g_win.reshape(in_cols, D)
    wt_qkv, wt_z = win_t[:AQKV], win_t[AQKV:AQKV + AV]
    wt_ba = jnp.pad(win_t[AQKV + AV:AQKV + AV + 2 * H], ((0, LANE - 2 * H), (0, 0)))
    wt_uv = win_t[AQKV + AV + 2 * H:]

    alog_row = jnp.pad(row(e_a_log), ((0, 0), (H, LANE - 2 * H)))
    dtb_row = jnp.pad(row(e_dt_bias), ((0, 0), (H, LANE - 2 * H)))
    eon_row = row(e_o_norm)
    eln_g_row, eln_b_row = row(e_ln_g), row(e_ln_b)
    w_s = e_w_s[0]
    bs_t = e_b_s[0].T
    fn_rows = [row(f_norm[l]) for l in range(L)]
    fin_row = row(final_norm)

    qkv_raw = matmul("proj_qkv", h0, wt_qkv, "nt")
    z_gate = matmul("proj_z", h0, wt_z, "nt")
    ba = matmul("proj_ba", h0, wt_ba, "nt")
    uv = matmul("proj_uv", h0, wt_uv, "nt")

    cwa = min(512, AQKV)
    qkv_post = make_qkv_post(dk, cwa, 2 * AQK // cwa)
    ident = lambda t: t
    qkv = conv_fwd("qkv_conv", qkv_raw, conv_w, [], ident, qkv_post, [0], 1, KA, cw=cwa, tr=512)
    bgc_fn = make_bgc(H)
    bgc = rowcall("bgc", bgc_fn, [ba], [alog_row, dtb_row], [(LANE, F32)])
    o_dn, xinv, states = deltanet_fwd(qkv, bgc, H)
    hB1, tok = gather_pass("1", hA1, o_dn)

    def mix_tile(o, z, uv_, o_norm_, ln_g, ln_b, w_s_, bs_t_):
        return jnp.concatenate([mixa_post_tile(o, z, o_norm_), mixb_tile(uv_, ln_g, ln_b, w_s_, bs_t_)], axis=-1)
    mix_consts = [eon_row, eln_g_row, eln_b_row, w_s, bs_t]
    mix = rowcall("mix", mix_tile, [o_dn, z_gate, uv], mix_consts, [(AV + BW, BF)], after=tok)
    g_wout, g_w1_0, g_w2_0 = gather_end("1", hB1, mix)
    wout = g_wout.reshape(-1, D)
    w1 = [jnp.moveaxis(g_w1_0, 0, 1).reshape(D, -1), None]
    w2 = [g_w2_0.reshape(-1, D), None]
    add_epi = lambda acc, r: (acc + r,)
    x1 = matmul("out_proj", mix, wout, "nn", epi=add_epi, extras=[x2d], tm=512, tn=2048)

    def relu2_epi(acc):
        r = jnp.maximum(acc, 0.0)
        return r * r, r

    a2_0, ar_0, hf0 = norm_matmul("ffn_up0", x1, fn_rows[0], w1[0], epi=relu2_epi, out_dtypes=(BF, BF))
    hB2, tok = gather_pass("2", hA2, a2_0)
    x2 = matmul("ffn_down0", a2_0, w2[0], "nn", epi=add_epi, extras=[x1], after=tok)
    ffn0 = (hf0, a2_0, ar_0)
    g_pw1, g_pw2 = gather_end("2", hB2, x2)
    pw1 = jnp.moveaxis(g_pw1, 0, 1).reshape(D, 2 * D)
    pw2 = g_pw2.reshape(D, D)

    bias_epi = lambda acc, b: (acc + b,)
    zc, h1 = norm_matmul("pw1", x2, on_row, pw1, epi=bias_epi, extras=[pw1_b_row])
    hB3, tok = gather_pass("3", hA3, zc)
    cwc = min(512, D)
    ncb = D // cwc
    cconv = conv_fwd("dw_conv", zc, dw_w, [tie(dw_b_row, tok)], glu_pre, bias_post, [0, ncb], 1, KC, hb=32, cw=cwc,
                     tr=512)
    g_w1_1, g_w2_1 = gather_end("3", hB3, cconv)
    w1[1] = jnp.moveaxis(g_w1_1, 0, 1).reshape(D, -1)
    w2[1] = g_w2_1.reshape(-1, D)
    ln_silu = lambda c, g, b: jax.nn.silu(ln_tile(c, g, b))
    s_act = rowcall("ln_silu", ln_silu, [cconv], [oln_g_row, oln_b_row], [(D, BF)])
    x3 = matmul("pw2", s_act, pw2, "nn", epi=lambda acc, r, b: (r + (acc + b),), extras=[x2, pw2_b_row],
                tm=512, tn=2048)
    a2_1, ar_1, hf1 = norm_matmul("ffn_up1", x3, fn_rows[1], w1[1], epi=relu2_epi, out_dtypes=(BF, BF))
    x4 = matmul("ffn_down1", a2_1, w2[1], "nn", epi=add_epi, extras=[x3])
    ffn1 = (hf1, a2_1, ar_1)

    def loss_bwd_tile(x_, t_, g_):
        l, vjp = jax.vjp(lambda a, b: loss_tile(a, b, t_), x_, g_)
        dx, dg = vjp(jnp.ones_like(l))
        return dx, dx, l, dg
    dx4, dx4_b, loss_part, d_final = rowcall("loss_head", loss_bwd_tile, [x4, tgt], [fin_row],
                                             [(D, F32), (D, BF)], [(1, 1), (1, D)])
    loss = lax.psum(loss_part[0, 0], ("x", "y", "c"))

    def reduce_begin(tag, grads, after):
        na = len(grads)
        lands = [lax.empty((4,) + g.shape[1:], g.dtype) for g in grads]
        ss, rs, bufs, tok = split_start(f"reduce{tag}_d2d_start", grads + lands, plan_reduce_d2d(na), 4 * na, after)
        return (na, ss, rs, bufs), tok

    def reduce_mid(tag, h, after):
        na, ss, rs, bufs = h
        bufs, tok = split_wait(f"reduce{tag}_d2d_wait", ss, rs, bufs, plan_reduce_d2d(na), after)
        parts = []
        for a, (g, rc) in enumerate(zip(bufs[:na], bufs[na:])):
            r, c = g.shape[1], g.shape[2]
            if r % SUBLANE:
                parts.append(slabcall(f"chip_sum{tag}_{a}", lambda p, q: (p + q,),
                                      [(g, lambda i, s: 2 * i + s[0]), (rc, lambda i, s: i)], [BF], where, 4))
                continue
            mine = lambda i, n, s: (2 * (i // (n // 4)) + s[0]) * (n // 4) + i % (n // 4)
            parts.append(rowcall(f"chip_sum{tag}_{a}", lambda p, q: (p + q,),
                                 [(g.reshape(N_DEV * r, c), mine), rc.reshape(4 * r, c)], [],
                                 [(c, BF)], tr=_tile(r, 512), sp=where, R=4 * r).reshape(4, r, c))
        lands = [lax.empty((3,) + p.shape[1:], p.dtype) for p in parts]
        ss, rs, bufs, tok = split_start(f"reduce{tag}_ici_start", parts + lands, plan_reduce_ici(na), 3 * na, tok)
        return (na, ss, rs, bufs), tok

    res = {}

    def reduce_end(tag, h, after, targets):
        na, ss, rs, bufs = h
        bufs, _ = split_wait(f"reduce{tag}_ici_wait", ss, rs, bufs, plan_reduce_ici(na), after)
        for a, (part, fin, (n, l)) in enumerate(zip(bufs[:na], bufs[na:], targets)):
            def f(p0, p1, p2, p3, w_, m_, v_):
                g = ((p0.astype(F32) + p1.astype(F32)) + p2.astype(F32)) + p3.astype(F32)
                return (g,) + adamw_tile(w_, g, m_, v_)
            r, C = fin.shape[-2], fin.shape[-1]
            if W[n].shape[1:] == (C, r):
                t = lambda arr: jnp.swapaxes(arr[l], 0, 1)
                outs = slabcall(f"adam{tag}_{a}", f, [(part, lambda i, s: s[1]), (fin, 0), (fin, 1), (fin, 2),
                                                      (t(W[n]), None), (t(Mo[n]), None), (t(Vo[n]), None)],
                                [F32] * 4, where)
                res[n] = tuple(jnp.swapaxes(o, 0, 1)[None] for o in outs)
                continue
            own = lambda i, n_, s: s[1] * n_ + i
            res[n] = rowcall(f"adam{tag}_{a}", f, [(part.reshape(4 * r, C), own), (fin, 0), (fin, 1), (fin, 2),
                                                   (W[n], l), (Mo[n], l), (Vo[n], l)], [],
                             [(C, F32)] * 4, tr=256, sp=where, R=r,
                             out_lead=(W[n].shape[0], l), into=res.get(n))

    dscale_epi = lambda acc, r: (acc * (2.0 * r.astype(F32)),)
    d_fnorm = [None] * L

    dpre1 = matmul("ffn_down_dx1", dx4_b, w2[1], "nt", epi=dscale_epi, extras=[ar_1], out_dtypes=(BF,))
    dw2_1 = matmul("ffn_down_dw1", a2_1, dx4_b, "tn")
    dw1_1 = matmul("ffn_up_dw1", hf1, dpre1, "tn", colshard=True)
    hD1, tok = reduce_begin("1", [dw1_1, dw2_1.reshape(N_DEV, -1, D)], dpre1)
    dhf1 = matmul("ffn_up_dx1", dpre1, w1[1], "nt", after=tok, tk=4096)
    dx3, dx3_b, d_fnorm[1] = rms_bwd("rms_f_bwd1", x3, fn_rows[1], dhf1, dx4)

    ds_act = matmul("pw2_dx", dx3_b, pw2, "nt")
    hI1, tok = reduce_mid("1", hD1, ds_act)
    d_pw2 = matmul("pw2_dw", s_act, dx3_b, "tn", after=tok)
    ln_silu_bwd = _vjp_rows(ln_silu, 1, 1)

    def ln_silu_bwd_tile(c_, ds_, dx3_, g_, b_):
        dc, dg, db = ln_silu_bwd(c_, ds_, g_, b_)
        return dc, dg, db, jnp.sum(dx3_, axis=0, keepdims=True)
    dcconv, d_oln_g, d_oln_b, d_pw2_b = rowcall(
        "ln_silu_bwd", ln_silu_bwd_tile, [cconv, ds_act, dx3], [oln_g_row, oln_b_row],
        [(D, F32)], [(1, D), (1, D), (1, D)])
    (dza, dzb), (sza, szb), d_dw, (d_dw_b,) = conv_bwd(
        "dw_conv_bwd", zc, dw_w, [tie(dw_b_row, tok)], [dcconv], glu_pre, bias_post, [0, ncb], KC, hb=32, cw=cwc, tr=256,
        recompute=False)
    dzc = jnp.concatenate([dza, dzb], axis=-1)
    d_pw1_b = jnp.concatenate([sza, szb], axis=-1)
    d_pw1 = matmul("pw1_dw", h1, dzc, "tn", colshard=True)
    dh1 = matmul("pw1_dx", dzc, pw1, "nt")
    dx2, dx2_b, d_onorm = rms_bwd("rms_o_bwd", x2, on_row, dh1, dx3)
    reduce_end("1", hI1, dx2, [('f_w1', 1), ('f_w2', 1)])

    hD2, tok = reduce_begin("2", [d_pw1, d_pw2.reshape(N_DEV, -1, D)], dx2)
    dpre0 = matmul("ffn_down_dx0", dx2_b, w2[0], "nt", epi=dscale_epi, extras=[ar_0], out_dtypes=(BF,), after=tok)
    dw2_0 = matmul("ffn_down_dw0", a2_0, dx2_b, "tn")
    hI2, tok = reduce_mid("2", hD2, dw2_0)
    dw1_0 = matmul("ffn_up_dw0", hf0, dpre0, "tn", colshard=True, after=tok)
    dhf0 = matmul("ffn_up_dx0", dpre0, w1[0], "nt", tk=4096)
    dx1, dx1_b, d_fnorm[0] = rms_bwd("rms_f_bwd0", x1, fn_rows[0], dhf0, dx2)
    reduce_end("2", hI2, dx1, [('o_pw1', 0), ('o_pw2', 0)])

    dmix = matmul("out_proj_dx", dx1_b, wout, "nt")
    d_wout = matmul("out_proj_dw", mix, dx1_b, "tn")
    hD3, tok = reduce_begin("3", [dw1_0, dw2_0.reshape(N_DEV, -1, D), d_wout.reshape(N_DEV, -1, D)], dmix)
    do_dn, dz_gate, duv, d_eon, d_eln_g, d_eln_b, d_ws, d_bs_t = rowcall(
        "mix_bwd", _vjp_rows(mix_tile, 3, 1), [o_dn, z_gate, uv, dmix], mix_consts,
        [(AV, F32), (AV, BF), (2 * BW, BF)], [(1, dv), (1, BW), (1, BW), w_s.shape, bs_t.shape], after=tok)
    dqkv, dbgc = deltanet_bwd(qkv, bgc, xinv, states, do_dn, H)
    hI3, tok = reduce_mid("3", hD3, dbgc)
    bgc_bwd = _vjp_rows(bgc_fn, 1, 1)
    dba, d_alog_row, d_dtb_row = rowcall(
        "bgc_bwd", bgc_bwd, [ba, dbgc], [alog_row, dtb_row], [(LANE, BF)], [(1, LANE), (1, LANE)])
    (dqkv_raw,), _, d_conv_w, _ = conv_bwd(
        "qkv_conv_bwd", qkv_raw, tie(conv_w, tok), [], [dqkv], ident, qkv_post, [0], KA, cw=cwa, tr=512)

    dw_qkv = matmul("proj_qkv_dw", dqkv_raw, h0, "tn")
    dw_z = matmul("proj_z_dw", dz_gate, h0, "tn")
    dw_ba = matmul("proj_ba_dw", dba, h0, "tn")
    dw_uv = matmul("proj_uv_dw", duv, h0, "tn")
    d_win_t = jnp.concatenate([dw_qkv, dw_z, dw_ba[:2 * H], dw_uv], axis=0)
    G_win = d_win_t.reshape(N_DEV, in_cols // N_DEV, D)
    hD4, tok = reduce_begin("4", [G_win], dw_uv)
    reduce_end("3", hI3, tok, [('f_w1', 0), ('f_w2', 0), ('e_w_out', 0)])
    hI4, tok = reduce_mid("4", hD4, [res[n][0] for n in ('f_w1', 'f_w2', 'o_pw1', 'e_w_out')])
    dh0 = matmul_sum("proj_dx", [(dqkv_raw, wt_qkv), (dz_gate, wt_z), (dba, wt_ba), (duv, wt_uv)], after=tok)
    grad_x, _, d_enorm = rms_bwd("rms_e_bwd", x2d, en_row, dh0, dx1, after=tok)

    d_alog = d_alog_row[:, H:2 * H]
    d_dtb = d_dtb_row[:, H:2 * H]
    small_grads = [d_enorm, d_conv_w, d_alog, d_dtb, d_eon, d_eln_g, d_eln_b, d_ws, d_bs_t.T,
                   d_onorm, d_pw1_b, d_dw, d_dw_b, d_oln_g, d_oln_b, d_pw2_b,
                   jnp.concatenate(d_fnorm, axis=0), d_final]
    full_shapes = [g.shape for g in small_grads]
    gs_all = all_gather("gather_small_grads", [pack(small_grads, 256)])[0]

    def sum8(*ps):
        s = ps[0]
        for p in ps[1:]:
            s = s + p
        return (s,)
    gs_sum = rowcall("small_sum", sum8, [(gs_all, d) for d in range(N_DEV)], [], [(LANE, F32)])
    reduce_end("4", hI4, gs_sum, [('e_w_in', 0)])
    g_full = dict(zip(small_names, unpack(gs_sum, full_shapes)))
    g_loc = {}
    for n in small_names:
        g = g_full[n]
        if n in small_sharded:
            per = g.shape[-1] // N_DEV
            g = lax.dynamic_slice_in_dim(g, dev * per, per, axis=-1)
        g_loc[n] = g.reshape(W[n].shape)
    packs = [packed_wmv[0], pack([g_loc[n] for n in small_names], 256), packed_wmv[1], packed_wmv[2]]
    d_s, m_s, v_s = rowcall("adam_small", adamw_tile, packs, [], [(LANE, F32)] * 3)
    shapes = [W[n].shape for n in small_names]
    for n, d_, m_, v_ in zip(small_names, unpack(d_s, shapes), unpack(m_s, shapes), unpack(v_s, shapes)):
        res[n] = (g_loc[n], d_, m_, v_)

    grads = [res[n][0] for n in names]
    deltas = [res[n][1] for n in names]
    new_m = [res[n][2] for n in names]
    new_v = [res[n][3] for n in names]
    return (loss, grad_x.reshape(x.shape), *grads, *deltas, *new_m, *new_v)
```

```python
import functools
import math

import jax
import jax.numpy as jnp
import numpy as np
from jax import lax
from jax.experimental import pallas as pl
from jax.experimental.pallas import tpu as pltpu

F32 = jnp.float32
BF = jnp.bfloat16
EPS = 1e-6
CHUNK = 64
B_BLOCK = 128
LANE = 128
SUBLANE = 8
N_DEV = 8
VMEM_LIMIT = 56 * 1024 * 1024

ADAM_LR = 0.001
ADAM_B1 = 0.9
ADAM_B2 = 0.999
ADAM_EPS = 1e-08
ADAM_WD = 0.01
ADAM_STEP = 10

MESH = pl.DeviceIdType.MESH
ANY = pl.BlockSpec(memory_space=pl.ANY)


def _tile(n, pref, mult=SUBLANE):
    if n <= pref:
        return n
    t = (pref // mult) * mult
    while t >= mult:
        if n % t == 0:
            return t
        t -= mult
    return n


THREE_PASS = 3


def _dg(a, b, ca, cb, hi):
    nb = a.ndim - 2
    batch = tuple(range(nb))
    dims = (((ca + nb,), (cb + nb,)), (batch, batch))
    if hi == THREE_PASS:
        a, b = a.astype(F32), b.astype(F32)
        ah, bh = a.astype(BF), b.astype(BF)
        al, bl = (a - ah.astype(F32)).astype(BF), (b - bh.astype(F32)).astype(BF)
        dot = lambda p, q: lax.dot_general(p, q, dims, preferred_element_type=F32)
        return dot(ah, bh) + (dot(ah, bl) + dot(al, bh))
    if hi:
        return lax.dot_general(a.astype(F32), b.astype(F32), dims,
                               precision=lax.Precision.HIGHEST, preferred_element_type=F32)
    return lax.dot_general(a.astype(BF), b.astype(BF), dims, preferred_element_type=F32)


@functools.partial(jax.custom_vjp, nondiff_argnums=(2, 3, 4))
def mm(a, b, ca, cb, hi=False):
    return _dg(a, b, ca, cb, hi)


def _mm_fwd(a, b, ca, cb, hi):
    return _dg(a, b, ca, cb, hi), (a, b)


def _mm_bwd(ca, cb, hi, res, g):
    a, b = res
    if ca == 1:
        da = mm(g, b, 1, 1 - cb, hi)
    else:
        da = mm(b, g, 1 - cb, 1, hi)
    if cb == 0:
        db = mm(a, g, 1 - ca, 0, hi)
    else:
        db = mm(g, a, 0, 1 - ca, hi)
    return da.astype(a.dtype), db.astype(b.dtype)


mm.defvjp(_mm_fwd, _mm_bwd)


def matmul(name, a, b, mode, epi=None, extras=(), out_dtypes=(F32,), colshard=False,
           tm=None, tn=1024, tk=2048, after=None):
    afters = [] if after is None else [after]
    slabs = b.ndim == 3
    if slabs:
        assert mode in ("nn", "nt") and not colshard, name
        b_rows, b_cols = b.shape[1], b.shape[0] * b.shape[2]
    else:
        b_rows, b_cols = b.shape
    if mode == "nn":
        (M, K), (K2, N) = a.shape, (b_rows, b_cols)
    elif mode == "nt":
        (M, K), (N, K2) = a.shape, (b_rows, b_cols)
    else:
        (K, M), (K2, N) = a.shape, b.shape
    assert K == K2, (name, a.shape, b.shape, mode)
    if tm is None:
        tm = 1024
    tm = _tile(M, tm)
    tn = N // N_DEV if colshard else _tile(N, tn, LANE)
    tk = _tile(K, tk, LANE)
    if slabs and mode == "nn":
        tn = b.shape[2]
    if slabs and mode == "nt":
        tk = b.shape[2]
    nk = K // tk
    grid = (M // tm, N // tn, nk)
    if mode == "nn":
        a_spec = pl.BlockSpec((tm, tk), lambda i, j, k: (i, k))
        b_spec = (pl.BlockSpec((None, tk, tn), lambda i, j, k: (j, k, 0)) if slabs
                  else pl.BlockSpec((tk, tn), lambda i, j, k: (k, j)))
        ca, cb = 1, 0
    elif mode == "nt":
        a_spec = pl.BlockSpec((tm, tk), lambda i, j, k: (i, k))
        b_spec = (pl.BlockSpec((None, tn, tk), lambda i, j, k: (k, j, 0)) if slabs
                  else pl.BlockSpec((tn, tk), lambda i, j, k: (j, k)))
        ca, cb = 1, 1
    else:
        a_spec = pl.BlockSpec((tk, tm), lambda i, j, k: (k, i))
        b_spec = pl.BlockSpec((tk, tn), lambda i, j, k: (k, j))
        ca, cb = 0, 0
    ex_specs = []
    for e in extras:
        if e.shape[0] == 1:
            ex_specs.append(pl.BlockSpec((1, tn), lambda i, j, k: (0, j)))
        else:
            assert e.shape == (M, N), (name, e.shape)
            ex_specs.append(pl.BlockSpec((tm, tn), lambda i, j, k: (i, j)))
    if colshard:
        out_shape = [jax.ShapeDtypeStruct((N_DEV, M, tn), dt) for dt in out_dtypes]
        out_specs = [pl.BlockSpec((None, tm, tn), lambda i, j, k: (j, i, 0)) for _ in out_dtypes]
    else:
        out_shape = [jax.ShapeDtypeStruct((M, N), dt) for dt in out_dtypes]
        out_specs = [pl.BlockSpec((tm, tn), lambda i, j, k: (i, j)) for _ in out_dtypes]
    n_ex, n_out = len(extras), len(out_dtypes)

    def body(*refs):
        a_ref, b_ref = refs[0], refs[1]
        ex_refs = refs[2:2 + n_ex]
        first_out = 2 + n_ex + len(afters)
        o_refs = refs[first_out:first_out + n_out]
        part = _dg(a_ref[...], b_ref[...], ca, cb, False)

        def finish(acc):
            res = (acc,) if epi is None else epi(acc, *[r[...] for r in ex_refs])
            for o_ref, r in zip(o_refs, res):
                o_ref[...] = r.astype(o_ref.dtype)

        if nk == 1:
            finish(part)
            return
        acc_ref = refs[-1]
        k = pl.program_id(2)

        @pl.when(k == 0)
        def _():
            acc_ref[...] = part

        @pl.when(k > 0)
        def _():
            acc_ref[...] += part

        @pl.when(k == nk - 1)
        def _():
            finish(acc_ref[...])

    outs = pl.pallas_call(
        body, name=name, grid=grid,
        in_specs=[a_spec, b_spec] + ex_specs + [ANY] * len(afters),
        out_specs=out_specs, out_shape=out_shape,
        scratch_shapes=[pltpu.VMEM((tm, tn), F32)] if nk > 1 else [],
        compiler_params=pltpu.CompilerParams(
            dimension_semantics=("parallel", "parallel", "arbitrary"),
            vmem_limit_bytes=VMEM_LIMIT),
    )(a, b, *extras, *afters)
    return outs[0] if n_out == 1 else tuple(outs)


def norm_matmul(name, x, g, b, epi=None, extras=(), out_dtypes=(F32,), tm=1024, tn=1024, after=None):
    M, K = x.shape
    slabs = b.ndim == 3
    N = b.shape[0] * b.shape[2] if slabs else b.shape[1]
    tm, tn = _tile(M, tm), (b.shape[2] if slabs else _tile(N, tn, LANE))
    b_spec = (pl.BlockSpec((None, K, tn), lambda i, j: (j, 0, 0)) if slabs
              else pl.BlockSpec((K, tn), lambda i, j: (0, j)))
    afters = [] if after is None else [after]
    ex_specs = [pl.BlockSpec((1, tn), lambda i, j: (0, j)) if e.shape[0] == 1
                else pl.BlockSpec((tm, tn), lambda i, j: (i, j)) for e in extras]
    n_ex, n_out = len(extras), len(out_dtypes)

    def body(*refs):
        x_ref, g_ref, b_ref = refs[:3]
        ex_refs = refs[3:3 + n_ex]
        first_out = 3 + n_ex + len(afters)
        o_refs = refs[first_out:first_out + n_out]
        h_ref, h_scr = refs[first_out + n_out], refs[-1]

        @pl.when(pl.program_id(1) == 0)
        def _():
            h = rms_tile(x_ref[...], g_ref[...]).astype(h_scr.dtype)
            h_scr[...] = h
            h_ref[...] = h

        acc = _dg(h_scr[...], b_ref[...], 1, 0, False)
        res = (acc,) if epi is None else epi(acc, *[r[...] for r in ex_refs])
        for o_ref, r in zip(o_refs, res):
            o_ref[...] = r.astype(o_ref.dtype)

    outs = pl.pallas_call(
        body, name=name, grid=(M // tm, N // tn),
        in_specs=[pl.BlockSpec((tm, K), lambda i, j: (i, 0)), pl.BlockSpec((1, K), lambda i, j: (0, 0)),
                  b_spec] + ex_specs + [ANY] * len(afters),
        out_specs=[pl.BlockSpec((tm, tn), lambda i, j: (i, j)) for _ in out_dtypes]
        + [pl.BlockSpec((tm, K), lambda i, j: (i, 0))],
        out_shape=[jax.ShapeDtypeStruct((M, N), dt) for dt in out_dtypes] + [jax.ShapeDtypeStruct((M, K), BF)],
        scratch_shapes=[pltpu.VMEM((tm, K), BF)],
        compiler_params=pltpu.CompilerParams(
            dimension_semantics=("parallel", "arbitrary"), vmem_limit_bytes=VMEM_LIMIT),
    )(x, g, b, *extras, *afters)
    return tuple(outs)


def matmul_sum(name, pairs, tm=512, tn=1024, after=None):
    M, N = pairs[0][0].shape[0], pairs[0][1].shape[1]
    tm, tn = _tile(M, tm), _tile(N, tn, LANE)
    afters = [] if after is None else [after]
    in_specs, operands = [], []
    for a, b in pairs:
        assert a.shape == (M, b.shape[0]) and b.shape[1] == N, (name, a.shape, b.shape)
        in_specs += [pl.BlockSpec((tm, a.shape[1]), lambda i, j: (i, 0)),
                     pl.BlockSpec((b.shape[0], tn), lambda i, j: (0, j))]
        operands += [a, b]
    n = len(pairs)

    def body(*refs):
        acc = _dg(refs[0][...], refs[1][...], 1, 0, False)
        for p in range(1, n):
            acc = acc + _dg(refs[2 * p][...], refs[2 * p + 1][...], 1, 0, False)
        refs[-1][...] = acc

    return pl.pallas_call(
        body, name=name, grid=(M // tm, N // tn),
        in_specs=in_specs + [ANY] * len(afters),
        out_specs=pl.BlockSpec((tm, tn), lambda i, j: (i, j)),
        out_shape=jax.ShapeDtypeStruct((M, N), F32),
        compiler_params=pltpu.CompilerParams(
            dimension_semantics=("parallel", "parallel"), vmem_limit_bytes=VMEM_LIMIT),
    )(*operands, *afters)


def rowcall(name, fn, rows, consts, out_rows, out_accs=(), tr=256, sp=None, R=None, after=None,
            out_lead=None, into=None):
    afters = ([] if after is None else [after]) + ([] if into is None else list(into))
    rows = [r if isinstance(r, tuple) else (r, None) for r in rows]
    R = rows[0][0].shape[-2] if R is None else R
    tr = _tile(R, tr)
    n = R // tr
    in_specs = []
    for arr, lead in rows:
        C = arr.shape[-1]
        if lead is None:
            assert arr.shape[-2] == R, (name, arr.shape, R)
            in_specs.append(pl.BlockSpec((tr, C), lambda i, *s: (i, 0)))
        elif callable(lead):
            in_specs.append(pl.BlockSpec((tr, C), lambda i, *s, lead=lead: (lead(i, n, *s), 0)))
        else:
            assert arr.shape[-2] == R, (name, arr.shape, R)
            in_specs.append(pl.BlockSpec((None, tr, C), lambda i, *s, lead=lead: (lead, i, 0)))
    for c in consts:
        in_specs.append(pl.BlockSpec(c.shape, lambda i, *s, nd=c.ndim: (0,) * nd))
    if out_lead is None:
        out_shape = [jax.ShapeDtypeStruct((R, C), dt) for C, dt in out_rows]
        out_specs = [pl.BlockSpec((tr, C), lambda i, *s: (i, 0)) for C, _ in out_rows]
    else:
        n_slab, slab = out_lead
        out_shape = [jax.ShapeDtypeStruct((n_slab, R, C), dt) for C, dt in out_rows]
        out_specs = [pl.BlockSpec((None, tr, C), lambda i, *s: (slab, i, 0)) for C, _ in out_rows]
    for shp in out_accs:
        out_shape.append(jax.ShapeDtypeStruct(shp, F32))
        out_specs.append(pl.BlockSpec(shp, lambda i, *s, nd=len(shp): (0,) * nd))
    n_in, n_row, n_acc = len(rows) + len(consts), len(out_rows), len(out_accs)
    n_sp = 0 if sp is None else 1

    def body(*refs):
        refs = refs[n_sp:]
        ins = [r[...] for r in refs[:n_in]]
        res = fn(*ins)
        if not isinstance(res, (tuple, list)):
            res = (res,)
        o_refs = refs[n_in + len(afters):]
        for o_ref, r in zip(o_refs[:n_row], res[:n_row]):
            o_ref[...] = r.astype(o_ref.dtype)
        if n_acc:
            first = pl.program_id(0) == 0
            for o_ref, r in zip(o_refs[n_row:], res[n_row:]):
                r = r.astype(F32).reshape(o_ref.shape)

                @pl.when(first)
                def _(o_ref=o_ref, r=r):
                    o_ref[...] = r

                @pl.when(jnp.logical_not(first))
                def _(o_ref=o_ref, r=r):
                    o_ref[...] += r

    params = pltpu.CompilerParams(dimension_semantics=("arbitrary",), vmem_limit_bytes=VMEM_LIMIT)
    operands = [a for a, _ in rows] + list(consts) + afters
    in_specs = in_specs + [ANY] * len(afters)
    aliases = {} if into is None else {n_sp + len(operands) - len(into) + k: k for k in range(len(into))}
    if sp is None:
        outs = pl.pallas_call(body, name=name, grid=(n,), in_specs=in_specs, out_specs=out_specs,
                              out_shape=out_shape, input_output_aliases=aliases,
                              compiler_params=params)(*operands)
    else:
        outs = pl.pallas_call(
            body, name=name, out_shape=out_shape, compiler_params=params, input_output_aliases=aliases,
            grid_spec=pltpu.PrefetchScalarGridSpec(
                num_scalar_prefetch=1, grid=(n,), in_specs=in_specs, out_specs=out_specs),
        )(sp, *operands)
    return outs[0] if len(outs) == 1 else tuple(outs)


def slabcall(name, fn, ins, out_dtypes, sp, n_out_slabs=None, cw=512):
    R, C = ins[0][0].shape[-2:]
    cw = _tile(C, cw, LANE)
    in_specs = []
    for arr, slab in ins:
        assert arr.shape[-2:] == (R, C), (name, arr.shape)
        if slab is None:
            in_specs.append(pl.BlockSpec((R, cw), lambda i, j, s: (0, j)))
        elif callable(slab):
            in_specs.append(pl.BlockSpec((None, R, cw), lambda i, j, s, slab=slab: (slab(i, s), 0, j)))
        else:
            in_specs.append(pl.BlockSpec((None, R, cw), lambda i, j, s, slab=slab: (slab, 0, j)))
    if n_out_slabs is None:
        out_shape = [jax.ShapeDtypeStruct((R, C), dt) for dt in out_dtypes]
        out_specs = [pl.BlockSpec((R, cw), lambda i, j, s: (0, j)) for _ in out_dtypes]
    else:
        out_shape = [jax.ShapeDtypeStruct((n_out_slabs, R, C), dt) for dt in out_dtypes]
        out_specs = [pl.BlockSpec((None, R, cw), lambda i, j, s: (i, 0, j)) for _ in out_dtypes]
    n_in = len(ins)

    def body(sp_ref, *refs):
        res = fn(*[r[...] for r in refs[:n_in]])
        for o_ref, r in zip(refs[n_in:], res):
            o_ref[...] = r.astype(o_ref.dtype)

    outs = pl.pallas_call(
        body, name=name, out_shape=out_shape,
        grid_spec=pltpu.PrefetchScalarGridSpec(
            num_scalar_prefetch=1, grid=(n_out_slabs or 1, C // cw), in_specs=in_specs, out_specs=out_specs),
        compiler_params=pltpu.CompilerParams(
            dimension_semantics=("arbitrary", "arbitrary"), vmem_limit_bytes=VMEM_LIMIT),
    )(sp, *[a for a, _ in ins])
    return outs[0] if len(outs) == 1 else tuple(outs)


def rms_tile(x, g):
    x = x.astype(F32)
    return x * lax.rsqrt(jnp.mean(x * x, axis=-1, keepdims=True) + EPS) * g


def gelu(x):
    return 0.5 * x * (1.0 + lax.erf(x * (1.0 / math.sqrt(2.0))))


def ln_tile(x, g, b):
    mu = jnp.mean(x, axis=-1, keepdims=True)
    xc = x - mu
    return xc * lax.rsqrt(jnp.mean(xc * xc, axis=-1, keepdims=True) + EPS) * g + b


def lane_groups(fn, width, *arrs):
    n = arrs[0].shape[-1] // width
    outs = [fn(*[a[:, i * width:(i + 1) * width] for a in arrs]) for i in range(n)]
    return jnp.concatenate(outs, axis=-1)


def mixa_post_tile(o, z, o_norm):
    dv = o_norm.shape[-1]
    on = lane_groups(lambda t: rms_tile(t, o_norm), dv, o)
    return on * jax.nn.silu(z)


def mixb_tile(uv, ln_g, ln_b, w_s, bs_t):
    G = w_s.shape[0]
    gw = ln_g.shape[-1]
    dg = gw // G
    tr = uv.shape[0]
    u = gelu(uv[:, :gw])
    vg = gelu(uv[:, gw:])
    ii = lax.broadcasted_iota(jnp.int32, (B_BLOCK, B_BLOCK), 0)
    jj = lax.broadcasted_iota(jnp.int32, (B_BLOCK, B_BLOCK), 1)
    mask = (jj // CHUNK) <= (ii // CHUNK)
    cols = []
    for g in range(G):
        sl = slice(g * dg, (g + 1) * dg)
        vn = ln_tile(vg[:, sl], ln_g[:, sl], ln_b[:, sl])
        wm = jnp.where(mask, w_s[g], 0.0)
        blocks = []
        for m in range(tr // B_BLOCK):
            blk = vn[m * B_BLOCK:(m + 1) * B_BLOCK, :]
            blocks.append(mm(wm, blk, 1, 0) + bs_t[:, g:g + 1])
        mixed = blocks[0] if len(blocks) == 1 else jnp.concatenate(blocks, axis=0)
        cols.append(u[:, sl] * mixed)
    return jnp.concatenate(cols, axis=-1)


def bgc_tile(ba, alog_row, dtb_row):
    tr = ba.shape[0]
    beta = jax.nn.sigmoid(ba)
    g = -jnp.exp(alog_row) * jax.nn.softplus(ba + dtb_row)
    ii = lax.broadcasted_iota(jnp.int32, (tr, tr), 0)
    jj = lax.broadcasted_iota(jnp.int32, (tr, tr), 1)
    tri = jnp.where((ii // CHUNK == jj // CHUNK) & (jj <= ii), 1.0, 0.0).astype(F32)
    gc = mm(tri, g, 1, 0, True)
    return beta, gc


def make_bgc(H):
    def f(ba, alog_row, dtb_row):
        beta, gc = bgc_tile(ba, alog_row, dtb_row)
        lane = lax.broadcasted_iota(jnp.int32, ba.shape, 1)
        return jnp.where(lane < H, beta, jnp.where(lane < 2 * H, gc, 0.0))
    return f


def loss_tile(x, g, target):
    y = rms_tile(x, g)
    err = y - target
    return 0.5 * jnp.sum(jnp.mean(err * err, axis=-1, keepdims=True), axis=0, keepdims=True)


def adamw_tile(w, g, m, v):
    m = ADAM_B1 * m + (1.0 - ADAM_B1) * g
    v = ADAM_B2 * v + (1.0 - ADAM_B2) * (g * g)
    m_hat = m / (1.0 - ADAM_B1 ** ADAM_STEP)
    v_hat = v / (1.0 - ADAM_B2 ** ADAM_STEP)
    delta = -ADAM_LR * (m_hat / (jnp.sqrt(v_hat) + ADAM_EPS) + ADAM_WD * w)
    return delta, m, v


CONV_ROWS = 32


def _shifted_copies(src, sh, rows):
    for b in range(SUBLANE):
        sh[b] = src[pl.ds(b, rows), :]


def _window(sh, off, rows):
    b = off % SUBLANE
    return sh[b, pl.ds(off - b, rows), :]


def _conv_rows(out, sh, w_ref, offsets, rows):
    for r0 in range(0, rows, CONV_ROWS):
        rc = min(CONV_ROWS, rows - r0)
        acc = w_ref[0:1, :] * _window(sh, offsets[0] + r0, rc)
        for k in range(1, len(offsets)):
            acc = acc + w_ref[k:k + 1, :] * _window(sh, offsets[k] + r0, rc)
        out[r0:r0 + rc, :] = acc


def _conv_wgrad(dsrc, d0, sh, offsets, rows):
    dws = []
    for off in offsets:
        acc = None
        for r0 in range(0, rows, CONV_ROWS):
            rc = min(CONV_ROWS, rows - r0)
            prod = dsrc[d0 + r0:d0 + r0 + rc, :] * _window(sh, off + r0, rc)
            for g in range(0, rc, SUBLANE):
                part = prod[g:g + SUBLANE, :]
                acc = part if acc is None else acc + part
        dws.append(jnp.sum(acc, axis=0, keepdims=True))
    return jnp.concatenate(dws, axis=0)


def _conv_specs(T, tr, hb, cw, col_blocks, rev):
    n = T // tr

    def ri(i):
        return (n - 1 - i) if rev else i

    tile_specs, halo_specs = [], []
    for off in col_blocks:
        tile_specs.append(pl.BlockSpec((tr, cw), lambda j, i, off=off: (ri(i), j + off)))
        halo_specs.append(pl.BlockSpec(
            (hb, cw), lambda j, i, off=off: (jnp.maximum(ri(i) * (tr // hb) - 1, 0), j + off)))
    return n, ri, tile_specs, halo_specs


def conv_fwd(name, x, w, consts, pre, post, col_blocks, n_out, K, out_dtype=F32, tr=256, hb=8, cw=512):
    T = x.shape[0]
    C = w.shape[1]
    tr, cw = _tile(T, tr, hb), min(cw, C)
    nb = len(col_blocks)
    n, ri, tile_specs, halo_specs = _conv_specs(T, tr, hb, cw, col_blocks, False)
    w_spec = pl.BlockSpec((K, cw), lambda j, i: (0, j))
    c_specs = [pl.BlockSpec((1, cw), lambda j, i: (0, j)) for _ in consts]

    def body(*refs):
        tiles = [r[...] for r in refs[:nb]]
        halos = [r[...] for r in refs[nb:2 * nb]]
        w_ref = refs[2 * nb]
        cs = [r[...] for r in refs[2 * nb + 1:2 * nb + 1 + len(consts)]]
        o_refs = refs[2 * nb + 1 + len(consts):-3]
        pbuf, shp, cbuf = refs[-3:]
        i = pl.program_id(1)
        pbuf[0:hb, :] = jnp.where(i > 0, pre(*halos), 0.0)
        pbuf[hb:hb + tr, :] = pre(*tiles)
        pbuf[hb + tr:hb + tr + SUBLANE, :] = jnp.zeros((SUBLANE, cw), F32)
        _shifted_copies(pbuf, shp, hb + tr)
        _conv_rows(cbuf, shp, w_ref, [hb - (K - 1) + k for k in range(K)], tr)
        res = post(cbuf[...], pl.program_id(0), *cs)
        for o_ref, r in zip(o_refs, res):
            o_ref[...] = r.astype(o_ref.dtype)

    outs = pl.pallas_call(
        body, name=name, grid=(C // cw, n),
        in_specs=tile_specs + halo_specs + [w_spec] + c_specs,
        out_specs=[pl.BlockSpec((tr, cw), lambda j, i: (i, j)) for _ in range(n_out)],
        out_shape=[jax.ShapeDtypeStruct((T, C), out_dtype) for _ in range(n_out)],
        scratch_shapes=[pltpu.VMEM((hb + tr + SUBLANE, cw), F32), pltpu.VMEM((SUBLANE, hb + tr, cw), F32),
                        pltpu.VMEM((tr, cw), F32)],
        compiler_params=pltpu.CompilerParams(
            dimension_semantics=("parallel", "arbitrary"), vmem_limit_bytes=VMEM_LIMIT),
    )(*([x] * nb), *([x] * nb), w, *consts)
    return outs[0] if n_out == 1 else tuple(outs)


def conv_bwd(name, x, w, consts, grads, pre, post, col_blocks, K, tr=256, hb=8, cw=512, recompute=True):
    T, Cx = x.shape
    C = w.shape[1]
    tr, cw = _tile(T, tr, hb), min(cw, C)
    nb = len(col_blocks)
    n, ri, tile_specs, halo_specs = _conv_specs(T, tr, hb, cw, col_blocks, True)
    w_spec = pl.BlockSpec((K, cw), lambda j, i: (0, j))
    c_specs = [pl.BlockSpec((1, cw), lambda j, i: (0, j)) for _ in consts]
    g_specs = [pl.BlockSpec((tr, cw), lambda j, i: (ri(i), j)) for _ in grads]
    nc, ng = len(consts), len(grads)

    def body(*refs):
        p = 0
        tile_refs = refs[p:p + nb]; p += nb
        halo_refs = refs[p:p + nb]; p += nb
        w_ref = refs[p]; p += 1
        cs = [r[...] for r in refs[p:p + nc]]; p += nc
        gs = [r[...] for r in refs[p:p + ng]]; p += ng
        dx_refs = refs[p:p + nb]; p += nb
        sum_refs = refs[p:p + nb]; p += nb
        dw_ref = refs[p]; p += 1
        dc_refs = refs[p:p + nc]; p += nc
        pbuf, dbuf, ebuf, carry, shp, shd, cbuf = refs[p:p + 7]
        i = pl.program_id(1)
        first = i == 0
        tiles = [r[...] for r in tile_refs]
        halos = [r[...] for r in halo_refs]
        p_tile, vjp_pre = jax.vjp(pre, *tiles)
        pbuf[0:hb, :] = jnp.where(ri(i) > 0, pre(*halos), 0.0)
        pbuf[hb:hb + tr, :] = p_tile
        pbuf[hb + tr:hb + tr + SUBLANE, :] = jnp.zeros((SUBLANE, cw), F32)
        _shifted_copies(pbuf, shp, hb + tr)
        taps = [hb - (K - 1) + k for k in range(K)]
        if recompute:
            _conv_rows(cbuf, shp, w_ref, taps, tr)
            c = cbuf[...]
        else:
            c = jnp.zeros((tr, cw), F32)
        cid = pl.program_id(0)
        _, vjp_post = jax.vjp(lambda c_, *cs_: post(c_, cid, *cs_), c, *cs)
        dres = vjp_post(tuple(g.astype(F32) for g in gs))
        dbuf[0:hb, :] = jnp.zeros((hb, cw), F32)
        dbuf[hb:hb + tr, :] = dres[0]
        dbuf[hb + tr:hb + tr + hb + SUBLANE, :] = jnp.zeros((hb + SUBLANE, cw), F32)
        _shifted_copies(dbuf, shd, hb + tr + hb)
        _conv_rows(ebuf, shd, w_ref, [K - 1 - k for k in range(K)], hb + tr)
        dw = _conv_wgrad(dbuf, hb, shp, taps, tr)

        @pl.when(jnp.logical_not(first))
        def _():
            ebuf[tr:tr + hb, :] += carry[...]

        carry[...] = ebuf[0:hb, :]
        dtiles = vjp_pre(ebuf[hb:hb + tr, :])
        for r, s, d in zip(dx_refs, sum_refs, dtiles):
            r[...] = d.astype(r.dtype)
            ds_ = jnp.sum(d, axis=0, keepdims=True)

            @pl.when(first)
            def _(s=s, ds_=ds_):
                s[...] = ds_

            @pl.when(jnp.logical_not(first))
            def _(s=s, ds_=ds_):
                s[...] += ds_

        accs = [(dw_ref, dw)] + [(r, d) for r, d in zip(dc_refs, dres[1:])]
        for r, d in accs:
            @pl.when(first)
            def _(r=r, d=d):
                r[...] = d

            @pl.when(jnp.logical_not(first))
            def _(r=r, d=d):
                r[...] += d

    n_cb = C // cw
    outs = pl.pallas_call(
        body, name=name, grid=(n_cb, n),
        in_specs=tile_specs + halo_specs + [w_spec] + c_specs + g_specs,
        out_specs=([pl.BlockSpec((tr, cw), lambda j, i: (ri(i), j)) for _ in col_blocks]
                   + [pl.BlockSpec((1, cw), lambda j, i: (0, j)) for _ in col_blocks]
                   + [pl.BlockSpec((K, cw), lambda j, i: (0, j))]
                   + [pl.BlockSpec((1, cw), lambda j, i: (0, j)) for _ in consts]),
        out_shape=([jax.ShapeDtypeStruct((T, C), BF) for _ in col_blocks]
                   + [jax.ShapeDtypeStruct((1, C), F32) for _ in col_blocks]
                   + [jax.ShapeDtypeStruct((K, C), F32)]
                   + [jax.ShapeDtypeStruct((1, C), F32) for _ in consts]),
        scratch_shapes=[pltpu.VMEM((hb + tr + SUBLANE, cw), F32), pltpu.VMEM((hb + tr + hb + SUBLANE, cw), F32),
                        pltpu.VMEM((hb + tr, cw), F32), pltpu.VMEM((hb, cw), F32),
                        pltpu.VMEM((SUBLANE, hb + tr, cw), F32), pltpu.VMEM((SUBLANE, hb + tr + hb, cw), F32),
                        pltpu.VMEM((tr, cw), F32)],
        compiler_params=pltpu.CompilerParams(
            dimension_semantics=("parallel", "arbitrary"), vmem_limit_bytes=VMEM_LIMIT),
    )(*([x] * nb), *([x] * nb), w, *consts, *grads)
    dxs = outs[:nb]
    sums = outs[nb:2 * nb]
    dw = outs[2 * nb]
    dcs = outs[2 * nb + 1:]
    return dxs, sums, dw, dcs


def make_qkv_post(dk, cw, n_qk_chunks):
    def l2(t):
        return t * lax.rsqrt(jnp.sum(t * t, axis=-1, keepdims=True) + EPS)

    def post(c, cid):
        s = jax.nn.silu(c)
        normed = lane_groups(l2, dk, s)
        return (jnp.where(cid < n_qk_chunks, normed, s),)
    return post


def glu_pre(za, zb):
    return za * jax.nn.sigmoid(zb)


def bias_post(c, cid, b):
    return (c + b,)


def _col_to_row(col):
    C = col.shape[-2]
    ii = lax.broadcasted_iota(jnp.int32, (C, C), 0)
    jj = lax.broadcasted_iota(jnp.int32, (C, C), 1)
    wide = jnp.broadcast_to(col, col.shape[:-1] + (C,))
    return jnp.sum(jnp.where(ii == jj, wide, 0.0), axis=-2, keepdims=True)


@jax.custom_vjp
def solve_with_inverse(a, rhs, x):
    return mm(x, rhs, 1, 0, THREE_PASS)


def _swi_fwd(a, rhs, x):
    sol = mm(x, rhs, 1, 0, THREE_PASS)
    return sol, (x, sol)


def _swi_bwd(res, dsol):
    x, sol = res
    drhs = mm(x, dsol, 0, 0, THREE_PASS)
    da = -mm(drhs, sol, 1, 1, THREE_PASS)
    return da, drhs, jnp.zeros_like(x)


solve_with_inverse.defvjp(_swi_fwd, _swi_bwd)


def unit_lower_inverse(a):
    C = a.shape[-1]
    ii = lax.broadcasted_iota(jnp.int32, (C, C), 0)
    jj = lax.broadcasted_iota(jnp.int32, (C, C), 1)
    x = jnp.where(ii == jj, 1.0, 0.0).astype(F32) - a
    p = mm(a, a, 1, 0, THREE_PASS)
    steps = int(math.log2(C)) - 1
    for s in range(steps):
        x = x + mm(x, p, 1, 0, THREE_PASS)
        if s < steps - 1:
            p = mm(p, p, 1, 0, THREE_PASS)
    return x


def dn_masks(C):
    ii = lax.broadcasted_iota(jnp.int32, (C, C), 0)
    jj = lax.broadcasted_iota(jnp.int32, (C, C), 1)
    return ii >= jj, ii > jj


def dn_pre(q, k, v, beta, gc):
    C, dk = q.shape[-2:]
    tri, strict = dn_masks(C)
    q = q * (dk ** -0.5)
    diff = gc - _col_to_row(gc)
    decay = jnp.where(tri, jnp.exp(jnp.where(tri, diff, 0.0)), 0.0)
    kb = k * beta
    vb = v * beta
    a = jnp.where(strict, mm(kb, k, 1, 1) * decay, 0.0)
    eg = jnp.exp(gc)
    rhs = jnp.concatenate([vb, kb * eg], axis=-1)
    attn = mm(q, k, 1, 1) * decay
    qd = q * eg
    g_last = gc[..., C - 1:C, :]
    kt = k * jnp.exp(g_last - gc)
    gl = jnp.exp(g_last)
    return a, rhs, attn, qd, kt, gl


def dn_chunk(q, k, v, beta, gc, state, x):
    dv = v.shape[-1]
    a, rhs, attn, qd, kt, gl = dn_pre(q, k, v, beta, gc)
    sol = solve_with_inverse(a, rhs, x)
    u, w = sol[..., :dv], sol[..., dv:]
    vn = u - mm(w, state, 1, 0)
    o = mm(qd, state, 1, 0) + mm(attn, vn, 1, 0)
    new_state = state * gl + mm(kt, vn, 0, 0)
    return o, new_state


def _by_head(q_ref, k_ref, v_ref, bg, H, dk, dv):
    qv = jnp.stack([q_ref[:, h * dk:(h + 1) * dk] for h in range(H)])
    kv = jnp.stack([k_ref[:, h * dk:(h + 1) * dk] for h in range(H)])
    vv = jnp.stack([v_ref[:, h * dv:(h + 1) * dv] for h in range(H)])
    beta = jnp.stack([bg[:, h:h + 1] for h in range(H)])
    gc = jnp.stack([bg[:, H + h:H + h + 1] for h in range(H)])
    return qv, kv, vv, beta, gc


def deltanet_fwd(qkv, bgc, H):
    T = qkv.shape[0]
    dk = dv = qkv.shape[1] // (3 * H)
    N = T // CHUNK

    def body(q_ref, k_ref, v_ref, bgc_ref, o_ref, x_ref, s_ref, state):
        @pl.when(pl.program_id(0) == 0)
        def _():
            state[...] = jnp.zeros((H, dk, dv), F32)

        qv, kv, vv, beta, gc = _by_head(q_ref, k_ref, v_ref, bgc_ref[...], H, dk, dv)
        a = dn_pre(qv, kv, vv, beta, gc)[0]
        x = unit_lower_inverse(a)
        s = state[...]
        o, s_new = dn_chunk(qv, kv, vv, beta, gc, s, x)
        for h in range(H):
            o_ref[:, h * dv:(h + 1) * dv] = o[h]
        x_ref[...] = x
        s_ref[...] = s
        state[...] = s_new

    return pl.pallas_call(
        body, name="deltanet_fwd", grid=(N,),
        in_specs=[pl.BlockSpec((CHUNK, H * dk), lambda n: (n, 0)),
                  pl.BlockSpec((CHUNK, H * dk), lambda n: (n, 1)),
                  pl.BlockSpec((CHUNK, H * dv), lambda n: (n, 2)),
                  pl.BlockSpec((CHUNK, LANE), lambda n: (n, 0))],
        out_specs=[pl.BlockSpec((CHUNK, H * dv), lambda n: (n, 0)),
                   pl.BlockSpec((None, H, CHUNK, CHUNK), lambda n: (n, 0, 0, 0)),
                   pl.BlockSpec((None, H, dk, dv), lambda n: (n, 0, 0, 0))],
        out_shape=[jax.ShapeDtypeStruct((T, H * dv), F32),
                   jax.ShapeDtypeStruct((N, H, CHUNK, CHUNK), F32),
                   jax.ShapeDtypeStruct((N, H, dk, dv), F32)],
        scratch_shapes=[pltpu.VMEM((H, dk, dv), F32)],
        compiler_params=pltpu.CompilerParams(
            dimension_semantics=("arbitrary",), vmem_limit_bytes=VMEM_LIMIT),
    )(qkv, qkv, qkv, bgc)


def deltanet_bwd(qkv, bgc, xinv, states, do, H):
    T = qkv.shape[0]
    dk = dv = qkv.shape[1] // (3 * H)
    N = T // CHUNK

    def body(q_ref, k_ref, v_ref, bgc_ref, x_ref, s_ref, do_ref, dqkv_ref, dbgc_ref, dstate):
        @pl.when(pl.program_id(0) == 0)
        def _():
            dstate[...] = jnp.zeros((H, dk, dv), F32)

        qv, kv, vv, beta, gc = _by_head(q_ref, k_ref, v_ref, bgc_ref[...], H, dk, dv)
        do = jnp.stack([do_ref[:, h * dv:(h + 1) * dv] for h in range(H)])
        _, vjp = jax.vjp(dn_chunk, qv, kv, vv, beta, gc, s_ref[...], x_ref[...])
        dq, dk_, dv_, dbeta, dgc, ds, _ = vjp((do, dstate[...]))
        dstate[...] = ds
        lane = lax.broadcasted_iota(jnp.int32, (CHUNK, LANE), 1)
        dbgc = jnp.zeros((CHUNK, LANE), F32)
        for h in range(H):
            dqkv_ref[:, h * dk:(h + 1) * dk] = dq[h]
            dqkv_ref[:, (H + h) * dk:(H + h + 1) * dk] = dk_[h]
            dqkv_ref[:, (2 * H + h) * dk:(2 * H + h + 1) * dk] = dv_[h]
            dbgc = dbgc + jnp.where(lane == h, dbeta[h], 0.0) + jnp.where(lane == h + H, dgc[h], 0.0)
        dbgc_ref[...] = dbgc

    rn = lambda n: N - 1 - n
    return pl.pallas_call(
        body, name="deltanet_bwd", grid=(N,),
        in_specs=[pl.BlockSpec((CHUNK, H * dk), lambda n: (rn(n), 0)),
                  pl.BlockSpec((CHUNK, H * dk), lambda n: (rn(n), 1)),
                  pl.BlockSpec((CHUNK, H * dv), lambda n: (rn(n), 2)),
                  pl.BlockSpec((CHUNK, LANE), lambda n: (rn(n), 0)),
                  pl.BlockSpec((None, H, CHUNK, CHUNK), lambda n: (rn(n), 0, 0, 0)),
                  pl.BlockSpec((None, H, dk, dv), lambda n: (rn(n), 0, 0, 0)),
                  pl.BlockSpec((CHUNK, H * dv), lambda n: (rn(n), 0))],
        out_specs=[pl.BlockSpec((CHUNK, 3 * H * dk), lambda n: (rn(n), 0)),
                   pl.BlockSpec((CHUNK, LANE), lambda n: (rn(n), 0))],
        out_shape=[jax.ShapeDtypeStruct((T, 3 * H * dk), F32),
                   jax.ShapeDtypeStruct((T, LANE), F32)],
        scratch_shapes=[pltpu.VMEM((H, dk, dv), F32)],
        compiler_params=pltpu.CompilerParams(
            dimension_semantics=("arbitrary",), vmem_limit_bytes=VMEM_LIMIT),
    )(qkv, qkv, qkv, bgc, xinv, states, do)


def _place():
    x, y, c = lax.axis_index("x"), lax.axis_index("y"), lax.axis_index("c")
    chips = [(1 - x, y), (x, 1 - y), (1 - x, 1 - y)]
    return x, y, c, chips


def all_gather(name, shards):
    na = len(shards)

    def body(*refs):
        ins, outs = refs[:na], refs[na:2 * na]
        send_sems, recv_sems, local_sems = refs[2 * na:]
        x, y, c, chips = _place()
        me, sibling = (x, y, c), (x, y, 1 - c)

        def copy(a, k, block, to, src=None):
            dst = outs[a].at[4 * block[0] + 2 * block[1] + block[2]]
            return pltpu.make_async_remote_copy(
                src_ref=dst if src is None else src, dst_ref=dst,
                send_sem=send_sems.at[a, k], recv_sem=recv_sems.at[a, k],
                device_id=to, device_id_type=MESH)

        mine, first, passed = [], [], []
        for a in range(na):
            cp = pltpu.make_async_copy(ins[a], outs[a].at[4 * x + 2 * y + c], local_sems.at[a])
            cp.start()
            mine.append(cp)
        for a in range(na):
            cps = [copy(a, 0, me, sibling, src=ins[a])]
            cps += [copy(a, 1 + j, me, (*chip, c), src=ins[a]) for j, chip in enumerate(chips)]
            for cp in cps:
                cp.start()
            first += cps
        for a in range(na):
            for j, chip in enumerate(chips):
                copy(a, 1 + j, (*chip, c), me).wait_recv()
                cp = copy(a, 4 + j, (*chip, c), sibling)
                cp.start()
                passed.append(cp)
        for a in range(na):
            copy(a, 0, sibling, me).wait_recv()
            for j, chip in enumerate(chips):
                copy(a, 4 + j, (*chip, 1 - c), me).wait_recv()
        for cp in first + passed:
            cp.wait_send()
        for cp in mine:
            cp.wait()

    outs = pl.pallas_call(
        body, name=name,
        in_specs=[ANY] * na, out_specs=[ANY] * na,
        out_shape=[jax.ShapeDtypeStruct((N_DEV,) + s.shape, s.dtype) for s in shards],
        scratch_shapes=[pltpu.SemaphoreType.DMA((na, 7)), pltpu.SemaphoreType.DMA((na, 7)),
                        pltpu.SemaphoreType.DMA((na,))],
    )(*shards)
    return list(outs)


HBM_SPEC = pl.BlockSpec(memory_space=pltpu.HBM)
SEM_SPEC = pl.BlockSpec(memory_space=pltpu.SEMAPHORE)
EFFECT = pltpu.SideEffectType.DATAFLOW_SIDE_EFFECTING


def _descriptors(plan, bufs, send_sems, recv_sems):
    return [pltpu.make_async_remote_copy(src_ref=src, dst_ref=dst, send_sem=send_sems.at[k],
                                         recv_sem=recv_sems.at[k], device_id=dev, device_id_type=MESH)
            for k, (src, dst, dev) in enumerate(plan(bufs))]


def split_start(name, bufs, plan, n, after):
    nb = len(bufs)

    def body(*refs):
        for cp in _descriptors(plan, refs[:nb], refs[nb + 1], refs[nb + 2]):
            cp.start()
        refs[-1][...] = jnp.zeros((SUBLANE, LANE), F32)

    outs = pl.pallas_call(
        body, name=name,
        out_shape=(pltpu.SemaphoreType.DMA((n,)), pltpu.SemaphoreType.DMA((n,)),
                   *[pltpu.HBM(b.shape, b.dtype) for b in bufs],
                   jax.ShapeDtypeStruct((SUBLANE, LANE), F32)),
        in_specs=[HBM_SPEC] * nb + [ANY],
        out_specs=(SEM_SPEC, SEM_SPEC, *[HBM_SPEC] * nb, pl.BlockSpec(memory_space=pltpu.VMEM)),
        input_output_aliases={i: 2 + i for i in range(nb)},
        compiler_params=pltpu.CompilerParams(has_side_effects=EFFECT),
    )(*[pltpu.with_memory_space_constraint(b, pltpu.HBM) for b in bufs], after)
    return outs[0], outs[1], list(outs[2:2 + nb]), outs[-1]


def split_wait(name, send_sems, recv_sems, bufs, plan, after):
    nb = len(bufs)

    def body(*refs):
        cps = _descriptors(plan, refs[:nb], refs[nb], refs[nb + 1])
        for cp in cps:
            cp.wait_recv()
        for cp in cps:
            cp.wait_send()
        refs[-1][...] = jnp.zeros((SUBLANE, LANE), F32)

    afters = list(after) if isinstance(after, (list, tuple)) else [after]
    outs = pl.pallas_call(
        body, name=name,
        out_shape=[pltpu.HBM(b.shape, b.dtype) for b in bufs] + [jax.ShapeDtypeStruct((SUBLANE, LANE), F32)],
        in_specs=[HBM_SPEC] * nb + [SEM_SPEC, SEM_SPEC] + [ANY] * len(afters),
        out_specs=[HBM_SPEC] * nb + [pl.BlockSpec(memory_space=pltpu.VMEM)],
        input_output_aliases={i: i for i in range(nb)},
        compiler_params=pltpu.CompilerParams(has_side_effects=EFFECT),
    )(*bufs, send_sems, recv_sems, *afters)
    return list(outs[:nb]), outs[-1]


def _block(px, py, pc):
    return 4 * px + 2 * py + pc


def plan_gather_ici(na):
    def plan(bufs):
        x, y, c, chips = _place()
        out = []
        for a in range(na):
            dst = bufs[na + a].at[_block(x, y, c)]
            out.append((bufs[a], dst, (x, y, 1 - c)))
            out += [(bufs[a], dst, (px, py, c)) for px, py in chips]
        return out
    return plan


def plan_gather_pass(na):
    def plan(bufs):
        x, y, c, chips = _place()
        out = []
        for a in range(na):
            for px, py in chips:
                blk = bufs[a].at[_block(px, py, c)]
                out.append((blk, blk, (x, y, 1 - c)))
        return out
    return plan


def plan_reduce_d2d(na):
    def plan(bufs):
        x, y, c, _ = _place()
        return [(bufs[a].at[2 * s + (1 - c)], bufs[na + a].at[s], (x, y, 1 - c))
                for a in range(na) for s in range(4)]
    return plan


def plan_reduce_ici(na):
    def plan(bufs):
        x, y, c, chips = _place()
        return [(bufs[a].at[2 * px + py], bufs[na + a].at[j], (px, py, c))
                for a in range(na) for j, (px, py) in enumerate(chips)]
    return plan


def place_own(name, land, shard, dev):
    r, c = shard.shape
    tr = _tile(r, 512)

    def body(sp_ref, s_ref, land_ref, o_ref):
        o_ref[...] = s_ref[...]

    return pl.pallas_call(
        body, name=name, out_shape=jax.ShapeDtypeStruct(land.shape, land.dtype),
        grid_spec=pltpu.PrefetchScalarGridSpec(
            num_scalar_prefetch=1, grid=(r // tr,),
            in_specs=[pl.BlockSpec((tr, c), lambda i, s: (i, 0)), ANY],
            out_specs=pl.BlockSpec((None, tr, c), lambda i, s: (s[0], i, 0))),
        input_output_aliases={2: 0},
        compiler_params=pltpu.CompilerParams(dimension_semantics=("arbitrary",)),
    )(dev, shard, land)


def pack(arrs, row_mult=SUBLANE):
    pieces = []
    for a in arrs:
        f = a.reshape(-1).astype(F32)
        pad = (-f.shape[0]) % LANE
        if pad:
            f = jnp.concatenate([f, jnp.zeros((pad,), F32)])
        pieces.append(f)
    flat = jnp.concatenate(pieces)
    rows = flat.shape[0] // LANE
    pad_rows = (-rows) % row_mult
    if pad_rows:
        flat = jnp.concatenate([flat, jnp.zeros((pad_rows * LANE,), F32)])
    return flat.reshape(-1, LANE)


def unpack(buf, shapes):
    flat = buf.reshape(-1)
    outs, off = [], 0
    for shp in shapes:
        n = int(np.prod(shp))
        outs.append(flat[off:off + n].reshape(shp))
        off += n + ((-n) % LANE)
    return outs


def _vjp_rows(fn, n_row_in, n_cot):
    def bwd(*args):
        rows = args[:n_row_in]
        cots = args[n_row_in:n_row_in + n_cot]
        consts = args[n_row_in + n_cot:]
        out, vjp = jax.vjp(fn, *rows, *consts)
        if isinstance(out, (tuple, list)):
            cot = tuple(c.astype(o.dtype) for c, o in zip(cots, out))
        else:
            cot = cots[0].astype(out.dtype)
        return vjp(cot)
    return bwd


def rms_bwd(name, x, g, dh, dres, after=None):
    D = x.shape[1]
    vj = _vjp_rows(rms_tile, 1, 1)

    def f(x_, dh_, dres_, g_):
        dx, dg = vj(x_, dh_, g_)
        dx = dx + dres_
        return dx, dx, dg
    return rowcall(name, f, [x, dh, dres], [g], [(D, F32), (D, BF)], [(1, D)], after=after)


def kernel(x, e_norm, e_w_in, e_conv_w, e_a_log, e_dt_bias, e_o_norm, e_ln_g, e_ln_b, e_w_s, e_b_s, e_w_out, o_norm, o_pw1, o_pw1_b, o_dw, o_dw_b, o_ln_g, o_ln_b, o_pw2, o_pw2_b, f_norm, f_w1, f_w2, final_norm, loss_target, m_e_norm, m_e_w_in, m_e_conv_w, m_e_a_log, m_e_dt_bias, m_e_o_norm, m_e_ln_g, m_e_ln_b, m_e_w_s, m_e_b_s, m_e_w_out, m_o_norm, m_o_pw1, m_o_pw1_b, m_o_dw, m_o_dw_b, m_o_ln_g, m_o_ln_b, m_o_pw2, m_o_pw2_b, m_f_norm, m_f_w1, m_f_w2, m_final_norm, v_e_norm, v_e_w_in, v_e_conv_w, v_e_a_log, v_e_dt_bias, v_e_o_norm, v_e_ln_g, v_e_ln_b, v_e_w_s, v_e_b_s, v_e_w_out, v_o_norm, v_o_pw1, v_o_pw1_b, v_o_dw, v_o_dw_b, v_o_ln_g, v_o_ln_b, v_o_pw2, v_o_pw2_b, v_f_norm, v_f_w1, v_f_w2, v_final_norm):
    names = ['e_norm', 'e_w_in', 'e_conv_w', 'e_a_log', 'e_dt_bias', 'e_o_norm', 'e_ln_g', 'e_ln_b', 'e_w_s', 'e_b_s', 'e_w_out', 'o_norm', 'o_pw1', 'o_pw1_b', 'o_dw', 'o_dw_b', 'o_ln_g', 'o_ln_b', 'o_pw2', 'o_pw2_b', 'f_norm', 'f_w1', 'f_w2', 'final_norm']
    W = dict(zip(names, [e_norm, e_w_in, e_conv_w, e_a_log, e_dt_bias, e_o_norm, e_ln_g, e_ln_b, e_w_s, e_b_s, e_w_out, o_norm, o_pw1, o_pw1_b, o_dw, o_dw_b, o_ln_g, o_ln_b, o_pw2, o_pw2_b, f_norm, f_w1, f_w2, final_norm]))
    Mo = dict(zip(names, [m_e_norm, m_e_w_in, m_e_conv_w, m_e_a_log, m_e_dt_bias, m_e_o_norm, m_e_ln_g, m_e_ln_b, m_e_w_s, m_e_b_s, m_e_w_out, m_o_norm, m_o_pw1, m_o_pw1_b, m_o_dw, m_o_dw_b, m_o_ln_g, m_o_ln_b, m_o_pw2, m_o_pw2_b, m_f_norm, m_f_w1, m_f_w2, m_final_norm]))
    Vo = dict(zip(names, [v_e_norm, v_e_w_in, v_e_conv_w, v_e_a_log, v_e_dt_bias, v_e_o_norm, v_e_ln_g, v_e_ln_b, v_e_w_s, v_e_b_s, v_e_w_out, v_o_norm, v_o_pw1, v_o_pw1_b, v_o_dw, v_o_dw_b, v_o_ln_g, v_o_ln_b, v_o_pw2, v_o_pw2_b, v_f_norm, v_f_w1, v_f_w2, v_final_norm]))

    T, D = x.shape[1], x.shape[2]
    H = e_a_log.shape[-1]
    dv = e_o_norm.shape[-1]
    dk = dv
    G = e_w_s.shape[1]
    AQK, AV, BW = H * dk, H * dv, e_ln_g.shape[-1]
    AQKV = 2 * AQK + AV
    in_cols = AQKV + AV + 2 * H + 2 * BW
    KA = e_conv_w.shape[1]
    KC = o_dw.shape[1]
    L = f_norm.shape[0]
    dev = 4 * lax.axis_index("x") + 2 * lax.axis_index("y") + lax.axis_index("c")
    x2d = x.reshape(T, D)
    tgt = loss_target.reshape(T, D)

    def tie(small, tok):
        return small + tok[0:1, 0:1]

    dev_sp = dev.astype(jnp.int32).reshape(1)
    where = jnp.stack([lax.axis_index("c"), 2 * lax.axis_index("x") + lax.axis_index("y")]).astype(jnp.int32)
    row = lambda a: a.reshape(1, -1).astype(F32)
    en_row = row(e_norm)

    def gather_begin(tag, shards, after):
        na = len(shards)
        lands = [lax.empty((N_DEV,) + s.shape, s.dtype) for s in shards]
        ss, rs, bufs, tok = split_start(f"gather{tag}_ici_start", shards + lands, plan_gather_ici(na), 4 * na, after)
        return (na, ss, rs, bufs), tok

    def gather_pass(tag, h, after):
        na, ss, rs, bufs = h
        bufs, tok = split_wait(f"gather{tag}_ici_wait", ss, rs, bufs, plan_gather_ici(na), after)
        ss, rs, lands, tok = split_start(f"gather{tag}_pass_start", bufs[na:], plan_gather_pass(na), 3 * na, tok)
        return (na, ss, rs, bufs[:na], lands), tok

    def gather_end(tag, h, after):
        na, ss, rs, shards, lands = h
        lands, _ = split_wait(f"gather{tag}_pass_wait", ss, rs, lands, plan_gather_pass(na), after)
        return [place_own(f"gather{tag}_own{a}", lands[a], shards[a], dev_sp) for a in range(na)]

    small_sharded = ['e_conv_w', 'o_norm', 'o_pw1_b', 'o_dw', 'o_dw_b', 'o_ln_g', 'o_ln_b', 'o_pw2_b']
    sm = all_gather("gather_small", [pack([W[n][0]]) for n in small_sharded])

    bfw = lambda w: w.astype(BF)
    hA0, tok = gather_begin("0", [bfw(jnp.swapaxes(e_w_in[0], 0, 1))], sm[0])
    hA1, tok = gather_begin("1", [bfw(e_w_out[0]), bfw(f_w1[0]), bfw(f_w2[0])], tok)
    hA2, tok = gather_begin("2", [bfw(o_pw1[0]), bfw(o_pw2[0])], tok)
    hA3, tok = gather_begin("3", [bfw(f_w1[1]), bfw(f_w2[1])], tok)
    h0 = rowcall("rms_e", rms_tile, [x2d], [en_row], [(D, BF)], after=tok)

    full = {}
    for n, g in zip(small_sharded, sm):
        shp = W[n][0].shape
        blocks = [unpack(g[d], [shp])[0] for d in range(N_DEV)]
        full[n] = jnp.concatenate(blocks, axis=-1)
    conv_w = full['e_conv_w']
    on_row, pw1_b_row = row(full['o_norm']), row(full['o_pw1_b'])
    dw_w, dw_b_row = full['o_dw'], row(full['o_dw_b'])
    oln_g_row, oln_b_row, pw2_b_row = row(full['o_ln_g']), row(full['o_ln_b']), row(full['o_pw2_b'])
    small_names = ['e_norm', 'e_conv_w', 'e_a_log', 'e_dt_bias', 'e_o_norm', 'e_ln_g', 'e_ln_b', 'e_w_s', 'e_b_s',
                   'o_norm', 'o_pw1_b', 'o_dw', 'o_dw_b', 'o_ln_g', 'o_ln_b', 'o_pw2_b', 'f_norm', 'final_norm']
    packed_wmv = [pack([A[n] for n in small_names], 256) for A in (W, Mo, Vo)]

    hB0, tok = gather_pass("0", hA0, [h0, conv_w, dw_w, pw1_b_row] + packed_wmv)
    (g_win,) = gather_end("0", hB0, tok)
    win_t = g_win.reshape(in_cols, D)
    wt_qkv, wt_z = win_t[:AQKV], win_t[AQKV:AQKV + AV]
    wt_ba = jnp.pad(win_t[AQKV + AV:AQKV + AV + 2 * H], ((0, LANE - 2 * H), (0, 0)))
    wt_uv = win_t[AQKV + AV + 2 * H:]

    alog_row = jnp.pad(row(e_a_log), ((0, 0), (H, LANE - 2 * H)))
    dtb_row = jnp.pad(row(e_dt_bias), ((0, 0), (H, LANE - 2 * H)))
    eon_row = row(e_o_norm)
    eln_g_row, eln_b_row = row(e_ln_g), row(e_ln_b)
    w_s = e_w_s[0]
    bs_t = e_b_s[0].T
    fn_rows = [row(f_norm[l]) for l in range(L)]
    fin_row = row(final_norm)

    qkv_raw = matmul("proj_qkv", h0, wt_qkv, "nt")
    z_gate = matmul("proj_z", h0, wt_z, "nt")
    ba = matmul("proj_ba", h0, wt_ba, "nt")
    uv = matmul("proj_uv", h0, wt_uv, "nt")

    cwa = min(512, AQKV)
    qkv_post = make_qkv_post(dk, cwa, 2 * AQK // cwa)
    ident = lambda t: t
    qkv = conv_fwd("qkv_conv", qkv_raw, conv_w, [], ident, qkv_post, [0], 1, KA, cw=cwa, tr=512)
    bgc_fn = make_bgc(H)
    bgc = rowcall("bgc", bgc_fn, [ba], [alog_row, dtb_row], [(LANE, F32)])
    o_dn, xinv, states = deltanet_fwd(qkv, bgc, H)
    hB1, tok = gather_pass("1", hA1, o_dn)

    def mix_tile(o, z, uv_, o_norm_, ln_g, ln_b, w_s_, bs_t_):
        return jnp.concatenate([mixa_post_tile(o, z, o_norm_), mixb_tile(uv_, ln_g, ln_b, w_s_, bs_t_)], axis=-1)
    mix_consts = [eon_row, eln_g_row, eln_b_row, w_s, bs_t]
    mix = rowcall("mix", mix_tile, [o_dn, z_gate, uv], mix_consts, [(AV + BW, BF)], after=tok)
    g_wout, g_w1_0, g_w2_0 = gather_end("1", hB1, mix)
    wout = g_wout.reshape(-1, D)
    w1 = [jnp.moveaxis(g_w1_0, 0, 1).reshape(D, -1), None]
    w2 = [g_w2_0.reshape(-1, D), None]
    add_epi = lambda acc, r: (acc + r,)
    x1 = matmul("out_proj", mix, wout, "nn", epi=add_epi, extras=[x2d], tm=512, tn=2048)

    def relu2_epi(acc):
        r = jnp.maximum(acc, 0.0)
        return r * r, r

    hB2, tok = gather_pass("2", hA2, x1)
    a2_0, ar_0, hf0 = norm_matmul("ffn_up0", x1, fn_rows[0], g_w1_0, epi=relu2_epi, out_dtypes=(BF, BF), after=tok)
    g_pw1, g_pw2 = gather_end("2", hB2, a2_0)
    x2 = matmul("ffn_down0", a2_0, w2[0], "nn", epi=add_epi, extras=[x1])
    ffn0 = (hf0, a2_0, ar_0)
    pw1 = jnp.moveaxis(g_pw1, 0, 1).reshape(D, 2 * D)
    pw2 = g_pw2.reshape(D, D)

    bias_epi = lambda acc, b: (acc + b,)
    zc, h1 = norm_matmul("pw1", x2, on_row, g_pw1, epi=bias_epi, extras=[pw1_b_row])
    hB3, tok = gather_pass("3", hA3, zc)
    cwc = min(512, D)
    ncb = D // cwc
    cconv = conv_fwd("dw_conv", zc, dw_w, [tie(dw_b_row, tok)], glu_pre, bias_post, [0, ncb], 1, KC, hb=32, cw=cwc,
                     tr=512)
    g_w1_1, g_w2_1 = gather_end("3", hB3, cconv)
    w1[1] = jnp.moveaxis(g_w1_1, 0, 1).reshape(D, -1)
    w2[1] = g_w2_1.reshape(-1, D)
    ln_silu = lambda c, g, b: jax.nn.silu(ln_tile(c, g, b))
    s_act = rowcall("ln_silu", ln_silu, [cconv], [oln_g_row, oln_b_row], [(D, BF)])
    x3 = matmul("pw2", s_act, pw2, "nn", epi=lambda acc, r, b: (r + (acc + b),), extras=[x2, pw2_b_row],
                tm=512, tn=2048)
    a2_1, ar_1, hf1 = norm_matmul("ffn_up1", x3, fn_rows[1], g_w1_1, epi=relu2_epi, out_dtypes=(BF, BF))
    x4 = matmul("ffn_down1", a2_1, w2[1], "nn", epi=add_epi, extras=[x3])
    ffn1 = (hf1, a2_1, ar_1)

    def loss_bwd_tile(x_, t_, g_):
        l, vjp = jax.vjp(lambda a, b: loss_tile(a, b, t_), x_, g_)
        dx, dg = vjp(jnp.ones_like(l))
        return dx, dx, l, dg
    dx4, dx4_b, loss_part, d_final = rowcall("loss_head", loss_bwd_tile, [x4, tgt], [fin_row],
                                             [(D, F32), (D, BF)], [(1, 1), (1, D)])
    loss = lax.psum(loss_part[0, 0], ("x", "y", "c"))

    def reduce_begin(tag, grads, after):
        na = len(grads)
        lands = [lax.empty((4,) + g.shape[1:], g.dtype) for g in grads]
        ss, rs, bufs, tok = split_start(f"reduce{tag}_d2d_start", grads + lands, plan_reduce_d2d(na), 4 * na, after)
        return (na, ss, rs, bufs), tok

    def reduce_mid(tag, h, after):
        na, ss, rs, bufs = h
        bufs, tok = split_wait(f"reduce{tag}_d2d_wait", ss, rs, bufs, plan_reduce_d2d(na), after)
        parts = []
        for a, (g, rc) in enumerate(zip(bufs[:na], bufs[na:])):
            r, c = g.shape[1], g.shape[2]
            if r % SUBLANE:
                parts.append(slabcall(f"chip_sum{tag}_{a}", lambda p, q: (p + q,),
                                      [(g, lambda i, s: 2 * i + s[0]), (rc, lambda i, s: i)], [BF], where, 4))
                continue
            mine = lambda i, n, s: (2 * (i // (n // 4)) + s[0]) * (n // 4) + i % (n // 4)
            parts.append(rowcall(f"chip_sum{tag}_{a}", lambda p, q: (p + q,),
                                 [(g.reshape(N_DEV * r, c), mine), rc.reshape(4 * r, c)], [],
                                 [(c, BF)], tr=_tile(r, 512), sp=where, R=4 * r).reshape(4, r, c))
        lands = [lax.empty((3,) + p.shape[1:], p.dtype) for p in parts]
        ss, rs, bufs, tok = split_start(f"reduce{tag}_ici_start", parts + lands, plan_reduce_ici(na), 3 * na, tok)
        return (na, ss, rs, bufs), tok

    res = {}

    def reduce_end(tag, h, after, targets):
        na, ss, rs, bufs = h
        bufs, _ = split_wait(f"reduce{tag}_ici_wait", ss, rs, bufs, plan_reduce_ici(na), after)
        for a, (part, fin, (n, l)) in enumerate(zip(bufs[:na], bufs[na:], targets)):
            def f(p0, p1, p2, p3, w_, m_, v_):
                g = ((p0.astype(F32) + p1.astype(F32)) + p2.astype(F32)) + p3.astype(F32)
                return (g,) + adamw_tile(w_, g, m_, v_)
            r, C = fin.shape[-2], fin.shape[-1]
            if W[n].shape[1:] == (C, r):
                t = lambda arr: jnp.swapaxes(arr[l], 0, 1)
                outs = slabcall(f"adam{tag}_{a}", f, [(part, lambda i, s: s[1]), (fin, 0), (fin, 1), (fin, 2),
                                                      (t(W[n]), None), (t(Mo[n]), None), (t(Vo[n]), None)],
                                [F32] * 4, where)
                res[n] = tuple(jnp.swapaxes(o, 0, 1)[None] for o in outs)
                continue
            own = lambda i, n_, s: s[1] * n_ + i
            res[n] = rowcall(f"adam{tag}_{a}", f, [(part.reshape(4 * r, C), own), (fin, 0), (fin, 1), (fin, 2),
                                                   (W[n], l), (Mo[n], l), (Vo[n], l)], [],
                             [(C, F32)] * 4, tr=256, sp=where, R=r,
                             out_lead=(W[n].shape[0], l), into=res.get(n))

    dscale_epi = lambda acc, r: (acc * (2.0 * r.astype(F32)),)
    d_fnorm = [None] * L

    dpre1 = matmul("ffn_down_dx1", dx4_b, w2[1], "nt", epi=dscale_epi, extras=[ar_1], out_dtypes=(BF,))
    dw2_1 = matmul("ffn_down_dw1", a2_1, dx4_b, "tn")
    dw1_1 = matmul("ffn_up_dw1", hf1, dpre1, "tn", colshard=True)
    hD1, tok = reduce_begin("1", [dw1_1, dw2_1.reshape(N_DEV, -1, D)], dpre1)
    dhf1 = matmul("ffn_up_dx1", dpre1, w1[1], "nt", after=tok, tk=4096)
    dx3, dx3_b, d_fnorm[1] = rms_bwd("rms_f_bwd1", x3, fn_rows[1], dhf1, dx4)

    ds_act = matmul("pw2_dx", dx3_b, pw2, "nt")
    hI1, tok = reduce_mid("1", hD1, ds_act)
    d_pw2 = matmul("pw2_dw", s_act, dx3_b, "tn", after=tok)
    ln_silu_bwd = _vjp_rows(ln_silu, 1, 1)

    def ln_silu_bwd_tile(c_, ds_, dx3_, g_, b_):
        dc, dg, db = ln_silu_bwd(c_, ds_, g_, b_)
        return dc, dg, db, jnp.sum(dx3_, axis=0, keepdims=True)
    dcconv, d_oln_g, d_oln_b, d_pw2_b = rowcall(
        "ln_silu_bwd", ln_silu_bwd_tile, [cconv, ds_act, dx3], [oln_g_row, oln_b_row],
        [(D, F32)], [(1, D), (1, D), (1, D)])
    (dza, dzb), (sza, szb), d_dw, (d_dw_b,) = conv_bwd(
        "dw_conv_bwd", zc, dw_w, [tie(dw_b_row, tok)], [dcconv], glu_pre, bias_post, [0, ncb], KC, hb=32, cw=cwc, tr=256,
        recompute=False)
    dzc = jnp.concatenate([dza, dzb], axis=-1)
    d_pw1_b = jnp.concatenate([sza, szb], axis=-1)
    d_pw1 = matmul("pw1_dw", h1, dzc, "tn", colshard=True)
    dh1 = matmul("pw1_dx", dzc, pw1, "nt")
    dx2, dx2_b, d_onorm = rms_bwd("rms_o_bwd", x2, on_row, dh1, dx3)
    reduce_end("1", hI1, dx2, [('f_w1', 1), ('f_w2', 1)])

    hD2, tok = reduce_begin("2", [d_pw1, d_pw2.reshape(N_DEV, -1, D)], dx2)
    dpre0 = matmul("ffn_down_dx0", dx2_b, w2[0], "nt", epi=dscale_epi, extras=[ar_0], out_dtypes=(BF,), after=tok)
    dw2_0 = matmul("ffn_down_dw0", a2_0, dx2_b, "tn")
    hI2, tok = reduce_mid("2", hD2, dw2_0)
    dw1_0 = matmul("ffn_up_dw0", hf0, dpre0, "tn", colshard=True, after=tok)
    dhf0 = matmul("ffn_up_dx0", dpre0, w1[0], "nt", tk=4096)
    dx1, dx1_b, d_fnorm[0] = rms_bwd("rms_f_bwd0", x1, fn_rows[0], dhf0, dx2)
    reduce_end("2", hI2, dx1, [('o_pw1', 0), ('o_pw2', 0)])

    dmix = matmul("out_proj_dx", dx1_b, wout, "nt")
    d_wout = matmul("out_proj_dw", mix, dx1_b, "tn")
    hD3, tok = reduce_begin("3", [dw1_0, dw2_0.reshape(N_DEV, -1, D), d_wout.reshape(N_DEV, -1, D)], dmix)
    do_dn, dz_gate, duv, d_eon, d_eln_g, d_eln_b, d_ws, d_bs_t = rowcall(
        "mix_bwd", _vjp_rows(mix_tile, 3, 1), [o_dn, z_gate, uv, dmix], mix_consts,
        [(AV, F32), (AV, BF), (2 * BW, BF)], [(1, dv), (1, BW), (1, BW), w_s.shape, bs_t.shape], after=tok)
    dqkv, dbgc = deltanet_bwd(qkv, bgc, xinv, states, do_dn, H)
    hI3, tok = reduce_mid("3", hD3, dbgc)
    bgc_bwd = _vjp_rows(bgc_fn, 1, 1)
    dba, d_alog_row, d_dtb_row = rowcall(
        "bgc_bwd", bgc_bwd, [ba, dbgc], [alog_row, dtb_row], [(LANE, BF)], [(1, LANE), (1, LANE)])
    (dqkv_raw,), _, d_conv_w, _ = conv_bwd(
        "qkv_conv_bwd", qkv_raw, tie(conv_w, tok), [], [dqkv], ident, qkv_post, [0], KA, cw=cwa, tr=512)

    dw_qkv = matmul("proj_qkv_dw", dqkv_raw, h0, "tn")
    dw_z = matmul("proj_z_dw", dz_gate, h0, "tn")
    dw_ba = matmul("proj_ba_dw", dba, h0, "tn")
    dw_uv = matmul("proj_uv_dw", duv, h0, "tn")
    d_win_t = jnp.concatenate([dw_qkv, dw_z, dw_ba[:2 * H], dw_uv], axis=0)
    G_win = d_win_t.reshape(N_DEV, in_cols // N_DEV, D)
    hD4, tok = reduce_begin("4", [G_win], dw_uv)
    reduce_end("3", hI3, tok, [('f_w1', 0), ('f_w2', 0), ('e_w_out', 0)])
    hI4, tok = reduce_mid("4", hD4, [res[n][0] for n in ('f_w1', 'f_w2', 'o_pw1', 'e_w_out')])
    dh0 = matmul_sum("proj_dx", [(dqkv_raw, wt_qkv), (dz_gate, wt_z), (dba, wt_ba), (duv, wt_uv)], after=tok)
    grad_x, _, d_enorm = rms_bwd("rms_e_bwd", x2d, en_row, dh0, dx1, after=tok)

    d_alog = d_alog_row[:, H:2 * H]
    d_dtb = d_dtb_row[:, H:2 * H]
    small_grads = [d_enorm, d_conv_w, d_alog, d_dtb, d_eon, d_eln_g, d_eln_b, d_ws, d_bs_t.T,
                   d_onorm, d_pw1_b, d_dw, d_dw_b, d_oln_g, d_oln_b, d_pw2_b,
                   jnp.concatenate(d_fnorm, axis=0), d_final]
    full_shapes = [g.shape for g in small_grads]
    gs_all = all_gather("gather_small_grads", [pack(small_grads, 256)])[0]

    def sum8(*ps):
        s = ps[0]
        for p in ps[1:]:
            s = s + p
        return (s,)
    gs_sum = rowcall("small_sum", sum8, [(gs_all, d) for d in range(N_DEV)], [], [(LANE, F32)])
    reduce_end("4", hI4, gs_sum, [('e_w_in', 0)])
    g_full = dict(zip(small_names, unpack(gs_sum, full_shapes)))
    g_loc = {}
    for n in small_names:
        g = g_full[n]
        if n in small_sharded:
            per = g.shape[-1] // N_DEV
            g = lax.dynamic_slice_in_dim(g, dev * per, per, axis=-1)
        g_loc[n] = g.reshape(W[n].shape)
    packs = [packed_wmv[0], pack([g_loc[n] for n in small_names], 256), packed_wmv[1], packed_wmv[2]]
    d_s, m_s, v_s = rowcall("adam_small", adamw_tile, packs, [], [(LANE, F32)] * 3)
    shapes = [W[n].shape for n in small_names]
    for n, d_, m_, v_ in zip(small_names, unpack(d_s, shapes), unpack(m_s, shapes), unpack(v_s, shapes)):
        res[n] = (g_loc[n], d_, m_, v_)

    grads = [res[n][0] for n in names]
    deltas = [res[n][1] for n in names]
    new_m = [res[n][2] for n in names]
    new_v = [res[n][3] for n in names]
    return (loss, grad_x.reshape(x.shape), *grads, *deltas, *new_m, *new_v)
```

```python
import functools
import math

import jax
import jax.numpy as jnp
import numpy as np
from jax import lax
from jax.experimental import pallas as pl
from jax.experimental.pallas import tpu as pltpu

F32 = jnp.float32
BF = jnp.bfloat16
EPS = 1e-6
CHUNK = 64
B_BLOCK = 128
LANE = 128
SUBLANE = 8
N_DEV = 8
VMEM_LIMIT = 56 * 1024 * 1024

ADAM_LR = 0.001
ADAM_B1 = 0.9
ADAM_B2 = 0.999
ADAM_EPS = 1e-08
ADAM_WD = 0.01
ADAM_STEP = 10

MESH = pl.DeviceIdType.MESH
ANY = pl.BlockSpec(memory_space=pl.ANY)


def _tile(n, pref, mult=SUBLANE):
    if n <= pref:
        return n
    t = (pref // mult) * mult
    while t >= mult:
        if n % t == 0:
            return t
        t -= mult
    return n


THREE_PASS = 3


def _dg(a, b, ca, cb, hi):
    nb = a.ndim - 2
    batch = tuple(range(nb))
    dims = (((ca + nb,), (cb + nb,)), (batch, batch))
    if hi == THREE_PASS:
        a, b = a.astype(F32), b.astype(F32)
        ah, bh = a.astype(BF), b.astype(BF)
        al, bl = (a - ah.astype(F32)).astype(BF), (b - bh.astype(F32)).astype(BF)
        dot = lambda p, q: lax.dot_general(p, q, dims, preferred_element_type=F32)
        return dot(ah, bh) + (dot(ah, bl) + dot(al, bh))
    if hi:
        return lax.dot_general(a.astype(F32), b.astype(F32), dims,
                               precision=lax.Precision.HIGHEST, preferred_element_type=F32)
    return lax.dot_general(a.astype(BF), b.astype(BF), dims, preferred_element_type=F32)


@functools.partial(jax.custom_vjp, nondiff_argnums=(2, 3, 4))
def mm(a, b, ca, cb, hi=False):
    return _dg(a, b, ca, cb, hi)


def _mm_fwd(a, b, ca, cb, hi):
    return _dg(a, b, ca, cb, hi), (a, b)


def _mm_bwd(ca, cb, hi, res, g):
    a, b = res
    if ca == 1:
        da = mm(g, b, 1, 1 - cb, hi)
    else:
        da = mm(b, g, 1 - cb, 1, hi)
    if cb == 0:
        db = mm(a, g, 1 - ca, 0, hi)
    else:
        db = mm(g, a, 0, 1 - ca, hi)
    return da.astype(a.dtype), db.astype(b.dtype)


mm.defvjp(_mm_fwd, _mm_bwd)


def matmul(name, a, b, mode, epi=None, extras=(), out_dtypes=(F32,), colshard=False,
           tm=None, tn=1024, tk=2048, after=None):
    afters = [] if after is None else [after]
    slabs = b.ndim == 3
    if slabs:
        assert mode in ("nn", "nt") and not colshard, name
        b_rows, b_cols = b.shape[1], b.shape[0] * b.shape[2]
    else:
        b_rows, b_cols = b.shape
    if mode == "nn":
        (M, K), (K2, N) = a.shape, (b_rows, b_cols)
    elif mode == "nt":
        (M, K), (N, K2) = a.shape, (b_rows, b_cols)
    else:
        (K, M), (K2, N) = a.shape, b.shape
    assert K == K2, (name, a.shape, b.shape, mode)
    if tm is None:
        tm = 1024
    tm = _tile(M, tm)
    tn = N // N_DEV if colshard else _tile(N, tn, LANE)
    tk = _tile(K, tk, LANE)
    if slabs and mode == "nn":
        tn = b.shape[2]
    if slabs and mode == "nt":
        tk = b.shape[2]
    nk = K // tk
    grid = (M // tm, N // tn, nk)
    if mode == "nn":
        a_spec = pl.BlockSpec((tm, tk), lambda i, j, k: (i, k))
        b_spec = (pl.BlockSpec((None, tk, tn), lambda i, j, k: (j, k, 0)) if slabs
                  else pl.BlockSpec((tk, tn), lambda i, j, k: (k, j)))
        ca, cb = 1, 0
    elif mode == "nt":
        a_spec = pl.BlockSpec((tm, tk), lambda i, j, k: (i, k))
        b_spec = (pl.BlockSpec((None, tn, tk), lambda i, j, k: (k, j, 0)) if slabs
                  else pl.BlockSpec((tn, tk), lambda i, j, k: (j, k)))
        ca, cb = 1, 1
    else:
        a_spec = pl.BlockSpec((tk, tm), lambda i, j, k: (k, i))
        b_spec = pl.BlockSpec((tk, tn), lambda i, j, k: (k, j))
        ca, cb = 0, 0
    ex_specs = []
    for e in extras:
        if e.shape[0] == 1:
            ex_specs.append(pl.BlockSpec((1, tn), lambda i, j, k: (0, j)))
        else:
            assert e.shape == (M, N), (name, e.shape)
            ex_specs.append(pl.BlockSpec((tm, tn), lambda i, j, k: (i, j)))
    if colshard:
        out_shape = [jax.ShapeDtypeStruct((N_DEV, M, tn), dt) for dt in out_dtypes]
        out_specs = [pl.BlockSpec((None, tm, tn), lambda i, j, k: (j, i, 0)) for _ in out_dtypes]
    else:
        out_shape = [jax.ShapeDtypeStruct((M, N), dt) for dt in out_dtypes]
        out_specs = [pl.BlockSpec((tm, tn), lambda i, j, k: (i, j)) for _ in out_dtypes]
    n_ex, n_out = len(extras), len(out_dtypes)

    def body(*refs):
        a_ref, b_ref = refs[0], refs[1]
        ex_refs = refs[2:2 + n_ex]
        first_out = 2 + n_ex + len(afters)
        o_refs = refs[first_out:first_out + n_out]
        part = _dg(a_ref[...], b_ref[...], ca, cb, False)

        def finish(acc):
            res = (acc,) if epi is None else epi(acc, *[r[...] for r in ex_refs])
            for o_ref, r in zip(o_refs, res):
                o_ref[...] = r.astype(o_ref.dtype)

        if nk == 1:
            finish(part)
            return
        acc_ref = refs[-1]
        k = pl.program_id(2)

        @pl.when(k == 0)
        def _():
            acc_ref[...] = part

        @pl.when(k > 0)
        def _():
            acc_ref[...] += part

        @pl.when(k == nk - 1)
        def _():
            finish(acc_ref[...])

    outs = pl.pallas_call(
        body, name=name, grid=grid,
        in_specs=[a_spec, b_spec] + ex_specs + [ANY] * len(afters),
        out_specs=out_specs, out_shape=out_shape,
        scratch_shapes=[pltpu.VMEM((tm, tn), F32)] if nk > 1 else [],
        compiler_params=pltpu.CompilerParams(
            dimension_semantics=("parallel", "parallel", "arbitrary"),
            vmem_limit_bytes=VMEM_LIMIT),
    )(a, b, *extras, *afters)
    return outs[0] if n_out == 1 else tuple(outs)


def norm_matmul(name, x, g, b, epi=None, extras=(), out_dtypes=(F32,), tm=1024, tn=1024, after=None):
    M, K = x.shape
    slabs = b.ndim == 3
    N = b.shape[0] * b.shape[2] if slabs else b.shape[1]
    tm, tn = _tile(M, tm), (b.shape[2] if slabs else _tile(N, tn, LANE))
    b_spec = (pl.BlockSpec((None, K, tn), lambda i, j: (j, 0, 0)) if slabs
              else pl.BlockSpec((K, tn), lambda i, j: (0, j)))
    afters = [] if after is None else [after]
    ex_specs = [pl.BlockSpec((1, tn), lambda i, j: (0, j)) if e.shape[0] == 1
                else pl.BlockSpec((tm, tn), lambda i, j: (i, j)) for e in extras]
    n_ex, n_out = len(extras), len(out_dtypes)

    def body(*refs):
        x_ref, g_ref, b_ref = refs[:3]
        ex_refs = refs[3:3 + n_ex]
        first_out = 3 + n_ex + len(afters)
        o_refs = refs[first_out:first_out + n_out]
        h_ref, h_scr = refs[first_out + n_out], refs[-1]

        @pl.when(pl.program_id(1) == 0)
        def _():
            h = rms_tile(x_ref[...], g_ref[...]).astype(h_scr.dtype)
            h_scr[...] = h
            h_ref[...] = h

        acc = _dg(h_scr[...], b_ref[...], 1, 0, False)
        res = (acc,) if epi is None else epi(acc, *[r[...] for r in ex_refs])
        for o_ref, r in zip(o_refs, res):
            o_ref[...] = r.astype(o_ref.dtype)

    outs = pl.pallas_call(
        body, name=name, grid=(M // tm, N // tn),
        in_specs=[pl.BlockSpec((tm, K), lambda i, j: (i, 0)), pl.BlockSpec((1, K), lambda i, j: (0, 0)),
                  b_spec] + ex_specs + [ANY] * len(afters),
        out_specs=[pl.BlockSpec((tm, tn), lambda i, j: (i, j)) for _ in out_dtypes]
        + [pl.BlockSpec((tm, K), lambda i, j: (i, 0))],
        out_shape=[jax.ShapeDtypeStruct((M, N), dt) for dt in out_dtypes] + [jax.ShapeDtypeStruct((M, K), BF)],
        scratch_shapes=[pltpu.VMEM((tm, K), BF)],
        compiler_params=pltpu.CompilerParams(
            dimension_semantics=("parallel", "arbitrary"), vmem_limit_bytes=VMEM_LIMIT),
    )(x, g, b, *extras, *afters)
    return tuple(outs)


def matmul_sum(name, pairs, tm=512, tn=1024, after=None):
    M, N = pairs[0][0].shape[0], pairs[0][1].shape[1]
    tm, tn = _tile(M, tm), _tile(N, tn, LANE)
    afters = [] if after is None else [after]
    in_specs, operands = [], []
    for a, b in pairs:
        assert a.shape == (M, b.shape[0]) and b.shape[1] == N, (name, a.shape, b.shape)
        in_specs += [pl.BlockSpec((tm, a.shape[1]), lambda i, j: (i, 0)),
                     pl.BlockSpec((b.shape[0], tn), lambda i, j: (0, j))]
        operands += [a, b]
    n = len(pairs)

    def body(*refs):
        acc = _dg(refs[0][...], refs[1][...], 1, 0, False)
        for p in range(1, n):
            acc = acc + _dg(refs[2 * p][...], refs[2 * p + 1][...], 1, 0, False)
        refs[-1][...] = acc

    return pl.pallas_call(
        body, name=name, grid=(M // tm, N // tn),
        in_specs=in_specs + [ANY] * len(afters),
        out_specs=pl.BlockSpec((tm, tn), lambda i, j: (i, j)),
        out_shape=jax.ShapeDtypeStruct((M, N), F32),
        compiler_params=pltpu.CompilerParams(
            dimension_semantics=("parallel", "parallel"), vmem_limit_bytes=VMEM_LIMIT),
    )(*operands, *afters)


def rowcall(name, fn, rows, consts, out_rows, out_accs=(), tr=256, sp=None, R=None, after=None,
            out_lead=None, into=None):
    afters = ([] if after is None else [after]) + ([] if into is None else list(into))
    rows = [r if isinstance(r, tuple) else (r, None) for r in rows]
    R = rows[0][0].shape[-2] if R is None else R
    tr = _tile(R, tr)
    n = R // tr
    in_specs = []
    for arr, lead in rows:
        C = arr.shape[-1]
        if lead is None:
            assert arr.shape[-2] == R, (name, arr.shape, R)
            in_specs.append(pl.BlockSpec((tr, C), lambda i, *s: (i, 0)))
        elif callable(lead):
            in_specs.append(pl.BlockSpec((tr, C), lambda i, *s, lead=lead: (lead(i, n, *s), 0)))
        else:
            assert arr.shape[-2] == R, (name, arr.shape, R)
            in_specs.append(pl.BlockSpec((None, tr, C), lambda i, *s, lead=lead: (lead, i, 0)))
    for c in consts:
        in_specs.append(pl.BlockSpec(c.shape, lambda i, *s, nd=c.ndim: (0,) * nd))
    if out_lead is None:
        out_shape = [jax.ShapeDtypeStruct((R, C), dt) for C, dt in out_rows]
        out_specs = [pl.BlockSpec((tr, C), lambda i, *s: (i, 0)) for C, _ in out_rows]
    else:
        n_slab, slab = out_lead
        out_shape = [jax.ShapeDtypeStruct((n_slab, R, C), dt) for C, dt in out_rows]
        out_specs = [pl.BlockSpec((None, tr, C), lambda i, *s: (slab, i, 0)) for C, _ in out_rows]
    for shp in out_accs:
        out_shape.append(jax.ShapeDtypeStruct(shp, F32))
        out_specs.append(pl.BlockSpec(shp, lambda i, *s, nd=len(shp): (0,) * nd))
    n_in, n_row, n_acc = len(rows) + len(consts), len(out_rows), len(out_accs)
    n_sp = 0 if sp is None else 1

    def body(*refs):
        refs = refs[n_sp:]
        ins = [r[...] for r in refs[:n_in]]
        res = fn(*ins)
        if not isinstance(res, (tuple, list)):
            res = (res,)
        o_refs = refs[n_in + len(afters):]
        for o_ref, r in zip(o_refs[:n_row], res[:n_row]):
            o_ref[...] = r.astype(o_ref.dtype)
        if n_acc:
            first = pl.program_id(0) == 0
            for o_ref, r in zip(o_refs[n_row:], res[n_row:]):
                r = r.astype(F32).reshape(o_ref.shape)

                @pl.when(first)
                def _(o_ref=o_ref, r=r):
                    o_ref[...] = r

                @pl.when(jnp.logical_not(first))
                def _(o_ref=o_ref, r=r):
                    o_ref[...] += r

    params = pltpu.CompilerParams(dimension_semantics=("arbitrary",), vmem_limit_bytes=VMEM_LIMIT)
    operands = [a for a, _ in rows] + list(consts) + afters
    in_specs = in_specs + [ANY] * len(afters)
    aliases = {} if into is None else {n_sp + len(operands) - len(into) + k: k for k in range(len(into))}
    if sp is None:
        outs = pl.pallas_call(body, name=name, grid=(n,), in_specs=in_specs, out_specs=out_specs,
                              out_shape=out_shape, input_output_aliases=aliases,
                              compiler_params=params)(*operands)
    else:
        outs = pl.pallas_call(
            body, name=name, out_shape=out_shape, compiler_params=params, input_output_aliases=aliases,
            grid_spec=pltpu.PrefetchScalarGridSpec(
                num_scalar_prefetch=1, grid=(n,), in_specs=in_specs, out_specs=out_specs),
        )(sp, *operands)
    return outs[0] if len(outs) == 1 else tuple(outs)


def slabcall(name, fn, ins, out_dtypes, sp, n_out_slabs=None, cw=512):
    R, C = ins[0][0].shape[-2:]
    cw = _tile(C, cw, LANE)
    in_specs = []
    for arr, slab in ins:
        assert arr.shape[-2:] == (R, C), (name, arr.shape)
        if slab is None:
            in_specs.append(pl.BlockSpec((R, cw), lambda i, j, s: (0, j)))
        elif callable(slab):
            in_specs.append(pl.BlockSpec((None, R, cw), lambda i, j, s, slab=slab: (slab(i, s), 0, j)))
        else:
            in_specs.append(pl.BlockSpec((None, R, cw), lambda i, j, s, slab=slab: (slab, 0, j)))
    if n_out_slabs is None:
        out_shape = [jax.ShapeDtypeStruct((R, C), dt) for dt in out_dtypes]
        out_specs = [pl.BlockSpec((R, cw), lambda i, j, s: (0, j)) for _ in out_dtypes]
    else:
        out_shape = [jax.ShapeDtypeStruct((n_out_slabs, R, C), dt) for dt in out_dtypes]
        out_specs = [pl.BlockSpec((None, R, cw), lambda i, j, s: (i, 0, j)) for _ in out_dtypes]
    n_in = len(ins)

    def body(sp_ref, *refs):
        res = fn(*[r[...] for r in refs[:n_in]])
        for o_ref, r in zip(refs[n_in:], res):
            o_ref[...] = r.astype(o_ref.dtype)

    outs = pl.pallas_call(
        body, name=name, out_shape=out_shape,
        grid_spec=pltpu.PrefetchScalarGridSpec(
            num_scalar_prefetch=1, grid=(n_out_slabs or 1, C // cw), in_specs=in_specs, out_specs=out_specs),
        compiler_params=pltpu.CompilerParams(
            dimension_semantics=("arbitrary", "arbitrary"), vmem_limit_bytes=VMEM_LIMIT),
    )(sp, *[a for a, _ in ins])
    return outs[0] if len(outs) == 1 else tuple(outs)


def rms_tile(x, g):
    x = x.astype(F32)
    return x * lax.rsqrt(jnp.mean(x * x, axis=-1, keepdims=True) + EPS) * g


def gelu(x):
    return 0.5 * x * (1.0 + lax.erf(x * (1.0 / math.sqrt(2.0))))


def ln_tile(x, g, b):
    mu = jnp.mean(x, axis=-1, keepdims=True)
    xc = x - mu
    return xc * lax.rsqrt(jnp.mean(xc * xc, axis=-1, keepdims=True) + EPS) * g + b


def lane_groups(fn, width, *arrs):
    n = arrs[0].shape[-1] // width
    outs = [fn(*[a[:, i * width:(i + 1) * width] for a in arrs]) for i in range(n)]
    return jnp.concatenate(outs, axis=-1)


def mixa_post_tile(o, z, o_norm):
    dv = o_norm.shape[-1]
    on = lane_groups(lambda t: rms_tile(t, o_norm), dv, o)
    return on * jax.nn.silu(z)


def mixb_tile(uv, ln_g, ln_b, w_s, bs_t):
    G = w_s.shape[0]
    gw = ln_g.shape[-1]
    dg = gw // G
    tr = uv.shape[0]
    u = gelu(uv[:, :gw])
    vg = gelu(uv[:, gw:])
    ii = lax.broadcasted_iota(jnp.int32, (B_BLOCK, B_BLOCK), 0)
    jj = lax.broadcasted_iota(jnp.int32, (B_BLOCK, B_BLOCK), 1)
    mask = (jj // CHUNK) <= (ii // CHUNK)
    cols = []
    for g in range(G):
        sl = slice(g * dg, (g + 1) * dg)
        vn = ln_tile(vg[:, sl], ln_g[:, sl], ln_b[:, sl])
        wm = jnp.where(mask, w_s[g], 0.0)
        blocks = []
        for m in range(tr // B_BLOCK):
            blk = vn[m * B_BLOCK:(m + 1) * B_BLOCK, :]
            blocks.append(mm(wm, blk, 1, 0) + bs_t[:, g:g + 1])
        mixed = blocks[0] if len(blocks) == 1 else jnp.concatenate(blocks, axis=0)
        cols.append(u[:, sl] * mixed)
    return jnp.concatenate(cols, axis=-1)


def bgc_tile(ba, alog_row, dtb_row):
    tr = ba.shape[0]
    beta = jax.nn.sigmoid(ba)
    g = -jnp.exp(alog_row) * jax.nn.softplus(ba + dtb_row)
    ii = lax.broadcasted_iota(jnp.int32, (tr, tr), 0)
    jj = lax.broadcasted_iota(jnp.int32, (tr, tr), 1)
    tri = jnp.where((ii // CHUNK == jj // CHUNK) & (jj <= ii), 1.0, 0.0).astype(F32)
    gc = mm(tri, g, 1, 0, True)
    return beta, gc


def make_bgc(H):
    def f(ba, alog_row, dtb_row):
        beta, gc = bgc_tile(ba, alog_row, dtb_row)
        lane = lax.broadcasted_iota(jnp.int32, ba.shape, 1)
        return jnp.where(lane < H, beta, jnp.where(lane < 2 * H, gc, 0.0))
    return f


def loss_tile(x, g, target):
    y = rms_tile(x, g)
    err = y - target
    return 0.5 * jnp.sum(jnp.mean(err * err, axis=-1, keepdims=True), axis=0, keepdims=True)


def adamw_tile(w, g, m, v):
    m = ADAM_B1 * m + (1.0 - ADAM_B1) * g
    v = ADAM_B2 * v + (1.0 - ADAM_B2) * (g * g)
    m_hat = m / (1.0 - ADAM_B1 ** ADAM_STEP)
    v_hat = v / (1.0 - ADAM_B2 ** ADAM_STEP)
    delta = -ADAM_LR * (m_hat / (jnp.sqrt(v_hat) + ADAM_EPS) + ADAM_WD * w)
    return delta, m, v


CONV_ROWS = 32


def _shifted_copies(src, sh, rows):
    for b in range(SUBLANE):
        sh[b] = src[pl.ds(b, rows), :]


def _window(sh, off, rows):
    b = off % SUBLANE
    return sh[b, pl.ds(off - b, rows), :]


def _conv_rows(out, sh, w_ref, offsets, rows):
    for r0 in range(0, rows, CONV_ROWS):
        rc = min(CONV_ROWS, rows - r0)
        acc = w_ref[0:1, :] * _window(sh, offsets[0] + r0, rc)
        for k in range(1, len(offsets)):
            acc = acc + w_ref[k:k + 1, :] * _window(sh, offsets[k] + r0, rc)
        out[r0:r0 + rc, :] = acc


def _conv_wgrad(dsrc, d0, sh, offsets, rows):
    dws = []
    for off in offsets:
        acc = None
        for r0 in range(0, rows, CONV_ROWS):
            rc = min(CONV_ROWS, rows - r0)
            prod = dsrc[d0 + r0:d0 + r0 + rc, :] * _window(sh, off + r0, rc)
            for g in range(0, rc, SUBLANE):
                part = prod[g:g + SUBLANE, :]
                acc = part if acc is None else acc + part
        dws.append(jnp.sum(acc, axis=0, keepdims=True))
    return jnp.concatenate(dws, axis=0)


def _conv_specs(T, tr, hb, cw, col_blocks, rev):
    n = T // tr

    def ri(i):
        return (n - 1 - i) if rev else i

    tile_specs, halo_specs = [], []
    for off in col_blocks:
        tile_specs.append(pl.BlockSpec((tr, cw), lambda j, i, off=off: (ri(i), j + off)))
        halo_specs.append(pl.BlockSpec(
            (hb, cw), lambda j, i, off=off: (jnp.maximum(ri(i) * (tr // hb) - 1, 0), j + off)))
    return n, ri, tile_specs, halo_specs


def conv_fwd(name, x, w, consts, pre, post, col_blocks, n_out, K, out_dtype=F32, tr=256, hb=8, cw=512):
    T = x.shape[0]
    C = w.shape[1]
    tr, cw = _tile(T, tr, hb), min(cw, C)
    nb = len(col_blocks)
    n, ri, tile_specs, halo_specs = _conv_specs(T, tr, hb, cw, col_blocks, False)
    w_spec = pl.BlockSpec((K, cw), lambda j, i: (0, j))
    c_specs = [pl.BlockSpec((1, cw), lambda j, i: (0, j)) for _ in consts]

    def body(*refs):
        tiles = [r[...] for r in refs[:nb]]
        halos = [r[...] for r in refs[nb:2 * nb]]
        w_ref = refs[2 * nb]
        cs = [r[...] for r in refs[2 * nb + 1:2 * nb + 1 + len(consts)]]
        o_refs = refs[2 * nb + 1 + len(consts):-3]
        pbuf, shp, cbuf = refs[-3:]
        i = pl.program_id(1)
        pbuf[0:hb, :] = jnp.where(i > 0, pre(*halos), 0.0)
        pbuf[hb:hb + tr, :] = pre(*tiles)
        pbuf[hb + tr:hb + tr + SUBLANE, :] = jnp.zeros((SUBLANE, cw), F32)
        _shifted_copies(pbuf, shp, hb + tr)
        _conv_rows(cbuf, shp, w_ref, [hb - (K - 1) + k for k in range(K)], tr)
        res = post(cbuf[...], pl.program_id(0), *cs)
        for o_ref, r in zip(o_refs, res):
            o_ref[...] = r.astype(o_ref.dtype)

    outs = pl.pallas_call(
        body, name=name, grid=(C // cw, n),
        in_specs=tile_specs + halo_specs + [w_spec] + c_specs,
        out_specs=[pl.BlockSpec((tr, cw), lambda j, i: (i, j)) for _ in range(n_out)],
        out_shape=[jax.ShapeDtypeStruct((T, C), out_dtype) for _ in range(n_out)],
        scratch_shapes=[pltpu.VMEM((hb + tr + SUBLANE, cw), F32), pltpu.VMEM((SUBLANE, hb + tr, cw), F32),
                        pltpu.VMEM((tr, cw), F32)],
        compiler_params=pltpu.CompilerParams(
            dimension_semantics=("parallel", "arbitrary"), vmem_limit_bytes=VMEM_LIMIT),
    )(*([x] * nb), *([x] * nb), w, *consts)
    return outs[0] if n_out == 1 else tuple(outs)


def conv_bwd(name, x, w, consts, grads, pre, post, col_blocks, K, tr=256, hb=8, cw=512, recompute=True):
    T, Cx = x.shape
    C = w.shape[1]
    tr, cw = _tile(T, tr, hb), min(cw, C)
    nb = len(col_blocks)
    n, ri, tile_specs, halo_specs = _conv_specs(T, tr, hb, cw, col_blocks, True)
    w_spec = pl.BlockSpec((K, cw), lambda j, i: (0, j))
    c_specs = [pl.BlockSpec((1, cw), lambda j, i: (0, j)) for _ in consts]
    g_specs = [pl.BlockSpec((tr, cw), lambda j, i: (ri(i), j)) for _ in grads]
    nc, ng = len(consts), len(grads)

    def body(*refs):
        p = 0
        tile_refs = refs[p:p + nb]; p += nb
        halo_refs = refs[p:p + nb]; p += nb
        w_ref = refs[p]; p += 1
        cs = [r[...] for r in refs[p:p + nc]]; p += nc
        gs = [r[...] for r in refs[p:p + ng]]; p += ng
        dx_refs = refs[p:p + nb]; p += nb
        sum_refs = refs[p:p + nb]; p += nb
        dw_ref = refs[p]; p += 1
        dc_refs = refs[p:p + nc]; p += nc
        pbuf, dbuf, ebuf, carry, shp, shd, cbuf = refs[p:p + 7]
        i = pl.program_id(1)
        first = i == 0
        tiles = [r[...] for r in tile_refs]
        halos = [r[...] for r in halo_refs]
        p_tile, vjp_pre = jax.vjp(pre, *tiles)
        pbuf[0:hb, :] = jnp.where(ri(i) > 0, pre(*halos), 0.0)
        pbuf[hb:hb + tr, :] = p_tile
        pbuf[hb + tr:hb + tr + SUBLANE, :] = jnp.zeros((SUBLANE, cw), F32)
        _shifted_copies(pbuf, shp, hb + tr)
        taps = [hb - (K - 1) + k for k in range(K)]
        if recompute:
            _conv_rows(cbuf, shp, w_ref, taps, tr)
            c = cbuf[...]
        else:
            c = jnp.zeros((tr, cw), F32)
        cid = pl.program_id(0)
        _, vjp_post = jax.vjp(lambda c_, *cs_: post(c_, cid, *cs_), c, *cs)
        dres = vjp_post(tuple(g.astype(F32) for g in gs))
        dbuf[0:hb, :] = jnp.zeros((hb, cw), F32)
        dbuf[hb:hb + tr, :] = dres[0]
        dbuf[hb + tr:hb + tr + hb + SUBLANE, :] = jnp.zeros((hb + SUBLANE, cw), F32)
        _shifted_copies(dbuf, shd, hb + tr + hb)
        _conv_rows(ebuf, shd, w_ref, [K - 1 - k for k in range(K)], hb + tr)
        dw = _conv_wgrad(dbuf, hb, shp, taps, tr)

        @pl.when(jnp.logical_not(first))
        def _():
            ebuf[tr:tr + hb, :] += carry[...]

        carry[...] = ebuf[0:hb, :]
        dtiles = vjp_pre(ebuf[hb:hb + tr, :])
        for r, s, d in zip(dx_refs, sum_refs, dtiles):
            r[...] = d.astype(r.dtype)
            ds_ = jnp.sum(d, axis=0, keepdims=True)

            @pl.when(first)
            def _(s=s, ds_=ds_):
                s[...] = ds_

            @pl.when(jnp.logical_not(first))
            def _(s=s, ds_=ds_):
                s[...] += ds_

        accs = [(dw_ref, dw)] + [(r, d) for r, d in zip(dc_refs, dres[1:])]
        for r, d in accs:
            @pl.when(first)
            def _(r=r, d=d):
                r[...] = d

            @pl.when(jnp.logical_not(first))
            def _(r=r, d=d):
                r[...] += d

    n_cb = C // cw
    outs = pl.pallas_call(
        body, name=name, grid=(n_cb, n),
        in_specs=tile_specs + halo_specs + [w_spec] + c_specs + g_specs,
        out_specs=([pl.BlockSpec((tr, cw), lambda j, i: (ri(i), j)) for _ in col_blocks]
                   + [pl.BlockSpec((1, cw), lambda j, i: (0, j)) for _ in col_blocks]
                   + [pl.BlockSpec((K, cw), lambda j, i: (0, j))]
                   + [pl.BlockSpec((1, cw), lambda j, i: (0, j)) for _ in consts]),
        out_shape=([jax.ShapeDtypeStruct((T, C), BF) for _ in col_blocks]
                   + [jax.ShapeDtypeStruct((1, C), F32) for _ in col_blocks]
                   + [jax.ShapeDtypeStruct((K, C), F32)]
                   + [jax.ShapeDtypeStruct((1, C), F32) for _ in consts]),
        scratch_shapes=[pltpu.VMEM((hb + tr + SUBLANE, cw), F32), pltpu.VMEM((hb + tr + hb + SUBLANE, cw), F32),
                        pltpu.VMEM((hb + tr, cw), F32), pltpu.VMEM((hb, cw), F32),
                        pltpu.VMEM((SUBLANE, hb + tr, cw), F32), pltpu.VMEM((SUBLANE, hb + tr + hb, cw), F32),
                        pltpu.VMEM((tr, cw), F32)],
        compiler_params=pltpu.CompilerParams(
            dimension_semantics=("parallel", "arbitrary"), vmem_limit_bytes=VMEM_LIMIT),
    )(*([x] * nb), *([x] * nb), w, *consts, *grads)
    dxs = outs[:nb]
    sums = outs[nb:2 * nb]
    dw = outs[2 * nb]
    dcs = outs[2 * nb + 1:]
    return dxs, sums, dw, dcs


def make_qkv_post(dk, cw, n_qk_chunks):
    def l2(t):
        return t * lax.rsqrt(jnp.sum(t * t, axis=-1, keepdims=True) + EPS)

    def post(c, cid):
        s = jax.nn.silu(c)
        normed = lane_groups(l2, dk, s)
        return (jnp.where(cid < n_qk_chunks, normed, s),)
    return post


def glu_pre(za, zb):
    return za * jax.nn.sigmoid(zb)


def bias_post(c, cid, b):
    return (c + b,)


def _col_to_row(col):
    C = col.shape[-2]
    ii = lax.broadcasted_iota(jnp.int32, (C, C), 0)
    jj = lax.broadcasted_iota(jnp.int32, (C, C), 1)
    wide = jnp.broadcast_to(col, col.shape[:-1] + (C,))
    return jnp.sum(jnp.where(ii == jj, wide, 0.0), axis=-2, keepdims=True)


@jax.custom_vjp
def solve_with_inverse(a, rhs, x):
    return mm(x, rhs, 1, 0, THREE_PASS)


def _swi_fwd(a, rhs, x):
    sol = mm(x, rhs, 1, 0, THREE_PASS)
    return sol, (x, sol)


def _swi_bwd(res, dsol):
    x, sol = res
    drhs = mm(x, dsol, 0, 0, THREE_PASS)
    da = -mm(drhs, sol, 1, 1, THREE_PASS)
    return da, drhs, jnp.zeros_like(x)


solve_with_inverse.defvjp(_swi_fwd, _swi_bwd)


def unit_lower_inverse(a):
    C = a.shape[-1]
    ii = lax.broadcasted_iota(jnp.int32, (C, C), 0)
    jj = lax.broadcasted_iota(jnp.int32, (C, C), 1)
    x = jnp.where(ii == jj, 1.0, 0.0).astype(F32) - a
    p = mm(a, a, 1, 0, THREE_PASS)
    steps = int(math.log2(C)) - 1
    for s in range(steps):
        x = x + mm(x, p, 1, 0, THREE_PASS)
        if s < steps - 1:
            p = mm(p, p, 1, 0, THREE_PASS)
    return x


def dn_masks(C):
    ii = lax.broadcasted_iota(jnp.int32, (C, C), 0)
    jj = lax.broadcasted_iota(jnp.int32, (C, C), 1)
    return ii >= jj, ii > jj


def dn_pre(q, k, v, beta, gc):
    C, dk = q.shape[-2:]
    tri, strict = dn_masks(C)
    q = q * (dk ** -0.5)
    diff = gc - _col_to_row(gc)
    decay = jnp.where(tri, jnp.exp(jnp.where(tri, diff, 0.0)), 0.0)
    kb = k * beta
    vb = v * beta
    a = jnp.where(strict, mm(kb, k, 1, 1) * decay, 0.0)
    eg = jnp.exp(gc)
    rhs = jnp.concatenate([vb, kb * eg], axis=-1)
    attn = mm(q, k, 1, 1) * decay
    qd = q * eg
    g_last = gc[..., C - 1:C, :]
    kt = k * jnp.exp(g_last - gc)
    gl = jnp.exp(g_last)
    return a, rhs, attn, qd, kt, gl


def dn_chunk(q, k, v, beta, gc, state, x):
    dv = v.shape[-1]
    a, rhs, attn, qd, kt, gl = dn_pre(q, k, v, beta, gc)
    sol = solve_with_inverse(a, rhs, x)
    u, w = sol[..., :dv], sol[..., dv:]
    vn = u - mm(w, state, 1, 0)
    o = mm(qd, state, 1, 0) + mm(attn, vn, 1, 0)
    new_state = state * gl + mm(kt, vn, 0, 0)
    return o, new_state


def _by_head(q_ref, k_ref, v_ref, bg, H, dk, dv):
    qv = jnp.stack([q_ref[:, h * dk:(h + 1) * dk] for h in range(H)])
    kv = jnp.stack([k_ref[:, h * dk:(h + 1) * dk] for h in range(H)])
    vv = jnp.stack([v_ref[:, h * dv:(h + 1) * dv] for h in range(H)])
    beta = jnp.stack([bg[:, h:h + 1] for h in range(H)])
    gc = jnp.stack([bg[:, H + h:H + h + 1] for h in range(H)])
    return qv, kv, vv, beta, gc


def deltanet_fwd(qkv, bgc, H):
    T = qkv.shape[0]
    dk = dv = qkv.shape[1] // (3 * H)
    N = T // CHUNK

    def body(q_ref, k_ref, v_ref, bgc_ref, o_ref, x_ref, s_ref, state):
        @pl.when(pl.program_id(0) == 0)
        def _():
            state[...] = jnp.zeros((H, dk, dv), F32)

        qv, kv, vv, beta, gc = _by_head(q_ref, k_ref, v_ref, bgc_ref[...], H, dk, dv)
        a = dn_pre(qv, kv, vv, beta, gc)[0]
        x = unit_lower_inverse(a)
        s = state[...]
        o, s_new = dn_chunk(qv, kv, vv, beta, gc, s, x)
        for h in range(H):
            o_ref[:, h * dv:(h + 1) * dv] = o[h]
        x_ref[...] = x
        s_ref[...] = s
        state[...] = s_new

    return pl.pallas_call(
        body, name="deltanet_fwd", grid=(N,),
        in_specs=[pl.BlockSpec((CHUNK, H * dk), lambda n: (n, 0)),
                  pl.BlockSpec((CHUNK, H * dk), lambda n: (n, 1)),
                  pl.BlockSpec((CHUNK, H * dv), lambda n: (n, 2)),
                  pl.BlockSpec((CHUNK, LANE), lambda n: (n, 0))],
        out_specs=[pl.BlockSpec((CHUNK, H * dv), lambda n: (n, 0)),
                   pl.BlockSpec((None, H, CHUNK, CHUNK), lambda n: (n, 0, 0, 0)),
                   pl.BlockSpec((None, H, dk, dv), lambda n: (n, 0, 0, 0))],
        out_shape=[jax.ShapeDtypeStruct((T, H * dv), F32),
                   jax.ShapeDtypeStruct((N, H, CHUNK, CHUNK), F32),
                   jax.ShapeDtypeStruct((N, H, dk, dv), F32)],
        scratch_shapes=[pltpu.VMEM((H, dk, dv), F32)],
        compiler_params=pltpu.CompilerParams(
            dimension_semantics=("arbitrary",), vmem_limit_bytes=VMEM_LIMIT),
    )(qkv, qkv, qkv, bgc)


def deltanet_bwd(qkv, bgc, xinv, states, do, H):
    T = qkv.shape[0]
    dk = dv = qkv.shape[1] // (3 * H)
    N = T // CHUNK

    def body(q_ref, k_ref, v_ref, bgc_ref, x_ref, s_ref, do_ref, dqkv_ref, dbgc_ref, dstate):
        @pl.when(pl.program_id(0) == 0)
        def _():
            dstate[...] = jnp.zeros((H, dk, dv), F32)

        qv, kv, vv, beta, gc = _by_head(q_ref, k_ref, v_ref, bgc_ref[...], H, dk, dv)
        do = jnp.stack([do_ref[:, h * dv:(h + 1) * dv] for h in range(H)])
        _, vjp = jax.vjp(dn_chunk, qv, kv, vv, beta, gc, s_ref[...], x_ref[...])
        dq, dk_, dv_, dbeta, dgc, ds, _ = vjp((do, dstate[...]))
        dstate[...] = ds
        lane = lax.broadcasted_iota(jnp.int32, (CHUNK, LANE), 1)
        dbgc = jnp.zeros((CHUNK, LANE), F32)
        for h in range(H):
            dqkv_ref[:, h * dk:(h + 1) * dk] = dq[h]
            dqkv_ref[:, (H + h) * dk:(H + h + 1) * dk] = dk_[h]
            dqkv_ref[:, (2 * H + h) * dk:(2 * H + h + 1) * dk] = dv_[h]
            dbgc = dbgc + jnp.where(lane == h, dbeta[h], 0.0) + jnp.where(lane == h + H, dgc[h], 0.0)
        dbgc_ref[...] = dbgc

    rn = lambda n: N - 1 - n
    return pl.pallas_call(
        body, name="deltanet_bwd", grid=(N,),
        in_specs=[pl.BlockSpec((CHUNK, H * dk), lambda n: (rn(n), 0)),
                  pl.BlockSpec((CHUNK, H * dk), lambda n: (rn(n), 1)),
                  pl.BlockSpec((CHUNK, H * dv), lambda n: (rn(n), 2)),
                  pl.BlockSpec((CHUNK, LANE), lambda n: (rn(n), 0)),
                  pl.BlockSpec((None, H, CHUNK, CHUNK), lambda n: (rn(n), 0, 0, 0)),
                  pl.BlockSpec((None, H, dk, dv), lambda n: (rn(n), 0, 0, 0)),
                  pl.BlockSpec((CHUNK, H * dv), lambda n: (rn(n), 0))],
        out_specs=[pl.BlockSpec((CHUNK, 3 * H * dk), lambda n: (rn(n), 0)),
                   pl.BlockSpec((CHUNK, LANE), lambda n: (rn(n), 0))],
        out_shape=[jax.ShapeDtypeStruct((T, 3 * H * dk), F32),
                   jax.ShapeDtypeStruct((T, LANE), F32)],
        scratch_shapes=[pltpu.VMEM((H, dk, dv), F32)],
        compiler_params=pltpu.CompilerParams(
            dimension_semantics=("arbitrary",), vmem_limit_bytes=VMEM_LIMIT),
    )(qkv, qkv, qkv, bgc, xinv, states, do)


def _place():
    x, y, c = lax.axis_index("x"), lax.axis_index("y"), lax.axis_index("c")
    chips = [(1 - x, y), (x, 1 - y), (1 - x, 1 - y)]
    return x, y, c, chips


def all_gather(name, shards):
    na = len(shards)

    def body(*refs):
        ins, outs = refs[:na], refs[na:2 * na]
        send_sems, recv_sems, local_sems = refs[2 * na:]
        x, y, c, chips = _place()
        me, sibling = (x, y, c), (x, y, 1 - c)

        def copy(a, k, block, to, src=None):
            dst = outs[a].at[4 * block[0] + 2 * block[1] + block[2]]
            return pltpu.make_async_remote_copy(
                src_ref=dst if src is None else src, dst_ref=dst,
                send_sem=send_sems.at[a, k], recv_sem=recv_sems.at[a, k],
                device_id=to, device_id_type=MESH)

        mine, first, passed = [], [], []
        for a in range(na):
            cp = pltpu.make_async_copy(ins[a], outs[a].at[4 * x + 2 * y + c], local_sems.at[a])
            cp.start()
            mine.append(cp)
        for a in range(na):
            cps = [copy(a, 0, me, sibling, src=ins[a])]
            cps += [copy(a, 1 + j, me, (*chip, c), src=ins[a]) for j, chip in enumerate(chips)]
            for cp in cps:
                cp.start()
            first += cps
        for a in range(na):
            for j, chip in enumerate(chips):
                copy(a, 1 + j, (*chip, c), me).wait_recv()
                cp = copy(a, 4 + j, (*chip, c), sibling)
                cp.start()
                passed.append(cp)
        for a in range(na):
            copy(a, 0, sibling, me).wait_recv()
            for j, chip in enumerate(chips):
                copy(a, 4 + j, (*chip, 1 - c), me).wait_recv()
        for cp in first + passed:
            cp.wait_send()
        for cp in mine:
            cp.wait()

    outs = pl.pallas_call(
        body, name=name,
        in_specs=[ANY] * na, out_specs=[ANY] * na,
        out_shape=[jax.ShapeDtypeStruct((N_DEV,) + s.shape, s.dtype) for s in shards],
        scratch_shapes=[pltpu.SemaphoreType.DMA((na, 7)), pltpu.SemaphoreType.DMA((na, 7)),
                        pltpu.SemaphoreType.DMA((na,))],
    )(*shards)
    return list(outs)


HBM_SPEC = pl.BlockSpec(memory_space=pltpu.HBM)
SEM_SPEC = pl.BlockSpec(memory_space=pltpu.SEMAPHORE)
EFFECT = pltpu.SideEffectType.DATAFLOW_SIDE_EFFECTING


def _descriptors(plan, bufs, send_sems, recv_sems):
    return [pltpu.make_async_remote_copy(src_ref=src, dst_ref=dst, send_sem=send_sems.at[k],
                                         recv_sem=recv_sems.at[k], device_id=dev, device_id_type=MESH)
            for k, (src, dst, dev) in enumerate(plan(bufs))]


def split_start(name, bufs, plan, n, after):
    nb = len(bufs)

    def body(*refs):
        for cp in _descriptors(plan, refs[:nb], refs[nb + 1], refs[nb + 2]):
            cp.start()
        refs[-1][...] = jnp.zeros((SUBLANE, LANE), F32)

    outs = pl.pallas_call(
        body, name=name,
        out_shape=(pltpu.SemaphoreType.DMA((n,)), pltpu.SemaphoreType.DMA((n,)),
                   *[pltpu.HBM(b.shape, b.dtype) for b in bufs],
                   jax.ShapeDtypeStruct((SUBLANE, LANE), F32)),
        in_specs=[HBM_SPEC] * nb + [ANY],
        out_specs=(SEM_SPEC, SEM_SPEC, *[HBM_SPEC] * nb, pl.BlockSpec(memory_space=pltpu.VMEM)),
        input_output_aliases={i: 2 + i for i in range(nb)},
        compiler_params=pltpu.CompilerParams(has_side_effects=EFFECT),
    )(*[pltpu.with_memory_space_constraint(b, pltpu.HBM) for b in bufs], after)
    return outs[0], outs[1], list(outs[2:2 + nb]), outs[-1]


def split_wait(name, send_sems, recv_sems, bufs, plan, after):
    nb = len(bufs)

    def body(*refs):
        cps = _descriptors(plan, refs[:nb], refs[nb], refs[nb + 1])
        for cp in cps:
            cp.wait_recv()
        for cp in cps:
            cp.wait_send()
        refs[-1][...] = jnp.zeros((SUBLANE, LANE), F32)

    afters = list(after) if isinstance(after, (list, tuple)) else [after]
    outs = pl.pallas_call(
        body, name=name,
        out_shape=[pltpu.HBM(b.shape, b.dtype) for b in bufs] + [jax.ShapeDtypeStruct((SUBLANE, LANE), F32)],
        in_specs=[HBM_SPEC] * nb + [SEM_SPEC, SEM_SPEC] + [ANY] * len(afters),
        out_specs=[HBM_SPEC] * nb + [pl.BlockSpec(memory_space=pltpu.VMEM)],
        input_output_aliases={i: i for i in range(nb)},
        compiler_params=pltpu.CompilerParams(has_side_effects=EFFECT),
    )(*bufs, send_sems, recv_sems, *afters)
    return list(outs[:nb]), outs[-1]


def _block(px, py, pc):
    return 4 * px + 2 * py + pc


def plan_gather_ici(na):
    def plan(bufs):
        x, y, c, chips = _place()
        out = []
        for a in range(na):
            dst = bufs[na + a].at[_block(x, y, c)]
            out.append((bufs[a], dst, (x, y, 1 - c)))
            out += [(bufs[a], dst, (px, py, c)) for px, py in chips]
        return out
    return plan


def plan_gather_pass(na):
    def plan(bufs):
        x, y, c, chips = _place()
        out = []
        for a in range(na):
            for px, py in chips:
                blk = bufs[a].at[_block(px, py, c)]
                out.append((blk, blk, (x, y, 1 - c)))
        return out
    return plan


def plan_gather_direct(bufs):
    x, y, c, _ = _place()
    dst = bufs[1].at[_block(x, y, c)]
    flip = lambda v, f: 1 - v if f else v
    return [(bufs[0], dst, (flip(x, m >> 2 & 1), flip(y, m >> 1 & 1), flip(c, m & 1))) for m in range(1, N_DEV)]


def plan_reduce_d2d(na):
    def plan(bufs):
        x, y, c, _ = _place()
        return [(bufs[a].at[2 * s + (1 - c)], bufs[na + a].at[s], (x, y, 1 - c))
                for a in range(na) for s in range(4)]
    return plan


def plan_reduce_ici(na):
    def plan(bufs):
        x, y, c, chips = _place()
        return [(bufs[a].at[2 * px + py], bufs[na + a].at[j], (px, py, c))
                for a in range(na) for j, (px, py) in enumerate(chips)]
    return plan


def place_own(name, land, shard, dev):
    r, c = shard.shape
    tr = _tile(r, 512)

    def body(sp_ref, s_ref, land_ref, o_ref):
        o_ref[...] = s_ref[...]

    return pl.pallas_call(
        body, name=name, out_shape=jax.ShapeDtypeStruct(land.shape, land.dtype),
        grid_spec=pltpu.PrefetchScalarGridSpec(
            num_scalar_prefetch=1, grid=(r // tr,),
            in_specs=[pl.BlockSpec((tr, c), lambda i, s: (i, 0)), ANY],
            out_specs=pl.BlockSpec((None, tr, c), lambda i, s: (s[0], i, 0))),
        input_output_aliases={2: 0},
        compiler_params=pltpu.CompilerParams(dimension_semantics=("arbitrary",)),
    )(dev, shard, land)


def pack(arrs, row_mult=SUBLANE):
    pieces = []
    for a in arrs:
        f = a.reshape(-1).astype(F32)
        pad = (-f.shape[0]) % LANE
        if pad:
            f = jnp.concatenate([f, jnp.zeros((pad,), F32)])
        pieces.append(f)
    flat = jnp.concatenate(pieces)
    rows = flat.shape[0] // LANE
    pad_rows = (-rows) % row_mult
    if pad_rows:
        flat = jnp.concatenate([flat, jnp.zeros((pad_rows * LANE,), F32)])
    return flat.reshape(-1, LANE)


def unpack(buf, shapes):
    flat = buf.reshape(-1)
    outs, off = [], 0
    for shp in shapes:
        n = int(np.prod(shp))
        outs.append(flat[off:off + n].reshape(shp))
        off += n + ((-n) % LANE)
    return outs


def _vjp_rows(fn, n_row_in, n_cot):
    def bwd(*args):
        rows = args[:n_row_in]
        cots = args[n_row_in:n_row_in + n_cot]
        consts = args[n_row_in + n_cot:]
        out, vjp = jax.vjp(fn, *rows, *consts)
        if isinstance(out, (tuple, list)):
            cot = tuple(c.astype(o.dtype) for c, o in zip(cots, out))
        else:
            cot = cots[0].astype(out.dtype)
        return vjp(cot)
    return bwd


def rms_bwd(name, x, g, dh, dres, after=None):
    D = x.shape[1]
    vj = _vjp_rows(rms_tile, 1, 1)

    def f(x_, dh_, dres_, g_):
        dx, dg = vj(x_, dh_, g_)
        dx = dx + dres_
        return dx, dx, dg
    return rowcall(name, f, [x, dh, dres], [g], [(D, F32), (D, BF)], [(1, D)], after=after)


def kernel(x, e_norm, e_w_in, e_conv_w, e_a_log, e_dt_bias, e_o_norm, e_ln_g, e_ln_b, e_w_s, e_b_s, e_w_out, o_norm, o_pw1, o_pw1_b, o_dw, o_dw_b, o_ln_g, o_ln_b, o_pw2, o_pw2_b, f_norm, f_w1, f_w2, final_norm, loss_target, m_e_norm, m_e_w_in, m_e_conv_w, m_e_a_log, m_e_dt_bias, m_e_o_norm, m_e_ln_g, m_e_ln_b, m_e_w_s, m_e_b_s, m_e_w_out, m_o_norm, m_o_pw1, m_o_pw1_b, m_o_dw, m_o_dw_b, m_o_ln_g, m_o_ln_b, m_o_pw2, m_o_pw2_b, m_f_norm, m_f_w1, m_f_w2, m_final_norm, v_e_norm, v_e_w_in, v_e_conv_w, v_e_a_log, v_e_dt_bias, v_e_o_norm, v_e_ln_g, v_e_ln_b, v_e_w_s, v_e_b_s, v_e_w_out, v_o_norm, v_o_pw1, v_o_pw1_b, v_o_dw, v_o_dw_b, v_o_ln_g, v_o_ln_b, v_o_pw2, v_o_pw2_b, v_f_norm, v_f_w1, v_f_w2, v_final_norm):
    names = ['e_norm', 'e_w_in', 'e_conv_w', 'e_a_log', 'e_dt_bias', 'e_o_norm', 'e_ln_g', 'e_ln_b', 'e_w_s', 'e_b_s', 'e_w_out', 'o_norm', 'o_pw1', 'o_pw1_b', 'o_dw', 'o_dw_b', 'o_ln_g', 'o_ln_b', 'o_pw2', 'o_pw2_b', 'f_norm', 'f_w1', 'f_w2', 'final_norm']
    W = dict(zip(names, [e_norm, e_w_in, e_conv_w, e_a_log, e_dt_bias, e_o_norm, e_ln_g, e_ln_b, e_w_s, e_b_s, e_w_out, o_norm, o_pw1, o_pw1_b, o_dw, o_dw_b, o_ln_g, o_ln_b, o_pw2, o_pw2_b, f_norm, f_w1, f_w2, final_norm]))
    Mo = dict(zip(names, [m_e_norm, m_e_w_in, m_e_conv_w, m_e_a_log, m_e_dt_bias, m_e_o_norm, m_e_ln_g, m_e_ln_b, m_e_w_s, m_e_b_s, m_e_w_out, m_o_norm, m_o_pw1, m_o_pw1_b, m_o_dw, m_o_dw_b, m_o_ln_g, m_o_ln_b, m_o_pw2, m_o_pw2_b, m_f_norm, m_f_w1, m_f_w2, m_final_norm]))
    Vo = dict(zip(names, [v_e_norm, v_e_w_in, v_e_conv_w, v_e_a_log, v_e_dt_bias, v_e_o_norm, v_e_ln_g, v_e_ln_b, v_e_w_s, v_e_b_s, v_e_w_out, v_o_norm, v_o_pw1, v_o_pw1_b, v_o_dw, v_o_dw_b, v_o_ln_g, v_o_ln_b, v_o_pw2, v_o_pw2_b, v_f_norm, v_f_w1, v_f_w2, v_final_norm]))

    T, D = x.shape[1], x.shape[2]
    H = e_a_log.shape[-1]
    dv = e_o_norm.shape[-1]
    dk = dv
    G = e_w_s.shape[1]
    AQK, AV, BW = H * dk, H * dv, e_ln_g.shape[-1]
    AQKV = 2 * AQK + AV
    in_cols = AQKV + AV + 2 * H + 2 * BW
    KA = e_conv_w.shape[1]
    KC = o_dw.shape[1]
    L = f_norm.shape[0]
    dev = 4 * lax.axis_index("x") + 2 * lax.axis_index("y") + lax.axis_index("c")
    x2d = x.reshape(T, D)
    tgt = loss_target.reshape(T, D)

    def tie(small, tok):
        return small + tok[0:1, 0:1]

    dev_sp = dev.astype(jnp.int32).reshape(1)
    where = jnp.stack([lax.axis_index("c"), 2 * lax.axis_index("x") + lax.axis_index("y")]).astype(jnp.int32)
    row = lambda a: a.reshape(1, -1).astype(F32)
    en_row = row(e_norm)

    def gather_begin(tag, shards, after):
        na = len(shards)
        lands = [lax.empty((N_DEV,) + s.shape, s.dtype) for s in shards]
        ss, rs, bufs, tok = split_start(f"gather{tag}_ici_start", shards + lands, plan_gather_ici(na), 4 * na, after)
        return (na, ss, rs, bufs), tok

    def gather_pass(tag, h, after):
        na, ss, rs, bufs = h
        bufs, tok = split_wait(f"gather{tag}_ici_wait", ss, rs, bufs, plan_gather_ici(na), after)
        ss, rs, lands, tok = split_start(f"gather{tag}_pass_start", bufs[na:], plan_gather_pass(na), 3 * na, tok)
        return (na, ss, rs, bufs[:na], lands), tok

    def gather_end(tag, h, after):
        na, ss, rs, shards, lands = h
        lands, _ = split_wait(f"gather{tag}_pass_wait", ss, rs, lands, plan_gather_pass(na), after)
        return [place_own(f"gather{tag}_own{a}", lands[a], shards[a], dev_sp) for a in range(na)]

    small_sharded = ['e_conv_w', 'o_norm', 'o_pw1_b', 'o_dw', 'o_dw_b', 'o_ln_g', 'o_ln_b', 'o_pw2_b']
    sm = all_gather("gather_small", [pack([W[n][0]]) for n in small_sharded])

    bfw = lambda w: w.astype(BF)
    hA0, tok = gather_begin("0", [bfw(jnp.swapaxes(e_w_in[0], 0, 1))], sm[0])
    hA1, tok = gather_begin("1", [bfw(e_w_out[0]), bfw(f_w1[0]), bfw(f_w2[0])], tok)
    hA2, tok = gather_begin("2", [bfw(o_pw1[0]), bfw(o_pw2[0])], tok)
    hA3, tok = gather_begin("3", [bfw(f_w1[1]), bfw(f_w2[1])], tok)
    h0 = rowcall("rms_e", rms_tile, [x2d], [en_row], [(D, BF)], after=tok)

    full = {}
    for n, g in zip(small_sharded, sm):
        shp = W[n][0].shape
        blocks = [unpack(g[d], [shp])[0] for d in range(N_DEV)]
        full[n] = jnp.concatenate(blocks, axis=-1)
    conv_w = full['e_conv_w']
    on_row, pw1_b_row = row(full['o_norm']), row(full['o_pw1_b'])
    dw_w, dw_b_row = full['o_dw'], row(full['o_dw_b'])
    oln_g_row, oln_b_row, pw2_b_row = row(full['o_ln_g']), row(full['o_ln_b']), row(full['o_pw2_b'])
    small_names = ['e_norm', 'e_conv_w', 'e_a_log', 'e_dt_bias', 'e_o_norm', 'e_ln_g', 'e_ln_b', 'e_w_s', 'e_b_s',
                   'o_norm', 'o_pw1_b', 'o_dw', 'o_dw_b', 'o_ln_g', 'o_ln_b', 'o_pw2_b', 'f_norm', 'final_norm']
    packed_wmv = [pack([A[n] for n in small_names], 256) for A in (W, Mo, Vo)]

    hB0, tok = gather_pass("0", hA0, [h0, conv_w, dw_w, pw1_b_row] + packed_wmv)
    (g_win,) = gather_end("0", hB0, tok)
    win_t = g_win.reshape(in_cols, D)
    wt_qkv, wt_z = win_t[:AQKV], win_t[AQKV:AQKV + AV]
    wt_ba = jnp.pad(win_t[AQKV + AV:AQKV + AV + 2 * H], ((0, LANE - 2 * H), (0, 0)))
    wt_uv = win_t[AQKV + AV + 2 * H:]

    alog_row = jnp.pad(row(e_a_log), ((0, 0), (H, LANE - 2 * H)))
    dtb_row = jnp.pad(row(e_dt_bias), ((0, 0), (H, LANE - 2 * H)))
    eon_row = row(e_o_norm)
    eln_g_row, eln_b_row = row(e_ln_g), row(e_ln_b)
    w_s = e_w_s[0]
    bs_t = e_b_s[0].T
    fn_rows = [row(f_norm[l]) for l in range(L)]
    fin_row = row(final_norm)

    qkv_raw = matmul("proj_qkv", h0, wt_qkv, "nt")
    z_gate = matmul("proj_z", h0, wt_z, "nt")
    ba = matmul("proj_ba", h0, wt_ba, "nt")
    uv = matmul("proj_uv", h0, wt_uv, "nt")

    cwa = min(512, AQKV)
    qkv_post = make_qkv_post(dk, cwa, 2 * AQK // cwa)
    ident = lambda t: t
    qkv = conv_fwd("qkv_conv", qkv_raw, conv_w, [], ident, qkv_post, [0], 1, KA, cw=cwa, tr=512)
    bgc_fn = make_bgc(H)
    bgc = rowcall("bgc", bgc_fn, [ba], [alog_row, dtb_row], [(LANE, F32)])
    o_dn, xinv, states = deltanet_fwd(qkv, bgc, H)
    hB1, tok = gather_pass("1", hA1, o_dn)

    def mix_tile(o, z, uv_, o_norm_, ln_g, ln_b, w_s_, bs_t_):
        return jnp.concatenate([mixa_post_tile(o, z, o_norm_), mixb_tile(uv_, ln_g, ln_b, w_s_, bs_t_)], axis=-1)
    mix_consts = [eon_row, eln_g_row, eln_b_row, w_s, bs_t]
    mix = rowcall("mix", mix_tile, [o_dn, z_gate, uv], mix_consts, [(AV + BW, BF)], after=tok)
    g_wout, g_w1_0, g_w2_0 = gather_end("1", hB1, mix)
    wout = g_wout.reshape(-1, D)
    w1 = [jnp.moveaxis(g_w1_0, 0, 1).reshape(D, -1), None]
    w2 = [g_w2_0.reshape(-1, D), None]
    add_epi = lambda acc, r: (acc + r,)
    x1 = matmul("out_proj", mix, wout, "nn", epi=add_epi, extras=[x2d], tm=512, tn=2048)

    def relu2_epi(acc):
        r = jnp.maximum(acc, 0.0)
        return r * r, r

    hB2, tok = gather_pass("2", hA2, x1)
    a2_0, ar_0, hf0 = norm_matmul("ffn_up0", x1, fn_rows[0], g_w1_0, epi=relu2_epi, out_dtypes=(BF, BF), after=tok)
    g_pw1, g_pw2 = gather_end("2", hB2, a2_0)
    x2 = matmul("ffn_down0", a2_0, w2[0], "nn", epi=add_epi, extras=[x1])
    ffn0 = (hf0, a2_0, ar_0)
    pw1 = jnp.moveaxis(g_pw1, 0, 1).reshape(D, 2 * D)
    pw2 = g_pw2.reshape(D, D)

    bias_epi = lambda acc, b: (acc + b,)
    zc, h1 = norm_matmul("pw1", x2, on_row, g_pw1, epi=bias_epi, extras=[pw1_b_row])
    hB3, tok = gather_pass("3", hA3, zc)
    cwc = min(512, D)
    ncb = D // cwc
    cconv = conv_fwd("dw_conv", zc, dw_w, [tie(dw_b_row, tok)], glu_pre, bias_post, [0, ncb], 1, KC, hb=32, cw=cwc,
                     tr=512)
    g_w1_1, g_w2_1 = gather_end("3", hB3, cconv)
    w1[1] = jnp.moveaxis(g_w1_1, 0, 1).reshape(D, -1)
    w2[1] = g_w2_1.reshape(-1, D)
    ln_silu = lambda c, g, b: jax.nn.silu(ln_tile(c, g, b))
    s_act = rowcall("ln_silu", ln_silu, [cconv], [oln_g_row, oln_b_row], [(D, BF)])
    x3 = matmul("pw2", s_act, pw2, "nn", epi=lambda acc, r, b: (r + (acc + b),), extras=[x2, pw2_b_row],
                tm=512, tn=2048)
    a2_1, ar_1, hf1 = norm_matmul("ffn_up1", x3, fn_rows[1], g_w1_1, epi=relu2_epi, out_dtypes=(BF, BF))
    x4 = matmul("ffn_down1", a2_1, w2[1], "nn", epi=add_epi, extras=[x3])
    ffn1 = (hf1, a2_1, ar_1)

    def loss_bwd_tile(x_, t_, g_):
        l, vjp = jax.vjp(lambda a, b: loss_tile(a, b, t_), x_, g_)
        dx, dg = vjp(jnp.ones_like(l))
        return dx, dx, l, dg
    dx4, dx4_b, loss_part, d_final = rowcall("loss_head", loss_bwd_tile, [x4, tgt], [fin_row],
                                             [(D, F32), (D, BF)], [(1, 1), (1, D)])
    loss = lax.psum(loss_part[0, 0], ("x", "y", "c"))

    def reduce_begin(tag, grads, after):
        na = len(grads)
        lands = [lax.empty((4,) + g.shape[1:], g.dtype) for g in grads]
        ss, rs, bufs, tok = split_start(f"reduce{tag}_d2d_start", grads + lands, plan_reduce_d2d(na), 4 * na, after)
        return (na, ss, rs, bufs), tok

    def reduce_mid(tag, h, after):
        na, ss, rs, bufs = h
        bufs, tok = split_wait(f"reduce{tag}_d2d_wait", ss, rs, bufs, plan_reduce_d2d(na), after)
        parts = []
        for a, (g, rc) in enumerate(zip(bufs[:na], bufs[na:])):
            r, c = g.shape[1], g.shape[2]
            if r % SUBLANE:
                parts.append(slabcall(f"chip_sum{tag}_{a}", lambda p, q: (p + q,),
                                      [(g, lambda i, s: 2 * i + s[0]), (rc, lambda i, s: i)], [BF], where, 4))
                continue
            mine = lambda i, n, s: (2 * (i // (n // 4)) + s[0]) * (n // 4) + i % (n // 4)
            parts.append(rowcall(f"chip_sum{tag}_{a}", lambda p, q: (p + q,),
                                 [(g.reshape(N_DEV * r, c), mine), rc.reshape(4 * r, c)], [],
                                 [(c, BF)], tr=_tile(r, 512), sp=where, R=4 * r).reshape(4, r, c))
        lands = [lax.empty((3,) + p.shape[1:], p.dtype) for p in parts]
        ss, rs, bufs, tok = split_start(f"reduce{tag}_ici_start", parts + lands, plan_reduce_ici(na), 3 * na, tok)
        return (na, ss, rs, bufs), tok

    res = {}

    def reduce_end(tag, h, after, targets):
        na, ss, rs, bufs = h
        bufs, _ = split_wait(f"reduce{tag}_ici_wait", ss, rs, bufs, plan_reduce_ici(na), after)
        for a, (part, fin, (n, l)) in enumerate(zip(bufs[:na], bufs[na:], targets)):
            def f(p0, p1, p2, p3, w_, m_, v_):
                g = ((p0.astype(F32) + p1.astype(F32)) + p2.astype(F32)) + p3.astype(F32)
                return (g,) + adamw_tile(w_, g, m_, v_)
            r, C = fin.shape[-2], fin.shape[-1]
            if W[n].shape[1:] == (C, r):
                t = lambda arr: jnp.swapaxes(arr[l], 0, 1)
                outs = slabcall(f"adam{tag}_{a}", f, [(part, lambda i, s: s[1]), (fin, 0), (fin, 1), (fin, 2),
                                                      (t(W[n]), None), (t(Mo[n]), None), (t(Vo[n]), None)],
                                [F32] * 4, where)
                res[n] = tuple(jnp.swapaxes(o, 0, 1)[None] for o in outs)
                continue
            own = lambda i, n_, s: s[1] * n_ + i
            res[n] = rowcall(f"adam{tag}_{a}", f, [(part.reshape(4 * r, C), own), (fin, 0), (fin, 1), (fin, 2),
                                                   (W[n], l), (Mo[n], l), (Vo[n], l)], [],
                             [(C, F32)] * 4, tr=256, sp=where, R=r,
                             out_lead=(W[n].shape[0], l), into=res.get(n))

    dscale_epi = lambda acc, r: (acc * (2.0 * r.astype(F32)),)
    d_fnorm = [None] * L

    dpre1 = matmul("ffn_down_dx1", dx4_b, w2[1], "nt", epi=dscale_epi, extras=[ar_1], out_dtypes=(BF,))
    dw2_1 = matmul("ffn_down_dw1", a2_1, dx4_b, "tn")
    dw1_1 = matmul("ffn_up_dw1", hf1, dpre1, "tn", colshard=True)
    hD1, tok = reduce_begin("1", [dw1_1, dw2_1.reshape(N_DEV, -1, D)], dpre1)
    dhf1 = matmul("ffn_up_dx1", dpre1, w1[1], "nt", after=tok, tk=4096)
    dx3, dx3_b, d_fnorm[1] = rms_bwd("rms_f_bwd1", x3, fn_rows[1], dhf1, dx4)

    ds_act = matmul("pw2_dx", dx3_b, pw2, "nt")
    hI1, tok = reduce_mid("1", hD1, ds_act)
    d_pw2 = matmul("pw2_dw", s_act, dx3_b, "tn", after=tok)
    ln_silu_bwd = _vjp_rows(ln_silu, 1, 1)

    def ln_silu_bwd_tile(c_, ds_, dx3_, g_, b_):
        dc, dg, db = ln_silu_bwd(c_, ds_, g_, b_)
        return dc, dg, db, jnp.sum(dx3_, axis=0, keepdims=True)
    dcconv, d_oln_g, d_oln_b, d_pw2_b = rowcall(
        "ln_silu_bwd", ln_silu_bwd_tile, [cconv, ds_act, dx3], [oln_g_row, oln_b_row],
        [(D, F32)], [(1, D), (1, D), (1, D)])
    (dza, dzb), (sza, szb), d_dw, (d_dw_b,) = conv_bwd(
        "dw_conv_bwd", zc, dw_w, [tie(dw_b_row, tok)], [dcconv], glu_pre, bias_post, [0, ncb], KC, hb=32, cw=cwc, tr=256,
        recompute=False)
    dzc = jnp.concatenate([dza, dzb], axis=-1)
    d_pw1_b = jnp.concatenate([sza, szb], axis=-1)
    d_pw1 = matmul("pw1_dw", h1, dzc, "tn", colshard=True)
    dh1 = matmul("pw1_dx", dzc, pw1, "nt")
    dx2, dx2_b, d_onorm = rms_bwd("rms_o_bwd", x2, on_row, dh1, dx3)
    reduce_end("1", hI1, dx2, [('f_w1', 1), ('f_w2', 1)])

    hD2, tok = reduce_begin("2", [d_pw1, d_pw2.reshape(N_DEV, -1, D)], dx2)
    dpre0 = matmul("ffn_down_dx0", dx2_b, w2[0], "nt", epi=dscale_epi, extras=[ar_0], out_dtypes=(BF,), after=tok)
    dw2_0 = matmul("ffn_down_dw0", a2_0, dx2_b, "tn")
    hI2, tok = reduce_mid("2", hD2, dw2_0)
    dw1_0 = matmul("ffn_up_dw0", hf0, dpre0, "tn", colshard=True, after=tok)
    dhf0 = matmul("ffn_up_dx0", dpre0, w1[0], "nt", tk=4096)
    dx1, dx1_b, d_fnorm[0] = rms_bwd("rms_f_bwd0", x1, fn_rows[0], dhf0, dx2)
    reduce_end("2", hI2, dx1, [('o_pw1', 0), ('o_pw2', 0)])

    dmix = matmul("out_proj_dx", dx1_b, wout, "nt")
    d_wout = matmul("out_proj_dw", mix, dx1_b, "tn")
    hD3, tok = reduce_begin("3", [dw1_0, dw2_0.reshape(N_DEV, -1, D), d_wout.reshape(N_DEV, -1, D)], dmix)
    do_dn, dz_gate, duv, d_eon, d_eln_g, d_eln_b, d_ws, d_bs_t = rowcall(
        "mix_bwd", _vjp_rows(mix_tile, 3, 1), [o_dn, z_gate, uv, dmix], mix_consts,
        [(AV, F32), (AV, BF), (2 * BW, BF)], [(1, dv), (1, BW), (1, BW), w_s.shape, bs_t.shape], after=tok)
    dqkv, dbgc = deltanet_bwd(qkv, bgc, xinv, states, do_dn, H)
    hI3, tok = reduce_mid("3", hD3, dbgc)
    bgc_bwd = _vjp_rows(bgc_fn, 1, 1)
    dba, d_alog_row, d_dtb_row = rowcall(
        "bgc_bwd", bgc_bwd, [ba, dbgc], [alog_row, dtb_row], [(LANE, BF)], [(1, LANE), (1, LANE)])
    (dqkv_raw,), _, d_conv_w, _ = conv_bwd(
        "qkv_conv_bwd", qkv_raw, tie(conv_w, tok), [], [dqkv], ident, qkv_post, [0], KA, cw=cwa, tr=512)

    dw_qkv = matmul("proj_qkv_dw", dqkv_raw, h0, "tn")
    dw_z = matmul("proj_z_dw", dz_gate, h0, "tn")
    dw_ba = matmul("proj_ba_dw", dba, h0, "tn")
    dw_uv = matmul("proj_uv_dw", duv, h0, "tn")
    d_win_t = jnp.concatenate([dw_qkv, dw_z, dw_ba[:2 * H], dw_uv], axis=0)
    G_win = d_win_t.reshape(N_DEV, in_cols // N_DEV, D)
    hD4, tok = reduce_begin("4", [G_win], dw_uv)
    reduce_end("3", hI3, tok, [('f_w1', 0), ('f_w2', 0), ('e_w_out', 0)])

    d_alog = d_alog_row[:, H:2 * H]
    d_dtb = d_dtb_row[:, H:2 * H]
    early_grads = [d_conv_w, d_alog, d_dtb, d_eon, d_eln_g, d_eln_b, d_ws, d_bs_t.T,
                   d_onorm, d_pw1_b, d_dw, d_dw_b, d_oln_g, d_oln_b, d_pw2_b,
                   jnp.concatenate(d_fnorm, axis=0), d_final]
    early_packed = pack(early_grads, 256)
    ss_s, rs_s, bufs_s, tok_s = split_start(
        "small_grads_start", [early_packed, lax.empty((N_DEV,) + early_packed.shape, F32)],
        plan_gather_direct, N_DEV - 1, d_conv_w)
    hI4, tok = reduce_mid("4", hD4, [tok_s] + [res[n][0] for n in ('f_w1', 'f_w2', 'o_pw1', 'e_w_out')])
    dh0 = matmul_sum("proj_dx", [(dqkv_raw, wt_qkv), (dz_gate, wt_z), (dba, wt_ba), (duv, wt_uv)], after=tok)
    grad_x, _, d_enorm = rms_bwd("rms_e_bwd", x2d, en_row, dh0, dx1, after=tok)
    late_all = all_gather("gather_e_norm_grad", [pack([d_enorm])])[0]
    (early_shard, early_land), _ = split_wait("small_grads_wait", ss_s, rs_s, bufs_s, plan_gather_direct, late_all)
    early_all = place_own("small_grads_own", early_land, early_shard, dev_sp)

    def sum8(*ps):
        s = ps[0]
        for p in ps[1:]:
            s = s + p
        return (s,)
    gs_sum = rowcall("small_sum", sum8, [(early_all, d) for d in range(N_DEV)], [], [(LANE, F32)])
    late_sum = rowcall("e_norm_sum", sum8, [(late_all, d) for d in range(N_DEV)], [], [(LANE, F32)])
    reduce_end("4", hI4, gs_sum, [('e_w_in', 0)])
    g_full = dict(zip(small_names[1:], unpack(gs_sum, [g.shape for g in early_grads])))
    g_full['e_norm'] = unpack(late_sum, [d_enorm.shape])[0]
    g_loc = {}
    for n in small_names:
        g = g_full[n]
        if n in small_sharded:
            per = g.shape[-1] // N_DEV
            g = lax.dynamic_slice_in_dim(g, dev * per, per, axis=-1)
        g_loc[n] = g.reshape(W[n].shape)
    packs = [packed_wmv[0], pack([g_loc[n] for n in small_names], 256), packed_wmv[1], packed_wmv[2]]
    d_s, m_s, v_s = rowcall("adam_small", adamw_tile, packs, [], [(LANE, F32)] * 3)
    shapes = [W[n].shape for n in small_names]
    for n, d_, m_, v_ in zip(small_names, unpack(d_s, shapes), unpack(m_s, shapes), unpack(v_s, shapes)):
        res[n] = (g_loc[n], d_, m_, v_)

    grads = [res[n][0] for n in names]
    deltas = [res[n][1] for n in names]
    new_m = [res[n][2] for n in names]
    new_v = [res[n][3] for n in names]
    return (loss, grad_x.reshape(x.shape), *grads, *deltas, *new_m, *new_v)
```

```python
import functools
import math

import jax
import jax.numpy as jnp
import numpy as np
from jax import lax
from jax.experimental import pallas as pl
from jax.experimental.pallas import tpu as pltpu

F32 = jnp.float32
BF = jnp.bfloat16
EPS = 1e-6
CHUNK = 64
B_BLOCK = 128
LANE = 128
SUBLANE = 8
N_DEV = 8
VMEM_LIMIT = 56 * 1024 * 1024

ADAM_LR = 0.001
ADAM_B1 = 0.9
ADAM_B2 = 0.999
ADAM_EPS = 1e-08
ADAM_WD = 0.01
ADAM_STEP = 10

MESH = pl.DeviceIdType.MESH
ANY = pl.BlockSpec(memory_space=pl.ANY)


def _tile(n, pref, mult=SUBLANE):
    if n <= pref:
        return n
    t = (pref // mult) * mult
    while t >= mult:
        if n % t == 0:
            return t
        t -= mult
    return n


THREE_PASS = 3


def _dg(a, b, ca, cb, hi):
    nb = a.ndim - 2
    batch = tuple(range(nb))
    dims = (((ca + nb,), (cb + nb,)), (batch, batch))
    if hi == THREE_PASS:
        a, b = a.astype(F32), b.astype(F32)
        ah, bh = a.astype(BF), b.astype(BF)
        al, bl = (a - ah.astype(F32)).astype(BF), (b - bh.astype(F32)).astype(BF)
        dot = lambda p, q: lax.dot_general(p, q, dims, preferred_element_type=F32)
        return dot(ah, bh) + (dot(ah, bl) + dot(al, bh))
    if hi:
        return lax.dot_general(a.astype(F32), b.astype(F32), dims,
                               precision=lax.Precision.HIGHEST, preferred_element_type=F32)
    return lax.dot_general(a.astype(BF), b.astype(BF), dims, preferred_element_type=F32)


@functools.partial(jax.custom_vjp, nondiff_argnums=(2, 3, 4))
def mm(a, b, ca, cb, hi=False):
    return _dg(a, b, ca, cb, hi)


def _mm_fwd(a, b, ca, cb, hi):
    return _dg(a, b, ca, cb, hi), (a, b)


def _mm_bwd(ca, cb, hi, res, g):
    a, b = res
    if ca == 1:
        da = mm(g, b, 1, 1 - cb, hi)
    else:
        da = mm(b, g, 1 - cb, 1, hi)
    if cb == 0:
        db = mm(a, g, 1 - ca, 0, hi)
    else:
        db = mm(g, a, 0, 1 - ca, hi)
    return da.astype(a.dtype), db.astype(b.dtype)


mm.defvjp(_mm_fwd, _mm_bwd)


def matmul(name, a, b, mode, epi=None, extras=(), out_dtypes=(F32,), colshard=False,
           tm=None, tn=1024, tk=2048, after=None, b_window=None):
    afters = [] if after is None else [after]
    slabs = b.ndim == 3
    if slabs:
        assert mode in ("nn", "nt") and not colshard, name
        b_rows, b_cols = b.shape[1], b.shape[0] * b.shape[2]
    else:
        b_rows, b_cols = b.shape
    if b_window is not None:
        assert mode == "nt" and not slabs, name
        b_start, b_rows = b_window
    if mode == "nn":
        (M, K), (K2, N) = a.shape, (b_rows, b_cols)
    elif mode == "nt":
        (M, K), (N, K2) = a.shape, (b_rows, b_cols)
    else:
        (K, M), (K2, N) = a.shape, b.shape
    assert K == K2, (name, a.shape, b.shape, mode)
    if tm is None:
        tm = 1024
    tm = _tile(M, tm)
    tn = N // N_DEV if colshard else _tile(N, tn, LANE)
    tk = _tile(K, tk, LANE)
    if slabs and mode == "nn":
        tn = b.shape[2]
    if slabs and mode == "nt":
        tk = b.shape[2]
    nk = K // tk
    grid = (M // tm, N // tn, nk)
    if mode == "nn":
        a_spec = pl.BlockSpec((tm, tk), lambda i, j, k: (i, k))
        b_spec = (pl.BlockSpec((None, tk, tn), lambda i, j, k: (j, k, 0)) if slabs
                  else pl.BlockSpec((tk, tn), lambda i, j, k: (k, j)))
        ca, cb = 1, 0
    elif mode == "nt":
        a_spec = pl.BlockSpec((tm, tk), lambda i, j, k: (i, k))
        off = 0
        if b_window is not None:
            assert b_start % tn == 0, (name, b_start, tn)
            off = b_start // tn
        b_spec = (pl.BlockSpec((None, tn, tk), lambda i, j, k: (k, j, 0)) if slabs
                  else pl.BlockSpec((tn, tk), lambda i, j, k: (j + off, k)))
        ca, cb = 1, 1
    else:
        a_spec = pl.BlockSpec((tk, tm), lambda i, j, k: (k, i))
        b_spec = pl.BlockSpec((tk, tn), lambda i, j, k: (k, j))
        ca, cb = 0, 0
    ex_specs = []
    for e in extras:
        if e.shape[0] == 1:
            ex_specs.append(pl.BlockSpec((1, tn), lambda i, j, k: (0, j)))
        else:
            assert e.shape == (M, N), (name, e.shape)
            ex_specs.append(pl.BlockSpec((tm, tn), lambda i, j, k: (i, j)))
    if colshard:
        out_shape = [jax.ShapeDtypeStruct((N_DEV, M, tn), dt) for dt in out_dtypes]
        out_specs = [pl.BlockSpec((None, tm, tn), lambda i, j, k: (j, i, 0)) for _ in out_dtypes]
    else:
        out_shape = [jax.ShapeDtypeStruct((M, N), dt) for dt in out_dtypes]
        out_specs = [pl.BlockSpec((tm, tn), lambda i, j, k: (i, j)) for _ in out_dtypes]
    n_ex, n_out = len(extras), len(out_dtypes)

    def body(*refs):
        a_ref, b_ref = refs[0], refs[1]
        ex_refs = refs[2:2 + n_ex]
        first_out = 2 + n_ex + len(afters)
        o_refs = refs[first_out:first_out + n_out]
        part = _dg(a_ref[...], b_ref[...], ca, cb, False)

        def finish(acc):
            res = (acc,) if epi is None else epi(acc, *[r[...] for r in ex_refs])
            for o_ref, r in zip(o_refs, res):
                o_ref[...] = r.astype(o_ref.dtype)

        if nk == 1:
            finish(part)
            return
        acc_ref = refs[-1]
        k = pl.program_id(2)

        @pl.when(k == 0)
        def _():
            acc_ref[...] = part

        @pl.when(k > 0)
        def _():
            acc_ref[...] += part

        @pl.when(k == nk - 1)
        def _():
            finish(acc_ref[...])

    outs = pl.pallas_call(
        body, name=name, grid=grid,
        in_specs=[a_spec, b_spec] + ex_specs + [ANY] * len(afters),
        out_specs=out_specs, out_shape=out_shape,
        scratch_shapes=[pltpu.VMEM((tm, tn), F32)] if nk > 1 else [],
        compiler_params=pltpu.CompilerParams(
            dimension_semantics=("parallel", "parallel", "arbitrary"),
            vmem_limit_bytes=VMEM_LIMIT),
    )(a, b, *extras, *afters)
    return outs[0] if n_out == 1 else tuple(outs)


def norm_matmul(name, x, g, b, epi=None, extras=(), out_dtypes=(F32,), tm=1024, tn=1024, after=None):
    M, K = x.shape
    slabs = b.ndim == 3
    N = b.shape[0] * b.shape[2] if slabs else b.shape[1]
    tm, tn = _tile(M, tm), (b.shape[2] if slabs else _tile(N, tn, LANE))
    b_spec = (pl.BlockSpec((None, K, tn), lambda i, j: (j, 0, 0)) if slabs
              else pl.BlockSpec((K, tn), lambda i, j: (0, j)))
    afters = [] if after is None else [after]
    ex_specs = [pl.BlockSpec((1, tn), lambda i, j: (0, j)) if e.shape[0] == 1
                else pl.BlockSpec((tm, tn), lambda i, j: (i, j)) for e in extras]
    n_ex, n_out = len(extras), len(out_dtypes)

    def body(*refs):
        x_ref, g_ref, b_ref = refs[:3]
        ex_refs = refs[3:3 + n_ex]
        first_out = 3 + n_ex + len(afters)
        o_refs = refs[first_out:first_out + n_out]
        h_ref, h_scr = refs[first_out + n_out], refs[-1]

        @pl.when(pl.program_id(1) == 0)
        def _():
            h = rms_tile(x_ref[...], g_ref[...]).astype(h_scr.dtype)
            h_scr[...] = h
            h_ref[...] = h

        acc = _dg(h_scr[...], b_ref[...], 1, 0, False)
        res = (acc,) if epi is None else epi(acc, *[r[...] for r in ex_refs])
        for o_ref, r in zip(o_refs, res):
            o_ref[...] = r.astype(o_ref.dtype)

    outs = pl.pallas_call(
        body, name=name, grid=(M // tm, N // tn),
        in_specs=[pl.BlockSpec((tm, K), lambda i, j: (i, 0)), pl.BlockSpec((1, K), lambda i, j: (0, 0)),
                  b_spec] + ex_specs + [ANY] * len(afters),
        out_specs=[pl.BlockSpec((tm, tn), lambda i, j: (i, j)) for _ in out_dtypes]
        + [pl.BlockSpec((tm, K), lambda i, j: (i, 0))],
        out_shape=[jax.ShapeDtypeStruct((M, N), dt) for dt in out_dtypes] + [jax.ShapeDtypeStruct((M, K), BF)],
        scratch_shapes=[pltpu.VMEM((tm, K), BF)],
        compiler_params=pltpu.CompilerParams(
            dimension_semantics=("parallel", "arbitrary"), vmem_limit_bytes=VMEM_LIMIT),
    )(x, g, b, *extras, *afters)
    return tuple(outs)


def matmul_sum(name, pairs, tm=512, tn=1024, after=None):
    pairs = [(a, b if isinstance(b, tuple) else (b, 0, b.shape[0])) for a, b in pairs]
    M, N = pairs[0][0].shape[0], pairs[0][1][0].shape[1]
    tm, tn = _tile(M, tm), _tile(N, tn, LANE)
    afters = [] if after is None else [after]
    in_specs, operands = [], []
    for a, (b, start, size) in pairs:
        assert a.shape == (M, size) and b.shape[1] == N and start % size == 0, (name, a.shape, b.shape, start)
        in_specs += [pl.BlockSpec((tm, size), lambda i, j: (i, 0)),
                     pl.BlockSpec((size, tn), lambda i, j, blk=start // size: (blk, j))]
        operands += [a, b]
    n = len(pairs)

    def body(*refs):
        acc = _dg(refs[0][...], refs[1][...], 1, 0, False)
        for p in range(1, n):
            acc = acc + _dg(refs[2 * p][...], refs[2 * p + 1][...], 1, 0, False)
        refs[-1][...] = acc

    return pl.pallas_call(
        body, name=name, grid=(M // tm, N // tn),
        in_specs=in_specs + [ANY] * len(afters),
        out_specs=pl.BlockSpec((tm, tn), lambda i, j: (i, j)),
        out_shape=jax.ShapeDtypeStruct((M, N), F32),
        compiler_params=pltpu.CompilerParams(
            dimension_semantics=("parallel", "parallel"), vmem_limit_bytes=VMEM_LIMIT),
    )(*operands, *afters)


def rowcall(name, fn, rows, consts, out_rows, out_accs=(), tr=256, sp=None, R=None, after=None,
            out_lead=None, into=None):
    afters = ([] if after is None else [after]) + ([] if into is None else list(into))
    rows = [r if isinstance(r, tuple) else (r, None) for r in rows]
    R = rows[0][0].shape[-2] if R is None else R
    tr = _tile(R, tr)
    n = R // tr
    in_specs = []
    for arr, lead in rows:
        C = arr.shape[-1]
        if lead is None:
            assert arr.shape[-2] == R, (name, arr.shape, R)
            in_specs.append(pl.BlockSpec((tr, C), lambda i, *s: (i, 0)))
        elif callable(lead):
            in_specs.append(pl.BlockSpec((tr, C), lambda i, *s, lead=lead: (lead(i, n, *s), 0)))
        else:
            assert arr.shape[-2] == R, (name, arr.shape, R)
            in_specs.append(pl.BlockSpec((None, tr, C), lambda i, *s, lead=lead: (lead, i, 0)))
    for c in consts:
        in_specs.append(pl.BlockSpec(c.shape, lambda i, *s, nd=c.ndim: (0,) * nd))
    if out_lead is None:
        out_shape = [jax.ShapeDtypeStruct((R, C), dt) for C, dt in out_rows]
        out_specs = [pl.BlockSpec((tr, C), lambda i, *s: (i, 0)) for C, _ in out_rows]
    else:
        n_slab, slab = out_lead
        out_shape = [jax.ShapeDtypeStruct((n_slab, R, C), dt) for C, dt in out_rows]
        out_specs = [pl.BlockSpec((None, tr, C), lambda i, *s: (slab, i, 0)) for C, _ in out_rows]
    for shp in out_accs:
        out_shape.append(jax.ShapeDtypeStruct(shp, F32))
        out_specs.append(pl.BlockSpec(shp, lambda i, *s, nd=len(shp): (0,) * nd))
    n_in, n_row, n_acc = len(rows) + len(consts), len(out_rows), len(out_accs)
    n_sp = 0 if sp is None else 1

    def body(*refs):
        refs = refs[n_sp:]
        ins = [r[...] for r in refs[:n_in]]
        res = fn(*ins)
        if not isinstance(res, (tuple, list)):
            res = (res,)
        o_refs = refs[n_in + len(afters):]
        for o_ref, r in zip(o_refs[:n_row], res[:n_row]):
            o_ref[...] = r.astype(o_ref.dtype)
        if n_acc:
            first = pl.program_id(0) == 0
            for o_ref, r in zip(o_refs[n_row:], res[n_row:]):
                r = r.astype(F32).reshape(o_ref.shape)

                @pl.when(first)
                def _(o_ref=o_ref, r=r):
                    o_ref[...] = r

                @pl.when(jnp.logical_not(first))
                def _(o_ref=o_ref, r=r):
                    o_ref[...] += r

    params = pltpu.CompilerParams(dimension_semantics=("arbitrary",), vmem_limit_bytes=VMEM_LIMIT)
    operands = [a for a, _ in rows] + list(consts) + afters
    in_specs = in_specs + [ANY] * len(afters)
    aliases = {} if into is None else {n_sp + len(operands) - len(into) + k: k for k in range(len(into))}
    if sp is None:
        outs = pl.pallas_call(body, name=name, grid=(n,), in_specs=in_specs, out_specs=out_specs,
                              out_shape=out_shape, input_output_aliases=aliases,
                              compiler_params=params)(*operands)
    else:
        outs = pl.pallas_call(
            body, name=name, out_shape=out_shape, compiler_params=params, input_output_aliases=aliases,
            grid_spec=pltpu.PrefetchScalarGridSpec(
                num_scalar_prefetch=1, grid=(n,), in_specs=in_specs, out_specs=out_specs),
        )(sp, *operands)
    return outs[0] if len(outs) == 1 else tuple(outs)


def slabcall(name, fn, ins, out_dtypes, sp, n_out_slabs=None, cw=512):
    R, C = ins[0][0].shape[-2:]
    cw = _tile(C, cw, LANE)
    in_specs = []
    for arr, slab in ins:
        assert arr.shape[-2:] == (R, C), (name, arr.shape)
        if slab is None:
            in_specs.append(pl.BlockSpec((R, cw), lambda i, j, s: (0, j)))
        elif callable(slab):
            in_specs.append(pl.BlockSpec((None, R, cw), lambda i, j, s, slab=slab: (slab(i, s), 0, j)))
        else:
            in_specs.append(pl.BlockSpec((None, R, cw), lambda i, j, s, slab=slab: (slab, 0, j)))
    if n_out_slabs is None:
        out_shape = [jax.ShapeDtypeStruct((R, C), dt) for dt in out_dtypes]
        out_specs = [pl.BlockSpec((R, cw), lambda i, j, s: (0, j)) for _ in out_dtypes]
    else:
        out_shape = [jax.ShapeDtypeStruct((n_out_slabs, R, C), dt) for dt in out_dtypes]
        out_specs = [pl.BlockSpec((None, R, cw), lambda i, j, s: (i, 0, j)) for _ in out_dtypes]
    n_in = len(ins)

    def body(sp_ref, *refs):
        res = fn(*[r[...] for r in refs[:n_in]])
        for o_ref, r in zip(refs[n_in:], res):
            o_ref[...] = r.astype(o_ref.dtype)

    outs = pl.pallas_call(
        body, name=name, out_shape=out_shape,
        grid_spec=pltpu.PrefetchScalarGridSpec(
            num_scalar_prefetch=1, grid=(n_out_slabs or 1, C // cw), in_specs=in_specs, out_specs=out_specs),
        compiler_params=pltpu.CompilerParams(
            dimension_semantics=("arbitrary", "arbitrary"), vmem_limit_bytes=VMEM_LIMIT),
    )(sp, *[a for a, _ in ins])
    return outs[0] if len(outs) == 1 else tuple(outs)


def rms_tile(x, g):
    x = x.astype(F32)
    return x * lax.rsqrt(jnp.mean(x * x, axis=-1, keepdims=True) + EPS) * g


def gelu(x):
    return 0.5 * x * (1.0 + lax.erf(x * (1.0 / math.sqrt(2.0))))


def ln_tile(x, g, b):
    mu = jnp.mean(x, axis=-1, keepdims=True)
    xc = x - mu
    return xc * lax.rsqrt(jnp.mean(xc * xc, axis=-1, keepdims=True) + EPS) * g + b


def lane_groups(fn, width, *arrs):
    n = arrs[0].shape[-1] // width
    outs = [fn(*[a[:, i * width:(i + 1) * width] for a in arrs]) for i in range(n)]
    return jnp.concatenate(outs, axis=-1)


def mixa_post_tile(o, z, o_norm):
    dv = o_norm.shape[-1]
    on = lane_groups(lambda t: rms_tile(t, o_norm), dv, o)
    return on * jax.nn.silu(z)


def mixb_tile(uv, ln_g, ln_b, w_s, bs_t):
    G = w_s.shape[0]
    gw = ln_g.shape[-1]
    dg = gw // G
    tr = uv.shape[0]
    u = gelu(uv[:, :gw])
    vg = gelu(uv[:, gw:])
    ii = lax.broadcasted_iota(jnp.int32, (B_BLOCK, B_BLOCK), 0)
    jj = lax.broadcasted_iota(jnp.int32, (B_BLOCK, B_BLOCK), 1)
    mask = (jj // CHUNK) <= (ii // CHUNK)
    cols = []
    for g in range(G):
        sl = slice(g * dg, (g + 1) * dg)
        vn = ln_tile(vg[:, sl], ln_g[:, sl], ln_b[:, sl])
        wm = jnp.where(mask, w_s[g], 0.0)
        blocks = []
        for m in range(tr // B_BLOCK):
            blk = vn[m * B_BLOCK:(m + 1) * B_BLOCK, :]
            blocks.append(mm(wm, blk, 1, 0) + bs_t[:, g:g + 1])
        mixed = blocks[0] if len(blocks) == 1 else jnp.concatenate(blocks, axis=0)
        cols.append(u[:, sl] * mixed)
    return jnp.concatenate(cols, axis=-1)


def bgc_tile(ba, alog_row, dtb_row):
    tr = ba.shape[0]
    beta = jax.nn.sigmoid(ba)
    g = -jnp.exp(alog_row) * jax.nn.softplus(ba + dtb_row)
    ii = lax.broadcasted_iota(jnp.int32, (tr, tr), 0)
    jj = lax.broadcasted_iota(jnp.int32, (tr, tr), 1)
    tri = jnp.where((ii // CHUNK == jj // CHUNK) & (jj <= ii), 1.0, 0.0).astype(F32)
    gc = mm(tri, g, 1, 0, True)
    return beta, gc


def make_bgc(H):
    def f(ba, alog_row, dtb_row):
        beta, gc = bgc_tile(ba, alog_row, dtb_row)
        lane = lax.broadcasted_iota(jnp.int32, ba.shape, 1)
        return jnp.where(lane < H, beta, jnp.where(lane < 2 * H, gc, 0.0))
    return f


def loss_tile(x, g, target):
    y = rms_tile(x, g)
    err = y - target
    return 0.5 * jnp.sum(jnp.mean(err * err, axis=-1, keepdims=True), axis=0, keepdims=True)


def adamw_tile(w, g, m, v):
    m = ADAM_B1 * m + (1.0 - ADAM_B1) * g
    v = ADAM_B2 * v + (1.0 - ADAM_B2) * (g * g)
    m_hat = m / (1.0 - ADAM_B1 ** ADAM_STEP)
    v_hat = v / (1.0 - ADAM_B2 ** ADAM_STEP)
    delta = -ADAM_LR * (m_hat / (jnp.sqrt(v_hat) + ADAM_EPS) + ADAM_WD * w)
    return delta, m, v


CONV_ROWS = 32


def _shifted_copies(src, sh, rows):
    for b in range(SUBLANE):
        sh[b] = src[pl.ds(b, rows), :]


def _window(sh, off, rows):
    b = off % SUBLANE
    return sh[b, pl.ds(off - b, rows), :]


def _conv_rows(out, sh, w_ref, offsets, rows):
    for r0 in range(0, rows, CONV_ROWS):
        rc = min(CONV_ROWS, rows - r0)
        acc = w_ref[0:1, :] * _window(sh, offsets[0] + r0, rc)
        for k in range(1, len(offsets)):
            acc = acc + w_ref[k:k + 1, :] * _window(sh, offsets[k] + r0, rc)
        out[r0:r0 + rc, :] = acc


def _conv_wgrad(dsrc, d0, sh, offsets, rows):
    dws = []
    for off in offsets:
        acc = None
        for r0 in range(0, rows, CONV_ROWS):
            rc = min(CONV_ROWS, rows - r0)
            prod = dsrc[d0 + r0:d0 + r0 + rc, :] * _window(sh, off + r0, rc)
            for g in range(0, rc, SUBLANE):
                part = prod[g:g + SUBLANE, :]
                acc = part if acc is None else acc + part
        dws.append(jnp.sum(acc, axis=0, keepdims=True))
    return jnp.concatenate(dws, axis=0)


def _conv_specs(T, tr, hb, cw, col_blocks, rev):
    n = T // tr

    def ri(i):
        return (n - 1 - i) if rev else i

    tile_specs, halo_specs = [], []
    for off in col_blocks:
        tile_specs.append(pl.BlockSpec((tr, cw), lambda j, i, off=off: (ri(i), j + off)))
        halo_specs.append(pl.BlockSpec(
            (hb, cw), lambda j, i, off=off: (jnp.maximum(ri(i) * (tr // hb) - 1, 0), j + off)))
    return n, ri, tile_specs, halo_specs


def conv_fwd(name, x, w, consts, pre, post, col_blocks, n_out, K, out_dtype=F32, tr=256, hb=8, cw=512):
    T = x.shape[0]
    C = w.shape[1]
    tr, cw = _tile(T, tr, hb), min(cw, C)
    nb = len(col_blocks)
    n, ri, tile_specs, halo_specs = _conv_specs(T, tr, hb, cw, col_blocks, False)
    w_spec = pl.BlockSpec((K, cw), lambda j, i: (0, j))
    c_specs = [pl.BlockSpec((1, cw), lambda j, i: (0, j)) for _ in consts]

    def body(*refs):
        tiles = [r[...] for r in refs[:nb]]
        halos = [r[...] for r in refs[nb:2 * nb]]
        w_ref = refs[2 * nb]
        cs = [r[...] for r in refs[2 * nb + 1:2 * nb + 1 + len(consts)]]
        o_refs = refs[2 * nb + 1 + len(consts):-3]
        pbuf, shp, cbuf = refs[-3:]
        i = pl.program_id(1)
        pbuf[0:hb, :] = jnp.where(i > 0, pre(*halos), 0.0)
        pbuf[hb:hb + tr, :] = pre(*tiles)
        pbuf[hb + tr:hb + tr + SUBLANE, :] = jnp.zeros((SUBLANE, cw), F32)
        _shifted_copies(pbuf, shp, hb + tr)
        _conv_rows(cbuf, shp, w_ref, [hb - (K - 1) + k for k in range(K)], tr)
        res = post(cbuf[...], pl.program_id(0), *cs)
        for o_ref, r in zip(o_refs, res):
            o_ref[...] = r.astype(o_ref.dtype)

    outs = pl.pallas_call(
        body, name=name, grid=(C // cw, n),
        in_specs=tile_specs + halo_specs + [w_spec] + c_specs,
        out_specs=[pl.BlockSpec((tr, cw), lambda j, i: (i, j)) for _ in range(n_out)],
        out_shape=[jax.ShapeDtypeStruct((T, C), out_dtype) for _ in range(n_out)],
        scratch_shapes=[pltpu.VMEM((hb + tr + SUBLANE, cw), F32), pltpu.VMEM((SUBLANE, hb + tr, cw), F32),
                        pltpu.VMEM((tr, cw), F32)],
        compiler_params=pltpu.CompilerParams(
            dimension_semantics=("parallel", "arbitrary"), vmem_limit_bytes=VMEM_LIMIT),
    )(*([x] * nb), *([x] * nb), w, *consts)
    return outs[0] if n_out == 1 else tuple(outs)


def conv_bwd(name, x, w, consts, grads, pre, post, col_blocks, K, tr=256, hb=8, cw=512, recompute=True):
    T, Cx = x.shape
    C = w.shape[1]
    tr, cw = _tile(T, tr, hb), min(cw, C)
    nb = len(col_blocks)
    n, ri, tile_specs, halo_specs = _conv_specs(T, tr, hb, cw, col_blocks, True)
    w_spec = pl.BlockSpec((K, cw), lambda j, i: (0, j))
    c_specs = [pl.BlockSpec((1, cw), lambda j, i: (0, j)) for _ in consts]
    g_specs = [pl.BlockSpec((tr, cw), lambda j, i: (ri(i), j)) for _ in grads]
    nc, ng = len(consts), len(grads)

    def body(*refs):
        p = 0
        tile_refs = refs[p:p + nb]; p += nb
        halo_refs = refs[p:p + nb]; p += nb
        w_ref = refs[p]; p += 1
        cs = [r[...] for r in refs[p:p + nc]]; p += nc
        gs = [r[...] for r in refs[p:p + ng]]; p += ng
        dx_refs = refs[p:p + nb]; p += nb
        sum_refs = refs[p:p + nb]; p += nb
        dw_ref = refs[p]; p += 1
        dc_refs = refs[p:p + nc]; p += nc
        pbuf, dbuf, ebuf, carry, shp, shd, cbuf = refs[p:p + 7]
        i = pl.program_id(1)
        first = i == 0
        tiles = [r[...] for r in tile_refs]
        halos = [r[...] for r in halo_refs]
        p_tile, vjp_pre = jax.vjp(pre, *tiles)
        pbuf[0:hb, :] = jnp.where(ri(i) > 0, pre(*halos), 0.0)
        pbuf[hb:hb + tr, :] = p_tile
        pbuf[hb + tr:hb + tr + SUBLANE, :] = jnp.zeros((SUBLANE, cw), F32)
        _shifted_copies(pbuf, shp, hb + tr)
        taps = [hb - (K - 1) + k for k in range(K)]
        if recompute:
            _conv_rows(cbuf, shp, w_ref, taps, tr)
            c = cbuf[...]
        else:
            c = jnp.zeros((tr, cw), F32)
        cid = pl.program_id(0)
        _, vjp_post = jax.vjp(lambda c_, *cs_: post(c_, cid, *cs_), c, *cs)
        dres = vjp_post(tuple(g.astype(F32) for g in gs))
        dbuf[0:hb, :] = jnp.zeros((hb, cw), F32)
        dbuf[hb:hb + tr, :] = dres[0]
        dbuf[hb + tr:hb + tr + hb + SUBLANE, :] = jnp.zeros((hb + SUBLANE, cw), F32)
        _shifted_copies(dbuf, shd, hb + tr + hb)
        _conv_rows(ebuf, shd, w_ref, [K - 1 - k for k in range(K)], hb + tr)
        dw = _conv_wgrad(dbuf, hb, shp, taps, tr)

        @pl.when(jnp.logical_not(first))
        def _():
            ebuf[tr:tr + hb, :] += carry[...]

        carry[...] = ebuf[0:hb, :]
        dtiles = vjp_pre(ebuf[hb:hb + tr, :])
        for r, s, d in zip(dx_refs, sum_refs, dtiles):
            r[...] = d.astype(r.dtype)
            ds_ = jnp.sum(d, axis=0, keepdims=True)

            @pl.when(first)
            def _(s=s, ds_=ds_):
                s[...] = ds_

            @pl.when(jnp.logical_not(first))
            def _(s=s, ds_=ds_):
                s[...] += ds_

        accs = [(dw_ref, dw)] + [(r, d) for r, d in zip(dc_refs, dres[1:])]
        for r, d in accs:
            @pl.when(first)
            def _(r=r, d=d):
                r[...] = d

            @pl.when(jnp.logical_not(first))
            def _(r=r, d=d):
                r[...] += d

    n_cb = C // cw
    outs = pl.pallas_call(
        body, name=name, grid=(n_cb, n),
        in_specs=tile_specs + halo_specs + [w_spec] + c_specs + g_specs,
        out_specs=([pl.BlockSpec((tr, cw), lambda j, i: (ri(i), j)) for _ in col_blocks]
                   + [pl.BlockSpec((1, cw), lambda j, i: (0, j)) for _ in col_blocks]
                   + [pl.BlockSpec((K, cw), lambda j, i: (0, j))]
                   + [pl.BlockSpec((1, cw), lambda j, i: (0, j)) for _ in consts]),
        out_shape=([jax.ShapeDtypeStruct((T, C), BF) for _ in col_blocks]
                   + [jax.ShapeDtypeStruct((1, C), F32) for _ in col_blocks]
                   + [jax.ShapeDtypeStruct((K, C), F32)]
                   + [jax.ShapeDtypeStruct((1, C), F32) for _ in consts]),
        scratch_shapes=[pltpu.VMEM((hb + tr + SUBLANE, cw), F32), pltpu.VMEM((hb + tr + hb + SUBLANE, cw), F32),
                        pltpu.VMEM((hb + tr, cw), F32), pltpu.VMEM((hb, cw), F32),
                        pltpu.VMEM((SUBLANE, hb + tr, cw), F32), pltpu.VMEM((SUBLANE, hb + tr + hb, cw), F32),
                        pltpu.VMEM((tr, cw), F32)],
        compiler_params=pltpu.CompilerParams(
            dimension_semantics=("parallel", "arbitrary"), vmem_limit_bytes=VMEM_LIMIT),
    )(*([x] * nb), *([x] * nb), w, *consts, *grads)
    dxs = outs[:nb]
    sums = outs[nb:2 * nb]
    dw = outs[2 * nb]
    dcs = outs[2 * nb + 1:]
    return dxs, sums, dw, dcs


def make_qkv_post(dk, cw, n_qk_chunks):
    def l2(t):
        return t * lax.rsqrt(jnp.sum(t * t, axis=-1, keepdims=True) + EPS)

    def post(c, cid):
        s = jax.nn.silu(c)
        normed = lane_groups(l2, dk, s)
        return (jnp.where(cid < n_qk_chunks, normed, s),)
    return post


def glu_pre(za, zb):
    return za * jax.nn.sigmoid(zb)


def bias_post(c, cid, b):
    return (c + b,)


def _col_to_row(col):
    C = col.shape[-2]
    ii = lax.broadcasted_iota(jnp.int32, (C, C), 0)
    jj = lax.broadcasted_iota(jnp.int32, (C, C), 1)
    wide = jnp.broadcast_to(col, col.shape[:-1] + (C,))
    return jnp.sum(jnp.where(ii == jj, wide, 0.0), axis=-2, keepdims=True)


@jax.custom_vjp
def solve_with_inverse(a, rhs, x):
    return mm(x, rhs, 1, 0, THREE_PASS)


def _swi_fwd(a, rhs, x):
    sol = mm(x, rhs, 1, 0, THREE_PASS)
    return sol, (x, sol)


def _swi_bwd(res, dsol):
    x, sol = res
    drhs = mm(x, dsol, 0, 0, THREE_PASS)
    da = -mm(drhs, sol, 1, 1, THREE_PASS)
    return da, drhs, jnp.zeros_like(x)


solve_with_inverse.defvjp(_swi_fwd, _swi_bwd)


def unit_lower_inverse(a):
    C = a.shape[-1]
    ii = lax.broadcasted_iota(jnp.int32, (C, C), 0)
    jj = lax.broadcasted_iota(jnp.int32, (C, C), 1)
    x = jnp.where(ii == jj, 1.0, 0.0).astype(F32) - a
    p = mm(a, a, 1, 0, THREE_PASS)
    steps = int(math.log2(C)) - 1
    for s in range(steps):
        x = x + mm(x, p, 1, 0, THREE_PASS)
        if s < steps - 1:
            p = mm(p, p, 1, 0, THREE_PASS)
    return x


def dn_masks(C):
    ii = lax.broadcasted_iota(jnp.int32, (C, C), 0)
    jj = lax.broadcasted_iota(jnp.int32, (C, C), 1)
    return ii >= jj, ii > jj


def dn_pre(q, k, v, beta, gc):
    C, dk = q.shape[-2:]
    tri, strict = dn_masks(C)
    q = q * (dk ** -0.5)
    diff = gc - _col_to_row(gc)
    decay = jnp.where(tri, jnp.exp(jnp.where(tri, diff, 0.0)), 0.0)
    kb = k * beta
    vb = v * beta
    a = jnp.where(strict, mm(kb, k, 1, 1) * decay, 0.0)
    eg = jnp.exp(gc)
    rhs = jnp.concatenate([vb, kb * eg], axis=-1)
    attn = mm(q, k, 1, 1) * decay
    qd = q * eg
    g_last = gc[..., C - 1:C, :]
    kt = k * jnp.exp(g_last - gc)
    gl = jnp.exp(g_last)
    return a, rhs, attn, qd, kt, gl


def dn_chunk(q, k, v, beta, gc, state, x):
    dv = v.shape[-1]
    a, rhs, attn, qd, kt, gl = dn_pre(q, k, v, beta, gc)
    sol = solve_with_inverse(a, rhs, x)
    u, w = sol[..., :dv], sol[..., dv:]
    vn = u - mm(w, state, 1, 0)
    o = mm(qd, state, 1, 0) + mm(attn, vn, 1, 0)
    new_state = state * gl + mm(kt, vn, 0, 0)
    return o, new_state


def _by_head(q_ref, k_ref, v_ref, bg, H, dk, dv):
    qv = jnp.stack([q_ref[:, h * dk:(h + 1) * dk] for h in range(H)])
    kv = jnp.stack([k_ref[:, h * dk:(h + 1) * dk] for h in range(H)])
    vv = jnp.stack([v_ref[:, h * dv:(h + 1) * dv] for h in range(H)])
    beta = jnp.stack([bg[:, h:h + 1] for h in range(H)])
    gc = jnp.stack([bg[:, H + h:H + h + 1] for h in range(H)])
    return qv, kv, vv, beta, gc


def deltanet_fwd(qkv, bgc, H):
    T = qkv.shape[0]
    dk = dv = qkv.shape[1] // (3 * H)
    N = T // CHUNK

    def body(q_ref, k_ref, v_ref, bgc_ref, o_ref, x_ref, s_ref, state):
        @pl.when(pl.program_id(0) == 0)
        def _():
            state[...] = jnp.zeros((H, dk, dv), F32)

        qv, kv, vv, beta, gc = _by_head(q_ref, k_ref, v_ref, bgc_ref[...], H, dk, dv)
        a = dn_pre(qv, kv, vv, beta, gc)[0]
        x = unit_lower_inverse(a)
        s = state[...]
        o, s_new = dn_chunk(qv, kv, vv, beta, gc, s, x)
        for h in range(H):
            o_ref[:, h * dv:(h + 1) * dv] = o[h]
        x_ref[...] = x
        s_ref[...] = s
        state[...] = s_new

    return pl.pallas_call(
        body, name="deltanet_fwd", grid=(N,),
        in_specs=[pl.BlockSpec((CHUNK, H * dk), lambda n: (n, 0)),
                  pl.BlockSpec((CHUNK, H * dk), lambda n: (n, 1)),
                  pl.BlockSpec((CHUNK, H * dv), lambda n: (n, 2)),
                  pl.BlockSpec((CHUNK, LANE), lambda n: (n, 0))],
        out_specs=[pl.BlockSpec((CHUNK, H * dv), lambda n: (n, 0)),
                   pl.BlockSpec((None, H, CHUNK, CHUNK), lambda n: (n, 0, 0, 0)),
                   pl.BlockSpec((None, H, dk, dv), lambda n: (n, 0, 0, 0))],
        out_shape=[jax.ShapeDtypeStruct((T, H * dv), F32),
                   jax.ShapeDtypeStruct((N, H, CHUNK, CHUNK), F32),
                   jax.ShapeDtypeStruct((N, H, dk, dv), F32)],
        scratch_shapes=[pltpu.VMEM((H, dk, dv), F32)],
        compiler_params=pltpu.CompilerParams(
            dimension_semantics=("arbitrary",), vmem_limit_bytes=VMEM_LIMIT),
    )(qkv, qkv, qkv, bgc)


def deltanet_bwd(qkv, bgc, xinv, states, do, H):
    T = qkv.shape[0]
    dk = dv = qkv.shape[1] // (3 * H)
    N = T // CHUNK

    def body(q_ref, k_ref, v_ref, bgc_ref, x_ref, s_ref, do_ref, dqkv_ref, dbgc_ref, dstate):
        @pl.when(pl.program_id(0) == 0)
        def _():
            dstate[...] = jnp.zeros((H, dk, dv), F32)

        qv, kv, vv, beta, gc = _by_head(q_ref, k_ref, v_ref, bgc_ref[...], H, dk, dv)
        do = jnp.stack([do_ref[:, h * dv:(h + 1) * dv] for h in range(H)])
        _, vjp = jax.vjp(dn_chunk, qv, kv, vv, beta, gc, s_ref[...], x_ref[...])
        dq, dk_, dv_, dbeta, dgc, ds, _ = vjp((do, dstate[...]))
        dstate[...] = ds
        lane = lax.broadcasted_iota(jnp.int32, (CHUNK, LANE), 1)
        dbgc = jnp.zeros((CHUNK, LANE), F32)
        for h in range(H):
            dqkv_ref[:, h * dk:(h + 1) * dk] = dq[h]
            dqkv_ref[:, (H + h) * dk:(H + h + 1) * dk] = dk_[h]
            dqkv_ref[:, (2 * H + h) * dk:(2 * H + h + 1) * dk] = dv_[h]
            dbgc = dbgc + jnp.where(lane == h, dbeta[h], 0.0) + jnp.where(lane == h + H, dgc[h], 0.0)
        dbgc_ref[...] = dbgc

    rn = lambda n: N - 1 - n
    return pl.pallas_call(
        body, name="deltanet_bwd", grid=(N,),
        in_specs=[pl.BlockSpec((CHUNK, H * dk), lambda n: (rn(n), 0)),
                  pl.BlockSpec((CHUNK, H * dk), lambda n: (rn(n), 1)),
                  pl.BlockSpec((CHUNK, H * dv), lambda n: (rn(n), 2)),
                  pl.BlockSpec((CHUNK, LANE), lambda n: (rn(n), 0)),
                  pl.BlockSpec((None, H, CHUNK, CHUNK), lambda n: (rn(n), 0, 0, 0)),
                  pl.BlockSpec((None, H, dk, dv), lambda n: (rn(n), 0, 0, 0)),
                  pl.BlockSpec((CHUNK, H * dv), lambda n: (rn(n), 0))],
        out_specs=[pl.BlockSpec((CHUNK, 3 * H * dk), lambda n: (rn(n), 0)),
                   pl.BlockSpec((CHUNK, LANE), lambda n: (rn(n), 0))],
        out_shape=[jax.ShapeDtypeStruct((T, 3 * H * dk), F32),
                   jax.ShapeDtypeStruct((T, LANE), F32)],
        scratch_shapes=[pltpu.VMEM((H, dk, dv), F32)],
        compiler_params=pltpu.CompilerParams(
            dimension_semantics=("arbitrary",), vmem_limit_bytes=VMEM_LIMIT),
    )(qkv, qkv, qkv, bgc, xinv, states, do)


def _place():
    x, y, c = lax.axis_index("x"), lax.axis_index("y"), lax.axis_index("c")
    chips = [(1 - x, y), (x, 1 - y), (1 - x, 1 - y)]
    return x, y, c, chips


def all_gather(name, shards):
    na = len(shards)

    def body(*refs):
        ins, outs = refs[:na], refs[na:2 * na]
        send_sems, recv_sems, local_sems = refs[2 * na:]
        x, y, c, chips = _place()
        me, sibling = (x, y, c), (x, y, 1 - c)

        def copy(a, k, block, to, src=None):
            dst = outs[a].at[4 * block[0] + 2 * block[1] + block[2]]
            return pltpu.make_async_remote_copy(
                src_ref=dst if src is None else src, dst_ref=dst,
                send_sem=send_sems.at[a, k], recv_sem=recv_sems.at[a, k],
                device_id=to, device_id_type=MESH)

        mine, first, passed = [], [], []
        for a in range(na):
            cp = pltpu.make_async_copy(ins[a], outs[a].at[4 * x + 2 * y + c], local_sems.at[a])
            cp.start()
            mine.append(cp)
        for a in range(na):
            cps = [copy(a, 0, me, sibling, src=ins[a])]
            cps += [copy(a, 1 + j, me, (*chip, c), src=ins[a]) for j, chip in enumerate(chips)]
            for cp in cps:
                cp.start()
            first += cps
        for a in range(na):
            for j, chip in enumerate(chips):
                copy(a, 1 + j, (*chip, c), me).wait_recv()
                cp = copy(a, 4 + j, (*chip, c), sibling)
                cp.start()
                passed.append(cp)
        for a in range(na):
            copy(a, 0, sibling, me).wait_recv()
            for j, chip in enumerate(chips):
                copy(a, 4 + j, (*chip, 1 - c), me).wait_recv()
        for cp in first + passed:
            cp.wait_send()
        for cp in mine:
            cp.wait()

    outs = pl.pallas_call(
        body, name=name,
        in_specs=[ANY] * na, out_specs=[ANY] * na,
        out_shape=[jax.ShapeDtypeStruct((N_DEV,) + s.shape, s.dtype) for s in shards],
        scratch_shapes=[pltpu.SemaphoreType.DMA((na, 7)), pltpu.SemaphoreType.DMA((na, 7)),
                        pltpu.SemaphoreType.DMA((na,))],
    )(*shards)
    return list(outs)


HBM_SPEC = pl.BlockSpec(memory_space=pltpu.HBM)
SEM_SPEC = pl.BlockSpec(memory_space=pltpu.SEMAPHORE)
EFFECT = pltpu.SideEffectType.DATAFLOW_SIDE_EFFECTING


def _descriptors(plan, bufs, send_sems, recv_sems):
    return [pltpu.make_async_remote_copy(src_ref=src, dst_ref=dst, send_sem=send_sems.at[k],
                                         recv_sem=recv_sems.at[k], device_id=dev, device_id_type=MESH)
            for k, (src, dst, dev) in enumerate(plan(bufs))]


def split_start(name, bufs, plan, n, after):
    nb = len(bufs)

    def body(*refs):
        for cp in _descriptors(plan, refs[:nb], refs[nb + 1], refs[nb + 2]):
            cp.start()
        refs[-1][...] = jnp.zeros((SUBLANE, LANE), F32)

    outs = pl.pallas_call(
        body, name=name,
        out_shape=(pltpu.SemaphoreType.DMA((n,)), pltpu.SemaphoreType.DMA((n,)),
                   *[pltpu.HBM(b.shape, b.dtype) for b in bufs],
                   jax.ShapeDtypeStruct((SUBLANE, LANE), F32)),
        in_specs=[HBM_SPEC] * nb + [ANY],
        out_specs=(SEM_SPEC, SEM_SPEC, *[HBM_SPEC] * nb, pl.BlockSpec(memory_space=pltpu.VMEM)),
        input_output_aliases={i: 2 + i for i in range(nb)},
        compiler_params=pltpu.CompilerParams(has_side_effects=EFFECT),
    )(*[pltpu.with_memory_space_constraint(b, pltpu.HBM) for b in bufs], after)
    return outs[0], outs[1], list(outs[2:2 + nb]), outs[-1]


def split_wait(name, send_sems, recv_sems, bufs, plan, after):
    nb = len(bufs)

    def body(*refs):
        cps = _descriptors(plan, refs[:nb], refs[nb], refs[nb + 1])
        for cp in cps:
            cp.wait_recv()
        for cp in cps:
            cp.wait_send()
        refs[-1][...] = jnp.zeros((SUBLANE, LANE), F32)

    afters = list(after) if isinstance(after, (list, tuple)) else [after]
    outs = pl.pallas_call(
        body, name=name,
        out_shape=[pltpu.HBM(b.shape, b.dtype) for b in bufs] + [jax.ShapeDtypeStruct((SUBLANE, LANE), F32)],
        in_specs=[HBM_SPEC] * nb + [SEM_SPEC, SEM_SPEC] + [ANY] * len(afters),
        out_specs=[HBM_SPEC] * nb + [pl.BlockSpec(memory_space=pltpu.VMEM)],
        input_output_aliases={i: i for i in range(nb)},
        compiler_params=pltpu.CompilerParams(has_side_effects=EFFECT),
    )(*bufs, send_sems, recv_sems, *afters)
    return list(outs[:nb]), outs[-1]


def _block(px, py, pc):
    return 4 * px + 2 * py + pc


def plan_gather_ici(na):
    def plan(bufs):
        x, y, c, chips = _place()
        out = []
        for a in range(na):
            dst = bufs[na + a].at[_block(x, y, c)]
            out.append((bufs[a], dst, (x, y, 1 - c)))
            out += [(bufs[a], dst, (px, py, c)) for px, py in chips]
        return out
    return plan


def plan_gather_pass(na):
    def plan(bufs):
        x, y, c, chips = _place()
        out = []
        for a in range(na):
            for px, py in chips:
                blk = bufs[a].at[_block(px, py, c)]
                out.append((blk, blk, (x, y, 1 - c)))
        return out
    return plan


def plan_gather_direct(bufs):
    x, y, c, _ = _place()
    dst = bufs[1].at[_block(x, y, c)]
    flip = lambda v, f: 1 - v if f else v
    return [(bufs[0], dst, (flip(x, m >> 2 & 1), flip(y, m >> 1 & 1), flip(c, m & 1))) for m in range(1, N_DEV)]


def plan_reduce_d2d(na):
    def plan(bufs):
        x, y, c, _ = _place()
        return [(bufs[a].at[2 * s + (1 - c)], bufs[na + a].at[s], (x, y, 1 - c))
                for a in range(na) for s in range(4)]
    return plan


def plan_reduce_ici(na):
    def plan(bufs):
        x, y, c, chips = _place()
        return [(bufs[a].at[2 * px + py], bufs[na + a].at[j], (px, py, c))
                for a in range(na) for j, (px, py) in enumerate(chips)]
    return plan


def place_own(name, land, shard, dev):
    r, c = shard.shape
    tr = _tile(r, 512)

    def body(sp_ref, s_ref, land_ref, o_ref):
        o_ref[...] = s_ref[...]

    return pl.pallas_call(
        body, name=name, out_shape=jax.ShapeDtypeStruct(land.shape, land.dtype),
        grid_spec=pltpu.PrefetchScalarGridSpec(
            num_scalar_prefetch=1, grid=(r // tr,),
            in_specs=[pl.BlockSpec((tr, c), lambda i, s: (i, 0)), ANY],
            out_specs=pl.BlockSpec((None, tr, c), lambda i, s: (s[0], i, 0))),
        input_output_aliases={2: 0},
        compiler_params=pltpu.CompilerParams(dimension_semantics=("arbitrary",)),
    )(dev, shard, land)


def pack(arrs, row_mult=SUBLANE):
    pieces = []
    for a in arrs:
        f = a.reshape(-1).astype(F32)
        pad = (-f.shape[0]) % LANE
        if pad:
            f = jnp.concatenate([f, jnp.zeros((pad,), F32)])
        pieces.append(f)
    flat = jnp.concatenate(pieces)
    rows = flat.shape[0] // LANE
    pad_rows = (-rows) % row_mult
    if pad_rows:
        flat = jnp.concatenate([flat, jnp.zeros((pad_rows * LANE,), F32)])
    return flat.reshape(-1, LANE)


def unpack(buf, shapes):
    flat = buf.reshape(-1)
    outs, off = [], 0
    for shp in shapes:
        n = int(np.prod(shp))
        outs.append(flat[off:off + n].reshape(shp))
        off += n + ((-n) % LANE)
    return outs


def _vjp_rows(fn, n_row_in, n_cot):
    def bwd(*args):
        rows = args[:n_row_in]
        cots = args[n_row_in:n_row_in + n_cot]
        consts = args[n_row_in + n_cot:]
        out, vjp = jax.vjp(fn, *rows, *consts)
        if isinstance(out, (tuple, list)):
            cot = tuple(c.astype(o.dtype) for c, o in zip(cots, out))
        else:
            cot = cots[0].astype(out.dtype)
        return vjp(cot)
    return bwd


def rms_bwd(name, x, g, dh, dres, after=None):
    D = x.shape[1]
    vj = _vjp_rows(rms_tile, 1, 1)

    def f(x_, dh_, dres_, g_):
        dx, dg = vj(x_, dh_, g_)
        dx = dx + dres_
        return dx, dx, dg, jnp.sum(dx, axis=0, keepdims=True)
    return rowcall(name, f, [x, dh, dres], [g], [(D, F32), (D, BF)], [(1, D), (1, D)], after=after, tr=512)


def kernel(x, e_norm, e_w_in, e_conv_w, e_a_log, e_dt_bias, e_o_norm, e_ln_g, e_ln_b, e_w_s, e_b_s, e_w_out, o_norm, o_pw1, o_pw1_b, o_dw, o_dw_b, o_ln_g, o_ln_b, o_pw2, o_pw2_b, f_norm, f_w1, f_w2, final_norm, loss_target, m_e_norm, m_e_w_in, m_e_conv_w, m_e_a_log, m_e_dt_bias, m_e_o_norm, m_e_ln_g, m_e_ln_b, m_e_w_s, m_e_b_s, m_e_w_out, m_o_norm, m_o_pw1, m_o_pw1_b, m_o_dw, m_o_dw_b, m_o_ln_g, m_o_ln_b, m_o_pw2, m_o_pw2_b, m_f_norm, m_f_w1, m_f_w2, m_final_norm, v_e_norm, v_e_w_in, v_e_conv_w, v_e_a_log, v_e_dt_bias, v_e_o_norm, v_e_ln_g, v_e_ln_b, v_e_w_s, v_e_b_s, v_e_w_out, v_o_norm, v_o_pw1, v_o_pw1_b, v_o_dw, v_o_dw_b, v_o_ln_g, v_o_ln_b, v_o_pw2, v_o_pw2_b, v_f_norm, v_f_w1, v_f_w2, v_final_norm):
    names = ['e_norm', 'e_w_in', 'e_conv_w', 'e_a_log', 'e_dt_bias', 'e_o_norm', 'e_ln_g', 'e_ln_b', 'e_w_s', 'e_b_s', 'e_w_out', 'o_norm', 'o_pw1', 'o_pw1_b', 'o_dw', 'o_dw_b', 'o_ln_g', 'o_ln_b', 'o_pw2', 'o_pw2_b', 'f_norm', 'f_w1', 'f_w2', 'final_norm']
    W = dict(zip(names, [e_norm, e_w_in, e_conv_w, e_a_log, e_dt_bias, e_o_norm, e_ln_g, e_ln_b, e_w_s, e_b_s, e_w_out, o_norm, o_pw1, o_pw1_b, o_dw, o_dw_b, o_ln_g, o_ln_b, o_pw2, o_pw2_b, f_norm, f_w1, f_w2, final_norm]))
    Mo = dict(zip(names, [m_e_norm, m_e_w_in, m_e_conv_w, m_e_a_log, m_e_dt_bias, m_e_o_norm, m_e_ln_g, m_e_ln_b, m_e_w_s, m_e_b_s, m_e_w_out, m_o_norm, m_o_pw1, m_o_pw1_b, m_o_dw, m_o_dw_b, m_o_ln_g, m_o_ln_b, m_o_pw2, m_o_pw2_b, m_f_norm, m_f_w1, m_f_w2, m_final_norm]))
    Vo = dict(zip(names, [v_e_norm, v_e_w_in, v_e_conv_w, v_e_a_log, v_e_dt_bias, v_e_o_norm, v_e_ln_g, v_e_ln_b, v_e_w_s, v_e_b_s, v_e_w_out, v_o_norm, v_o_pw1, v_o_pw1_b, v_o_dw, v_o_dw_b, v_o_ln_g, v_o_ln_b, v_o_pw2, v_o_pw2_b, v_f_norm, v_f_w1, v_f_w2, v_final_norm]))

    T, D = x.shape[1], x.shape[2]
    H = e_a_log.shape[-1]
    dv = e_o_norm.shape[-1]
    dk = dv
    G = e_w_s.shape[1]
    AQK, AV, BW = H * dk, H * dv, e_ln_g.shape[-1]
    AQKV = 2 * AQK + AV
    in_cols = AQKV + AV + 2 * H + 2 * BW
    KA = e_conv_w.shape[1]
    KC = o_dw.shape[1]
    L = f_norm.shape[0]
    dev = 4 * lax.axis_index("x") + 2 * lax.axis_index("y") + lax.axis_index("c")
    x2d = x.reshape(T, D)
    tgt = loss_target.reshape(T, D)

    def tie(small, tok):
        return small + tok[0:1, 0:1]

    dev_sp = dev.astype(jnp.int32).reshape(1)
    where = jnp.stack([lax.axis_index("c"), 2 * lax.axis_index("x") + lax.axis_index("y")]).astype(jnp.int32)
    row = lambda a: a.reshape(1, -1).astype(F32)
    en_row = row(e_norm)

    def gather_begin(tag, shards, after):
        na = len(shards)
        lands = [lax.empty((N_DEV,) + s.shape, s.dtype) for s in shards]
        ss, rs, bufs, tok = split_start(f"gather{tag}_ici_start", shards + lands, plan_gather_ici(na), 4 * na, after)
        return (na, ss, rs, bufs), tok

    def gather_pass(tag, h, after):
        na, ss, rs, bufs = h
        bufs, tok = split_wait(f"gather{tag}_ici_wait", ss, rs, bufs, plan_gather_ici(na), after)
        ss, rs, lands, tok = split_start(f"gather{tag}_pass_start", bufs[na:], plan_gather_pass(na), 3 * na, tok)
        return (na, ss, rs, bufs[:na], lands), tok

    def gather_end(tag, h, after):
        na, ss, rs, shards, lands = h
        lands, _ = split_wait(f"gather{tag}_pass_wait", ss, rs, lands, plan_gather_pass(na), after)
        return [place_own(f"gather{tag}_own{a}", lands[a], shards[a], dev_sp) for a in range(na)]

    small_sharded = ['e_conv_w', 'o_norm', 'o_pw1_b', 'o_dw', 'o_dw_b', 'o_ln_g', 'o_ln_b', 'o_pw2_b']
    sm = all_gather("gather_small", [pack([W[n][0]]) for n in small_sharded])

    bfw = lambda w: w.astype(BF)
    hA0, tok = gather_begin("0", [bfw(jnp.swapaxes(e_w_in[0], 0, 1))], sm[0])
    hA1, tok = gather_begin("1", [bfw(e_w_out[0]), bfw(f_w1[0]), bfw(f_w2[0])], tok)
    hA2, tok = gather_begin("2", [bfw(o_pw1[0]), bfw(o_pw2[0])], tok)
    hA3, tok = gather_begin("3", [bfw(f_w1[1]), bfw(f_w2[1])], tok)
    h0 = rowcall("rms_e", rms_tile, [x2d], [en_row], [(D, BF)], after=tok, tr=512)

    full = {}
    for n, g in zip(small_sharded, sm):
        shp = W[n][0].shape
        blocks = [unpack(g[d], [shp])[0] for d in range(N_DEV)]
        full[n] = jnp.concatenate(blocks, axis=-1)
    conv_w = full['e_conv_w']
    on_row, pw1_b_row = row(full['o_norm']), row(full['o_pw1_b'])
    dw_w, dw_b_row = full['o_dw'], row(full['o_dw_b'])
    oln_g_row, oln_b_row, pw2_b_row = row(full['o_ln_g']), row(full['o_ln_b']), row(full['o_pw2_b'])
    small_names = ['e_norm', 'e_conv_w', 'e_a_log', 'e_dt_bias', 'e_o_norm', 'e_ln_g', 'e_ln_b', 'e_w_s', 'e_b_s',
                   'o_norm', 'o_pw1_b', 'o_dw', 'o_dw_b', 'o_ln_g', 'o_ln_b', 'o_pw2_b', 'f_norm', 'final_norm']
    packed_wmv = [pack([A[n] for n in small_names], 256) for A in (W, Mo, Vo)]

    hB0, tok = gather_pass("0", hA0, [h0, conv_w, dw_w, pw1_b_row] + packed_wmv)
    (g_win,) = gather_end("0", hB0, tok)
    win_t = g_win.reshape(in_cols, D)
    wt_ba = jnp.pad(win_t[AQKV + AV:AQKV + AV + 2 * H], ((0, LANE - 2 * H), (0, 0)))
    wt_uv = win_t[AQKV + AV + 2 * H:]

    alog_row = jnp.pad(row(e_a_log), ((0, 0), (H, LANE - 2 * H)))
    dtb_row = jnp.pad(row(e_dt_bias), ((0, 0), (H, LANE - 2 * H)))
    eon_row = row(e_o_norm)
    eln_g_row, eln_b_row = row(e_ln_g), row(e_ln_b)
    w_s = e_w_s[0]
    bs_t = e_b_s[0].T
    fn_rows = [row(f_norm[l]) for l in range(L)]
    fin_row = row(final_norm)

    qkv_raw = matmul("proj_qkv", h0, win_t, "nt", b_window=(0, AQKV))
    z_gate = matmul("proj_z", h0, win_t, "nt", b_window=(AQKV, AV))
    ba = matmul("proj_ba", h0, wt_ba, "nt")
    uv = matmul("proj_uv", h0, wt_uv, "nt")

    cwa = min(512, AQKV)
    qkv_post = make_qkv_post(dk, cwa, 2 * AQK // cwa)
    ident = lambda t: t
    qkv = conv_fwd("qkv_conv", qkv_raw, conv_w, [], ident, qkv_post, [0], 1, KA, cw=cwa, tr=512)
    bgc_fn = make_bgc(H)
    bgc = rowcall("bgc", bgc_fn, [ba], [alog_row, dtb_row], [(LANE, F32)])
    o_dn, xinv, states = deltanet_fwd(qkv, bgc, H)
    hB1, tok = gather_pass("1", hA1, o_dn)

    def mix_tile(o, z, uv_, o_norm_, ln_g, ln_b, w_s_, bs_t_):
        return jnp.concatenate([mixa_post_tile(o, z, o_norm_), mixb_tile(uv_, ln_g, ln_b, w_s_, bs_t_)], axis=-1)
    mix_consts = [eon_row, eln_g_row, eln_b_row, w_s, bs_t]
    mix = rowcall("mix", mix_tile, [o_dn, z_gate, uv], mix_consts, [(AV + BW, BF)], after=tok)
    g_wout, g_w1_0, g_w2_0 = gather_end("1", hB1, mix)
    wout = g_wout.reshape(-1, D)
    w1 = [jnp.moveaxis(g_w1_0, 0, 1).reshape(D, -1), None]
    w2 = [g_w2_0.reshape(-1, D), None]
    add_epi = lambda acc, r: (acc + r,)
    x1 = matmul("out_proj", mix, wout, "nn", epi=add_epi, extras=[x2d], tm=512, tn=2048)

    def relu2_epi(acc):
        r = jnp.maximum(acc, 0.0)
        return r * r, r

    hB2, tok = gather_pass("2", hA2, x1)
    a2_0, ar_0, hf0 = norm_matmul("ffn_up0", x1, fn_rows[0], g_w1_0, epi=relu2_epi, out_dtypes=(BF, BF), after=tok)
    g_pw1, g_pw2 = gather_end("2", hB2, a2_0)
    x2 = matmul("ffn_down0", a2_0, w2[0], "nn", epi=add_epi, extras=[x1])
    ffn0 = (hf0, a2_0, ar_0)
    pw1 = jnp.moveaxis(g_pw1, 0, 1).reshape(D, 2 * D)
    pw2 = g_pw2.reshape(D, D)

    bias_epi = lambda acc, b: (acc + b,)
    zc, h1 = norm_matmul("pw1", x2, on_row, g_pw1, epi=bias_epi, extras=[pw1_b_row])
    hB3, tok = gather_pass("3", hA3, zc)
    cwc = min(512, D)
    ncb = D // cwc
    cconv = conv_fwd("dw_conv", zc, dw_w, [tie(dw_b_row, tok)], glu_pre, bias_post, [0, ncb], 1, KC, hb=32, cw=cwc,
                     tr=512)
    g_w1_1, g_w2_1 = gather_end("3", hB3, cconv)
    w1[1] = jnp.moveaxis(g_w1_1, 0, 1).reshape(D, -1)
    w2[1] = g_w2_1.reshape(-1, D)
    ln_silu = lambda c, g, b: jax.nn.silu(ln_tile(c, g, b))
    s_act = rowcall("ln_silu", ln_silu, [cconv], [oln_g_row, oln_b_row], [(D, BF)], tr=512)
    x3 = matmul("pw2", s_act, pw2, "nn", epi=lambda acc, r, b: (r + (acc + b),), extras=[x2, pw2_b_row],
                tm=512, tn=2048)
    a2_1, ar_1, hf1 = norm_matmul("ffn_up1", x3, fn_rows[1], g_w1_1, epi=relu2_epi, out_dtypes=(BF, BF))
    x4 = matmul("ffn_down1", a2_1, w2[1], "nn", epi=add_epi, extras=[x3])
    ffn1 = (hf1, a2_1, ar_1)

    def loss_bwd_tile(x_, t_, g_):
        l, vjp = jax.vjp(lambda a, b: loss_tile(a, b, t_), x_, g_)
        dx, dg = vjp(jnp.ones_like(l))
        return dx, dx, l, dg
    dx4, dx4_b, loss_part, d_final = rowcall("loss_head", loss_bwd_tile, [x4, tgt], [fin_row],
                                             [(D, F32), (D, BF)], [(1, 1), (1, D)], tr=512)
    loss = lax.psum(loss_part[0, 0], ("x", "y", "c"))

    def reduce_begin(tag, grads, after):
        na = len(grads)
        lands = [lax.empty((4,) + g.shape[1:], g.dtype) for g in grads]
        ss, rs, bufs, tok = split_start(f"reduce{tag}_d2d_start", grads + lands, plan_reduce_d2d(na), 4 * na, after)
        return (na, ss, rs, bufs), tok

    def reduce_mid(tag, h, after):
        na, ss, rs, bufs = h
        bufs, tok = split_wait(f"reduce{tag}_d2d_wait", ss, rs, bufs, plan_reduce_d2d(na), after)
        parts = []
        for a, (g, rc) in enumerate(zip(bufs[:na], bufs[na:])):
            r, c = g.shape[1], g.shape[2]
            if r % SUBLANE:
                parts.append(slabcall(f"chip_sum{tag}_{a}", lambda p, q: (p + q,),
                                      [(g, lambda i, s: 2 * i + s[0]), (rc, lambda i, s: i)], [BF], where, 4))
                continue
            mine = lambda i, n, s: (2 * (i // (n // 4)) + s[0]) * (n // 4) + i % (n // 4)
            parts.append(rowcall(f"chip_sum{tag}_{a}", lambda p, q: (p + q,),
                                 [(g.reshape(N_DEV * r, c), mine), rc.reshape(4 * r, c)], [],
                                 [(c, BF)], tr=_tile(r, 512), sp=where, R=4 * r).reshape(4, r, c))
        lands = [lax.empty((3,) + p.shape[1:], p.dtype) for p in parts]
        ss, rs, bufs, tok = split_start(f"reduce{tag}_ici_start", parts + lands, plan_reduce_ici(na), 3 * na, tok)
        return (na, ss, rs, bufs), tok

    res = {}

    def reduce_end(tag, h, after, targets):
        na, ss, rs, bufs = h
        bufs, _ = split_wait(f"reduce{tag}_ici_wait", ss, rs, bufs, plan_reduce_ici(na), after)
        for a, (part, fin, (n, l)) in enumerate(zip(bufs[:na], bufs[na:], targets)):
            def f(p0, p1, p2, p3, w_, m_, v_):
                g = ((p0.astype(F32) + p1.astype(F32)) + p2.astype(F32)) + p3.astype(F32)
                return (g,) + adamw_tile(w_, g, m_, v_)
            r, C = fin.shape[-2], fin.shape[-1]
            if W[n].shape[1:] == (C, r):
                t = lambda arr: jnp.swapaxes(arr[l], 0, 1)
                outs = slabcall(f"adam{tag}_{a}", f, [(part, lambda i, s: s[1]), (fin, 0), (fin, 1), (fin, 2),
                                                      (t(W[n]), None), (t(Mo[n]), None), (t(Vo[n]), None)],
                                [F32] * 4, where)
                res[n] = tuple(jnp.swapaxes(o, 0, 1)[None] for o in outs)
                continue
            own = lambda i, n_, s: s[1] * n_ + i
            res[n] = rowcall(f"adam{tag}_{a}", f, [(part.reshape(4 * r, C), own), (fin, 0), (fin, 1), (fin, 2),
                                                   (W[n], l), (Mo[n], l), (Vo[n], l)], [],
                             [(C, F32)] * 4, tr=256, sp=where, R=r,
                             out_lead=(W[n].shape[0], l), into=res.get(n))

    dscale_epi = lambda acc, r: (acc * (2.0 * r.astype(F32)),)
    d_fnorm = [None] * L

    dpre1 = matmul("ffn_down_dx1", dx4_b, w2[1], "nt", epi=dscale_epi, extras=[ar_1], out_dtypes=(BF,))
    dw2_1 = matmul("ffn_down_dw1", a2_1, dx4_b, "tn")
    dw1_1 = matmul("ffn_up_dw1", hf1, dpre1, "tn", colshard=True)
    hD1, tok = reduce_begin("1", [dw1_1, dw2_1.reshape(N_DEV, -1, D)], dpre1)
    dhf1 = matmul("ffn_up_dx1", dpre1, w1[1], "nt", after=tok, tk=4096)
    dx3, dx3_b, d_fnorm[1], d_pw2_b = rms_bwd("rms_f_bwd1", x3, fn_rows[1], dhf1, dx4)

    ds_act = matmul("pw2_dx", dx3_b, pw2, "nt")
    hI1, tok = reduce_mid("1", hD1, ds_act)
    d_pw2 = matmul("pw2_dw", s_act, dx3_b, "tn", after=tok)
    dcconv, d_oln_g, d_oln_b = rowcall(
        "ln_silu_bwd", _vjp_rows(ln_silu, 1, 1), [cconv, ds_act], [oln_g_row, oln_b_row],
        [(D, F32)], [(1, D), (1, D)], tr=512)
    (dza, dzb), (sza, szb), d_dw, (d_dw_b,) = conv_bwd(
        "dw_conv_bwd", zc, dw_w, [tie(dw_b_row, tok)], [dcconv], glu_pre, bias_post, [0, ncb], KC, hb=32, cw=cwc, tr=256,
        recompute=False)
    dzc = jnp.concatenate([dza, dzb], axis=-1)
    d_pw1_b = jnp.concatenate([sza, szb], axis=-1)
    d_pw1 = matmul("pw1_dw", h1, dzc, "tn", colshard=True)
    dh1 = matmul("pw1_dx", dzc, pw1, "nt")
    dx2, dx2_b, d_onorm, _ = rms_bwd("rms_o_bwd", x2, on_row, dh1, dx3)
    reduce_end("1", hI1, dx2, [('f_w1', 1), ('f_w2', 1)])

    hD2, tok = reduce_begin("2", [d_pw1, d_pw2.reshape(N_DEV, -1, D)], dx2)
    dpre0 = matmul("ffn_down_dx0", dx2_b, w2[0], "nt", epi=dscale_epi, extras=[ar_0], out_dtypes=(BF,), after=tok)
    dw2_0 = matmul("ffn_down_dw0", a2_0, dx2_b, "tn")
    hI2, tok = reduce_mid("2", hD2, dw2_0)
    dw1_0 = matmul("ffn_up_dw0", hf0, dpre0, "tn", colshard=True, after=tok)
    dhf0 = matmul("ffn_up_dx0", dpre0, w1[0], "nt", tk=4096)
    dx1, dx1_b, d_fnorm[0], _ = rms_bwd("rms_f_bwd0", x1, fn_rows[0], dhf0, dx2)
    reduce_end("2", hI2, dx1, [('o_pw1', 0), ('o_pw2', 0)])

    dmix = matmul("out_proj_dx", dx1_b, wout, "nt")
    d_wout = matmul("out_proj_dw", mix, dx1_b, "tn")
    hD3, tok = reduce_begin("3", [dw1_0, dw2_0.reshape(N_DEV, -1, D), d_wout.reshape(N_DEV, -1, D)], dmix)
    do_dn, dz_gate, duv, d_eon, d_eln_g, d_eln_b, d_ws, d_bs_t = rowcall(
        "mix_bwd", _vjp_rows(mix_tile, 3, 1), [o_dn, z_gate, uv, dmix], mix_consts,
        [(AV, F32), (AV, BF), (2 * BW, BF)], [(1, dv), (1, BW), (1, BW), w_s.shape, bs_t.shape], after=tok)
    dqkv, dbgc = deltanet_bwd(qkv, bgc, xinv, states, do_dn, H)
    hI3, tok = reduce_mid("3", hD3, dbgc)
    bgc_bwd = _vjp_rows(bgc_fn, 1, 1)
    dba, d_alog_row, d_dtb_row = rowcall(
        "bgc_bwd", bgc_bwd, [ba, dbgc], [alog_row, dtb_row], [(LANE, BF)], [(1, LANE), (1, LANE)])
    (dqkv_raw,), _, d_conv_w, _ = conv_bwd(
        "qkv_conv_bwd", qkv_raw, tie(conv_w, tok), [], [dqkv], ident, qkv_post, [0], KA, cw=cwa, tr=512)

    dw_qkv = matmul("proj_qkv_dw", dqkv_raw, h0, "tn")
    dw_z = matmul("proj_z_dw", dz_gate, h0, "tn")
    dw_ba = matmul("proj_ba_dw", dba, h0, "tn")
    dw_uv = matmul("proj_uv_dw", duv, h0, "tn")
    d_win_t = jnp.concatenate([dw_qkv, dw_z, dw_ba[:2 * H], dw_uv], axis=0)
    G_win = d_win_t.reshape(N_DEV, in_cols // N_DEV, D)
    hD4, tok = reduce_begin("4", [G_win], dw_uv)
    reduce_end("3", hI3, tok, [('f_w1', 0), ('f_w2', 0), ('e_w_out', 0)])

    d_alog = d_alog_row[:, H:2 * H]
    d_dtb = d_dtb_row[:, H:2 * H]
    early_grads = [d_conv_w, d_alog, d_dtb, d_eon, d_eln_g, d_eln_b, d_ws, d_bs_t.T,
                   d_onorm, d_pw1_b, d_dw, d_dw_b, d_oln_g, d_oln_b, d_pw2_b,
                   jnp.concatenate(d_fnorm, axis=0), d_final]
    early_packed = pack(early_grads, 256)
    ss_s, rs_s, bufs_s, tok_s = split_start(
        "small_grads_start", [early_packed, lax.empty((N_DEV,) + early_packed.shape, F32)],
        plan_gather_direct, N_DEV - 1, d_conv_w)
    hI4, tok = reduce_mid("4", hD4, [tok_s] + [res[n][0] for n in ('f_w1', 'f_w2', 'o_pw1', 'e_w_out')])
    dh0 = matmul_sum("proj_dx", [(dqkv_raw, (win_t, 0, AQKV)), (dz_gate, (win_t, AQKV, AV)),
                                (dba, wt_ba), (duv, wt_uv)], after=tok)
    grad_x, _, d_enorm, _ = rms_bwd("rms_e_bwd", x2d, en_row, dh0, dx1, after=tok)
    late_all = all_gather("gather_e_norm_grad", [pack([d_enorm])])[0]
    (early_shard, early_land), _ = split_wait("small_grads_wait", ss_s, rs_s, bufs_s, plan_gather_direct, late_all)
    early_all = place_own("small_grads_own", early_land, early_shard, dev_sp)

    def sum8(*ps):
        s = ps[0]
        for p in ps[1:]:
            s = s + p
        return (s,)
    gs_sum = rowcall("small_sum", sum8, [(early_all, d) for d in range(N_DEV)], [], [(LANE, F32)])
    late_sum = rowcall("e_norm_sum", sum8, [(late_all, d) for d in range(N_DEV)], [], [(LANE, F32)])
    reduce_end("4", hI4, gs_sum, [('e_w_in', 0)])
    g_full = dict(zip(small_names[1:], unpack(gs_sum, [g.shape for g in early_grads])))
    g_full['e_norm'] = unpack(late_sum, [d_enorm.shape])[0]
    g_loc = {}
    for n in small_names:
        g = g_full[n]
        if n in small_sharded:
            per = g.shape[-1] // N_DEV
            g = lax.dynamic_slice_in_dim(g, dev * per, per, axis=-1)
        g_loc[n] = g.reshape(W[n].shape)
    packs = [packed_wmv[0], pack([g_loc[n] for n in small_names], 256), packed_wmv[1], packed_wmv[2]]
    d_s, m_s, v_s = rowcall("adam_small", adamw_tile, packs, [], [(LANE, F32)] * 3)
    shapes = [W[n].shape for n in small_names]
    for n, d_, m_, v_ in zip(small_names, unpack(d_s, shapes), unpack(m_s, shapes), unpack(v_s, shapes)):
        res[n] = (g_loc[n], d_, m_, v_)

    grads = [res[n][0] for n in names]
    deltas = [res[n][1] for n in names]
    new_m = [res[n][2] for n in names]
    new_v = [res[n][3] for n in names]
    return (loss, grad_x.reshape(x.shape), *grads, *deltas, *new_m, *new_v)
```

```python
import functools
import math

import jax
import jax.numpy as jnp
import numpy as np
from jax import lax
from jax.experimental import pallas as pl
from jax.experimental.pallas import tpu as pltpu

F32 = jnp.float32
BF = jnp.bfloat16
EPS = 1e-6
CHUNK = 64
B_BLOCK = 128
LANE = 128
SUBLANE = 8
N_DEV = 8
VMEM_LIMIT = 56 * 1024 * 1024

ADAM_LR = 0.001
ADAM_B1 = 0.9
ADAM_B2 = 0.999
ADAM_EPS = 1e-08
ADAM_WD = 0.01
ADAM_STEP = 10

MESH = pl.DeviceIdType.MESH
ANY = pl.BlockSpec(memory_space=pl.ANY)


def _tile(n, pref, mult=SUBLANE):
    if n <= pref:
        return n
    t = (pref // mult) * mult
    while t >= mult:
        if n % t == 0:
            return t
        t -= mult
    return n


THREE_PASS = 3


def _dg(a, b, ca, cb, hi):
    nb = a.ndim - 2
    batch = tuple(range(nb))
    dims = (((ca + nb,), (cb + nb,)), (batch, batch))
    if hi == THREE_PASS:
        a, b = a.astype(F32), b.astype(F32)
        ah, bh = a.astype(BF), b.astype(BF)
        al, bl = (a - ah.astype(F32)).astype(BF), (b - bh.astype(F32)).astype(BF)
        dot = lambda p, q: lax.dot_general(p, q, dims, preferred_element_type=F32)
        return dot(ah, bh) + (dot(ah, bl) + dot(al, bh))
    if hi:
        return lax.dot_general(a.astype(F32), b.astype(F32), dims,
                               precision=lax.Precision.HIGHEST, preferred_element_type=F32)
    return lax.dot_general(a.astype(BF), b.astype(BF), dims, preferred_element_type=F32)


@functools.partial(jax.custom_vjp, nondiff_argnums=(2, 3, 4))
def mm(a, b, ca, cb, hi=False):
    return _dg(a, b, ca, cb, hi)


def _mm_fwd(a, b, ca, cb, hi):
    return _dg(a, b, ca, cb, hi), (a, b)


def _mm_bwd(ca, cb, hi, res, g):
    a, b = res
    if ca == 1:
        da = mm(g, b, 1, 1 - cb, hi)
    else:
        da = mm(b, g, 1 - cb, 1, hi)
    if cb == 0:
        db = mm(a, g, 1 - ca, 0, hi)
    else:
        db = mm(g, a, 0, 1 - ca, hi)
    return da.astype(a.dtype), db.astype(b.dtype)


mm.defvjp(_mm_fwd, _mm_bwd)


def matmul(name, a, b, mode, epi=None, extras=(), out_dtypes=(F32,), colshard=False,
           tm=None, tn=1024, tk=2048, after=None, b_window=None):
    afters = [] if after is None else [after]
    slabs = b.ndim == 3
    if slabs:
        assert mode in ("nn", "nt") and not colshard, name
        b_rows, b_cols = b.shape[1], b.shape[0] * b.shape[2]
    else:
        b_rows, b_cols = b.shape
    if b_window is not None:
        assert mode == "nt" and not slabs, name
        b_start, b_rows = b_window
    if mode == "nn":
        (M, K), (K2, N) = a.shape, (b_rows, b_cols)
    elif mode == "nt":
        (M, K), (N, K2) = a.shape, (b_rows, b_cols)
    else:
        (K, M), (K2, N) = a.shape, b.shape
    assert K == K2, (name, a.shape, b.shape, mode)
    if tm is None:
        tm = 1024
    tm = _tile(M, tm)
    tn = N // N_DEV if colshard else _tile(N, tn, LANE)
    tk = _tile(K, tk, LANE)
    if slabs and mode == "nn":
        tn = b.shape[2]
    if slabs and mode == "nt":
        tk = b.shape[2]
    nk = K // tk
    grid = (M // tm, N // tn, nk)
    if mode == "nn":
        a_spec = pl.BlockSpec((tm, tk), lambda i, j, k: (i, k))
        b_spec = (pl.BlockSpec((None, tk, tn), lambda i, j, k: (j, k, 0)) if slabs
                  else pl.BlockSpec((tk, tn), lambda i, j, k: (k, j)))
        ca, cb = 1, 0
    elif mode == "nt":
        a_spec = pl.BlockSpec((tm, tk), lambda i, j, k: (i, k))
        off = 0
        if b_window is not None:
            assert b_start % tn == 0, (name, b_start, tn)
            off = b_start // tn
        b_spec = (pl.BlockSpec((None, tn, tk), lambda i, j, k: (k, j, 0)) if slabs
                  else pl.BlockSpec((tn, tk), lambda i, j, k: (j + off, k)))
        ca, cb = 1, 1
    else:
        a_spec = pl.BlockSpec((tk, tm), lambda i, j, k: (k, i))
        b_spec = pl.BlockSpec((tk, tn), lambda i, j, k: (k, j))
        ca, cb = 0, 0
    ex_specs = []
    for e in extras:
        if e.shape[0] == 1:
            ex_specs.append(pl.BlockSpec((1, tn), lambda i, j, k: (0, j)))
        else:
            assert e.shape == (M, N), (name, e.shape)
            ex_specs.append(pl.BlockSpec((tm, tn), lambda i, j, k: (i, j)))
    if colshard:
        out_shape = [jax.ShapeDtypeStruct((N_DEV, M, tn), dt) for dt in out_dtypes]
        out_specs = [pl.BlockSpec((None, tm, tn), lambda i, j, k: (j, i, 0)) for _ in out_dtypes]
    else:
        out_shape = [jax.ShapeDtypeStruct((M, N), dt) for dt in out_dtypes]
        out_specs = [pl.BlockSpec((tm, tn), lambda i, j, k: (i, j)) for _ in out_dtypes]
    n_ex, n_out = len(extras), len(out_dtypes)

    def body(*refs):
        a_ref, b_ref = refs[0], refs[1]
        ex_refs = refs[2:2 + n_ex]
        first_out = 2 + n_ex + len(afters)
        o_refs = refs[first_out:first_out + n_out]
        part = _dg(a_ref[...], b_ref[...], ca, cb, False)

        def finish(acc):
            res = (acc,) if epi is None else epi(acc, *[r[...] for r in ex_refs])
            for o_ref, r in zip(o_refs, res):
                o_ref[...] = r.astype(o_ref.dtype)

        if nk == 1:
            finish(part)
            return
        acc_ref = refs[-1]
        k = pl.program_id(2)

        @pl.when(k == 0)
        def _():
            acc_ref[...] = part

        @pl.when(k > 0)
        def _():
            acc_ref[...] += part

        @pl.when(k == nk - 1)
        def _():
            finish(acc_ref[...])

    outs = pl.pallas_call(
        body, name=name, grid=grid,
        in_specs=[a_spec, b_spec] + ex_specs + [ANY] * len(afters),
        out_specs=out_specs, out_shape=out_shape,
        scratch_shapes=[pltpu.VMEM((tm, tn), F32)] if nk > 1 else [],
        compiler_params=pltpu.CompilerParams(
            dimension_semantics=("parallel", "parallel", "arbitrary"),
            vmem_limit_bytes=VMEM_LIMIT),
    )(a, b, *extras, *afters)
    return outs[0] if n_out == 1 else tuple(outs)


def norm_matmul(name, x, g, b, epi=None, extras=(), out_dtypes=(F32,), tm=1024, tn=1024, after=None):
    M, K = x.shape
    slabs = b.ndim == 3
    N = b.shape[0] * b.shape[2] if slabs else b.shape[1]
    tm, tn = _tile(M, tm), (b.shape[2] if slabs else _tile(N, tn, LANE))
    b_spec = (pl.BlockSpec((None, K, tn), lambda i, j: (j, 0, 0)) if slabs
              else pl.BlockSpec((K, tn), lambda i, j: (0, j)))
    afters = [] if after is None else [after]
    ex_specs = [pl.BlockSpec((1, tn), lambda i, j: (0, j)) if e.shape[0] == 1
                else pl.BlockSpec((tm, tn), lambda i, j: (i, j)) for e in extras]
    n_ex, n_out = len(extras), len(out_dtypes)

    def body(*refs):
        x_ref, g_ref, b_ref = refs[:3]
        ex_refs = refs[3:3 + n_ex]
        first_out = 3 + n_ex + len(afters)
        o_refs = refs[first_out:first_out + n_out]
        h_ref, h_scr = refs[first_out + n_out], refs[-1]

        @pl.when(pl.program_id(1) == 0)
        def _():
            h = rms_tile(x_ref[...], g_ref[...]).astype(h_scr.dtype)
            h_scr[...] = h
            h_ref[...] = h

        acc = _dg(h_scr[...], b_ref[...], 1, 0, False)
        res = (acc,) if epi is None else epi(acc, *[r[...] for r in ex_refs])
        for o_ref, r in zip(o_refs, res):
            o_ref[...] = r.astype(o_ref.dtype)

    outs = pl.pallas_call(
        body, name=name, grid=(M // tm, N // tn),
        in_specs=[pl.BlockSpec((tm, K), lambda i, j: (i, 0)), pl.BlockSpec((1, K), lambda i, j: (0, 0)),
                  b_spec] + ex_specs + [ANY] * len(afters),
        out_specs=[pl.BlockSpec((tm, tn), lambda i, j: (i, j)) for _ in out_dtypes]
        + [pl.BlockSpec((tm, K), lambda i, j: (i, 0))],
        out_shape=[jax.ShapeDtypeStruct((M, N), dt) for dt in out_dtypes] + [jax.ShapeDtypeStruct((M, K), BF)],
        scratch_shapes=[pltpu.VMEM((tm, K), BF)],
        compiler_params=pltpu.CompilerParams(
            dimension_semantics=("parallel", "arbitrary"), vmem_limit_bytes=VMEM_LIMIT),
    )(x, g, b, *extras, *afters)
    return tuple(outs)


def matmul_sum(name, pairs, tm=512, tn=1024, after=None):
    pairs = [(a, b if isinstance(b, tuple) else (b, 0, b.shape[0])) for a, b in pairs]
    M, N = pairs[0][0].shape[0], pairs[0][1][0].shape[1]
    tm, tn = _tile(M, tm), _tile(N, tn, LANE)
    afters = [] if after is None else [after]
    in_specs, operands = [], []
    for a, (b, start, size) in pairs:
        assert a.shape == (M, size) and b.shape[1] == N and start % size == 0, (name, a.shape, b.shape, start)
        in_specs += [pl.BlockSpec((tm, size), lambda i, j: (i, 0)),
                     pl.BlockSpec((size, tn), lambda i, j, blk=start // size: (blk, j))]
        operands += [a, b]
    n = len(pairs)

    def body(*refs):
        acc = _dg(refs[0][...], refs[1][...], 1, 0, False)
        for p in range(1, n):
            acc = acc + _dg(refs[2 * p][...], refs[2 * p + 1][...], 1, 0, False)
        refs[-1][...] = acc

    return pl.pallas_call(
        body, name=name, grid=(M // tm, N // tn),
        in_specs=in_specs + [ANY] * len(afters),
        out_specs=pl.BlockSpec((tm, tn), lambda i, j: (i, j)),
        out_shape=jax.ShapeDtypeStruct((M, N), F32),
        compiler_params=pltpu.CompilerParams(
            dimension_semantics=("parallel", "parallel"), vmem_limit_bytes=VMEM_LIMIT),
    )(*operands, *afters)


def rowcall(name, fn, rows, consts, out_rows, out_accs=(), tr=256, sp=None, R=None, after=None,
            out_lead=None, into=None):
    afters = ([] if after is None else [after]) + ([] if into is None else list(into))
    rows = [r if isinstance(r, tuple) else (r, None) for r in rows]
    R = rows[0][0].shape[-2] if R is None else R
    tr = _tile(R, tr)
    n = R // tr
    in_specs = []
    for arr, lead in rows:
        C = arr.shape[-1]
        if lead is None:
            assert arr.shape[-2] == R, (name, arr.shape, R)
            in_specs.append(pl.BlockSpec((tr, C), lambda i, *s: (i, 0)))
        elif callable(lead):
            in_specs.append(pl.BlockSpec((tr, C), lambda i, *s, lead=lead: (lead(i, n, *s), 0)))
        else:
            assert arr.shape[-2] == R, (name, arr.shape, R)
            in_specs.append(pl.BlockSpec((None, tr, C), lambda i, *s, lead=lead: (lead, i, 0)))
    for c in consts:
        in_specs.append(pl.BlockSpec(c.shape, lambda i, *s, nd=c.ndim: (0,) * nd))
    if out_lead is None:
        out_shape = [jax.ShapeDtypeStruct((R, C), dt) for C, dt in out_rows]
        out_specs = [pl.BlockSpec((tr, C), lambda i, *s: (i, 0)) for C, _ in out_rows]
    else:
        n_slab, slab = out_lead
        out_shape = [jax.ShapeDtypeStruct((n_slab, R, C), dt) for C, dt in out_rows]
        out_specs = [pl.BlockSpec((None, tr, C), lambda i, *s: (slab, i, 0)) for C, _ in out_rows]
    for shp in out_accs:
        out_shape.append(jax.ShapeDtypeStruct(shp, F32))
        out_specs.append(pl.BlockSpec(shp, lambda i, *s, nd=len(shp): (0,) * nd))
    n_in, n_row, n_acc = len(rows) + len(consts), len(out_rows), len(out_accs)
    n_sp = 0 if sp is None else 1

    def body(*refs):
        refs = refs[n_sp:]
        ins = [r[...] for r in refs[:n_in]]
        res = fn(*ins)
        if not isinstance(res, (tuple, list)):
            res = (res,)
        o_refs = refs[n_in + len(afters):]
        for o_ref, r in zip(o_refs[:n_row], res[:n_row]):
            o_ref[...] = r.astype(o_ref.dtype)
        if n_acc:
            first = pl.program_id(0) == 0
            for o_ref, r in zip(o_refs[n_row:], res[n_row:]):
                r = r.astype(F32).reshape(o_ref.shape)

                @pl.when(first)
                def _(o_ref=o_ref, r=r):
                    o_ref[...] = r

                @pl.when(jnp.logical_not(first))
                def _(o_ref=o_ref, r=r):
                    o_ref[...] += r

    params = pltpu.CompilerParams(dimension_semantics=("arbitrary",), vmem_limit_bytes=VMEM_LIMIT)
    operands = [a for a, _ in rows] + list(consts) + afters
    in_specs = in_specs + [ANY] * len(afters)
    aliases = {} if into is None else {n_sp + len(operands) - len(into) + k: k for k in range(len(into))}
    if sp is None:
        outs = pl.pallas_call(body, name=name, grid=(n,), in_specs=in_specs, out_specs=out_specs,
                              out_shape=out_shape, input_output_aliases=aliases,
                              compiler_params=params)(*operands)
    else:
        outs = pl.pallas_call(
            body, name=name, out_shape=out_shape, compiler_params=params, input_output_aliases=aliases,
            grid_spec=pltpu.PrefetchScalarGridSpec(
                num_scalar_prefetch=1, grid=(n,), in_specs=in_specs, out_specs=out_specs),
        )(sp, *operands)
    return outs[0] if len(outs) == 1 else tuple(outs)


def slabcall(name, fn, ins, out_dtypes, sp, n_out_slabs=None, cw=512):
    R, C = ins[0][0].shape[-2:]
    cw = _tile(C, cw, LANE)
    in_specs = []
    for arr, slab in ins:
        assert arr.shape[-2:] == (R, C), (name, arr.shape)
        if slab is None:
            in_specs.append(pl.BlockSpec((R, cw), lambda i, j, s: (0, j)))
        elif callable(slab):
            in_specs.append(pl.BlockSpec((None, R, cw), lambda i, j, s, slab=slab: (slab(i, s), 0, j)))
        else:
            in_specs.append(pl.BlockSpec((None, R, cw), lambda i, j, s, slab=slab: (slab, 0, j)))
    if n_out_slabs is None:
        out_shape = [jax.ShapeDtypeStruct((R, C), dt) for dt in out_dtypes]
        out_specs = [pl.BlockSpec((R, cw), lambda i, j, s: (0, j)) for _ in out_dtypes]
    else:
        out_shape = [jax.ShapeDtypeStruct((n_out_slabs, R, C), dt) for dt in out_dtypes]
        out_specs = [pl.BlockSpec((None, R, cw), lambda i, j, s: (i, 0, j)) for _ in out_dtypes]
    n_in = len(ins)

    def body(sp_ref, *refs):
        res = fn(*[r[...] for r in refs[:n_in]])
        for o_ref, r in zip(refs[n_in:], res):
            o_ref[...] = r.astype(o_ref.dtype)

    outs = pl.pallas_call(
        body, name=name, out_shape=out_shape,
        grid_spec=pltpu.PrefetchScalarGridSpec(
            num_scalar_prefetch=1, grid=(n_out_slabs or 1, C // cw), in_specs=in_specs, out_specs=out_specs),
        compiler_params=pltpu.CompilerParams(
            dimension_semantics=("arbitrary", "arbitrary"), vmem_limit_bytes=VMEM_LIMIT),
    )(sp, *[a for a, _ in ins])
    return outs[0] if len(outs) == 1 else tuple(outs)


def rms_tile(x, g):
    x = x.astype(F32)
    return x * lax.rsqrt(jnp.mean(x * x, axis=-1, keepdims=True) + EPS) * g


def gelu(x):
    return 0.5 * x * (1.0 + lax.erf(x * (1.0 / math.sqrt(2.0))))


def ln_tile(x, g, b):
    mu = jnp.mean(x, axis=-1, keepdims=True)
    xc = x - mu
    return xc * lax.rsqrt(jnp.mean(xc * xc, axis=-1, keepdims=True) + EPS) * g + b


def lane_groups(fn, width, *arrs):
    n = arrs[0].shape[-1] // width
    outs = [fn(*[a[:, i * width:(i + 1) * width] for a in arrs]) for i in range(n)]
    return jnp.concatenate(outs, axis=-1)


def mixa_post_tile(o, z, o_norm):
    dv = o_norm.shape[-1]
    on = lane_groups(lambda t: rms_tile(t, o_norm), dv, o)
    return on * jax.nn.silu(z)


def mixb_tile(uv, ln_g, ln_b, w_s, bs_t):
    G = w_s.shape[0]
    gw = ln_g.shape[-1]
    dg = gw // G
    tr = uv.shape[0]
    u = gelu(uv[:, :gw])
    vg = gelu(uv[:, gw:])
    ii = lax.broadcasted_iota(jnp.int32, (B_BLOCK, B_BLOCK), 0)
    jj = lax.broadcasted_iota(jnp.int32, (B_BLOCK, B_BLOCK), 1)
    mask = (jj // CHUNK) <= (ii // CHUNK)
    cols = []
    for g in range(G):
        sl = slice(g * dg, (g + 1) * dg)
        vn = ln_tile(vg[:, sl], ln_g[:, sl], ln_b[:, sl])
        wm = jnp.where(mask, w_s[g], 0.0)
        blocks = []
        for m in range(tr // B_BLOCK):
            blk = vn[m * B_BLOCK:(m + 1) * B_BLOCK, :]
            blocks.append(mm(wm, blk, 1, 0) + bs_t[:, g:g + 1])
        mixed = blocks[0] if len(blocks) == 1 else jnp.concatenate(blocks, axis=0)
        cols.append(u[:, sl] * mixed)
    return jnp.concatenate(cols, axis=-1)


def bgc_tile(ba, alog_row, dtb_row):
    tr = ba.shape[0]
    beta = jax.nn.sigmoid(ba)
    g = -jnp.exp(alog_row) * jax.nn.softplus(ba + dtb_row)
    ii = lax.broadcasted_iota(jnp.int32, (tr, tr), 0)
    jj = lax.broadcasted_iota(jnp.int32, (tr, tr), 1)
    tri = jnp.where((ii // CHUNK == jj // CHUNK) & (jj <= ii), 1.0, 0.0).astype(F32)
    gc = mm(tri, g, 1, 0, True)
    return beta, gc


def make_bgc(H):
    def f(ba, alog_row, dtb_row):
        beta, gc = bgc_tile(ba, alog_row, dtb_row)
        lane = lax.broadcasted_iota(jnp.int32, ba.shape, 1)
        return jnp.where(lane < H, beta, jnp.where(lane < 2 * H, gc, 0.0))
    return f


def loss_tile(x, g, target):
    y = rms_tile(x, g)
    err = y - target
    return 0.5 * jnp.sum(jnp.mean(err * err, axis=-1, keepdims=True), axis=0, keepdims=True)


def adamw_tile(w, g, m, v):
    m = ADAM_B1 * m + (1.0 - ADAM_B1) * g
    v = ADAM_B2 * v + (1.0 - ADAM_B2) * (g * g)
    m_hat = m / (1.0 - ADAM_B1 ** ADAM_STEP)
    v_hat = v / (1.0 - ADAM_B2 ** ADAM_STEP)
    delta = -ADAM_LR * (m_hat / (jnp.sqrt(v_hat) + ADAM_EPS) + ADAM_WD * w)
    return delta, m, v


CONV_ROWS = 32


def _shifted_copies(src, sh, rows):
    for b in range(SUBLANE):
        sh[b] = src[pl.ds(b, rows), :]


def _window(sh, off, rows):
    b = off % SUBLANE
    return sh[b, pl.ds(off - b, rows), :]


def _conv_rows(out, sh, w_ref, offsets, rows):
    for r0 in range(0, rows, CONV_ROWS):
        rc = min(CONV_ROWS, rows - r0)
        acc = w_ref[0:1, :] * _window(sh, offsets[0] + r0, rc)
        for k in range(1, len(offsets)):
            acc = acc + w_ref[k:k + 1, :] * _window(sh, offsets[k] + r0, rc)
        out[r0:r0 + rc, :] = acc


def _conv_wgrad(dsrc, d0, sh, offsets, rows):
    dws = []
    for off in offsets:
        acc = None
        for r0 in range(0, rows, CONV_ROWS):
            rc = min(CONV_ROWS, rows - r0)
            prod = dsrc[d0 + r0:d0 + r0 + rc, :] * _window(sh, off + r0, rc)
            for g in range(0, rc, SUBLANE):
                part = prod[g:g + SUBLANE, :]
                acc = part if acc is None else acc + part
        dws.append(jnp.sum(acc, axis=0, keepdims=True))
    return jnp.concatenate(dws, axis=0)


def _conv_specs(T, tr, hb, cw, col_blocks, rev):
    n = T // tr

    def ri(i):
        return (n - 1 - i) if rev else i

    tile_specs, halo_specs = [], []
    for off in col_blocks:
        tile_specs.append(pl.BlockSpec((tr, cw), lambda j, i, off=off: (ri(i), j + off)))
        halo_specs.append(pl.BlockSpec(
            (hb, cw), lambda j, i, off=off: (jnp.maximum(ri(i) * (tr // hb) - 1, 0), j + off)))
    return n, ri, tile_specs, halo_specs


def conv_fwd(name, x, w, consts, pre, post, col_blocks, n_out, K, out_dtype=F32, tr=256, hb=8, cw=512):
    T = x.shape[0]
    C = w.shape[1]
    tr, cw = _tile(T, tr, hb), min(cw, C)
    nb = len(col_blocks)
    n, ri, tile_specs, halo_specs = _conv_specs(T, tr, hb, cw, col_blocks, False)
    w_spec = pl.BlockSpec((K, cw), lambda j, i: (0, j))
    c_specs = [pl.BlockSpec((1, cw), lambda j, i: (0, j)) for _ in consts]

    def body(*refs):
        tiles = [r[...] for r in refs[:nb]]
        halos = [r[...] for r in refs[nb:2 * nb]]
        w_ref = refs[2 * nb]
        cs = [r[...] for r in refs[2 * nb + 1:2 * nb + 1 + len(consts)]]
        o_refs = refs[2 * nb + 1 + len(consts):-3]
        pbuf, shp, cbuf = refs[-3:]
        i = pl.program_id(1)
        pbuf[0:hb, :] = jnp.where(i > 0, pre(*halos), 0.0)
        pbuf[hb:hb + tr, :] = pre(*tiles)
        pbuf[hb + tr:hb + tr + SUBLANE, :] = jnp.zeros((SUBLANE, cw), F32)
        _shifted_copies(pbuf, shp, hb + tr)
        _conv_rows(cbuf, shp, w_ref, [hb - (K - 1) + k for k in range(K)], tr)
        res = post(cbuf[...], pl.program_id(0), *cs)
        for o_ref, r in zip(o_refs, res):
            o_ref[...] = r.astype(o_ref.dtype)

    outs = pl.pallas_call(
        body, name=name, grid=(C // cw, n),
        in_specs=tile_specs + halo_specs + [w_spec] + c_specs,
        out_specs=[pl.BlockSpec((tr, cw), lambda j, i: (i, j)) for _ in range(n_out)],
        out_shape=[jax.ShapeDtypeStruct((T, C), out_dtype) for _ in range(n_out)],
        scratch_shapes=[pltpu.VMEM((hb + tr + SUBLANE, cw), F32), pltpu.VMEM((SUBLANE, hb + tr, cw), F32),
                        pltpu.VMEM((tr, cw), F32)],
        compiler_params=pltpu.CompilerParams(
            dimension_semantics=("parallel", "arbitrary"), vmem_limit_bytes=VMEM_LIMIT),
    )(*([x] * nb), *([x] * nb), w, *consts)
    return outs[0] if n_out == 1 else tuple(outs)


def conv_bwd(name, x, w, consts, grads, pre, post, col_blocks, K, tr=256, hb=8, cw=512, recompute=True):
    T, Cx = x.shape
    C = w.shape[1]
    tr, cw = _tile(T, tr, hb), min(cw, C)
    nb = len(col_blocks)
    n, ri, tile_specs, halo_specs = _conv_specs(T, tr, hb, cw, col_blocks, True)
    w_spec = pl.BlockSpec((K, cw), lambda j, i: (0, j))
    c_specs = [pl.BlockSpec((1, cw), lambda j, i: (0, j)) for _ in consts]
    g_specs = [pl.BlockSpec((tr, cw), lambda j, i: (ri(i), j)) for _ in grads]
    nc, ng = len(consts), len(grads)

    def body(*refs):
        p = 0
        tile_refs = refs[p:p + nb]; p += nb
        halo_refs = refs[p:p + nb]; p += nb
        w_ref = refs[p]; p += 1
        cs = [r[...] for r in refs[p:p + nc]]; p += nc
        gs = [r[...] for r in refs[p:p + ng]]; p += ng
        dx_refs = refs[p:p + nb]; p += nb
        sum_refs = refs[p:p + nb]; p += nb
        dw_ref = refs[p]; p += 1
        dc_refs = refs[p:p + nc]; p += nc
        pbuf, dbuf, ebuf, carry, shp, shd, cbuf = refs[p:p + 7]
        i = pl.program_id(1)
        first = i == 0
        tiles = [r[...] for r in tile_refs]
        halos = [r[...] for r in halo_refs]
        p_tile, vjp_pre = jax.vjp(pre, *tiles)
        pbuf[0:hb, :] = jnp.where(ri(i) > 0, pre(*halos), 0.0)
        pbuf[hb:hb + tr, :] = p_tile
        pbuf[hb + tr:hb + tr + SUBLANE, :] = jnp.zeros((SUBLANE, cw), F32)
        _shifted_copies(pbuf, shp, hb + tr)
        taps = [hb - (K - 1) + k for k in range(K)]
        if recompute:
            _conv_rows(cbuf, shp, w_ref, taps, tr)
            c = cbuf[...]
        else:
            c = jnp.zeros((tr, cw), F32)
        cid = pl.program_id(0)
        _, vjp_post = jax.vjp(lambda c_, *cs_: post(c_, cid, *cs_), c, *cs)
        dres = vjp_post(tuple(g.astype(F32) for g in gs))
        dbuf[0:hb, :] = jnp.zeros((hb, cw), F32)
        dbuf[hb:hb + tr, :] = dres[0]
        dbuf[hb + tr:hb + tr + hb + SUBLANE, :] = jnp.zeros((hb + SUBLANE, cw), F32)
        _shifted_copies(dbuf, shd, hb + tr + hb)
        _conv_rows(ebuf, shd, w_ref, [K - 1 - k for k in range(K)], hb + tr)
        dw = _conv_wgrad(dbuf, hb, shp, taps, tr)

        @pl.when(jnp.logical_not(first))
        def _():
            ebuf[tr:tr + hb, :] += carry[...]

        carry[...] = ebuf[0:hb, :]
        dtiles = vjp_pre(ebuf[hb:hb + tr, :])
        for r, s, d in zip(dx_refs, sum_refs, dtiles):
            r[...] = d.astype(r.dtype)
            ds_ = jnp.sum(d, axis=0, keepdims=True)

            @pl.when(first)
            def _(s=s, ds_=ds_):
                s[...] = ds_

            @pl.when(jnp.logical_not(first))
            def _(s=s, ds_=ds_):
                s[...] += ds_

        accs = [(dw_ref, dw)] + [(r, d) for r, d in zip(dc_refs, dres[1:])]
        for r, d in accs:
            @pl.when(first)
            def _(r=r, d=d):
                r[...] = d

            @pl.when(jnp.logical_not(first))
            def _(r=r, d=d):
                r[...] += d

    n_cb = C // cw
    outs = pl.pallas_call(
        body, name=name, grid=(n_cb, n),
        in_specs=tile_specs + halo_specs + [w_spec] + c_specs + g_specs,
        out_specs=([pl.BlockSpec((tr, cw), lambda j, i: (ri(i), j)) for _ in col_blocks]
                   + [pl.BlockSpec((1, cw), lambda j, i: (0, j)) for _ in col_blocks]
                   + [pl.BlockSpec((K, cw), lambda j, i: (0, j))]
                   + [pl.BlockSpec((1, cw), lambda j, i: (0, j)) for _ in consts]),
        out_shape=([jax.ShapeDtypeStruct((T, C), BF) for _ in col_blocks]
                   + [jax.ShapeDtypeStruct((1, C), F32) for _ in col_blocks]
                   + [jax.ShapeDtypeStruct((K, C), F32)]
                   + [jax.ShapeDtypeStruct((1, C), F32) for _ in consts]),
        scratch_shapes=[pltpu.VMEM((hb + tr + SUBLANE, cw), F32), pltpu.VMEM((hb + tr + hb + SUBLANE, cw), F32),
                        pltpu.VMEM((hb + tr, cw), F32), pltpu.VMEM((hb, cw), F32),
                        pltpu.VMEM((SUBLANE, hb + tr, cw), F32), pltpu.VMEM((SUBLANE, hb + tr + hb, cw), F32),
                        pltpu.VMEM((tr, cw), F32)],
        compiler_params=pltpu.CompilerParams(
            dimension_semantics=("parallel", "arbitrary"), vmem_limit_bytes=VMEM_LIMIT),
    )(*([x] * nb), *([x] * nb), w, *consts, *grads)
    dxs = outs[:nb]
    sums = outs[nb:2 * nb]
    dw = outs[2 * nb]
    dcs = outs[2 * nb + 1:]
    return dxs, sums, dw, dcs


def make_qkv_post(dk, cw, n_qk_chunks):
    def l2(t):
        return t * lax.rsqrt(jnp.sum(t * t, axis=-1, keepdims=True) + EPS)

    def post(c, cid):
        s = jax.nn.silu(c)
        normed = lane_groups(l2, dk, s)
        return (jnp.where(cid < n_qk_chunks, normed, s),)
    return post


def glu_pre(za, zb):
    return za * jax.nn.sigmoid(zb)


def bias_post(c, cid, b):
    return (c + b,)


def _col_to_row(col):
    C = col.shape[-2]
    ii = lax.broadcasted_iota(jnp.int32, (C, C), 0)
    jj = lax.broadcasted_iota(jnp.int32, (C, C), 1)
    wide = jnp.broadcast_to(col, col.shape[:-1] + (C,))
    return jnp.sum(jnp.where(ii == jj, wide, 0.0), axis=-2, keepdims=True)


@jax.custom_vjp
def solve_with_inverse(a, rhs, x):
    return mm(x, rhs, 1, 0, THREE_PASS)


def _swi_fwd(a, rhs, x):
    sol = mm(x, rhs, 1, 0, THREE_PASS)
    return sol, (x, sol)


def _swi_bwd(res, dsol):
    x, sol = res
    drhs = mm(x, dsol, 0, 0, THREE_PASS)
    da = -mm(drhs, sol, 1, 1, THREE_PASS)
    return da, drhs, jnp.zeros_like(x)


solve_with_inverse.defvjp(_swi_fwd, _swi_bwd)


def unit_lower_inverse(a):
    C = a.shape[-1]
    ii = lax.broadcasted_iota(jnp.int32, (C, C), 0)
    jj = lax.broadcasted_iota(jnp.int32, (C, C), 1)
    x = jnp.where(ii == jj, 1.0, 0.0).astype(F32) - a
    p = mm(a, a, 1, 0, THREE_PASS)
    steps = int(math.log2(C)) - 1
    for s in range(steps):
        x = x + mm(x, p, 1, 0, THREE_PASS)
        if s < steps - 1:
            p = mm(p, p, 1, 0, THREE_PASS)
    return x


def dn_masks(C):
    ii = lax.broadcasted_iota(jnp.int32, (C, C), 0)
    jj = lax.broadcasted_iota(jnp.int32, (C, C), 1)
    return ii >= jj, ii > jj


def dn_pre(q, k, v, beta, gc):
    C, dk = q.shape[-2:]
    tri, strict = dn_masks(C)
    q = q * (dk ** -0.5)
    diff = gc - _col_to_row(gc)
    decay = jnp.where(tri, jnp.exp(jnp.where(tri, diff, 0.0)), 0.0)
    kb = k * beta
    vb = v * beta
    a = jnp.where(strict, mm(kb, k, 1, 1) * decay, 0.0)
    eg = jnp.exp(gc)
    rhs = jnp.concatenate([vb, kb * eg], axis=-1)
    attn = mm(q, k, 1, 1) * decay
    qd = q * eg
    g_last = gc[..., C - 1:C, :]
    kt = k * jnp.exp(g_last - gc)
    gl = jnp.exp(g_last)
    return a, rhs, attn, qd, kt, gl


def dn_chunk(q, k, v, beta, gc, state, x):
    dv = v.shape[-1]
    a, rhs, attn, qd, kt, gl = dn_pre(q, k, v, beta, gc)
    sol = solve_with_inverse(a, rhs, x)
    u, w = sol[..., :dv], sol[..., dv:]
    vn = u - mm(w, state, 1, 0)
    o = mm(qd, state, 1, 0) + mm(attn, vn, 1, 0)
    new_state = state * gl + mm(kt, vn, 0, 0)
    return o, new_state


def _by_head(q_ref, k_ref, v_ref, bg, H, dk, dv):
    qv = jnp.stack([q_ref[:, h * dk:(h + 1) * dk] for h in range(H)])
    kv = jnp.stack([k_ref[:, h * dk:(h + 1) * dk] for h in range(H)])
    vv = jnp.stack([v_ref[:, h * dv:(h + 1) * dv] for h in range(H)])
    beta = jnp.stack([bg[:, h:h + 1] for h in range(H)])
    gc = jnp.stack([bg[:, H + h:H + h + 1] for h in range(H)])
    return qv, kv, vv, beta, gc


def deltanet_fwd(qkv, bgc, H):
    T = qkv.shape[0]
    dk = dv = qkv.shape[1] // (3 * H)
    N = T // CHUNK

    def body(q_ref, k_ref, v_ref, bgc_ref, o_ref, x_ref, s_ref, state):
        @pl.when(pl.program_id(0) == 0)
        def _():
            state[...] = jnp.zeros((H, dk, dv), F32)

        qv, kv, vv, beta, gc = _by_head(q_ref, k_ref, v_ref, bgc_ref[...], H, dk, dv)
        a = dn_pre(qv, kv, vv, beta, gc)[0]
        x = unit_lower_inverse(a)
        s = state[...]
        o, s_new = dn_chunk(qv, kv, vv, beta, gc, s, x)
        for h in range(H):
            o_ref[:, h * dv:(h + 1) * dv] = o[h]
        x_ref[...] = x
        s_ref[...] = s
        state[...] = s_new

    return pl.pallas_call(
        body, name="deltanet_fwd", grid=(N,),
        in_specs=[pl.BlockSpec((CHUNK, H * dk), lambda n: (n, 0)),
                  pl.BlockSpec((CHUNK, H * dk), lambda n: (n, 1)),
                  pl.BlockSpec((CHUNK, H * dv), lambda n: (n, 2)),
                  pl.BlockSpec((CHUNK, LANE), lambda n: (n, 0))],
        out_specs=[pl.BlockSpec((CHUNK, H * dv), lambda n: (n, 0)),
                   pl.BlockSpec((None, H, CHUNK, CHUNK), lambda n: (n, 0, 0, 0)),
                   pl.BlockSpec((None, H, dk, dv), lambda n: (n, 0, 0, 0))],
        out_shape=[jax.ShapeDtypeStruct((T, H * dv), F32),
                   jax.ShapeDtypeStruct((N, H, CHUNK, CHUNK), F32),
                   jax.ShapeDtypeStruct((N, H, dk, dv), F32)],
        scratch_shapes=[pltpu.VMEM((H, dk, dv), F32)],
        compiler_params=pltpu.CompilerParams(
            dimension_semantics=("arbitrary",), vmem_limit_bytes=VMEM_LIMIT),
    )(qkv, qkv, qkv, bgc)


def deltanet_bwd(qkv, bgc, xinv, states, do, H):
    T = qkv.shape[0]
    dk = dv = qkv.shape[1] // (3 * H)
    N = T // CHUNK

    def body(q_ref, k_ref, v_ref, bgc_ref, x_ref, s_ref, do_ref, dqkv_ref, dbgc_ref, dstate):
        @pl.when(pl.program_id(0) == 0)
        def _():
            dstate[...] = jnp.zeros((H, dk, dv), F32)

        qv, kv, vv, beta, gc = _by_head(q_ref, k_ref, v_ref, bgc_ref[...], H, dk, dv)
        do = jnp.stack([do_ref[:, h * dv:(h + 1) * dv] for h in range(H)])
        _, vjp = jax.vjp(dn_chunk, qv, kv, vv, beta, gc, s_ref[...], x_ref[...])
        dq, dk_, dv_, dbeta, dgc, ds, _ = vjp((do, dstate[...]))
        dstate[...] = ds
        lane = lax.broadcasted_iota(jnp.int32, (CHUNK, LANE), 1)
        dbgc = jnp.zeros((CHUNK, LANE), F32)
        for h in range(H):
            dqkv_ref[:, h * dk:(h + 1) * dk] = dq[h]
            dqkv_ref[:, (H + h) * dk:(H + h + 1) * dk] = dk_[h]
            dqkv_ref[:, (2 * H + h) * dk:(2 * H + h + 1) * dk] = dv_[h]
            dbgc = dbgc + jnp.where(lane == h, dbeta[h], 0.0) + jnp.where(lane == h + H, dgc[h], 0.0)
        dbgc_ref[...] = dbgc

    rn = lambda n: N - 1 - n
    return pl.pallas_call(
        body, name="deltanet_bwd", grid=(N,),
        in_specs=[pl.BlockSpec((CHUNK, H * dk), lambda n: (rn(n), 0)),
                  pl.BlockSpec((CHUNK, H * dk), lambda n: (rn(n), 1)),
                  pl.BlockSpec((CHUNK, H * dv), lambda n: (rn(n), 2)),
                  pl.BlockSpec((CHUNK, LANE), lambda n: (rn(n), 0)),
                  pl.BlockSpec((None, H, CHUNK, CHUNK), lambda n: (rn(n), 0, 0, 0)),
                  pl.BlockSpec((None, H, dk, dv), lambda n: (rn(n), 0, 0, 0)),
                  pl.BlockSpec((CHUNK, H * dv), lambda n: (rn(n), 0))],
        out_specs=[pl.BlockSpec((CHUNK, 3 * H * dk), lambda n: (rn(n), 0)),
                   pl.BlockSpec((CHUNK, LANE), lambda n: (rn(n), 0))],
        out_shape=[jax.ShapeDtypeStruct((T, 3 * H * dk), F32),
                   jax.ShapeDtypeStruct((T, LANE), F32)],
        scratch_shapes=[pltpu.VMEM((H, dk, dv), F32)],
        compiler_params=pltpu.CompilerParams(
            dimension_semantics=("arbitrary",), vmem_limit_bytes=VMEM_LIMIT),
    )(qkv, qkv, qkv, bgc, xinv, states, do)


def _place():
    x, y, c = lax.axis_index("x"), lax.axis_index("y"), lax.axis_index("c")
    chips = [(1 - x, y), (x, 1 - y), (1 - x, 1 - y)]
    return x, y, c, chips


def all_gather(name, shards, after=None):
    na = len(shards)
    afters = [] if after is None else [after]

    def body(*refs):
        ins, outs = refs[:na], refs[na + len(afters):2 * na + len(afters)]
        send_sems, recv_sems, local_sems = refs[2 * na + len(afters):]
        x, y, c, chips = _place()
        me, sibling = (x, y, c), (x, y, 1 - c)

        def copy(a, k, block, to, src=None):
            dst = outs[a].at[4 * block[0] + 2 * block[1] + block[2]]
            return pltpu.make_async_remote_copy(
                src_ref=dst if src is None else src, dst_ref=dst,
                send_sem=send_sems.at[a, k], recv_sem=recv_sems.at[a, k],
                device_id=to, device_id_type=MESH)

        mine, first, passed = [], [], []
        for a in range(na):
            cp = pltpu.make_async_copy(ins[a], outs[a].at[4 * x + 2 * y + c], local_sems.at[a])
            cp.start()
            mine.append(cp)
        for a in range(na):
            cps = [copy(a, 0, me, sibling, src=ins[a])]
            cps += [copy(a, 1 + j, me, (*chip, c), src=ins[a]) for j, chip in enumerate(chips)]
            for cp in cps:
                cp.start()
            first += cps
        for a in range(na):
            for j, chip in enumerate(chips):
                copy(a, 1 + j, (*chip, c), me).wait_recv()
                cp = copy(a, 4 + j, (*chip, c), sibling)
                cp.start()
                passed.append(cp)
        for a in range(na):
            copy(a, 0, sibling, me).wait_recv()
            for j, chip in enumerate(chips):
                copy(a, 4 + j, (*chip, 1 - c), me).wait_recv()
        for cp in first + passed:
            cp.wait_send()
        for cp in mine:
            cp.wait()

    outs = pl.pallas_call(
        body, name=name,
        in_specs=[ANY] * (na + len(afters)), out_specs=[ANY] * na,
        out_shape=[jax.ShapeDtypeStruct((N_DEV,) + s.shape, s.dtype) for s in shards],
        scratch_shapes=[pltpu.SemaphoreType.DMA((na, 7)), pltpu.SemaphoreType.DMA((na, 7)),
                        pltpu.SemaphoreType.DMA((na,))],
    )(*shards, *afters)
    return list(outs)


HBM_SPEC = pl.BlockSpec(memory_space=pltpu.HBM)
SEM_SPEC = pl.BlockSpec(memory_space=pltpu.SEMAPHORE)
EFFECT = pltpu.SideEffectType.DATAFLOW_SIDE_EFFECTING


def _descriptors(plan, bufs, send_sems, recv_sems):
    return [pltpu.make_async_remote_copy(src_ref=src, dst_ref=dst, send_sem=send_sems.at[k],
                                         recv_sem=recv_sems.at[k], device_id=dev, device_id_type=MESH)
            for k, (src, dst, dev) in enumerate(plan(bufs))]


def split_start(name, bufs, plan, n, after):
    nb = len(bufs)

    def body(*refs):
        for cp in _descriptors(plan, refs[:nb], refs[nb + 1], refs[nb + 2]):
            cp.start()
        refs[-1][...] = jnp.zeros((SUBLANE, LANE), F32)

    outs = pl.pallas_call(
        body, name=name,
        out_shape=(pltpu.SemaphoreType.DMA((n,)), pltpu.SemaphoreType.DMA((n,)),
                   *[pltpu.HBM(b.shape, b.dtype) for b in bufs],
                   jax.ShapeDtypeStruct((SUBLANE, LANE), F32)),
        in_specs=[HBM_SPEC] * nb + [ANY],
        out_specs=(SEM_SPEC, SEM_SPEC, *[HBM_SPEC] * nb, pl.BlockSpec(memory_space=pltpu.VMEM)),
        input_output_aliases={i: 2 + i for i in range(nb)},
        compiler_params=pltpu.CompilerParams(has_side_effects=EFFECT),
    )(*[pltpu.with_memory_space_constraint(b, pltpu.HBM) for b in bufs], after)
    return outs[0], outs[1], list(outs[2:2 + nb]), outs[-1]


def split_wait(name, send_sems, recv_sems, bufs, plan, after):
    nb = len(bufs)

    def body(*refs):
        cps = _descriptors(plan, refs[:nb], refs[nb], refs[nb + 1])
        for cp in cps:
            cp.wait_recv()
        for cp in cps:
            cp.wait_send()
        refs[-1][...] = jnp.zeros((SUBLANE, LANE), F32)

    afters = list(after) if isinstance(after, (list, tuple)) else [after]
    outs = pl.pallas_call(
        body, name=name,
        out_shape=[pltpu.HBM(b.shape, b.dtype) for b in bufs] + [jax.ShapeDtypeStruct((SUBLANE, LANE), F32)],
        in_specs=[HBM_SPEC] * nb + [SEM_SPEC, SEM_SPEC] + [ANY] * len(afters),
        out_specs=[HBM_SPEC] * nb + [pl.BlockSpec(memory_space=pltpu.VMEM)],
        input_output_aliases={i: i for i in range(nb)},
        compiler_params=pltpu.CompilerParams(has_side_effects=EFFECT),
    )(*bufs, send_sems, recv_sems, *afters)
    return list(outs[:nb]), outs[-1]


def _block(px, py, pc):
    return 4 * px + 2 * py + pc


def plan_gather_ici(na):
    def plan(bufs):
        x, y, c, chips = _place()
        out = []
        for a in range(na):
            dst = bufs[na + a].at[_block(x, y, c)]
            out.append((bufs[a], dst, (x, y, 1 - c)))
            out += [(bufs[a], dst, (px, py, c)) for px, py in chips]
        return out
    return plan


def plan_gather_pass(na):
    def plan(bufs):
        x, y, c, chips = _place()
        out = []
        for a in range(na):
            for px, py in chips:
                blk = bufs[a].at[_block(px, py, c)]
                out.append((blk, blk, (x, y, 1 - c)))
        return out
    return plan


def plan_gather_direct(bufs):
    x, y, c, _ = _place()
    dst = bufs[1].at[_block(x, y, c)]
    flip = lambda v, f: 1 - v if f else v
    return [(bufs[0], dst, (flip(x, m >> 2 & 1), flip(y, m >> 1 & 1), flip(c, m & 1))) for m in range(1, N_DEV)]


def plan_reduce_d2d(na):
    def plan(bufs):
        x, y, c, _ = _place()
        return [(bufs[a].at[2 * s + (1 - c)], bufs[na + a].at[s], (x, y, 1 - c))
                for a in range(na) for s in range(4)]
    return plan


def plan_reduce_ici(na):
    def plan(bufs):
        x, y, c, chips = _place()
        return [(bufs[a].at[2 * px + py], bufs[na + a].at[j], (px, py, c))
                for a in range(na) for j, (px, py) in enumerate(chips)]
    return plan


def place_own(name, land, shard, dev):
    r, c = shard.shape
    tr = _tile(r, 512)

    def body(sp_ref, s_ref, land_ref, o_ref):
        o_ref[...] = s_ref[...]

    return pl.pallas_call(
        body, name=name, out_shape=jax.ShapeDtypeStruct(land.shape, land.dtype),
        grid_spec=pltpu.PrefetchScalarGridSpec(
            num_scalar_prefetch=1, grid=(r // tr,),
            in_specs=[pl.BlockSpec((tr, c), lambda i, s: (i, 0)), ANY],
            out_specs=pl.BlockSpec((None, tr, c), lambda i, s: (s[0], i, 0))),
        input_output_aliases={2: 0},
        compiler_params=pltpu.CompilerParams(dimension_semantics=("arbitrary",)),
    )(dev, shard, land)


def pack(arrs, row_mult=SUBLANE):
    pieces = []
    for a in arrs:
        f = a.reshape(-1).astype(F32)
        pad = (-f.shape[0]) % LANE
        if pad:
            f = jnp.concatenate([f, jnp.zeros((pad,), F32)])
        pieces.append(f)
    flat = jnp.concatenate(pieces)
    rows = flat.shape[0] // LANE
    pad_rows = (-rows) % row_mult
    if pad_rows:
        flat = jnp.concatenate([flat, jnp.zeros((pad_rows * LANE,), F32)])
    return flat.reshape(-1, LANE)


def unpack(buf, shapes):
    flat = buf.reshape(-1)
    outs, off = [], 0
    for shp in shapes:
        n = int(np.prod(shp))
        outs.append(flat[off:off + n].reshape(shp))
        off += n + ((-n) % LANE)
    return outs


def _vjp_rows(fn, n_row_in, n_cot):
    def bwd(*args):
        rows = args[:n_row_in]
        cots = args[n_row_in:n_row_in + n_cot]
        consts = args[n_row_in + n_cot:]
        out, vjp = jax.vjp(fn, *rows, *consts)
        if isinstance(out, (tuple, list)):
            cot = tuple(c.astype(o.dtype) for c, o in zip(cots, out))
        else:
            cot = cots[0].astype(out.dtype)
        return vjp(cot)
    return bwd


def rms_bwd(name, x, g, dh, dres, after=None):
    D = x.shape[1]
    vj = _vjp_rows(rms_tile, 1, 1)

    def f(x_, dh_, dres_, g_):
        dx, dg = vj(x_, dh_, g_)
        dx = dx + dres_
        return dx, dx, dg, jnp.sum(dx, axis=0, keepdims=True)
    return rowcall(name, f, [x, dh, dres], [g], [(D, F32), (D, BF)], [(1, D), (1, D)], after=after, tr=512)


def kernel(x, e_norm, e_w_in, e_conv_w, e_a_log, e_dt_bias, e_o_norm, e_ln_g, e_ln_b, e_w_s, e_b_s, e_w_out, o_norm, o_pw1, o_pw1_b, o_dw, o_dw_b, o_ln_g, o_ln_b, o_pw2, o_pw2_b, f_norm, f_w1, f_w2, final_norm, loss_target, m_e_norm, m_e_w_in, m_e_conv_w, m_e_a_log, m_e_dt_bias, m_e_o_norm, m_e_ln_g, m_e_ln_b, m_e_w_s, m_e_b_s, m_e_w_out, m_o_norm, m_o_pw1, m_o_pw1_b, m_o_dw, m_o_dw_b, m_o_ln_g, m_o_ln_b, m_o_pw2, m_o_pw2_b, m_f_norm, m_f_w1, m_f_w2, m_final_norm, v_e_norm, v_e_w_in, v_e_conv_w, v_e_a_log, v_e_dt_bias, v_e_o_norm, v_e_ln_g, v_e_ln_b, v_e_w_s, v_e_b_s, v_e_w_out, v_o_norm, v_o_pw1, v_o_pw1_b, v_o_dw, v_o_dw_b, v_o_ln_g, v_o_ln_b, v_o_pw2, v_o_pw2_b, v_f_norm, v_f_w1, v_f_w2, v_final_norm):
    names = ['e_norm', 'e_w_in', 'e_conv_w', 'e_a_log', 'e_dt_bias', 'e_o_norm', 'e_ln_g', 'e_ln_b', 'e_w_s', 'e_b_s', 'e_w_out', 'o_norm', 'o_pw1', 'o_pw1_b', 'o_dw', 'o_dw_b', 'o_ln_g', 'o_ln_b', 'o_pw2', 'o_pw2_b', 'f_norm', 'f_w1', 'f_w2', 'final_norm']
    W = dict(zip(names, [e_norm, e_w_in, e_conv_w, e_a_log, e_dt_bias, e_o_norm, e_ln_g, e_ln_b, e_w_s, e_b_s, e_w_out, o_norm, o_pw1, o_pw1_b, o_dw, o_dw_b, o_ln_g, o_ln_b, o_pw2, o_pw2_b, f_norm, f_w1, f_w2, final_norm]))
    Mo = dict(zip(names, [m_e_norm, m_e_w_in, m_e_conv_w, m_e_a_log, m_e_dt_bias, m_e_o_norm, m_e_ln_g, m_e_ln_b, m_e_w_s, m_e_b_s, m_e_w_out, m_o_norm, m_o_pw1, m_o_pw1_b, m_o_dw, m_o_dw_b, m_o_ln_g, m_o_ln_b, m_o_pw2, m_o_pw2_b, m_f_norm, m_f_w1, m_f_w2, m_final_norm]))
    Vo = dict(zip(names, [v_e_norm, v_e_w_in, v_e_conv_w, v_e_a_log, v_e_dt_bias, v_e_o_norm, v_e_ln_g, v_e_ln_b, v_e_w_s, v_e_b_s, v_e_w_out, v_o_norm, v_o_pw1, v_o_pw1_b, v_o_dw, v_o_dw_b, v_o_ln_g, v_o_ln_b, v_o_pw2, v_o_pw2_b, v_f_norm, v_f_w1, v_f_w2, v_final_norm]))

    T, D = x.shape[1], x.shape[2]
    H = e_a_log.shape[-1]
    dv = e_o_norm.shape[-1]
    dk = dv
    G = e_w_s.shape[1]
    AQK, AV, BW = H * dk, H * dv, e_ln_g.shape[-1]
    AQKV = 2 * AQK + AV
    in_cols = AQKV + AV + 2 * H + 2 * BW
    KA = e_conv_w.shape[1]
    KC = o_dw.shape[1]
    L = f_norm.shape[0]
    dev = 4 * lax.axis_index("x") + 2 * lax.axis_index("y") + lax.axis_index("c")
    x2d = x.reshape(T, D)
    tgt = loss_target.reshape(T, D)

    def tie(small, tok):
        return small + tok[0:1, 0:1]

    dev_sp = dev.astype(jnp.int32).reshape(1)
    where = jnp.stack([lax.axis_index("c"), 2 * lax.axis_index("x") + lax.axis_index("y")]).astype(jnp.int32)
    row = lambda a: a.reshape(1, -1).astype(F32)
    en_row = row(e_norm)

    def gather_begin(tag, shards, after):
        na = len(shards)
        lands = [lax.empty((N_DEV,) + s.shape, s.dtype) for s in shards]
        ss, rs, bufs, tok = split_start(f"gather{tag}_ici_start", shards + lands, plan_gather_ici(na), 4 * na, after)
        return (na, ss, rs, bufs), tok

    def gather_pass(tag, h, after):
        na, ss, rs, bufs = h
        bufs, tok = split_wait(f"gather{tag}_ici_wait", ss, rs, bufs, plan_gather_ici(na), after)
        ss, rs, lands, tok = split_start(f"gather{tag}_pass_start", bufs[na:], plan_gather_pass(na), 3 * na, tok)
        return (na, ss, rs, bufs[:na], lands), tok

    def gather_end(tag, h, after):
        na, ss, rs, shards, lands = h
        lands, _ = split_wait(f"gather{tag}_pass_wait", ss, rs, lands, plan_gather_pass(na), after)
        return [place_own(f"gather{tag}_own{a}", lands[a], shards[a], dev_sp) for a in range(na)]

    bfw = lambda w: w.astype(BF)
    hA0, tok = gather_begin("0", [bfw(jnp.swapaxes(e_w_in[0], 0, 1))], en_row)
    h0 = rowcall("rms_e", rms_tile, [x2d], [en_row], [(D, BF)], after=tok, tr=512)
    small_sharded = ['e_conv_w', 'o_norm', 'o_pw1_b', 'o_dw', 'o_dw_b', 'o_ln_g', 'o_ln_b', 'o_pw2_b']
    sm = all_gather("gather_small", [pack([W[n][0]]) for n in small_sharded], after=h0)
    hA1, tok = gather_begin("1", [bfw(e_w_out[0]), bfw(f_w1[0]), bfw(f_w2[0])], sm[0])
    hA2, tok = gather_begin("2", [bfw(o_pw1[0]), bfw(o_pw2[0])], tok)
    hA3, tok = gather_begin("3", [bfw(f_w1[1]), bfw(f_w2[1])], tok)

    full = {}
    for n, g in zip(small_sharded, sm):
        shp = W[n][0].shape
        blocks = [unpack(g[d], [shp])[0] for d in range(N_DEV)]
        full[n] = jnp.concatenate(blocks, axis=-1)
    conv_w = full['e_conv_w']
    on_row, pw1_b_row = row(full['o_norm']), row(full['o_pw1_b'])
    dw_w, dw_b_row = full['o_dw'], row(full['o_dw_b'])
    oln_g_row, oln_b_row, pw2_b_row = row(full['o_ln_g']), row(full['o_ln_b']), row(full['o_pw2_b'])
    small_names = ['e_norm', 'e_conv_w', 'e_a_log', 'e_dt_bias', 'e_o_norm', 'e_ln_g', 'e_ln_b', 'e_w_s', 'e_b_s',
                   'o_norm', 'o_pw1_b', 'o_dw', 'o_dw_b', 'o_ln_g', 'o_ln_b', 'o_pw2_b', 'f_norm', 'final_norm']
    packed_wmv = [pack([A[n] for n in small_names], 256) for A in (W, Mo, Vo)]

    hB0, tok = gather_pass("0", hA0, [tok, conv_w, dw_w, pw1_b_row] + packed_wmv)
    (g_win,) = gather_end("0", hB0, tok)
    win_t = g_win.reshape(in_cols, D)
    wt_ba = jnp.pad(win_t[AQKV + AV:AQKV + AV + 2 * H], ((0, LANE - 2 * H), (0, 0)))
    wt_uv = win_t[AQKV + AV + 2 * H:]

    alog_row = jnp.pad(row(e_a_log), ((0, 0), (H, LANE - 2 * H)))
    dtb_row = jnp.pad(row(e_dt_bias), ((0, 0), (H, LANE - 2 * H)))
    eon_row = row(e_o_norm)
    eln_g_row, eln_b_row = row(e_ln_g), row(e_ln_b)
    w_s = e_w_s[0]
    bs_t = e_b_s[0].T
    fn_rows = [row(f_norm[l]) for l in range(L)]
    fin_row = row(final_norm)

    qkv_raw = matmul("proj_qkv", h0, win_t, "nt", b_window=(0, AQKV))
    z_gate = matmul("proj_z", h0, win_t, "nt", b_window=(AQKV, AV))
    ba = matmul("proj_ba", h0, wt_ba, "nt")
    uv = matmul("proj_uv", h0, wt_uv, "nt")

    cwa = min(512, AQKV)
    qkv_post = make_qkv_post(dk, cwa, 2 * AQK // cwa)
    ident = lambda t: t
    qkv = conv_fwd("qkv_conv", qkv_raw, conv_w, [], ident, qkv_post, [0], 1, KA, cw=cwa, tr=512)
    bgc_fn = make_bgc(H)
    bgc = rowcall("bgc", bgc_fn, [ba], [alog_row, dtb_row], [(LANE, F32)])
    o_dn, xinv, states = deltanet_fwd(qkv, bgc, H)
    hB1, tok = gather_pass("1", hA1, o_dn)

    def mix_tile(o, z, uv_, o_norm_, ln_g, ln_b, w_s_, bs_t_):
        return jnp.concatenate([mixa_post_tile(o, z, o_norm_), mixb_tile(uv_, ln_g, ln_b, w_s_, bs_t_)], axis=-1)
    mix_consts = [eon_row, eln_g_row, eln_b_row, w_s, bs_t]
    mix = rowcall("mix", mix_tile, [o_dn, z_gate, uv], mix_consts, [(AV + BW, BF)], after=tok)
    g_wout, g_w1_0, g_w2_0 = gather_end("1", hB1, mix)
    wout = g_wout.reshape(-1, D)
    w1 = [jnp.moveaxis(g_w1_0, 0, 1).reshape(D, -1), None]
    w2 = [g_w2_0.reshape(-1, D), None]
    add_epi = lambda acc, r: (acc + r,)
    x1 = matmul("out_proj", mix, wout, "nn", epi=add_epi, extras=[x2d], tm=512, tn=2048)

    def relu2_epi(acc):
        r = jnp.maximum(acc, 0.0)
        return r * r, r

    hB2, tok = gather_pass("2", hA2, x1)
    a2_0, ar_0, hf0 = norm_matmul("ffn_up0", x1, fn_rows[0], g_w1_0, epi=relu2_epi, out_dtypes=(BF, BF), after=tok)
    g_pw1, g_pw2 = gather_end("2", hB2, a2_0)
    x2 = matmul("ffn_down0", a2_0, w2[0], "nn", epi=add_epi, extras=[x1])
    ffn0 = (hf0, a2_0, ar_0)
    pw1 = jnp.moveaxis(g_pw1, 0, 1).reshape(D, 2 * D)
    pw2 = g_pw2.reshape(D, D)

    bias_epi = lambda acc, b: (acc + b,)
    zc, h1 = norm_matmul("pw1", x2, on_row, g_pw1, epi=bias_epi, extras=[pw1_b_row])
    hB3, tok = gather_pass("3", hA3, zc)
    cwc = min(512, D)
    ncb = D // cwc
    cconv = conv_fwd("dw_conv", zc, dw_w, [tie(dw_b_row, tok)], glu_pre, bias_post, [0, ncb], 1, KC, hb=32, cw=cwc,
                     tr=512)
    g_w1_1, g_w2_1 = gather_end("3", hB3, cconv)
    w1[1] = jnp.moveaxis(g_w1_1, 0, 1).reshape(D, -1)
    w2[1] = g_w2_1.reshape(-1, D)
    ln_silu = lambda c, g, b: jax.nn.silu(ln_tile(c, g, b))
    s_act = rowcall("ln_silu", ln_silu, [cconv], [oln_g_row, oln_b_row], [(D, BF)], tr=512)
    x3 = matmul("pw2", s_act, pw2, "nn", epi=lambda acc, r, b: (r + (acc + b),), extras=[x2, pw2_b_row],
                tm=512, tn=2048)
    a2_1, ar_1, hf1 = norm_matmul("ffn_up1", x3, fn_rows[1], g_w1_1, epi=relu2_epi, out_dtypes=(BF, BF))
    x4 = matmul("ffn_down1", a2_1, w2[1], "nn", epi=add_epi, extras=[x3])
    ffn1 = (hf1, a2_1, ar_1)

    def loss_bwd_tile(x_, t_, g_):
        l, vjp = jax.vjp(lambda a, b: loss_tile(a, b, t_), x_, g_)
        dx, dg = vjp(jnp.ones_like(l))
        return dx, dx, l, dg
    dx4, dx4_b, loss_part, d_final = rowcall("loss_head", loss_bwd_tile, [x4, tgt], [fin_row],
                                             [(D, F32), (D, BF)], [(1, 1), (1, D)], tr=512)
    loss = lax.psum(loss_part[0, 0], ("x", "y", "c"))

    def reduce_begin(tag, grads, after):
        na = len(grads)
        lands = [lax.empty((4,) + g.shape[1:], g.dtype) for g in grads]
        ss, rs, bufs, tok = split_start(f"reduce{tag}_d2d_start", grads + lands, plan_reduce_d2d(na), 4 * na, after)
        return (na, ss, rs, bufs), tok

    def reduce_mid(tag, h, after):
        na, ss, rs, bufs = h
        bufs, tok = split_wait(f"reduce{tag}_d2d_wait", ss, rs, bufs, plan_reduce_d2d(na), after)
        parts = []
        for a, (g, rc) in enumerate(zip(bufs[:na], bufs[na:])):
            r, c = g.shape[1], g.shape[2]
            if r % SUBLANE:
                parts.append(slabcall(f"chip_sum{tag}_{a}", lambda p, q: (p + q,),
                                      [(g, lambda i, s: 2 * i + s[0]), (rc, lambda i, s: i)], [BF], where, 4))
                continue
            mine = lambda i, n, s: (2 * (i // (n // 4)) + s[0]) * (n // 4) + i % (n // 4)
            parts.append(rowcall(f"chip_sum{tag}_{a}", lambda p, q: (p + q,),
                                 [(g.reshape(N_DEV * r, c), mine), rc.reshape(4 * r, c)], [],
                                 [(c, BF)], tr=_tile(r, 512), sp=where, R=4 * r).reshape(4, r, c))
        lands = [lax.empty((3,) + p.shape[1:], p.dtype) for p in parts]
        ss, rs, bufs, tok = split_start(f"reduce{tag}_ici_start", parts + lands, plan_reduce_ici(na), 3 * na, tok)
        return (na, ss, rs, bufs), tok

    res = {}

    def reduce_end(tag, h, after, targets):
        na, ss, rs, bufs = h
        bufs, _ = split_wait(f"reduce{tag}_ici_wait", ss, rs, bufs, plan_reduce_ici(na), after)
        for a, (part, fin, (n, l)) in enumerate(zip(bufs[:na], bufs[na:], targets)):
            def f(p0, p1, p2, p3, w_, m_, v_):
                g = ((p0.astype(F32) + p1.astype(F32)) + p2.astype(F32)) + p3.astype(F32)
                return (g,) + adamw_tile(w_, g, m_, v_)
            r, C = fin.shape[-2], fin.shape[-1]
            if W[n].shape[1:] == (C, r):
                t = lambda arr: jnp.swapaxes(arr[l], 0, 1)
                outs = slabcall(f"adam{tag}_{a}", f, [(part, lambda i, s: s[1]), (fin, 0), (fin, 1), (fin, 2),
                                                      (t(W[n]), None), (t(Mo[n]), None), (t(Vo[n]), None)],
                                [F32] * 4, where)
                res[n] = tuple(jnp.swapaxes(o, 0, 1)[None] for o in outs)
                continue
            own = lambda i, n_, s: s[1] * n_ + i
            res[n] = rowcall(f"adam{tag}_{a}", f, [(part.reshape(4 * r, C), own), (fin, 0), (fin, 1), (fin, 2),
                                                   (W[n], l), (Mo[n], l), (Vo[n], l)], [],
                             [(C, F32)] * 4, tr=256, sp=where, R=r,
                             out_lead=(W[n].shape[0], l), into=res.get(n))

    dscale_epi = lambda acc, r: (acc * (2.0 * r.astype(F32)),)
    d_fnorm = [None] * L

    dpre1 = matmul("ffn_down_dx1", dx4_b, w2[1], "nt", epi=dscale_epi, extras=[ar_1], out_dtypes=(BF,))
    dw2_1 = matmul("ffn_down_dw1", a2_1, dx4_b, "tn")
    dw1_1 = matmul("ffn_up_dw1", hf1, dpre1, "tn", colshard=True)
    hD1, tok = reduce_begin("1", [dw1_1, dw2_1.reshape(N_DEV, -1, D)], dpre1)
    dhf1 = matmul("ffn_up_dx1", dpre1, w1[1], "nt", after=tok, tk=4096)
    dx3, dx3_b, d_fnorm[1], d_pw2_b = rms_bwd("rms_f_bwd1", x3, fn_rows[1], dhf1, dx4)

    ds_act = matmul("pw2_dx", dx3_b, pw2, "nt")
    hI1, tok = reduce_mid("1", hD1, ds_act)
    d_pw2 = matmul("pw2_dw", s_act, dx3_b, "tn", after=tok)
    dcconv, d_oln_g, d_oln_b = rowcall(
        "ln_silu_bwd", _vjp_rows(ln_silu, 1, 1), [cconv, ds_act], [oln_g_row, oln_b_row],
        [(D, F32)], [(1, D), (1, D)], tr=512)
    (dza, dzb), (sza, szb), d_dw, (d_dw_b,) = conv_bwd(
        "dw_conv_bwd", zc, dw_w, [tie(dw_b_row, tok)], [dcconv], glu_pre, bias_post, [0, ncb], KC, hb=32, cw=cwc, tr=256,
        recompute=False)
    dzc = jnp.concatenate([dza, dzb], axis=-1)
    d_pw1_b = jnp.concatenate([sza, szb], axis=-1)
    d_pw1 = matmul("pw1_dw", h1, dzc, "tn", colshard=True)
    dh1 = matmul("pw1_dx", dzc, pw1, "nt")
    dx2, dx2_b, d_onorm, _ = rms_bwd("rms_o_bwd", x2, on_row, dh1, dx3)
    reduce_end("1", hI1, dx2, [('f_w1', 1), ('f_w2', 1)])

    hD2, tok = reduce_begin("2", [d_pw1, d_pw2.reshape(N_DEV, -1, D)], dx2)
    dpre0 = matmul("ffn_down_dx0", dx2_b, w2[0], "nt", epi=dscale_epi, extras=[ar_0], out_dtypes=(BF,), after=tok)
    dw2_0 = matmul("ffn_down_dw0", a2_0, dx2_b, "tn")
    hI2, tok = reduce_mid("2", hD2, dw2_0)
    dw1_0 = matmul("ffn_up_dw0", hf0, dpre0, "tn", colshard=True, after=tok)
    dhf0 = matmul("ffn_up_dx0", dpre0, w1[0], "nt", tk=4096)
    dx1, dx1_b, d_fnorm[0], _ = rms_bwd("rms_f_bwd0", x1, fn_rows[0], dhf0, dx2)
    reduce_end("2", hI2, dx1, [('o_pw1', 0), ('o_pw2', 0)])

    dmix = matmul("out_proj_dx", dx1_b, wout, "nt")
    d_wout = matmul("out_proj_dw", mix, dx1_b, "tn")
    hD3, tok = reduce_begin("3", [dw1_0, dw2_0.reshape(N_DEV, -1, D), d_wout.reshape(N_DEV, -1, D)], dmix)
    do_dn, dz_gate, duv, d_eon, d_eln_g, d_eln_b, d_ws, d_bs_t = rowcall(
        "mix_bwd", _vjp_rows(mix_tile, 3, 1), [o_dn, z_gate, uv, dmix], mix_consts,
        [(AV, F32), (AV, BF), (2 * BW, BF)], [(1, dv), (1, BW), (1, BW), w_s.shape, bs_t.shape], after=tok)
    dqkv, dbgc = deltanet_bwd(qkv, bgc, xinv, states, do_dn, H)
    hI3, tok = reduce_mid("3", hD3, dbgc)
    bgc_bwd = _vjp_rows(bgc_fn, 1, 1)
    dba, d_alog_row, d_dtb_row = rowcall(
        "bgc_bwd", bgc_bwd, [ba, dbgc], [alog_row, dtb_row], [(LANE, BF)], [(1, LANE), (1, LANE)])
    (dqkv_raw,), _, d_conv_w, _ = conv_bwd(
        "qkv_conv_bwd", qkv_raw, tie(conv_w, tok), [], [dqkv], ident, qkv_post, [0], KA, cw=cwa, tr=512)

    dw_qkv = matmul("proj_qkv_dw", dqkv_raw, h0, "tn")
    dw_z = matmul("proj_z_dw", dz_gate, h0, "tn")
    dw_ba = matmul("proj_ba_dw", dba, h0, "tn")
    dw_uv = matmul("proj_uv_dw", duv, h0, "tn")
    d_win_t = jnp.concatenate([dw_qkv, dw_z, dw_ba[:2 * H], dw_uv], axis=0)
    G_win = d_win_t.reshape(N_DEV, in_cols // N_DEV, D)
    hD4, tok = reduce_begin("4", [G_win], dw_uv)
    reduce_end("3", hI3, tok, [('f_w1', 0), ('f_w2', 0), ('e_w_out', 0)])

    d_alog = d_alog_row[:, H:2 * H]
    d_dtb = d_dtb_row[:, H:2 * H]
    early_grads = [d_conv_w, d_alog, d_dtb, d_eon, d_eln_g, d_eln_b, d_ws, d_bs_t.T,
                   d_onorm, d_pw1_b, d_dw, d_dw_b, d_oln_g, d_oln_b, d_pw2_b,
                   jnp.concatenate(d_fnorm, axis=0), d_final]
    early_packed = pack(early_grads, 256)
    ss_s, rs_s, bufs_s, tok_s = split_start(
        "small_grads_start", [early_packed, lax.empty((N_DEV,) + early_packed.shape, F32)],
        plan_gather_direct, N_DEV - 1, d_conv_w)
    hI4, tok = reduce_mid("4", hD4, [tok_s] + [res[n][0] for n in ('f_w1', 'f_w2', 'o_pw1', 'e_w_out')])
    dh0 = matmul_sum("proj_dx", [(dqkv_raw, (win_t, 0, AQKV)), (dz_gate, (win_t, AQKV, AV)),
                                (dba, wt_ba), (duv, wt_uv)], after=tok)
    grad_x, _, d_enorm, _ = rms_bwd("rms_e_bwd", x2d, en_row, dh0, dx1, after=tok)
    late_all = all_gather("gather_e_norm_grad", [pack([d_enorm])])[0]
    (early_shard, early_land), _ = split_wait("small_grads_wait", ss_s, rs_s, bufs_s, plan_gather_direct, late_all)
    early_all = place_own("small_grads_own", early_land, early_shard, dev_sp)

    def sum8(*ps):
        s = ps[0]
        for p in ps[1:]:
            s = s + p
        return (s,)
    gs_sum = rowcall("small_sum", sum8, [(early_all, d) for d in range(N_DEV)], [], [(LANE, F32)])
    late_sum = rowcall("e_norm_sum", sum8, [(late_all, d) for d in range(N_DEV)], [], [(LANE, F32)])
    reduce_end("4", hI4, gs_sum, [('e_w_in', 0)])
    g_full = dict(zip(small_names[1:], unpack(gs_sum, [g.shape for g in early_grads])))
    g_full['e_norm'] = unpack(late_sum, [d_enorm.shape])[0]
    g_loc = {}
    for n in small_names:
        g = g_full[n]
        if n in small_sharded:
            per = g.shape[-1] // N_DEV
            g = lax.dynamic_slice_in_dim(g, dev * per, per, axis=-1)
        g_loc[n] = g.reshape(W[n].shape)
    packs = [packed_wmv[0], pack([g_loc[n] for n in small_names], 256), packed_wmv[1], packed_wmv[2]]
    d_s, m_s, v_s = rowcall("adam_small", adamw_tile, packs, [], [(LANE, F32)] * 3)
    shapes = [W[n].shape for n in small_names]
    for n, d_, m_, v_ in zip(small_names, unpack(d_s, shapes), unpack(m_s, shapes), unpack(v_s, shapes)):
        res[n] = (g_loc[n], d_, m_, v_)

    grads = [res[n][0] for n in names]
    deltas = [res[n][1] for n in names]
    new_m = [res[n][2] for n in names]
    new_v = [res[n][3] for n in names]
    return (loss, grad_x.reshape(x.shape), *grads, *deltas, *new_m, *new_v)
```

```python
import functools
import math

import jax
import jax.numpy as jnp
import numpy as np
from jax import lax
from jax.experimental import pallas as pl
from jax.experimental.pallas import tpu as pltpu

F32 = jnp.float32
BF = jnp.bfloat16
EPS = 1e-6
CHUNK = 64
B_BLOCK = 128
LANE = 128
SUBLANE = 8
N_DEV = 8
VMEM_LIMIT = 56 * 1024 * 1024

ADAM_LR = 0.001
ADAM_B1 = 0.9
ADAM_B2 = 0.999
ADAM_EPS = 1e-08
ADAM_WD = 0.01
ADAM_STEP = 10

MESH = pl.DeviceIdType.MESH
ANY = pl.BlockSpec(memory_space=pl.ANY)


def _tile(n, pref, mult=SUBLANE):
    if n <= pref:
        return n
    t = (pref // mult) * mult
    while t >= mult:
        if n % t == 0:
            return t
        t -= mult
    return n


THREE_PASS = 3


def _dg(a, b, ca, cb, hi):
    nb = a.ndim - 2
    batch = tuple(range(nb))
    dims = (((ca + nb,), (cb + nb,)), (batch, batch))
    if hi == THREE_PASS:
        a, b = a.astype(F32), b.astype(F32)
        ah, bh = a.astype(BF), b.astype(BF)
        al, bl = (a - ah.astype(F32)).astype(BF), (b - bh.astype(F32)).astype(BF)
        dot = lambda p, q: lax.dot_general(p, q, dims, preferred_element_type=F32)
        return dot(ah, bh) + (dot(ah, bl) + dot(al, bh))
    if hi:
        return lax.dot_general(a.astype(F32), b.astype(F32), dims,
                               precision=lax.Precision.HIGHEST, preferred_element_type=F32)
    return lax.dot_general(a.astype(BF), b.astype(BF), dims, preferred_element_type=F32)


@functools.partial(jax.custom_vjp, nondiff_argnums=(2, 3, 4))
def mm(a, b, ca, cb, hi=False):
    return _dg(a, b, ca, cb, hi)


def _mm_fwd(a, b, ca, cb, hi):
    return _dg(a, b, ca, cb, hi), (a, b)


def _mm_bwd(ca, cb, hi, res, g):
    a, b = res
    if ca == 1:
        da = mm(g, b, 1, 1 - cb, hi)
    else:
        da = mm(b, g, 1 - cb, 1, hi)
    if cb == 0:
        db = mm(a, g, 1 - ca, 0, hi)
    else:
        db = mm(g, a, 0, 1 - ca, hi)
    return da.astype(a.dtype), db.astype(b.dtype)


mm.defvjp(_mm_fwd, _mm_bwd)


def matmul(name, a, b, mode, epi=None, extras=(), out_dtypes=(F32,), colshard=False,
           tm=None, tn=1024, tk=None, after=None, b_window=None):
    afters = [] if after is None else [after]
    slabs = b.ndim == 3
    if slabs:
        assert mode in ("nn", "nt") and not colshard, name
        b_rows, b_cols = b.shape[1], b.shape[0] * b.shape[2]
    else:
        b_rows, b_cols = b.shape
    if b_window is not None:
        assert mode == "nt" and not slabs, name
        b_start, b_rows = b_window
    if mode == "nn":
        (M, K), (K2, N) = a.shape, (b_rows, b_cols)
    elif mode == "nt":
        (M, K), (N, K2) = a.shape, (b_rows, b_cols)
    else:
        (K, M), (K2, N) = a.shape, b.shape
    assert K == K2, (name, a.shape, b.shape, mode)
    if tm is None:
        tm = 1024
    if tk is None:
        tk = 4096 if mode == "tn" else 2048
    tm = _tile(M, tm)
    tn = N // N_DEV if colshard else _tile(N, tn, LANE)
    tk = _tile(K, tk, LANE)
    if slabs and mode == "nn":
        tn = b.shape[2]
    if slabs and mode == "nt":
        tk = b.shape[2]
    nk = K // tk
    grid = (M // tm, N // tn, nk)
    if mode == "nn":
        a_spec = pl.BlockSpec((tm, tk), lambda i, j, k: (i, k))
        b_spec = (pl.BlockSpec((None, tk, tn), lambda i, j, k: (j, k, 0)) if slabs
                  else pl.BlockSpec((tk, tn), lambda i, j, k: (k, j)))
        ca, cb = 1, 0
    elif mode == "nt":
        a_spec = pl.BlockSpec((tm, tk), lambda i, j, k: (i, k))
        off = 0
        if b_window is not None:
            assert b_start % tn == 0, (name, b_start, tn)
            off = b_start // tn
        b_spec = (pl.BlockSpec((None, tn, tk), lambda i, j, k: (k, j, 0)) if slabs
                  else pl.BlockSpec((tn, tk), lambda i, j, k: (j + off, k)))
        ca, cb = 1, 1
    else:
        a_spec = pl.BlockSpec((tk, tm), lambda i, j, k: (k, i))
        b_spec = pl.BlockSpec((tk, tn), lambda i, j, k: (k, j))
        ca, cb = 0, 0
    ex_specs = []
    for e in extras:
        if e.shape[0] == 1:
            ex_specs.append(pl.BlockSpec((1, tn), lambda i, j, k: (0, j)))
        else:
            assert e.shape == (M, N), (name, e.shape)
            ex_specs.append(pl.BlockSpec((tm, tn), lambda i, j, k: (i, j)))
    if colshard:
        out_shape = [jax.ShapeDtypeStruct((N_DEV, M, tn), dt) for dt in out_dtypes]
        out_specs = [pl.BlockSpec((None, tm, tn), lambda i, j, k: (j, i, 0)) for _ in out_dtypes]
    else:
        out_shape = [jax.ShapeDtypeStruct((M, N), dt) for dt in out_dtypes]
        out_specs = [pl.BlockSpec((tm, tn), lambda i, j, k: (i, j)) for _ in out_dtypes]
    n_ex, n_out = len(extras), len(out_dtypes)

    def body(*refs):
        a_ref, b_ref = refs[0], refs[1]
        ex_refs = refs[2:2 + n_ex]
        first_out = 2 + n_ex + len(afters)
        o_refs = refs[first_out:first_out + n_out]
        part = _dg(a_ref[...], b_ref[...], ca, cb, False)

        def finish(acc):
            res = (acc,) if epi is None else epi(acc, *[r[...] for r in ex_refs])
            for o_ref, r in zip(o_refs, res):
                o_ref[...] = r.astype(o_ref.dtype)

        if nk == 1:
            finish(part)
            return
        acc_ref = refs[-1]
        k = pl.program_id(2)

        @pl.when(k == 0)
        def _():
            acc_ref[...] = part

        @pl.when(k > 0)
        def _():
            acc_ref[...] += part

        @pl.when(k == nk - 1)
        def _():
            finish(acc_ref[...])

    outs = pl.pallas_call(
        body, name=name, grid=grid,
        in_specs=[a_spec, b_spec] + ex_specs + [ANY] * len(afters),
        out_specs=out_specs, out_shape=out_shape,
        scratch_shapes=[pltpu.VMEM((tm, tn), F32)] if nk > 1 else [],
        compiler_params=pltpu.CompilerParams(
            dimension_semantics=("parallel", "parallel", "arbitrary"),
            vmem_limit_bytes=VMEM_LIMIT),
    )(a, b, *extras, *afters)
    return outs[0] if n_out == 1 else tuple(outs)


def norm_matmul(name, x, g, b, epi=None, extras=(), out_dtypes=(F32,), tm=1024, tn=1024, after=None):
    M, K = x.shape
    slabs = b.ndim == 3
    N = b.shape[0] * b.shape[2] if slabs else b.shape[1]
    tm, tn = _tile(M, tm), (b.shape[2] if slabs else _tile(N, tn, LANE))
    b_spec = (pl.BlockSpec((None, K, tn), lambda i, j: (j, 0, 0)) if slabs
              else pl.BlockSpec((K, tn), lambda i, j: (0, j)))
    afters = [] if after is None else [after]
    ex_specs = [pl.BlockSpec((1, tn), lambda i, j: (0, j)) if e.shape[0] == 1
                else pl.BlockSpec((tm, tn), lambda i, j: (i, j)) for e in extras]
    n_ex, n_out = len(extras), len(out_dtypes)

    def body(*refs):
        x_ref, g_ref, b_ref = refs[:3]
        ex_refs = refs[3:3 + n_ex]
        first_out = 3 + n_ex + len(afters)
        o_refs = refs[first_out:first_out + n_out]
        h_ref, h_scr = refs[first_out + n_out], refs[-1]

        @pl.when(pl.program_id(1) == 0)
        def _():
            h = rms_tile(x_ref[...], g_ref[...]).astype(h_scr.dtype)
            h_scr[...] = h
            h_ref[...] = h

        acc = _dg(h_scr[...], b_ref[...], 1, 0, False)
        res = (acc,) if epi is None else epi(acc, *[r[...] for r in ex_refs])
        for o_ref, r in zip(o_refs, res):
            o_ref[...] = r.astype(o_ref.dtype)

    outs = pl.pallas_call(
        body, name=name, grid=(M // tm, N // tn),
        in_specs=[pl.BlockSpec((tm, K), lambda i, j: (i, 0)), pl.BlockSpec((1, K), lambda i, j: (0, 0)),
                  b_spec] + ex_specs + [ANY] * len(afters),
        out_specs=[pl.BlockSpec((tm, tn), lambda i, j: (i, j)) for _ in out_dtypes]
        + [pl.BlockSpec((tm, K), lambda i, j: (i, 0))],
        out_shape=[jax.ShapeDtypeStruct((M, N), dt) for dt in out_dtypes] + [jax.ShapeDtypeStruct((M, K), BF)],
        scratch_shapes=[pltpu.VMEM((tm, K), BF)],
        compiler_params=pltpu.CompilerParams(
            dimension_semantics=("parallel", "arbitrary"), vmem_limit_bytes=VMEM_LIMIT),
    )(x, g, b, *extras, *afters)
    return tuple(outs)


def matmul_sum(name, pairs, tm=512, tn=1024, after=None):
    pairs = [(a, b if isinstance(b, tuple) else (b, 0, b.shape[0])) for a, b in pairs]
    M, N = pairs[0][0].shape[0], pairs[0][1][0].shape[1]
    tm, tn = _tile(M, tm), _tile(N, tn, LANE)
    afters = [] if after is None else [after]
    in_specs, operands = [], []
    for a, (b, start, size) in pairs:
        assert a.shape == (M, size) and b.shape[1] == N and start % size == 0, (name, a.shape, b.shape, start)
        in_specs += [pl.BlockSpec((tm, size), lambda i, j: (i, 0)),
                     pl.BlockSpec((size, tn), lambda i, j, blk=start // size: (blk, j))]
        operands += [a, b]
    n = len(pairs)

    def body(*refs):
        acc = _dg(refs[0][...], refs[1][...], 1, 0, False)
        for p in range(1, n):
            acc = acc + _dg(refs[2 * p][...], refs[2 * p + 1][...], 1, 0, False)
        refs[-1][...] = acc

    return pl.pallas_call(
        body, name=name, grid=(M // tm, N // tn),
        in_specs=in_specs + [ANY] * len(afters),
        out_specs=pl.BlockSpec((tm, tn), lambda i, j: (i, j)),
        out_shape=jax.ShapeDtypeStruct((M, N), F32),
        compiler_params=pltpu.CompilerParams(
            dimension_semantics=("parallel", "parallel"), vmem_limit_bytes=VMEM_LIMIT),
    )(*operands, *afters)


def rowcall(name, fn, rows, consts, out_rows, out_accs=(), tr=256, sp=None, R=None, after=None,
            out_lead=None, into=None):
    afters = ([] if after is None else [after]) + ([] if into is None else list(into))
    rows = [r if isinstance(r, tuple) else (r, None) for r in rows]
    R = rows[0][0].shape[-2] if R is None else R
    tr = _tile(R, tr)
    n = R // tr
    in_specs = []
    for arr, lead in rows:
        C = arr.shape[-1]
        if lead is None:
            assert arr.shape[-2] == R, (name, arr.shape, R)
            in_specs.append(pl.BlockSpec((tr, C), lambda i, *s: (i, 0)))
        elif callable(lead):
            in_specs.append(pl.BlockSpec((tr, C), lambda i, *s, lead=lead: (lead(i, n, *s), 0)))
        else:
            assert arr.shape[-2] == R, (name, arr.shape, R)
            in_specs.append(pl.BlockSpec((None, tr, C), lambda i, *s, lead=lead: (lead, i, 0)))
    for c in consts:
        in_specs.append(pl.BlockSpec(c.shape, lambda i, *s, nd=c.ndim: (0,) * nd))
    if out_lead is None:
        out_shape = [jax.ShapeDtypeStruct((R, C), dt) for C, dt in out_rows]
        out_specs = [pl.BlockSpec((tr, C), lambda i, *s: (i, 0)) for C, _ in out_rows]
    else:
        n_slab, slab = out_lead
        out_shape = [jax.ShapeDtypeStruct((n_slab, R, C), dt) for C, dt in out_rows]
        out_specs = [pl.BlockSpec((None, tr, C), lambda i, *s: (slab, i, 0)) for C, _ in out_rows]
    for shp in out_accs:
        out_shape.append(jax.ShapeDtypeStruct(shp, F32))
        out_specs.append(pl.BlockSpec(shp, lambda i, *s, nd=len(shp): (0,) * nd))
    n_in, n_row, n_acc = len(rows) + len(consts), len(out_rows), len(out_accs)
    n_sp = 0 if sp is None else 1

    def body(*refs):
        refs = refs[n_sp:]
        ins = [r[...] for r in refs[:n_in]]
        res = fn(*ins)
        if not isinstance(res, (tuple, list)):
            res = (res,)
        o_refs = refs[n_in + len(afters):]
        for o_ref, r in zip(o_refs[:n_row], res[:n_row]):
            o_ref[...] = r.astype(o_ref.dtype)
        if n_acc:
            first = pl.program_id(0) == 0
            for o_ref, r in zip(o_refs[n_row:], res[n_row:]):
                r = r.astype(F32).reshape(o_ref.shape)

                @pl.when(first)
                def _(o_ref=o_ref, r=r):
                    o_ref[...] = r

                @pl.when(jnp.logical_not(first))
                def _(o_ref=o_ref, r=r):
                    o_ref[...] += r

    params = pltpu.CompilerParams(dimension_semantics=("arbitrary",), vmem_limit_bytes=VMEM_LIMIT)
    operands = [a for a, _ in rows] + list(consts) + afters
    in_specs = in_specs + [ANY] * len(afters)
    aliases = {} if into is None else {n_sp + len(operands) - len(into) + k: k for k in range(len(into))}
    if sp is None:
        outs = pl.pallas_call(body, name=name, grid=(n,), in_specs=in_specs, out_specs=out_specs,
                              out_shape=out_shape, input_output_aliases=aliases,
                              compiler_params=params)(*operands)
    else:
        outs = pl.pallas_call(
            body, name=name, out_shape=out_shape, compiler_params=params, input_output_aliases=aliases,
            grid_spec=pltpu.PrefetchScalarGridSpec(
                num_scalar_prefetch=1, grid=(n,), in_specs=in_specs, out_specs=out_specs),
        )(sp, *operands)
    return outs[0] if len(outs) == 1 else tuple(outs)


def slabcall(name, fn, ins, out_dtypes, sp, n_out_slabs=None, cw=512):
    R, C = ins[0][0].shape[-2:]
    cw = _tile(C, cw, LANE)
    in_specs = []
    for arr, slab in ins:
        assert arr.shape[-2:] == (R, C), (name, arr.shape)
        if slab is None:
            in_specs.append(pl.BlockSpec((R, cw), lambda i, j, s: (0, j)))
        elif callable(slab):
            in_specs.append(pl.BlockSpec((None, R, cw), lambda i, j, s, slab=slab: (slab(i, s), 0, j)))
        else:
            in_specs.append(pl.BlockSpec((None, R, cw), lambda i, j, s, slab=slab: (slab, 0, j)))
    if n_out_slabs is None:
        out_shape = [jax.ShapeDtypeStruct((R, C), dt) for dt in out_dtypes]
        out_specs = [pl.BlockSpec((R, cw), lambda i, j, s: (0, j)) for _ in out_dtypes]
    else:
        out_shape = [jax.ShapeDtypeStruct((n_out_slabs, R, C), dt) for dt in out_dtypes]
        out_specs = [pl.BlockSpec((None, R, cw), lambda i, j, s: (i, 0, j)) for _ in out_dtypes]
    n_in = len(ins)

    def body(sp_ref, *refs):
        res = fn(*[r[...] for r in refs[:n_in]])
        for o_ref, r in zip(refs[n_in:], res):
            o_ref[...] = r.astype(o_ref.dtype)

    outs = pl.pallas_call(
        body, name=name, out_shape=out_shape,
        grid_spec=pltpu.PrefetchScalarGridSpec(
            num_scalar_prefetch=1, grid=(n_out_slabs or 1, C // cw), in_specs=in_specs, out_specs=out_specs),
        compiler_params=pltpu.CompilerParams(
            dimension_semantics=("arbitrary", "arbitrary"), vmem_limit_bytes=VMEM_LIMIT),
    )(sp, *[a for a, _ in ins])
    return outs[0] if len(outs) == 1 else tuple(outs)


def rms_tile(x, g):
    x = x.astype(F32)
    return x * lax.rsqrt(jnp.mean(x * x, axis=-1, keepdims=True) + EPS) * g


def gelu(x):
    return 0.5 * x * (1.0 + lax.erf(x * (1.0 / math.sqrt(2.0))))


def ln_tile(x, g, b):
    mu = jnp.mean(x, axis=-1, keepdims=True)
    xc = x - mu
    return xc * lax.rsqrt(jnp.mean(xc * xc, axis=-1, keepdims=True) + EPS) * g + b


def lane_groups(fn, width, *arrs):
    n = arrs[0].shape[-1] // width
    outs = [fn(*[a[:, i * width:(i + 1) * width] for a in arrs]) for i in range(n)]
    return jnp.concatenate(outs, axis=-1)


def mixa_post_tile(o, z, o_norm):
    dv = o_norm.shape[-1]
    on = lane_groups(lambda t: rms_tile(t, o_norm), dv, o)
    return on * jax.nn.silu(z)


def mixb_tile(uv, ln_g, ln_b, w_s, bs_t):
    G = w_s.shape[0]
    gw = ln_g.shape[-1]
    dg = gw // G
    tr = uv.shape[0]
    u = gelu(uv[:, :gw])
    vg = gelu(uv[:, gw:])
    ii = lax.broadcasted_iota(jnp.int32, (B_BLOCK, B_BLOCK), 0)
    jj = lax.broadcasted_iota(jnp.int32, (B_BLOCK, B_BLOCK), 1)
    mask = (jj // CHUNK) <= (ii // CHUNK)
    cols = []
    for g in range(G):
        sl = slice(g * dg, (g + 1) * dg)
        vn = ln_tile(vg[:, sl], ln_g[:, sl], ln_b[:, sl])
        wm = jnp.where(mask, w_s[g], 0.0)
        blocks = []
        for m in range(tr // B_BLOCK):
            blk = vn[m * B_BLOCK:(m + 1) * B_BLOCK, :]
            blocks.append(mm(wm, blk, 1, 0) + bs_t[:, g:g + 1])
        mixed = blocks[0] if len(blocks) == 1 else jnp.concatenate(blocks, axis=0)
        cols.append(u[:, sl] * mixed)
    return jnp.concatenate(cols, axis=-1)


def bgc_tile(ba, alog_row, dtb_row):
    tr = ba.shape[0]
    beta = jax.nn.sigmoid(ba)
    g = -jnp.exp(alog_row) * jax.nn.softplus(ba + dtb_row)
    ii = lax.broadcasted_iota(jnp.int32, (tr, tr), 0)
    jj = lax.broadcasted_iota(jnp.int32, (tr, tr), 1)
    tri = jnp.where((ii // CHUNK == jj // CHUNK) & (jj <= ii), 1.0, 0.0).astype(F32)
    gc = mm(tri, g, 1, 0, True)
    return beta, gc


def make_bgc(H):
    def f(ba, alog_row, dtb_row):
        beta, gc = bgc_tile(ba, alog_row, dtb_row)
        lane = lax.broadcasted_iota(jnp.int32, ba.shape, 1)
        return jnp.where(lane < H, beta, jnp.where(lane < 2 * H, gc, 0.0))
    return f


def loss_tile(x, g, target):
    y = rms_tile(x, g)
    err = y - target
    return 0.5 * jnp.sum(jnp.mean(err * err, axis=-1, keepdims=True), axis=0, keepdims=True)


def adamw_tile(w, g, m, v):
    m = ADAM_B1 * m + (1.0 - ADAM_B1) * g
    v = ADAM_B2 * v + (1.0 - ADAM_B2) * (g * g)
    m_hat = m / (1.0 - ADAM_B1 ** ADAM_STEP)
    v_hat = v / (1.0 - ADAM_B2 ** ADAM_STEP)
    delta = -ADAM_LR * (m_hat / (jnp.sqrt(v_hat) + ADAM_EPS) + ADAM_WD * w)
    return delta, m, v


CONV_ROWS = 32


def _shifted_copies(src, sh, rows):
    for b in range(SUBLANE):
        sh[b] = src[pl.ds(b, rows), :]


def _window(sh, off, rows):
    b = off % SUBLANE
    return sh[b, pl.ds(off - b, rows), :]


def _conv_rows(out, sh, w_ref, offsets, rows):
    for r0 in range(0, rows, CONV_ROWS):
        rc = min(CONV_ROWS, rows - r0)
        acc = w_ref[0:1, :] * _window(sh, offsets[0] + r0, rc)
        for k in range(1, len(offsets)):
            acc = acc + w_ref[k:k + 1, :] * _window(sh, offsets[k] + r0, rc)
        out[r0:r0 + rc, :] = acc


def _conv_wgrad(dsrc, d0, sh, offsets, rows):
    dws = []
    for off in offsets:
        acc = None
        for r0 in range(0, rows, CONV_ROWS):
            rc = min(CONV_ROWS, rows - r0)
            prod = dsrc[d0 + r0:d0 + r0 + rc, :] * _window(sh, off + r0, rc)
            for g in range(0, rc, SUBLANE):
                part = prod[g:g + SUBLANE, :]
                acc = part if acc is None else acc + part
        dws.append(jnp.sum(acc, axis=0, keepdims=True))
    return jnp.concatenate(dws, axis=0)


def _conv_specs(T, tr, hb, cw, col_blocks, rev):
    n = T // tr

    def ri(i):
        return (n - 1 - i) if rev else i

    tile_specs, halo_specs = [], []
    for off in col_blocks:
        tile_specs.append(pl.BlockSpec((tr, cw), lambda j, i, off=off: (ri(i), j + off)))
        halo_specs.append(pl.BlockSpec(
            (hb, cw), lambda j, i, off=off: (jnp.maximum(ri(i) * (tr // hb) - 1, 0), j + off)))
    return n, ri, tile_specs, halo_specs


def conv_fwd(name, x, w, consts, pre, post, col_blocks, n_out, K, out_dtype=F32, tr=256, hb=8, cw=512):
    T = x.shape[0]
    C = w.shape[1]
    tr, cw = _tile(T, tr, hb), min(cw, C)
    nb = len(col_blocks)
    n, ri, tile_specs, halo_specs = _conv_specs(T, tr, hb, cw, col_blocks, False)
    w_spec = pl.BlockSpec((K, cw), lambda j, i: (0, j))
    c_specs = [pl.BlockSpec((1, cw), lambda j, i: (0, j)) for _ in consts]

    def body(*refs):
        tiles = [r[...] for r in refs[:nb]]
        halos = [r[...] for r in refs[nb:2 * nb]]
        w_ref = refs[2 * nb]
        cs = [r[...] for r in refs[2 * nb + 1:2 * nb + 1 + len(consts)]]
        o_refs = refs[2 * nb + 1 + len(consts):-3]
        pbuf, shp, cbuf = refs[-3:]
        i = pl.program_id(1)
        pbuf[0:hb, :] = jnp.where(i > 0, pre(*halos), 0.0)
        pbuf[hb:hb + tr, :] = pre(*tiles)
        pbuf[hb + tr:hb + tr + SUBLANE, :] = jnp.zeros((SUBLANE, cw), F32)
        _shifted_copies(pbuf, shp, hb + tr)
        _conv_rows(cbuf, shp, w_ref, [hb - (K - 1) + k for k in range(K)], tr)
        res = post(cbuf[...], pl.program_id(0), *cs)
        for o_ref, r in zip(o_refs, res):
            o_ref[...] = r.astype(o_ref.dtype)

    outs = pl.pallas_call(
        body, name=name, grid=(C // cw, n),
        in_specs=tile_specs + halo_specs + [w_spec] + c_specs,
        out_specs=[pl.BlockSpec((tr, cw), lambda j, i: (i, j)) for _ in range(n_out)],
        out_shape=[jax.ShapeDtypeStruct((T, C), out_dtype) for _ in range(n_out)],
        scratch_shapes=[pltpu.VMEM((hb + tr + SUBLANE, cw), F32), pltpu.VMEM((SUBLANE, hb + tr, cw), F32),
                        pltpu.VMEM((tr, cw), F32)],
        compiler_params=pltpu.CompilerParams(
            dimension_semantics=("parallel", "arbitrary"), vmem_limit_bytes=VMEM_LIMIT),
    )(*([x] * nb), *([x] * nb), w, *consts)
    return outs[0] if n_out == 1 else tuple(outs)


def conv_bwd(name, x, w, consts, grads, pre, post, col_blocks, K, tr=256, hb=8, cw=512, recompute=True):
    T, Cx = x.shape
    C = w.shape[1]
    tr, cw = _tile(T, tr, hb), min(cw, C)
    nb = len(col_blocks)
    n, ri, tile_specs, halo_specs = _conv_specs(T, tr, hb, cw, col_blocks, True)
    w_spec = pl.BlockSpec((K, cw), lambda j, i: (0, j))
    c_specs = [pl.BlockSpec((1, cw), lambda j, i: (0, j)) for _ in consts]
    g_specs = [pl.BlockSpec((tr, cw), lambda j, i: (ri(i), j)) for _ in grads]
    nc, ng = len(consts), len(grads)

    def body(*refs):
        p = 0
        tile_refs = refs[p:p + nb]; p += nb
        halo_refs = refs[p:p + nb]; p += nb
        w_ref = refs[p]; p += 1
        cs = [r[...] for r in refs[p:p + nc]]; p += nc
        gs = [r[...] for r in refs[p:p + ng]]; p += ng
        dx_refs = refs[p:p + nb]; p += nb
        sum_refs = refs[p:p + nb]; p += nb
        dw_ref = refs[p]; p += 1
        dc_refs = refs[p:p + nc]; p += nc
        pbuf, dbuf, ebuf, carry, shp, shd, cbuf = refs[p:p + 7]
        i = pl.program_id(1)
        first = i == 0
        tiles = [r[...] for r in tile_refs]
        halos = [r[...] for r in halo_refs]
        p_tile, vjp_pre = jax.vjp(pre, *tiles)
        pbuf[0:hb, :] = jnp.where(ri(i) > 0, pre(*halos), 0.0)
        pbuf[hb:hb + tr, :] = p_tile
        pbuf[hb + tr:hb + tr + SUBLANE, :] = jnp.zeros((SUBLANE, cw), F32)
        _shifted_copies(pbuf, shp, hb + tr)
        taps = [hb - (K - 1) + k for k in range(K)]
        if recompute:
            _conv_rows(cbuf, shp, w_ref, taps, tr)
            c = cbuf[...]
        else:
            c = jnp.zeros((tr, cw), F32)
        cid = pl.program_id(0)
        _, vjp_post = jax.vjp(lambda c_, *cs_: post(c_, cid, *cs_), c, *cs)
        dres = vjp_post(tuple(g.astype(F32) for g in gs))
        dbuf[0:hb, :] = jnp.zeros((hb, cw), F32)
        dbuf[hb:hb + tr, :] = dres[0]
        dbuf[hb + tr:hb + tr + hb + SUBLANE, :] = jnp.zeros((hb + SUBLANE, cw), F32)
        _shifted_copies(dbuf, shd, hb + tr + hb)
        _conv_rows(ebuf, shd, w_ref, [K - 1 - k for k in range(K)], hb + tr)
        dw = _conv_wgrad(dbuf, hb, shp, taps, tr)

        @pl.when(jnp.logical_not(first))
        def _():
            ebuf[tr:tr + hb, :] += carry[...]

        carry[...] = ebuf[0:hb, :]
        dtiles = vjp_pre(ebuf[hb:hb + tr, :])
        for r, s, d in zip(dx_refs, sum_refs, dtiles):
            r[...] = d.astype(r.dtype)
            ds_ = jnp.sum(d, axis=0, keepdims=True)

            @pl.when(first)
            def _(s=s, ds_=ds_):
                s[...] = ds_

            @pl.when(jnp.logical_not(first))
            def _(s=s, ds_=ds_):
                s[...] += ds_

        accs = [(dw_ref, dw)] + [(r, d) for r, d in zip(dc_refs, dres[1:])]
        for r, d in accs:
            @pl.when(first)
            def _(r=r, d=d):
                r[...] = d

            @pl.when(jnp.logical_not(first))
            def _(r=r, d=d):
                r[...] += d

    n_cb = C // cw
    outs = pl.pallas_call(
        body, name=name, grid=(n_cb, n),
        in_specs=tile_specs + halo_specs + [w_spec] + c_specs + g_specs,
        out_specs=([pl.BlockSpec((tr, cw), lambda j, i: (ri(i), j)) for _ in col_blocks]
                   + [pl.BlockSpec((1, cw), lambda j, i: (0, j)) for _ in col_blocks]
                   + [pl.BlockSpec((K, cw), lambda j, i: (0, j))]
                   + [pl.BlockSpec((1, cw), lambda j, i: (0, j)) for _ in consts]),
        out_shape=([jax.ShapeDtypeStruct((T, C), BF) for _ in col_blocks]
                   + [jax.ShapeDtypeStruct((1, C), F32) for _ in col_blocks]
                   + [jax.ShapeDtypeStruct((K, C), F32)]
                   + [jax.ShapeDtypeStruct((1, C), F32) for _ in consts]),
        scratch_shapes=[pltpu.VMEM((hb + tr + SUBLANE, cw), F32), pltpu.VMEM((hb + tr + hb + SUBLANE, cw), F32),
                        pltpu.VMEM((hb + tr, cw), F32), pltpu.VMEM((hb, cw), F32),
                        pltpu.VMEM((SUBLANE, hb + tr, cw), F32), pltpu.VMEM((SUBLANE, hb + tr + hb, cw), F32),
                        pltpu.VMEM((tr, cw), F32)],
        compiler_params=pltpu.CompilerParams(
            dimension_semantics=("parallel", "arbitrary"), vmem_limit_bytes=VMEM_LIMIT),
    )(*([x] * nb), *([x] * nb), w, *consts, *grads)
    dxs = outs[:nb]
    sums = outs[nb:2 * nb]
    dw = outs[2 * nb]
    dcs = outs[2 * nb + 1:]
    return dxs, sums, dw, dcs


def make_qkv_post(dk, cw, n_qk_chunks):
    def l2(t):
        return t * lax.rsqrt(jnp.sum(t * t, axis=-1, keepdims=True) + EPS)

    def post(c, cid):
        s = jax.nn.silu(c)
        normed = lane_groups(l2, dk, s)
        return (jnp.where(cid < n_qk_chunks, normed, s),)
    return post


def glu_pre(za, zb):
    return za * jax.nn.sigmoid(zb)


def bias_post(c, cid, b):
    return (c + b,)


def _col_to_row(col):
    C = col.shape[-2]
    ii = lax.broadcasted_iota(jnp.int32, (C, C), 0)
    jj = lax.broadcasted_iota(jnp.int32, (C, C), 1)
    wide = jnp.broadcast_to(col, col.shape[:-1] + (C,))
    return jnp.sum(jnp.where(ii == jj, wide, 0.0), axis=-2, keepdims=True)


@jax.custom_vjp
def solve_with_inverse(a, rhs, x):
    return mm(x, rhs, 1, 0, THREE_PASS)


def _swi_fwd(a, rhs, x):
    sol = mm(x, rhs, 1, 0, THREE_PASS)
    return sol, (x, sol)


def _swi_bwd(res, dsol):
    x, sol = res
    drhs = mm(x, dsol, 0, 0, THREE_PASS)
    da = -mm(drhs, sol, 1, 1, THREE_PASS)
    return da, drhs, jnp.zeros_like(x)


solve_with_inverse.defvjp(_swi_fwd, _swi_bwd)


def unit_lower_inverse(a):
    C = a.shape[-1]
    ii = lax.broadcasted_iota(jnp.int32, (C, C), 0)
    jj = lax.broadcasted_iota(jnp.int32, (C, C), 1)
    x = jnp.where(ii == jj, 1.0, 0.0).astype(F32) - a
    p = mm(a, a, 1, 0, THREE_PASS)
    steps = int(math.log2(C)) - 1
    for s in range(steps):
        x = x + mm(x, p, 1, 0, THREE_PASS)
        if s < steps - 1:
            p = mm(p, p, 1, 0, THREE_PASS)
    return x


def dn_masks(C):
    ii = lax.broadcasted_iota(jnp.int32, (C, C), 0)
    jj = lax.broadcasted_iota(jnp.int32, (C, C), 1)
    return ii >= jj, ii > jj


def dn_pre(q, k, v, beta, gc):
    C, dk = q.shape[-2:]
    tri, strict = dn_masks(C)
    q = q * (dk ** -0.5)
    diff = gc - _col_to_row(gc)
    decay = jnp.where(tri, jnp.exp(jnp.where(tri, diff, 0.0)), 0.0)
    kb = k * beta
    vb = v * beta
    a = jnp.where(strict, mm(kb, k, 1, 1) * decay, 0.0)
    eg = jnp.exp(gc)
    rhs = jnp.concatenate([vb, kb * eg], axis=-1)
    attn = mm(q, k, 1, 1) * decay
    qd = q * eg
    g_last = gc[..., C - 1:C, :]
    kt = k * jnp.exp(g_last - gc)
    gl = jnp.exp(g_last)
    return a, rhs, attn, qd, kt, gl


def dn_chunk(q, k, v, beta, gc, state, x):
    dv = v.shape[-1]
    a, rhs, attn, qd, kt, gl = dn_pre(q, k, v, beta, gc)
    sol = solve_with_inverse(a, rhs, x)
    u, w = sol[..., :dv], sol[..., dv:]
    vn = u - mm(w, state, 1, 0)
    o = mm(qd, state, 1, 0) + mm(attn, vn, 1, 0)
    new_state = state * gl + mm(kt, vn, 0, 0)
    return o, new_state


def _by_head(q_ref, k_ref, v_ref, bg, H, dk, dv):
    qv = jnp.stack([q_ref[:, h * dk:(h + 1) * dk] for h in range(H)])
    kv = jnp.stack([k_ref[:, h * dk:(h + 1) * dk] for h in range(H)])
    vv = jnp.stack([v_ref[:, h * dv:(h + 1) * dv] for h in range(H)])
    beta = jnp.stack([bg[:, h:h + 1] for h in range(H)])
    gc = jnp.stack([bg[:, H + h:H + h + 1] for h in range(H)])
    return qv, kv, vv, beta, gc


def deltanet_fwd(qkv, bgc, H):
    T = qkv.shape[0]
    dk = dv = qkv.shape[1] // (3 * H)
    N = T // CHUNK

    def body(q_ref, k_ref, v_ref, bgc_ref, o_ref, x_ref, s_ref, state):
        @pl.when(pl.program_id(0) == 0)
        def _():
            state[...] = jnp.zeros((H, dk, dv), F32)

        qv, kv, vv, beta, gc = _by_head(q_ref, k_ref, v_ref, bgc_ref[...], H, dk, dv)
        a = dn_pre(qv, kv, vv, beta, gc)[0]
        x = unit_lower_inverse(a)
        s = state[...]
        o, s_new = dn_chunk(qv, kv, vv, beta, gc, s, x)
        for h in range(H):
            o_ref[:, h * dv:(h + 1) * dv] = o[h]
        x_ref[...] = x
        s_ref[...] = s
        state[...] = s_new

    return pl.pallas_call(
        body, name="deltanet_fwd", grid=(N,),
        in_specs=[pl.BlockSpec((CHUNK, H * dk), lambda n: (n, 0)),
                  pl.BlockSpec((CHUNK, H * dk), lambda n: (n, 1)),
                  pl.BlockSpec((CHUNK, H * dv), lambda n: (n, 2)),
                  pl.BlockSpec((CHUNK, LANE), lambda n: (n, 0))],
        out_specs=[pl.BlockSpec((CHUNK, H * dv), lambda n: (n, 0)),
                   pl.BlockSpec((None, H, CHUNK, CHUNK), lambda n: (n, 0, 0, 0)),
                   pl.BlockSpec((None, H, dk, dv), lambda n: (n, 0, 0, 0))],
        out_shape=[jax.ShapeDtypeStruct((T, H * dv), F32),
                   jax.ShapeDtypeStruct((N, H, CHUNK, CHUNK), F32),
                   jax.ShapeDtypeStruct((N, H, dk, dv), F32)],
        scratch_shapes=[pltpu.VMEM((H, dk, dv), F32)],
        compiler_params=pltpu.CompilerParams(
            dimension_semantics=("arbitrary",), vmem_limit_bytes=VMEM_LIMIT),
    )(qkv, qkv, qkv, bgc)


def deltanet_bwd(qkv, bgc, xinv, states, do, H):
    T = qkv.shape[0]
    dk = dv = qkv.shape[1] // (3 * H)
    N = T // CHUNK

    def body(q_ref, k_ref, v_ref, bgc_ref, x_ref, s_ref, do_ref, dqkv_ref, dbgc_ref, dstate):
        @pl.when(pl.program_id(0) == 0)
        def _():
            dstate[...] = jnp.zeros((H, dk, dv), F32)

        qv, kv, vv, beta, gc = _by_head(q_ref, k_ref, v_ref, bgc_ref[...], H, dk, dv)
        do = jnp.stack([do_ref[:, h * dv:(h + 1) * dv] for h in range(H)])
        _, vjp = jax.vjp(dn_chunk, qv, kv, vv, beta, gc, s_ref[...], x_ref[...])
        dq, dk_, dv_, dbeta, dgc, ds, _ = vjp((do, dstate[...]))
        dstate[...] = ds
        lane = lax.broadcasted_iota(jnp.int32, (CHUNK, LANE), 1)
        dbgc = jnp.zeros((CHUNK, LANE), F32)
        for h in range(H):
            dqkv_ref[:, h * dk:(h + 1) * dk] = dq[h]
            dqkv_ref[:, (H + h) * dk:(H + h + 1) * dk] = dk_[h]
            dqkv_ref[:, (2 * H + h) * dk:(2 * H + h + 1) * dk] = dv_[h]
            dbgc = dbgc + jnp.where(lane == h, dbeta[h], 0.0) + jnp.where(lane == h + H, dgc[h], 0.0)
        dbgc_ref[...] = dbgc

    rn = lambda n: N - 1 - n
    return pl.pallas_call(
        body, name="deltanet_bwd", grid=(N,),
        in_specs=[pl.BlockSpec((CHUNK, H * dk), lambda n: (rn(n), 0)),
                  pl.BlockSpec((CHUNK, H * dk), lambda n: (rn(n), 1)),
                  pl.BlockSpec((CHUNK, H * dv), lambda n: (rn(n), 2)),
                  pl.BlockSpec((CHUNK, LANE), lambda n: (rn(n), 0)),
                  pl.BlockSpec((None, H, CHUNK, CHUNK), lambda n: (rn(n), 0, 0, 0)),
                  pl.BlockSpec((None, H, dk, dv), lambda n: (rn(n), 0, 0, 0)),
                  pl.BlockSpec((CHUNK, H * dv), lambda n: (rn(n), 0))],
        out_specs=[pl.BlockSpec((CHUNK, 3 * H * dk), lambda n: (rn(n), 0)),
                   pl.BlockSpec((CHUNK, LANE), lambda n: (rn(n), 0))],
        out_shape=[jax.ShapeDtypeStruct((T, 3 * H * dk), F32),
                   jax.ShapeDtypeStruct((T, LANE), F32)],
        scratch_shapes=[pltpu.VMEM((H, dk, dv), F32)],
        compiler_params=pltpu.CompilerParams(
            dimension_semantics=("arbitrary",), vmem_limit_bytes=VMEM_LIMIT),
    )(qkv, qkv, qkv, bgc, xinv, states, do)


def _place():
    x, y, c = lax.axis_index("x"), lax.axis_index("y"), lax.axis_index("c")
    chips = [(1 - x, y), (x, 1 - y), (1 - x, 1 - y)]
    return x, y, c, chips


def all_gather(name, shards):
    na = len(shards)

    def body(*refs):
        ins, outs = refs[:na], refs[na:2 * na]
        send_sems, recv_sems, local_sems = refs[2 * na:]
        x, y, c, chips = _place()
        me, sibling = (x, y, c), (x, y, 1 - c)

        def copy(a, k, block, to, src=None):
            dst = outs[a].at[4 * block[0] + 2 * block[1] + block[2]]
            return pltpu.make_async_remote_copy(
                src_ref=dst if src is None else src, dst_ref=dst,
                send_sem=send_sems.at[a, k], recv_sem=recv_sems.at[a, k],
                device_id=to, device_id_type=MESH)

        mine, first, passed = [], [], []
        for a in range(na):
            cp = pltpu.make_async_copy(ins[a], outs[a].at[4 * x + 2 * y + c], local_sems.at[a])
            cp.start()
            mine.append(cp)
        for a in range(na):
            cps = [copy(a, 0, me, sibling, src=ins[a])]
            cps += [copy(a, 1 + j, me, (*chip, c), src=ins[a]) for j, chip in enumerate(chips)]
            for cp in cps:
                cp.start()
            first += cps
        for a in range(na):
            for j, chip in enumerate(chips):
                copy(a, 1 + j, (*chip, c), me).wait_recv()
                cp = copy(a, 4 + j, (*chip, c), sibling)
                cp.start()
                passed.append(cp)
        for a in range(na):
            copy(a, 0, sibling, me).wait_recv()
            for j, chip in enumerate(chips):
                copy(a, 4 + j, (*chip, 1 - c), me).wait_recv()
        for cp in first + passed:
            cp.wait_send()
        for cp in mine:
            cp.wait()

    outs = pl.pallas_call(
        body, name=name,
        in_specs=[ANY] * na, out_specs=[ANY] * na,
        out_shape=[jax.ShapeDtypeStruct((N_DEV,) + s.shape, s.dtype) for s in shards],
        scratch_shapes=[pltpu.SemaphoreType.DMA((na, 7)), pltpu.SemaphoreType.DMA((na, 7)),
                        pltpu.SemaphoreType.DMA((na,))],
    )(*shards)
    return list(outs)


HBM_SPEC = pl.BlockSpec(memory_space=pltpu.HBM)
SEM_SPEC = pl.BlockSpec(memory_space=pltpu.SEMAPHORE)
EFFECT = pltpu.SideEffectType.DATAFLOW_SIDE_EFFECTING


def _descriptors(plan, bufs, send_sems, recv_sems):
    return [pltpu.make_async_remote_copy(src_ref=src, dst_ref=dst, send_sem=send_sems.at[k],
                                         recv_sem=recv_sems.at[k], device_id=dev, device_id_type=MESH)
            for k, (src, dst, dev) in enumerate(plan(bufs))]


def split_start(name, bufs, plan, n, after):
    nb = len(bufs)

    def body(*refs):
        for cp in _descriptors(plan, refs[:nb], refs[nb + 1], refs[nb + 2]):
            cp.start()
        refs[-1][...] = jnp.zeros((SUBLANE, LANE), F32)

    outs = pl.pallas_call(
        body, name=name,
        out_shape=(pltpu.SemaphoreType.DMA((n,)), pltpu.SemaphoreType.DMA((n,)),
                   *[pltpu.HBM(b.shape, b.dtype) for b in bufs],
                   jax.ShapeDtypeStruct((SUBLANE, LANE), F32)),
        in_specs=[HBM_SPEC] * nb + [ANY],
        out_specs=(SEM_SPEC, SEM_SPEC, *[HBM_SPEC] * nb, pl.BlockSpec(memory_space=pltpu.VMEM)),
        input_output_aliases={i: 2 + i for i in range(nb)},
        compiler_params=pltpu.CompilerParams(has_side_effects=EFFECT),
    )(*[pltpu.with_memory_space_constraint(b, pltpu.HBM) for b in bufs], after)
    return outs[0], outs[1], list(outs[2:2 + nb]), outs[-1]


def split_wait(name, send_sems, recv_sems, bufs, plan, after):
    nb = len(bufs)

    def body(*refs):
        cps = _descriptors(plan, refs[:nb], refs[nb], refs[nb + 1])
        for cp in cps:
            cp.wait_recv()
        for cp in cps:
            cp.wait_send()
        refs[-1][...] = jnp.zeros((SUBLANE, LANE), F32)

    afters = list(after) if isinstance(after, (list, tuple)) else [after]
    outs = pl.pallas_call(
        body, name=name,
        out_shape=[pltpu.HBM(b.shape, b.dtype) for b in bufs] + [jax.ShapeDtypeStruct((SUBLANE, LANE), F32)],
        in_specs=[HBM_SPEC] * nb + [SEM_SPEC, SEM_SPEC] + [ANY] * len(afters),
        out_specs=[HBM_SPEC] * nb + [pl.BlockSpec(memory_space=pltpu.VMEM)],
        input_output_aliases={i: i for i in range(nb)},
        compiler_params=pltpu.CompilerParams(has_side_effects=EFFECT),
    )(*bufs, send_sems, recv_sems, *afters)
    return list(outs[:nb]), outs[-1]


def _block(px, py, pc):
    return 4 * px + 2 * py + pc


def plan_gather_ici(na):
    def plan(bufs):
        x, y, c, chips = _place()
        out = []
        for a in range(na):
            dst = bufs[na + a].at[_block(x, y, c)]
            out.append((bufs[a], dst, (x, y, 1 - c)))
            out += [(bufs[a], dst, (px, py, c)) for px, py in chips]
        return out
    return plan


def plan_gather_pass(na):
    def plan(bufs):
        x, y, c, chips = _place()
        out = []
        for a in range(na):
            for px, py in chips:
                blk = bufs[a].at[_block(px, py, c)]
                out.append((blk, blk, (x, y, 1 - c)))
        return out
    return plan


def plan_gather_direct(bufs):
    x, y, c, _ = _place()
    dst = bufs[1].at[_block(x, y, c)]
    flip = lambda v, f: 1 - v if f else v
    return [(bufs[0], dst, (flip(x, m >> 2 & 1), flip(y, m >> 1 & 1), flip(c, m & 1))) for m in range(1, N_DEV)]


def plan_reduce_d2d(na):
    def plan(bufs):
        x, y, c, _ = _place()
        return [(bufs[a].at[2 * s + (1 - c)], bufs[na + a].at[s], (x, y, 1 - c))
                for a in range(na) for s in range(4)]
    return plan


def plan_reduce_ici(na):
    def plan(bufs):
        x, y, c, chips = _place()
        return [(bufs[a].at[2 * px + py], bufs[na + a].at[j], (px, py, c))
                for a in range(na) for j, (px, py) in enumerate(chips)]
    return plan


def place_own(name, land, shard, dev):
    r, c = shard.shape
    tr = _tile(r, 512)

    def body(sp_ref, s_ref, land_ref, o_ref):
        o_ref[...] = s_ref[...]

    return pl.pallas_call(
        body, name=name, out_shape=jax.ShapeDtypeStruct(land.shape, land.dtype),
        grid_spec=pltpu.PrefetchScalarGridSpec(
            num_scalar_prefetch=1, grid=(r // tr,),
            in_specs=[pl.BlockSpec((tr, c), lambda i, s: (i, 0)), ANY],
            out_specs=pl.BlockSpec((None, tr, c), lambda i, s: (s[0], i, 0))),
        input_output_aliases={2: 0},
        compiler_params=pltpu.CompilerParams(dimension_semantics=("arbitrary",)),
    )(dev, shard, land)


def pack(arrs, row_mult=SUBLANE):
    pieces = []
    for a in arrs:
        f = a.reshape(-1).astype(F32)
        pad = (-f.shape[0]) % LANE
        if pad:
            f = jnp.concatenate([f, jnp.zeros((pad,), F32)])
        pieces.append(f)
    flat = jnp.concatenate(pieces)
    rows = flat.shape[0] // LANE
    pad_rows = (-rows) % row_mult
    if pad_rows:
        flat = jnp.concatenate([flat, jnp.zeros((pad_rows * LANE,), F32)])
    return flat.reshape(-1, LANE)


def unpack(buf, shapes):
    flat = buf.reshape(-1)
    outs, off = [], 0
    for shp in shapes:
        n = int(np.prod(shp))
        outs.append(flat[off:off + n].reshape(shp))
        off += n + ((-n) % LANE)
    return outs


def _vjp_rows(fn, n_row_in, n_cot):
    def bwd(*args):
        rows = args[:n_row_in]
        cots = args[n_row_in:n_row_in + n_cot]
        consts = args[n_row_in + n_cot:]
        out, vjp = jax.vjp(fn, *rows, *consts)
        if isinstance(out, (tuple, list)):
            cot = tuple(c.astype(o.dtype) for c, o in zip(cots, out))
        else:
            cot = cots[0].astype(out.dtype)
        return vjp(cot)
    return bwd


def rms_bwd(name, x, g, dh, dres, after=None):
    D = x.shape[1]
    vj = _vjp_rows(rms_tile, 1, 1)

    def f(x_, dh_, dres_, g_):
        dx, dg = vj(x_, dh_, g_)
        dx = dx + dres_
        return dx, dx, dg, jnp.sum(dx, axis=0, keepdims=True)
    return rowcall(name, f, [x, dh, dres], [g], [(D, F32), (D, BF)], [(1, D), (1, D)], after=after, tr=512)


def kernel(x, e_norm, e_w_in, e_conv_w, e_a_log, e_dt_bias, e_o_norm, e_ln_g, e_ln_b, e_w_s, e_b_s, e_w_out, o_norm, o_pw1, o_pw1_b, o_dw, o_dw_b, o_ln_g, o_ln_b, o_pw2, o_pw2_b, f_norm, f_w1, f_w2, final_norm, loss_target, m_e_norm, m_e_w_in, m_e_conv_w, m_e_a_log, m_e_dt_bias, m_e_o_norm, m_e_ln_g, m_e_ln_b, m_e_w_s, m_e_b_s, m_e_w_out, m_o_norm, m_o_pw1, m_o_pw1_b, m_o_dw, m_o_dw_b, m_o_ln_g, m_o_ln_b, m_o_pw2, m_o_pw2_b, m_f_norm, m_f_w1, m_f_w2, m_final_norm, v_e_norm, v_e_w_in, v_e_conv_w, v_e_a_log, v_e_dt_bias, v_e_o_norm, v_e_ln_g, v_e_ln_b, v_e_w_s, v_e_b_s, v_e_w_out, v_o_norm, v_o_pw1, v_o_pw1_b, v_o_dw, v_o_dw_b, v_o_ln_g, v_o_ln_b, v_o_pw2, v_o_pw2_b, v_f_norm, v_f_w1, v_f_w2, v_final_norm):
    names = ['e_norm', 'e_w_in', 'e_conv_w', 'e_a_log', 'e_dt_bias', 'e_o_norm', 'e_ln_g', 'e_ln_b', 'e_w_s', 'e_b_s', 'e_w_out', 'o_norm', 'o_pw1', 'o_pw1_b', 'o_dw', 'o_dw_b', 'o_ln_g', 'o_ln_b', 'o_pw2', 'o_pw2_b', 'f_norm', 'f_w1', 'f_w2', 'final_norm']
    W = dict(zip(names, [e_norm, e_w_in, e_conv_w, e_a_log, e_dt_bias, e_o_norm, e_ln_g, e_ln_b, e_w_s, e_b_s, e_w_out, o_norm, o_pw1, o_pw1_b, o_dw, o_dw_b, o_ln_g, o_ln_b, o_pw2, o_pw2_b, f_norm, f_w1, f_w2, final_norm]))
    Mo = dict(zip(names, [m_e_norm, m_e_w_in, m_e_conv_w, m_e_a_log, m_e_dt_bias, m_e_o_norm, m_e_ln_g, m_e_ln_b, m_e_w_s, m_e_b_s, m_e_w_out, m_o_norm, m_o_pw1, m_o_pw1_b, m_o_dw, m_o_dw_b, m_o_ln_g, m_o_ln_b, m_o_pw2, m_o_pw2_b, m_f_norm, m_f_w1, m_f_w2, m_final_norm]))
    Vo = dict(zip(names, [v_e_norm, v_e_w_in, v_e_conv_w, v_e_a_log, v_e_dt_bias, v_e_o_norm, v_e_ln_g, v_e_ln_b, v_e_w_s, v_e_b_s, v_e_w_out, v_o_norm, v_o_pw1, v_o_pw1_b, v_o_dw, v_o_dw_b, v_o_ln_g, v_o_ln_b, v_o_pw2, v_o_pw2_b, v_f_norm, v_f_w1, v_f_w2, v_final_norm]))

    T, D = x.shape[1], x.shape[2]
    H = e_a_log.shape[-1]
    dv = e_o_norm.shape[-1]
    dk = dv
    G = e_w_s.shape[1]
    AQK, AV, BW = H * dk, H * dv, e_ln_g.shape[-1]
    AQKV = 2 * AQK + AV
    in_cols = AQKV + AV + 2 * H + 2 * BW
    KA = e_conv_w.shape[1]
    KC = o_dw.shape[1]
    L = f_norm.shape[0]
    dev = 4 * lax.axis_index("x") + 2 * lax.axis_index("y") + lax.axis_index("c")
    x2d = x.reshape(T, D)
    tgt = loss_target.reshape(T, D)

    def tie(small, tok):
        return small + tok[0:1, 0:1]

    dev_sp = dev.astype(jnp.int32).reshape(1)
    where = jnp.stack([lax.axis_index("c"), 2 * lax.axis_index("x") + lax.axis_index("y")]).astype(jnp.int32)
    row = lambda a: a.reshape(1, -1).astype(F32)
    en_row = row(e_norm)

    def gather_begin(tag, shards, after):
        na = len(shards)
        lands = [lax.empty((N_DEV,) + s.shape, s.dtype) for s in shards]
        ss, rs, bufs, tok = split_start(f"gather{tag}_ici_start", shards + lands, plan_gather_ici(na), 4 * na, after)
        return (na, ss, rs, bufs), tok

    def gather_pass(tag, h, after):
        na, ss, rs, bufs = h
        bufs, tok = split_wait(f"gather{tag}_ici_wait", ss, rs, bufs, plan_gather_ici(na), after)
        ss, rs, lands, tok = split_start(f"gather{tag}_pass_start", bufs[na:], plan_gather_pass(na), 3 * na, tok)
        return (na, ss, rs, bufs[:na], lands), tok

    def gather_end(tag, h, after):
        na, ss, rs, shards, lands = h
        lands, _ = split_wait(f"gather{tag}_pass_wait", ss, rs, lands, plan_gather_pass(na), after)
        return [place_own(f"gather{tag}_own{a}", lands[a], shards[a], dev_sp) for a in range(na)]

    small_sharded = ['e_conv_w', 'o_norm', 'o_pw1_b', 'o_dw', 'o_dw_b', 'o_ln_g', 'o_ln_b', 'o_pw2_b']
    sm = all_gather("gather_small", [pack([W[n][0]]) for n in small_sharded])

    bfw = lambda w: w.astype(BF)
    hA0, tok = gather_begin("0", [bfw(jnp.swapaxes(e_w_in[0], 0, 1))], sm[0])
    hA1, tok = gather_begin("1", [bfw(e_w_out[0]), bfw(f_w1[0]), bfw(f_w2[0])], tok)
    hA2, tok = gather_begin("2", [bfw(o_pw1[0]), bfw(o_pw2[0])], tok)
    hA3, tok = gather_begin("3", [bfw(f_w1[1]), bfw(f_w2[1])], tok)
    h0 = rowcall("rms_e", rms_tile, [x2d], [en_row], [(D, BF)], after=tok, tr=512)

    full = {}
    for n, g in zip(small_sharded, sm):
        shp = W[n][0].shape
        blocks = [unpack(g[d], [shp])[0] for d in range(N_DEV)]
        full[n] = jnp.concatenate(blocks, axis=-1)
    conv_w = full['e_conv_w']
    on_row, pw1_b_row = row(full['o_norm']), row(full['o_pw1_b'])
    dw_w, dw_b_row = full['o_dw'], row(full['o_dw_b'])
    oln_g_row, oln_b_row, pw2_b_row = row(full['o_ln_g']), row(full['o_ln_b']), row(full['o_pw2_b'])
    small_names = ['e_norm', 'e_conv_w', 'e_a_log', 'e_dt_bias', 'e_o_norm', 'e_ln_g', 'e_ln_b', 'e_w_s', 'e_b_s',
                   'o_norm', 'o_pw1_b', 'o_dw', 'o_dw_b', 'o_ln_g', 'o_ln_b', 'o_pw2_b', 'f_norm', 'final_norm']
    packed_wmv = [pack([A[n] for n in small_names], 256) for A in (W, Mo, Vo)]

    hB0, tok = gather_pass("0", hA0, [h0, conv_w, dw_w, pw1_b_row] + packed_wmv)
    (g_win,) = gather_end("0", hB0, tok)
    win_t = g_win.reshape(in_cols, D)
    wt_ba = jnp.pad(win_t[AQKV + AV:AQKV + AV + 2 * H], ((0, LANE - 2 * H), (0, 0)))
    wt_uv = win_t[AQKV + AV + 2 * H:]

    alog_row = jnp.pad(row(e_a_log), ((0, 0), (H, LANE - 2 * H)))
    dtb_row = jnp.pad(row(e_dt_bias), ((0, 0), (H, LANE - 2 * H)))
    eon_row = row(e_o_norm)
    eln_g_row, eln_b_row = row(e_ln_g), row(e_ln_b)
    w_s = e_w_s[0]
    bs_t = e_b_s[0].T
    fn_rows = [row(f_norm[l]) for l in range(L)]
    fin_row = row(final_norm)

    qkv_raw = matmul("proj_qkv", h0, win_t, "nt", b_window=(0, AQKV))
    z_gate = matmul("proj_z", h0, win_t, "nt", b_window=(AQKV, AV))
    ba = matmul("proj_ba", h0, wt_ba, "nt")
    uv = matmul("proj_uv", h0, wt_uv, "nt")

    cwa = min(512, AQKV)
    qkv_post = make_qkv_post(dk, cwa, 2 * AQK // cwa)
    ident = lambda t: t
    qkv = conv_fwd("qkv_conv", qkv_raw, conv_w, [], ident, qkv_post, [0], 1, KA, cw=cwa, tr=512)
    bgc_fn = make_bgc(H)
    bgc = rowcall("bgc", bgc_fn, [ba], [alog_row, dtb_row], [(LANE, F32)])
    o_dn, xinv, states = deltanet_fwd(qkv, bgc, H)
    hB1, tok = gather_pass("1", hA1, o_dn)

    def mix_tile(o, z, uv_, o_norm_, ln_g, ln_b, w_s_, bs_t_):
        return jnp.concatenate([mixa_post_tile(o, z, o_norm_), mixb_tile(uv_, ln_g, ln_b, w_s_, bs_t_)], axis=-1)
    mix_consts = [eon_row, eln_g_row, eln_b_row, w_s, bs_t]
    mix = rowcall("mix", mix_tile, [o_dn, z_gate, uv], mix_consts, [(AV + BW, BF)], after=tok)
    g_wout, g_w1_0, g_w2_0 = gather_end("1", hB1, mix)
    wout = g_wout.reshape(-1, D)
    w1 = [jnp.moveaxis(g_w1_0, 0, 1).reshape(D, -1), None]
    w2 = [g_w2_0.reshape(-1, D), None]
    add_epi = lambda acc, r: (acc + r,)
    x1 = matmul("out_proj", mix, wout, "nn", epi=add_epi, extras=[x2d], tm=512, tn=2048)

    def relu2_epi(acc):
        r = jnp.maximum(acc, 0.0)
        return r * r, r

    hB2, tok = gather_pass("2", hA2, x1)
    a2_0, ar_0, hf0 = norm_matmul("ffn_up0", x1, fn_rows[0], g_w1_0, epi=relu2_epi, out_dtypes=(BF, BF), after=tok)
    g_pw1, g_pw2 = gather_end("2", hB2, a2_0)
    x2 = matmul("ffn_down0", a2_0, w2[0], "nn", epi=add_epi, extras=[x1])
    ffn0 = (hf0, a2_0, ar_0)
    pw1 = jnp.moveaxis(g_pw1, 0, 1).reshape(D, 2 * D)
    pw2 = g_pw2.reshape(D, D)

    bias_epi = lambda acc, b: (acc + b,)
    zc, h1 = norm_matmul("pw1", x2, on_row, g_pw1, epi=bias_epi, extras=[pw1_b_row])
    hB3, tok = gather_pass("3", hA3, zc)
    cwc = min(512, D)
    ncb = D // cwc
    cconv = conv_fwd("dw_conv", zc, dw_w, [tie(dw_b_row, tok)], glu_pre, bias_post, [0, ncb], 1, KC, hb=32, cw=cwc,
                     tr=512)
    g_w1_1, g_w2_1 = gather_end("3", hB3, cconv)
    w1[1] = jnp.moveaxis(g_w1_1, 0, 1).reshape(D, -1)
    w2[1] = g_w2_1.reshape(-1, D)
    ln_silu = lambda c, g, b: jax.nn.silu(ln_tile(c, g, b))
    s_act = rowcall("ln_silu", ln_silu, [cconv], [oln_g_row, oln_b_row], [(D, BF)], tr=512)
    x3 = matmul("pw2", s_act, pw2, "nn", epi=lambda acc, r, b: (r + (acc + b),), extras=[x2, pw2_b_row],
                tm=512, tn=2048)
    a2_1, ar_1, hf1 = norm_matmul("ffn_up1", x3, fn_rows[1], g_w1_1, epi=relu2_epi, out_dtypes=(BF, BF))
    x4 = matmul("ffn_down1", a2_1, w2[1], "nn", epi=add_epi, extras=[x3])
    ffn1 = (hf1, a2_1, ar_1)

    def loss_bwd_tile(x_, t_, g_):
        l, vjp = jax.vjp(lambda a, b: loss_tile(a, b, t_), x_, g_)
        dx, dg = vjp(jnp.ones_like(l))
        return dx, dx, l, dg
    dx4, dx4_b, loss_part, d_final = rowcall("loss_head", loss_bwd_tile, [x4, tgt], [fin_row],
                                             [(D, F32), (D, BF)], [(1, 1), (1, D)], tr=512)
    loss = lax.psum(loss_part[0, 0], ("x", "y", "c"))

    def reduce_begin(tag, grads, after):
        na = len(grads)
        lands = [lax.empty((4,) + g.shape[1:], g.dtype) for g in grads]
        ss, rs, bufs, tok = split_start(f"reduce{tag}_d2d_start", grads + lands, plan_reduce_d2d(na), 4 * na, after)
        return (na, ss, rs, bufs), tok

    def reduce_mid(tag, h, after):
        na, ss, rs, bufs = h
        bufs, tok = split_wait(f"reduce{tag}_d2d_wait", ss, rs, bufs, plan_reduce_d2d(na), after)
        parts = []
        for a, (g, rc) in enumerate(zip(bufs[:na], bufs[na:])):
            r, c = g.shape[1], g.shape[2]
            if r % SUBLANE:
                parts.append(slabcall(f"chip_sum{tag}_{a}", lambda p, q: (p + q,),
                                      [(g, lambda i, s: 2 * i + s[0]), (rc, lambda i, s: i)], [BF], where, 4))
                continue
            mine = lambda i, n, s: (2 * (i // (n // 4)) + s[0]) * (n // 4) + i % (n // 4)
            parts.append(rowcall(f"chip_sum{tag}_{a}", lambda p, q: (p + q,),
                                 [(g.reshape(N_DEV * r, c), mine), rc.reshape(4 * r, c)], [],
                                 [(c, BF)], tr=_tile(r, 512), sp=where, R=4 * r).reshape(4, r, c))
        lands = [lax.empty((3,) + p.shape[1:], p.dtype) for p in parts]
        ss, rs, bufs, tok = split_start(f"reduce{tag}_ici_start", parts + lands, plan_reduce_ici(na), 3 * na, tok)
        return (na, ss, rs, bufs), tok

    res = {}

    def reduce_end(tag, h, after, targets):
        na, ss, rs, bufs = h
        bufs, _ = split_wait(f"reduce{tag}_ici_wait", ss, rs, bufs, plan_reduce_ici(na), after)
        for a, (part, fin, (n, l)) in enumerate(zip(bufs[:na], bufs[na:], targets)):
            def f(p0, p1, p2, p3, w_, m_, v_):
                g = ((p0.astype(F32) + p1.astype(F32)) + p2.astype(F32)) + p3.astype(F32)
                return (g,) + adamw_tile(w_, g, m_, v_)
            r, C = fin.shape[-2], fin.shape[-1]
            if W[n].shape[1:] == (C, r):
                t = lambda arr: jnp.swapaxes(arr[l], 0, 1)
                outs = slabcall(f"adam{tag}_{a}", f, [(part, lambda i, s: s[1]), (fin, 0), (fin, 1), (fin, 2),
                                                      (t(W[n]), None), (t(Mo[n]), None), (t(Vo[n]), None)],
                                [F32] * 4, where)
                res[n] = tuple(jnp.swapaxes(o, 0, 1)[None] for o in outs)
                continue
            own = lambda i, n_, s: s[1] * n_ + i
            res[n] = rowcall(f"adam{tag}_{a}", f, [(part.reshape(4 * r, C), own), (fin, 0), (fin, 1), (fin, 2),
                                                   (W[n], l), (Mo[n], l), (Vo[n], l)], [],
                             [(C, F32)] * 4, tr=256, sp=where, R=r,
                             out_lead=(W[n].shape[0], l), into=res.get(n))

    dscale_epi = lambda acc, r: (acc * (2.0 * r.astype(F32)),)
    d_fnorm = [None] * L

    dpre1 = matmul("ffn_down_dx1", dx4_b, w2[1], "nt", epi=dscale_epi, extras=[ar_1], out_dtypes=(BF,))
    dw2_1 = matmul("ffn_down_dw1", a2_1, dx4_b, "tn")
    dw1_1 = matmul("ffn_up_dw1", hf1, dpre1, "tn", colshard=True)
    hD1, tok = reduce_begin("1", [dw1_1, dw2_1.reshape(N_DEV, -1, D)], dpre1)
    dhf1 = matmul("ffn_up_dx1", dpre1, w1[1], "nt", after=tok, tk=4096)
    dx3, dx3_b, d_fnorm[1], d_pw2_b = rms_bwd("rms_f_bwd1", x3, fn_rows[1], dhf1, dx4)

    ds_act = matmul("pw2_dx", dx3_b, pw2, "nt")
    hI1, tok = reduce_mid("1", hD1, ds_act)
    d_pw2 = matmul("pw2_dw", s_act, dx3_b, "tn", after=tok)
    dcconv, d_oln_g, d_oln_b = rowcall(
        "ln_silu_bwd", _vjp_rows(ln_silu, 1, 1), [cconv, ds_act], [oln_g_row, oln_b_row],
        [(D, F32)], [(1, D), (1, D)], tr=512)
    (dza, dzb), (sza, szb), d_dw, (d_dw_b,) = conv_bwd(
        "dw_conv_bwd", zc, dw_w, [tie(dw_b_row, tok)], [dcconv], glu_pre, bias_post, [0, ncb], KC, hb=32, cw=cwc, tr=256,
        recompute=False)
    dzc = jnp.concatenate([dza, dzb], axis=-1)
    d_pw1_b = jnp.concatenate([sza, szb], axis=-1)
    d_pw1 = matmul("pw1_dw", h1, dzc, "tn", colshard=True)
    dh1 = matmul("pw1_dx", dzc, pw1, "nt")
    dx2, dx2_b, d_onorm, _ = rms_bwd("rms_o_bwd", x2, on_row, dh1, dx3)
    reduce_end("1", hI1, dx2, [('f_w1', 1), ('f_w2', 1)])

    hD2, tok = reduce_begin("2", [d_pw1, d_pw2.reshape(N_DEV, -1, D)], dx2)
    dpre0 = matmul("ffn_down_dx0", dx2_b, w2[0], "nt", epi=dscale_epi, extras=[ar_0], out_dtypes=(BF,), after=tok)
    dw2_0 = matmul("ffn_down_dw0", a2_0, dx2_b, "tn")
    hI2, tok = reduce_mid("2", hD2, dw2_0)
    dw1_0 = matmul("ffn_up_dw0", hf0, dpre0, "tn", colshard=True, after=tok)
    dhf0 = matmul("ffn_up_dx0", dpre0, w1[0], "nt", tk=4096)
    dx1, dx1_b, d_fnorm[0], _ = rms_bwd("rms_f_bwd0", x1, fn_rows[0], dhf0, dx2)
    reduce_end("2", hI2, dx1, [('o_pw1', 0), ('o_pw2', 0)])

    dmix = matmul("out_proj_dx", dx1_b, wout, "nt")
    d_wout = matmul("out_proj_dw", mix, dx1_b, "tn")
    hD3, tok = reduce_begin("3", [dw1_0, dw2_0.reshape(N_DEV, -1, D), d_wout.reshape(N_DEV, -1, D)], dmix)
    do_dn, dz_gate, duv, d_eon, d_eln_g, d_eln_b, d_ws, d_bs_t = rowcall(
        "mix_bwd", _vjp_rows(mix_tile, 3, 1), [o_dn, z_gate, uv, dmix], mix_consts,
        [(AV, F32), (AV, BF), (2 * BW, BF)], [(1, dv), (1, BW), (1, BW), w_s.shape, bs_t.shape], after=tok)
    dqkv, dbgc = deltanet_bwd(qkv, bgc, xinv, states, do_dn, H)
    hI3, tok = reduce_mid("3", hD3, dbgc)
    bgc_bwd = _vjp_rows(bgc_fn, 1, 1)
    dba, d_alog_row, d_dtb_row = rowcall(
        "bgc_bwd", bgc_bwd, [ba, dbgc], [alog_row, dtb_row], [(LANE, BF)], [(1, LANE), (1, LANE)])
    (dqkv_raw,), _, d_conv_w, _ = conv_bwd(
        "qkv_conv_bwd", qkv_raw, tie(conv_w, tok), [], [dqkv], ident, qkv_post, [0], KA, cw=cwa, tr=512)

    dw_qkv = matmul("proj_qkv_dw", dqkv_raw, h0, "tn")
    dw_z = matmul("proj_z_dw", dz_gate, h0, "tn")
    dw_ba = matmul("proj_ba_dw", dba, h0, "tn")
    dw_uv = matmul("proj_uv_dw", duv, h0, "tn")
    d_win_t = jnp.concatenate([dw_qkv, dw_z, dw_ba[:2 * H], dw_uv], axis=0)
    G_win = d_win_t.reshape(N_DEV, in_cols // N_DEV, D)
    hD4, tok = reduce_begin("4", [G_win], dw_uv)
    reduce_end("3", hI3, tok, [('f_w1', 0), ('f_w2', 0), ('e_w_out', 0)])

    d_alog = d_alog_row[:, H:2 * H]
    d_dtb = d_dtb_row[:, H:2 * H]
    early_grads = [d_conv_w, d_alog, d_dtb, d_eon, d_eln_g, d_eln_b, d_ws, d_bs_t.T,
                   d_onorm, d_pw1_b, d_dw, d_dw_b, d_oln_g, d_oln_b, d_pw2_b,
                   jnp.concatenate(d_fnorm, axis=0), d_final]
    early_packed = pack(early_grads, 256)
    ss_s, rs_s, bufs_s, tok_s = split_start(
        "small_grads_start", [early_packed, lax.empty((N_DEV,) + early_packed.shape, F32)],
        plan_gather_direct, N_DEV - 1, d_conv_w)
    hI4, tok = reduce_mid("4", hD4, [tok_s] + [res[n][0] for n in ('f_w1', 'f_w2', 'o_pw1', 'e_w_out')])
    dh0 = matmul_sum("proj_dx", [(dqkv_raw, (win_t, 0, AQKV)), (dz_gate, (win_t, AQKV, AV)),
                                (dba, wt_ba), (duv, wt_uv)], after=tok)
    grad_x, _, d_enorm, _ = rms_bwd("rms_e_bwd", x2d, en_row, dh0, dx1, after=tok)
    late_all = all_gather("gather_e_norm_grad", [pack([d_enorm])])[0]
    (early_shard, early_land), _ = split_wait("small_grads_wait", ss_s, rs_s, bufs_s, plan_gather_direct, late_all)
    early_all = place_own("small_grads_own", early_land, early_shard, dev_sp)

    def sum8(*ps):
        s = ps[0]
        for p in ps[1:]:
            s = s + p
        return (s,)
    gs_sum = rowcall("small_sum", sum8, [(early_all, d) for d in range(N_DEV)], [], [(LANE, F32)])
    late_sum = rowcall("e_norm_sum", sum8, [(late_all, d) for d in range(N_DEV)], [], [(LANE, F32)])
    reduce_end("4", hI4, gs_sum, [('e_w_in', 0)])
    g_full = dict(zip(small_names[1:], unpack(gs_sum, [g.shape for g in early_grads])))
    g_full['e_norm'] = unpack(late_sum, [d_enorm.shape])[0]
    g_loc = {}
    for n in small_names:
        g = g_full[n]
        if n in small_sharded:
            per = g.shape[-1] // N_DEV
            g = lax.dynamic_slice_in_dim(g, dev * per, per, axis=-1)
        g_loc[n] = g.reshape(W[n].shape)
    packs = [packed_wmv[0], pack([g_loc[n] for n in small_names], 256), packed_wmv[1], packed_wmv[2]]
    d_s, m_s, v_s = rowcall("adam_small", adamw_tile, packs, [], [(LANE, F32)] * 3)
    shapes = [W[n].shape for n in small_names]
    for n, d_, m_, v_ in zip(small_names, unpack(d_s, shapes), unpack(m_s, shapes), unpack(v_s, shapes)):
        res[n] = (g_loc[n], d_, m_, v_)

    grads = [res[n][0] for n in names]
    deltas = [res[n][1] for n in names]
    new_m = [res[n][2] for n in names]
    new_v = [res[n][3] for n in names]
    return (loss, grad_x.reshape(x.shape), *grads, *deltas, *new_m, *new_v)
```

```python
import functools
import math

import jax
import jax.numpy as jnp
import numpy as np
from jax import lax
from jax.experimental import pallas as pl
from jax.experimental.pallas import tpu as pltpu

F32 = jnp.float32
BF = jnp.bfloat16
EPS = 1e-6
CHUNK = 64
B_BLOCK = 128
LANE = 128
SUBLANE = 8
N_DEV = 8
VMEM_LIMIT = 56 * 1024 * 1024

ADAM_LR = 0.001
ADAM_B1 = 0.9
ADAM_B2 = 0.999
ADAM_EPS = 1e-08
ADAM_WD = 0.01
ADAM_STEP = 10

MESH = pl.DeviceIdType.MESH
ANY = pl.BlockSpec(memory_space=pl.ANY)


def _tile(n, pref, mult=SUBLANE):
    if n <= pref:
        return n
    t = (pref // mult) * mult
    while t >= mult:
        if n % t == 0:
            return t
        t -= mult
    return n


THREE_PASS = 3


def _dg(a, b, ca, cb, hi):
    nb = a.ndim - 2
    batch = tuple(range(nb))
    dims = (((ca + nb,), (cb + nb,)), (batch, batch))
    if hi == THREE_PASS:
        a, b = a.astype(F32), b.astype(F32)
        ah, bh = a.astype(BF), b.astype(BF)
        al, bl = (a - ah.astype(F32)).astype(BF), (b - bh.astype(F32)).astype(BF)
        dot = lambda p, q: lax.dot_general(p, q, dims, preferred_element_type=F32)
        return dot(ah, bh) + (dot(ah, bl) + dot(al, bh))
    if hi:
        return lax.dot_general(a.astype(F32), b.astype(F32), dims,
                               precision=lax.Precision.HIGHEST, preferred_element_type=F32)
    return lax.dot_general(a.astype(BF), b.astype(BF), dims, preferred_element_type=F32)


@functools.partial(jax.custom_vjp, nondiff_argnums=(2, 3, 4))
def mm(a, b, ca, cb, hi=False):
    return _dg(a, b, ca, cb, hi)


def _mm_fwd(a, b, ca, cb, hi):
    return _dg(a, b, ca, cb, hi), (a, b)


def _mm_bwd(ca, cb, hi, res, g):
    a, b = res
    if ca == 1:
        da = mm(g, b, 1, 1 - cb, hi)
    else:
        da = mm(b, g, 1 - cb, 1, hi)
    if cb == 0:
        db = mm(a, g, 1 - ca, 0, hi)
    else:
        db = mm(g, a, 0, 1 - ca, hi)
    return da.astype(a.dtype), db.astype(b.dtype)


mm.defvjp(_mm_fwd, _mm_bwd)


def matmul(name, a, b, mode, epi=None, extras=(), out_dtypes=(F32,), colshard=False,
           tm=None, tn=1024, tk=None, after=None, b_window=None):
    afters = [] if after is None else [after]
    slabs = b.ndim == 3
    if slabs:
        assert mode in ("nn", "nt") and not colshard, name
        b_rows, b_cols = b.shape[1], b.shape[0] * b.shape[2]
    else:
        b_rows, b_cols = b.shape
    if b_window is not None:
        assert mode == "nt" and not slabs, name
        b_start, b_rows = b_window
    if mode == "nn":
        (M, K), (K2, N) = a.shape, (b_rows, b_cols)
    elif mode == "nt":
        (M, K), (N, K2) = a.shape, (b_rows, b_cols)
    else:
        (K, M), (K2, N) = a.shape, b.shape
    assert K == K2, (name, a.shape, b.shape, mode)
    if tm is None:
        tm = 1024
    if tk is None:
        tk = 4096 if mode == "tn" else 2048
    tm = _tile(M, tm)
    tn = N // N_DEV if colshard else _tile(N, tn, LANE)
    tk = _tile(K, tk, LANE)
    if slabs and mode == "nn":
        tn = b.shape[2]
    if slabs and mode == "nt":
        tk = b.shape[2]
    nk = K // tk
    grid = (M // tm, N // tn, nk)
    if mode == "nn":
        a_spec = pl.BlockSpec((tm, tk), lambda i, j, k: (i, k))
        b_spec = (pl.BlockSpec((None, tk, tn), lambda i, j, k: (j, k, 0)) if slabs
                  else pl.BlockSpec((tk, tn), lambda i, j, k: (k, j)))
        ca, cb = 1, 0
    elif mode == "nt":
        a_spec = pl.BlockSpec((tm, tk), lambda i, j, k: (i, k))
        off = 0
        if b_window is not None:
            assert b_start % tn == 0, (name, b_start, tn)
            off = b_start // tn
        b_spec = (pl.BlockSpec((None, tn, tk), lambda i, j, k: (k, j, 0)) if slabs
                  else pl.BlockSpec((tn, tk), lambda i, j, k: (j + off, k)))
        ca, cb = 1, 1
    else:
        a_spec = pl.BlockSpec((tk, tm), lambda i, j, k: (k, i))
        b_spec = pl.BlockSpec((tk, tn), lambda i, j, k: (k, j))
        ca, cb = 0, 0
    ex_specs = []
    for e in extras:
        if e.shape[0] == 1:
            ex_specs.append(pl.BlockSpec((1, tn), lambda i, j, k: (0, j)))
        else:
            assert e.shape == (M, N), (name, e.shape)
            ex_specs.append(pl.BlockSpec((tm, tn), lambda i, j, k: (i, j)))
    if colshard:
        out_shape = [jax.ShapeDtypeStruct((N_DEV, M, tn), dt) for dt in out_dtypes]
        out_specs = [pl.BlockSpec((None, tm, tn), lambda i, j, k: (j, i, 0)) for _ in out_dtypes]
    else:
        out_shape = [jax.ShapeDtypeStruct((M, N), dt) for dt in out_dtypes]
        out_specs = [pl.BlockSpec((tm, tn), lambda i, j, k: (i, j)) for _ in out_dtypes]
    n_ex, n_out = len(extras), len(out_dtypes)

    def body(*refs):
        a_ref, b_ref = refs[0], refs[1]
        ex_refs = refs[2:2 + n_ex]
        first_out = 2 + n_ex + len(afters)
        o_refs = refs[first_out:first_out + n_out]
        part = _dg(a_ref[...], b_ref[...], ca, cb, False)

        def finish(acc):
            res = (acc,) if epi is None else epi(acc, *[r[...] for r in ex_refs])
            for o_ref, r in zip(o_refs, res):
                o_ref[...] = r.astype(o_ref.dtype)

        if nk == 1:
            finish(part)
            return
        acc_ref = refs[-1]
        k = pl.program_id(2)

        @pl.when(k == 0)
        def _():
            acc_ref[...] = part

        @pl.when(k > 0)
        def _():
            acc_ref[...] += part

        @pl.when(k == nk - 1)
        def _():
            finish(acc_ref[...])

    outs = pl.pallas_call(
        body, name=name, grid=grid,
        in_specs=[a_spec, b_spec] + ex_specs + [ANY] * len(afters),
        out_specs=out_specs, out_shape=out_shape,
        scratch_shapes=[pltpu.VMEM((tm, tn), F32)] if nk > 1 else [],
        compiler_params=pltpu.CompilerParams(
            dimension_semantics=("parallel", "parallel", "arbitrary"),
            vmem_limit_bytes=VMEM_LIMIT),
    )(a, b, *extras, *afters)
    return outs[0] if n_out == 1 else tuple(outs)


def norm_matmul(name, x, g, b, epi=None, extras=(), out_dtypes=(F32,), tm=1024, tn=1024, after=None):
    M, K = x.shape
    slabs = b.ndim == 3
    N = b.shape[0] * b.shape[2] if slabs else b.shape[1]
    tm, tn = _tile(M, tm), (b.shape[2] if slabs else _tile(N, tn, LANE))
    b_spec = (pl.BlockSpec((None, K, tn), lambda i, j: (j, 0, 0)) if slabs
              else pl.BlockSpec((K, tn), lambda i, j: (0, j)))
    afters = [] if after is None else [after]
    ex_specs = [pl.BlockSpec((1, tn), lambda i, j: (0, j)) if e.shape[0] == 1
                else pl.BlockSpec((tm, tn), lambda i, j: (i, j)) for e in extras]
    n_ex, n_out = len(extras), len(out_dtypes)

    def body(*refs):
        x_ref, g_ref, b_ref = refs[:3]
        ex_refs = refs[3:3 + n_ex]
        first_out = 3 + n_ex + len(afters)
        o_refs = refs[first_out:first_out + n_out]
        h_ref, h_scr = refs[first_out + n_out], refs[-1]

        @pl.when(pl.program_id(1) == 0)
        def _():
            h = rms_tile(x_ref[...], g_ref[...]).astype(h_scr.dtype)
            h_scr[...] = h
            h_ref[...] = h

        acc = _dg(h_scr[...], b_ref[...], 1, 0, False)
        res = (acc,) if epi is None else epi(acc, *[r[...] for r in ex_refs])
        for o_ref, r in zip(o_refs, res):
            o_ref[...] = r.astype(o_ref.dtype)

    outs = pl.pallas_call(
        body, name=name, grid=(M // tm, N // tn),
        in_specs=[pl.BlockSpec((tm, K), lambda i, j: (i, 0)), pl.BlockSpec((1, K), lambda i, j: (0, 0)),
                  b_spec] + ex_specs + [ANY] * len(afters),
        out_specs=[pl.BlockSpec((tm, tn), lambda i, j: (i, j)) for _ in out_dtypes]
        + [pl.BlockSpec((tm, K), lambda i, j: (i, 0))],
        out_shape=[jax.ShapeDtypeStruct((M, N), dt) for dt in out_dtypes] + [jax.ShapeDtypeStruct((M, K), BF)],
        scratch_shapes=[pltpu.VMEM((tm, K), BF)],
        compiler_params=pltpu.CompilerParams(
            dimension_semantics=("parallel", "arbitrary"), vmem_limit_bytes=VMEM_LIMIT),
    )(x, g, b, *extras, *afters)
    return tuple(outs)


def matmul_sum(name, pairs, tm=512, tn=1024, after=None):
    pairs = [(a, b if isinstance(b, tuple) else (b, 0, b.shape[0])) for a, b in pairs]
    M, N = pairs[0][0].shape[0], pairs[0][1][0].shape[1]
    tm, tn = _tile(M, tm), _tile(N, tn, LANE)
    afters = [] if after is None else [after]
    in_specs, operands = [], []
    for a, (b, start, size) in pairs:
        assert a.shape == (M, size) and b.shape[1] == N and start % size == 0, (name, a.shape, b.shape, start)
        in_specs += [pl.BlockSpec((tm, size), lambda i, j: (i, 0)),
                     pl.BlockSpec((size, tn), lambda i, j, blk=start // size: (blk, j))]
        operands += [a, b]
    n = len(pairs)

    def body(*refs):
        acc = _dg(refs[0][...], refs[1][...], 1, 0, False)
        for p in range(1, n):
            acc = acc + _dg(refs[2 * p][...], refs[2 * p + 1][...], 1, 0, False)
        refs[-1][...] = acc

    return pl.pallas_call(
        body, name=name, grid=(M // tm, N // tn),
        in_specs=in_specs + [ANY] * len(afters),
        out_specs=pl.BlockSpec((tm, tn), lambda i, j: (i, j)),
        out_shape=jax.ShapeDtypeStruct((M, N), F32),
        compiler_params=pltpu.CompilerParams(
            dimension_semantics=("parallel", "parallel"), vmem_limit_bytes=VMEM_LIMIT),
    )(*operands, *afters)


def rowcall(name, fn, rows, consts, out_rows, out_accs=(), tr=256, sp=None, R=None, after=None,
            out_lead=None, into=None):
    afters = ([] if after is None else [after]) + ([] if into is None else list(into))
    rows = [r if isinstance(r, tuple) else (r, None) for r in rows]
    R = rows[0][0].shape[-2] if R is None else R
    tr = _tile(R, tr)
    n = R // tr
    in_specs = []
    for arr, lead in rows:
        C = arr.shape[-1]
        if lead is None:
            assert arr.shape[-2] == R, (name, arr.shape, R)
            in_specs.append(pl.BlockSpec((tr, C), lambda i, *s: (i, 0)))
        elif callable(lead):
            in_specs.append(pl.BlockSpec((tr, C), lambda i, *s, lead=lead: (lead(i, n, *s), 0)))
        else:
            assert arr.shape[-2] == R, (name, arr.shape, R)
            in_specs.append(pl.BlockSpec((None, tr, C), lambda i, *s, lead=lead: (lead, i, 0)))
    for c in consts:
        in_specs.append(pl.BlockSpec(c.shape, lambda i, *s, nd=c.ndim: (0,) * nd))
    if out_lead is None:
        out_shape = [jax.ShapeDtypeStruct((R, C), dt) for C, dt in out_rows]
        out_specs = [pl.BlockSpec((tr, C), lambda i, *s: (i, 0)) for C, _ in out_rows]
    else:
        n_slab, slab = out_lead
        out_shape = [jax.ShapeDtypeStruct((n_slab, R, C), dt) for C, dt in out_rows]
        out_specs = [pl.BlockSpec((None, tr, C), lambda i, *s: (slab, i, 0)) for C, _ in out_rows]
    for shp in out_accs:
        out_shape.append(jax.ShapeDtypeStruct(shp, F32))
        out_specs.append(pl.BlockSpec(shp, lambda i, *s, nd=len(shp): (0,) * nd))
    n_in, n_row, n_acc = len(rows) + len(consts), len(out_rows), len(out_accs)
    n_sp = 0 if sp is None else 1

    def body(*refs):
        refs = refs[n_sp:]
        ins = [r[...] for r in refs[:n_in]]
        res = fn(*ins)
        if not isinstance(res, (tuple, list)):
            res = (res,)
        o_refs = refs[n_in + len(afters):]
        for o_ref, r in zip(o_refs[:n_row], res[:n_row]):
            o_ref[...] = r.astype(o_ref.dtype)
        if n_acc:
            first = pl.program_id(0) == 0
            for o_ref, r in zip(o_refs[n_row:], res[n_row:]):
                r = r.astype(F32).reshape(o_ref.shape)

                @pl.when(first)
                def _(o_ref=o_ref, r=r):
                    o_ref[...] = r

                @pl.when(jnp.logical_not(first))
                def _(o_ref=o_ref, r=r):
                    o_ref[...] += r

    params = pltpu.CompilerParams(dimension_semantics=("arbitrary",), vmem_limit_bytes=VMEM_LIMIT)
    operands = [a for a, _ in rows] + list(consts) + afters
    in_specs = in_specs + [ANY] * len(afters)
    aliases = {} if into is None else {n_sp + len(operands) - len(into) + k: k for k in range(len(into))}
    if sp is None:
        outs = pl.pallas_call(body, name=name, grid=(n,), in_specs=in_specs, out_specs=out_specs,
                              out_shape=out_shape, input_output_aliases=aliases,
                              compiler_params=params)(*operands)
    else:
        outs = pl.pallas_call(
            body, name=name, out_shape=out_shape, compiler_params=params, input_output_aliases=aliases,
            grid_spec=pltpu.PrefetchScalarGridSpec(
                num_scalar_prefetch=1, grid=(n,), in_specs=in_specs, out_specs=out_specs),
        )(sp, *operands)
    return outs[0] if len(outs) == 1 else tuple(outs)


def slabcall(name, fn, ins, out_dtypes, sp, n_out_slabs=None, cw=512):
    R, C = ins[0][0].shape[-2:]
    cw = _tile(C, cw, LANE)
    in_specs = []
    for arr, slab in ins:
        assert arr.shape[-2:] == (R, C), (name, arr.shape)
        if slab is None:
            in_specs.append(pl.BlockSpec((R, cw), lambda i, j, s: (0, j)))
        elif callable(slab):
            in_specs.append(pl.BlockSpec((None, R, cw), lambda i, j, s, slab=slab: (slab(i, s), 0, j)))
        else:
            in_specs.append(pl.BlockSpec((None, R, cw), lambda i, j, s, slab=slab: (slab, 0, j)))
    if n_out_slabs is None:
        out_shape = [jax.ShapeDtypeStruct((R, C), dt) for dt in out_dtypes]
        out_specs = [pl.BlockSpec((R, cw), lambda i, j, s: (0, j)) for _ in out_dtypes]
    else:
        out_shape = [jax.ShapeDtypeStruct((n_out_slabs, R, C), dt) for dt in out_dtypes]
        out_specs = [pl.BlockSpec((None, R, cw), lambda i, j, s: (i, 0, j)) for _ in out_dtypes]
    n_in = len(ins)

    def body(sp_ref, *refs):
        res = fn(*[r[...] for r in refs[:n_in]])
        for o_ref, r in zip(refs[n_in:], res):
            o_ref[...] = r.astype(o_ref.dtype)

    outs = pl.pallas_call(
        body, name=name, out_shape=out_shape,
        grid_spec=pltpu.PrefetchScalarGridSpec(
            num_scalar_prefetch=1, grid=(n_out_slabs or 1, C // cw), in_specs=in_specs, out_specs=out_specs),
        compiler_params=pltpu.CompilerParams(
            dimension_semantics=("arbitrary", "arbitrary"), vmem_limit_bytes=VMEM_LIMIT),
    )(sp, *[a for a, _ in ins])
    return outs[0] if len(outs) == 1 else tuple(outs)


def rms_tile(x, g):
    x = x.astype(F32)
    return x * lax.rsqrt(jnp.mean(x * x, axis=-1, keepdims=True) + EPS) * g


def gelu(x):
    return 0.5 * x * (1.0 + lax.erf(x * (1.0 / math.sqrt(2.0))))


def ln_tile(x, g, b):
    mu = jnp.mean(x, axis=-1, keepdims=True)
    xc = x - mu
    return xc * lax.rsqrt(jnp.mean(xc * xc, axis=-1, keepdims=True) + EPS) * g + b


def lane_groups(fn, width, *arrs):
    n = arrs[0].shape[-1] // width
    outs = [fn(*[a[:, i * width:(i + 1) * width] for a in arrs]) for i in range(n)]
    return jnp.concatenate(outs, axis=-1)


def mixa_post_tile(o, z, o_norm):
    dv = o_norm.shape[-1]
    on = lane_groups(lambda t: rms_tile(t, o_norm), dv, o)
    return on * jax.nn.silu(z)


def mixb_tile(uv, ln_g, ln_b, w_s, bs_t):
    G = w_s.shape[0]
    gw = ln_g.shape[-1]
    dg = gw // G
    tr = uv.shape[0]
    u = gelu(uv[:, :gw])
    vg = gelu(uv[:, gw:])
    ii = lax.broadcasted_iota(jnp.int32, (B_BLOCK, B_BLOCK), 0)
    jj = lax.broadcasted_iota(jnp.int32, (B_BLOCK, B_BLOCK), 1)
    mask = (jj // CHUNK) <= (ii // CHUNK)
    cols = []
    for g in range(G):
        sl = slice(g * dg, (g + 1) * dg)
        vn = ln_tile(vg[:, sl], ln_g[:, sl], ln_b[:, sl])
        wm = jnp.where(mask, w_s[g], 0.0)
        blocks = []
        for m in range(tr // B_BLOCK):
            blk = vn[m * B_BLOCK:(m + 1) * B_BLOCK, :]
            blocks.append(mm(wm, blk, 1, 0) + bs_t[:, g:g + 1])
        mixed = blocks[0] if len(blocks) == 1 else jnp.concatenate(blocks, axis=0)
        cols.append(u[:, sl] * mixed)
    return jnp.concatenate(cols, axis=-1)


def bgc_tile(ba, alog_row, dtb_row):
    tr = ba.shape[0]
    beta = jax.nn.sigmoid(ba)
    g = -jnp.exp(alog_row) * jax.nn.softplus(ba + dtb_row)
    ii = lax.broadcasted_iota(jnp.int32, (tr, tr), 0)
    jj = lax.broadcasted_iota(jnp.int32, (tr, tr), 1)
    tri = jnp.where((ii // CHUNK == jj // CHUNK) & (jj <= ii), 1.0, 0.0).astype(F32)
    gc = mm(tri, g, 1, 0, True)
    return beta, gc


def make_bgc(H):
    def f(ba, alog_row, dtb_row):
        beta, gc = bgc_tile(ba, alog_row, dtb_row)
        lane = lax.broadcasted_iota(jnp.int32, ba.shape, 1)
        return jnp.where(lane < H, beta, jnp.where(lane < 2 * H, gc, 0.0))
    return f


def loss_tile(x, g, target):
    y = rms_tile(x, g)
    err = y - target
    return 0.5 * jnp.sum(jnp.mean(err * err, axis=-1, keepdims=True), axis=0, keepdims=True)


def adamw_tile(w, g, m, v):
    m = ADAM_B1 * m + (1.0 - ADAM_B1) * g
    v = ADAM_B2 * v + (1.0 - ADAM_B2) * (g * g)
    m_hat = m / (1.0 - ADAM_B1 ** ADAM_STEP)
    v_hat = v / (1.0 - ADAM_B2 ** ADAM_STEP)
    delta = -ADAM_LR * (m_hat / (jnp.sqrt(v_hat) + ADAM_EPS) + ADAM_WD * w)
    return delta, m, v


CONV_ROWS = 32


def _shifted_copies(src, sh, rows):
    for b in range(SUBLANE):
        sh[b] = src[pl.ds(b, rows), :]


def _window(sh, off, rows):
    b = off % SUBLANE
    return sh[b, pl.ds(off - b, rows), :]


def _conv_rows(out, sh, w_ref, offsets, rows):
    for r0 in range(0, rows, CONV_ROWS):
        rc = min(CONV_ROWS, rows - r0)
        acc = w_ref[0:1, :] * _window(sh, offsets[0] + r0, rc)
        for k in range(1, len(offsets)):
            acc = acc + w_ref[k:k + 1, :] * _window(sh, offsets[k] + r0, rc)
        out[r0:r0 + rc, :] = acc


def _conv_wgrad(dsrc, d0, sh, offsets, rows):
    dws = []
    for off in offsets:
        acc = None
        for r0 in range(0, rows, CONV_ROWS):
            rc = min(CONV_ROWS, rows - r0)
            prod = dsrc[d0 + r0:d0 + r0 + rc, :] * _window(sh, off + r0, rc)
            for g in range(0, rc, SUBLANE):
                part = prod[g:g + SUBLANE, :]
                acc = part if acc is None else acc + part
        dws.append(jnp.sum(acc, axis=0, keepdims=True))
    return jnp.concatenate(dws, axis=0)


def _conv_specs(T, tr, hb, cw, col_blocks, rev):
    n = T // tr

    def ri(i):
        return (n - 1 - i) if rev else i

    tile_specs, halo_specs = [], []
    for off in col_blocks:
        tile_specs.append(pl.BlockSpec((tr, cw), lambda j, i, off=off: (ri(i), j + off)))
        halo_specs.append(pl.BlockSpec(
            (hb, cw), lambda j, i, off=off: (jnp.maximum(ri(i) * (tr // hb) - 1, 0), j + off)))
    return n, ri, tile_specs, halo_specs


def conv_fwd(name, x, w, consts, pre, post, col_blocks, n_out, K, out_dtype=F32, tr=256, hb=8, cw=512):
    T = x.shape[0]
    C = w.shape[1]
    tr, cw = _tile(T, tr, hb), min(cw, C)
    nb = len(col_blocks)
    n, ri, tile_specs, halo_specs = _conv_specs(T, tr, hb, cw, col_blocks, False)
    w_spec = pl.BlockSpec((K, cw), lambda j, i: (0, j))
    c_specs = [pl.BlockSpec((1, cw), lambda j, i: (0, j)) for _ in consts]

    def body(*refs):
        tiles = [r[...] for r in refs[:nb]]
        halos = [r[...] for r in refs[nb:2 * nb]]
        w_ref = refs[2 * nb]
        cs = [r[...] for r in refs[2 * nb + 1:2 * nb + 1 + len(consts)]]
        o_refs = refs[2 * nb + 1 + len(consts):-3]
        pbuf, shp, cbuf = refs[-3:]
        i = pl.program_id(1)
        pbuf[0:hb, :] = jnp.where(i > 0, pre(*halos), 0.0)
        pbuf[hb:hb + tr, :] = pre(*tiles)
        pbuf[hb + tr:hb + tr + SUBLANE, :] = jnp.zeros((SUBLANE, cw), F32)
        _shifted_copies(pbuf, shp, hb + tr)
        _conv_rows(cbuf, shp, w_ref, [hb - (K - 1) + k for k in range(K)], tr)
        res = post(cbuf[...], pl.program_id(0), *cs)
        for o_ref, r in zip(o_refs, res):
            o_ref[...] = r.astype(o_ref.dtype)

    outs = pl.pallas_call(
        body, name=name, grid=(C // cw, n),
        in_specs=tile_specs + halo_specs + [w_spec] + c_specs,
        out_specs=[pl.BlockSpec((tr, cw), lambda j, i: (i, j)) for _ in range(n_out)],
        out_shape=[jax.ShapeDtypeStruct((T, C), out_dtype) for _ in range(n_out)],
        scratch_shapes=[pltpu.VMEM((hb + tr + SUBLANE, cw), F32), pltpu.VMEM((SUBLANE, hb + tr, cw), F32),
                        pltpu.VMEM((tr, cw), F32)],
        compiler_params=pltpu.CompilerParams(
            dimension_semantics=("parallel", "arbitrary"), vmem_limit_bytes=VMEM_LIMIT),
    )(*([x] * nb), *([x] * nb), w, *consts)
    return outs[0] if n_out == 1 else tuple(outs)


def conv_bwd(name, x, w, consts, grads, pre, post, col_blocks, K, tr=256, hb=8, cw=512, recompute=True):
    T, Cx = x.shape
    C = w.shape[1]
    tr, cw = _tile(T, tr, hb), min(cw, C)
    nb = len(col_blocks)
    n, ri, tile_specs, halo_specs = _conv_specs(T, tr, hb, cw, col_blocks, True)
    w_spec = pl.BlockSpec((K, cw), lambda j, i: (0, j))
    c_specs = [pl.BlockSpec((1, cw), lambda j, i: (0, j)) for _ in consts]
    g_specs = [pl.BlockSpec((tr, cw), lambda j, i: (ri(i), j)) for _ in grads]
    nc, ng = len(consts), len(grads)

    def body(*refs):
        p = 0
        tile_refs = refs[p:p + nb]; p += nb
        halo_refs = refs[p:p + nb]; p += nb
        w_ref = refs[p]; p += 1
        cs = [r[...] for r in refs[p:p + nc]]; p += nc
        gs = [r[...] for r in refs[p:p + ng]]; p += ng
        dx_refs = refs[p:p + nb]; p += nb
        sum_refs = refs[p:p + nb]; p += nb
        dw_ref = refs[p]; p += 1
        dc_refs = refs[p:p + nc]; p += nc
        pbuf, dbuf, ebuf, carry, shp, shd, cbuf = refs[p:p + 7]
        i = pl.program_id(1)
        first = i == 0
        tiles = [r[...] for r in tile_refs]
        halos = [r[...] for r in halo_refs]
        p_tile, vjp_pre = jax.vjp(pre, *tiles)
        pbuf[0:hb, :] = jnp.where(ri(i) > 0, pre(*halos), 0.0)
        pbuf[hb:hb + tr, :] = p_tile
        pbuf[hb + tr:hb + tr + SUBLANE, :] = jnp.zeros((SUBLANE, cw), F32)
        _shifted_copies(pbuf, shp, hb + tr)
        taps = [hb - (K - 1) + k for k in range(K)]
        if recompute:
            _conv_rows(cbuf, shp, w_ref, taps, tr)
            c = cbuf[...]
        else:
            c = jnp.zeros((tr, cw), F32)
        cid = pl.program_id(0)
        _, vjp_post = jax.vjp(lambda c_, *cs_: post(c_, cid, *cs_), c, *cs)
        dres = vjp_post(tuple(g.astype(F32) for g in gs))
        dbuf[0:hb, :] = jnp.zeros((hb, cw), F32)
        dbuf[hb:hb + tr, :] = dres[0]
        dbuf[hb + tr:hb + tr + hb + SUBLANE, :] = jnp.zeros((hb + SUBLANE, cw), F32)
        _shifted_copies(dbuf, shd, hb + tr + hb)
        _conv_rows(ebuf, shd, w_ref, [K - 1 - k for k in range(K)], hb + tr)
        dw = _conv_wgrad(dbuf, hb, shp, taps, tr)

        @pl.when(jnp.logical_not(first))
        def _():
            ebuf[tr:tr + hb, :] += carry[...]

        carry[...] = ebuf[0:hb, :]
        dtiles = vjp_pre(ebuf[hb:hb + tr, :])
        for r, s, d in zip(dx_refs, sum_refs, dtiles):
            r[...] = d.astype(r.dtype)
            ds_ = jnp.sum(d, axis=0, keepdims=True)

            @pl.when(first)
            def _(s=s, ds_=ds_):
                s[...] = ds_

            @pl.when(jnp.logical_not(first))
            def _(s=s, ds_=ds_):
                s[...] += ds_

        accs = [(dw_ref, dw)] + [(r, d) for r, d in zip(dc_refs, dres[1:])]
        for r, d in accs:
            @pl.when(first)
            def _(r=r, d=d):
                r[...] = d

            @pl.when(jnp.logical_not(first))
            def _(r=r, d=d):
                r[...] += d

    n_cb = C // cw
    outs = pl.pallas_call(
        body, name=name, grid=(n_cb, n),
        in_specs=tile_specs + halo_specs + [w_spec] + c_specs + g_specs,
        out_specs=([pl.BlockSpec((tr, cw), lambda j, i: (ri(i), j)) for _ in col_blocks]
                   + [pl.BlockSpec((1, cw), lambda j, i: (0, j)) for _ in col_blocks]
                   + [pl.BlockSpec((K, cw), lambda j, i: (0, j))]
                   + [pl.BlockSpec((1, cw), lambda j, i: (0, j)) for _ in consts]),
        out_shape=([jax.ShapeDtypeStruct((T, C), BF) for _ in col_blocks]
                   + [jax.ShapeDtypeStruct((1, C), F32) for _ in col_blocks]
                   + [jax.ShapeDtypeStruct((K, C), F32)]
                   + [jax.ShapeDtypeStruct((1, C), F32) for _ in consts]),
        scratch_shapes=[pltpu.VMEM((hb + tr + SUBLANE, cw), F32), pltpu.VMEM((hb + tr + hb + SUBLANE, cw), F32),
                        pltpu.VMEM((hb + tr, cw), F32), pltpu.VMEM((hb, cw), F32),
                        pltpu.VMEM((SUBLANE, hb + tr, cw), F32), pltpu.VMEM((SUBLANE, hb + tr + hb, cw), F32),
                        pltpu.VMEM((tr, cw), F32)],
        compiler_params=pltpu.CompilerParams(
            dimension_semantics=("parallel", "arbitrary"), vmem_limit_bytes=VMEM_LIMIT),
    )(*([x] * nb), *([x] * nb), w, *consts, *grads)
    dxs = outs[:nb]
    sums = outs[nb:2 * nb]
    dw = outs[2 * nb]
    dcs = outs[2 * nb + 1:]
    return dxs, sums, dw, dcs


def make_qkv_post(dk, cw, n_qk_chunks):
    def l2(t):
        return t * lax.rsqrt(jnp.sum(t * t, axis=-1, keepdims=True) + EPS)

    def post(c, cid):
        s = jax.nn.silu(c)
        normed = lane_groups(l2, dk, s)
        return (jnp.where(cid < n_qk_chunks, normed, s),)
    return post


def glu_pre(za, zb):
    return za * jax.nn.sigmoid(zb)


def bias_post(c, cid, b):
    return (c + b,)


def _col_to_row(col):
    C = col.shape[-2]
    ii = lax.broadcasted_iota(jnp.int32, (C, C), 0)
    jj = lax.broadcasted_iota(jnp.int32, (C, C), 1)
    wide = jnp.broadcast_to(col, col.shape[:-1] + (C,))
    return jnp.sum(jnp.where(ii == jj, wide, 0.0), axis=-2, keepdims=True)


@jax.custom_vjp
def solve_with_inverse(a, rhs, x):
    return mm(x, rhs, 1, 0, THREE_PASS)


def _swi_fwd(a, rhs, x):
    sol = mm(x, rhs, 1, 0, THREE_PASS)
    return sol, (x, sol)


def _swi_bwd(res, dsol):
    x, sol = res
    drhs = mm(x, dsol, 0, 0, THREE_PASS)
    da = -mm(drhs, sol, 1, 1, THREE_PASS)
    return da, drhs, jnp.zeros_like(x)


solve_with_inverse.defvjp(_swi_fwd, _swi_bwd)


def unit_lower_inverse(a):
    C = a.shape[-1]
    ii = lax.broadcasted_iota(jnp.int32, (C, C), 0)
    jj = lax.broadcasted_iota(jnp.int32, (C, C), 1)
    x = jnp.where(ii == jj, 1.0, 0.0).astype(F32) - a
    p = mm(a, a, 1, 0, THREE_PASS)
    steps = int(math.log2(C)) - 1
    for s in range(steps):
        x = x + mm(x, p, 1, 0, THREE_PASS)
        if s < steps - 1:
            p = mm(p, p, 1, 0, THREE_PASS)
    return x


def dn_masks(C):
    ii = lax.broadcasted_iota(jnp.int32, (C, C), 0)
    jj = lax.broadcasted_iota(jnp.int32, (C, C), 1)
    return ii >= jj, ii > jj


def dn_pre(q, k, v, beta, gc):
    C, dk = q.shape[-2:]
    tri, strict = dn_masks(C)
    q = q * (dk ** -0.5)
    diff = gc - _col_to_row(gc)
    decay = jnp.where(tri, jnp.exp(jnp.where(tri, diff, 0.0)), 0.0)
    kb = k * beta
    vb = v * beta
    a = jnp.where(strict, mm(kb, k, 1, 1) * decay, 0.0)
    eg = jnp.exp(gc)
    rhs = jnp.concatenate([vb, kb * eg], axis=-1)
    attn = mm(q, k, 1, 1) * decay
    qd = q * eg
    g_last = gc[..., C - 1:C, :]
    kt = k * jnp.exp(g_last - gc)
    gl = jnp.exp(g_last)
    return a, rhs, attn, qd, kt, gl


def dn_chunk(q, k, v, beta, gc, state, x):
    dv = v.shape[-1]
    a, rhs, attn, qd, kt, gl = dn_pre(q, k, v, beta, gc)
    sol = solve_with_inverse(a, rhs, x)
    u, w = sol[..., :dv], sol[..., dv:]
    vn = u - mm(w, state, 1, 0)
    o = mm(qd, state, 1, 0) + mm(attn, vn, 1, 0)
    new_state = state * gl + mm(kt, vn, 0, 0)
    return o, new_state


def _by_head(q_ref, k_ref, v_ref, bg, H, dk, dv):
    qv = jnp.stack([q_ref[:, h * dk:(h + 1) * dk] for h in range(H)])
    kv = jnp.stack([k_ref[:, h * dk:(h + 1) * dk] for h in range(H)])
    vv = jnp.stack([v_ref[:, h * dv:(h + 1) * dv] for h in range(H)])
    beta = jnp.stack([bg[:, h:h + 1] for h in range(H)])
    gc = jnp.stack([bg[:, H + h:H + h + 1] for h in range(H)])
    return qv, kv, vv, beta, gc


def deltanet_fwd(qkv, bgc, H):
    T = qkv.shape[0]
    dk = dv = qkv.shape[1] // (3 * H)
    N = T // CHUNK

    def body(q_ref, k_ref, v_ref, bgc_ref, o_ref, x_ref, s_ref, state):
        @pl.when(pl.program_id(0) == 0)
        def _():
            state[...] = jnp.zeros((H, dk, dv), F32)

        qv, kv, vv, beta, gc = _by_head(q_ref, k_ref, v_ref, bgc_ref[...], H, dk, dv)
        a = dn_pre(qv, kv, vv, beta, gc)[0]
        x = unit_lower_inverse(a)
        s = state[...]
        o, s_new = dn_chunk(qv, kv, vv, beta, gc, s, x)
        for h in range(H):
            o_ref[:, h * dv:(h + 1) * dv] = o[h]
        x_ref[...] = x
        s_ref[...] = s
        state[...] = s_new

    return pl.pallas_call(
        body, name="deltanet_fwd", grid=(N,),
        in_specs=[pl.BlockSpec((CHUNK, H * dk), lambda n: (n, 0)),
                  pl.BlockSpec((CHUNK, H * dk), lambda n: (n, 1)),
                  pl.BlockSpec((CHUNK, H * dv), lambda n: (n, 2)),
                  pl.BlockSpec((CHUNK, LANE), lambda n: (n, 0))],
        out_specs=[pl.BlockSpec((CHUNK, H * dv), lambda n: (n, 0)),
                   pl.BlockSpec((None, H, CHUNK, CHUNK), lambda n: (n, 0, 0, 0)),
                   pl.BlockSpec((None, H, dk, dv), lambda n: (n, 0, 0, 0))],
        out_shape=[jax.ShapeDtypeStruct((T, H * dv), F32),
                   jax.ShapeDtypeStruct((N, H, CHUNK, CHUNK), F32),
                   jax.ShapeDtypeStruct((N, H, dk, dv), F32)],
        scratch_shapes=[pltpu.VMEM((H, dk, dv), F32)],
        compiler_params=pltpu.CompilerParams(
            dimension_semantics=("arbitrary",), vmem_limit_bytes=VMEM_LIMIT),
    )(qkv, qkv, qkv, bgc)


def deltanet_bwd(qkv, bgc, xinv, states, do, H):
    T = qkv.shape[0]
    dk = dv = qkv.shape[1] // (3 * H)
    N = T // CHUNK

    def body(q_ref, k_ref, v_ref, bgc_ref, x_ref, s_ref, do_ref, dqkv_ref, dbgc_ref, dstate):
        @pl.when(pl.program_id(0) == 0)
        def _():
            dstate[...] = jnp.zeros((H, dk, dv), F32)

        qv, kv, vv, beta, gc = _by_head(q_ref, k_ref, v_ref, bgc_ref[...], H, dk, dv)
        do = jnp.stack([do_ref[:, h * dv:(h + 1) * dv] for h in range(H)])
        _, vjp = jax.vjp(dn_chunk, qv, kv, vv, beta, gc, s_ref[...], x_ref[...])
        dq, dk_, dv_, dbeta, dgc, ds, _ = vjp((do, dstate[...]))
        dstate[...] = ds
        lane = lax.broadcasted_iota(jnp.int32, (CHUNK, LANE), 1)
        dbgc = jnp.zeros((CHUNK, LANE), F32)
        for h in range(H):
            dqkv_ref[:, h * dk:(h + 1) * dk] = dq[h]
            dqkv_ref[:, (H + h) * dk:(H + h + 1) * dk] = dk_[h]
            dqkv_ref[:, (2 * H + h) * dk:(2 * H + h + 1) * dk] = dv_[h]
            dbgc = dbgc + jnp.where(lane == h, dbeta[h], 0.0) + jnp.where(lane == h + H, dgc[h], 0.0)
        dbgc_ref[...] = dbgc

    rn = lambda n: N - 1 - n
    return pl.pallas_call(
        body, name="deltanet_bwd", grid=(N,),
        in_specs=[pl.BlockSpec((CHUNK, H * dk), lambda n: (rn(n), 0)),
                  pl.BlockSpec((CHUNK, H * dk), lambda n: (rn(n), 1)),
                  pl.BlockSpec((CHUNK, H * dv), lambda n: (rn(n), 2)),
                  pl.BlockSpec((CHUNK, LANE), lambda n: (rn(n), 0)),
                  pl.BlockSpec((None, H, CHUNK, CHUNK), lambda n: (rn(n), 0, 0, 0)),
                  pl.BlockSpec((None, H, dk, dv), lambda n: (rn(n), 0, 0, 0)),
                  pl.BlockSpec((CHUNK, H * dv), lambda n: (rn(n), 0))],
        out_specs=[pl.BlockSpec((CHUNK, 3 * H * dk), lambda n: (rn(n), 0)),
                   pl.BlockSpec((CHUNK, LANE), lambda n: (rn(n), 0))],
        out_shape=[jax.ShapeDtypeStruct((T, 3 * H * dk), F32),
                   jax.ShapeDtypeStruct((T, LANE), F32)],
        scratch_shapes=[pltpu.VMEM((H, dk, dv), F32)],
        compiler_params=pltpu.CompilerParams(
            dimension_semantics=("arbitrary",), vmem_limit_bytes=VMEM_LIMIT),
    )(qkv, qkv, qkv, bgc, xinv, states, do)


def _place():
    x, y, c = lax.axis_index("x"), lax.axis_index("y"), lax.axis_index("c")
    chips = [(1 - x, y), (x, 1 - y), (1 - x, 1 - y)]
    return x, y, c, chips


def all_gather(name, shards):
    na = len(shards)

    def body(*refs):
        ins, outs = refs[:na], refs[na:2 * na]
        send_sems, recv_sems, local_sems = refs[2 * na:]
        x, y, c, chips = _place()
        me, sibling = (x, y, c), (x, y, 1 - c)

        def copy(a, k, block, to, src=None):
            dst = outs[a].at[4 * block[0] + 2 * block[1] + block[2]]
            return pltpu.make_async_remote_copy(
                src_ref=dst if src is None else src, dst_ref=dst,
                send_sem=send_sems.at[a, k], recv_sem=recv_sems.at[a, k],
                device_id=to, device_id_type=MESH)

        mine, first, passed = [], [], []
        for a in range(na):
            cp = pltpu.make_async_copy(ins[a], outs[a].at[4 * x + 2 * y + c], local_sems.at[a])
            cp.start()
            mine.append(cp)
        for a in range(na):
            cps = [copy(a, 0, me, sibling, src=ins[a])]
            cps += [copy(a, 1 + j, me, (*chip, c), src=ins[a]) for j, chip in enumerate(chips)]
            for cp in cps:
                cp.start()
            first += cps
        for a in range(na):
            for j, chip in enumerate(chips):
                copy(a, 1 + j, (*chip, c), me).wait_recv()
                cp = copy(a, 4 + j, (*chip, c), sibling)
                cp.start()
                passed.append(cp)
        for a in range(na):
            copy(a, 0, sibling, me).wait_recv()
            for j, chip in enumerate(chips):
                copy(a, 4 + j, (*chip, 1 - c), me).wait_recv()
        for cp in first + passed:
            cp.wait_send()
        for cp in mine:
            cp.wait()

    outs = pl.pallas_call(
        body, name=name,
        in_specs=[ANY] * na, out_specs=[ANY] * na,
        out_shape=[jax.ShapeDtypeStruct((N_DEV,) + s.shape, s.dtype) for s in shards],
        scratch_shapes=[pltpu.SemaphoreType.DMA((na, 7)), pltpu.SemaphoreType.DMA((na, 7)),
                        pltpu.SemaphoreType.DMA((na,))],
    )(*shards)
    return list(outs)


HBM_SPEC = pl.BlockSpec(memory_space=pltpu.HBM)
SEM_SPEC = pl.BlockSpec(memory_space=pltpu.SEMAPHORE)
EFFECT = pltpu.SideEffectType.DATAFLOW_SIDE_EFFECTING


def _descriptors(plan, bufs, send_sems, recv_sems):
    return [pltpu.make_async_remote_copy(src_ref=src, dst_ref=dst, send_sem=send_sems.at[k],
                                         recv_sem=recv_sems.at[k], device_id=dev, device_id_type=MESH)
            for k, (src, dst, dev) in enumerate(plan(bufs))]


def split_start(name, bufs, plan, n, after):
    nb = len(bufs)

    def body(*refs):
        for cp in _descriptors(plan, refs[:nb], refs[nb + 1], refs[nb + 2]):
            cp.start()
        refs[-1][...] = jnp.zeros((SUBLANE, LANE), F32)

    outs = pl.pallas_call(
        body, name=name,
        out_shape=(pltpu.SemaphoreType.DMA((n,)), pltpu.SemaphoreType.DMA((n,)),
                   *[pltpu.HBM(b.shape, b.dtype) for b in bufs],
                   jax.ShapeDtypeStruct((SUBLANE, LANE), F32)),
        in_specs=[HBM_SPEC] * nb + [ANY],
        out_specs=(SEM_SPEC, SEM_SPEC, *[HBM_SPEC] * nb, pl.BlockSpec(memory_space=pltpu.VMEM)),
        input_output_aliases={i: 2 + i for i in range(nb)},
        compiler_params=pltpu.CompilerParams(has_side_effects=EFFECT),
    )(*[pltpu.with_memory_space_constraint(b, pltpu.HBM) for b in bufs], after)
    return outs[0], outs[1], list(outs[2:2 + nb]), outs[-1]


def split_wait(name, send_sems, recv_sems, bufs, plan, after):
    nb = len(bufs)

    def body(*refs):
        cps = _descriptors(plan, refs[:nb], refs[nb], refs[nb + 1])
        for cp in cps:
            cp.wait_recv()
        for cp in cps:
            cp.wait_send()
        refs[-1][...] = jnp.zeros((SUBLANE, LANE), F32)

    afters = list(after) if isinstance(after, (list, tuple)) else [after]
    outs = pl.pallas_call(
        body, name=name,
        out_shape=[pltpu.HBM(b.shape, b.dtype) for b in bufs] + [jax.ShapeDtypeStruct((SUBLANE, LANE), F32)],
        in_specs=[HBM_SPEC] * nb + [SEM_SPEC, SEM_SPEC] + [ANY] * len(afters),
        out_specs=[HBM_SPEC] * nb + [pl.BlockSpec(memory_space=pltpu.VMEM)],
        input_output_aliases={i: i for i in range(nb)},
        compiler_params=pltpu.CompilerParams(has_side_effects=EFFECT),
    )(*bufs, send_sems, recv_sems, *afters)
    return list(outs[:nb]), outs[-1]


def _block(px, py, pc):
    return 4 * px + 2 * py + pc


def plan_gather_ici(na):
    def plan(bufs):
        x, y, c, chips = _place()
        out = []
        for a in range(na):
            dst = bufs[na + a].at[_block(x, y, c)]
            out.append((bufs[a], dst, (x, y, 1 - c)))
            out += [(bufs[a], dst, (px, py, c)) for px, py in chips]
        return out
    return plan


def plan_gather_pass(na):
    def plan(bufs):
        x, y, c, chips = _place()
        out = []
        for a in range(na):
            for px, py in chips:
                blk = bufs[a].at[_block(px, py, c)]
                out.append((blk, blk, (x, y, 1 - c)))
        return out
    return plan


def plan_gather_direct(bufs):
    x, y, c, _ = _place()
    dst = bufs[1].at[_block(x, y, c)]
    flip = lambda v, f: 1 - v if f else v
    return [(bufs[0], dst, (flip(x, m >> 2 & 1), flip(y, m >> 1 & 1), flip(c, m & 1))) for m in range(1, N_DEV)]


def plan_reduce_d2d(na):
    def plan(bufs):
        x, y, c, _ = _place()
        return [(bufs[a].at[2 * s + (1 - c)], bufs[na + a].at[s], (x, y, 1 - c))
                for a in range(na) for s in range(4)]
    return plan


def plan_reduce_ici(na):
    def plan(bufs):
        x, y, c, chips = _place()
        return [(bufs[a].at[2 * px + py], bufs[na + a].at[j], (px, py, c))
                for a in range(na) for j, (px, py) in enumerate(chips)]
    return plan


def place_own(name, land, shard, dev):
    r, c = shard.shape
    tr = _tile(r, 512)

    def body(sp_ref, s_ref, land_ref, o_ref):
        o_ref[...] = s_ref[...]

    return pl.pallas_call(
        body, name=name, out_shape=jax.ShapeDtypeStruct(land.shape, land.dtype),
        grid_spec=pltpu.PrefetchScalarGridSpec(
            num_scalar_prefetch=1, grid=(r // tr,),
            in_specs=[pl.BlockSpec((tr, c), lambda i, s: (i, 0)), ANY],
            out_specs=pl.BlockSpec((None, tr, c), lambda i, s: (s[0], i, 0))),
        input_output_aliases={2: 0},
        compiler_params=pltpu.CompilerParams(dimension_semantics=("arbitrary",)),
    )(dev, shard, land)


def pack(arrs, row_mult=SUBLANE):
    pieces = []
    for a in arrs:
        f = a.reshape(-1).astype(F32)
        pad = (-f.shape[0]) % LANE
        if pad:
            f = jnp.concatenate([f, jnp.zeros((pad,), F32)])
        pieces.append(f)
    flat = jnp.concatenate(pieces)
    rows = flat.shape[0] // LANE
    pad_rows = (-rows) % row_mult
    if pad_rows:
        flat = jnp.concatenate([flat, jnp.zeros((pad_rows * LANE,), F32)])
    return flat.reshape(-1, LANE)


def unpack(buf, shapes):
    flat = buf.reshape(-1)
    outs, off = [], 0
    for shp in shapes:
        n = int(np.prod(shp))
        outs.append(flat[off:off + n].reshape(shp))
        off += n + ((-n) % LANE)
    return outs


def _vjp_rows(fn, n_row_in, n_cot):
    def bwd(*args):
        rows = args[:n_row_in]
        cots = args[n_row_in:n_row_in + n_cot]
        consts = args[n_row_in + n_cot:]
        out, vjp = jax.vjp(fn, *rows, *consts)
        if isinstance(out, (tuple, list)):
            cot = tuple(c.astype(o.dtype) for c, o in zip(cots, out))
        else:
            cot = cots[0].astype(out.dtype)
        return vjp(cot)
    return bwd


def rms_bwd(name, x, g, dh, dres, after=None):
    D = x.shape[1]
    vj = _vjp_rows(rms_tile, 1, 1)

    def f(x_, dh_, dres_, g_):
        dx, dg = vj(x_, dh_, g_)
        dx = dx + dres_
        return dx, dx, dg, jnp.sum(dx, axis=0, keepdims=True)
    return rowcall(name, f, [x, dh, dres], [g], [(D, F32), (D, BF)], [(1, D), (1, D)], after=after, tr=512)


def kernel(x, e_norm, e_w_in, e_conv_w, e_a_log, e_dt_bias, e_o_norm, e_ln_g, e_ln_b, e_w_s, e_b_s, e_w_out, o_norm, o_pw1, o_pw1_b, o_dw, o_dw_b, o_ln_g, o_ln_b, o_pw2, o_pw2_b, f_norm, f_w1, f_w2, final_norm, loss_target, m_e_norm, m_e_w_in, m_e_conv_w, m_e_a_log, m_e_dt_bias, m_e_o_norm, m_e_ln_g, m_e_ln_b, m_e_w_s, m_e_b_s, m_e_w_out, m_o_norm, m_o_pw1, m_o_pw1_b, m_o_dw, m_o_dw_b, m_o_ln_g, m_o_ln_b, m_o_pw2, m_o_pw2_b, m_f_norm, m_f_w1, m_f_w2, m_final_norm, v_e_norm, v_e_w_in, v_e_conv_w, v_e_a_log, v_e_dt_bias, v_e_o_norm, v_e_ln_g, v_e_ln_b, v_e_w_s, v_e_b_s, v_e_w_out, v_o_norm, v_o_pw1, v_o_pw1_b, v_o_dw, v_o_dw_b, v_o_ln_g, v_o_ln_b, v_o_pw2, v_o_pw2_b, v_f_norm, v_f_w1, v_f_w2, v_final_norm):
    names = ['e_norm', 'e_w_in', 'e_conv_w', 'e_a_log', 'e_dt_bias', 'e_o_norm', 'e_ln_g', 'e_ln_b', 'e_w_s', 'e_b_s', 'e_w_out', 'o_norm', 'o_pw1', 'o_pw1_b', 'o_dw', 'o_dw_b', 'o_ln_g', 'o_ln_b', 'o_pw2', 'o_pw2_b', 'f_norm', 'f_w1', 'f_w2', 'final_norm']
    W = dict(zip(names, [e_norm, e_w_in, e_conv_w, e_a_log, e_dt_bias, e_o_norm, e_ln_g, e_ln_b, e_w_s, e_b_s, e_w_out, o_norm, o_pw1, o_pw1_b, o_dw, o_dw_b, o_ln_g, o_ln_b, o_pw2, o_pw2_b, f_norm, f_w1, f_w2, final_norm]))
    Mo = dict(zip(names, [m_e_norm, m_e_w_in, m_e_conv_w, m_e_a_log, m_e_dt_bias, m_e_o_norm, m_e_ln_g, m_e_ln_b, m_e_w_s, m_e_b_s, m_e_w_out, m_o_norm, m_o_pw1, m_o_pw1_b, m_o_dw, m_o_dw_b, m_o_ln_g, m_o_ln_b, m_o_pw2, m_o_pw2_b, m_f_norm, m_f_w1, m_f_w2, m_final_norm]))
    Vo = dict(zip(names, [v_e_norm, v_e_w_in, v_e_conv_w, v_e_a_log, v_e_dt_bias, v_e_o_norm, v_e_ln_g, v_e_ln_b, v_e_w_s, v_e_b_s, v_e_w_out, v_o_norm, v_o_pw1, v_o_pw1_b, v_o_dw, v_o_dw_b, v_o_ln_g, v_o_ln_b, v_o_pw2, v_o_pw2_b, v_f_norm, v_f_w1, v_f_w2, v_final_norm]))

    T, D = x.shape[1], x.shape[2]
    H = e_a_log.shape[-1]
    dv = e_o_norm.shape[-1]
    dk = dv
    G = e_w_s.shape[1]
    AQK, AV, BW = H * dk, H * dv, e_ln_g.shape[-1]
    AQKV = 2 * AQK + AV
    in_cols = AQKV + AV + 2 * H + 2 * BW
    KA = e_conv_w.shape[1]
    KC = o_dw.shape[1]
    L = f_norm.shape[0]
    dev = 4 * lax.axis_index("x") + 2 * lax.axis_index("y") + lax.axis_index("c")
    x2d = x.reshape(T, D)
    tgt = loss_target.reshape(T, D)

    def tie(small, tok):
        return small + tok[0:1, 0:1]

    dev_sp = dev.astype(jnp.int32).reshape(1)
    where = jnp.stack([lax.axis_index("c"), 2 * lax.axis_index("x") + lax.axis_index("y")]).astype(jnp.int32)
    row = lambda a: a.reshape(1, -1).astype(F32)
    en_row = row(e_norm)

    def gather_begin(tag, shards, after):
        na = len(shards)
        lands = [lax.empty((N_DEV,) + s.shape, s.dtype) for s in shards]
        ss, rs, bufs, tok = split_start(f"gather{tag}_ici_start", shards + lands, plan_gather_ici(na), 4 * na, after)
        return (na, ss, rs, bufs), tok

    def gather_pass(tag, h, after):
        na, ss, rs, bufs = h
        bufs, tok = split_wait(f"gather{tag}_ici_wait", ss, rs, bufs, plan_gather_ici(na), after)
        ss, rs, lands, tok = split_start(f"gather{tag}_pass_start", bufs[na:], plan_gather_pass(na), 3 * na, tok)
        return (na, ss, rs, bufs[:na], lands), tok

    def gather_end(tag, h, after):
        na, ss, rs, shards, lands = h
        lands, _ = split_wait(f"gather{tag}_pass_wait", ss, rs, lands, plan_gather_pass(na), after)
        return [place_own(f"gather{tag}_own{a}", lands[a], shards[a], dev_sp) for a in range(na)]

    small_sharded = ['e_conv_w', 'o_norm', 'o_pw1_b', 'o_dw', 'o_dw_b', 'o_ln_g', 'o_ln_b', 'o_pw2_b']
    sm = all_gather("gather_small", [pack([W[n][0]]) for n in small_sharded])

    bfw = lambda w: w.astype(BF)
    hA0, tok = gather_begin("0", [bfw(jnp.swapaxes(e_w_in[0], 0, 1))], sm[0])
    hA1, tok = gather_begin("1", [bfw(e_w_out[0]), bfw(f_w1[0]), bfw(f_w2[0])], tok)
    hA2, tok = gather_begin("2", [bfw(o_pw1[0]), bfw(o_pw2[0])], tok)
    hA3, tok = gather_begin("3", [bfw(f_w1[1]), bfw(f_w2[1])], tok)
    h0 = rowcall("rms_e", rms_tile, [x2d], [en_row], [(D, BF)], after=tok, tr=512)

    full = {}
    for n, g in zip(small_sharded, sm):
        shp = W[n][0].shape
        blocks = [unpack(g[d], [shp])[0] for d in range(N_DEV)]
        full[n] = jnp.concatenate(blocks, axis=-1)
    conv_w = full['e_conv_w']
    on_row, pw1_b_row = row(full['o_norm']), row(full['o_pw1_b'])
    dw_w, dw_b_row = full['o_dw'], row(full['o_dw_b'])
    oln_g_row, oln_b_row, pw2_b_row = row(full['o_ln_g']), row(full['o_ln_b']), row(full['o_pw2_b'])
    small_names = ['e_norm', 'e_conv_w', 'e_a_log', 'e_dt_bias', 'e_o_norm', 'e_ln_g', 'e_ln_b', 'e_w_s', 'e_b_s',
                   'o_norm', 'o_pw1_b', 'o_dw', 'o_dw_b', 'o_ln_g', 'o_ln_b', 'o_pw2_b', 'f_norm', 'final_norm']
    packed_wmv = [pack([A[n] for n in small_names], 256) for A in (W, Mo, Vo)]

    hB0, tok = gather_pass("0", hA0, [h0, conv_w, dw_w, pw1_b_row] + packed_wmv)
    (g_win,) = gather_end("0", hB0, tok)
    win_t = g_win.reshape(in_cols, D)
    wt_ba = jnp.pad(win_t[AQKV + AV:AQKV + AV + 2 * H], ((0, LANE - 2 * H), (0, 0)))
    wt_uv = win_t[AQKV + AV + 2 * H:]

    alog_row = jnp.pad(row(e_a_log), ((0, 0), (H, LANE - 2 * H)))
    dtb_row = jnp.pad(row(e_dt_bias), ((0, 0), (H, LANE - 2 * H)))
    eon_row = row(e_o_norm)
    eln_g_row, eln_b_row = row(e_ln_g), row(e_ln_b)
    w_s = e_w_s[0]
    bs_t = e_b_s[0].T
    fn_rows = [row(f_norm[l]) for l in range(L)]
    fin_row = row(final_norm)

    qkv_raw = matmul("proj_qkv", h0, win_t, "nt", b_window=(0, AQKV))
    z_gate = matmul("proj_z", h0, win_t, "nt", b_window=(AQKV, AV))
    ba = matmul("proj_ba", h0, wt_ba, "nt")
    uv = matmul("proj_uv", h0, wt_uv, "nt")

    cwa = min(512, AQKV)
    qkv_post = make_qkv_post(dk, cwa, 2 * AQK // cwa)
    ident = lambda t: t
    qkv = conv_fwd("qkv_conv", qkv_raw, conv_w, [], ident, qkv_post, [0], 1, KA, cw=cwa, tr=512)
    bgc_fn = make_bgc(H)
    bgc = rowcall("bgc", bgc_fn, [ba], [alog_row, dtb_row], [(LANE, F32)])
    o_dn, xinv, states = deltanet_fwd(qkv, bgc, H)
    hB1, tok = gather_pass("1", hA1, o_dn)

    def mix_tile(o, z, uv_, o_norm_, ln_g, ln_b, w_s_, bs_t_):
        return jnp.concatenate([mixa_post_tile(o, z, o_norm_), mixb_tile(uv_, ln_g, ln_b, w_s_, bs_t_)], axis=-1)
    mix_consts = [eon_row, eln_g_row, eln_b_row, w_s, bs_t]
    mix = rowcall("mix", mix_tile, [o_dn, z_gate, uv], mix_consts, [(AV + BW, BF)], after=tok)
    g_wout, g_w1_0, g_w2_0 = gather_end("1", hB1, mix)
    wout = g_wout.reshape(-1, D)
    w1 = [jnp.moveaxis(g_w1_0, 0, 1).reshape(D, -1), None]
    w2 = [g_w2_0.reshape(-1, D), None]
    add_epi = lambda acc, r: (acc + r,)
    x1 = matmul("out_proj", mix, wout, "nn", epi=add_epi, extras=[x2d], tm=512, tn=2048)

    def relu2_epi(acc):
        r = jnp.maximum(acc, 0.0)
        return r * r, r

    hB2, tok = gather_pass("2", hA2, x1)
    a2_0, ar_0, hf0 = norm_matmul("ffn_up0", x1, fn_rows[0], g_w1_0, epi=relu2_epi, out_dtypes=(BF, BF), after=tok)
    g_pw1, g_pw2 = gather_end("2", hB2, a2_0)
    x2 = matmul("ffn_down0", a2_0, w2[0], "nn", epi=add_epi, extras=[x1])
    ffn0 = (hf0, a2_0, ar_0)
    pw1 = jnp.moveaxis(g_pw1, 0, 1).reshape(D, 2 * D)
    pw2 = g_pw2.reshape(D, D)

    bias_epi = lambda acc, b: (acc + b,)
    zc, h1 = norm_matmul("pw1", x2, on_row, g_pw1, epi=bias_epi, extras=[pw1_b_row])
    hB3, tok = gather_pass("3", hA3, zc)
    cwc = min(512, D)
    ncb = D // cwc
    cconv = conv_fwd("dw_conv", zc, dw_w, [tie(dw_b_row, tok)], glu_pre, bias_post, [0, ncb], 1, KC, hb=32, cw=cwc,
                     tr=512)
    g_w1_1, g_w2_1 = gather_end("3", hB3, cconv)
    w1[1] = jnp.moveaxis(g_w1_1, 0, 1).reshape(D, -1)
    w2[1] = g_w2_1.reshape(-1, D)
    ln_silu = lambda c, g, b: jax.nn.silu(ln_tile(c, g, b))
    s_act = rowcall("ln_silu", ln_silu, [cconv], [oln_g_row, oln_b_row], [(D, BF)], tr=512)
    x3 = matmul("pw2", s_act, pw2, "nn", epi=lambda acc, r, b: (r + (acc + b),), extras=[x2, pw2_b_row],
                tm=512, tn=2048)
    a2_1, ar_1, hf1 = norm_matmul("ffn_up1", x3, fn_rows[1], g_w1_1, epi=relu2_epi, out_dtypes=(BF, BF))
    x4 = matmul("ffn_down1", a2_1, w2[1], "nn", epi=add_epi, extras=[x3])
    ffn1 = (hf1, a2_1, ar_1)

    def loss_bwd_tile(x_, t_, g_):
        l, vjp = jax.vjp(lambda a, b: loss_tile(a, b, t_), x_, g_)
        dx, dg = vjp(jnp.ones_like(l))
        return dx, dx, l, dg
    dx4, dx4_b, loss_part, d_final = rowcall("loss_head", loss_bwd_tile, [x4, tgt], [fin_row],
                                             [(D, F32), (D, BF)], [(1, 1), (1, D)], tr=512)
    loss = lax.psum(loss_part[0, 0], ("x", "y", "c"))

    def reduce_begin(tag, grads, after):
        na = len(grads)
        lands = [lax.empty((4,) + g.shape[1:], g.dtype) for g in grads]
        ss, rs, bufs, tok = split_start(f"reduce{tag}_d2d_start", grads + lands, plan_reduce_d2d(na), 4 * na, after)
        return (na, ss, rs, bufs), tok

    def reduce_mid(tag, h, after):
        na, ss, rs, bufs = h
        bufs, tok = split_wait(f"reduce{tag}_d2d_wait", ss, rs, bufs, plan_reduce_d2d(na), after)
        parts = []
        for a, (g, rc) in enumerate(zip(bufs[:na], bufs[na:])):
            r, c = g.shape[1], g.shape[2]
            if r % SUBLANE:
                parts.append(slabcall(f"chip_sum{tag}_{a}", lambda p, q: (p + q,),
                                      [(g, lambda i, s: 2 * i + s[0]), (rc, lambda i, s: i)], [BF], where, 4))
                continue
            mine = lambda i, n, s: (2 * (i // (n // 4)) + s[0]) * (n // 4) + i % (n // 4)
            parts.append(rowcall(f"chip_sum{tag}_{a}", lambda p, q: (p + q,),
                                 [(g.reshape(N_DEV * r, c), mine), rc.reshape(4 * r, c)], [],
                                 [(c, BF)], tr=_tile(r, 512), sp=where, R=4 * r).reshape(4, r, c))
        lands = [lax.empty((3,) + p.shape[1:], p.dtype) for p in parts]
        ss, rs, bufs, tok = split_start(f"reduce{tag}_ici_start", parts + lands, plan_reduce_ici(na), 3 * na, tok)
        return (na, ss, rs, bufs), tok

    res = {}

    def reduce_end(tag, h, after, targets):
        na, ss, rs, bufs = h
        bufs, _ = split_wait(f"reduce{tag}_ici_wait", ss, rs, bufs, plan_reduce_ici(na), after)
        for a, (part, fin, (n, l)) in enumerate(zip(bufs[:na], bufs[na:], targets)):
            def f(p0, p1, p2, p3, w_, m_, v_):
                g = ((p0.astype(F32) + p1.astype(F32)) + p2.astype(F32)) + p3.astype(F32)
                return (g,) + adamw_tile(w_, g, m_, v_)
            r, C = fin.shape[-2], fin.shape[-1]
            if W[n].shape[1:] == (C, r):
                t = lambda arr: jnp.swapaxes(arr[l], 0, 1)
                outs = slabcall(f"adam{tag}_{a}", f, [(part, lambda i, s: s[1]), (fin, 0), (fin, 1), (fin, 2),
                                                      (t(W[n]), None), (t(Mo[n]), None), (t(Vo[n]), None)],
                                [F32] * 4, where)
                res[n] = tuple(jnp.swapaxes(o, 0, 1)[None] for o in outs)
                continue
            own = lambda i, n_, s: s[1] * n_ + i
            res[n] = rowcall(f"adam{tag}_{a}", f, [(part.reshape(4 * r, C), own), (fin, 0), (fin, 1), (fin, 2),
                                                   (W[n], l), (Mo[n], l), (Vo[n], l)], [],
                             [(C, F32)] * 4, tr=256, sp=where, R=r,
                             out_lead=(W[n].shape[0], l), into=res.get(n))

    dscale_epi = lambda acc, r: (acc * (2.0 * r.astype(F32)),)
    d_fnorm = [None] * L

    dpre1 = matmul("ffn_down_dx1", dx4_b, w2[1], "nt", epi=dscale_epi, extras=[ar_1], out_dtypes=(BF,))
    dw2_1 = matmul("ffn_down_dw1", a2_1, dx4_b, "tn")
    dw1_1 = matmul("ffn_up_dw1", hf1, dpre1, "tn", colshard=True)
    hD1, tok = reduce_begin("1", [dw1_1, dw2_1.reshape(N_DEV, -1, D)], dpre1)
    dhf1 = matmul("ffn_up_dx1", dpre1, w1[1], "nt", after=tok, tk=4096)
    dx3, dx3_b, d_fnorm[1], d_pw2_b = rms_bwd("rms_f_bwd1", x3, fn_rows[1], dhf1, dx4)

    ds_act = matmul("pw2_dx", dx3_b, pw2, "nt")
    hI1, tok = reduce_mid("1", hD1, ds_act)
    d_pw2 = matmul("pw2_dw", s_act, dx3_b, "tn", after=tok)
    dcconv, d_oln_g, d_oln_b = rowcall(
        "ln_silu_bwd", _vjp_rows(ln_silu, 1, 1), [cconv, ds_act], [oln_g_row, oln_b_row],
        [(D, F32)], [(1, D), (1, D)], tr=512)
    (dza, dzb), (sza, szb), d_dw, (d_dw_b,) = conv_bwd(
        "dw_conv_bwd", zc, dw_w, [tie(dw_b_row, tok)], [dcconv], glu_pre, bias_post, [0, ncb], KC, hb=32, cw=cwc, tr=256,
        recompute=False)
    dzc = jnp.concatenate([dza, dzb], axis=-1)
    d_pw1_b = jnp.concatenate([sza, szb], axis=-1)
    d_pw1 = matmul("pw1_dw", h1, dzc, "tn", colshard=True)
    dh1 = matmul("pw1_dx", dzc, pw1, "nt", tk=4096)
    dx2, dx2_b, d_onorm, _ = rms_bwd("rms_o_bwd", x2, on_row, dh1, dx3)
    reduce_end("1", hI1, dx2, [('f_w1', 1), ('f_w2', 1)])

    hD2, tok = reduce_begin("2", [d_pw1, d_pw2.reshape(N_DEV, -1, D)], dx2)
    dpre0 = matmul("ffn_down_dx0", dx2_b, w2[0], "nt", epi=dscale_epi, extras=[ar_0], out_dtypes=(BF,), after=tok)
    dw2_0 = matmul("ffn_down_dw0", a2_0, dx2_b, "tn")
    hI2, tok = reduce_mid("2", hD2, dw2_0)
    dw1_0 = matmul("ffn_up_dw0", hf0, dpre0, "tn", colshard=True, after=tok)
    dhf0 = matmul("ffn_up_dx0", dpre0, w1[0], "nt", tk=4096)
    dx1, dx1_b, d_fnorm[0], _ = rms_bwd("rms_f_bwd0", x1, fn_rows[0], dhf0, dx2)
    reduce_end("2", hI2, dx1, [('o_pw1', 0), ('o_pw2', 0)])

    dmix = matmul("out_proj_dx", dx1_b, wout, "nt")
    d_wout = matmul("out_proj_dw", mix, dx1_b, "tn")
    hD3, tok = reduce_begin("3", [dw1_0, dw2_0.reshape(N_DEV, -1, D), d_wout.reshape(N_DEV, -1, D)], dmix)
    do_dn, dz_gate, duv, d_eon, d_eln_g, d_eln_b, d_ws, d_bs_t = rowcall(
        "mix_bwd", _vjp_rows(mix_tile, 3, 1), [o_dn, z_gate, uv, dmix], mix_consts,
        [(AV, F32), (AV, BF), (2 * BW, BF)], [(1, dv), (1, BW), (1, BW), w_s.shape, bs_t.shape], after=tok)
    dqkv, dbgc = deltanet_bwd(qkv, bgc, xinv, states, do_dn, H)
    hI3, tok = reduce_mid("3", hD3, dbgc)
    bgc_bwd = _vjp_rows(bgc_fn, 1, 1)
    dba, d_alog_row, d_dtb_row = rowcall(
        "bgc_bwd", bgc_bwd, [ba, dbgc], [alog_row, dtb_row], [(LANE, BF)], [(1, LANE), (1, LANE)])
    (dqkv_raw,), _, d_conv_w, _ = conv_bwd(
        "qkv_conv_bwd", qkv_raw, tie(conv_w, tok), [], [dqkv], ident, qkv_post, [0], KA, cw=cwa, tr=512)

    dw_qkv = matmul("proj_qkv_dw", dqkv_raw, h0, "tn")
    dw_z = matmul("proj_z_dw", dz_gate, h0, "tn")
    dw_ba = matmul("proj_ba_dw", dba, h0, "tn")
    dw_uv = matmul("proj_uv_dw", duv, h0, "tn")
    d_win_t = jnp.concatenate([dw_qkv, dw_z, dw_ba[:2 * H], dw_uv], axis=0)
    G_win = d_win_t.reshape(N_DEV, in_cols // N_DEV, D)
    hD4, tok = reduce_begin("4", [G_win], dw_uv)
    reduce_end("3", hI3, tok, [('f_w1', 0), ('f_w2', 0), ('e_w_out', 0)])

    d_alog = d_alog_row[:, H:2 * H]
    d_dtb = d_dtb_row[:, H:2 * H]
    early_grads = [d_conv_w, d_alog, d_dtb, d_eon, d_eln_g, d_eln_b, d_ws, d_bs_t.T,
                   d_onorm, d_pw1_b, d_dw, d_dw_b, d_oln_g, d_oln_b, d_pw2_b,
                   jnp.concatenate(d_fnorm, axis=0), d_final]
    early_packed = pack(early_grads, 256)
    ss_s, rs_s, bufs_s, tok_s = split_start(
        "small_grads_start", [early_packed, lax.empty((N_DEV,) + early_packed.shape, F32)],
        plan_gather_direct, N_DEV - 1, d_conv_w)
    hI4, tok = reduce_mid("4", hD4, [tok_s] + [res[n][0] for n in ('f_w1', 'f_w2', 'o_pw1', 'e_w_out')])
    dh0 = matmul_sum("proj_dx", [(dqkv_raw, (win_t, 0, AQKV)), (dz_gate, (win_t, AQKV, AV)),
                                (dba, wt_ba), (duv, wt_uv)], after=tok)
    grad_x, _, d_enorm, _ = rms_bwd("rms_e_bwd", x2d, en_row, dh0, dx1, after=tok)
    late_all = all_gather("gather_e_norm_grad", [pack([d_enorm])])[0]
    (early_shard, early_land), _ = split_wait("small_grads_wait", ss_s, rs_s, bufs_s, plan_gather_direct, late_all)
    early_all = place_own("small_grads_own", early_land, early_shard, dev_sp)

    def sum8(*ps):
        s = ps[0]
        for p in ps[1:]:
            s = s + p
        return (s,)
    gs_sum = rowcall("small_sum", sum8, [(early_all, d) for d in range(N_DEV)], [], [(LANE, F32)])
    late_sum = rowcall("e_norm_sum", sum8, [(late_all, d) for d in range(N_DEV)], [], [(LANE, F32)])
    reduce_end("4", hI4, gs_sum, [('e_w_in', 0)])
    g_full = dict(zip(small_names[1:], unpack(gs_sum, [g.shape for g in early_grads])))
    g_full['e_norm'] = unpack(late_sum, [d_enorm.shape])[0]
    g_loc = {}
    for n in small_names:
        g = g_full[n]
        if n in small_sharded:
            per = g.shape[-1] // N_DEV
            g = lax.dynamic_slice_in_dim(g, dev * per, per, axis=-1)
        g_loc[n] = g.reshape(W[n].shape)
    packs = [packed_wmv[0], pack([g_loc[n] for n in small_names], 256), packed_wmv[1], packed_wmv[2]]
    d_s, m_s, v_s = rowcall("adam_small", adamw_tile, packs, [], [(LANE, F32)] * 3)
    shapes = [W[n].shape for n in small_names]
    for n, d_, m_, v_ in zip(small_names, unpack(d_s, shapes), unpack(m_s, shapes), unpack(v_s, shapes)):
        res[n] = (g_loc[n], d_, m_, v_)

    grads = [res[n][0] for n in names]
    deltas = [res[n][1] for n in names]
    new_m = [res[n][2] for n in names]
    new_v = [res[n][3] for n in names]
    return (loss, grad_x.reshape(x.shape), *grads, *deltas, *new_m, *new_v)
```

```python
import functools
import math

import jax
import jax.numpy as jnp
import numpy as np
from jax import lax
from jax.experimental import pallas as pl
from jax.experimental.pallas import tpu as pltpu

F32 = jnp.float32
BF = jnp.bfloat16
EPS = 1e-6
CHUNK = 64
B_BLOCK = 128
LANE = 128
SUBLANE = 8
N_DEV = 8
VMEM_LIMIT = 56 * 1024 * 1024

ADAM_LR = 0.001
ADAM_B1 = 0.9
ADAM_B2 = 0.999
ADAM_EPS = 1e-08
ADAM_WD = 0.01
ADAM_STEP = 10

MESH = pl.DeviceIdType.MESH
ANY = pl.BlockSpec(memory_space=pl.ANY)


def _tile(n, pref, mult=SUBLANE):
    if n <= pref:
        return n
    t = (pref // mult) * mult
    while t >= mult:
        if n % t == 0:
            return t
        t -= mult
    return n


THREE_PASS = 3


def _dg(a, b, ca, cb, hi):
    nb = a.ndim - 2
    batch = tuple(range(nb))
    dims = (((ca + nb,), (cb + nb,)), (batch, batch))
    if hi == THREE_PASS:
        a, b = a.astype(F32), b.astype(F32)
        ah, bh = a.astype(BF), b.astype(BF)
        al, bl = (a - ah.astype(F32)).astype(BF), (b - bh.astype(F32)).astype(BF)
        dot = lambda p, q: lax.dot_general(p, q, dims, preferred_element_type=F32)
        return dot(ah, bh) + (dot(ah, bl) + dot(al, bh))
    if hi:
        return lax.dot_general(a.astype(F32), b.astype(F32), dims,
                               precision=lax.Precision.HIGHEST, preferred_element_type=F32)
    return lax.dot_general(a.astype(BF), b.astype(BF), dims, preferred_element_type=F32)


@functools.partial(jax.custom_vjp, nondiff_argnums=(2, 3, 4))
def mm(a, b, ca, cb, hi=False):
    return _dg(a, b, ca, cb, hi)


def _mm_fwd(a, b, ca, cb, hi):
    return _dg(a, b, ca, cb, hi), (a, b)


def _mm_bwd(ca, cb, hi, res, g):
    a, b = res
    if ca == 1:
        da = mm(g, b, 1, 1 - cb, hi)
    else:
        da = mm(b, g, 1 - cb, 1, hi)
    if cb == 0:
        db = mm(a, g, 1 - ca, 0, hi)
    else:
        db = mm(g, a, 0, 1 - ca, hi)
    return da.astype(a.dtype), db.astype(b.dtype)


mm.defvjp(_mm_fwd, _mm_bwd)


def matmul(name, a, b, mode, epi=None, extras=(), out_dtypes=(F32,), colshard=False,
           tm=None, tn=1024, tk=None, after=None, b_window=None):
    afters = [] if after is None else [after]
    slabs = b.ndim == 3
    if slabs:
        assert mode in ("nn", "nt") and not colshard, name
        b_rows, b_cols = b.shape[1], b.shape[0] * b.shape[2]
    else:
        b_rows, b_cols = b.shape
    if b_window is not None:
        assert mode == "nt" and not slabs, name
        b_start, b_rows = b_window
    if mode == "nn":
        (M, K), (K2, N) = a.shape, (b_rows, b_cols)
    elif mode == "nt":
        (M, K), (N, K2) = a.shape, (b_rows, b_cols)
    else:
        (K, M), (K2, N) = a.shape, b.shape
    assert K == K2, (name, a.shape, b.shape, mode)
    if tm is None:
        tm = 1024
    if tk is None:
        tk = 4096 if mode == "tn" else 2048
    tm = _tile(M, tm)
    tn = N // N_DEV if colshard else _tile(N, tn, LANE)
    tk = _tile(K, tk, LANE)
    if slabs and mode == "nn":
        tn = b.shape[2]
    if slabs and mode == "nt":
        tk = b.shape[2]
    nk = K // tk
    grid = (M // tm, N // tn, nk)
    if mode == "nn":
        a_spec = pl.BlockSpec((tm, tk), lambda i, j, k: (i, k))
        b_spec = (pl.BlockSpec((None, tk, tn), lambda i, j, k: (j, k, 0)) if slabs
                  else pl.BlockSpec((tk, tn), lambda i, j, k: (k, j)))
        ca, cb = 1, 0
    elif mode == "nt":
        a_spec = pl.BlockSpec((tm, tk), lambda i, j, k: (i, k))
        off = 0
        if b_window is not None:
            assert b_start % tn == 0, (name, b_start, tn)
            off = b_start // tn
        b_spec = (pl.BlockSpec((None, tn, tk), lambda i, j, k: (k, j, 0)) if slabs
                  else pl.BlockSpec((tn, tk), lambda i, j, k: (j + off, k)))
        ca, cb = 1, 1
    else:
        a_spec = pl.BlockSpec((tk, tm), lambda i, j, k: (k, i))
        b_spec = pl.BlockSpec((tk, tn), lambda i, j, k: (k, j))
        ca, cb = 0, 0
    ex_specs = []
    for e in extras:
        if e.shape[0] == 1:
            ex_specs.append(pl.BlockSpec((1, tn), lambda i, j, k: (0, j)))
        else:
            assert e.shape == (M, N), (name, e.shape)
            ex_specs.append(pl.BlockSpec((tm, tn), lambda i, j, k: (i, j)))
    if colshard:
        out_shape = [jax.ShapeDtypeStruct((N_DEV, M, tn), dt) for dt in out_dtypes]
        out_specs = [pl.BlockSpec((None, tm, tn), lambda i, j, k: (j, i, 0)) for _ in out_dtypes]
    else:
        out_shape = [jax.ShapeDtypeStruct((M, N), dt) for dt in out_dtypes]
        out_specs = [pl.BlockSpec((tm, tn), lambda i, j, k: (i, j)) for _ in out_dtypes]
    n_ex, n_out = len(extras), len(out_dtypes)

    def body(*refs):
        a_ref, b_ref = refs[0], refs[1]
        ex_refs = refs[2:2 + n_ex]
        first_out = 2 + n_ex + len(afters)
        o_refs = refs[first_out:first_out + n_out]
        part = _dg(a_ref[...], b_ref[...], ca, cb, False)

        def finish(acc):
            res = (acc,) if epi is None else epi(acc, *[r[...] for r in ex_refs])
            for o_ref, r in zip(o_refs, res):
                o_ref[...] = r.astype(o_ref.dtype)

        if nk == 1:
            finish(part)
            return
        acc_ref = refs[-1]
        k = pl.program_id(2)

        @pl.when(k == 0)
        def _():
            acc_ref[...] = part

        @pl.when(k > 0)
        def _():
            acc_ref[...] += part

        @pl.when(k == nk - 1)
        def _():
            finish(acc_ref[...])

    outs = pl.pallas_call(
        body, name=name, grid=grid,
        in_specs=[a_spec, b_spec] + ex_specs + [ANY] * len(afters),
        out_specs=out_specs, out_shape=out_shape,
        scratch_shapes=[pltpu.VMEM((tm, tn), F32)] if nk > 1 else [],
        compiler_params=pltpu.CompilerParams(
            dimension_semantics=("parallel", "parallel", "arbitrary"),
            vmem_limit_bytes=VMEM_LIMIT),
    )(a, b, *extras, *afters)
    return outs[0] if n_out == 1 else tuple(outs)


def norm_matmul(name, x, g, b, epi=None, extras=(), out_dtypes=(F32,), tm=1024, tn=1024, after=None):
    M, K = x.shape
    slabs = b.ndim == 3
    N = b.shape[0] * b.shape[2] if slabs else b.shape[1]
    tm, tn = _tile(M, tm), (b.shape[2] if slabs else _tile(N, tn, LANE))
    b_spec = (pl.BlockSpec((None, K, tn), lambda i, j: (j, 0, 0)) if slabs
              else pl.BlockSpec((K, tn), lambda i, j: (0, j)))
    afters = [] if after is None else [after]
    ex_specs = [pl.BlockSpec((1, tn), lambda i, j: (0, j)) if e.shape[0] == 1
                else pl.BlockSpec((tm, tn), lambda i, j: (i, j)) for e in extras]
    n_ex, n_out = len(extras), len(out_dtypes)

    def body(*refs):
        x_ref, g_ref, b_ref = refs[:3]
        ex_refs = refs[3:3 + n_ex]
        first_out = 3 + n_ex + len(afters)
        o_refs = refs[first_out:first_out + n_out]
        h_ref, h_scr = refs[first_out + n_out], refs[-1]

        @pl.when(pl.program_id(1) == 0)
        def _():
            h = rms_tile(x_ref[...], g_ref[...]).astype(h_scr.dtype)
            h_scr[...] = h
            h_ref[...] = h

        acc = _dg(h_scr[...], b_ref[...], 1, 0, False)
        res = (acc,) if epi is None else epi(acc, *[r[...] for r in ex_refs])
        for o_ref, r in zip(o_refs, res):
            o_ref[...] = r.astype(o_ref.dtype)

    outs = pl.pallas_call(
        body, name=name, grid=(M // tm, N // tn),
        in_specs=[pl.BlockSpec((tm, K), lambda i, j: (i, 0)), pl.BlockSpec((1, K), lambda i, j: (0, 0)),
                  b_spec] + ex_specs + [ANY] * len(afters),
        out_specs=[pl.BlockSpec((tm, tn), lambda i, j: (i, j)) for _ in out_dtypes]
        + [pl.BlockSpec((tm, K), lambda i, j: (i, 0))],
        out_shape=[jax.ShapeDtypeStruct((M, N), dt) for dt in out_dtypes] + [jax.ShapeDtypeStruct((M, K), BF)],
        scratch_shapes=[pltpu.VMEM((tm, K), BF)],
        compiler_params=pltpu.CompilerParams(
            dimension_semantics=("parallel", "arbitrary"), vmem_limit_bytes=VMEM_LIMIT),
    )(x, g, b, *extras, *afters)
    return tuple(outs)


def matmul_sum(name, pairs, tm=512, tn=1024, after=None):
    pairs = [(a, b if isinstance(b, tuple) else (b, 0, b.shape[0])) for a, b in pairs]
    M, N = pairs[0][0].shape[0], pairs[0][1][0].shape[1]
    tm, tn = _tile(M, tm), _tile(N, tn, LANE)
    afters = [] if after is None else [after]
    in_specs, operands = [], []
    for a, (b, start, size) in pairs:
        assert a.shape == (M, size) and b.shape[1] == N and start % size == 0, (name, a.shape, b.shape, start)
        in_specs += [pl.BlockSpec((tm, size), lambda i, j: (i, 0)),
                     pl.BlockSpec((size, tn), lambda i, j, blk=start // size: (blk, j))]
        operands += [a, b]
    n = len(pairs)

    def body(*refs):
        acc = _dg(refs[0][...], refs[1][...], 1, 0, False)
        for p in range(1, n):
            acc = acc + _dg(refs[2 * p][...], refs[2 * p + 1][...], 1, 0, False)
        refs[-1][...] = acc

    return pl.pallas_call(
        body, name=name, grid=(M // tm, N // tn),
        in_specs=in_specs + [ANY] * len(afters),
        out_specs=pl.BlockSpec((tm, tn), lambda i, j: (i, j)),
        out_shape=jax.ShapeDtypeStruct((M, N), F32),
        compiler_params=pltpu.CompilerParams(
            dimension_semantics=("parallel", "parallel"), vmem_limit_bytes=VMEM_LIMIT),
    )(*operands, *afters)


def rowcall(name, fn, rows, consts, out_rows, out_accs=(), tr=256, sp=None, R=None, after=None,
            out_lead=None, into=None):
    afters = ([] if after is None else [after]) + ([] if into is None else list(into))
    rows = [r if isinstance(r, tuple) else (r, None) for r in rows]
    R = rows[0][0].shape[-2] if R is None else R
    tr = _tile(R, tr)
    n = R // tr
    in_specs = []
    for arr, lead in rows:
        C = arr.shape[-1]
        if lead is None:
            assert arr.shape[-2] == R, (name, arr.shape, R)
            in_specs.append(pl.BlockSpec((tr, C), lambda i, *s: (i, 0)))
        elif callable(lead):
            in_specs.append(pl.BlockSpec((tr, C), lambda i, *s, lead=lead: (lead(i, n, *s), 0)))
        else:
            assert arr.shape[-2] == R, (name, arr.shape, R)
            in_specs.append(pl.BlockSpec((None, tr, C), lambda i, *s, lead=lead: (lead, i, 0)))
    for c in consts:
        in_specs.append(pl.BlockSpec(c.shape, lambda i, *s, nd=c.ndim: (0,) * nd))
    if out_lead is None:
        out_shape = [jax.ShapeDtypeStruct((R, C), dt) for C, dt in out_rows]
        out_specs = [pl.BlockSpec((tr, C), lambda i, *s: (i, 0)) for C, _ in out_rows]
    else:
        n_slab, slab = out_lead
        out_shape = [jax.ShapeDtypeStruct((n_slab, R, C), dt) for C, dt in out_rows]
        out_specs = [pl.BlockSpec((None, tr, C), lambda i, *s: (slab, i, 0)) for C, _ in out_rows]
    for shp in out_accs:
        out_shape.append(jax.ShapeDtypeStruct(shp, F32))
        out_specs.append(pl.BlockSpec(shp, lambda i, *s, nd=len(shp): (0,) * nd))
    n_in, n_row, n_acc = len(rows) + len(consts), len(out_rows), len(out_accs)
    n_sp = 0 if sp is None else 1

    def body(*refs):
        refs = refs[n_sp:]
        ins = [r[...] for r in refs[:n_in]]
        res = fn(*ins)
        if not isinstance(res, (tuple, list)):
            res = (res,)
        o_refs = refs[n_in + len(afters):]
        for o_ref, r in zip(o_refs[:n_row], res[:n_row]):
            o_ref[...] = r.astype(o_ref.dtype)
        if n_acc:
            first = pl.program_id(0) == 0
            for o_ref, r in zip(o_refs[n_row:], res[n_row:]):
                r = r.astype(F32).reshape(o_ref.shape)

                @pl.when(first)
                def _(o_ref=o_ref, r=r):
                    o_ref[...] = r

                @pl.when(jnp.logical_not(first))
                def _(o_ref=o_ref, r=r):
                    o_ref[...] += r

    params = pltpu.CompilerParams(dimension_semantics=("arbitrary",), vmem_limit_bytes=VMEM_LIMIT)
    operands = [a for a, _ in rows] + list(consts) + afters
    in_specs = in_specs + [ANY] * len(afters)
    aliases = {} if into is None else {n_sp + len(operands) - len(into) + k: k for k in range(len(into))}
    if sp is None:
        outs = pl.pallas_call(body, name=name, grid=(n,), in_specs=in_specs, out_specs=out_specs,
                              out_shape=out_shape, input_output_aliases=aliases,
                              compiler_params=params)(*operands)
    else:
        outs = pl.pallas_call(
            body, name=name, out_shape=out_shape, compiler_params=params, input_output_aliases=aliases,
            grid_spec=pltpu.PrefetchScalarGridSpec(
                num_scalar_prefetch=1, grid=(n,), in_specs=in_specs, out_specs=out_specs),
        )(sp, *operands)
    return outs[0] if len(outs) == 1 else tuple(outs)


def slabcall(name, fn, ins, out_dtypes, sp, n_out_slabs=None, cw=512):
    R, C = ins[0][0].shape[-2:]
    cw = _tile(C, cw, LANE)
    in_specs = []
    for arr, slab in ins:
        assert arr.shape[-2:] == (R, C), (name, arr.shape)
        if slab is None:
            in_specs.append(pl.BlockSpec((R, cw), lambda i, j, s: (0, j)))
        elif callable(slab):
            in_specs.append(pl.BlockSpec((None, R, cw), lambda i, j, s, slab=slab: (slab(i, s), 0, j)))
        else:
            in_specs.append(pl.BlockSpec((None, R, cw), lambda i, j, s, slab=slab: (slab, 0, j)))
    if n_out_slabs is None:
        out_shape = [jax.ShapeDtypeStruct((R, C), dt) for dt in out_dtypes]
        out_specs = [pl.BlockSpec((R, cw), lambda i, j, s: (0, j)) for _ in out_dtypes]
    else:
        out_shape = [jax.ShapeDtypeStruct((n_out_slabs, R, C), dt) for dt in out_dtypes]
        out_specs = [pl.BlockSpec((None, R, cw), lambda i, j, s: (i, 0, j)) for _ in out_dtypes]
    n_in = len(ins)

    def body(sp_ref, *refs):
        res = fn(*[r[...] for r in refs[:n_in]])
        for o_ref, r in zip(refs[n_in:], res):
            o_ref[...] = r.astype(o_ref.dtype)

    outs = pl.pallas_call(
        body, name=name, out_shape=out_shape,
        grid_spec=pltpu.PrefetchScalarGridSpec(
            num_scalar_prefetch=1, grid=(n_out_slabs or 1, C // cw), in_specs=in_specs, out_specs=out_specs),
        compiler_params=pltpu.CompilerParams(
            dimension_semantics=("arbitrary", "arbitrary"), vmem_limit_bytes=VMEM_LIMIT),
    )(sp, *[a for a, _ in ins])
    return outs[0] if len(outs) == 1 else tuple(outs)


def rms_tile(x, g):
    x = x.astype(F32)
    return x * lax.rsqrt(jnp.mean(x * x, axis=-1, keepdims=True) + EPS) * g


def gelu(x):
    return 0.5 * x * (1.0 + lax.erf(x * (1.0 / math.sqrt(2.0))))


def ln_tile(x, g, b):
    mu = jnp.mean(x, axis=-1, keepdims=True)
    xc = x - mu
    return xc * lax.rsqrt(jnp.mean(xc * xc, axis=-1, keepdims=True) + EPS) * g + b


def lane_groups(fn, width, *arrs):
    n = arrs[0].shape[-1] // width
    outs = [fn(*[a[:, i * width:(i + 1) * width] for a in arrs]) for i in range(n)]
    return jnp.concatenate(outs, axis=-1)


def mixa_post_tile(o, z, o_norm):
    dv = o_norm.shape[-1]
    on = lane_groups(lambda t: rms_tile(t, o_norm), dv, o)
    return on * jax.nn.silu(z)


def mixb_tile(uv, ln_g, ln_b, w_s, bs_t):
    G = w_s.shape[0]
    gw = ln_g.shape[-1]
    dg = gw // G
    tr = uv.shape[0]
    u = gelu(uv[:, :gw])
    vg = gelu(uv[:, gw:])
    ii = lax.broadcasted_iota(jnp.int32, (B_BLOCK, B_BLOCK), 0)
    jj = lax.broadcasted_iota(jnp.int32, (B_BLOCK, B_BLOCK), 1)
    mask = (jj // CHUNK) <= (ii // CHUNK)
    cols = []
    for g in range(G):
        sl = slice(g * dg, (g + 1) * dg)
        vn = ln_tile(vg[:, sl], ln_g[:, sl], ln_b[:, sl])
        wm = jnp.where(mask, w_s[g], 0.0)
        blocks = []
        for m in range(tr // B_BLOCK):
            blk = vn[m * B_BLOCK:(m + 1) * B_BLOCK, :]
            blocks.append(mm(wm, blk, 1, 0) + bs_t[:, g:g + 1])
        mixed = blocks[0] if len(blocks) == 1 else jnp.concatenate(blocks, axis=0)
        cols.append(u[:, sl] * mixed)
    return jnp.concatenate(cols, axis=-1)


def bgc_tile(ba, alog_row, dtb_row):
    tr = ba.shape[0]
    beta = jax.nn.sigmoid(ba)
    g = -jnp.exp(alog_row) * jax.nn.softplus(ba + dtb_row)
    ii = lax.broadcasted_iota(jnp.int32, (tr, tr), 0)
    jj = lax.broadcasted_iota(jnp.int32, (tr, tr), 1)
    tri = jnp.where((ii // CHUNK == jj // CHUNK) & (jj <= ii), 1.0, 0.0).astype(F32)
    gc = mm(tri, g, 1, 0, True)
    return beta, gc


def make_bgc(H):
    def f(ba, alog_row, dtb_row):
        beta, gc = bgc_tile(ba, alog_row, dtb_row)
        lane = lax.broadcasted_iota(jnp.int32, ba.shape, 1)
        return jnp.where(lane < H, beta, jnp.where(lane < 2 * H, gc, 0.0))
    return f


def loss_tile(x, g, target):
    y = rms_tile(x, g)
    err = y - target
    return 0.5 * jnp.sum(jnp.mean(err * err, axis=-1, keepdims=True), axis=0, keepdims=True)


def adamw_tile(w, g, m, v):
    m = ADAM_B1 * m + (1.0 - ADAM_B1) * g
    v = ADAM_B2 * v + (1.0 - ADAM_B2) * (g * g)
    m_hat = m / (1.0 - ADAM_B1 ** ADAM_STEP)
    v_hat = v / (1.0 - ADAM_B2 ** ADAM_STEP)
    delta = -ADAM_LR * (m_hat / (jnp.sqrt(v_hat) + ADAM_EPS) + ADAM_WD * w)
    return delta, m, v


CONV_ROWS = 32


def _shifted_copies(src, sh, rows):
    for b in range(SUBLANE):
        sh[b] = src[pl.ds(b, rows), :]


def _window(sh, off, rows):
    b = off % SUBLANE
    return sh[b, pl.ds(off - b, rows), :]


def _conv_rows(out, sh, w_ref, offsets, rows):
    for r0 in range(0, rows, CONV_ROWS):
        rc = min(CONV_ROWS, rows - r0)
        acc = w_ref[0:1, :] * _window(sh, offsets[0] + r0, rc)
        for k in range(1, len(offsets)):
            acc = acc + w_ref[k:k + 1, :] * _window(sh, offsets[k] + r0, rc)
        out[r0:r0 + rc, :] = acc


def _conv_wgrad(dsrc, d0, sh, offsets, rows):
    dws = []
    for off in offsets:
        acc = None
        for r0 in range(0, rows, CONV_ROWS):
            rc = min(CONV_ROWS, rows - r0)
            prod = dsrc[d0 + r0:d0 + r0 + rc, :] * _window(sh, off + r0, rc)
            for g in range(0, rc, SUBLANE):
                part = prod[g:g + SUBLANE, :]
                acc = part if acc is None else acc + part
        dws.append(jnp.sum(acc, axis=0, keepdims=True))
    return jnp.concatenate(dws, axis=0)


def _conv_specs(T, tr, hb, cw, col_blocks, rev):
    n = T // tr

    def ri(i):
        return (n - 1 - i) if rev else i

    tile_specs, halo_specs = [], []
    for off in col_blocks:
        tile_specs.append(pl.BlockSpec((tr, cw), lambda j, i, off=off: (ri(i), j + off)))
        halo_specs.append(pl.BlockSpec(
            (hb, cw), lambda j, i, off=off: (jnp.maximum(ri(i) * (tr // hb) - 1, 0), j + off)))
    return n, ri, tile_specs, halo_specs


def conv_fwd(name, x, w, consts, pre, post, col_blocks, n_out, K, out_dtype=F32, tr=256, hb=8, cw=512):
    T = x.shape[0]
    C = w.shape[1]
    tr, cw = _tile(T, tr, hb), min(cw, C)
    nb = len(col_blocks)
    n, ri, tile_specs, halo_specs = _conv_specs(T, tr, hb, cw, col_blocks, False)
    w_spec = pl.BlockSpec((K, cw), lambda j, i: (0, j))
    c_specs = [pl.BlockSpec((1, cw), lambda j, i: (0, j)) for _ in consts]

    def body(*refs):
        tiles = [r[...] for r in refs[:nb]]
        halos = [r[...] for r in refs[nb:2 * nb]]
        w_ref = refs[2 * nb]
        cs = [r[...] for r in refs[2 * nb + 1:2 * nb + 1 + len(consts)]]
        o_refs = refs[2 * nb + 1 + len(consts):-3]
        pbuf, shp, cbuf = refs[-3:]
        i = pl.program_id(1)
        pbuf[0:hb, :] = jnp.where(i > 0, pre(*halos), 0.0)
        pbuf[hb:hb + tr, :] = pre(*tiles)
        pbuf[hb + tr:hb + tr + SUBLANE, :] = jnp.zeros((SUBLANE, cw), F32)
        _shifted_copies(pbuf, shp, hb + tr)
        _conv_rows(cbuf, shp, w_ref, [hb - (K - 1) + k for k in range(K)], tr)
        res = post(cbuf[...], pl.program_id(0), *cs)
        for o_ref, r in zip(o_refs, res):
            o_ref[...] = r.astype(o_ref.dtype)

    outs = pl.pallas_call(
        body, name=name, grid=(C // cw, n),
        in_specs=tile_specs + halo_specs + [w_spec] + c_specs,
        out_specs=[pl.BlockSpec((tr, cw), lambda j, i: (i, j)) for _ in range(n_out)],
        out_shape=[jax.ShapeDtypeStruct((T, C), out_dtype) for _ in range(n_out)],
        scratch_shapes=[pltpu.VMEM((hb + tr + SUBLANE, cw), F32), pltpu.VMEM((SUBLANE, hb + tr, cw), F32),
                        pltpu.VMEM((tr, cw), F32)],
        compiler_params=pltpu.CompilerParams(
            dimension_semantics=("parallel", "arbitrary"), vmem_limit_bytes=VMEM_LIMIT),
    )(*([x] * nb), *([x] * nb), w, *consts)
    return outs[0] if n_out == 1 else tuple(outs)


def conv_bwd(name, x, w, consts, grads, pre, post, col_blocks, K, tr=256, hb=8, cw=512, recompute=True):
    T, Cx = x.shape
    C = w.shape[1]
    tr, cw = _tile(T, tr, hb), min(cw, C)
    nb = len(col_blocks)
    n, ri, tile_specs, halo_specs = _conv_specs(T, tr, hb, cw, col_blocks, True)
    w_spec = pl.BlockSpec((K, cw), lambda j, i: (0, j))
    c_specs = [pl.BlockSpec((1, cw), lambda j, i: (0, j)) for _ in consts]
    g_specs = [pl.BlockSpec((tr, cw), lambda j, i: (ri(i), j)) for _ in grads]
    nc, ng = len(consts), len(grads)

    def body(*refs):
        p = 0
        tile_refs = refs[p:p + nb]; p += nb
        halo_refs = refs[p:p + nb]; p += nb
        w_ref = refs[p]; p += 1
        cs = [r[...] for r in refs[p:p + nc]]; p += nc
        gs = [r[...] for r in refs[p:p + ng]]; p += ng
        dx_refs = refs[p:p + nb]; p += nb
        sum_refs = refs[p:p + nb]; p += nb
        dw_ref = refs[p]; p += 1
        dc_refs = refs[p:p + nc]; p += nc
        pbuf, dbuf, ebuf, carry, shp, shd, cbuf = refs[p:p + 7]
        i = pl.program_id(1)
        first = i == 0
        tiles = [r[...] for r in tile_refs]
        halos = [r[...] for r in halo_refs]
        p_tile, vjp_pre = jax.vjp(pre, *tiles)
        pbuf[0:hb, :] = jnp.where(ri(i) > 0, pre(*halos), 0.0)
        pbuf[hb:hb + tr, :] = p_tile
        pbuf[hb + tr:hb + tr + SUBLANE, :] = jnp.zeros((SUBLANE, cw), F32)
        _shifted_copies(pbuf, shp, hb + tr)
        taps = [hb - (K - 1) + k for k in range(K)]
        if recompute:
            _conv_rows(cbuf, shp, w_ref, taps, tr)
            c = cbuf[...]
        else:
            c = jnp.zeros((tr, cw), F32)
        cid = pl.program_id(0)
        _, vjp_post = jax.vjp(lambda c_, *cs_: post(c_, cid, *cs_), c, *cs)
        dres = vjp_post(tuple(g.astype(F32) for g in gs))
        dbuf[0:hb, :] = jnp.zeros((hb, cw), F32)
        dbuf[hb:hb + tr, :] = dres[0]
        dbuf[hb + tr:hb + tr + hb + SUBLANE, :] = jnp.zeros((hb + SUBLANE, cw), F32)
        _shifted_copies(dbuf, shd, hb + tr + hb)
        _conv_rows(ebuf, shd, w_ref, [K - 1 - k for k in range(K)], hb + tr)
        dw = _conv_wgrad(dbuf, hb, shp, taps, tr)

        @pl.when(jnp.logical_not(first))
        def _():
            ebuf[tr:tr + hb, :] += carry[...]

        carry[...] = ebuf[0:hb, :]
        dtiles = vjp_pre(ebuf[hb:hb + tr, :])
        for r, s, d in zip(dx_refs, sum_refs, dtiles):
            r[...] = d.astype(r.dtype)
            ds_ = jnp.sum(d, axis=0, keepdims=True)

            @pl.when(first)
            def _(s=s, ds_=ds_):
                s[...] = ds_

            @pl.when(jnp.logical_not(first))
            def _(s=s, ds_=ds_):
                s[...] += ds_

        accs = [(dw_ref, dw)] + [(r, d) for r, d in zip(dc_refs, dres[1:])]
        for r, d in accs:
            @pl.when(first)
            def _(r=r, d=d):
                r[...] = d

            @pl.when(jnp.logical_not(first))
            def _(r=r, d=d):
                r[...] += d

    n_cb = C // cw
    outs = pl.pallas_call(
        body, name=name, grid=(n_cb, n),
        in_specs=tile_specs + halo_specs + [w_spec] + c_specs + g_specs,
        out_specs=([pl.BlockSpec((tr, cw), lambda j, i: (ri(i), j)) for _ in col_blocks]
                   + [pl.BlockSpec((1, cw), lambda j, i: (0, j)) for _ in col_blocks]
                   + [pl.BlockSpec((K, cw), lambda j, i: (0, j))]
                   + [pl.BlockSpec((1, cw), lambda j, i: (0, j)) for _ in consts]),
        out_shape=([jax.ShapeDtypeStruct((T, C), BF) for _ in col_blocks]
                   + [jax.ShapeDtypeStruct((1, C), F32) for _ in col_blocks]
                   + [jax.ShapeDtypeStruct((K, C), F32)]
                   + [jax.ShapeDtypeStruct((1, C), F32) for _ in consts]),
        scratch_shapes=[pltpu.VMEM((hb + tr + SUBLANE, cw), F32), pltpu.VMEM((hb + tr + hb + SUBLANE, cw), F32),
                        pltpu.VMEM((hb + tr, cw), F32), pltpu.VMEM((hb, cw), F32),
                        pltpu.VMEM((SUBLANE, hb + tr, cw), F32), pltpu.VMEM((SUBLANE, hb + tr + hb, cw), F32),
                        pltpu.VMEM((tr, cw), F32)],
        compiler_params=pltpu.CompilerParams(
            dimension_semantics=("parallel", "arbitrary"), vmem_limit_bytes=VMEM_LIMIT),
    )(*([x] * nb), *([x] * nb), w, *consts, *grads)
    dxs = outs[:nb]
    sums = outs[nb:2 * nb]
    dw = outs[2 * nb]
    dcs = outs[2 * nb + 1:]
    return dxs, sums, dw, dcs


def make_qkv_post(dk, cw, n_qk_chunks):
    def l2(t):
        return t * lax.rsqrt(jnp.sum(t * t, axis=-1, keepdims=True) + EPS)

    def post(c, cid):
        s = jax.nn.silu(c)
        normed = lane_groups(l2, dk, s)
        return (jnp.where(cid < n_qk_chunks, normed, s),)
    return post


def glu_pre(za, zb):
    return za * jax.nn.sigmoid(zb)


def bias_post(c, cid, b):
    return (c + b,)


def _col_to_row(col):
    C = col.shape[-2]
    ii = lax.broadcasted_iota(jnp.int32, (C, C), 0)
    jj = lax.broadcasted_iota(jnp.int32, (C, C), 1)
    wide = jnp.broadcast_to(col, col.shape[:-1] + (C,))
    return jnp.sum(jnp.where(ii == jj, wide, 0.0), axis=-2, keepdims=True)


@jax.custom_vjp
def solve_with_inverse(a, rhs, x):
    return mm(x, rhs, 1, 0, THREE_PASS)


def _swi_fwd(a, rhs, x):
    sol = mm(x, rhs, 1, 0, THREE_PASS)
    return sol, (x, sol)


def _swi_bwd(res, dsol):
    x, sol = res
    drhs = mm(x, dsol, 0, 0, THREE_PASS)
    da = -mm(drhs, sol, 1, 1, THREE_PASS)
    return da, drhs, jnp.zeros_like(x)


solve_with_inverse.defvjp(_swi_fwd, _swi_bwd)


def unit_lower_inverse(a):
    C = a.shape[-1]
    ii = lax.broadcasted_iota(jnp.int32, (C, C), 0)
    jj = lax.broadcasted_iota(jnp.int32, (C, C), 1)
    x = jnp.where(ii == jj, 1.0, 0.0).astype(F32) - a
    p = mm(a, a, 1, 0, THREE_PASS)
    steps = int(math.log2(C)) - 1
    for s in range(steps):
        x = x + mm(x, p, 1, 0, THREE_PASS)
        if s < steps - 1:
            p = mm(p, p, 1, 0, THREE_PASS)
    return x


def dn_masks(C):
    ii = lax.broadcasted_iota(jnp.int32, (C, C), 0)
    jj = lax.broadcasted_iota(jnp.int32, (C, C), 1)
    return ii >= jj, ii > jj


def dn_pre(q, k, v, beta, gc):
    C, dk = q.shape[-2:]
    tri, strict = dn_masks(C)
    q = q * (dk ** -0.5)
    diff = gc - _col_to_row(gc)
    decay = jnp.where(tri, jnp.exp(jnp.where(tri, diff, 0.0)), 0.0)
    kb = k * beta
    vb = v * beta
    a = jnp.where(strict, mm(kb, k, 1, 1) * decay, 0.0)
    eg = jnp.exp(gc)
    rhs = jnp.concatenate([vb, kb * eg], axis=-1)
    attn = mm(q, k, 1, 1) * decay
    qd = q * eg
    g_last = gc[..., C - 1:C, :]
    kt = k * jnp.exp(g_last - gc)
    gl = jnp.exp(g_last)
    return a, rhs, attn, qd, kt, gl


def dn_chunk(q, k, v, beta, gc, state, x):
    dv = v.shape[-1]
    a, rhs, attn, qd, kt, gl = dn_pre(q, k, v, beta, gc)
    sol = solve_with_inverse(a, rhs, x)
    u, w = sol[..., :dv], sol[..., dv:]
    vn = u - mm(w, state, 1, 0)
    o = mm(qd, state, 1, 0) + mm(attn, vn, 1, 0)
    new_state = state * gl + mm(kt, vn, 0, 0)
    return o, new_state


def _by_head(q_ref, k_ref, v_ref, bg, H, dk, dv):
    qv = jnp.stack([q_ref[:, h * dk:(h + 1) * dk] for h in range(H)])
    kv = jnp.stack([k_ref[:, h * dk:(h + 1) * dk] for h in range(H)])
    vv = jnp.stack([v_ref[:, h * dv:(h + 1) * dv] for h in range(H)])
    beta = jnp.stack([bg[:, h:h + 1] for h in range(H)])
    gc = jnp.stack([bg[:, H + h:H + h + 1] for h in range(H)])
    return qv, kv, vv, beta, gc


def deltanet_fwd(qkv, bgc, H):
    T = qkv.shape[0]
    dk = dv = qkv.shape[1] // (3 * H)
    N = T // CHUNK

    def body(q_ref, k_ref, v_ref, bgc_ref, o_ref, x_ref, s_ref, state):
        @pl.when(pl.program_id(0) == 0)
        def _():
            state[...] = jnp.zeros((H, dk, dv), F32)

        qv, kv, vv, beta, gc = _by_head(q_ref, k_ref, v_ref, bgc_ref[...], H, dk, dv)
        a = dn_pre(qv, kv, vv, beta, gc)[0]
        x = unit_lower_inverse(a)
        s = state[...]
        o, s_new = dn_chunk(qv, kv, vv, beta, gc, s, x)
        for h in range(H):
            o_ref[:, h * dv:(h + 1) * dv] = o[h]
        x_ref[...] = x
        s_ref[...] = s
        state[...] = s_new

    return pl.pallas_call(
        body, name="deltanet_fwd", grid=(N,),
        in_specs=[pl.BlockSpec((CHUNK, H * dk), lambda n: (n, 0)),
                  pl.BlockSpec((CHUNK, H * dk), lambda n: (n, 1)),
                  pl.BlockSpec((CHUNK, H * dv), lambda n: (n, 2)),
                  pl.BlockSpec((CHUNK, LANE), lambda n: (n, 0))],
        out_specs=[pl.BlockSpec((CHUNK, H * dv), lambda n: (n, 0)),
                   pl.BlockSpec((None, H, CHUNK, CHUNK), lambda n: (n, 0, 0, 0)),
                   pl.BlockSpec((None, H, dk, dv), lambda n: (n, 0, 0, 0))],
        out_shape=[jax.ShapeDtypeStruct((T, H * dv), F32),
                   jax.ShapeDtypeStruct((N, H, CHUNK, CHUNK), F32),
                   jax.ShapeDtypeStruct((N, H, dk, dv), F32)],
        scratch_shapes=[pltpu.VMEM((H, dk, dv), F32)],
        compiler_params=pltpu.CompilerParams(
            dimension_semantics=("arbitrary",), vmem_limit_bytes=VMEM_LIMIT),
    )(qkv, qkv, qkv, bgc)


def deltanet_bwd(qkv, bgc, xinv, states, do, H):
    T = qkv.shape[0]
    dk = dv = qkv.shape[1] // (3 * H)
    N = T // CHUNK

    def body(q_ref, k_ref, v_ref, bgc_ref, x_ref, s_ref, do_ref, dqkv_ref, dbgc_ref, dstate):
        @pl.when(pl.program_id(0) == 0)
        def _():
            dstate[...] = jnp.zeros((H, dk, dv), F32)

        qv, kv, vv, beta, gc = _by_head(q_ref, k_ref, v_ref, bgc_ref[...], H, dk, dv)
        do = jnp.stack([do_ref[:, h * dv:(h + 1) * dv] for h in range(H)])
        _, vjp = jax.vjp(dn_chunk, qv, kv, vv, beta, gc, s_ref[...], x_ref[...])
        dq, dk_, dv_, dbeta, dgc, ds, _ = vjp((do, dstate[...]))
        dstate[...] = ds
        lane = lax.broadcasted_iota(jnp.int32, (CHUNK, LANE), 1)
        dbgc = jnp.zeros((CHUNK, LANE), F32)
        for h in range(H):
            dqkv_ref[:, h * dk:(h + 1) * dk] = dq[h]
            dqkv_ref[:, (H + h) * dk:(H + h + 1) * dk] = dk_[h]
            dqkv_ref[:, (2 * H + h) * dk:(2 * H + h + 1) * dk] = dv_[h]
            dbgc = dbgc + jnp.where(lane == h, dbeta[h], 0.0) + jnp.where(lane == h + H, dgc[h], 0.0)
        dbgc_ref[...] = dbgc

    rn = lambda n: N - 1 - n
    return pl.pallas_call(
        body, name="deltanet_bwd", grid=(N,),
        in_specs=[pl.BlockSpec((CHUNK, H * dk), lambda n: (rn(n), 0)),
                  pl.BlockSpec((CHUNK, H * dk), lambda n: (rn(n), 1)),
                  pl.BlockSpec((CHUNK, H * dv), lambda n: (rn(n), 2)),
                  pl.BlockSpec((CHUNK, LANE), lambda n: (rn(n), 0)),
                  pl.BlockSpec((None, H, CHUNK, CHUNK), lambda n: (rn(n), 0, 0, 0)),
                  pl.BlockSpec((None, H, dk, dv), lambda n: (rn(n), 0, 0, 0)),
                  pl.BlockSpec((CHUNK, H * dv), lambda n: (rn(n), 0))],
        out_specs=[pl.BlockSpec((CHUNK, 3 * H * dk), lambda n: (rn(n), 0)),
                   pl.BlockSpec((CHUNK, LANE), lambda n: (rn(n), 0))],
        out_shape=[jax.ShapeDtypeStruct((T, 3 * H * dk), F32),
                   jax.ShapeDtypeStruct((T, LANE), F32)],
        scratch_shapes=[pltpu.VMEM((H, dk, dv), F32)],
        compiler_params=pltpu.CompilerParams(
            dimension_semantics=("arbitrary",), vmem_limit_bytes=VMEM_LIMIT),
    )(qkv, qkv, qkv, bgc, xinv, states, do)


def _place():
    x, y, c = lax.axis_index("x"), lax.axis_index("y"), lax.axis_index("c")
    chips = [(1 - x, y), (x, 1 - y), (1 - x, 1 - y)]
    return x, y, c, chips


def all_gather(name, shards):
    na = len(shards)

    def body(*refs):
        ins, outs = refs[:na], refs[na:2 * na]
        send_sems, recv_sems, local_sems = refs[2 * na:]
        x, y, c, chips = _place()
        me, sibling = (x, y, c), (x, y, 1 - c)

        def copy(a, k, block, to, src=None):
            dst = outs[a].at[4 * block[0] + 2 * block[1] + block[2]]
            return pltpu.make_async_remote_copy(
                src_ref=dst if src is None else src, dst_ref=dst,
                send_sem=send_sems.at[a, k], recv_sem=recv_sems.at[a, k],
                device_id=to, device_id_type=MESH)

        mine, first, passed = [], [], []
        for a in range(na):
            cp = pltpu.make_async_copy(ins[a], outs[a].at[4 * x + 2 * y + c], local_sems.at[a])
            cp.start()
            mine.append(cp)
        for a in range(na):
            cps = [copy(a, 0, me, sibling, src=ins[a])]
            cps += [copy(a, 1 + j, me, (*chip, c), src=ins[a]) for j, chip in enumerate(chips)]
            for cp in cps:
                cp.start()
            first += cps
        for a in range(na):
            for j, chip in enumerate(chips):
                copy(a, 1 + j, (*chip, c), me).wait_recv()
                cp = copy(a, 4 + j, (*chip, c), sibling)
                cp.start()
                passed.append(cp)
        for a in range(na):
            copy(a, 0, sibling, me).wait_recv()
            for j, chip in enumerate(chips):
                copy(a, 4 + j, (*chip, 1 - c), me).wait_recv()
        for cp in first + passed:
            cp.wait_send()
        for cp in mine:
            cp.wait()

    outs = pl.pallas_call(
        body, name=name,
        in_specs=[ANY] * na, out_specs=[ANY] * na,
        out_shape=[jax.ShapeDtypeStruct((N_DEV,) + s.shape, s.dtype) for s in shards],
        scratch_shapes=[pltpu.SemaphoreType.DMA((na, 7)), pltpu.SemaphoreType.DMA((na, 7)),
                        pltpu.SemaphoreType.DMA((na,))],
    )(*shards)
    return list(outs)


HBM_SPEC = pl.BlockSpec(memory_space=pltpu.HBM)
SEM_SPEC = pl.BlockSpec(memory_space=pltpu.SEMAPHORE)
EFFECT = pltpu.SideEffectType.DATAFLOW_SIDE_EFFECTING


def _descriptors(plan, bufs, send_sems, recv_sems):
    return [pltpu.make_async_remote_copy(src_ref=src, dst_ref=dst, send_sem=send_sems.at[k],
                                         recv_sem=recv_sems.at[k], device_id=dev, device_id_type=MESH)
            for k, (src, dst, dev) in enumerate(plan(bufs))]


def split_start(name, bufs, plan, n, after):
    nb = len(bufs)

    def body(*refs):
        for cp in _descriptors(plan, refs[:nb], refs[nb + 1], refs[nb + 2]):
            cp.start()
        refs[-1][...] = jnp.zeros((SUBLANE, LANE), F32)

    outs = pl.pallas_call(
        body, name=name,
        out_shape=(pltpu.SemaphoreType.DMA((n,)), pltpu.SemaphoreType.DMA((n,)),
                   *[pltpu.HBM(b.shape, b.dtype) for b in bufs],
                   jax.ShapeDtypeStruct((SUBLANE, LANE), F32)),
        in_specs=[HBM_SPEC] * nb + [ANY],
        out_specs=(SEM_SPEC, SEM_SPEC, *[HBM_SPEC] * nb, pl.BlockSpec(memory_space=pltpu.VMEM)),
        input_output_aliases={i: 2 + i for i in range(nb)},
        compiler_params=pltpu.CompilerParams(has_side_effects=EFFECT),
    )(*[pltpu.with_memory_space_constraint(b, pltpu.HBM) for b in bufs], after)
    return outs[0], outs[1], list(outs[2:2 + nb]), outs[-1]


def split_wait(name, send_sems, recv_sems, bufs, plan, after):
    nb = len(bufs)

    def body(*refs):
        cps = _descriptors(plan, refs[:nb], refs[nb], refs[nb + 1])
        for cp in cps:
            cp.wait_recv()
        for cp in cps:
            cp.wait_send()
        refs[-1][...] = jnp.zeros((SUBLANE, LANE), F32)

    afters = list(after) if isinstance(after, (list, tuple)) else [after]
    outs = pl.pallas_call(
        body, name=name,
        out_shape=[pltpu.HBM(b.shape, b.dtype) for b in bufs] + [jax.ShapeDtypeStruct((SUBLANE, LANE), F32)],
        in_specs=[HBM_SPEC] * nb + [SEM_SPEC, SEM_SPEC] + [ANY] * len(afters),
        out_specs=[HBM_SPEC] * nb + [pl.BlockSpec(memory_space=pltpu.VMEM)],
        input_output_aliases={i: i for i in range(nb)},
        compiler_params=pltpu.CompilerParams(has_side_effects=EFFECT),
    )(*bufs, send_sems, recv_sems, *afters)
    return list(outs[:nb]), outs[-1]


def _block(px, py, pc):
    return 4 * px + 2 * py + pc


def plan_gather_ici(na):
    def plan(bufs):
        x, y, c, chips = _place()
        out = []
        for a in range(na):
            dst = bufs[na + a].at[_block(x, y, c)]
            out.append((bufs[a], dst, (x, y, 1 - c)))
            out += [(bufs[a], dst, (px, py, c)) for px, py in chips]
        return out
    return plan


def plan_gather_pass(na):
    def plan(bufs):
        x, y, c, chips = _place()
        out = []
        for a in range(na):
            for px, py in chips:
                blk = bufs[a].at[_block(px, py, c)]
                out.append((blk, blk, (x, y, 1 - c)))
        return out
    return plan


def plan_gather_direct(bufs):
    x, y, c, _ = _place()
    dst = bufs[1].at[_block(x, y, c)]
    flip = lambda v, f: 1 - v if f else v
    return [(bufs[0], dst, (flip(x, m >> 2 & 1), flip(y, m >> 1 & 1), flip(c, m & 1))) for m in range(1, N_DEV)]


def plan_reduce_d2d(na):
    def plan(bufs):
        x, y, c, _ = _place()
        return [(bufs[a].at[2 * s + (1 - c)], bufs[na + a].at[s], (x, y, 1 - c))
                for a in range(na) for s in range(4)]
    return plan


def plan_reduce_ici(na):
    def plan(bufs):
        x, y, c, chips = _place()
        return [(bufs[a].at[2 * px + py], bufs[na + a].at[j], (px, py, c))
                for a in range(na) for j, (px, py) in enumerate(chips)]
    return plan


def place_own(name, land, shard, dev):
    r, c = shard.shape
    tr = _tile(r, 512)

    def body(sp_ref, s_ref, land_ref, o_ref):
        o_ref[...] = s_ref[...]

    return pl.pallas_call(
        body, name=name, out_shape=jax.ShapeDtypeStruct(land.shape, land.dtype),
        grid_spec=pltpu.PrefetchScalarGridSpec(
            num_scalar_prefetch=1, grid=(r // tr,),
            in_specs=[pl.BlockSpec((tr, c), lambda i, s: (i, 0)), ANY],
            out_specs=pl.BlockSpec((None, tr, c), lambda i, s: (s[0], i, 0))),
        input_output_aliases={2: 0},
        compiler_params=pltpu.CompilerParams(dimension_semantics=("arbitrary",)),
    )(dev, shard, land)


def pack(arrs, row_mult=SUBLANE):
    pieces = []
    for a in arrs:
        f = a.reshape(-1).astype(F32)
        pad = (-f.shape[0]) % LANE
        if pad:
            f = jnp.concatenate([f, jnp.zeros((pad,), F32)])
        pieces.append(f)
    flat = jnp.concatenate(pieces)
    rows = flat.shape[0] // LANE
    pad_rows = (-rows) % row_mult
    if pad_rows:
        flat = jnp.concatenate([flat, jnp.zeros((pad_rows * LANE,), F32)])
    return flat.reshape(-1, LANE)


def unpack(buf, shapes):
    flat = buf.reshape(-1)
    outs, off = [], 0
    for shp in shapes:
        n = int(np.prod(shp))
        outs.append(flat[off:off + n].reshape(shp))
        off += n + ((-n) % LANE)
    return outs


def _vjp_rows(fn, n_row_in, n_cot):
    def bwd(*args):
        rows = args[:n_row_in]
        cots = args[n_row_in:n_row_in + n_cot]
        consts = args[n_row_in + n_cot:]
        out, vjp = jax.vjp(fn, *rows, *consts)
        if isinstance(out, (tuple, list)):
            cot = tuple(c.astype(o.dtype) for c, o in zip(cots, out))
        else:
            cot = cots[0].astype(out.dtype)
        return vjp(cot)
    return bwd


def rms_bwd(name, x, g, dh, dres, after=None):
    D = x.shape[1]
    vj = _vjp_rows(rms_tile, 1, 1)

    def f(x_, dh_, dres_, g_):
        dx, dg = vj(x_, dh_, g_)
        dx = dx + dres_
        return dx, dx, dg, jnp.sum(dx, axis=0, keepdims=True)
    return rowcall(name, f, [x, dh, dres], [g], [(D, F32), (D, BF)], [(1, D), (1, D)], after=after, tr=512)


def kernel(x, e_norm, e_w_in, e_conv_w, e_a_log, e_dt_bias, e_o_norm, e_ln_g, e_ln_b, e_w_s, e_b_s, e_w_out, o_norm, o_pw1, o_pw1_b, o_dw, o_dw_b, o_ln_g, o_ln_b, o_pw2, o_pw2_b, f_norm, f_w1, f_w2, final_norm, loss_target, m_e_norm, m_e_w_in, m_e_conv_w, m_e_a_log, m_e_dt_bias, m_e_o_norm, m_e_ln_g, m_e_ln_b, m_e_w_s, m_e_b_s, m_e_w_out, m_o_norm, m_o_pw1, m_o_pw1_b, m_o_dw, m_o_dw_b, m_o_ln_g, m_o_ln_b, m_o_pw2, m_o_pw2_b, m_f_norm, m_f_w1, m_f_w2, m_final_norm, v_e_norm, v_e_w_in, v_e_conv_w, v_e_a_log, v_e_dt_bias, v_e_o_norm, v_e_ln_g, v_e_ln_b, v_e_w_s, v_e_b_s, v_e_w_out, v_o_norm, v_o_pw1, v_o_pw1_b, v_o_dw, v_o_dw_b, v_o_ln_g, v_o_ln_b, v_o_pw2, v_o_pw2_b, v_f_norm, v_f_w1, v_f_w2, v_final_norm):
    names = ['e_norm', 'e_w_in', 'e_conv_w', 'e_a_log', 'e_dt_bias', 'e_o_norm', 'e_ln_g', 'e_ln_b', 'e_w_s', 'e_b_s', 'e_w_out', 'o_norm', 'o_pw1', 'o_pw1_b', 'o_dw', 'o_dw_b', 'o_ln_g', 'o_ln_b', 'o_pw2', 'o_pw2_b', 'f_norm', 'f_w1', 'f_w2', 'final_norm']
    W = dict(zip(names, [e_norm, e_w_in, e_conv_w, e_a_log, e_dt_bias, e_o_norm, e_ln_g, e_ln_b, e_w_s, e_b_s, e_w_out, o_norm, o_pw1, o_pw1_b, o_dw, o_dw_b, o_ln_g, o_ln_b, o_pw2, o_pw2_b, f_norm, f_w1, f_w2, final_norm]))
    Mo = dict(zip(names, [m_e_norm, m_e_w_in, m_e_conv_w, m_e_a_log, m_e_dt_bias, m_e_o_norm, m_e_ln_g, m_e_ln_b, m_e_w_s, m_e_b_s, m_e_w_out, m_o_norm, m_o_pw1, m_o_pw1_b, m_o_dw, m_o_dw_b, m_o_ln_g, m_o_ln_b, m_o_pw2, m_o_pw2_b, m_f_norm, m_f_w1, m_f_w2, m_final_norm]))
    Vo = dict(zip(names, [v_e_norm, v_e_w_in, v_e_conv_w, v_e_a_log, v_e_dt_bias, v_e_o_norm, v_e_ln_g, v_e_ln_b, v_e_w_s, v_e_b_s, v_e_w_out, v_o_norm, v_o_pw1, v_o_pw1_b, v_o_dw, v_o_dw_b, v_o_ln_g, v_o_ln_b, v_o_pw2, v_o_pw2_b, v_f_norm, v_f_w1, v_f_w2, v_final_norm]))

    T, D = x.shape[1], x.shape[2]
    H = e_a_log.shape[-1]
    dv = e_o_norm.shape[-1]
    dk = dv
    G = e_w_s.shape[1]
    AQK, AV, BW = H * dk, H * dv, e_ln_g.shape[-1]
    AQKV = 2 * AQK + AV
    in_cols = AQKV + AV + 2 * H + 2 * BW
    KA = e_conv_w.shape[1]
    KC = o_dw.shape[1]
    L = f_norm.shape[0]
    dev = 4 * lax.axis_index("x") + 2 * lax.axis_index("y") + lax.axis_index("c")
    x2d = x.reshape(T, D)
    tgt = loss_target.reshape(T, D)

    def tie(small, tok):
        return small + tok[0:1, 0:1]

    dev_sp = dev.astype(jnp.int32).reshape(1)
    where = jnp.stack([lax.axis_index("c"), 2 * lax.axis_index("x") + lax.axis_index("y")]).astype(jnp.int32)
    row = lambda a: a.reshape(1, -1).astype(F32)
    en_row = row(e_norm)

    def gather_begin(tag, shards, after):
        na = len(shards)
        lands = [lax.empty((N_DEV,) + s.shape, s.dtype) for s in shards]
        ss, rs, bufs, tok = split_start(f"gather{tag}_ici_start", shards + lands, plan_gather_ici(na), 4 * na, after)
        return (na, ss, rs, bufs), tok

    def gather_pass(tag, h, after):
        na, ss, rs, bufs = h
        bufs, tok = split_wait(f"gather{tag}_ici_wait", ss, rs, bufs, plan_gather_ici(na), after)
        ss, rs, lands, tok = split_start(f"gather{tag}_pass_start", bufs[na:], plan_gather_pass(na), 3 * na, tok)
        return (na, ss, rs, bufs[:na], lands), tok

    def gather_end(tag, h, after):
        na, ss, rs, shards, lands = h
        lands, _ = split_wait(f"gather{tag}_pass_wait", ss, rs, lands, plan_gather_pass(na), after)
        return [place_own(f"gather{tag}_own{a}", lands[a], shards[a], dev_sp) for a in range(na)]

    small_sharded = ['e_conv_w', 'o_norm', 'o_pw1_b', 'o_dw', 'o_dw_b', 'o_ln_g', 'o_ln_b', 'o_pw2_b']
    sm = all_gather("gather_small", [pack([W[n][0]]) for n in small_sharded])

    bfw = lambda w: w.astype(BF)
    hA0, tok = gather_begin("0", [bfw(jnp.swapaxes(e_w_in[0], 0, 1))], sm[0])
    hA1, tok = gather_begin("1", [bfw(e_w_out[0]), bfw(f_w1[0]), bfw(f_w2[0])], tok)
    hA2, tok = gather_begin("2", [bfw(o_pw1[0]), bfw(o_pw2[0])], tok)
    hA3, tok = gather_begin("3", [bfw(f_w1[1]), bfw(f_w2[1])], tok)
    h0 = rowcall("rms_e", rms_tile, [x2d], [en_row], [(D, BF)], after=tok, tr=512)

    full = {}
    for n, g in zip(small_sharded, sm):
        shp = W[n][0].shape
        blocks = [unpack(g[d], [shp])[0] for d in range(N_DEV)]
        full[n] = jnp.concatenate(blocks, axis=-1)
    conv_w = full['e_conv_w']
    on_row, pw1_b_row = row(full['o_norm']), row(full['o_pw1_b'])
    dw_w, dw_b_row = full['o_dw'], row(full['o_dw_b'])
    oln_g_row, oln_b_row, pw2_b_row = row(full['o_ln_g']), row(full['o_ln_b']), row(full['o_pw2_b'])
    small_names = ['e_norm', 'e_conv_w', 'e_a_log', 'e_dt_bias', 'e_o_norm', 'e_ln_g', 'e_ln_b', 'e_w_s', 'e_b_s',
                   'o_norm', 'o_pw1_b', 'o_dw', 'o_dw_b', 'o_ln_g', 'o_ln_b', 'o_pw2_b', 'f_norm', 'final_norm']
    packed_wmv = [pack([A[n] for n in small_names], 256) for A in (W, Mo, Vo)]

    hB0, tok = gather_pass("0", hA0, [h0, conv_w, dw_w, pw1_b_row] + packed_wmv)
    (g_win,) = gather_end("0", hB0, tok)
    win_t = g_win.reshape(in_cols, D)
    wt_ba = jnp.pad(win_t[AQKV + AV:AQKV + AV + 2 * H], ((0, LANE - 2 * H), (0, 0)))
    wt_uv = win_t[AQKV + AV + 2 * H:]

    alog_row = jnp.pad(row(e_a_log), ((0, 0), (H, LANE - 2 * H)))
    dtb_row = jnp.pad(row(e_dt_bias), ((0, 0), (H, LANE - 2 * H)))
    eon_row = row(e_o_norm)
    eln_g_row, eln_b_row = row(e_ln_g), row(e_ln_b)
    w_s = e_w_s[0]
    bs_t = e_b_s[0].T
    fn_rows = [row(f_norm[l]) for l in range(L)]
    fin_row = row(final_norm)

    qkv_raw = matmul("proj_qkv", h0, win_t, "nt", b_window=(0, AQKV))
    z_gate = matmul("proj_z", h0, win_t, "nt", b_window=(AQKV, AV))
    ba = matmul("proj_ba", h0, wt_ba, "nt")
    uv = matmul("proj_uv", h0, wt_uv, "nt")

    cwa = min(512, AQKV)
    qkv_post = make_qkv_post(dk, cwa, 2 * AQK // cwa)
    ident = lambda t: t
    qkv = conv_fwd("qkv_conv", qkv_raw, conv_w, [], ident, qkv_post, [0], 1, KA, cw=cwa, tr=512)
    bgc_fn = make_bgc(H)
    bgc = rowcall("bgc", bgc_fn, [ba], [alog_row, dtb_row], [(LANE, F32)])
    o_dn, xinv, states = deltanet_fwd(qkv, bgc, H)
    hB1, tok = gather_pass("1", hA1, o_dn)

    def mix_tile(o, z, uv_, o_norm_, ln_g, ln_b, w_s_, bs_t_):
        return jnp.concatenate([mixa_post_tile(o, z, o_norm_), mixb_tile(uv_, ln_g, ln_b, w_s_, bs_t_)], axis=-1)
    mix_consts = [eon_row, eln_g_row, eln_b_row, w_s, bs_t]
    mix = rowcall("mix", mix_tile, [o_dn, z_gate, uv], mix_consts, [(AV + BW, BF)], after=tok)
    g_wout, g_w1_0, g_w2_0 = gather_end("1", hB1, mix)
    wout = g_wout.reshape(-1, D)
    w1 = [jnp.moveaxis(g_w1_0, 0, 1).reshape(D, -1), None]
    w2 = [g_w2_0.reshape(-1, D), None]
    add_epi = lambda acc, r: (acc + r,)
    x1 = matmul("out_proj", mix, wout, "nn", epi=add_epi, extras=[x2d], tm=512, tn=2048)

    def relu2_epi(acc):
        r = jnp.maximum(acc, 0.0)
        return r * r, r

    hB2, tok = gather_pass("2", hA2, x1)
    hf0 = rowcall("rms_f0", rms_tile, [x1], [fn_rows[0]], [(D, BF)], tr=512, after=tok)
    a2_0, ar_0 = matmul("ffn_up0", hf0, g_w1_0, "nn", epi=relu2_epi, out_dtypes=(BF, BF))
    g_pw1, g_pw2 = gather_end("2", hB2, a2_0)
    x2 = matmul("ffn_down0", a2_0, w2[0], "nn", epi=add_epi, extras=[x1])
    ffn0 = (hf0, a2_0, ar_0)
    pw1 = jnp.moveaxis(g_pw1, 0, 1).reshape(D, 2 * D)
    pw2 = g_pw2.reshape(D, D)

    bias_epi = lambda acc, b: (acc + b,)
    zc, h1 = norm_matmul("pw1", x2, on_row, g_pw1, epi=bias_epi, extras=[pw1_b_row])
    hB3, tok = gather_pass("3", hA3, zc)
    cwc = min(512, D)
    ncb = D // cwc
    cconv = conv_fwd("dw_conv", zc, dw_w, [tie(dw_b_row, tok)], glu_pre, bias_post, [0, ncb], 1, KC, hb=32, cw=cwc,
                     tr=512)
    g_w1_1, g_w2_1 = gather_end("3", hB3, cconv)
    w1[1] = jnp.moveaxis(g_w1_1, 0, 1).reshape(D, -1)
    w2[1] = g_w2_1.reshape(-1, D)
    ln_silu = lambda c, g, b: jax.nn.silu(ln_tile(c, g, b))
    s_act = rowcall("ln_silu", ln_silu, [cconv], [oln_g_row, oln_b_row], [(D, BF)], tr=512)
    x3 = matmul("pw2", s_act, pw2, "nn", epi=lambda acc, r, b: (r + (acc + b),), extras=[x2, pw2_b_row],
                tm=512, tn=2048)
    hf1 = rowcall("rms_f1", rms_tile, [x3], [fn_rows[1]], [(D, BF)], tr=512)
    a2_1, ar_1 = matmul("ffn_up1", hf1, g_w1_1, "nn", epi=relu2_epi, out_dtypes=(BF, BF))
    x4 = matmul("ffn_down1", a2_1, w2[1], "nn", epi=add_epi, extras=[x3])
    ffn1 = (hf1, a2_1, ar_1)

    def loss_bwd_tile(x_, t_, g_):
        l, vjp = jax.vjp(lambda a, b: loss_tile(a, b, t_), x_, g_)
        dx, dg = vjp(jnp.ones_like(l))
        return dx, dx, l, dg
    dx4, dx4_b, loss_part, d_final = rowcall("loss_head", loss_bwd_tile, [x4, tgt], [fin_row],
                                             [(D, F32), (D, BF)], [(1, 1), (1, D)], tr=512)
    loss = lax.psum(loss_part[0, 0], ("x", "y", "c"))

    def reduce_begin(tag, grads, after):
        na = len(grads)
        lands = [lax.empty((4,) + g.shape[1:], g.dtype) for g in grads]
        ss, rs, bufs, tok = split_start(f"reduce{tag}_d2d_start", grads + lands, plan_reduce_d2d(na), 4 * na, after)
        return (na, ss, rs, bufs), tok

    def reduce_mid(tag, h, after):
        na, ss, rs, bufs = h
        bufs, tok = split_wait(f"reduce{tag}_d2d_wait", ss, rs, bufs, plan_reduce_d2d(na), after)
        parts = []
        for a, (g, rc) in enumerate(zip(bufs[:na], bufs[na:])):
            r, c = g.shape[1], g.shape[2]
            if r % SUBLANE:
                parts.append(slabcall(f"chip_sum{tag}_{a}", lambda p, q: (p + q,),
                                      [(g, lambda i, s: 2 * i + s[0]), (rc, lambda i, s: i)], [BF], where, 4))
                continue
            mine = lambda i, n, s: (2 * (i // (n // 4)) + s[0]) * (n // 4) + i % (n // 4)
            parts.append(rowcall(f"chip_sum{tag}_{a}", lambda p, q: (p + q,),
                                 [(g.reshape(N_DEV * r, c), mine), rc.reshape(4 * r, c)], [],
                                 [(c, BF)], tr=_tile(r, 512), sp=where, R=4 * r).reshape(4, r, c))
        lands = [lax.empty((3,) + p.shape[1:], p.dtype) for p in parts]
        ss, rs, bufs, tok = split_start(f"reduce{tag}_ici_start", parts + lands, plan_reduce_ici(na), 3 * na, tok)
        return (na, ss, rs, bufs), tok

    res = {}

    def reduce_end(tag, h, after, targets):
        na, ss, rs, bufs = h
        bufs, _ = split_wait(f"reduce{tag}_ici_wait", ss, rs, bufs, plan_reduce_ici(na), after)
        for a, (part, fin, (n, l)) in enumerate(zip(bufs[:na], bufs[na:], targets)):
            def f(p0, p1, p2, p3, w_, m_, v_):
                g = ((p0.astype(F32) + p1.astype(F32)) + p2.astype(F32)) + p3.astype(F32)
                return (g,) + adamw_tile(w_, g, m_, v_)
            r, C = fin.shape[-2], fin.shape[-1]
            if W[n].shape[1:] == (C, r):
                t = lambda arr: jnp.swapaxes(arr[l], 0, 1)
                outs = slabcall(f"adam{tag}_{a}", f, [(part, lambda i, s: s[1]), (fin, 0), (fin, 1), (fin, 2),
                                                      (t(W[n]), None), (t(Mo[n]), None), (t(Vo[n]), None)],
                                [F32] * 4, where)
                res[n] = tuple(jnp.swapaxes(o, 0, 1)[None] for o in outs)
                continue
            own = lambda i, n_, s: s[1] * n_ + i
            res[n] = rowcall(f"adam{tag}_{a}", f, [(part.reshape(4 * r, C), own), (fin, 0), (fin, 1), (fin, 2),
                                                   (W[n], l), (Mo[n], l), (Vo[n], l)], [],
                             [(C, F32)] * 4, tr=256, sp=where, R=r,
                             out_lead=(W[n].shape[0], l), into=res.get(n))

    dscale_epi = lambda acc, r: (acc * (2.0 * r.astype(F32)),)
    d_fnorm = [None] * L

    dpre1 = matmul("ffn_down_dx1", dx4_b, w2[1], "nt", epi=dscale_epi, extras=[ar_1], out_dtypes=(BF,))
    dw2_1 = matmul("ffn_down_dw1", a2_1, dx4_b, "tn")
    dw1_1 = matmul("ffn_up_dw1", hf1, dpre1, "tn", colshard=True)
    hD1, tok = reduce_begin("1", [dw1_1, dw2_1.reshape(N_DEV, -1, D)], dpre1)
    dhf1 = matmul("ffn_up_dx1", dpre1, w1[1], "nt", after=tok, tk=4096)
    dx3, dx3_b, d_fnorm[1], d_pw2_b = rms_bwd("rms_f_bwd1", x3, fn_rows[1], dhf1, dx4)

    ds_act = matmul("pw2_dx", dx3_b, pw2, "nt")
    hI1, tok = reduce_mid("1", hD1, ds_act)
    d_pw2 = matmul("pw2_dw", s_act, dx3_b, "tn", after=tok)
    dcconv, d_oln_g, d_oln_b = rowcall(
        "ln_silu_bwd", _vjp_rows(ln_silu, 1, 1), [cconv, ds_act], [oln_g_row, oln_b_row],
        [(D, F32)], [(1, D), (1, D)], tr=512)
    (dza, dzb), (sza, szb), d_dw, (d_dw_b,) = conv_bwd(
        "dw_conv_bwd", zc, dw_w, [tie(dw_b_row, tok)], [dcconv], glu_pre, bias_post, [0, ncb], KC, hb=32, cw=cwc, tr=256,
        recompute=False)
    dzc = jnp.concatenate([dza, dzb], axis=-1)
    d_pw1_b = jnp.concatenate([sza, szb], axis=-1)
    d_pw1 = matmul("pw1_dw", h1, dzc, "tn", colshard=True)
    dh1 = matmul("pw1_dx", dzc, pw1, "nt", tk=4096)
    dx2, dx2_b, d_onorm, _ = rms_bwd("rms_o_bwd", x2, on_row, dh1, dx3)
    reduce_end("1", hI1, dx2, [('f_w1', 1), ('f_w2', 1)])

    hD2, tok = reduce_begin("2", [d_pw1, d_pw2.reshape(N_DEV, -1, D)], dx2)
    dpre0 = matmul("ffn_down_dx0", dx2_b, w2[0], "nt", epi=dscale_epi, extras=[ar_0], out_dtypes=(BF,), after=tok)
    dw2_0 = matmul("ffn_down_dw0", a2_0, dx2_b, "tn")
    hI2, tok = reduce_mid("2", hD2, dw2_0)
    dw1_0 = matmul("ffn_up_dw0", hf0, dpre0, "tn", colshard=True, after=tok)
    dhf0 = matmul("ffn_up_dx0", dpre0, w1[0], "nt", tk=4096)
    dx1, dx1_b, d_fnorm[0], _ = rms_bwd("rms_f_bwd0", x1, fn_rows[0], dhf0, dx2)
    reduce_end("2", hI2, dx1, [('o_pw1', 0), ('o_pw2', 0)])

    dmix = matmul("out_proj_dx", dx1_b, wout, "nt")
    d_wout = matmul("out_proj_dw", mix, dx1_b, "tn")
    hD3, tok = reduce_begin("3", [dw1_0, dw2_0.reshape(N_DEV, -1, D), d_wout.reshape(N_DEV, -1, D)], dmix)
    do_dn, dz_gate, duv, d_eon, d_eln_g, d_eln_b, d_ws, d_bs_t = rowcall(
        "mix_bwd", _vjp_rows(mix_tile, 3, 1), [o_dn, z_gate, uv, dmix], mix_consts,
        [(AV, F32), (AV, BF), (2 * BW, BF)], [(1, dv), (1, BW), (1, BW), w_s.shape, bs_t.shape], after=tok)
    dqkv, dbgc = deltanet_bwd(qkv, bgc, xinv, states, do_dn, H)
    hI3, tok = reduce_mid("3", hD3, dbgc)
    bgc_bwd = _vjp_rows(bgc_fn, 1, 1)
    dba, d_alog_row, d_dtb_row = rowcall(
        "bgc_bwd", bgc_bwd, [ba, dbgc], [alog_row, dtb_row], [(LANE, BF)], [(1, LANE), (1, LANE)])
    (dqkv_raw,), _, d_conv_w, _ = conv_bwd(
        "qkv_conv_bwd", qkv_raw, tie(conv_w, tok), [], [dqkv], ident, qkv_post, [0], KA, cw=cwa, tr=512)

    dw_qkv = matmul("proj_qkv_dw", dqkv_raw, h0, "tn")
    dw_z = matmul("proj_z_dw", dz_gate, h0, "tn")
    dw_ba = matmul("proj_ba_dw", dba, h0, "tn")
    dw_uv = matmul("proj_uv_dw", duv, h0, "tn")
    d_win_t = jnp.concatenate([dw_qkv, dw_z, dw_ba[:2 * H], dw_uv], axis=0)
    G_win = d_win_t.reshape(N_DEV, in_cols // N_DEV, D)
    hD4, tok = reduce_begin("4", [G_win], dw_uv)
    reduce_end("3", hI3, tok, [('f_w1', 0), ('f_w2', 0), ('e_w_out', 0)])

    d_alog = d_alog_row[:, H:2 * H]
    d_dtb = d_dtb_row[:, H:2 * H]
    early_grads = [d_conv_w, d_alog, d_dtb, d_eon, d_eln_g, d_eln_b, d_ws, d_bs_t.T,
                   d_onorm, d_pw1_b, d_dw, d_dw_b, d_oln_g, d_oln_b, d_pw2_b,
                   jnp.concatenate(d_fnorm, axis=0), d_final]
    early_packed = pack(early_grads, 256)
    ss_s, rs_s, bufs_s, tok_s = split_start(
        "small_grads_start", [early_packed, lax.empty((N_DEV,) + early_packed.shape, F32)],
        plan_gather_direct, N_DEV - 1, d_conv_w)
    hI4, tok = reduce_mid("4", hD4, [tok_s] + [res[n][0] for n in ('f_w1', 'f_w2', 'o_pw1', 'e_w_out')])
    dh0 = matmul_sum("proj_dx", [(dqkv_raw, (win_t, 0, AQKV)), (dz_gate, (win_t, AQKV, AV)),
                                (dba, wt_ba), (duv, wt_uv)], after=tok)
    grad_x, _, d_enorm, _ = rms_bwd("rms_e_bwd", x2d, en_row, dh0, dx1, after=tok)
    late_all = all_gather("gather_e_norm_grad", [pack([d_enorm])])[0]
    (early_shard, early_land), _ = split_wait("small_grads_wait", ss_s, rs_s, bufs_s, plan_gather_direct, late_all)
    early_all = place_own("small_grads_own", early_land, early_shard, dev_sp)

    def sum8(*ps):
        s = ps[0]
        for p in ps[1:]:
            s = s + p
        return (s,)
    gs_sum = rowcall("small_sum", sum8, [(early_all, d) for d in range(N_DEV)], [], [(LANE, F32)])
    late_sum = rowcall("e_norm_sum", sum8, [(late_all, d) for d in range(N_DEV)], [], [(LANE, F32)])
    reduce_end("4", hI4, gs_sum, [('e_w_in', 0)])
    g_full = dict(zip(small_names[1:], unpack(gs_sum, [g.shape for g in early_grads])))
    g_full['e_norm'] = unpack(late_sum, [d_enorm.shape])[0]
    g_loc = {}
    for n in small_names:
        g = g_full[n]
        if n in small_sharded:
            per = g.shape[-1] // N_DEV
            g = lax.dynamic_slice_in_dim(g, dev * per, per, axis=-1)
        g_loc[n] = g.reshape(W[n].shape)
    packs = [packed_wmv[0], pack([g_loc[n] for n in small_names], 256), packed_wmv[1], packed_wmv[2]]
    d_s, m_s, v_s = rowcall("adam_small", adamw_tile, packs, [], [(LANE, F32)] * 3)
    shapes = [W[n].shape for n in small_names]
    for n, d_, m_, v_ in zip(small_names, unpack(d_s, shapes), unpack(m_s, shapes), unpack(v_s, shapes)):
        res[n] = (g_loc[n], d_, m_, v_)

    grads = [res[n][0] for n in names]
    deltas = [res[n][1] for n in names]
    new_m = [res[n][2] for n in names]
    new_v = [res[n][3] for n in names]
    return (loss, grad_x.reshape(x.shape), *grads, *deltas, *new_m, *new_v)
```
